```python
import jax
import jax.numpy as jnp
from jax import lax
import numpy as np

D_MODEL = 1024
BATCH = 8
SEQ = 8192
DEPTH = 2

MIX_WIDTH = D_MODEL // 2
N_BRANCH = 3
NORM_EPS = 1e-6

SGU_GROUPS = 4
SGU_CHUNK = 128
SGU_WIDTH = MIX_WIDTH
SGU_GROUP_DIM = SGU_WIDTH // SGU_GROUPS

SWA_HEADS = 8
SWA_KV_HEADS = 2
SWA_HEAD_DIM = MIX_WIDTH // SWA_HEADS
SWA_GROUP = SWA_HEADS // SWA_KV_HEADS
WINDOW = 128
ROPE_THETA = 500000.0
ROPE_DIM = SWA_HEAD_DIM // 4

DN_HEADS = 4
DN_HEAD_DIM = MIX_WIDTH // DN_HEADS
DN_CONV = 4
DN_CHUNK = 64

D_FF = ((8 * D_MODEL // 3 + 255) // 256) * 256

IN_WIDTHS = (SGU_WIDTH, SGU_WIDTH,
             SWA_HEADS * SWA_HEAD_DIM, SWA_KV_HEADS * SWA_HEAD_DIM, SWA_KV_HEADS * SWA_HEAD_DIM,
             3 * MIX_WIDTH, MIX_WIDTH, DN_HEADS, DN_HEADS,
             N_BRANCH * D_MODEL)
IN_COLS = sum(IN_WIDTHS)

kernel_name = 'hybrid_gated_parallel_mixers'


def rmsnorm(x, g):
    xf = x.astype(jnp.float32)
    y = xf * lax.rsqrt(jnp.mean(xf * xf, axis=-1, keepdims=True) + NORM_EPS)
    return (y * g.astype(jnp.float32)).astype(x.dtype)


def layernorm(x, g, b):
    xf = x.astype(jnp.float32)
    xc = xf - jnp.mean(xf, axis=-1, keepdims=True)
    y = xc * lax.rsqrt(jnp.mean(xc * xc, axis=-1, keepdims=True) + NORM_EPS)
    return (y * g.astype(jnp.float32) + b.astype(jnp.float32)).astype(x.dtype)


def l2norm(x):
    return x * lax.rsqrt(jnp.sum(x * x, axis=-1, keepdims=True) + NORM_EPS)


def split_columns(t):
    parts, start = [], 0
    for w in IN_WIDTHS:
        parts.append(t[..., start:start + w])
        start += w
    return parts


def rotary_tables(positions):
    inv_freq = ROPE_THETA ** (-jnp.arange(0, ROPE_DIM, 2, dtype=jnp.float32) / ROPE_DIM)
    ang = positions.astype(jnp.float32)[..., None] * inv_freq
    return jnp.cos(ang)[:, :, None, :], jnp.sin(ang)[:, :, None, :]


def apply_partial_rope(x, cos, sin):
    half = ROPE_DIM // 2
    x1, x2, rest = x[..., :half], x[..., half:ROPE_DIM], x[..., ROPE_DIM:]
    c, s = cos.astype(x.dtype), sin.astype(x.dtype)
    return jnp.concatenate([x1 * c - x2 * s, x2 * c + x1 * s, rest], axis=-1)


def spatial_gating(u, v, ln_g, ln_b, w_s, b_s):
    B_, S_ = u.shape[:2]
    nc = S_ // SGU_CHUNK
    vn = layernorm(v, ln_g, ln_b).reshape(B_, nc, SGU_CHUNK, SGU_GROUPS, SGU_GROUP_DIM)
    causal = jnp.tril(jnp.ones((SGU_CHUNK, SGU_CHUNK), dtype=bool))
    w_causal = jnp.where(causal, w_s, 0.0).astype(vn.dtype)
    mixed = jnp.einsum('gts,bnsgc->bntgc', w_causal, vn) + b_s.T.astype(vn.dtype)[None, None, :, :, None]
    return u * mixed.reshape(B_, S_, SGU_WIDTH)


def sliding_window_attention(q, k, v, sinks, cos, sin):
    B_, S_ = q.shape[:2]
    nc = S_ // WINDOW
    q = apply_partial_rope(q, cos, sin) * (SWA_HEAD_DIM ** -0.5)
    k = apply_partial_rope(k, cos, sin)
    qb = q.reshape(B_, nc, WINDOW, SWA_KV_HEADS, SWA_GROUP, SWA_HEAD_DIM)

    def band(t):
        cur = t.reshape(B_, nc, WINDOW, SWA_KV_HEADS, SWA_HEAD_DIM)
        prev = jnp.concatenate([jnp.zeros_like(cur[:, :1]), cur[:, :-1]], axis=1)
        return jnp.concatenate([prev, cur], axis=2)

    kb, vb = band(k), band(v)
    logits = jnp.einsum('bnqkgd,bnskd->bnkgqs', qb, kb).astype(jnp.float32)
    qi = jnp.arange(WINDOW)[:, None]
    sj = jnp.arange(2 * WINDOW)[None, :]
    diff = qi + WINDOW - sj
    in_band = (diff >= 0) & (diff < WINDOW)
    valid = (jnp.arange(nc) > 0)[:, None, None] | (sj >= WINDOW)[None]
    mask = in_band[None] & valid
    logits = jnp.where(mask[None, :, None, None], logits, -jnp.inf)
    sink = jnp.broadcast_to(sinks.astype(jnp.float32).reshape(1, 1, SWA_KV_HEADS, SWA_GROUP, 1, 1),
                            logits.shape[:-1] + (1,))
    probs = jax.nn.softmax(jnp.concatenate([logits, sink], axis=-1), axis=-1)[..., :-1]
    out = jnp.einsum('bnkgqs,bnskd->bnqkgd', probs.astype(vb.dtype), vb)
    return out.reshape(B_, S_, SWA_HEADS * SWA_HEAD_DIM)


def causal_short_conv(x, w):
    S_ = x.shape[1]
    xp = jnp.pad(x, ((0, 0), (DN_CONV - 1, 0), (0, 0)))
    out = xp[:, 0:S_] * w[0]
    for i in range(1, DN_CONV):
        out = out + xp[:, i:i + S_] * w[i]
    return jax.nn.silu(out)


def gated_deltanet(qkv, z, beta_logit, a_logit, conv_w, a_log, dt_bias, norm_g):
    B_, S_ = qkv.shape[:2]
    in_dtype = qkv.dtype
    nt = S_ // DN_CHUNK
    H, hd, C = DN_HEADS, DN_HEAD_DIM, DN_CHUNK
    qkv = causal_short_conv(qkv, conv_w).astype(jnp.float32)
    q = l2norm(qkv[..., :MIX_WIDTH].reshape(B_, S_, H, hd)) * (hd ** -0.5)
    k = l2norm(qkv[..., MIX_WIDTH:2 * MIX_WIDTH].reshape(B_, S_, H, hd))
    v = qkv[..., 2 * MIX_WIDTH:].reshape(B_, S_, H, hd)
    beta = jax.nn.sigmoid(beta_logit.astype(jnp.float32))
    g = -jnp.exp(a_log.astype(jnp.float32)) * jax.nn.softplus(a_logit.astype(jnp.float32) + dt_bias.astype(jnp.float32))

    def to_chunks(t):
        t = t.reshape((B_, nt, C) + t.shape[2:])
        return jnp.swapaxes(jnp.swapaxes(t, 0, 1), 2, 3)

    q, k, v, beta, g = to_chunks(q), to_chunks(k), to_chunks(v), to_chunks(beta), to_chunks(g)
    gc = jnp.cumsum(g, axis=-1)
    tril = jnp.tril(jnp.ones((C, C), dtype=bool))
    strict = jnp.tril(jnp.ones((C, C), dtype=bool), -1)
    decay = jnp.exp(jnp.where(tril, gc[..., :, None] - gc[..., None, :], -jnp.inf))
    k_beta = k * beta[..., None]
    lower = jnp.where(strict, jnp.einsum('nbhid,nbhjd->nbhij', k_beta, k) * decay, 0.0)
    a_mat = lower + jnp.eye(C, dtype=jnp.float32)
    rhs = jnp.concatenate([v * beta[..., None], k_beta * jnp.exp(gc)[..., None]], axis=-1)
    sol = lax.linalg.triangular_solve(a_mat, rhs, left_side=True, lower=True, unit_diagonal=True)
    u_c, w_c = sol[..., :hd], sol[..., hd:]
    attn = jnp.einsum('nbhid,nbhjd->nbhij', q, k) * decay
    q_dec = q * jnp.exp(gc)[..., None]
    k_dec = k * jnp.exp(gc[..., -1:] - gc)[..., None]
    c_dec = jnp.exp(gc[..., -1])

    def step(state, xs):
        qd, wc, uc, at, kd, cd = xs
        v_new = uc - jnp.einsum('bhcd,bhde->bhce', wc, state)
        o_c = jnp.einsum('bhcd,bhde->bhce', qd, state) + jnp.einsum('bhij,bhje->bhie', at, v_new)
        state = state * cd[..., None, None] + jnp.einsum('bhcd,bhce->bhde', kd, v_new)
        return state, o_c

    state0 = jnp.zeros((B_, H, hd, hd), dtype=jnp.float32)
    _, o = lax.scan(step, state0, (q_dec, w_c, u_c, attn, k_dec, c_dec))
    o = jnp.swapaxes(jnp.swapaxes(o, 0, 1), 2, 3).reshape(B_, S_, H, hd)
    o = rmsnorm(o, norm_g) * jax.nn.silu(z.astype(jnp.float32).reshape(B_, S_, H, hd))
    return o.reshape(B_, S_, MIX_WIDTH).astype(in_dtype)


def _fwd_setup_inputs(seed: int = 0) -> dict:
    key = jax.random.key(seed)
    ks = jax.random.split(key, 20)
    f32 = jnp.float32

    def nrm(k, shape, scale):
        return jax.random.normal(k, shape, dtype=f32) * scale

    dt = jnp.exp(jax.random.uniform(ks[11], (DEPTH, DN_HEADS), dtype=f32,
                                    minval=np.log(1e-3), maxval=np.log(1e-1)))
    return {
        'x': nrm(ks[0], (BATCH, SEQ, D_MODEL), 1.0),
        'positions': jnp.broadcast_to(jnp.arange(SEQ, dtype=jnp.int32), (BATCH, SEQ)),
        'attn_norm': 1.0 + nrm(ks[1], (DEPTH, D_MODEL), 0.02),
        'w_in': nrm(ks[2], (DEPTH, D_MODEL, IN_COLS), D_MODEL ** -0.5),
        'sgu_ln_g': 1.0 + nrm(ks[3], (DEPTH, SGU_WIDTH), 0.02),
        'sgu_ln_b': nrm(ks[4], (DEPTH, SGU_WIDTH), 0.02),
        'sgu_w': nrm(ks[5], (DEPTH, SGU_GROUPS, SGU_CHUNK, SGU_CHUNK), SGU_CHUNK ** -0.5),
        'sgu_b': 1.0 + nrm(ks[6], (DEPTH, SGU_GROUPS, SGU_CHUNK), 0.02),
        'attn_sinks': nrm(ks[7], (DEPTH, SWA_HEADS), 0.5),
        'dn_conv_w': nrm(ks[8], (DEPTH, DN_CONV, 3 * MIX_WIDTH), DN_CONV ** -0.5),
        'dn_a_log': jnp.log(jax.random.uniform(ks[9], (DEPTH, DN_HEADS), dtype=f32, minval=1.0, maxval=16.0)),
        'dn_dt_bias': dt + jnp.log(-jnp.expm1(-dt)),
        'dn_norm': 1.0 + nrm(ks[10], (DEPTH, DN_HEAD_DIM), 0.02),
        'w_branch': nrm(ks[12], (DEPTH, N_BRANCH, MIX_WIDTH, D_MODEL), MIX_WIDTH ** -0.5),
        'w_out': nrm(ks[13], (DEPTH, D_MODEL, D_MODEL), D_MODEL ** -0.5),
        'ffn_norm': 1.0 + nrm(ks[14], (DEPTH, D_MODEL), 0.02),
        'w_gate_up': nrm(ks[15], (DEPTH, D_MODEL, 2 * D_FF), D_MODEL ** -0.5),
        'w_down': nrm(ks[16], (DEPTH, D_FF, D_MODEL), D_FF ** -0.5),
        'final_norm': 1.0 + nrm(ks[17], (D_MODEL,), 0.02),
    }


def _fwd_reference(x, positions, attn_norm, w_in, sgu_ln_g, sgu_ln_b, sgu_w, sgu_b, attn_sinks,
              dn_conv_w, dn_a_log, dn_dt_bias, dn_norm, w_branch, w_out, ffn_norm,
              w_gate_up, w_down, final_norm):
    B_, S_ = x.shape[:2]
    cos, sin = rotary_tables(positions)
    for layer in range(DEPTH):
        h = rmsnorm(x, attn_norm[layer])
        proj = jnp.einsum('bsd,dc->bsc', h, w_in[layer])
        u_a, v_a, q_b, k_b, v_b, qkv_c, z_c, beta_c, a_c, gate_pre = split_columns(proj)
        out_a = spatial_gating(jax.nn.gelu(u_a), jax.nn.gelu(v_a), sgu_ln_g[layer], sgu_ln_b[layer],
                               sgu_w[layer], sgu_b[layer])
        out_b = sliding_window_attention(q_b.reshape(B_, S_, SWA_HEADS, SWA_HEAD_DIM),
                                         k_b.reshape(B_, S_, SWA_KV_HEADS, SWA_HEAD_DIM),
                                         v_b.reshape(B_, S_, SWA_KV_HEADS, SWA_HEAD_DIM),
                                         attn_sinks[layer], cos, sin)
        out_c = gated_deltanet(qkv_c, z_c, beta_c, a_c, dn_conv_w[layer], dn_a_log[layer],
                               dn_dt_bias[layer], dn_norm[layer])
        branches = jnp.stack([out_a, out_b, out_c], axis=0)
        branch_d = jnp.einsum('nbsc,ncd->nbsd', branches, w_branch[layer])
        gates = jax.nn.sigmoid(gate_pre.reshape(B_, S_, N_BRANCH, D_MODEL))
        merged = jnp.einsum('bsnd,nbsd->bsd', gates, branch_d)
        x = x + jnp.einsum('bsd,de->bse', merged, w_out[layer])
        h2 = rmsnorm(x, ffn_norm[layer])
        gu = jnp.einsum('bsd,df->bsf', h2, w_gate_up[layer])
        x = x + jnp.einsum('bsf,fd->bsd', jax.nn.silu(gu[..., :D_FF]) * gu[..., D_FF:], w_down[layer])
    return rmsnorm(x, final_norm)


import jax as _jax
import jax.numpy as _jnp

TWIN_FORMAT = 'train_step'
FWD_PARAMS = ['x', 'positions', 'attn_norm', 'w_in', 'sgu_ln_g', 'sgu_ln_b', 'sgu_w', 'sgu_b', 'attn_sinks', 'dn_conv_w', 'dn_a_log', 'dn_dt_bias', 'dn_norm', 'w_branch', 'w_out', 'ffn_norm', 'w_gate_up', 'w_down', 'final_norm']
TWIN_WEIGHTS = ['attn_norm', 'w_in', 'sgu_ln_g', 'sgu_ln_b', 'sgu_w', 'sgu_b', 'attn_sinks', 'dn_conv_w', 'dn_a_log', 'dn_dt_bias', 'dn_norm', 'w_branch', 'w_out', 'ffn_norm', 'w_gate_up', 'w_down', 'final_norm']
TWIN_DIFF_INPUT = 'x'
TWIN_INPUTS = ['x', 'positions', 'attn_norm', 'w_in', 'sgu_ln_g', 'sgu_ln_b', 'sgu_w', 'sgu_b', 'attn_sinks', 'dn_conv_w', 'dn_a_log', 'dn_dt_bias', 'dn_norm', 'w_branch', 'w_out', 'ffn_norm', 'w_gate_up', 'w_down', 'final_norm', 'loss_target', 'm_attn_norm', 'm_w_in', 'm_sgu_ln_g', 'm_sgu_ln_b', 'm_sgu_w', 'm_sgu_b', 'm_attn_sinks', 'm_dn_conv_w', 'm_dn_a_log', 'm_dn_dt_bias', 'm_dn_norm', 'm_w_branch', 'm_w_out', 'm_ffn_norm', 'm_w_gate_up', 'm_w_down', 'm_final_norm', 'v_attn_norm', 'v_w_in', 'v_sgu_ln_g', 'v_sgu_ln_b', 'v_sgu_w', 'v_sgu_b', 'v_attn_sinks', 'v_dn_conv_w', 'v_dn_a_log', 'v_dn_dt_bias', 'v_dn_norm', 'v_w_branch', 'v_w_out', 'v_ffn_norm', 'v_w_gate_up', 'v_w_down', 'v_final_norm']
TWIN_OUTPUTS = ['loss', 'grad_x', 'grad_attn_norm', 'grad_w_in', 'grad_sgu_ln_g', 'grad_sgu_ln_b', 'grad_sgu_w', 'grad_sgu_b', 'grad_attn_sinks', 'grad_dn_conv_w', 'grad_dn_a_log', 'grad_dn_dt_bias', 'grad_dn_norm', 'grad_w_branch', 'grad_w_out', 'grad_ffn_norm', 'grad_w_gate_up', 'grad_w_down', 'grad_final_norm', 'delta_attn_norm', 'delta_w_in', 'delta_sgu_ln_g', 'delta_sgu_ln_b', 'delta_sgu_w', 'delta_sgu_b', 'delta_attn_sinks', 'delta_dn_conv_w', 'delta_dn_a_log', 'delta_dn_dt_bias', 'delta_dn_norm', 'delta_w_branch', 'delta_w_out', 'delta_ffn_norm', 'delta_w_gate_up', 'delta_w_down', 'delta_final_norm', 'new_m_attn_norm', 'new_m_w_in', 'new_m_sgu_ln_g', 'new_m_sgu_ln_b', 'new_m_sgu_w', 'new_m_sgu_b', 'new_m_attn_sinks', 'new_m_dn_conv_w', 'new_m_dn_a_log', 'new_m_dn_dt_bias', 'new_m_dn_norm', 'new_m_w_branch', 'new_m_w_out', 'new_m_ffn_norm', 'new_m_w_gate_up', 'new_m_w_down', 'new_m_final_norm', 'new_v_attn_norm', 'new_v_w_in', 'new_v_sgu_ln_g', 'new_v_sgu_ln_b', 'new_v_sgu_w', 'new_v_sgu_b', 'new_v_attn_sinks', 'new_v_dn_conv_w', 'new_v_dn_a_log', 'new_v_dn_dt_bias', 'new_v_dn_norm', 'new_v_w_branch', 'new_v_w_out', 'new_v_ffn_norm', 'new_v_w_gate_up', 'new_v_w_down', 'new_v_final_norm']
TWIN_LEAF_KINDS = {'loss': 'loss', 'grad_x': 'grad_x', 'grad_attn_norm': 'grad_w', 'grad_w_in': 'grad_w', 'grad_sgu_ln_g': 'grad_w', 'grad_sgu_ln_b': 'grad_w', 'grad_sgu_w': 'grad_w', 'grad_sgu_b': 'grad_w', 'grad_attn_sinks': 'grad_w', 'grad_dn_conv_w': 'grad_w', 'grad_dn_a_log': 'grad_w', 'grad_dn_dt_bias': 'grad_w', 'grad_dn_norm': 'grad_w', 'grad_w_branch': 'grad_w', 'grad_w_out': 'grad_w', 'grad_ffn_norm': 'grad_w', 'grad_w_gate_up': 'grad_w', 'grad_w_down': 'grad_w', 'grad_final_norm': 'grad_w', 'delta_attn_norm': 'delta_w', 'delta_w_in': 'delta_w', 'delta_sgu_ln_g': 'delta_w', 'delta_sgu_ln_b': 'delta_w', 'delta_sgu_w': 'delta_w', 'delta_sgu_b': 'delta_w', 'delta_attn_sinks': 'delta_w', 'delta_dn_conv_w': 'delta_w', 'delta_dn_a_log': 'delta_w', 'delta_dn_dt_bias': 'delta_w', 'delta_dn_norm': 'delta_w', 'delta_w_branch': 'delta_w', 'delta_w_out': 'delta_w', 'delta_ffn_norm': 'delta_w', 'delta_w_gate_up': 'delta_w', 'delta_w_down': 'delta_w', 'delta_final_norm': 'delta_w', 'new_m_attn_norm': 'new_m', 'new_m_w_in': 'new_m', 'new_m_sgu_ln_g': 'new_m', 'new_m_sgu_ln_b': 'new_m', 'new_m_sgu_w': 'new_m', 'new_m_sgu_b': 'new_m', 'new_m_attn_sinks': 'new_m', 'new_m_dn_conv_w': 'new_m', 'new_m_dn_a_log': 'new_m', 'new_m_dn_dt_bias': 'new_m', 'new_m_dn_norm': 'new_m', 'new_m_w_branch': 'new_m', 'new_m_w_out': 'new_m', 'new_m_ffn_norm': 'new_m', 'new_m_w_gate_up': 'new_m', 'new_m_w_down': 'new_m', 'new_m_final_norm': 'new_m', 'new_v_attn_norm': 'new_v', 'new_v_w_in': 'new_v', 'new_v_sgu_ln_g': 'new_v', 'new_v_sgu_ln_b': 'new_v', 'new_v_sgu_w': 'new_v', 'new_v_sgu_b': 'new_v', 'new_v_attn_sinks': 'new_v', 'new_v_dn_conv_w': 'new_v', 'new_v_dn_a_log': 'new_v', 'new_v_dn_dt_bias': 'new_v', 'new_v_dn_norm': 'new_v', 'new_v_w_branch': 'new_v', 'new_v_w_out': 'new_v', 'new_v_ffn_norm': 'new_v', 'new_v_w_gate_up': 'new_v', 'new_v_w_down': 'new_v', 'new_v_final_norm': 'new_v'}


def _forward(args):
    return _fwd_reference(*[args[k] for k in FWD_PARAMS])


def _output_shape():
    def fwd():
        inp = _fwd_setup_inputs(0)
        return _fwd_reference(*[inp[k] for k in FWD_PARAMS])
    out = _jax.eval_shape(fwd)
    return out.shape, out.dtype

N_MICROBATCH = 1
ADAM_LR = 0.001
ADAM_B1 = 0.9
ADAM_B2 = 0.999
ADAM_EPS = 1e-08
ADAM_WD = 0.01
ADAM_STEP = 10
PER_EXAMPLE_BATCH_AXIS = {'x': 0, 'positions': 0, 'loss_target': 0}
SHARED_INPUTS = []
_WEIGHT_DTYPES = {'attn_norm': _jnp.float32, 'w_in': _jnp.float32, 'sgu_ln_g': _jnp.float32, 'sgu_ln_b': _jnp.float32, 'sgu_w': _jnp.float32, 'sgu_b': _jnp.float32, 'attn_sinks': _jnp.float32, 'dn_conv_w': _jnp.float32, 'dn_a_log': _jnp.float32, 'dn_dt_bias': _jnp.float32, 'dn_norm': _jnp.float32, 'w_branch': _jnp.float32, 'w_out': _jnp.float32, 'ffn_norm': _jnp.float32, 'w_gate_up': _jnp.float32, 'w_down': _jnp.float32, 'final_norm': _jnp.float32}
MOMENT_SCALE = {'attn_norm': 1.950564e-01, 'w_in': 7.498705e-02, 'sgu_ln_g': 9.390244e-02, 'sgu_ln_b': 8.362273e-02, 'sgu_w': 8.538801e-02, 'sgu_b': 1.238069e-01, 'attn_sinks': 3.705896e-02, 'dn_conv_w': 8.624624e-02, 'dn_a_log': 5.246475e-01, 'dn_dt_bias': 5.206333e-01, 'dn_norm': 2.250136e-01, 'w_branch': 7.812052e-02, 'w_out': 1.342026e-01, 'ffn_norm': 1.843274e-01, 'w_gate_up': 7.332540e-02, 'w_down': 1.197177e-01, 'final_norm': 6.409353e+01}


def _to_microbatches(a, axis):
    t = _jnp.moveaxis(a, axis, 0)
    t = t.reshape((N_MICROBATCH, t.shape[0] // N_MICROBATCH) + t.shape[1:])
    return _jnp.moveaxis(t, 1, axis + 1)


def setup_inputs(seed: int = 0) -> dict:
    inp = _fwd_setup_inputs(seed)
    key = _jax.random.fold_in(_jax.random.key(seed), 7919)
    shape, _ = _output_shape()
    out = dict(inp)
    out["loss_target"] = _jax.random.normal(_jax.random.fold_in(key, 0), shape, _jnp.float32)
    for i, name in enumerate(TWIN_WEIGHTS):
        w = inp[name].astype(_jnp.float32)
        if MOMENT_SCALE is None:
            s = _jnp.sqrt(_jnp.mean(_jnp.square(w)) + 1e-30)
        else:
            s = MOMENT_SCALE[name]
        km, kv = _jax.random.split(_jax.random.fold_in(key, i + 1))
        out[name] = w
        out["m_" + name] = s * _jax.random.normal(km, w.shape, _jnp.float32)
        out["v_" + name] = (s * s) * _jax.random.uniform(kv, w.shape, _jnp.float32, 0.5, 1.5)
    if N_MICROBATCH > 1:
        for name, axis in PER_EXAMPLE_BATCH_AXIS.items():
            out[name] = _to_microbatches(out[name], axis)
    return {'x': out['x'], 'positions': out['positions'], 'attn_norm': out['attn_norm'], 'w_in': out['w_in'], 'sgu_ln_g': out['sgu_ln_g'], 'sgu_ln_b': out['sgu_ln_b'], 'sgu_w': out['sgu_w'], 'sgu_b': out['sgu_b'], 'attn_sinks': out['attn_sinks'], 'dn_conv_w': out['dn_conv_w'], 'dn_a_log': out['dn_a_log'], 'dn_dt_bias': out['dn_dt_bias'], 'dn_norm': out['dn_norm'], 'w_branch': out['w_branch'], 'w_out': out['w_out'], 'ffn_norm': out['ffn_norm'], 'w_gate_up': out['w_gate_up'], 'w_down': out['w_down'], 'final_norm': out['final_norm'], 'loss_target': out['loss_target'], 'm_attn_norm': out['m_attn_norm'], 'm_w_in': out['m_w_in'], 'm_sgu_ln_g': out['m_sgu_ln_g'], 'm_sgu_ln_b': out['m_sgu_ln_b'], 'm_sgu_w': out['m_sgu_w'], 'm_sgu_b': out['m_sgu_b'], 'm_attn_sinks': out['m_attn_sinks'], 'm_dn_conv_w': out['m_dn_conv_w'], 'm_dn_a_log': out['m_dn_a_log'], 'm_dn_dt_bias': out['m_dn_dt_bias'], 'm_dn_norm': out['m_dn_norm'], 'm_w_branch': out['m_w_branch'], 'm_w_out': out['m_w_out'], 'm_ffn_norm': out['m_ffn_norm'], 'm_w_gate_up': out['m_w_gate_up'], 'm_w_down': out['m_w_down'], 'm_final_norm': out['m_final_norm'], 'v_attn_norm': out['v_attn_norm'], 'v_w_in': out['v_w_in'], 'v_sgu_ln_g': out['v_sgu_ln_g'], 'v_sgu_ln_b': out['v_sgu_ln_b'], 'v_sgu_w': out['v_sgu_w'], 'v_sgu_b': out['v_sgu_b'], 'v_attn_sinks': out['v_attn_sinks'], 'v_dn_conv_w': out['v_dn_conv_w'], 'v_dn_a_log': out['v_dn_a_log'], 'v_dn_dt_bias': out['v_dn_dt_bias'], 'v_dn_norm': out['v_dn_norm'], 'v_w_branch': out['v_w_branch'], 'v_w_out': out['v_w_out'], 'v_ffn_norm': out['v_ffn_norm'], 'v_w_gate_up': out['v_w_gate_up'], 'v_w_down': out['v_w_down'], 'v_final_norm': out['v_final_norm']}


def _loss(weights, diff, rest, loss_target):
    with _jax.named_scope("forward"):
        args = {**rest, TWIN_DIFF_INPUT: diff, **{k: w.astype(_WEIGHT_DTYPES[k]) for k, w in weights.items()}}
        y = _forward(args)
    with _jax.named_scope("loss_head"):
        err = _jnp.square(y.astype(_jnp.float32) - loss_target)
        return 0.5 * _jnp.sum(_jnp.mean(err, axis=-1)) if err.ndim else 0.5 * err


def _adamw(w, g, m, v):
    m = ADAM_B1 * m + (1.0 - ADAM_B1) * g
    v = ADAM_B2 * v + (1.0 - ADAM_B2) * _jnp.square(g)
    m_hat = m / (1.0 - ADAM_B1 ** ADAM_STEP)
    v_hat = v / (1.0 - ADAM_B2 ** ADAM_STEP)
    delta = -ADAM_LR * (m_hat / (_jnp.sqrt(v_hat) + ADAM_EPS) + ADAM_WD * w)
    return delta, m, v


def reference(x, positions, attn_norm, w_in, sgu_ln_g, sgu_ln_b, sgu_w, sgu_b, attn_sinks, dn_conv_w, dn_a_log, dn_dt_bias, dn_norm, w_branch, w_out, ffn_norm, w_gate_up, w_down, final_norm, loss_target, m_attn_norm, m_w_in, m_sgu_ln_g, m_sgu_ln_b, m_sgu_w, m_sgu_b, m_attn_sinks, m_dn_conv_w, m_dn_a_log, m_dn_dt_bias, m_dn_norm, m_w_branch, m_w_out, m_ffn_norm, m_w_gate_up, m_w_down, m_final_norm, v_attn_norm, v_w_in, v_sgu_ln_g, v_sgu_ln_b, v_sgu_w, v_sgu_b, v_attn_sinks, v_dn_conv_w, v_dn_a_log, v_dn_dt_bias, v_dn_norm, v_w_branch, v_w_out, v_ffn_norm, v_w_gate_up, v_w_down, v_final_norm):
    given = dict(x=x, positions=positions, attn_norm=attn_norm, w_in=w_in, sgu_ln_g=sgu_ln_g, sgu_ln_b=sgu_ln_b, sgu_w=sgu_w, sgu_b=sgu_b, attn_sinks=attn_sinks, dn_conv_w=dn_conv_w, dn_a_log=dn_a_log, dn_dt_bias=dn_dt_bias, dn_norm=dn_norm, w_branch=w_branch, w_out=w_out, ffn_norm=ffn_norm, w_gate_up=w_gate_up, w_down=w_down, final_norm=final_norm, loss_target=loss_target, m_attn_norm=m_attn_norm, m_w_in=m_w_in, m_sgu_ln_g=m_sgu_ln_g, m_sgu_ln_b=m_sgu_ln_b, m_sgu_w=m_sgu_w, m_sgu_b=m_sgu_b, m_attn_sinks=m_attn_sinks, m_dn_conv_w=m_dn_conv_w, m_dn_a_log=m_dn_a_log, m_dn_dt_bias=m_dn_dt_bias, m_dn_norm=m_dn_norm, m_w_branch=m_w_branch, m_w_out=m_w_out, m_ffn_norm=m_ffn_norm, m_w_gate_up=m_w_gate_up, m_w_down=m_w_down, m_final_norm=m_final_norm, v_attn_norm=v_attn_norm, v_w_in=v_w_in, v_sgu_ln_g=v_sgu_ln_g, v_sgu_ln_b=v_sgu_ln_b, v_sgu_w=v_sgu_w, v_sgu_b=v_sgu_b, v_attn_sinks=v_attn_sinks, v_dn_conv_w=v_dn_conv_w, v_dn_a_log=v_dn_a_log, v_dn_dt_bias=v_dn_dt_bias, v_dn_norm=v_dn_norm, v_w_branch=v_w_branch, v_w_out=v_w_out, v_ffn_norm=v_ffn_norm, v_w_gate_up=v_w_gate_up, v_w_down=v_w_down, v_final_norm=v_final_norm)
    weights = {n: given[n] for n in TWIN_WEIGHTS}
    shared = {n: given[n] for n in SHARED_INPUTS}
    per_example = {n: given[n] for n in ['x', 'positions']}
    grad_fn = _jax.value_and_grad(_loss, argnums=(0, 1))

    def one_microbatch(ex, loss_target):
        ex = dict(ex)
        diff = ex.pop(TWIN_DIFF_INPUT)
        return grad_fn(weights, diff, {**shared, **ex}, loss_target)

    if N_MICROBATCH == 1:
        loss, (grad_w, grad_x) = one_microbatch(per_example, given["loss_target"])
    else:
        def body(carry, xs):
            loss_sum, grad_sum = carry
            l_k, (gw_k, gx_k) = one_microbatch(xs[0], xs[1])
            with _jax.named_scope("update"):
                return (loss_sum + l_k, _jax.tree.map(_jnp.add, grad_sum, gw_k)), gx_k

        init = (_jnp.zeros((), _jnp.float32), _jax.tree.map(_jnp.zeros_like, weights))
        (loss, grad_w), grad_x = _jax.lax.scan(body, init, (per_example, given["loss_target"]))
    with _jax.named_scope("update"):
        delta_w, new_m, new_v = {}, {}, {}
        for n in TWIN_WEIGHTS:
            delta_w[n], new_m[n], new_v[n] = _adamw(weights[n], grad_w[n], given["m_" + n], given["v_" + n])
    return (loss, grad_x, *[grad_w[n] for n in TWIN_WEIGHTS], *[delta_w[n] for n in TWIN_WEIGHTS],
            *[new_m[n] for n in TWIN_WEIGHTS], *[new_v[n] for n in TWIN_WEIGHTS])
```

```python
import functools

import jax
import jax.numpy as jnp
from jax import lax
from jax.experimental import pallas as pl
from jax.experimental.pallas import tpu as pltpu

f32 = jnp.float32
bf16 = jnp.bfloat16
HIGHEST = lax.Precision.HIGHEST

N_DEV = 8
MESH_AXES = ("x", "y", "c")
NORM_EPS = 1e-6
MIX = 512
SGU_GROUPS, SGU_CHUNK = 4, 128
SWA_HEADS, SWA_KV, SWA_HD, WINDOW = 8, 2, 64, 128
ROPE_THETA, ROPE_DIM = 500000.0, 16
DN_HEADS, DN_HD, DN_CONV, DN_CHUNK = 4, 128, 4, 64
ADAM_LR, ADAM_B1, ADAM_B2, ADAM_EPS, ADAM_WD, ADAM_STEP = 0.001, 0.9, 0.999, 1e-08, 0.01, 10

LANES = 128
VMEM_LIMIT = 56 * 1024 * 1024

SEC_A, SEC_Z, SEC_QKV, SEC_G, SEC_B = 0, 1024, 1536, 3072, 6144
W_MAIN = 6912


def _params(n_axes, **kw):
    return pltpu.CompilerParams(dimension_semantics=("arbitrary",) * n_axes, vmem_limit_bytes=VMEM_LIMIT, **kw)


def _tile(n, target, mult=LANES):
    if n <= target:
        return n
    best = None
    for t in range(mult, target + 1, mult):
        if n % t == 0:
            best = t
    assert best is not None, (n, target, mult)
    return best


def _dg(a, b, ca, cb):
    return lax.dot_general(a.astype(bf16), b.astype(bf16), (((ca,), (cb,)), ((), ())), preferred_element_type=f32)


@functools.partial(jax.custom_vjp, nondiff_argnums=(2, 3))
def bdot(a, b, ca, cb):
    return _dg(a, b, ca, cb)


def _bdot_fwd(a, b, ca, cb):
    return _dg(a, b, ca, cb), (a, b)


def _bdot_bwd(ca, cb, res, ct):
    a, b = res
    da = _dg(ct, b, 1, 1 - cb) if ca == 1 else _dg(b, ct, 1 - cb, 1)
    db = _dg(a, ct, 1 - ca, 0) if cb == 0 else _dg(ct, a, 0, 1 - ca)
    return da, db


bdot.defvjp(_bdot_fwd, _bdot_bwd)


def hdot(a, b, ca=1, cb=0):
    return lax.dot_general(a, b, (((ca,), (cb,)), ((), ())), precision=HIGHEST, preferred_element_type=f32)


@functools.partial(jax.custom_vjp, nondiff_argnums=(1,))
def lroll(x, shift):
    return pltpu.roll(x, shift, 1)


def _lroll_fwd(x, shift):
    return pltpu.roll(x, shift, 1), None


def _lroll_bwd(shift, _, ct):
    return (pltpu.roll(ct, ct.shape[1] - shift, 1),)


lroll.defvjp(_lroll_fwd, _lroll_bwd)


@jax.custom_vjp
def tri_inv(low):
    n = low.shape[0]
    row = lax.broadcasted_iota(jnp.int32, (n, n), 0)
    col = lax.broadcasted_iota(jnp.int32, (n, n), 1)
    eye = (row == col).astype(f32)
    m = -low
    p = eye + m
    span = 2
    while span < n:
        m = hdot(m, m)
        p = p + hdot(p, m)
        span *= 2
    return p


def _tri_inv_fwd(low):
    t = tri_inv(low)
    return t, t


def _tri_inv_bwd(t, dt):
    return (-hdot(t, hdot(dt, t, 1, 1), 0, 0),)


tri_inv.defvjp(_tri_inv_fwd, _tri_inv_bwd)


def _sigmoid(x):
    return 1.0 / (1.0 + jnp.exp(-x))


def _rms(x, g):
    return x * lax.rsqrt(jnp.mean(x * x, axis=-1, keepdims=True) + NORM_EPS) * g


def _lane_col(x, lane_idx):
    lane = lax.broadcasted_iota(jnp.int32, x.shape, 1)
    return jnp.sum(jnp.where(lane == lane_idx, x, 0.0), axis=1, keepdims=True)


def matmul(a, b, mode, out_dtype, *, residual=None, tm=1024, tn=768, tk=1024, name):
    if mode == "nn":
        (m, k), n = a.shape, b.shape[1]
    elif mode == "nt":
        (m, k), n = a.shape, b.shape[0]
    else:
        (k, m), n = a.shape, b.shape[1]
    tm, tn = _tile(m, tm), _tile(n, tn)
    tk = _tile(k, tk)
    nk = k // tk
    if mode == "nn":
        a_spec = pl.BlockSpec((tm, tk), lambda i, j, kk: (i, kk))
        b_spec = pl.BlockSpec((tk, tn), lambda i, j, kk: (kk, j))
        ca, cb = 1, 0
    elif mode == "nt":
        a_spec = pl.BlockSpec((tm, tk), lambda i, j, kk: (i, kk))
        b_spec = pl.BlockSpec((tn, tk), lambda i, j, kk: (j, kk))
        ca, cb = 1, 1
    else:
        a_spec = pl.BlockSpec((tk, tm), lambda i, j, kk: (kk, i))
        b_spec = pl.BlockSpec((tk, tn), lambda i, j, kk: (kk, j))
        ca, cb = 0, 0
    o_spec = pl.BlockSpec((tm, tn), lambda i, j, kk: (i, j))
    has_res = residual is not None

    def body(*refs):
        if has_res:
            a_ref, b_ref, r_ref, o_ref, acc_ref = refs
        else:
            a_ref, b_ref, o_ref, acc_ref = refs
        kk = pl.program_id(2)

        @pl.when(kk == 0)
        def _():
            acc_ref[...] = jnp.zeros_like(acc_ref)

        acc_ref[...] += _dg(a_ref[...], b_ref[...], ca, cb)

        @pl.when(kk == nk - 1)
        def _():
            acc = acc_ref[...]
            if has_res:
                acc = acc + r_ref[...]
            o_ref[...] = acc.astype(out_dtype)

    in_specs = [a_spec, b_spec] + ([o_spec] if has_res else [])
    args = (a, b) + ((residual,) if has_res else ())
    return pl.pallas_call(
        body, name=name, grid=(m // tm, n // tn, nk), in_specs=in_specs, out_specs=o_spec,
        out_shape=jax.ShapeDtypeStruct((m, n), out_dtype), scratch_shapes=[pltpu.VMEM((tm, tn), f32)],
        compiler_params=_params(3),
    )(*args)


def rmsnorm_fwd(x, g_row, *, name):
    s, d = x.shape
    ts = _tile(s, 512, 16)

    def body(x_ref, g_ref, o_ref):
        o_ref[...] = _rms(x_ref[...], g_ref[...]).astype(bf16)

    return pl.pallas_call(
        body, name=name, grid=(s // ts,),
        in_specs=[pl.BlockSpec((ts, d), lambda i: (i, 0)), pl.BlockSpec((1, d), lambda i: (0, 0))],
        out_specs=pl.BlockSpec((ts, d), lambda i: (i, 0)), out_shape=jax.ShapeDtypeStruct((s, d), bf16),
        compiler_params=_params(1),
    )(x, g_row)


def rmsnorm_bwd(x, g_row, dh, dres, *, name):
    s, d = x.shape
    ts = _tile(s, 512, 16)

    def body(x_ref, g_ref, dh_ref, dres_ref, dx_ref, dg_ref):
        @pl.when(pl.program_id(0) == 0)
        def _():
            dg_ref[...] = jnp.zeros_like(dg_ref)

        _, vjp = jax.vjp(_rms, x_ref[...], g_ref[...])
        dx, dg = vjp(dh_ref[...])
        dx_ref[...] = dx + dres_ref[...]
        dg_ref[...] += dg

    row = pl.BlockSpec((ts, d), lambda i: (i, 0))
    vec = pl.BlockSpec((1, d), lambda i: (0, 0))
    return pl.pallas_call(
        body, name=name, grid=(s // ts,), in_specs=[row, vec, row, row], out_specs=[row, vec],
        out_shape=[jax.ShapeDtypeStruct((s, d), f32), jax.ShapeDtypeStruct((1, d), f32)],
        compiler_params=_params(1),
    )(x, g_row, dh, dres)


def loss_head(x, g_row, target, *, name):
    s, d = x.shape
    ts = _tile(s, 512, 16)

    def body(x_ref, g_ref, t_ref, dx_ref, dg_ref, loss_ref):
        @pl.when(pl.program_id(0) == 0)
        def _():
            dg_ref[...] = jnp.zeros_like(dg_ref)
            loss_ref[...] = jnp.zeros_like(loss_ref)

        y, vjp = jax.vjp(_rms, x_ref[...], g_ref[...])
        err = y - t_ref[...]
        dx, dg = vjp(err * (1.0 / d))
        dx_ref[...] = dx
        dg_ref[...] += dg
        loss_ref[...] += 0.5 * jnp.sum(jnp.sum(err * err, axis=1, keepdims=True) * (1.0 / d), axis=0, keepdims=True)

    row = pl.BlockSpec((ts, d), lambda i: (i, 0))
    vec = pl.BlockSpec((1, d), lambda i: (0, 0))
    one = pl.BlockSpec((1, LANES), lambda i: (0, 0))
    return pl.pallas_call(
        body, name=name, grid=(s // ts,), in_specs=[row, vec, row], out_specs=[row, vec, one],
        out_shape=[jax.ShapeDtypeStruct((s, d), f32), jax.ShapeDtypeStruct((1, d), f32),
                   jax.ShapeDtypeStruct((1, LANES), f32)],
        compiler_params=_params(1),
    )(x, g_row, target)


def _sgu_chunk(p_a, ln_g, ln_b, w, b_t):
    t = SGU_CHUNK
    u = jax.nn.gelu(p_a[:, :MIX])
    v = jax.nn.gelu(p_a[:, MIX:])
    vc = v - jnp.mean(v, axis=-1, keepdims=True)
    vn = vc * lax.rsqrt(jnp.mean(vc * vc, axis=-1, keepdims=True) + NORM_EPS) * ln_g + ln_b
    causal = lax.broadcasted_iota(jnp.int32, (t, t), 0) >= lax.broadcasted_iota(jnp.int32, (t, t), 1)
    outs = []
    for g in range(SGU_GROUPS):
        sl = slice(g * LANES, (g + 1) * LANES)
        mixed = bdot(jnp.where(causal, w[g], 0.0), vn[:, sl], 1, 0) + b_t[:, g:g + 1]
        outs.append(u[:, sl] * mixed)
    return jnp.concatenate(outs, axis=1)


def _sgu_specs(s, ts):
    return [pl.BlockSpec((ts, 2 * MIX), lambda i: (i, SEC_A // (2 * MIX))),
            pl.BlockSpec((1, MIX), lambda i: (0, 0)), pl.BlockSpec((1, MIX), lambda i: (0, 0)),
            pl.BlockSpec((SGU_GROUPS, SGU_CHUNK, SGU_CHUNK), lambda i: (0, 0, 0)),
            pl.BlockSpec((SGU_CHUNK, SGU_GROUPS), lambda i: (0, 0))]


def sgu_fwd(proj, ln_g, ln_b, w, b_t, *, name):
    s = proj.shape[0]
    ts = _tile(s, 512)
    n_chunk = ts // SGU_CHUNK

    def body(p_ref, g_ref, b_ref, w_ref, bt_ref, o_ref):
        def step(c, carry):
            rows = pl.ds(pl.multiple_of(c * SGU_CHUNK, SGU_CHUNK), SGU_CHUNK)
            o_ref[rows, :] = _sgu_chunk(p_ref[rows, :], g_ref[...], b_ref[...], w_ref[...], bt_ref[...]).astype(bf16)
            return carry
        lax.fori_loop(0, n_chunk, step, 0)

    return pl.pallas_call(
        body, name=name, grid=(s // ts,), in_specs=_sgu_specs(s, ts),
        out_specs=pl.BlockSpec((ts, MIX), lambda i: (i, 0)), out_shape=jax.ShapeDtypeStruct((s, MIX), bf16),
        compiler_params=_params(1),
    )(proj, ln_g, ln_b, w, b_t)


def sgu_bwd(proj, ln_g, ln_b, w, b_t, d_out, dproj, *, name):
    s = proj.shape[0]
    ts = _tile(s, 512)
    n_chunk = ts // SGU_CHUNK

    def body(p_ref, g_ref, b_ref, w_ref, bt_ref, do_ref, _, dp_ref, dg_ref, db_ref, dw_ref, dbt_ref):
        @pl.when(pl.program_id(0) == 0)
        def _():
            dg_ref[...] = jnp.zeros_like(dg_ref)
            db_ref[...] = jnp.zeros_like(db_ref)
            dw_ref[...] = jnp.zeros_like(dw_ref)
            dbt_ref[...] = jnp.zeros_like(dbt_ref)

        def step(c, carry):
            rows = pl.ds(pl.multiple_of(c * SGU_CHUNK, SGU_CHUNK), SGU_CHUNK)
            _, vjp = jax.vjp(_sgu_chunk, p_ref[rows, :], g_ref[...], b_ref[...], w_ref[...], bt_ref[...])
            dp, dg, db, dw, dbt = vjp(do_ref[rows, :])
            dp_ref[rows, :] = dp.astype(bf16)
            dg_ref[...] += dg
            db_ref[...] += db
            dw_ref[...] += dw
            dbt_ref[...] += dbt
            return carry
        lax.fori_loop(0, n_chunk, step, 0)

    specs = _sgu_specs(s, ts)
    return pl.pallas_call(
        body, name=name, grid=(s // ts,),
        in_specs=specs + [pl.BlockSpec((ts, MIX), lambda i: (i, 0)), pl.BlockSpec(memory_space=pl.ANY)],
        out_specs=[specs[0], specs[1], specs[2], specs[3], specs[4]],
        out_shape=[jax.ShapeDtypeStruct(dproj.shape, bf16), jax.ShapeDtypeStruct((1, MIX), f32),
                   jax.ShapeDtypeStruct((1, MIX), f32), jax.ShapeDtypeStruct(w.shape, f32),
                   jax.ShapeDtypeStruct(b_t.shape, f32)],
        input_output_aliases={6: 0}, compiler_params=_params(1),
    )(proj, ln_g, ln_b, w, b_t, d_out, dproj)


def _rope(x, pos, inv_freq):
    w = x.shape[1]
    d = lax.broadcasted_iota(jnp.int32, (1, w), 1) % SWA_HD
    half = ROPE_DIM // 2
    ang = pos * inv_freq
    c, s = jnp.cos(ang), jnp.sin(ang)
    lo = jnp.where(d < half, s, 0.0)
    hi = jnp.where((d >= half) & (d < ROPE_DIM), s, 0.0)
    return x * c - lroll(x, w - half) * lo + lroll(x, half) * hi


def _swa_block(q, kp, kc, vp, vc, sink_row, pos_q, pos_p, inv_q, inv_k, prev_ok):
    t = WINDOW
    q = _rope(q, pos_q, inv_q) * (SWA_HD ** -0.5)
    keys = jnp.concatenate([_rope(kp, pos_p, inv_k), _rope(kc, pos_q, inv_k)], axis=0)
    vals = jnp.concatenate([vp, vc], axis=0)
    qi = lax.broadcasted_iota(jnp.int32, (t, 2 * t), 0)
    sj = lax.broadcasted_iota(jnp.int32, (t, 2 * t), 1)
    diff = qi + t - sj
    mask = (diff >= 0) & (diff < t) & (prev_ok | (sj >= t))
    lane_half = lax.broadcasted_iota(jnp.int32, (t, LANES), 1) // SWA_HD
    group = SWA_HEADS // SWA_KV
    slabs = []
    for pair in range(SWA_HEADS // 2):
        q_pair = q[:, pair * LANES:(pair + 1) * LANES]
        acc = jnp.zeros((t, LANES), f32)
        for half in range(2):
            h = 2 * pair + half
            kv = h // group
            qm = jnp.where(lane_half == half, q_pair, 0.0)
            if half != kv:
                qm = lroll(qm, SWA_HD)
            logits = jnp.where(mask, bdot(qm, keys, 1, 1), -1e30)
            sink = _lane_col(sink_row, h)
            m = jnp.maximum(jnp.max(logits, axis=1, keepdims=True), sink)
            p = jnp.exp(logits - m)
            probs = p / (jnp.sum(p, axis=1, keepdims=True) + jnp.exp(sink - m))
            o = jnp.where(lane_half == kv, bdot(probs, vals, 1, 0), 0.0)
            if half != kv:
                o = lroll(o, SWA_HD)
            acc = acc + o
        slabs.append(acc)
    return jnp.concatenate(slabs, axis=1)


def _swa_in_specs(nc, clamp):
    t = WINDOW
    qb, kb, vb = SEC_B // MIX, (SEC_B + MIX) // LANES, (SEC_B + MIX + LANES) // LANES

    def cur(i):
        return jnp.minimum(i, nc - 1) if clamp else i

    def prev(i):
        return jnp.maximum(cur(i) - 1, 0)

    return [pl.BlockSpec((t, MIX), lambda i: (cur(i), qb)),
            pl.BlockSpec((t, LANES), lambda i: (prev(i), kb)), pl.BlockSpec((t, LANES), lambda i: (cur(i), kb)),
            pl.BlockSpec((t, LANES), lambda i: (prev(i), vb)), pl.BlockSpec((t, LANES), lambda i: (cur(i), vb)),
            pl.BlockSpec((1, LANES), lambda i: (0, 0)),
            pl.BlockSpec((t, 1), lambda i: (cur(i), 0)), pl.BlockSpec((t, 1), lambda i: (prev(i), 0)),
            pl.BlockSpec((1, MIX), lambda i: (0, 0)), pl.BlockSpec((1, LANES), lambda i: (0, 0))]


def swa_fwd(proj, sink_row, posf, inv_q, inv_k, *, name):
    s = proj.shape[0]
    nc = s // WINDOW

    def body(q_ref, kp_ref, kc_ref, vp_ref, vc_ref, sink_ref, pq_ref, pp_ref, iq_ref, ik_ref, o_ref):
        prev_ok = pl.program_id(0) > 0
        o_ref[...] = _swa_block(q_ref[...], kp_ref[...], kc_ref[...], vp_ref[...], vc_ref[...], sink_ref[...],
                                pq_ref[...], pp_ref[...], iq_ref[...], ik_ref[...], prev_ok).astype(bf16)

    return pl.pallas_call(
        body, name=name, grid=(nc,), in_specs=_swa_in_specs(nc, False),
        out_specs=pl.BlockSpec((WINDOW, MIX), lambda i: (i, 0)), out_shape=jax.ShapeDtypeStruct((s, MIX), bf16),
        compiler_params=_params(1),
    )(proj, proj, proj, proj, proj, sink_row, posf, posf, inv_q, inv_k)


def swa_bwd(proj, sink_row, posf, inv_q, inv_k, d_out, dproj, *, name):
    s = proj.shape[0]
    nc = s // WINDOW
    t = WINDOW

    def body(q_ref, kp_ref, kc_ref, vp_ref, vc_ref, sink_ref, pq_ref, pp_ref, iq_ref, ik_ref, do_ref, _,
             dp_ref, dsink_ref, cq_ref, ck_ref, cv_ref):
        i = pl.program_id(0)

        @pl.when(i == 0)
        def _():
            dsink_ref[...] = jnp.zeros_like(dsink_ref)

        def write(dk_prev, dv_prev):
            dp_ref[:, :MIX] = cq_ref[...].astype(bf16)
            dp_ref[:, MIX:MIX + LANES] = (ck_ref[...] + dk_prev).astype(bf16)
            dp_ref[:, MIX + LANES:] = (cv_ref[...] + dv_prev).astype(bf16)

        @pl.when(i < nc)
        def _():
            fn = functools.partial(_swa_block, pos_q=pq_ref[...], pos_p=pp_ref[...], inv_q=iq_ref[...],
                                   inv_k=ik_ref[...], prev_ok=i > 0)
            _, vjp = jax.vjp(fn, q_ref[...], kp_ref[...], kc_ref[...], vp_ref[...], vc_ref[...], sink_ref[...])
            dq, dkp, dkc, dvp, dvc, dsink = vjp(do_ref[...])
            dsink_ref[...] += dsink

            @pl.when(i > 0)
            def _():
                write(dkp, dvp)

            cq_ref[...] = dq
            ck_ref[...] = dkc
            cv_ref[...] = dvc

        @pl.when(i == nc)
        def _():
            write(0.0, 0.0)

    return pl.pallas_call(
        body, name=name, grid=(nc + 1,),
        in_specs=_swa_in_specs(nc, True) + [pl.BlockSpec((t, MIX), lambda i: (jnp.minimum(i, nc - 1), 0)),
                                            pl.BlockSpec(memory_space=pl.ANY)],
        out_specs=[pl.BlockSpec((t, MIX + 2 * LANES), lambda i: (jnp.maximum(i - 1, 0), SEC_B // (MIX + 2 * LANES))),
                   pl.BlockSpec((1, LANES), lambda i: (0, 0))],
        out_shape=[jax.ShapeDtypeStruct(dproj.shape, bf16), jax.ShapeDtypeStruct((1, LANES), f32)],
        scratch_shapes=[pltpu.VMEM((t, MIX), f32), pltpu.VMEM((t, LANES), f32), pltpu.VMEM((t, LANES), f32)],
        input_output_aliases={11: 0}, compiler_params=_params(1),
    )(proj, proj, proj, proj, proj, sink_row, posf, posf, inv_q, inv_k, d_out, dproj)


CONV_PAD = 8


def _conv_pre(xp, w, rows):
    off = CONV_PAD - (DN_CONV - 1)
    pre = xp[off:off + rows] * w[0:1]
    for i in range(1, DN_CONV):
        pre = pre + xp[off + i:off + i + rows] * w[i:i + 1]
    return pre


def conv_fwd(proj, conv_w8, *, name):
    s = proj.shape[0]
    wq = 3 * MIX
    ts = _tile(s, 512)
    nb = ts // CONV_PAD

    def body(x_ref, prev_ref, w_ref, o_ref):
        prev = jnp.where(pl.program_id(0) > 0, prev_ref[...], 0.0)
        pre = _conv_pre(jnp.concatenate([prev, x_ref[...]], axis=0), w_ref[...], ts)
        o_ref[...] = pre * _sigmoid(pre)

    return pl.pallas_call(
        body, name=name, grid=(s // ts,),
        in_specs=[pl.BlockSpec((ts, wq), lambda i: (i, SEC_QKV // wq)),
                  pl.BlockSpec((CONV_PAD, wq), lambda i: (jnp.maximum(i * nb - 1, 0), SEC_QKV // wq)),
                  pl.BlockSpec((CONV_PAD, wq), lambda i: (0, 0))],
        out_specs=pl.BlockSpec((ts, wq), lambda i: (i, 0)), out_shape=jax.ShapeDtypeStruct((s, wq), f32),
        compiler_params=_params(1),
    )(proj, proj, conv_w8)


def conv_bwd(proj, conv_w8, dxc, dproj, *, name):
    s = proj.shape[0]
    wq = 3 * MIX
    ts = _tile(s, 512)
    nb = ts // CONV_PAD
    nt = s // ts
    last_blk = s // CONV_PAD - 1

    def body(x_ref, prev_ref, next_ref, w_ref, d_ref, dnext_ref, _, dp_ref, dw_ref):
        i = pl.program_id(0)

        @pl.when(i == 0)
        def _():
            dw_ref[...] = jnp.zeros_like(dw_ref)

        w = w_ref[...]
        prev = jnp.where(i > 0, prev_ref[...], 0.0)
        more = i < nt - 1
        xp = jnp.concatenate([prev, x_ref[...], jnp.where(more, next_ref[...], 0.0)], axis=0)
        ext = ts + CONV_PAD
        pre = _conv_pre(xp, w, ext)
        sig = _sigmoid(pre)
        dxc_ext = jnp.concatenate([d_ref[...], jnp.where(more, dnext_ref[...], 0.0)], axis=0)
        dpre = dxc_ext * sig * (1.0 + pre * (1.0 - sig))
        off = CONV_PAD - (DN_CONV - 1)
        d_raw = jnp.zeros((ts, wq), f32)
        dws = []
        for k in range(DN_CONV):
            shift = DN_CONV - 1 - k
            d_raw = d_raw + dpre[shift:shift + ts] * w[k:k + 1]
            dws.append(jnp.sum(dpre[:ts] * xp[off + k:off + k + ts], axis=0, keepdims=True))
        dp_ref[...] = d_raw.astype(bf16)
        dw_ref[...] += jnp.concatenate(dws + [jnp.zeros((CONV_PAD - DN_CONV, wq), f32)], axis=0)

    sec = SEC_QKV // wq
    return pl.pallas_call(
        body, name=name, grid=(nt,),
        in_specs=[pl.BlockSpec((ts, wq), lambda i: (i, sec)),
                  pl.BlockSpec((CONV_PAD, wq), lambda i: (jnp.maximum(i * nb - 1, 0), sec)),
                  pl.BlockSpec((CONV_PAD, wq), lambda i: (jnp.minimum((i + 1) * nb, last_blk), sec)),
                  pl.BlockSpec((CONV_PAD, wq), lambda i: (0, 0)),
                  pl.BlockSpec((ts, wq), lambda i: (i, 0)),
                  pl.BlockSpec((CONV_PAD, wq), lambda i: (jnp.minimum((i + 1) * nb, last_blk), 0)),
                  pl.BlockSpec(memory_space=pl.ANY)],
        out_specs=[pl.BlockSpec((ts, wq), lambda i: (i, sec)), pl.BlockSpec((CONV_PAD, wq), lambda i: (0, 0))],
        out_shape=[jax.ShapeDtypeStruct(dproj.shape, bf16), jax.ShapeDtypeStruct((CONV_PAD, wq), f32)],
        input_output_aliases={6: 0}, compiler_params=_params(1),
    )(proj, proj, proj, conv_w8, dxc, dxc, dproj)


def _dn_chunk(states, xc, z, ba, alog_row, dtb_row, norm_row):
    c = DN_CHUNK
    row = lax.broadcasted_iota(jnp.int32, (c, c), 0)
    col = lax.broadcasted_iota(jnp.int32, (c, c), 1)
    tril, strict = row >= col, row > col
    beta_all = _sigmoid(ba)
    g_all = -jnp.exp(alog_row) * jax.nn.softplus(ba + dtb_row)
    gc_all = hdot(tril.astype(f32), g_all)
    gc_t = gc_all.T
    new_states, outs = [], []
    for h in range(DN_HEADS):
        sl = slice(h * DN_HD, (h + 1) * DN_HD)
        q = xc[:, sl]
        k = xc[:, MIX + h * DN_HD:MIX + (h + 1) * DN_HD]
        v = xc[:, 2 * MIX + h * DN_HD:2 * MIX + (h + 1) * DN_HD]
        q = q * lax.rsqrt(jnp.sum(q * q, axis=-1, keepdims=True) + NORM_EPS) * (DN_HD ** -0.5)
        k = k * lax.rsqrt(jnp.sum(k * k, axis=-1, keepdims=True) + NORM_EPS)
        beta = _lane_col(beta_all, h)
        g_col = _lane_col(gc_all, DN_HEADS + h)
        g_row = gc_t[DN_HEADS + h:DN_HEADS + h + 1, :]
        g_last = g_col[c - 1:c, :]
        decay = jnp.where(tril, jnp.exp(jnp.where(tril, g_col - g_row, 0.0)), 0.0)
        kb = k * beta
        low = jnp.where(strict, bdot(kb, k, 1, 1) * decay, 0.0)
        t_inv = tri_inv(low)
        e_gc = jnp.exp(g_col)
        u = hdot(t_inv, v * beta)
        w = hdot(t_inv, kb * e_gc)
        attn = bdot(q, k, 1, 1) * decay
        state = states[h]
        v_new = u - bdot(w, state, 1, 0)
        o = bdot(q * e_gc, state, 1, 0) + bdot(attn, v_new, 1, 0)
        new_states.append(state * jnp.exp(g_last) + bdot(k * jnp.exp(g_last - g_col), v_new, 0, 0))
        zh = z[:, sl]
        outs.append(_rms(o, norm_row) * (zh * _sigmoid(zh)))
    return tuple(new_states), jnp.concatenate(outs, axis=1)


def _dn_specs(ts, order):
    zb = SEC_Z // MIX
    return [pl.BlockSpec((ts, 3 * MIX), lambda i: (order(i), 0)),
            pl.BlockSpec((ts, MIX), lambda i: (order(i), zb)),
            pl.BlockSpec((ts, LANES), lambda i: (order(i), 0)),
            pl.BlockSpec((1, LANES), lambda i: (0, 0)), pl.BlockSpec((1, LANES), lambda i: (0, 0)),
            pl.BlockSpec((1, LANES), lambda i: (0, 0))]


def dn_fwd(xc, proj, p_ba, alog_row, dtb_row, norm_row, *, name):
    s = xc.shape[0]
    ts = _tile(s, 512)
    n_chunk = ts // DN_CHUNK

    def body(xc_ref, z_ref, ba_ref, al_ref, dt_ref, nr_ref, o_ref, st_ref, state_ref):
        @pl.when(pl.program_id(0) == 0)
        def _():
            state_ref[...] = jnp.zeros_like(state_ref)

        def step(c, carry):
            rows = pl.ds(pl.multiple_of(c * DN_CHUNK, DN_CHUNK), DN_CHUNK)
            states = tuple(state_ref[h] for h in range(DN_HEADS))
            st_ref[c] = state_ref[...]
            new_states, out = _dn_chunk(states, xc_ref[rows, :], z_ref[rows, :], ba_ref[rows, :],
                                        al_ref[...], dt_ref[...], nr_ref[...])
            for h in range(DN_HEADS):
                state_ref[h] = new_states[h]
            o_ref[rows, :] = out.astype(bf16)
            return carry
        lax.fori_loop(0, n_chunk, step, 0)

    return pl.pallas_call(
        body, name=name, grid=(s // ts,), in_specs=_dn_specs(ts, lambda i: i),
        out_specs=[pl.BlockSpec((ts, MIX), lambda i: (i, 0)),
                   pl.BlockSpec((n_chunk, DN_HEADS, DN_HD, DN_HD), lambda i: (i, 0, 0, 0))],
        out_shape=[jax.ShapeDtypeStruct((s, MIX), bf16),
                   jax.ShapeDtypeStruct((s // DN_CHUNK, DN_HEADS, DN_HD, DN_HD), f32)],
        scratch_shapes=[pltpu.VMEM((DN_HEADS, DN_HD, DN_HD), f32)],
        compiler_params=_params(1),
    )(xc, proj, p_ba, alog_row, dtb_row, norm_row)


def dn_bwd(xc, proj, p_ba, alog_row, dtb_row, norm_row, saved, d_out, dproj, *, name):
    s = xc.shape[0]
    ts = _tile(s, 512)
    n_chunk = ts // DN_CHUNK
    nt = s // ts

    def body(xc_ref, z_ref, ba_ref, al_ref, dt_ref, nr_ref, st_ref, do_ref, _,
             dz_ref, dxc_ref, dba_ref, dal_ref, ddt_ref, dnr_ref, dstate_ref):
        @pl.when(pl.program_id(0) == 0)
        def _():
            dstate_ref[...] = jnp.zeros_like(dstate_ref)
            dal_ref[...] = jnp.zeros_like(dal_ref)
            ddt_ref[...] = jnp.zeros_like(ddt_ref)
            dnr_ref[...] = jnp.zeros_like(dnr_ref)

        def step(it, carry):
            c = n_chunk - 1 - it
            rows = pl.ds(pl.multiple_of(c * DN_CHUNK, DN_CHUNK), DN_CHUNK)
            states = tuple(st_ref[c, h] for h in range(DN_HEADS))
            _, vjp = jax.vjp(_dn_chunk, states, xc_ref[rows, :], z_ref[rows, :], ba_ref[rows, :],
                             al_ref[...], dt_ref[...], nr_ref[...])
            d_states = tuple(dstate_ref[h] for h in range(DN_HEADS))
            d_in, dxc, dz, dba, dal, ddt, dnr = vjp((d_states, do_ref[rows, :]))
            for h in range(DN_HEADS):
                dstate_ref[h] = d_in[h]
            dxc_ref[rows, :] = dxc
            dz_ref[rows, :] = dz.astype(bf16)
            dba_ref[rows, :] = dba.astype(bf16)
            dal_ref[...] += dal
            ddt_ref[...] += ddt
            dnr_ref[...] += dnr
            return carry
        lax.fori_loop(0, n_chunk, step, 0)

    def rev(i):
        return nt - 1 - i

    specs = _dn_specs(ts, rev)
    vec = pl.BlockSpec((1, LANES), lambda i: (0, 0))
    return pl.pallas_call(
        body, name=name, grid=(nt,),
        in_specs=specs + [pl.BlockSpec((n_chunk, DN_HEADS, DN_HD, DN_HD), lambda i: (rev(i), 0, 0, 0)),
                          pl.BlockSpec((ts, MIX), lambda i: (rev(i), 0)), pl.BlockSpec(memory_space=pl.ANY)],
        out_specs=[specs[1], specs[0], specs[2], vec, vec, vec],
        out_shape=[jax.ShapeDtypeStruct(dproj.shape, bf16), jax.ShapeDtypeStruct((s, 3 * MIX), f32),
                   jax.ShapeDtypeStruct((s, LANES), bf16)] + [jax.ShapeDtypeStruct((1, LANES), f32)] * 3,
        scratch_shapes=[pltpu.VMEM((DN_HEADS, DN_HD, DN_HD), f32)],
        input_output_aliases={8: 0}, compiler_params=_params(1),
    )(xc, proj, p_ba, alog_row, dtb_row, norm_row, saved, d_out, dproj)


def _merge_in_specs(ts, d):
    row = pl.BlockSpec((ts, MIX), lambda i: (i, 0))
    return [row, row, row, pl.BlockSpec((ts, 3 * d), lambda i: (i, SEC_G // (3 * d))),
            pl.BlockSpec((3, MIX, d), lambda i: (0, 0, 0))]


def merge_fwd(out_a, out_b, out_c, proj, w_branch, *, name):
    s, d = out_a.shape[0], w_branch.shape[2]
    ts = _tile(s, 256, 16)

    def body(a_ref, b_ref, c_ref, g_ref, w_ref, o_ref):
        acc = jnp.zeros((ts, d), f32)
        for n, r in enumerate((a_ref, b_ref, c_ref)):
            acc = acc + _sigmoid(g_ref[:, n * d:(n + 1) * d]) * _dg(r[...], w_ref[n], 1, 0)
        o_ref[...] = acc.astype(bf16)

    return pl.pallas_call(
        body, name=name, grid=(s // ts,), in_specs=_merge_in_specs(ts, d),
        out_specs=pl.BlockSpec((ts, d), lambda i: (i, 0)), out_shape=jax.ShapeDtypeStruct((s, d), bf16),
        compiler_params=_params(1),
    )(out_a, out_b, out_c, proj, w_branch)


def merge_bwd(out_a, out_b, out_c, proj, w_branch, d_merged, dproj, *, name):
    s, d = out_a.shape[0], w_branch.shape[2]
    ts = _tile(s, 256, 16)

    def body(a_ref, b_ref, c_ref, g_ref, w_ref, dm_ref, _, dg_ref, da_ref, db_ref, dc_ref, dw_ref):
        @pl.when(pl.program_id(0) == 0)
        def _():
            dw_ref[...] = jnp.zeros_like(dw_ref)

        dm = dm_ref[...]
        for n, (r, dr) in enumerate(((a_ref, da_ref), (b_ref, db_ref), (c_ref, dc_ref))):
            gate = _sigmoid(g_ref[:, n * d:(n + 1) * d])
            branch = _dg(r[...], w_ref[n], 1, 0)
            dg_ref[:, n * d:(n + 1) * d] = (dm * branch * gate * (1.0 - gate)).astype(bf16)
            d_branch = dm * gate
            dr[...] = _dg(d_branch, w_ref[n], 1, 1)
            dw_ref[n] += _dg(r[...], d_branch, 0, 0)

    specs = _merge_in_specs(ts, d)
    row_f = pl.BlockSpec((ts, MIX), lambda i: (i, 0))
    return pl.pallas_call(
        body, name=name, grid=(s // ts,),
        in_specs=specs + [pl.BlockSpec((ts, d), lambda i: (i, 0)), pl.BlockSpec(memory_space=pl.ANY)],
        out_specs=[specs[3], row_f, row_f, row_f, specs[4]],
        out_shape=[jax.ShapeDtypeStruct(dproj.shape, bf16)] + [jax.ShapeDtypeStruct((s, MIX), f32)] * 3
        + [jax.ShapeDtypeStruct(w_branch.shape, f32)],
        input_output_aliases={6: 0}, compiler_params=_params(1),
    )(out_a, out_b, out_c, proj, w_branch, d_merged, dproj)


def swiglu_fwd(gu, *, name):
    s, f2 = gu.shape
    ff = f2 // 2
    ts, tf = _tile(s, 512, 16), _tile(ff, 1408)
    nf = ff // tf

    def body(g_ref, u_ref, o_ref):
        g = g_ref[...]
        o_ref[...] = (g * _sigmoid(g) * u_ref[...]).astype(bf16)

    return pl.pallas_call(
        body, name=name, grid=(s // ts, nf),
        in_specs=[pl.BlockSpec((ts, tf), lambda i, j: (i, j)), pl.BlockSpec((ts, tf), lambda i, j: (i, nf + j))],
        out_specs=pl.BlockSpec((ts, tf), lambda i, j: (i, j)), out_shape=jax.ShapeDtypeStruct((s, ff), bf16),
        compiler_params=_params(2),
    )(gu, gu)


def swiglu_bwd(gu, d_act, *, name):
    s, f2 = gu.shape
    ff = f2 // 2
    ts, tf = _tile(s, 512, 16), _tile(ff, 1408)
    nf = ff // tf

    def body(g_ref, u_ref, d_ref, o_ref):
        g, d = g_ref[...], d_ref[...]
        sig = _sigmoid(g)
        d_gate = d * u_ref[...] * sig * (1.0 + g * (1.0 - sig))
        d_up = d * g * sig
        o_ref[...] = jnp.where(pl.program_id(1) < nf, d_gate, d_up).astype(bf16)

    return pl.pallas_call(
        body, name=name, grid=(s // ts, 2 * nf),
        in_specs=[pl.BlockSpec((ts, tf), lambda i, j: (i, j % nf)), pl.BlockSpec((ts, tf), lambda i, j: (i, nf + j % nf)),
                  pl.BlockSpec((ts, tf), lambda i, j: (i, j % nf))],
        out_specs=pl.BlockSpec((ts, tf), lambda i, j: (i, j)), out_shape=jax.ShapeDtypeStruct((s, f2), bf16),
        compiler_params=_params(2),
    )(gu, gu, d_act)


def adamw(w, m, v, g_parts, *, name):
    r = w.shape[0]
    tr = _tile(r, 1024, 8)
    n_parts = g_parts.shape[0]

    def body(w_ref, m_ref, v_ref, gp_ref, g_ref, d_ref, nm_ref, nv_ref):
        g = gp_ref[0]
        for k in range(1, n_parts):
            g = g + gp_ref[k]
        nm = ADAM_B1 * m_ref[...] + (1.0 - ADAM_B1) * g
        nv = ADAM_B2 * v_ref[...] + (1.0 - ADAM_B2) * jnp.square(g)
        m_hat = nm / (1.0 - ADAM_B1 ** ADAM_STEP)
        v_hat = nv / (1.0 - ADAM_B2 ** ADAM_STEP)
        g_ref[...] = g
        d_ref[...] = -ADAM_LR * (m_hat / (jnp.sqrt(v_hat) + ADAM_EPS) + ADAM_WD * w_ref[...])
        nm_ref[...] = nm
        nv_ref[...] = nv

    row = pl.BlockSpec((tr, LANES), lambda i: (i, 0))
    return pl.pallas_call(
        body, name=name, grid=(r // tr,),
        in_specs=[row, row, row, pl.BlockSpec((n_parts, tr, LANES), lambda i: (0, i, 0))],
        out_specs=[row] * 4, out_shape=[jax.ShapeDtypeStruct((r, LANES), f32)] * 4, compiler_params=_params(1),
    )(w, m, v, g_parts)


def _mesh_pos():
    return lax.axis_index("x"), lax.axis_index("y"), lax.axis_index("c")


def _dev_index(p):
    return 4 * p[0] + 2 * p[1] + p[2]


def all_gather(blocks, *, name):
    n = len(blocks)
    any_spec = pl.BlockSpec(memory_space=pl.ANY)

    def body(*refs):
        ins, outs = refs[:n], refs[n:2 * n]
        send_sems, recv_sems, local_sems = refs[2 * n:]
        x, y, c = _mesh_pos()
        me, sibling = (x, y, c), (x, y, 1 - c)
        chips = [(1 - x, y), (x, 1 - y), (1 - x, 1 - y)]

        def copy(a, k, block, to, src=None):
            dst = outs[a].at[_dev_index(block)]
            return pltpu.make_async_remote_copy(
                src_ref=dst if src is None else src, dst_ref=dst, send_sem=send_sems.at[a, k],
                recv_sem=recv_sems.at[a, k], device_id=to, device_id_type=pl.DeviceIdType.MESH)

        mine = [pltpu.make_async_copy(ins[a], outs[a].at[_dev_index(me)], local_sems.at[a]) for a in range(n)]
        for cp in mine:
            cp.start()
        first = []
        for a in range(n):
            first.append(copy(a, 0, me, sibling, src=ins[a]))
            first += [copy(a, 1 + j, me, (*chip, c), src=ins[a]) for j, chip in enumerate(chips)]
        for cp in first:
            cp.start()
        passed = []
        for j, chip in enumerate(chips):
            for a in range(n):
                copy(a, 1 + j, (*chip, c), me).wait_recv()
                fwd = copy(a, 4 + j, (*chip, c), sibling)
                fwd.start()
                passed.append(fwd)
        for a in range(n):
            copy(a, 0, sibling, me).wait_recv()
            for j, chip in enumerate(chips):
                copy(a, 4 + j, (*chip, 1 - c), me).wait_recv()
        for cp in first + passed:
            cp.wait_send()
        for cp in mine:
            cp.wait()

    return pl.pallas_call(
        body, name=name, in_specs=[any_spec] * n, out_specs=[any_spec] * n,
        out_shape=[jax.ShapeDtypeStruct((N_DEV,) + b.shape, b.dtype) for b in blocks],
        scratch_shapes=[pltpu.SemaphoreType.DMA((n, 7)), pltpu.SemaphoreType.DMA((n, 7)),
                        pltpu.SemaphoreType.DMA((n,))],
    )(*blocks)


def exchange_grads(parts, small, *, name):
    any_spec = pl.BlockSpec(memory_space=pl.ANY)

    def body(parts_ref, small_ref, recv_ref, small_all_ref, send_sems, recv_sems, local_sems):
        x, y, c = _mesh_pos()
        mine = _dev_index((x, y, c))
        local = [pltpu.make_async_copy(parts_ref.at[mine], recv_ref.at[mine], local_sems.at[0]),
                 pltpu.make_async_copy(small_ref, small_all_ref.at[mine], local_sems.at[1])]
        for cp in local:
            cp.start()
        sends, arrivals = [], []
        for k in range(1, N_DEV):
            peer = (1 - x if k & 4 else x, 1 - y if k & 2 else y, 1 - c if k & 1 else c)
            theirs = _dev_index(peer)
            for a, (src, dst_mine, dst_theirs) in enumerate((
                    (parts_ref.at[theirs], recv_ref.at[mine], recv_ref.at[theirs]),
                    (small_ref, small_all_ref.at[mine], small_all_ref.at[theirs]))):
                sems = dict(send_sem=send_sems.at[a, k - 1], recv_sem=recv_sems.at[a, k - 1], device_id=peer,
                            device_id_type=pl.DeviceIdType.MESH)
                sends.append(pltpu.make_async_remote_copy(src_ref=src, dst_ref=dst_mine, **sems))
                arrivals.append(pltpu.make_async_remote_copy(src_ref=src, dst_ref=dst_theirs, **sems))
        for cp in sends:
            cp.start()
        for cp in arrivals:
            cp.wait_recv()
        for cp in sends:
            cp.wait_send()
        for cp in local:
            cp.wait()

    return pl.pallas_call(
        body, name=name, in_specs=[any_spec] * 2, out_specs=[any_spec] * 2,
        out_shape=[jax.ShapeDtypeStruct(parts.shape, parts.dtype),
                   jax.ShapeDtypeStruct((N_DEV,) + small.shape, small.dtype)],
        scratch_shapes=[pltpu.SemaphoreType.DMA((2, 7)), pltpu.SemaphoreType.DMA((2, 7)),
                        pltpu.SemaphoreType.DMA((2,))],
    )(parts, small)


def _flat_rows(vecs, row_mult, dtype):
    flat = jnp.concatenate([v.reshape(-1).astype(dtype) for v in vecs])
    rows = -(-flat.shape[0] // (LANES * row_mult)) * row_mult
    return jnp.pad(flat, (0, rows * LANES - flat.shape[0])).reshape(rows, LANES)


def _split_flat(flat, shapes):
    out, off = [], 0
    for shp in shapes:
        size = 1
        for dim in shp:
            size *= dim
        out.append(flat[..., off:off + size].reshape(flat.shape[:-1] + tuple(shp)))
        off += size
    return out


def _unshard(g, axis):
    g = jnp.moveaxis(g, 0, axis)
    shp = g.shape
    return g.reshape(shp[:axis] + (shp[axis] * shp[axis + 1],) + shp[axis + 2:])


def _shard_parts(full, axis):
    shp = full.shape
    g = full.reshape(shp[:axis] + (N_DEV, shp[axis] // N_DEV) + shp[axis + 1:])
    return jnp.moveaxis(g, axis, 0).reshape(N_DEV, -1)


def _pad_lanes(row, width=LANES, at=0):
    return jnp.pad(row, (at, width - at - row.shape[0])).reshape(1, width)


def kernel(x, positions, attn_norm, w_in, sgu_ln_g, sgu_ln_b, sgu_w, sgu_b, attn_sinks, dn_conv_w, dn_a_log, dn_dt_bias, dn_norm, w_branch, w_out, ffn_norm, w_gate_up, w_down, final_norm, loss_target, m_attn_norm, m_w_in, m_sgu_ln_g, m_sgu_ln_b, m_sgu_w, m_sgu_b, m_attn_sinks, m_dn_conv_w, m_dn_a_log, m_dn_dt_bias, m_dn_norm, m_w_branch, m_w_out, m_ffn_norm, m_w_gate_up, m_w_down, m_final_norm, v_attn_norm, v_w_in, v_sgu_ln_g, v_sgu_ln_b, v_sgu_w, v_sgu_b, v_attn_sinks, v_dn_conv_w, v_dn_a_log, v_dn_dt_bias, v_dn_norm, v_w_branch, v_w_out, v_ffn_norm, v_w_gate_up, v_w_down, v_final_norm):
    given = dict(locals())
    depth, d_model = attn_norm.shape
    s = x.shape[1]
    d_ff = w_down.shape[1] * N_DEV
    x2 = x.reshape(s, d_model)
    target = loss_target.reshape(s, d_model)
    posf = positions.reshape(s, 1).astype(f32)
    inv_freq = ROPE_THETA ** (-jnp.arange(0, ROPE_DIM, 2, dtype=f32) / ROPE_DIM)
    inv_head = jnp.concatenate([inv_freq, inv_freq, jnp.zeros((SWA_HD - ROPE_DIM,), f32)])
    inv_q = jnp.tile(inv_head, SWA_HEADS).reshape(1, MIX)
    inv_k = jnp.tile(inv_head, SWA_KV).reshape(1, LANES)

    big_names = ["w_in", "w_branch", "w_out", "w_gate_up", "w_down"]
    big_axis = {"w_in": 2, "w_branch": 3, "w_out": 1, "w_gate_up": 2, "w_down": 1, "dn_conv_w": 2}
    got_w, got_conv = all_gather([_flat_rows([given[n] for n in big_names], 16, bf16),
                                  _flat_rows([dn_conv_w], 8, f32)], name="gather_weights")
    full = {n: _unshard(g, big_axis[n]) for n, g in zip(
        big_names, _split_flat(got_w.reshape(N_DEV, -1), [given[n].shape for n in big_names]))}
    conv_full = _unshard(_split_flat(got_conv.reshape(N_DEV, -1), [dn_conv_w.shape])[0], 2)

    c_b, c_qkv, c_z, c_ba, c_g = 2 * MIX, 2 * MIX + 768, 2 * MIX + 768 + 3 * MIX, 3840, 3848
    layers = []
    for l in range(depth):
        wi = full["w_in"][l]
        layers.append(dict(
            w_main=jnp.concatenate([wi[:, :c_b], wi[:, c_z:c_ba], wi[:, c_qkv:c_z], wi[:, c_g:], wi[:, c_b:c_qkv]], axis=1),
            w_ba=jnp.pad(wi[:, c_ba:c_g], ((0, 0), (0, LANES - (c_g - c_ba)))),
            w_branch=full["w_branch"][l], w_out=full["w_out"][l], w_gu=full["w_gate_up"][l], w_down=full["w_down"][l],
            conv_w8=jnp.pad(conv_full[l], ((0, CONV_PAD - DN_CONV), (0, 0))),
            attn_norm=attn_norm[l].reshape(1, -1), ffn_norm=ffn_norm[l].reshape(1, -1),
            ln_g=sgu_ln_g[l].reshape(1, -1), ln_b=sgu_ln_b[l].reshape(1, -1), sgu_w=sgu_w[l], sgu_bt=sgu_b[l].T,
            sink_row=_pad_lanes(attn_sinks[l]), alog_row=_pad_lanes(dn_a_log[l], at=DN_HEADS),
            dtb_row=_pad_lanes(dn_dt_bias[l], at=DN_HEADS), norm_row=dn_norm[l].reshape(1, -1)))

    saved = []
    h_in = x2
    for l, p in enumerate(layers):
        t = f"l{l}_"
        h = rmsnorm_fwd(h_in, p["attn_norm"], name=t + "attn_norm")
        proj = matmul(h, p["w_main"], "nn", f32, name=t + "in_proj")
        p_ba = matmul(h, p["w_ba"], "nn", f32, name=t + "in_proj_ba")
        out_a = sgu_fwd(proj, p["ln_g"], p["ln_b"], p["sgu_w"], p["sgu_bt"], name=t + "sgu")
        out_b = swa_fwd(proj, p["sink_row"], posf, inv_q, inv_k, name=t + "swa")
        xc = conv_fwd(proj, p["conv_w8"], name=t + "dn_conv")
        out_c, states = dn_fwd(xc, proj, p_ba, p["alog_row"], p["dtb_row"], p["norm_row"], name=t + "deltanet")
        merged = merge_fwd(out_a, out_b, out_c, proj, p["w_branch"], name=t + "merge")
        x_mid = matmul(merged, p["w_out"], "nn", f32, residual=h_in, name=t + "out_proj")
        h2 = rmsnorm_fwd(x_mid, p["ffn_norm"], name=t + "ffn_norm")
        gu = matmul(h2, p["w_gu"], "nn", f32, tn=1408, name=t + "gate_up")
        act = swiglu_fwd(gu, name=t + "swiglu")
        x_out = matmul(act, p["w_down"], "nn", f32, residual=x_mid, tk=1408, name=t + "down")
        saved.append(dict(x_in=h_in, h=h, proj=proj, p_ba=p_ba, out_a=out_a, out_b=out_b, out_c=out_c, xc=xc,
                          states=states, merged=merged, x_mid=x_mid, h2=h2, gu=gu, act=act))
        h_in = x_out

    dx, d_final_norm, loss_row = loss_head(h_in, final_norm.reshape(1, -1), target, name="loss_head")
    loss = lax.psum(loss_row[0, 0], MESH_AXES)

    grads = {}
    per_layer = []
    for l in reversed(range(depth)):
        p, sv, t = layers[l], saved[l], f"l{l}_b_"
        d_act = matmul(dx, p["w_down"], "nt", f32, tn=1408, name=t + "down_dx")
        gw_down = matmul(sv["act"], dx, "tn", f32, tm=1408, name=t + "down_dw")
        d_gu = swiglu_bwd(sv["gu"], d_act, name=t + "swiglu")
        gw_gu = matmul(sv["h2"], d_gu, "tn", f32, tn=1408, name=t + "gate_up_dw")
        d_h2 = matmul(d_gu, p["w_gu"], "nt", f32, tk=1408, name=t + "gate_up_dx")
        dx_mid, g_ffn = rmsnorm_bwd(sv["x_mid"], p["ffn_norm"], d_h2, dx, name=t + "ffn_norm")
        d_merged = matmul(dx_mid, p["w_out"], "nt", f32, name=t + "out_proj_dx")
        gw_out = matmul(sv["merged"], dx_mid, "tn", f32, name=t + "out_proj_dw")
        dproj = jnp.zeros((s, W_MAIN), bf16)
        dproj, d_a, d_b, d_c, gw_branch = merge_bwd(sv["out_a"], sv["out_b"], sv["out_c"], sv["proj"], p["w_branch"],
                                                   d_merged, dproj, name=t + "merge")
        dproj, g_ln_g, g_ln_b, g_sgu_w, g_sgu_bt = sgu_bwd(sv["proj"], p["ln_g"], p["ln_b"], p["sgu_w"], p["sgu_bt"],
                                                         d_a, dproj, name=t + "sgu")
        dproj, g_sink = swa_bwd(sv["proj"], p["sink_row"], posf, inv_q, inv_k, d_b, dproj, name=t + "swa")
        dproj, dxc, dba, g_alog, g_dtb, g_dnorm = dn_bwd(sv["xc"], sv["proj"], sv["p_ba"], p["alog_row"], p["dtb_row"],
                                                        p["norm_row"], sv["states"], d_c, dproj, name=t + "deltanet")
        dproj, g_conv8 = conv_bwd(sv["proj"], p["conv_w8"], dxc, dproj, name=t + "dn_conv")
        gw_main = matmul(sv["h"], dproj, "tn", f32, name=t + "in_proj_dw")
        gw_ba = matmul(sv["h"], dba, "tn", f32, name=t + "in_proj_ba_dw")
        d_h = matmul(dproj, p["w_main"], "nt", f32, tk=768, name=t + "in_proj_dx")
        d_h = matmul(dba, p["w_ba"], "nt", f32, residual=d_h, name=t + "in_proj_ba_dx")
        dx, g_attn = rmsnorm_bwd(sv["x_in"], p["attn_norm"], d_h, dx_mid, name=t + "attn_norm")
        gw_in = jnp.concatenate([gw_main[:, SEC_A:SEC_Z], gw_main[:, SEC_B:], gw_main[:, SEC_QKV:SEC_G],
                                 gw_main[:, SEC_Z:SEC_QKV], gw_ba[:, :c_g - c_ba], gw_main[:, SEC_G:SEC_B]], axis=1)
        per_layer.append(dict(
            attn_norm=g_attn[0], w_in=gw_in, sgu_ln_g=g_ln_g[0], sgu_ln_b=g_ln_b[0], sgu_w=g_sgu_w, sgu_b=g_sgu_bt.T,
            attn_sinks=g_sink[0, :SWA_HEADS], dn_conv_w=g_conv8[:DN_CONV], dn_a_log=g_alog[0, DN_HEADS:2 * DN_HEADS],
            dn_dt_bias=g_dtb[0, DN_HEADS:2 * DN_HEADS], dn_norm=g_dnorm[0], w_branch=gw_branch, w_out=gw_out,
            ffn_norm=g_ffn[0], w_gate_up=gw_gu, w_down=gw_down))
    per_layer.reverse()
    for n in per_layer[0]:
        grads[n] = jnp.stack([pp[n] for pp in per_layer])
    grads["final_norm"] = d_final_norm[0]

    shard_names = ["w_in", "dn_conv_w", "w_branch", "w_out", "w_gate_up", "w_down"]
    rep_names = ["attn_norm", "sgu_ln_g", "sgu_ln_b", "sgu_w", "sgu_b", "attn_sinks", "dn_a_log", "dn_dt_bias",
                 "dn_norm", "ffn_norm", "final_norm"]
    shard_w = _flat_rows([given[n] for n in shard_names], 1024, f32)
    rows = shard_w.shape[0]
    parts = jnp.concatenate([_shard_parts(grads[n], big_axis[n]) for n in shard_names], axis=1)
    parts = jnp.pad(parts, ((0, 0), (0, rows * LANES - parts.shape[1]))).reshape(N_DEV, rows, LANES)
    rep_w = _flat_rows([given[n] for n in rep_names], 128, f32)
    small = _flat_rows([grads[n] for n in rep_names], 128, f32)
    recv, small_all = exchange_grads(parts, small, name="exchange_grads")

    outs_sh = adamw(shard_w, _flat_rows([given["m_" + n] for n in shard_names], 1024, f32),
                    _flat_rows([given["v_" + n] for n in shard_names], 1024, f32), recv, name="adamw_sharded")
    outs_rep = adamw(rep_w, _flat_rows([given["m_" + n] for n in rep_names], 128, f32),
                     _flat_rows([given["v_" + n] for n in rep_names], 128, f32), small_all, name="adamw_replicated")
    results = [{}, {}, {}, {}]
    for names, outs in ((shard_names, outs_sh), (rep_names, outs_rep)):
        for res, flat in zip(results, outs):
            for n, val in zip(names, _split_flat(flat.reshape(-1), [given[n].shape for n in names])):
                res[n] = val
    order = ["attn_norm", "w_in", "sgu_ln_g", "sgu_ln_b", "sgu_w", "sgu_b", "attn_sinks", "dn_conv_w", "dn_a_log",
             "dn_dt_bias", "dn_norm", "w_branch", "w_out", "ffn_norm", "w_gate_up", "w_down", "final_norm"]
    return (loss, dx.reshape(x.shape), *[res[n] for res in results for n in order])
```

```python
import functools

import jax
import jax.numpy as jnp
from jax import lax
from jax.experimental import pallas as pl
from jax.experimental.pallas import tpu as pltpu

f32 = jnp.float32
bf16 = jnp.bfloat16

N_DEV = 8
MESH_AXES = ("x", "y", "c")
NORM_EPS = 1e-6
MIX = 512
SGU_GROUPS, SGU_CHUNK = 4, 128
SWA_HEADS, SWA_KV, SWA_HD, WINDOW = 8, 2, 64, 128
ROPE_THETA, ROPE_DIM = 500000.0, 16
DN_HEADS, DN_HD, DN_CONV, DN_CHUNK = 4, 128, 4, 64
ADAM_LR, ADAM_B1, ADAM_B2, ADAM_EPS, ADAM_WD, ADAM_STEP = 0.001, 0.9, 0.999, 1e-08, 0.01, 10

LANES = 128
VMEM_LIMIT = 56 * 1024 * 1024

SEC_A, SEC_Z, SEC_QKV, SEC_G, SEC_B = 0, 1024, 1536, 3072, 6144
W_MAIN = 6912
C_B, C_QKV, C_Z, C_BA, C_G = 1024, 1792, 3328, 3840, 3848


def _params(n_axes, **kw):
    return pltpu.CompilerParams(dimension_semantics=("arbitrary",) * n_axes, vmem_limit_bytes=VMEM_LIMIT, **kw)


def _tile(n, target, mult=LANES):
    if n <= target:
        return n
    best = None
    for t in range(mult, target + 1, mult):
        if n % t == 0:
            best = t
    assert best is not None, (n, target, mult)
    return best


def _dg(a, b, ca, cb):
    return lax.dot_general(a.astype(bf16), b.astype(bf16), (((ca,), (cb,)), ((), ())), preferred_element_type=f32)


def _dg3(a, b, ca, cb):
    a_hi, b_hi = a.astype(bf16), b.astype(bf16)
    a_lo, b_lo = (a - a_hi.astype(f32)).astype(bf16), (b - b_hi.astype(f32)).astype(bf16)

    def dot(p, q):
        return lax.dot_general(p, q, (((ca,), (cb,)), ((), ())), preferred_element_type=f32)

    return dot(a_hi, b_hi) + (dot(a_hi, b_lo) + dot(a_lo, b_hi))


def _differentiable_dot(core):
    @functools.partial(jax.custom_vjp, nondiff_argnums=(2, 3))
    def dot(a, b, ca, cb):
        return core(a, b, ca, cb)

    def fwd(a, b, ca, cb):
        return core(a, b, ca, cb), (a, b)

    def bwd(ca, cb, res, ct):
        a, b = res
        da = core(ct, b, 1, 1 - cb) if ca == 1 else core(b, ct, 1 - cb, 1)
        db = core(a, ct, 1 - ca, 0) if cb == 0 else core(ct, a, 0, 1 - ca)
        return da, db

    dot.defvjp(fwd, bwd)
    return dot


bdot = _differentiable_dot(_dg)
_hdot = _differentiable_dot(_dg3)


def hdot(a, b, ca=1, cb=0):
    return _hdot(a, b, ca, cb)


@functools.partial(jax.custom_vjp, nondiff_argnums=(1,))
def lroll(x, shift):
    return pltpu.roll(x, shift, 1)


def _lroll_fwd(x, shift):
    return pltpu.roll(x, shift, 1), None


def _lroll_bwd(shift, _, ct):
    return (pltpu.roll(ct, ct.shape[1] - shift, 1),)


lroll.defvjp(_lroll_fwd, _lroll_bwd)


@jax.custom_vjp
def tri_inv(low):
    n = low.shape[0]
    row = lax.broadcasted_iota(jnp.int32, (n, n), 0)
    col = lax.broadcasted_iota(jnp.int32, (n, n), 1)
    eye = (row == col).astype(f32)
    m = -low
    p = eye + m
    span = 2
    while span < n:
        m = hdot(m, m)
        p = p + hdot(p, m)
        span *= 2
    return p


def _tri_inv_fwd(low):
    t = tri_inv(low)
    return t, t


def _tri_inv_bwd(t, dt):
    return (-hdot(t, hdot(dt, t, 1, 1), 0, 0),)


tri_inv.defvjp(_tri_inv_fwd, _tri_inv_bwd)


def _sigmoid(x):
    return 1.0 / (1.0 + jnp.exp(-x))


def _rms(x, g):
    return x * lax.rsqrt(jnp.mean(x * x, axis=-1, keepdims=True) + NORM_EPS) * g


def _lane_col(x, lane_idx):
    lane = lax.broadcasted_iota(jnp.int32, x.shape, 1)
    return jnp.sum(jnp.where(lane == lane_idx, x, 0.0), axis=1, keepdims=True)


def matmul(a, b, mode, out_dtype, *, residual=None, group=None, tm=1024, tn=768, tk=1024, name):
    dims = {"a": ("m", "k") if mode != "tn" else ("k", "m"),
            "b": {"nn": ("k", "n"), "nt": ("n", "k"), "tn": ("k", "n")}[mode], "o": ("m", "n")}
    full, groups = {}, 1
    for arr, key in ((a, "a"), (b, "b")):
        grouped = group in dims[key]
        if grouped:
            groups = arr.shape[0]
        full[dims[key][0]], full[dims[key][1]] = arr.shape[1:] if grouped else arr.shape
    want = {"m": tm, "n": tn, "k": tk}
    tiles = {d: full[d] if d == group else _tile(full[d], want[d]) for d in "mnk"}
    steps = {d: groups if d == group else full[d] // tiles[d] for d in "mnk"}

    def spec(key):
        d0, d1 = dims[key]

        def index(i, j, kk):
            at = {"m": i, "n": j, "k": kk}
            if group in (d0, d1):
                return (at[group], 0 if d0 == group else at[d0], 0 if d1 == group else at[d1])
            return (at[d0], at[d1])

        block = (tiles[d0], tiles[d1])
        return pl.BlockSpec(((None,) + block) if group in (d0, d1) else block, index)

    ca, cb = {"nn": (1, 0), "nt": (1, 1), "tn": (0, 0)}[mode]
    nk = steps["k"]
    o_spec = spec("o")
    out_shape = (groups, full["m"], full["n"]) if group in ("m", "n") else (full["m"], full["n"])
    has_res = residual is not None

    def body(*refs):
        if has_res:
            a_ref, b_ref, r_ref, o_ref, acc_ref = refs
        else:
            a_ref, b_ref, o_ref, acc_ref = refs
        kk = pl.program_id(2)

        @pl.when(kk == 0)
        def _():
            acc_ref[...] = jnp.zeros_like(acc_ref)

        acc_ref[...] += _dg(a_ref[...], b_ref[...], ca, cb)

        @pl.when(kk == nk - 1)
        def _():
            acc = acc_ref[...]
            if has_res:
                acc = acc + r_ref[...]
            o_ref[...] = acc.astype(out_dtype)

    in_specs = [spec("a"), spec("b")] + ([o_spec] if has_res else [])
    args = (a, b) + ((residual,) if has_res else ())
    return pl.pallas_call(
        body, name=name, grid=(steps["m"], steps["n"], nk), in_specs=in_specs, out_specs=o_spec,
        out_shape=jax.ShapeDtypeStruct(out_shape, out_dtype),
        scratch_shapes=[pltpu.VMEM((tiles["m"], tiles["n"]), f32)], compiler_params=_params(3),
    )(*args)


def rmsnorm_fwd(x, g_row, *, name):
    s, d = x.shape
    ts = _tile(s, 512, 16)

    def body(x_ref, g_ref, o_ref):
        o_ref[...] = _rms(x_ref[...], g_ref[...]).astype(bf16)

    return pl.pallas_call(
        body, name=name, grid=(s // ts,),
        in_specs=[pl.BlockSpec((ts, d), lambda i: (i, 0)), pl.BlockSpec((1, d), lambda i: (0, 0))],
        out_specs=pl.BlockSpec((ts, d), lambda i: (i, 0)), out_shape=jax.ShapeDtypeStruct((s, d), bf16),
        compiler_params=_params(1),
    )(x, g_row)


def rmsnorm_bwd(x, g_row, dh, dres, *, name):
    s, d = x.shape
    ts = _tile(s, 512, 16)

    def body(x_ref, g_ref, dh_ref, dres_ref, dx_ref, dg_ref):
        @pl.when(pl.program_id(0) == 0)
        def _():
            dg_ref[...] = jnp.zeros_like(dg_ref)

        _, vjp = jax.vjp(_rms, x_ref[...], g_ref[...])
        dx, dg = vjp(dh_ref[...])
        dx_ref[...] = dx + dres_ref[...]
        dg_ref[...] += dg

    row = pl.BlockSpec((ts, d), lambda i: (i, 0))
    vec = pl.BlockSpec((1, d), lambda i: (0, 0))
    return pl.pallas_call(
        body, name=name, grid=(s // ts,), in_specs=[row, vec, row, row], out_specs=[row, vec],
        out_shape=[jax.ShapeDtypeStruct((s, d), f32), jax.ShapeDtypeStruct((1, d), f32)],
        compiler_params=_params(1),
    )(x, g_row, dh, dres)


def loss_head(x, g_row, target, *, name):
    s, d = x.shape
    ts = _tile(s, 512, 16)

    def body(x_ref, g_ref, t_ref, dx_ref, dg_ref, loss_ref):
        @pl.when(pl.program_id(0) == 0)
        def _():
            dg_ref[...] = jnp.zeros_like(dg_ref)
            loss_ref[...] = jnp.zeros_like(loss_ref)

        y, vjp = jax.vjp(_rms, x_ref[...], g_ref[...])
        err = y - t_ref[...]
        dx, dg = vjp(err * (1.0 / d))
        dx_ref[...] = dx
        dg_ref[...] += dg
        loss_ref[...] += 0.5 * jnp.sum(jnp.sum(err * err, axis=1, keepdims=True) * (1.0 / d), axis=0, keepdims=True)

    row = pl.BlockSpec((ts, d), lambda i: (i, 0))
    vec = pl.BlockSpec((1, d), lambda i: (0, 0))
    one = pl.BlockSpec((1, LANES), lambda i: (0, 0))
    return pl.pallas_call(
        body, name=name, grid=(s // ts,), in_specs=[row, vec, row], out_specs=[row, vec, one],
        out_shape=[jax.ShapeDtypeStruct((s, d), f32), jax.ShapeDtypeStruct((1, d), f32),
                   jax.ShapeDtypeStruct((1, LANES), f32)],
        compiler_params=_params(1),
    )(x, g_row, target)


def _sgu_chunk(p_a, ln_g, ln_b, w, b_t):
    t = SGU_CHUNK
    u = jax.nn.gelu(p_a[:, :MIX])
    v = jax.nn.gelu(p_a[:, MIX:])
    vc = v - jnp.mean(v, axis=-1, keepdims=True)
    vn = vc * lax.rsqrt(jnp.mean(vc * vc, axis=-1, keepdims=True) + NORM_EPS) * ln_g + ln_b
    causal = lax.broadcasted_iota(jnp.int32, (t, t), 0) >= lax.broadcasted_iota(jnp.int32, (t, t), 1)
    outs = []
    for g in range(SGU_GROUPS):
        sl = slice(g * LANES, (g + 1) * LANES)
        mixed = bdot(jnp.where(causal, w[g], 0.0), vn[:, sl], 1, 0) + b_t[:, g:g + 1]
        outs.append(u[:, sl] * mixed)
    return jnp.concatenate(outs, axis=1)


def _sgu_specs(s, ts):
    return [pl.BlockSpec((ts, 2 * MIX), lambda i: (i, SEC_A // (2 * MIX))),
            pl.BlockSpec((1, MIX), lambda i: (0, 0)), pl.BlockSpec((1, MIX), lambda i: (0, 0)),
            pl.BlockSpec((SGU_GROUPS, SGU_CHUNK, SGU_CHUNK), lambda i: (0, 0, 0)),
            pl.BlockSpec((SGU_CHUNK, SGU_GROUPS), lambda i: (0, 0))]


def sgu_fwd(proj, ln_g, ln_b, w, b_t, *, name):
    s = proj.shape[0]
    ts = _tile(s, 512)
    n_chunk = ts // SGU_CHUNK

    def body(p_ref, g_ref, b_ref, w_ref, bt_ref, o_ref):
        def step(c, carry):
            rows = pl.ds(pl.multiple_of(c * SGU_CHUNK, SGU_CHUNK), SGU_CHUNK)
            o_ref[rows, :] = _sgu_chunk(p_ref[rows, :], g_ref[...], b_ref[...], w_ref[...], bt_ref[...]).astype(bf16)
            return carry
        lax.fori_loop(0, n_chunk, step, 0)

    return pl.pallas_call(
        body, name=name, grid=(s // ts,), in_specs=_sgu_specs(s, ts),
        out_specs=pl.BlockSpec((ts, MIX), lambda i: (i, 0)), out_shape=jax.ShapeDtypeStruct((s, MIX), bf16),
        compiler_params=_params(1),
    )(proj, ln_g, ln_b, w, b_t)


def sgu_bwd(proj, ln_g, ln_b, w, b_t, d_out, dproj, *, name):
    s = proj.shape[0]
    ts = _tile(s, 512)
    n_chunk = ts // SGU_CHUNK

    def body(p_ref, g_ref, b_ref, w_ref, bt_ref, do_ref, _, dp_ref, dg_ref, db_ref, dw_ref, dbt_ref):
        @pl.when(pl.program_id(0) == 0)
        def _():
            dg_ref[...] = jnp.zeros_like(dg_ref)
            db_ref[...] = jnp.zeros_like(db_ref)
            dw_ref[...] = jnp.zeros_like(dw_ref)
            dbt_ref[...] = jnp.zeros_like(dbt_ref)

        def step(c, carry):
            rows = pl.ds(pl.multiple_of(c * SGU_CHUNK, SGU_CHUNK), SGU_CHUNK)
            _, vjp = jax.vjp(_sgu_chunk, p_ref[rows, :], g_ref[...], b_ref[...], w_ref[...], bt_ref[...])
            dp, dg, db, dw, dbt = vjp(do_ref[rows, :])
            dp_ref[rows, :] = dp.astype(bf16)
            dg_ref[...] += dg
            db_ref[...] += db
            dw_ref[...] += dw
            dbt_ref[...] += dbt
            return carry
        lax.fori_loop(0, n_chunk, step, 0)

    specs = _sgu_specs(s, ts)
    return pl.pallas_call(
        body, name=name, grid=(s // ts,),
        in_specs=specs + [pl.BlockSpec((ts, MIX), lambda i: (i, 0)), pl.BlockSpec(memory_space=pl.ANY)],
        out_specs=[specs[0], specs[1], specs[2], specs[3], specs[4]],
        out_shape=[jax.ShapeDtypeStruct(dproj.shape, bf16), jax.ShapeDtypeStruct((1, MIX), f32),
                   jax.ShapeDtypeStruct((1, MIX), f32), jax.ShapeDtypeStruct(w.shape, f32),
                   jax.ShapeDtypeStruct(b_t.shape, f32)],
        input_output_aliases={6: 0}, compiler_params=_params(1),
    )(proj, ln_g, ln_b, w, b_t, d_out, dproj)


def _rope(x, pos, inv_freq):
    w = x.shape[1]
    d = lax.broadcasted_iota(jnp.int32, (1, w), 1) % SWA_HD
    half = ROPE_DIM // 2
    ang = pos * inv_freq
    c, s = jnp.cos(ang), jnp.sin(ang)
    lo = jnp.where(d < half, s, 0.0)
    hi = jnp.where((d >= half) & (d < ROPE_DIM), s, 0.0)
    return x * c - lroll(x, w - half) * lo + lroll(x, half) * hi


def _swa_block(q, kp, kc, vp, vc, sink_row, pos_q, pos_p, inv_q, inv_k, prev_ok):
    t = WINDOW
    q = _rope(q, pos_q, inv_q) * (SWA_HD ** -0.5)
    keys = jnp.concatenate([_rope(kp, pos_p, inv_k), _rope(kc, pos_q, inv_k)], axis=0)
    vals = jnp.concatenate([vp, vc], axis=0)
    qi = lax.broadcasted_iota(jnp.int32, (t, 2 * t), 0)
    sj = lax.broadcasted_iota(jnp.int32, (t, 2 * t), 1)
    diff = qi + t - sj
    mask = (diff >= 0) & (diff < t) & (prev_ok | (sj >= t))
    lane_half = lax.broadcasted_iota(jnp.int32, (t, LANES), 1) // SWA_HD
    group = SWA_HEADS // SWA_KV
    slabs = []
    for pair in range(SWA_HEADS // 2):
        q_pair = q[:, pair * LANES:(pair + 1) * LANES]
        acc = jnp.zeros((t, LANES), f32)
        for half in range(2):
            h = 2 * pair + half
            kv = h // group
            qm = jnp.where(lane_half == half, q_pair, 0.0)
            if half != kv:
                qm = lroll(qm, SWA_HD)
            logits = jnp.where(mask, bdot(qm, keys, 1, 1), -1e30)
            sink = _lane_col(sink_row, h)
            m = jnp.maximum(jnp.max(logits, axis=1, keepdims=True), sink)
            p = jnp.exp(logits - m)
            probs = p / (jnp.sum(p, axis=1, keepdims=True) + jnp.exp(sink - m))
            o = jnp.where(lane_half == kv, bdot(probs, vals, 1, 0), 0.0)
            if half != kv:
                o = lroll(o, SWA_HD)
            acc = acc + o
        slabs.append(acc)
    return jnp.concatenate(slabs, axis=1)


def _swa_in_specs(nc, clamp):
    t = WINDOW
    qb, kb, vb = SEC_B // MIX, (SEC_B + MIX) // LANES, (SEC_B + MIX + LANES) // LANES

    def cur(i):
        return jnp.minimum(i, nc - 1) if clamp else i

    def prev(i):
        return jnp.maximum(cur(i) - 1, 0)

    return [pl.BlockSpec((t, MIX), lambda i: (cur(i), qb)),
            pl.BlockSpec((t, LANES), lambda i: (prev(i), kb)), pl.BlockSpec((t, LANES), lambda i: (cur(i), kb)),
            pl.BlockSpec((t, LANES), lambda i: (prev(i), vb)), pl.BlockSpec((t, LANES), lambda i: (cur(i), vb)),
            pl.BlockSpec((1, LANES), lambda i: (0, 0)),
            pl.BlockSpec((t, 1), lambda i: (cur(i), 0)), pl.BlockSpec((t, 1), lambda i: (prev(i), 0)),
            pl.BlockSpec((1, MIX), lambda i: (0, 0)), pl.BlockSpec((1, LANES), lambda i: (0, 0))]


def swa_fwd(proj, sink_row, posf, inv_q, inv_k, *, name):
    s = proj.shape[0]
    nc = s // WINDOW

    def body(q_ref, kp_ref, kc_ref, vp_ref, vc_ref, sink_ref, pq_ref, pp_ref, iq_ref, ik_ref, o_ref):
        prev_ok = pl.program_id(0) > 0
        o_ref[...] = _swa_block(q_ref[...], kp_ref[...], kc_ref[...], vp_ref[...], vc_ref[...], sink_ref[...],
                                pq_ref[...], pp_ref[...], iq_ref[...], ik_ref[...], prev_ok).astype(bf16)

    return pl.pallas_call(
        body, name=name, grid=(nc,), in_specs=_swa_in_specs(nc, False),
        out_specs=pl.BlockSpec((WINDOW, MIX), lambda i: (i, 0)), out_shape=jax.ShapeDtypeStruct((s, MIX), bf16),
        compiler_params=_params(1),
    )(proj, proj, proj, proj, proj, sink_row, posf, posf, inv_q, inv_k)


def swa_bwd(proj, sink_row, posf, inv_q, inv_k, d_out, dproj, *, name):
    s = proj.shape[0]
    nc = s // WINDOW
    t = WINDOW

    def body(q_ref, kp_ref, kc_ref, vp_ref, vc_ref, sink_ref, pq_ref, pp_ref, iq_ref, ik_ref, do_ref, _,
             dp_ref, dsink_ref, cq_ref, ck_ref, cv_ref):
        i = pl.program_id(0)

        @pl.when(i == 0)
        def _():
            dsink_ref[...] = jnp.zeros_like(dsink_ref)

        def write(dk_prev, dv_prev):
            dp_ref[:, :MIX] = cq_ref[...].astype(bf16)
            dp_ref[:, MIX:MIX + LANES] = (ck_ref[...] + dk_prev).astype(bf16)
            dp_ref[:, MIX + LANES:] = (cv_ref[...] + dv_prev).astype(bf16)

        @pl.when(i < nc)
        def _():
            fn = functools.partial(_swa_block, pos_q=pq_ref[...], pos_p=pp_ref[...], inv_q=iq_ref[...],
                                   inv_k=ik_ref[...], prev_ok=i > 0)
            _, vjp = jax.vjp(fn, q_ref[...], kp_ref[...], kc_ref[...], vp_ref[...], vc_ref[...], sink_ref[...])
            dq, dkp, dkc, dvp, dvc, dsink = vjp(do_ref[...])
            dsink_ref[...] += dsink

            @pl.when(i > 0)
            def _():
                write(dkp, dvp)

            cq_ref[...] = dq
            ck_ref[...] = dkc
            cv_ref[...] = dvc

        @pl.when(i == nc)
        def _():
            write(0.0, 0.0)

    return pl.pallas_call(
        body, name=name, grid=(nc + 1,),
        in_specs=_swa_in_specs(nc, True) + [pl.BlockSpec((t, MIX), lambda i: (jnp.minimum(i, nc - 1), 0)),
                                            pl.BlockSpec(memory_space=pl.ANY)],
        out_specs=[pl.BlockSpec((t, MIX + 2 * LANES), lambda i: (jnp.maximum(i - 1, 0), SEC_B // (MIX + 2 * LANES))),
                   pl.BlockSpec((1, LANES), lambda i: (0, 0))],
        out_shape=[jax.ShapeDtypeStruct(dproj.shape, bf16), jax.ShapeDtypeStruct((1, LANES), f32)],
        scratch_shapes=[pltpu.VMEM((t, MIX), f32), pltpu.VMEM((t, LANES), f32), pltpu.VMEM((t, LANES), f32)],
        input_output_aliases={11: 0}, compiler_params=_params(1),
    )(proj, proj, proj, proj, proj, sink_row, posf, posf, inv_q, inv_k, d_out, dproj)


CONV_PAD = 8


def _conv_pre(xp, w, rows):
    off = CONV_PAD - (DN_CONV - 1)
    pre = xp[off:off + rows] * w[0:1]
    for i in range(1, DN_CONV):
        pre = pre + xp[off + i:off + i + rows] * w[i:i + 1]
    return pre


def conv_fwd(proj, conv_w8, *, name):
    s = proj.shape[0]
    wq = 3 * MIX
    ts = _tile(s, 512)
    nb = ts // CONV_PAD

    def body(x_ref, prev_ref, w_ref, o_ref):
        prev = jnp.where(pl.program_id(0) > 0, prev_ref[...], 0.0)
        pre = _conv_pre(jnp.concatenate([prev, x_ref[...]], axis=0), w_ref[...], ts)
        o_ref[...] = pre * _sigmoid(pre)

    return pl.pallas_call(
        body, name=name, grid=(s // ts,),
        in_specs=[pl.BlockSpec((ts, wq), lambda i: (i, SEC_QKV // wq)),
                  pl.BlockSpec((CONV_PAD, wq), lambda i: (jnp.maximum(i * nb - 1, 0), SEC_QKV // wq)),
                  pl.BlockSpec((CONV_PAD, wq), lambda i: (0, 0))],
        out_specs=pl.BlockSpec((ts, wq), lambda i: (i, 0)), out_shape=jax.ShapeDtypeStruct((s, wq), f32),
        compiler_params=_params(1),
    )(proj, proj, conv_w8)


def conv_bwd(proj, conv_w8, dxc, dproj, *, name):
    s = proj.shape[0]
    wq = 3 * MIX
    ts = _tile(s, 512)
    nb = ts // CONV_PAD
    nt = s // ts
    last_blk = s // CONV_PAD - 1

    def body(x_ref, prev_ref, next_ref, w_ref, d_ref, dnext_ref, _, dp_ref, dw_ref):
        i = pl.program_id(0)

        @pl.when(i == 0)
        def _():
            dw_ref[...] = jnp.zeros_like(dw_ref)

        w = w_ref[...]
        prev = jnp.where(i > 0, prev_ref[...], 0.0)
        more = i < nt - 1
        xp = jnp.concatenate([prev, x_ref[...], jnp.where(more, next_ref[...], 0.0)], axis=0)
        ext = ts + CONV_PAD
        pre = _conv_pre(xp, w, ext)
        sig = _sigmoid(pre)
        dxc_ext = jnp.concatenate([d_ref[...], jnp.where(more, dnext_ref[...], 0.0)], axis=0)
        dpre = dxc_ext * sig * (1.0 + pre * (1.0 - sig))
        off = CONV_PAD - (DN_CONV - 1)
        d_raw = jnp.zeros((ts, wq), f32)
        dws = []
        for k in range(DN_CONV):
            shift = DN_CONV - 1 - k
            d_raw = d_raw + dpre[shift:shift + ts] * w[k:k + 1]
            dws.append(jnp.sum(dpre[:ts] * xp[off + k:off + k + ts], axis=0, keepdims=True))
        dp_ref[...] = d_raw.astype(bf16)
        dw_ref[...] += jnp.concatenate(dws + [jnp.zeros((CONV_PAD - DN_CONV, wq), f32)], axis=0)

    sec = SEC_QKV // wq
    return pl.pallas_call(
        body, name=name, grid=(nt,),
        in_specs=[pl.BlockSpec((ts, wq), lambda i: (i, sec)),
                  pl.BlockSpec((CONV_PAD, wq), lambda i: (jnp.maximum(i * nb - 1, 0), sec)),
                  pl.BlockSpec((CONV_PAD, wq), lambda i: (jnp.minimum((i + 1) * nb, last_blk), sec)),
                  pl.BlockSpec((CONV_PAD, wq), lambda i: (0, 0)),
                  pl.BlockSpec((ts, wq), lambda i: (i, 0)),
                  pl.BlockSpec((CONV_PAD, wq), lambda i: (jnp.minimum((i + 1) * nb, last_blk), 0)),
                  pl.BlockSpec(memory_space=pl.ANY)],
        out_specs=[pl.BlockSpec((ts, wq), lambda i: (i, sec)), pl.BlockSpec((CONV_PAD, wq), lambda i: (0, 0))],
        out_shape=[jax.ShapeDtypeStruct(dproj.shape, bf16), jax.ShapeDtypeStruct((CONV_PAD, wq), f32)],
        input_output_aliases={6: 0}, compiler_params=_params(1),
    )(proj, proj, proj, conv_w8, dxc, dxc, dproj)


def _dn_chunk(states, xc, z, ba, alog_row, dtb_row, norm_row):
    c = DN_CHUNK
    row = lax.broadcasted_iota(jnp.int32, (c, c), 0)
    col = lax.broadcasted_iota(jnp.int32, (c, c), 1)
    tril, strict = row >= col, row > col
    beta_all = _sigmoid(ba)
    g_all = -jnp.exp(alog_row) * jax.nn.softplus(ba + dtb_row)
    gc_all = hdot(tril.astype(f32), g_all)
    gc_t = gc_all.T
    new_states, outs = [], []
    for h in range(DN_HEADS):
        sl = slice(h * DN_HD, (h + 1) * DN_HD)
        q = xc[:, sl]
        k = xc[:, MIX + h * DN_HD:MIX + (h + 1) * DN_HD]
        v = xc[:, 2 * MIX + h * DN_HD:2 * MIX + (h + 1) * DN_HD]
        q = q * lax.rsqrt(jnp.sum(q * q, axis=-1, keepdims=True) + NORM_EPS) * (DN_HD ** -0.5)
        k = k * lax.rsqrt(jnp.sum(k * k, axis=-1, keepdims=True) + NORM_EPS)
        beta = _lane_col(beta_all, h)
        g_col = _lane_col(gc_all, DN_HEADS + h)
        g_row = gc_t[DN_HEADS + h:DN_HEADS + h + 1, :]
        g_last = g_col[c - 1:c, :]
        decay = jnp.where(tril, jnp.exp(jnp.where(tril, g_col - g_row, 0.0)), 0.0)
        kb = k * beta
        low = jnp.where(strict, bdot(kb, k, 1, 1) * decay, 0.0)
        t_inv = tri_inv(low)
        e_gc = jnp.exp(g_col)
        u = hdot(t_inv, v * beta)
        w = hdot(t_inv, kb * e_gc)
        attn = bdot(q, k, 1, 1) * decay
        state = states[h]
        v_new = u - bdot(w, state, 1, 0)
        o = bdot(q * e_gc, state, 1, 0) + bdot(attn, v_new, 1, 0)
        new_states.append(state * jnp.exp(g_last) + bdot(k * jnp.exp(g_last - g_col), v_new, 0, 0))
        zh = z[:, sl]
        outs.append(_rms(o, norm_row) * (zh * _sigmoid(zh)))
    return tuple(new_states), jnp.concatenate(outs, axis=1)


DN_STEP = 2 * DN_CHUNK


def _dn_step(states, xc, z, ba, alog_row, dtb_row, norm_row):
    outs = []
    for c in range(DN_STEP // DN_CHUNK):
        rows = slice(c * DN_CHUNK, (c + 1) * DN_CHUNK)
        states, out = _dn_chunk(states, xc[rows], z[rows], ba[rows], alog_row, dtb_row, norm_row)
        outs.append(out)
    return states, jnp.concatenate(outs, axis=0)


def _dn_specs(ts, order):
    zb = SEC_Z // MIX
    return [pl.BlockSpec((ts, 3 * MIX), lambda i: (order(i), 0)),
            pl.BlockSpec((ts, MIX), lambda i: (order(i), zb)),
            pl.BlockSpec((ts, LANES), lambda i: (order(i), 0)),
            pl.BlockSpec((1, LANES), lambda i: (0, 0)), pl.BlockSpec((1, LANES), lambda i: (0, 0)),
            pl.BlockSpec((1, LANES), lambda i: (0, 0))]


def dn_fwd(xc, proj, p_ba, alog_row, dtb_row, norm_row, *, name):
    s = xc.shape[0]
    ts = _tile(s, 512)
    n_step = ts // DN_STEP

    def body(xc_ref, z_ref, ba_ref, al_ref, dt_ref, nr_ref, o_ref, st_ref, state_ref):
        @pl.when(pl.program_id(0) == 0)
        def _():
            state_ref[...] = jnp.zeros_like(state_ref)

        def step(c, carry):
            rows = pl.ds(pl.multiple_of(c * DN_STEP, DN_STEP), DN_STEP)
            states = tuple(state_ref[h] for h in range(DN_HEADS))
            st_ref[c] = state_ref[...]
            new_states, out = _dn_step(states, xc_ref[rows, :], z_ref[rows, :], ba_ref[rows, :],
                                       al_ref[...], dt_ref[...], nr_ref[...])
            for h in range(DN_HEADS):
                state_ref[h] = new_states[h]
            o_ref[rows, :] = out.astype(bf16)
            return carry
        lax.fori_loop(0, n_step, step, 0)

    return pl.pallas_call(
        body, name=name, grid=(s // ts,), in_specs=_dn_specs(ts, lambda i: i),
        out_specs=[pl.BlockSpec((ts, MIX), lambda i: (i, 0)),
                   pl.BlockSpec((n_step, DN_HEADS, DN_HD, DN_HD), lambda i: (i, 0, 0, 0))],
        out_shape=[jax.ShapeDtypeStruct((s, MIX), bf16),
                   jax.ShapeDtypeStruct((s // DN_STEP, DN_HEADS, DN_HD, DN_HD), f32)],
        scratch_shapes=[pltpu.VMEM((DN_HEADS, DN_HD, DN_HD), f32)],
        compiler_params=_params(1),
    )(xc, proj, p_ba, alog_row, dtb_row, norm_row)


def dn_bwd(xc, proj, p_ba, alog_row, dtb_row, norm_row, saved, d_out, dproj, *, name):
    s = xc.shape[0]
    ts = _tile(s, 512)
    n_step = ts // DN_STEP
    nt = s // ts

    def body(xc_ref, z_ref, ba_ref, al_ref, dt_ref, nr_ref, st_ref, do_ref, _,
             dz_ref, dxc_ref, dba_ref, dal_ref, ddt_ref, dnr_ref, dstate_ref):
        @pl.when(pl.program_id(0) == 0)
        def _():
            dstate_ref[...] = jnp.zeros_like(dstate_ref)
            dal_ref[...] = jnp.zeros_like(dal_ref)
            ddt_ref[...] = jnp.zeros_like(ddt_ref)
            dnr_ref[...] = jnp.zeros_like(dnr_ref)

        def step(it, carry):
            c = n_step - 1 - it
            rows = pl.ds(pl.multiple_of(c * DN_STEP, DN_STEP), DN_STEP)
            states = tuple(st_ref[c, h] for h in range(DN_HEADS))
            _, vjp = jax.vjp(_dn_step, states, xc_ref[rows, :], z_ref[rows, :], ba_ref[rows, :],
                             al_ref[...], dt_ref[...], nr_ref[...])
            d_states = tuple(dstate_ref[h] for h in range(DN_HEADS))
            d_in, dxc, dz, dba, dal, ddt, dnr = vjp((d_states, do_ref[rows, :]))
            for h in range(DN_HEADS):
                dstate_ref[h] = d_in[h]
            dxc_ref[rows, :] = dxc
            dz_ref[rows, :] = dz.astype(bf16)
            dba_ref[rows, :] = dba.astype(bf16)
            dal_ref[...] += dal
            ddt_ref[...] += ddt
            dnr_ref[...] += dnr
            return carry
        lax.fori_loop(0, n_step, step, 0)

    def rev(i):
        return nt - 1 - i

    specs = _dn_specs(ts, rev)
    vec = pl.BlockSpec((1, LANES), lambda i: (0, 0))
    return pl.pallas_call(
        body, name=name, grid=(nt,),
        in_specs=specs + [pl.BlockSpec((n_step, DN_HEADS, DN_HD, DN_HD), lambda i: (rev(i), 0, 0, 0)),
                          pl.BlockSpec((ts, MIX), lambda i: (rev(i), 0)), pl.BlockSpec(memory_space=pl.ANY)],
        out_specs=[specs[1], specs[0], specs[2], vec, vec, vec],
        out_shape=[jax.ShapeDtypeStruct(dproj.shape, bf16), jax.ShapeDtypeStruct((s, 3 * MIX), f32),
                   jax.ShapeDtypeStruct((s, LANES), bf16)] + [jax.ShapeDtypeStruct((1, LANES), f32)] * 3,
        scratch_shapes=[pltpu.VMEM((DN_HEADS, DN_HD, DN_HD), f32)],
        input_output_aliases={8: 0}, compiler_params=_params(1),
    )(xc, proj, p_ba, alog_row, dtb_row, norm_row, saved, d_out, dproj)


def _merge_in_specs(ts, d):
    row = pl.BlockSpec((ts, MIX), lambda i: (i, 0))
    return [row, row, row, pl.BlockSpec((ts, 3 * d), lambda i: (i, SEC_G // (3 * d))),
            pl.BlockSpec((3, MIX, d), lambda i: (0, 0, 0))]


def merge_fwd(out_a, out_b, out_c, proj, w_branch, *, name):
    s, d = out_a.shape[0], w_branch.shape[2]
    ts = _tile(s, 256, 16)

    def body(a_ref, b_ref, c_ref, g_ref, w_ref, o_ref):
        acc = jnp.zeros((ts, d), f32)
        for n, r in enumerate((a_ref, b_ref, c_ref)):
            acc = acc + _sigmoid(g_ref[:, n * d:(n + 1) * d]) * _dg(r[...], w_ref[n], 1, 0)
        o_ref[...] = acc.astype(bf16)

    return pl.pallas_call(
        body, name=name, grid=(s // ts,), in_specs=_merge_in_specs(ts, d),
        out_specs=pl.BlockSpec((ts, d), lambda i: (i, 0)), out_shape=jax.ShapeDtypeStruct((s, d), bf16),
        compiler_params=_params(1),
    )(out_a, out_b, out_c, proj, w_branch)


def merge_bwd(out_a, out_b, out_c, proj, w_branch, d_merged, dproj, *, name):
    s, d = out_a.shape[0], w_branch.shape[2]
    ts = _tile(s, 256, 16)

    def body(a_ref, b_ref, c_ref, g_ref, w_ref, dm_ref, _, dg_ref, da_ref, db_ref, dc_ref, dw_ref):
        @pl.when(pl.program_id(0) == 0)
        def _():
            dw_ref[...] = jnp.zeros_like(dw_ref)

        dm = dm_ref[...]
        for n, (r, dr) in enumerate(((a_ref, da_ref), (b_ref, db_ref), (c_ref, dc_ref))):
            gate = _sigmoid(g_ref[:, n * d:(n + 1) * d])
            branch = _dg(r[...], w_ref[n], 1, 0)
            dg_ref[:, n * d:(n + 1) * d] = (dm * branch * gate * (1.0 - gate)).astype(bf16)
            d_branch = dm * gate
            dr[...] = _dg(d_branch, w_ref[n], 1, 1)
            dw_ref[n] += _dg(r[...], d_branch, 0, 0)

    specs = _merge_in_specs(ts, d)
    row_f = pl.BlockSpec((ts, MIX), lambda i: (i, 0))
    return pl.pallas_call(
        body, name=name, grid=(s // ts,),
        in_specs=specs + [pl.BlockSpec((ts, d), lambda i: (i, 0)), pl.BlockSpec(memory_space=pl.ANY)],
        out_specs=[specs[3], row_f, row_f, row_f, specs[4]],
        out_shape=[jax.ShapeDtypeStruct(dproj.shape, bf16)] + [jax.ShapeDtypeStruct((s, MIX), f32)] * 3
        + [jax.ShapeDtypeStruct(w_branch.shape, f32)],
        input_output_aliases={6: 0}, compiler_params=_params(1),
    )(out_a, out_b, out_c, proj, w_branch, d_merged, dproj)


def swiglu_fwd(gu, *, name):
    g2, s, w = gu.shape
    ng = g2 // 2
    ts = _tile(s, 1024, 16)

    def body(g_ref, u_ref, o_ref):
        g = g_ref[...]
        o_ref[...] = (g * _sigmoid(g) * u_ref[...]).astype(bf16)

    return pl.pallas_call(
        body, name=name, grid=(s // ts, ng),
        in_specs=[pl.BlockSpec((None, ts, w), lambda i, j: (j, i, 0)),
                  pl.BlockSpec((None, ts, w), lambda i, j: (ng + j, i, 0))],
        out_specs=pl.BlockSpec((None, ts, w), lambda i, j: (j, i, 0)), out_shape=jax.ShapeDtypeStruct((ng, s, w), bf16),
        compiler_params=_params(2),
    )(gu, gu)


def swiglu_bwd(gu, d_act, *, name):
    g2, s, w = gu.shape
    ng = g2 // 2
    ts = _tile(s, 1024, 16)

    def body(g_ref, u_ref, d_ref, o_ref):
        g, d = g_ref[...], d_ref[...]
        sig = _sigmoid(g)

        @pl.when(pl.program_id(1) < ng)
        def _():
            o_ref[...] = (d * u_ref[...] * sig * (1.0 + g * (1.0 - sig))).astype(bf16)

        @pl.when(pl.program_id(1) >= ng)
        def _():
            o_ref[...] = (d * g * sig).astype(bf16)

    return pl.pallas_call(
        body, name=name, grid=(s // ts, g2),
        in_specs=[pl.BlockSpec((None, ts, w), lambda i, j: (j % ng, i, 0)),
                  pl.BlockSpec((None, ts, w), lambda i, j: (ng + j % ng, i, 0)),
                  pl.BlockSpec((None, ts, w), lambda i, j: (j % ng, i, 0))],
        out_specs=pl.BlockSpec((None, ts, w), lambda i, j: (j, i, 0)), out_shape=jax.ShapeDtypeStruct((g2, s, w), bf16),
        compiler_params=_params(2),
    )(gu, gu, d_act)


def adamw(w, m, v, g_parts, *, name):
    r, cols = w.shape
    n_parts = g_parts.shape[0]
    lanes = -(-cols // LANES) * LANES
    tr = _tile(r, max(16, (128 * 1024) // lanes), 16)

    def body(w_ref, m_ref, v_ref, gp_ref, g_ref, d_ref, nm_ref, nv_ref):
        g = gp_ref[0].astype(f32)
        for k in range(1, n_parts):
            g = g + gp_ref[k].astype(f32)
        nm = ADAM_B1 * m_ref[...] + (1.0 - ADAM_B1) * g
        nv = ADAM_B2 * v_ref[...] + (1.0 - ADAM_B2) * jnp.square(g)
        m_hat = nm / (1.0 - ADAM_B1 ** ADAM_STEP)
        v_hat = nv / (1.0 - ADAM_B2 ** ADAM_STEP)
        g_ref[...] = g
        d_ref[...] = -ADAM_LR * (m_hat / (jnp.sqrt(v_hat) + ADAM_EPS) + ADAM_WD * w_ref[...])
        nm_ref[...] = nm
        nv_ref[...] = nv

    row = pl.BlockSpec((tr, cols), lambda i: (i, 0))
    return pl.pallas_call(
        body, name=name, grid=(r // tr,),
        in_specs=[row, row, row, pl.BlockSpec((n_parts, tr, cols), lambda i: (0, i, 0))],
        out_specs=[row] * 4, out_shape=[jax.ShapeDtypeStruct((r, cols), f32)] * 4, compiler_params=_params(1),
    )(w, m, v, g_parts)


def _mesh_pos():
    return lax.axis_index("x"), lax.axis_index("y"), lax.axis_index("c")


def _dev_index(p):
    return 4 * p[0] + 2 * p[1] + p[2]


def all_gather(blocks, *, name):
    n = len(blocks)
    any_spec = pl.BlockSpec(memory_space=pl.ANY)

    def body(*refs):
        ins, outs = refs[:n], refs[n:2 * n]
        send_sems, recv_sems, local_sems = refs[2 * n:]
        x, y, c = _mesh_pos()
        me, sibling = (x, y, c), (x, y, 1 - c)
        chips = [(1 - x, y), (x, 1 - y), (1 - x, 1 - y)]

        def copy(a, k, block, to, src=None):
            dst = outs[a].at[_dev_index(block)]
            return pltpu.make_async_remote_copy(
                src_ref=dst if src is None else src, dst_ref=dst, send_sem=send_sems.at[a, k],
                recv_sem=recv_sems.at[a, k], device_id=to, device_id_type=pl.DeviceIdType.MESH)

        mine = [pltpu.make_async_copy(ins[a], outs[a].at[_dev_index(me)], local_sems.at[a]) for a in range(n)]
        for cp in mine:
            cp.start()
        first = []
        for a in range(n):
            first.append(copy(a, 0, me, sibling, src=ins[a]))
            first += [copy(a, 1 + j, me, (*chip, c), src=ins[a]) for j, chip in enumerate(chips)]
        for cp in first:
            cp.start()
        passed = []
        for j, chip in enumerate(chips):
            for a in range(n):
                copy(a, 1 + j, (*chip, c), me).wait_recv()
                fwd = copy(a, 4 + j, (*chip, c), sibling)
                fwd.start()
                passed.append(fwd)
        for a in range(n):
            copy(a, 0, sibling, me).wait_recv()
            for j, chip in enumerate(chips):
                copy(a, 4 + j, (*chip, 1 - c), me).wait_recv()
        for cp in first + passed:
            cp.wait_send()
        for cp in mine:
            cp.wait()

    return pl.pallas_call(
        body, name=name, in_specs=[any_spec] * n, out_specs=[any_spec] * n,
        out_shape=[jax.ShapeDtypeStruct((N_DEV,) + b.shape, b.dtype) for b in blocks],
        scratch_shapes=[pltpu.SemaphoreType.DMA((n, 7)), pltpu.SemaphoreType.DMA((n, 7)),
                        pltpu.SemaphoreType.DMA((n,))],
    )(*blocks)


def exchange_grads(parts, small, *, name):
    n = len(parts)
    any_spec = pl.BlockSpec(memory_space=pl.ANY)

    def body(*refs):
        part_refs, small_ref = refs[:n], refs[n]
        recv_refs, small_all_ref = refs[n + 1:2 * n + 1], refs[2 * n + 1]
        send_sems, recv_sems, local_sems = refs[2 * n + 2:]
        x, y, c = _mesh_pos()
        mine = _dev_index((x, y, c))
        local = [pltpu.make_async_copy(part_refs[a].at[mine], recv_refs[a].at[mine], local_sems.at[a]) for a in range(n)]
        local.append(pltpu.make_async_copy(small_ref, small_all_ref.at[mine], local_sems.at[n]))
        for cp in local:
            cp.start()
        sends, arrivals = [], []
        for k in range(1, N_DEV):
            peer = (1 - x if k & 4 else x, 1 - y if k & 2 else y, 1 - c if k & 1 else c)
            theirs = _dev_index(peer)
            flows = [(part_refs[a].at[theirs], recv_refs[a].at[mine], recv_refs[a].at[theirs]) for a in range(n)]
            flows.append((small_ref, small_all_ref.at[mine], small_all_ref.at[theirs]))
            for a, (src, dst_mine, dst_theirs) in enumerate(flows):
                sems = dict(send_sem=send_sems.at[a, k - 1], recv_sem=recv_sems.at[a, k - 1], device_id=peer,
                            device_id_type=pl.DeviceIdType.MESH)
                sends.append(pltpu.make_async_remote_copy(src_ref=src, dst_ref=dst_mine, **sems))
                arrivals.append(pltpu.make_async_remote_copy(src_ref=src, dst_ref=dst_theirs, **sems))
        for cp in sends:
            cp.start()
        for cp in arrivals:
            cp.wait_recv()
        for cp in sends:
            cp.wait_send()
        for cp in local:
            cp.wait()

    outs = pl.pallas_call(
        body, name=name, in_specs=[any_spec] * (n + 1), out_specs=[any_spec] * (n + 1),
        out_shape=[jax.ShapeDtypeStruct(p.shape, p.dtype) for p in parts]
        + [jax.ShapeDtypeStruct((N_DEV,) + small.shape, small.dtype)],
        scratch_shapes=[pltpu.SemaphoreType.DMA((n + 1, 7)), pltpu.SemaphoreType.DMA((n + 1, 7)),
                        pltpu.SemaphoreType.DMA((n + 1,))],
    )(*parts, small)
    return outs[:n], outs[n]


def _rows128(arr):
    flat = arr.reshape(-1)
    rows = -(-flat.shape[0] // (8 * LANES)) * 8
    return jnp.pad(flat, (0, rows * LANES - flat.shape[0])).reshape(rows, LANES)


def _pad_lanes(row, width=LANES, at=0):
    return jnp.pad(row, (at, width - at - row.shape[0])).reshape(1, width)


def _w_in_sections(got):
    d = got.shape[1]
    wi = jnp.transpose(got, (1, 0, 2)).reshape(d, -1)
    w_main = jnp.concatenate([wi[:, :C_B], wi[:, C_Z:C_BA], wi[:, C_QKV:C_Z], wi[:, C_G:], wi[:, C_B:C_QKV]], axis=1)
    return w_main, jnp.pad(wi[:, C_BA:C_G], ((0, 0), (0, LANES - (C_G - C_BA))))


def _w_in_parts(gw_main, gw_ba):
    d = gw_main.shape[0]
    full = jnp.concatenate([gw_main[:, SEC_A:SEC_Z], gw_main[:, SEC_B:], gw_main[:, SEC_QKV:SEC_G],
                            gw_main[:, SEC_Z:SEC_QKV], gw_ba[:, :C_G - C_BA], gw_main[:, SEC_G:SEC_B]], axis=1)
    return jnp.transpose(full.reshape(d, N_DEV, -1), (1, 0, 2))


def kernel(x, positions, attn_norm, w_in, sgu_ln_g, sgu_ln_b, sgu_w, sgu_b, attn_sinks, dn_conv_w, dn_a_log, dn_dt_bias, dn_norm, w_branch, w_out, ffn_norm, w_gate_up, w_down, final_norm, loss_target, m_attn_norm, m_w_in, m_sgu_ln_g, m_sgu_ln_b, m_sgu_w, m_sgu_b, m_attn_sinks, m_dn_conv_w, m_dn_a_log, m_dn_dt_bias, m_dn_norm, m_w_branch, m_w_out, m_ffn_norm, m_w_gate_up, m_w_down, m_final_norm, v_attn_norm, v_w_in, v_sgu_ln_g, v_sgu_ln_b, v_sgu_w, v_sgu_b, v_attn_sinks, v_dn_conv_w, v_dn_a_log, v_dn_dt_bias, v_dn_norm, v_w_branch, v_w_out, v_ffn_norm, v_w_gate_up, v_w_down, v_final_norm):
    given = dict(locals())
    depth, d_model = attn_norm.shape
    s = x.shape[1]
    x2 = x.reshape(s, d_model)
    target = loss_target.reshape(s, d_model)
    posf = positions.reshape(s, 1).astype(f32)
    inv_freq = ROPE_THETA ** (-jnp.arange(0, ROPE_DIM, 2, dtype=f32) / ROPE_DIM)
    inv_head = jnp.concatenate([inv_freq, inv_freq, jnp.zeros((SWA_HD - ROPE_DIM,), f32)])
    inv_q = jnp.tile(inv_head, SWA_HEADS).reshape(1, MIX)
    inv_k = jnp.tile(inv_head, SWA_KV).reshape(1, LANES)

    big_names = ["w_in", "w_branch", "w_out", "w_gate_up", "w_down"]
    blocks = [given[n][l].astype(bf16) for l in range(depth) for n in big_names] + [dn_conv_w[l] for l in range(depth)]
    got = all_gather(blocks, name="gather_weights")
    layers = []
    for l in range(depth):
        g = dict(zip(big_names, got[l * len(big_names):(l + 1) * len(big_names)]))
        w_main, w_ba = _w_in_sections(g["w_in"])
        conv_full = jnp.transpose(got[depth * len(big_names) + l], (1, 0, 2)).reshape(DN_CONV, -1)
        layers.append(dict(
            w_main=w_main, w_ba=w_ba,
            w_branch=jnp.transpose(g["w_branch"], (1, 2, 0, 3)).reshape(3, MIX, d_model),
            w_out=g["w_out"].reshape(d_model, d_model),
            w_gu=g["w_gate_up"],
            w_down=g["w_down"].reshape(N_DEV // 2, -1, d_model),
            conv_w8=jnp.pad(conv_full, ((0, CONV_PAD - DN_CONV), (0, 0))),
            attn_norm=attn_norm[l].reshape(1, -1), ffn_norm=ffn_norm[l].reshape(1, -1),
            ln_g=sgu_ln_g[l].reshape(1, -1), ln_b=sgu_ln_b[l].reshape(1, -1), sgu_w=sgu_w[l], sgu_bt=sgu_b[l].T,
            sink_row=_pad_lanes(attn_sinks[l]), alog_row=_pad_lanes(dn_a_log[l], at=DN_HEADS),
            dtb_row=_pad_lanes(dn_dt_bias[l], at=DN_HEADS), norm_row=dn_norm[l].reshape(1, -1)))

    saved = []
    h_in = x2
    for l, p in enumerate(layers):
        t = f"l{l}_"
        h = rmsnorm_fwd(h_in, p["attn_norm"], name=t + "attn_norm")
        proj = matmul(h, p["w_main"], "nn", f32, name=t + "in_proj")
        p_ba = matmul(h, p["w_ba"], "nn", f32, name=t + "in_proj_ba")
        out_a = sgu_fwd(proj, p["ln_g"], p["ln_b"], p["sgu_w"], p["sgu_bt"], name=t + "sgu")
        out_b = swa_fwd(proj, p["sink_row"], posf, inv_q, inv_k, name=t + "swa")
        xc = conv_fwd(proj, p["conv_w8"], name=t + "dn_conv")
        out_c, states = dn_fwd(xc, proj, p_ba, p["alog_row"], p["dtb_row"], p["norm_row"], name=t + "deltanet")
        merged = merge_fwd(out_a, out_b, out_c, proj, p["w_branch"], name=t + "merge")
        x_mid = matmul(merged, p["w_out"], "nn", f32, residual=h_in, name=t + "out_proj")
        h2 = rmsnorm_fwd(x_mid, p["ffn_norm"], name=t + "ffn_norm")
        gu = matmul(h2, p["w_gu"], "nn", f32, group="n", name=t + "gate_up")
        act = swiglu_fwd(gu, name=t + "swiglu")
        x_out = matmul(act, p["w_down"], "nn", f32, residual=x_mid, group="k", name=t + "down")
        saved.append(dict(x_in=h_in, h=h, proj=proj, p_ba=p_ba, out_a=out_a, out_b=out_b, out_c=out_c, xc=xc,
                          states=states, merged=merged, x_mid=x_mid, h2=h2, gu=gu, act=act))
        h_in = x_out

    dx, d_final_norm, loss_row = loss_head(h_in, final_norm.reshape(1, -1), target, name="loss_head")
    loss = lax.psum(loss_row[0, 0], MESH_AXES)

    shard_names = ["w_in", "dn_conv_w", "w_branch", "w_out", "w_gate_up", "w_down"]
    rep_names = ["attn_norm", "sgu_ln_g", "sgu_ln_b", "sgu_w", "sgu_b", "attn_sinks", "dn_a_log", "dn_dt_bias",
                 "dn_norm", "ffn_norm"]
    per_layer = []
    for l in reversed(range(depth)):
        p, sv, t = layers[l], saved[l], f"l{l}_b_"
        d_act = matmul(dx, p["w_down"], "nt", f32, group="n", name=t + "down_dx")
        gw_down = matmul(sv["act"], dx, "tn", bf16, group="m", name=t + "down_dw")
        d_gu = swiglu_bwd(sv["gu"], d_act, name=t + "swiglu")
        gw_gu = matmul(sv["h2"], d_gu, "tn", bf16, group="n", name=t + "gate_up_dw")
        d_h2 = matmul(d_gu, p["w_gu"], "nt", f32, group="k", name=t + "gate_up_dx")
        dx_mid, g_ffn = rmsnorm_bwd(sv["x_mid"], p["ffn_norm"], d_h2, dx, name=t + "ffn_norm")
        d_merged = matmul(dx_mid, p["w_out"], "nt", f32, name=t + "out_proj_dx")
        gw_out = matmul(sv["merged"], dx_mid, "tn", bf16, name=t + "out_proj_dw")
        dproj = lax.empty((s, W_MAIN), bf16)
        dproj, d_a, d_b, d_c, gw_branch = merge_bwd(sv["out_a"], sv["out_b"], sv["out_c"], sv["proj"], p["w_branch"],
                                                   d_merged, dproj, name=t + "merge")
        dproj, g_ln_g, g_ln_b, g_sgu_w, g_sgu_bt = sgu_bwd(sv["proj"], p["ln_g"], p["ln_b"], p["sgu_w"], p["sgu_bt"],
                                                         d_a, dproj, name=t + "sgu")
        dproj, g_sink = swa_bwd(sv["proj"], p["sink_row"], posf, inv_q, inv_k, d_b, dproj, name=t + "swa")
        dproj, dxc, dba, g_alog, g_dtb, g_dnorm = dn_bwd(sv["xc"], sv["proj"], sv["p_ba"], p["alog_row"], p["dtb_row"],
                                                        p["norm_row"], sv["states"], d_c, dproj, name=t + "deltanet")
        dproj, g_conv8 = conv_bwd(sv["proj"], p["conv_w8"], dxc, dproj, name=t + "dn_conv")
        gw_main = matmul(sv["h"], dproj, "tn", bf16, name=t + "in_proj_dw")
        gw_ba = matmul(sv["h"], dba, "tn", bf16, name=t + "in_proj_ba_dw")
        d_h = matmul(dproj, p["w_main"], "nt", f32, tk=768, name=t + "in_proj_dx")
        d_h = matmul(dba, p["w_ba"], "nt", f32, residual=d_h, name=t + "in_proj_ba_dx")
        dx, g_attn = rmsnorm_bwd(sv["x_in"], p["attn_norm"], d_h, dx_mid, name=t + "attn_norm")
        per_layer.append(dict(
            w_in=_w_in_parts(gw_main, gw_ba),
            dn_conv_w=jnp.transpose(g_conv8[:DN_CONV].reshape(DN_CONV, N_DEV, -1), (1, 0, 2)),
            w_branch=jnp.transpose(gw_branch.reshape(3, MIX, N_DEV, -1), (2, 0, 1, 3)).astype(bf16),
            w_out=gw_out.reshape(N_DEV, -1, d_model), w_gate_up=gw_gu, w_down=gw_down.reshape(N_DEV, -1, d_model),
            attn_norm=g_attn, sgu_ln_g=g_ln_g, sgu_ln_b=g_ln_b, sgu_w=g_sgu_w, sgu_b=g_sgu_bt.T,
            attn_sinks=g_sink[0, :SWA_HEADS], dn_a_log=g_alog[0, DN_HEADS:2 * DN_HEADS],
            dn_dt_bias=g_dtb[0, DN_HEADS:2 * DN_HEADS], dn_norm=g_dnorm, ffn_norm=g_ffn))
    per_layer.reverse()

    parts = [jnp.stack([pp[n] for pp in per_layer], axis=1) for n in shard_names]
    rep_grads = {n: jnp.stack([pp[n].reshape(given[n].shape[1:]) for pp in per_layer]) for n in rep_names}
    rep_grads["final_norm"] = d_final_norm[0]
    rep_names = rep_names + ["final_norm"]
    rep_rows = [_rows128(given[n]).shape[0] for n in rep_names]
    pad_rows = -sum(rep_rows) % 16

    def small_rows(values):
        pieces = [_rows128(values[n]) for n in rep_names]
        return jnp.concatenate(pieces + [jnp.zeros((pad_rows, LANES), f32)], axis=0)

    recv, small_all = exchange_grads(parts, small_rows(rep_grads), name="exchange_grads")

    results = [{}, {}, {}, {}]
    for n, got_n in zip(shard_names, recv):
        shp = given[n].shape
        two = (-1, shp[-1])
        outs = adamw(given[n].reshape(two), given["m_" + n].reshape(two), given["v_" + n].reshape(two),
                     got_n.reshape((N_DEV,) + given[n].reshape(two).shape), name="adamw_" + n)
        for res, val in zip(results, outs):
            res[n] = val.reshape(shp)
    outs = adamw(small_rows(given), small_rows({n: given["m_" + n] for n in rep_names}),
                 small_rows({n: given["v_" + n] for n in rep_names}), small_all, name="adamw_replicated")
    for res, val in zip(results, outs):
        row = 0
        for n, nr in zip(rep_names, rep_rows):
            res[n] = val[row:row + nr].reshape(-1)[:given[n].size].reshape(given[n].shape)
            row += nr
    order = ["attn_norm", "w_in", "sgu_ln_g", "sgu_ln_b", "sgu_w", "sgu_b", "attn_sinks", "dn_conv_w", "dn_a_log",
             "dn_dt_bias", "dn_norm", "w_branch", "w_out", "ffn_norm", "w_gate_up", "w_down", "final_norm"]
    return (loss, dx.reshape(x.shape), *[res[n] for res in results for n in order])
```

```python
import functools

import jax
import jax.numpy as jnp
from jax import lax
from jax.experimental import pallas as pl
from jax.experimental.pallas import tpu as pltpu

f32 = jnp.float32
bf16 = jnp.bfloat16

N_DEV = 8
MESH_AXES = ("x", "y", "c")
NORM_EPS = 1e-6
MIX = 512
SGU_GROUPS, SGU_CHUNK = 4, 128
SWA_HEADS, SWA_KV, SWA_HD, WINDOW = 8, 2, 64, 128
ROPE_THETA, ROPE_DIM = 500000.0, 16
DN_HEADS, DN_HD, DN_CONV, DN_CHUNK = 4, 128, 4, 64
ADAM_LR, ADAM_B1, ADAM_B2, ADAM_EPS, ADAM_WD, ADAM_STEP = 0.001, 0.9, 0.999, 1e-08, 0.01, 10

LANES = 128
VMEM_LIMIT = 56 * 1024 * 1024

SEC_A, SEC_Z, SEC_QKV, SEC_G, SEC_B = 0, 1024, 1536, 3072, 6144
W_MAIN = 6912
C_B, C_QKV, C_Z, C_BA, C_G = 1024, 1792, 3328, 3840, 3848


def _params(n_axes, **kw):
    return pltpu.CompilerParams(dimension_semantics=("arbitrary",) * n_axes, vmem_limit_bytes=VMEM_LIMIT, **kw)


def _tile(n, target, mult=LANES):
    if n <= target:
        return n
    best = None
    for t in range(mult, target + 1, mult):
        if n % t == 0:
            best = t
    assert best is not None, (n, target, mult)
    return best


def _dg(a, b, ca, cb):
    return lax.dot_general(a.astype(bf16), b.astype(bf16), (((ca,), (cb,)), ((), ())), preferred_element_type=f32)


def _dg3(a, b, ca, cb):
    a_hi, b_hi = a.astype(bf16), b.astype(bf16)
    a_lo, b_lo = (a - a_hi.astype(f32)).astype(bf16), (b - b_hi.astype(f32)).astype(bf16)

    def dot(p, q):
        return lax.dot_general(p, q, (((ca,), (cb,)), ((), ())), preferred_element_type=f32)

    return dot(a_hi, b_hi) + (dot(a_hi, b_lo) + dot(a_lo, b_hi))


def _differentiable_dot(core):
    @functools.partial(jax.custom_vjp, nondiff_argnums=(2, 3))
    def dot(a, b, ca, cb):
        return core(a, b, ca, cb)

    def fwd(a, b, ca, cb):
        return core(a, b, ca, cb), (a, b)

    def bwd(ca, cb, res, ct):
        a, b = res
        da = core(ct, b, 1, 1 - cb) if ca == 1 else core(b, ct, 1 - cb, 1)
        db = core(a, ct, 1 - ca, 0) if cb == 0 else core(ct, a, 0, 1 - ca)
        return da, db

    dot.defvjp(fwd, bwd)
    return dot


bdot = _differentiable_dot(_dg)
_hdot = _differentiable_dot(_dg3)


def hdot(a, b, ca=1, cb=0):
    return _hdot(a, b, ca, cb)


@functools.partial(jax.custom_vjp, nondiff_argnums=(1,))
def lroll(x, shift):
    return pltpu.roll(x, shift, 1)


def _lroll_fwd(x, shift):
    return pltpu.roll(x, shift, 1), None


def _lroll_bwd(shift, _, ct):
    return (pltpu.roll(ct, ct.shape[1] - shift, 1),)


lroll.defvjp(_lroll_fwd, _lroll_bwd)


@functools.partial(jax.custom_vjp, nondiff_argnums=(1,))
def tri_inv(low, nil):
    n = low.shape[0]
    row = lax.broadcasted_iota(jnp.int32, (n, n), 0)
    col = lax.broadcasted_iota(jnp.int32, (n, n), 1)
    eye = (row == col).astype(f32)
    m = -low
    p = eye + m
    span = 2
    while span < nil:
        m = hdot(m, m)
        p = p + hdot(p, m)
        span *= 2
    return p


def _tri_inv_fwd(low, nil):
    t = tri_inv(low, nil)
    return t, t


def _tri_inv_bwd(nil, t, dt):
    return (-hdot(t, hdot(dt, t, 1, 1), 0, 0),)


tri_inv.defvjp(_tri_inv_fwd, _tri_inv_bwd)


def _sigmoid(x):
    return 1.0 / (1.0 + jnp.exp(-x))


def _rms(x, g):
    return x * lax.rsqrt(jnp.mean(x * x, axis=-1, keepdims=True) + NORM_EPS) * g


def _lane_col(x, lane_idx):
    lane = lax.broadcasted_iota(jnp.int32, x.shape, 1)
    return jnp.sum(jnp.where(lane == lane_idx, x, 0.0), axis=1, keepdims=True)


def matmul(a, b, mode, out_dtype, *, residual=None, group=None, tm=1024, tn=768, tk=1024, name):
    dims = {"a": ("m", "k") if mode != "tn" else ("k", "m"),
            "b": {"nn": ("k", "n"), "nt": ("n", "k"), "tn": ("k", "n")}[mode], "o": ("m", "n")}
    full, groups = {}, 1
    for arr, key in ((a, "a"), (b, "b")):
        grouped = group in dims[key]
        if grouped:
            groups = arr.shape[0]
        full[dims[key][0]], full[dims[key][1]] = arr.shape[1:] if grouped else arr.shape
    want = {"m": tm, "n": tn, "k": tk}
    tiles = {d: full[d] if d == group else _tile(full[d], want[d]) for d in "mnk"}
    steps = {d: groups if d == group else full[d] // tiles[d] for d in "mnk"}

    def spec(key):
        d0, d1 = dims[key]

        def index(i, j, kk):
            at = {"m": i, "n": j, "k": kk}
            if group in (d0, d1):
                return (at[group], 0 if d0 == group else at[d0], 0 if d1 == group else at[d1])
            return (at[d0], at[d1])

        block = (tiles[d0], tiles[d1])
        return pl.BlockSpec(((None,) + block) if group in (d0, d1) else block, index)

    ca, cb = {"nn": (1, 0), "nt": (1, 1), "tn": (0, 0)}[mode]
    nk = steps["k"]
    o_spec = spec("o")
    out_shape = (groups, full["m"], full["n"]) if group in ("m", "n") else (full["m"], full["n"])
    has_res = residual is not None

    def body(*refs):
        if has_res:
            a_ref, b_ref, r_ref, o_ref, acc_ref = refs
        else:
            a_ref, b_ref, o_ref, acc_ref = refs
        kk = pl.program_id(2)

        @pl.when(kk == 0)
        def _():
            acc_ref[...] = jnp.zeros_like(acc_ref)

        acc_ref[...] += _dg(a_ref[...], b_ref[...], ca, cb)

        @pl.when(kk == nk - 1)
        def _():
            acc = acc_ref[...]
            if has_res:
                acc = acc + r_ref[...]
            o_ref[...] = acc.astype(out_dtype)

    in_specs = [spec("a"), spec("b")] + ([o_spec] if has_res else [])
    args = (a, b) + ((residual,) if has_res else ())
    return pl.pallas_call(
        body, name=name, grid=(steps["m"], steps["n"], nk), in_specs=in_specs, out_specs=o_spec,
        out_shape=jax.ShapeDtypeStruct(out_shape, out_dtype),
        scratch_shapes=[pltpu.VMEM((tiles["m"], tiles["n"]), f32)], compiler_params=_params(3),
    )(*args)


def rmsnorm_fwd(x, g_row, *, name):
    s, d = x.shape
    ts = _tile(s, 512, 16)

    def body(x_ref, g_ref, o_ref):
        o_ref[...] = _rms(x_ref[...], g_ref[...]).astype(bf16)

    return pl.pallas_call(
        body, name=name, grid=(s // ts,),
        in_specs=[pl.BlockSpec((ts, d), lambda i: (i, 0)), pl.BlockSpec((1, d), lambda i: (0, 0))],
        out_specs=pl.BlockSpec((ts, d), lambda i: (i, 0)), out_shape=jax.ShapeDtypeStruct((s, d), bf16),
        compiler_params=_params(1),
    )(x, g_row)


def rmsnorm_bwd(x, g_row, dh, dres, *, name):
    s, d = x.shape
    ts = _tile(s, 512, 16)

    def body(x_ref, g_ref, dh_ref, dres_ref, dx_ref, dg_ref):
        @pl.when(pl.program_id(0) == 0)
        def _():
            dg_ref[...] = jnp.zeros_like(dg_ref)

        _, vjp = jax.vjp(_rms, x_ref[...], g_ref[...])
        dx, dg = vjp(dh_ref[...])
        dx_ref[...] = dx + dres_ref[...]
        dg_ref[...] += dg

    row = pl.BlockSpec((ts, d), lambda i: (i, 0))
    vec = pl.BlockSpec((1, d), lambda i: (0, 0))
    return pl.pallas_call(
        body, name=name, grid=(s // ts,), in_specs=[row, vec, row, row], out_specs=[row, vec],
        out_shape=[jax.ShapeDtypeStruct((s, d), f32), jax.ShapeDtypeStruct((1, d), f32)],
        compiler_params=_params(1),
    )(x, g_row, dh, dres)


def loss_head(x, g_row, target, *, name):
    s, d = x.shape
    ts = _tile(s, 512, 16)

    def body(x_ref, g_ref, t_ref, dx_ref, dg_ref, loss_ref):
        @pl.when(pl.program_id(0) == 0)
        def _():
            dg_ref[...] = jnp.zeros_like(dg_ref)
            loss_ref[...] = jnp.zeros_like(loss_ref)

        y, vjp = jax.vjp(_rms, x_ref[...], g_ref[...])
        err = y - t_ref[...]
        dx, dg = vjp(err * (1.0 / d))
        dx_ref[...] = dx
        dg_ref[...] += dg
        loss_ref[...] += 0.5 * jnp.sum(jnp.sum(err * err, axis=1, keepdims=True) * (1.0 / d), axis=0, keepdims=True)

    row = pl.BlockSpec((ts, d), lambda i: (i, 0))
    vec = pl.BlockSpec((1, d), lambda i: (0, 0))
    one = pl.BlockSpec((1, LANES), lambda i: (0, 0))
    return pl.pallas_call(
        body, name=name, grid=(s // ts,), in_specs=[row, vec, row], out_specs=[row, vec, one],
        out_shape=[jax.ShapeDtypeStruct((s, d), f32), jax.ShapeDtypeStruct((1, d), f32),
                   jax.ShapeDtypeStruct((1, LANES), f32)],
        compiler_params=_params(1),
    )(x, g_row, target)


def _sgu_chunk(p_a, ln_g, ln_b, w, b_t):
    t = SGU_CHUNK
    u = jax.nn.gelu(p_a[:, :MIX])
    v = jax.nn.gelu(p_a[:, MIX:])
    vc = v - jnp.mean(v, axis=-1, keepdims=True)
    vn = vc * lax.rsqrt(jnp.mean(vc * vc, axis=-1, keepdims=True) + NORM_EPS) * ln_g + ln_b
    causal = lax.broadcasted_iota(jnp.int32, (t, t), 0) >= lax.broadcasted_iota(jnp.int32, (t, t), 1)
    outs = []
    for g in range(SGU_GROUPS):
        sl = slice(g * LANES, (g + 1) * LANES)
        mixed = bdot(jnp.where(causal, w[g], 0.0), vn[:, sl], 1, 0) + b_t[:, g:g + 1]
        outs.append(u[:, sl] * mixed)
    return jnp.concatenate(outs, axis=1)


def _sgu_specs(s, ts):
    return [pl.BlockSpec((ts, 2 * MIX), lambda i: (i, SEC_A // (2 * MIX))),
            pl.BlockSpec((1, MIX), lambda i: (0, 0)), pl.BlockSpec((1, MIX), lambda i: (0, 0)),
            pl.BlockSpec((SGU_GROUPS, SGU_CHUNK, SGU_CHUNK), lambda i: (0, 0, 0)),
            pl.BlockSpec((SGU_CHUNK, SGU_GROUPS), lambda i: (0, 0))]


def sgu_fwd(proj, ln_g, ln_b, w, b_t, *, name):
    s = proj.shape[0]
    ts = _tile(s, 512)
    n_chunk = ts // SGU_CHUNK

    def body(p_ref, g_ref, b_ref, w_ref, bt_ref, o_ref):
        def step(c, carry):
            rows = pl.ds(pl.multiple_of(c * SGU_CHUNK, SGU_CHUNK), SGU_CHUNK)
            o_ref[rows, :] = _sgu_chunk(p_ref[rows, :], g_ref[...], b_ref[...], w_ref[...], bt_ref[...]).astype(bf16)
            return carry
        lax.fori_loop(0, n_chunk, step, 0)

    return pl.pallas_call(
        body, name=name, grid=(s // ts,), in_specs=_sgu_specs(s, ts),
        out_specs=pl.BlockSpec((ts, MIX), lambda i: (i, 0)), out_shape=jax.ShapeDtypeStruct((s, MIX), bf16),
        compiler_params=_params(1),
    )(proj, ln_g, ln_b, w, b_t)


def sgu_bwd(proj, ln_g, ln_b, w, b_t, d_out, dproj, *, name):
    s = proj.shape[0]
    ts = _tile(s, 512)
    n_chunk = ts // SGU_CHUNK

    def body(p_ref, g_ref, b_ref, w_ref, bt_ref, do_ref, _, dp_ref, dg_ref, db_ref, dw_ref, dbt_ref):
        @pl.when(pl.program_id(0) == 0)
        def _():
            dg_ref[...] = jnp.zeros_like(dg_ref)
            db_ref[...] = jnp.zeros_like(db_ref)
            dw_ref[...] = jnp.zeros_like(dw_ref)
            dbt_ref[...] = jnp.zeros_like(dbt_ref)

        def step(c, carry):
            rows = pl.ds(pl.multiple_of(c * SGU_CHUNK, SGU_CHUNK), SGU_CHUNK)
            _, vjp = jax.vjp(_sgu_chunk, p_ref[rows, :], g_ref[...], b_ref[...], w_ref[...], bt_ref[...])
            dp, dg, db, dw, dbt = vjp(do_ref[rows, :])
            dp_ref[rows, :] = dp.astype(bf16)
            dg_ref[...] += dg
            db_ref[...] += db
            dw_ref[...] += dw
            dbt_ref[...] += dbt
            return carry
        lax.fori_loop(0, n_chunk, step, 0)

    specs = _sgu_specs(s, ts)
    return pl.pallas_call(
        body, name=name, grid=(s // ts,),
        in_specs=specs + [pl.BlockSpec((ts, MIX), lambda i: (i, 0)), pl.BlockSpec(memory_space=pl.ANY)],
        out_specs=[specs[0], specs[1], specs[2], specs[3], specs[4]],
        out_shape=[jax.ShapeDtypeStruct(dproj.shape, bf16), jax.ShapeDtypeStruct((1, MIX), f32),
                   jax.ShapeDtypeStruct((1, MIX), f32), jax.ShapeDtypeStruct(w.shape, f32),
                   jax.ShapeDtypeStruct(b_t.shape, f32)],
        input_output_aliases={6: 0}, compiler_params=_params(1),
    )(proj, ln_g, ln_b, w, b_t, d_out, dproj)


def _rope(x, pos, inv_freq):
    w = x.shape[1]
    d = lax.broadcasted_iota(jnp.int32, (1, w), 1) % SWA_HD
    half = ROPE_DIM // 2
    ang = pos * inv_freq
    c, s = jnp.cos(ang), jnp.sin(ang)
    lo = jnp.where(d < half, s, 0.0)
    hi = jnp.where((d >= half) & (d < ROPE_DIM), s, 0.0)
    return x * c - lroll(x, w - half) * lo + lroll(x, half) * hi


def _swa_block(q, kp, kc, vp, vc, sink_row, pos_q, pos_p, inv_q, inv_k, prev_ok):
    t = WINDOW
    q = _rope(q, pos_q, inv_q) * (SWA_HD ** -0.5)
    keys = jnp.concatenate([_rope(kp, pos_p, inv_k), _rope(kc, pos_q, inv_k)], axis=0)
    vals = jnp.concatenate([vp, vc], axis=0)
    qi = lax.broadcasted_iota(jnp.int32, (t, 2 * t), 0)
    sj = lax.broadcasted_iota(jnp.int32, (t, 2 * t), 1)
    diff = qi + t - sj
    mask = (diff >= 0) & (diff < t) & (prev_ok | (sj >= t))
    lane_half = lax.broadcasted_iota(jnp.int32, (t, LANES), 1) // SWA_HD
    group = SWA_HEADS // SWA_KV
    slabs = []
    for pair in range(SWA_HEADS // 2):
        q_pair = q[:, pair * LANES:(pair + 1) * LANES]
        acc = jnp.zeros((t, LANES), f32)
        for half in range(2):
            h = 2 * pair + half
            kv = h // group
            qm = jnp.where(lane_half == half, q_pair, 0.0)
            if half != kv:
                qm = lroll(qm, SWA_HD)
            logits = jnp.where(mask, bdot(qm, keys, 1, 1), -1e30)
            sink = _lane_col(sink_row, h)
            m = jnp.maximum(jnp.max(logits, axis=1, keepdims=True), sink)
            p = jnp.exp(logits - m)
            probs = p / (jnp.sum(p, axis=1, keepdims=True) + jnp.exp(sink - m))
            o = jnp.where(lane_half == kv, bdot(probs, vals, 1, 0), 0.0)
            if half != kv:
                o = lroll(o, SWA_HD)
            acc = acc + o
        slabs.append(acc)
    return jnp.concatenate(slabs, axis=1)


def _swa_in_specs(nc, clamp):
    t = WINDOW
    qb, kb, vb = SEC_B // MIX, (SEC_B + MIX) // LANES, (SEC_B + MIX + LANES) // LANES

    def cur(i):
        return jnp.minimum(i, nc - 1) if clamp else i

    def prev(i):
        return jnp.maximum(cur(i) - 1, 0)

    return [pl.BlockSpec((t, MIX), lambda i: (cur(i), qb)),
            pl.BlockSpec((t, LANES), lambda i: (prev(i), kb)), pl.BlockSpec((t, LANES), lambda i: (cur(i), kb)),
            pl.BlockSpec((t, LANES), lambda i: (prev(i), vb)), pl.BlockSpec((t, LANES), lambda i: (cur(i), vb)),
            pl.BlockSpec((1, LANES), lambda i: (0, 0)),
            pl.BlockSpec((t, 1), lambda i: (cur(i), 0)), pl.BlockSpec((t, 1), lambda i: (prev(i), 0)),
            pl.BlockSpec((1, MIX), lambda i: (0, 0)), pl.BlockSpec((1, LANES), lambda i: (0, 0))]


def swa_fwd(proj, sink_row, posf, inv_q, inv_k, *, name):
    s = proj.shape[0]
    nc = s // WINDOW

    def body(q_ref, kp_ref, kc_ref, vp_ref, vc_ref, sink_ref, pq_ref, pp_ref, iq_ref, ik_ref, o_ref):
        prev_ok = pl.program_id(0) > 0
        o_ref[...] = _swa_block(q_ref[...], kp_ref[...], kc_ref[...], vp_ref[...], vc_ref[...], sink_ref[...],
                                pq_ref[...], pp_ref[...], iq_ref[...], ik_ref[...], prev_ok).astype(bf16)

    return pl.pallas_call(
        body, name=name, grid=(nc,), in_specs=_swa_in_specs(nc, False),
        out_specs=pl.BlockSpec((WINDOW, MIX), lambda i: (i, 0)), out_shape=jax.ShapeDtypeStruct((s, MIX), bf16),
        compiler_params=_params(1),
    )(proj, proj, proj, proj, proj, sink_row, posf, posf, inv_q, inv_k)


def swa_bwd(proj, sink_row, posf, inv_q, inv_k, d_out, dproj, *, name):
    s = proj.shape[0]
    nc = s // WINDOW
    t = WINDOW

    def body(q_ref, kp_ref, kc_ref, vp_ref, vc_ref, sink_ref, pq_ref, pp_ref, iq_ref, ik_ref, do_ref, _,
             dp_ref, dsink_ref, cq_ref, ck_ref, cv_ref):
        i = pl.program_id(0)

        @pl.when(i == 0)
        def _():
            dsink_ref[...] = jnp.zeros_like(dsink_ref)

        def write(dk_prev, dv_prev):
            dp_ref[:, :MIX] = cq_ref[...].astype(bf16)
            dp_ref[:, MIX:MIX + LANES] = (ck_ref[...] + dk_prev).astype(bf16)
            dp_ref[:, MIX + LANES:] = (cv_ref[...] + dv_prev).astype(bf16)

        @pl.when(i < nc)
        def _():
            fn = functools.partial(_swa_block, pos_q=pq_ref[...], pos_p=pp_ref[...], inv_q=iq_ref[...],
                                   inv_k=ik_ref[...], prev_ok=i > 0)
            _, vjp = jax.vjp(fn, q_ref[...], kp_ref[...], kc_ref[...], vp_ref[...], vc_ref[...], sink_ref[...])
            dq, dkp, dkc, dvp, dvc, dsink = vjp(do_ref[...])
            dsink_ref[...] += dsink

            @pl.when(i > 0)
            def _():
                write(dkp, dvp)

            cq_ref[...] = dq
            ck_ref[...] = dkc
            cv_ref[...] = dvc

        @pl.when(i == nc)
        def _():
            write(0.0, 0.0)

    return pl.pallas_call(
        body, name=name, grid=(nc + 1,),
        in_specs=_swa_in_specs(nc, True) + [pl.BlockSpec((t, MIX), lambda i: (jnp.minimum(i, nc - 1), 0)),
                                            pl.BlockSpec(memory_space=pl.ANY)],
        out_specs=[pl.BlockSpec((t, MIX + 2 * LANES), lambda i: (jnp.maximum(i - 1, 0), SEC_B // (MIX + 2 * LANES))),
                   pl.BlockSpec((1, LANES), lambda i: (0, 0))],
        out_shape=[jax.ShapeDtypeStruct(dproj.shape, bf16), jax.ShapeDtypeStruct((1, LANES), f32)],
        scratch_shapes=[pltpu.VMEM((t, MIX), f32), pltpu.VMEM((t, LANES), f32), pltpu.VMEM((t, LANES), f32)],
        input_output_aliases={11: 0}, compiler_params=_params(1),
    )(proj, proj, proj, proj, proj, sink_row, posf, posf, inv_q, inv_k, d_out, dproj)


CONV_PAD = 8


def _conv_pre(xp, w, rows):
    off = CONV_PAD - (DN_CONV - 1)
    pre = xp[off:off + rows] * w[0:1]
    for i in range(1, DN_CONV):
        pre = pre + xp[off + i:off + i + rows] * w[i:i + 1]
    return pre


def conv_fwd(proj, conv_w8, *, name):
    s = proj.shape[0]
    wq = 3 * MIX
    ts = _tile(s, 512)
    nb = ts // CONV_PAD

    def body(x_ref, prev_ref, w_ref, o_ref):
        prev = jnp.where(pl.program_id(0) > 0, prev_ref[...], 0.0)
        pre = _conv_pre(jnp.concatenate([prev, x_ref[...]], axis=0), w_ref[...], ts)
        o_ref[...] = pre * _sigmoid(pre)

    return pl.pallas_call(
        body, name=name, grid=(s // ts,),
        in_specs=[pl.BlockSpec((ts, wq), lambda i: (i, SEC_QKV // wq)),
                  pl.BlockSpec((CONV_PAD, wq), lambda i: (jnp.maximum(i * nb - 1, 0), SEC_QKV // wq)),
                  pl.BlockSpec((CONV_PAD, wq), lambda i: (0, 0))],
        out_specs=pl.BlockSpec((ts, wq), lambda i: (i, 0)), out_shape=jax.ShapeDtypeStruct((s, wq), f32),
        compiler_params=_params(1),
    )(proj, proj, conv_w8)


def conv_bwd(proj, conv_w8, dxc, dproj, *, name):
    s = proj.shape[0]
    wq = 3 * MIX
    ts = _tile(s, 512)
    nb = ts // CONV_PAD
    nt = s // ts
    last_blk = s // CONV_PAD - 1

    def body(x_ref, prev_ref, next_ref, w_ref, d_ref, dnext_ref, _, dp_ref, dw_ref):
        i = pl.program_id(0)

        @pl.when(i == 0)
        def _():
            dw_ref[...] = jnp.zeros_like(dw_ref)

        w = w_ref[...]
        prev = jnp.where(i > 0, prev_ref[...], 0.0)
        more = i < nt - 1
        xp = jnp.concatenate([prev, x_ref[...], jnp.where(more, next_ref[...], 0.0)], axis=0)
        ext = ts + CONV_PAD
        pre = _conv_pre(xp, w, ext)
        sig = _sigmoid(pre)
        dxc_ext = jnp.concatenate([d_ref[...], jnp.where(more, dnext_ref[...], 0.0)], axis=0)
        dpre = dxc_ext * sig * (1.0 + pre * (1.0 - sig))
        off = CONV_PAD - (DN_CONV - 1)
        d_raw = jnp.zeros((ts, wq), f32)
        dws = []
        for k in range(DN_CONV):
            shift = DN_CONV - 1 - k
            d_raw = d_raw + dpre[shift:shift + ts] * w[k:k + 1]
            dws.append(jnp.sum(dpre[:ts] * xp[off + k:off + k + ts], axis=0, keepdims=True))
        dp_ref[...] = d_raw.astype(bf16)
        dw_ref[...] += jnp.concatenate(dws + [jnp.zeros((CONV_PAD - DN_CONV, wq), f32)], axis=0)

    sec = SEC_QKV // wq
    return pl.pallas_call(
        body, name=name, grid=(nt,),
        in_specs=[pl.BlockSpec((ts, wq), lambda i: (i, sec)),
                  pl.BlockSpec((CONV_PAD, wq), lambda i: (jnp.maximum(i * nb - 1, 0), sec)),
                  pl.BlockSpec((CONV_PAD, wq), lambda i: (jnp.minimum((i + 1) * nb, last_blk), sec)),
                  pl.BlockSpec((CONV_PAD, wq), lambda i: (0, 0)),
                  pl.BlockSpec((ts, wq), lambda i: (i, 0)),
                  pl.BlockSpec((CONV_PAD, wq), lambda i: (jnp.minimum((i + 1) * nb, last_blk), 0)),
                  pl.BlockSpec(memory_space=pl.ANY)],
        out_specs=[pl.BlockSpec((ts, wq), lambda i: (i, sec)), pl.BlockSpec((CONV_PAD, wq), lambda i: (0, 0))],
        out_shape=[jax.ShapeDtypeStruct(dproj.shape, bf16), jax.ShapeDtypeStruct((CONV_PAD, wq), f32)],
        input_output_aliases={6: 0}, compiler_params=_params(1),
    )(proj, proj, proj, conv_w8, dxc, dxc, dproj)


def _dn_chunk(state, xc, z, ba, alog_row, dtb_row, norm_row):
    c, nh = DN_CHUNK, DN_HEADS
    n = c * nh
    row = lax.broadcasted_iota(jnp.int32, (n, n), 0)
    col = lax.broadcasted_iota(jnp.int32, (n, n), 1)
    same_head = (row // c) == (col // c)
    tril, strict = same_head & (row >= col), same_head & (row > col)
    tril_c = lax.broadcasted_iota(jnp.int32, (c, c), 0) >= lax.broadcasted_iota(jnp.int32, (c, c), 1)
    beta_all = _sigmoid(ba)
    g_all = -jnp.exp(alog_row) * jax.nn.softplus(ba + dtb_row)
    gc_all = hdot(tril_c.astype(f32), g_all)
    gc_t = gc_all.T

    def stack(piece):
        return jnp.concatenate([piece(h) for h in range(nh)], axis=0)

    q = stack(lambda h: xc[:, h * DN_HD:(h + 1) * DN_HD])
    k = stack(lambda h: xc[:, MIX + h * DN_HD:MIX + (h + 1) * DN_HD])
    v = stack(lambda h: xc[:, 2 * MIX + h * DN_HD:2 * MIX + (h + 1) * DN_HD])
    zs = stack(lambda h: z[:, h * DN_HD:(h + 1) * DN_HD])
    q = q * lax.rsqrt(jnp.sum(q * q, axis=-1, keepdims=True) + NORM_EPS) * (DN_HD ** -0.5)
    k = k * lax.rsqrt(jnp.sum(k * k, axis=-1, keepdims=True) + NORM_EPS)
    beta = stack(lambda h: _lane_col(beta_all, h))
    g_cols = [_lane_col(gc_all, nh + h) for h in range(nh)]
    g_col = jnp.concatenate(g_cols, axis=0)
    g_row = jnp.concatenate([gc_t[nh + h:nh + h + 1, :] for h in range(nh)], axis=1)
    g_last = stack(lambda h: jnp.broadcast_to(g_cols[h][c - 1:c, :], (c, 1)))
    decay = jnp.where(tril, jnp.exp(jnp.where(tril, g_col - g_row, 0.0)), 0.0)
    kb = k * beta
    low = jnp.where(strict, bdot(kb, k, 1, 1) * decay, 0.0)
    t_inv = tri_inv(low, c)
    e_gc = jnp.exp(g_col)
    uw = hdot(t_inv, jnp.concatenate([v * beta, kb * e_gc], axis=1))
    u, w = uw[:, :DN_HD], uw[:, DN_HD:]
    attn = bdot(q, k, 1, 1) * decay
    own = (lax.broadcasted_iota(jnp.int32, (n, nh * DN_HD), 1) // DN_HD
           == lax.broadcasted_iota(jnp.int32, (n, nh * DN_HD), 0) // c)

    def spread(a):
        return jnp.where(own, jnp.concatenate([a] * nh, axis=1), 0.0)

    v_new = u - bdot(spread(w), state, 1, 0)
    o = bdot(spread(q * e_gc), state, 1, 0) + bdot(attn, v_new, 1, 0)
    keep = stack(lambda h: jnp.broadcast_to(jnp.exp(g_cols[h][c - 1:c, :]), (DN_HD, 1)))
    new_state = state * keep + bdot(spread(k * jnp.exp(g_last - g_col)), v_new, 0, 0)
    out = _rms(o, norm_row) * (zs * _sigmoid(zs))
    return new_state, jnp.concatenate([out[h * c:(h + 1) * c] for h in range(nh)], axis=1)


DN_STEP = 2 * DN_CHUNK


def _dn_step(state, xc, z, ba, alog_row, dtb_row, norm_row):
    outs = []
    for c in range(DN_STEP // DN_CHUNK):
        rows = slice(c * DN_CHUNK, (c + 1) * DN_CHUNK)
        state, out = _dn_chunk(state, xc[rows], z[rows], ba[rows], alog_row, dtb_row, norm_row)
        outs.append(out)
    return state, jnp.concatenate(outs, axis=0)


def _dn_specs(ts, order):
    zb = SEC_Z // MIX
    return [pl.BlockSpec((ts, 3 * MIX), lambda i: (order(i), 0)),
            pl.BlockSpec((ts, MIX), lambda i: (order(i), zb)),
            pl.BlockSpec((ts, LANES), lambda i: (order(i), 0)),
            pl.BlockSpec((1, LANES), lambda i: (0, 0)), pl.BlockSpec((1, LANES), lambda i: (0, 0)),
            pl.BlockSpec((1, LANES), lambda i: (0, 0))]


def dn_fwd(xc, proj, p_ba, alog_row, dtb_row, norm_row, *, name):
    s = xc.shape[0]
    ts = _tile(s, 512)
    n_step = ts // DN_STEP

    def body(xc_ref, z_ref, ba_ref, al_ref, dt_ref, nr_ref, o_ref, st_ref, state_ref):
        @pl.when(pl.program_id(0) == 0)
        def _():
            state_ref[...] = jnp.zeros_like(state_ref)

        def step(c, carry):
            rows = pl.ds(pl.multiple_of(c * DN_STEP, DN_STEP), DN_STEP)
            st_ref[c] = state_ref[...]
            new_state, out = _dn_step(state_ref[...], xc_ref[rows, :], z_ref[rows, :], ba_ref[rows, :],
                                      al_ref[...], dt_ref[...], nr_ref[...])
            state_ref[...] = new_state
            o_ref[rows, :] = out.astype(bf16)
            return carry
        lax.fori_loop(0, n_step, step, 0)

    return pl.pallas_call(
        body, name=name, grid=(s // ts,), in_specs=_dn_specs(ts, lambda i: i),
        out_specs=[pl.BlockSpec((ts, MIX), lambda i: (i, 0)),
                   pl.BlockSpec((n_step, DN_HEADS * DN_HD, DN_HD), lambda i: (i, 0, 0))],
        out_shape=[jax.ShapeDtypeStruct((s, MIX), bf16),
                   jax.ShapeDtypeStruct((s // DN_STEP, DN_HEADS * DN_HD, DN_HD), f32)],
        scratch_shapes=[pltpu.VMEM((DN_HEADS * DN_HD, DN_HD), f32)],
        compiler_params=_params(1),
    )(xc, proj, p_ba, alog_row, dtb_row, norm_row)


def dn_bwd(xc, proj, p_ba, alog_row, dtb_row, norm_row, saved, d_out, dproj, *, name):
    s = xc.shape[0]
    ts = _tile(s, 512)
    n_step = ts // DN_STEP
    nt = s // ts

    def body(xc_ref, z_ref, ba_ref, al_ref, dt_ref, nr_ref, st_ref, do_ref, _,
             dz_ref, dxc_ref, dba_ref, dal_ref, ddt_ref, dnr_ref, dstate_ref):
        @pl.when(pl.program_id(0) == 0)
        def _():
            dstate_ref[...] = jnp.zeros_like(dstate_ref)
            dal_ref[...] = jnp.zeros_like(dal_ref)
            ddt_ref[...] = jnp.zeros_like(ddt_ref)
            dnr_ref[...] = jnp.zeros_like(dnr_ref)

        def step(it, carry):
            c = n_step - 1 - it
            rows = pl.ds(pl.multiple_of(c * DN_STEP, DN_STEP), DN_STEP)
            _, vjp = jax.vjp(_dn_step, st_ref[c], xc_ref[rows, :], z_ref[rows, :], ba_ref[rows, :],
                             al_ref[...], dt_ref[...], nr_ref[...])
            d_in, dxc, dz, dba, dal, ddt, dnr = vjp((dstate_ref[...], do_ref[rows, :]))
            dstate_ref[...] = d_in
            dxc_ref[rows, :] = dxc
            dz_ref[rows, :] = dz.astype(bf16)
            dba_ref[rows, :] = dba.astype(bf16)
            dal_ref[...] += dal
            ddt_ref[...] += ddt
            dnr_ref[...] += dnr
            return carry
        lax.fori_loop(0, n_step, step, 0)

    def rev(i):
        return nt - 1 - i

    specs = _dn_specs(ts, rev)
    vec = pl.BlockSpec((1, LANES), lambda i: (0, 0))
    return pl.pallas_call(
        body, name=name, grid=(nt,),
        in_specs=specs + [pl.BlockSpec((n_step, DN_HEADS * DN_HD, DN_HD), lambda i: (rev(i), 0, 0)),
                          pl.BlockSpec((ts, MIX), lambda i: (rev(i), 0)), pl.BlockSpec(memory_space=pl.ANY)],
        out_specs=[specs[1], specs[0], specs[2], vec, vec, vec],
        out_shape=[jax.ShapeDtypeStruct(dproj.shape, bf16), jax.ShapeDtypeStruct((s, 3 * MIX), f32),
                   jax.ShapeDtypeStruct((s, LANES), bf16)] + [jax.ShapeDtypeStruct((1, LANES), f32)] * 3,
        scratch_shapes=[pltpu.VMEM((DN_HEADS * DN_HD, DN_HD), f32)],
        input_output_aliases={8: 0}, compiler_params=_params(1),
    )(xc, proj, p_ba, alog_row, dtb_row, norm_row, saved, d_out, dproj)


def _merge_in_specs(ts, d):
    row = pl.BlockSpec((ts, MIX), lambda i: (i, 0))
    return [row, row, row, pl.BlockSpec((ts, 3 * d), lambda i: (i, SEC_G // (3 * d))),
            pl.BlockSpec((3, MIX, d), lambda i: (0, 0, 0))]


def merge_fwd(out_a, out_b, out_c, proj, w_branch, *, name):
    s, d = out_a.shape[0], w_branch.shape[2]
    ts = _tile(s, 256, 16)

    def body(a_ref, b_ref, c_ref, g_ref, w_ref, o_ref):
        acc = jnp.zeros((ts, d), f32)
        for n, r in enumerate((a_ref, b_ref, c_ref)):
            acc = acc + _sigmoid(g_ref[:, n * d:(n + 1) * d]) * _dg(r[...], w_ref[n], 1, 0)
        o_ref[...] = acc.astype(bf16)

    return pl.pallas_call(
        body, name=name, grid=(s // ts,), in_specs=_merge_in_specs(ts, d),
        out_specs=pl.BlockSpec((ts, d), lambda i: (i, 0)), out_shape=jax.ShapeDtypeStruct((s, d), bf16),
        compiler_params=_params(1),
    )(out_a, out_b, out_c, proj, w_branch)


def merge_bwd(out_a, out_b, out_c, proj, w_branch, d_merged, dproj, *, name):
    s, d = out_a.shape[0], w_branch.shape[2]
    ts = _tile(s, 256, 16)

    def body(a_ref, b_ref, c_ref, g_ref, w_ref, dm_ref, _, dg_ref, da_ref, db_ref, dc_ref, dw_ref):
        @pl.when(pl.program_id(0) == 0)
        def _():
            dw_ref[...] = jnp.zeros_like(dw_ref)

        dm = dm_ref[...]
        for n, (r, dr) in enumerate(((a_ref, da_ref), (b_ref, db_ref), (c_ref, dc_ref))):
            gate = _sigmoid(g_ref[:, n * d:(n + 1) * d])
            branch = _dg(r[...], w_ref[n], 1, 0)
            dg_ref[:, n * d:(n + 1) * d] = (dm * branch * gate * (1.0 - gate)).astype(bf16)
            d_branch = dm * gate
            dr[...] = _dg(d_branch, w_ref[n], 1, 1)
            dw_ref[n] += _dg(r[...], d_branch, 0, 0)

    specs = _merge_in_specs(ts, d)
    row_f = pl.BlockSpec((ts, MIX), lambda i: (i, 0))
    return pl.pallas_call(
        body, name=name, grid=(s // ts,),
        in_specs=specs + [pl.BlockSpec((ts, d), lambda i: (i, 0)), pl.BlockSpec(memory_space=pl.ANY)],
        out_specs=[specs[3], row_f, row_f, row_f, specs[4]],
        out_shape=[jax.ShapeDtypeStruct(dproj.shape, bf16)] + [jax.ShapeDtypeStruct((s, MIX), f32)] * 3
        + [jax.ShapeDtypeStruct(w_branch.shape, f32)],
        input_output_aliases={6: 0}, compiler_params=_params(1),
    )(out_a, out_b, out_c, proj, w_branch, d_merged, dproj)


def swiglu_fwd(gu, *, name):
    g2, s, w = gu.shape
    ng = g2 // 2
    ts = _tile(s, 1024, 16)

    def body(g_ref, u_ref, o_ref):
        g = g_ref[...].astype(f32)
        o_ref[...] = (g * _sigmoid(g) * u_ref[...].astype(f32)).astype(bf16)

    return pl.pallas_call(
        body, name=name, grid=(s // ts, ng),
        in_specs=[pl.BlockSpec((None, ts, w), lambda i, j: (j, i, 0)),
                  pl.BlockSpec((None, ts, w), lambda i, j: (ng + j, i, 0))],
        out_specs=pl.BlockSpec((None, ts, w), lambda i, j: (j, i, 0)), out_shape=jax.ShapeDtypeStruct((ng, s, w), bf16),
        compiler_params=_params(2),
    )(gu, gu)


def swiglu_bwd(gu, d_act, *, name):
    g2, s, w = gu.shape
    ng = g2 // 2
    ts = _tile(s, 1024, 16)

    def body(g_ref, u_ref, d_ref, o_ref):
        g, d = g_ref[...].astype(f32), d_ref[...]
        sig = _sigmoid(g)

        @pl.when(pl.program_id(1) < ng)
        def _():
            o_ref[...] = (d * u_ref[...].astype(f32) * sig * (1.0 + g * (1.0 - sig))).astype(bf16)

        @pl.when(pl.program_id(1) >= ng)
        def _():
            o_ref[...] = (d * g * sig).astype(bf16)

    return pl.pallas_call(
        body, name=name, grid=(s // ts, g2),
        in_specs=[pl.BlockSpec((None, ts, w), lambda i, j: (j % ng, i, 0)),
                  pl.BlockSpec((None, ts, w), lambda i, j: (ng + j % ng, i, 0)),
                  pl.BlockSpec((None, ts, w), lambda i, j: (j % ng, i, 0))],
        out_specs=pl.BlockSpec((None, ts, w), lambda i, j: (j, i, 0)), out_shape=jax.ShapeDtypeStruct((g2, s, w), bf16),
        compiler_params=_params(2),
    )(gu, gu, d_act)


def adamw(w, m, v, g_parts, *, name):
    r, cols = w.shape
    n_parts = g_parts.shape[0]
    lanes = -(-cols // LANES) * LANES
    tr = _tile(r, max(16, (128 * 1024) // lanes), 16)

    def body(w_ref, m_ref, v_ref, gp_ref, g_ref, d_ref, nm_ref, nv_ref):
        g = gp_ref[0].astype(f32)
        for k in range(1, n_parts):
            g = g + gp_ref[k].astype(f32)
        nm = ADAM_B1 * m_ref[...] + (1.0 - ADAM_B1) * g
        nv = ADAM_B2 * v_ref[...] + (1.0 - ADAM_B2) * jnp.square(g)
        m_hat = nm / (1.0 - ADAM_B1 ** ADAM_STEP)
        v_hat = nv / (1.0 - ADAM_B2 ** ADAM_STEP)
        g_ref[...] = g
        d_ref[...] = -ADAM_LR * (m_hat / (jnp.sqrt(v_hat) + ADAM_EPS) + ADAM_WD * w_ref[...])
        nm_ref[...] = nm
        nv_ref[...] = nv

    row = pl.BlockSpec((tr, cols), lambda i: (i, 0))
    return pl.pallas_call(
        body, name=name, grid=(r // tr,),
        in_specs=[row, row, row, pl.BlockSpec((n_parts, tr, cols), lambda i: (0, i, 0))],
        out_specs=[row] * 4, out_shape=[jax.ShapeDtypeStruct((r, cols), f32)] * 4, compiler_params=_params(1),
    )(w, m, v, g_parts)


def _mesh_pos():
    return lax.axis_index("x"), lax.axis_index("y"), lax.axis_index("c")


def _dev_index(p):
    return 4 * p[0] + 2 * p[1] + p[2]


def all_gather(blocks, *, name):
    n = len(blocks)
    any_spec = pl.BlockSpec(memory_space=pl.ANY)

    def body(*refs):
        ins, outs = refs[:n], refs[n:2 * n]
        send_sems, recv_sems, local_sems = refs[2 * n:]
        x, y, c = _mesh_pos()
        me, sibling = (x, y, c), (x, y, 1 - c)
        chips = [(1 - x, y), (x, 1 - y), (1 - x, 1 - y)]

        def copy(a, k, block, to, src=None):
            dst = outs[a].at[_dev_index(block)]
            return pltpu.make_async_remote_copy(
                src_ref=dst if src is None else src, dst_ref=dst, send_sem=send_sems.at[a, k],
                recv_sem=recv_sems.at[a, k], device_id=to, device_id_type=pl.DeviceIdType.MESH)

        mine = [pltpu.make_async_copy(ins[a], outs[a].at[_dev_index(me)], local_sems.at[a]) for a in range(n)]
        for cp in mine:
            cp.start()
        first = []
        for a in range(n):
            first.append(copy(a, 0, me, sibling, src=ins[a]))
            first += [copy(a, 1 + j, me, (*chip, c), src=ins[a]) for j, chip in enumerate(chips)]
        for cp in first:
            cp.start()
        passed = []
        for j, chip in enumerate(chips):
            for a in range(n):
                copy(a, 1 + j, (*chip, c), me).wait_recv()
                fwd = copy(a, 4 + j, (*chip, c), sibling)
                fwd.start()
                passed.append(fwd)
        for a in range(n):
            copy(a, 0, sibling, me).wait_recv()
            for j, chip in enumerate(chips):
                copy(a, 4 + j, (*chip, 1 - c), me).wait_recv()
        for cp in first + passed:
            cp.wait_send()
        for cp in mine:
            cp.wait()

    return pl.pallas_call(
        body, name=name, in_specs=[any_spec] * n, out_specs=[any_spec] * n,
        out_shape=[jax.ShapeDtypeStruct((N_DEV,) + b.shape, b.dtype) for b in blocks],
        scratch_shapes=[pltpu.SemaphoreType.DMA((n, 7)), pltpu.SemaphoreType.DMA((n, 7)),
                        pltpu.SemaphoreType.DMA((n,))],
    )(*blocks)


def exchange_grads(parts, small, *, name):
    n = len(parts)
    any_spec = pl.BlockSpec(memory_space=pl.ANY)

    def body(*refs):
        part_refs, small_ref = refs[:n], refs[n]
        recv_refs, small_all_ref = refs[n + 1:2 * n + 1], refs[2 * n + 1]
        send_sems, recv_sems, local_sems = refs[2 * n + 2:]
        x, y, c = _mesh_pos()
        mine = _dev_index((x, y, c))
        local = [pltpu.make_async_copy(part_refs[a].at[mine], recv_refs[a].at[mine], local_sems.at[a]) for a in range(n)]
        local.append(pltpu.make_async_copy(small_ref, small_all_ref.at[mine], local_sems.at[n]))
        for cp in local:
            cp.start()
        sends, arrivals = [], []
        for k in range(1, N_DEV):
            peer = (1 - x if k & 4 else x, 1 - y if k & 2 else y, 1 - c if k & 1 else c)
            theirs = _dev_index(peer)
            flows = [(part_refs[a].at[theirs], recv_refs[a].at[mine], recv_refs[a].at[theirs]) for a in range(n)]
            flows.append((small_ref, small_all_ref.at[mine], small_all_ref.at[theirs]))
            for a, (src, dst_mine, dst_theirs) in enumerate(flows):
                sems = dict(send_sem=send_sems.at[a, k - 1], recv_sem=recv_sems.at[a, k - 1], device_id=peer,
                            device_id_type=pl.DeviceIdType.MESH)
                sends.append(pltpu.make_async_remote_copy(src_ref=src, dst_ref=dst_mine, **sems))
                arrivals.append(pltpu.make_async_remote_copy(src_ref=src, dst_ref=dst_theirs, **sems))
        for cp in sends:
            cp.start()
        for cp in arrivals:
            cp.wait_recv()
        for cp in sends:
            cp.wait_send()
        for cp in local:
            cp.wait()

    outs = pl.pallas_call(
        body, name=name, in_specs=[any_spec] * (n + 1), out_specs=[any_spec] * (n + 1),
        out_shape=[jax.ShapeDtypeStruct(p.shape, p.dtype) for p in parts]
        + [jax.ShapeDtypeStruct((N_DEV,) + small.shape, small.dtype)],
        scratch_shapes=[pltpu.SemaphoreType.DMA((n + 1, 7)), pltpu.SemaphoreType.DMA((n + 1, 7)),
                        pltpu.SemaphoreType.DMA((n + 1,))],
    )(*parts, small)
    return outs[:n], outs[n]


def _rows128(arr):
    flat = arr.reshape(-1)
    rows = -(-flat.shape[0] // (8 * LANES)) * 8
    return jnp.pad(flat, (0, rows * LANES - flat.shape[0])).reshape(rows, LANES)


def _pad_lanes(row, width=LANES, at=0):
    return jnp.pad(row, (at, width - at - row.shape[0])).reshape(1, width)


def _w_in_sections(got):
    d = got.shape[1]
    wi = jnp.transpose(got, (1, 0, 2)).reshape(d, -1)
    w_main = jnp.concatenate([wi[:, :C_B], wi[:, C_Z:C_BA], wi[:, C_QKV:C_Z], wi[:, C_G:], wi[:, C_B:C_QKV]], axis=1)
    return w_main, jnp.pad(wi[:, C_BA:C_G], ((0, 0), (0, LANES - (C_G - C_BA))))


def _w_in_parts(gw_main, gw_ba):
    d = gw_main.shape[0]
    full = jnp.concatenate([gw_main[:, SEC_A:SEC_Z], gw_main[:, SEC_B:], gw_main[:, SEC_QKV:SEC_G],
                            gw_main[:, SEC_Z:SEC_QKV], gw_ba[:, :C_G - C_BA], gw_main[:, SEC_G:SEC_B]], axis=1)
    return jnp.transpose(full.reshape(d, N_DEV, -1), (1, 0, 2))


def kernel(x, positions, attn_norm, w_in, sgu_ln_g, sgu_ln_b, sgu_w, sgu_b, attn_sinks, dn_conv_w, dn_a_log, dn_dt_bias, dn_norm, w_branch, w_out, ffn_norm, w_gate_up, w_down, final_norm, loss_target, m_attn_norm, m_w_in, m_sgu_ln_g, m_sgu_ln_b, m_sgu_w, m_sgu_b, m_attn_sinks, m_dn_conv_w, m_dn_a_log, m_dn_dt_bias, m_dn_norm, m_w_branch, m_w_out, m_ffn_norm, m_w_gate_up, m_w_down, m_final_norm, v_attn_norm, v_w_in, v_sgu_ln_g, v_sgu_ln_b, v_sgu_w, v_sgu_b, v_attn_sinks, v_dn_conv_w, v_dn_a_log, v_dn_dt_bias, v_dn_norm, v_w_branch, v_w_out, v_ffn_norm, v_w_gate_up, v_w_down, v_final_norm):
    given = dict(locals())
    depth, d_model = attn_norm.shape
    s = x.shape[1]
    x2 = x.reshape(s, d_model)
    target = loss_target.reshape(s, d_model)
    posf = positions.reshape(s, 1).astype(f32)
    inv_freq = ROPE_THETA ** (-jnp.arange(0, ROPE_DIM, 2, dtype=f32) / ROPE_DIM)
    inv_head = jnp.concatenate([inv_freq, inv_freq, jnp.zeros((SWA_HD - ROPE_DIM,), f32)])
    inv_q = jnp.tile(inv_head, SWA_HEADS).reshape(1, MIX)
    inv_k = jnp.tile(inv_head, SWA_KV).reshape(1, LANES)

    big_names = ["w_in", "w_branch", "w_out", "w_gate_up", "w_down"]
    blocks = [given[n][l].astype(bf16) for l in range(depth) for n in big_names] + [dn_conv_w[l] for l in range(depth)]
    got = all_gather(blocks, name="gather_weights")
    layers = []
    for l in range(depth):
        g = dict(zip(big_names, got[l * len(big_names):(l + 1) * len(big_names)]))
        w_main, w_ba = _w_in_sections(g["w_in"])
        conv_full = jnp.transpose(got[depth * len(big_names) + l], (1, 0, 2)).reshape(DN_CONV, -1)
        layers.append(dict(
            w_main=w_main, w_ba=w_ba,
            w_branch=jnp.transpose(g["w_branch"], (1, 2, 0, 3)).reshape(3, MIX, d_model),
            w_out=g["w_out"].reshape(d_model, d_model),
            w_gu=g["w_gate_up"],
            w_down=g["w_down"].reshape(N_DEV // 2, -1, d_model),
            conv_w8=jnp.pad(conv_full, ((0, CONV_PAD - DN_CONV), (0, 0))),
            attn_norm=attn_norm[l].reshape(1, -1), ffn_norm=ffn_norm[l].reshape(1, -1),
            ln_g=sgu_ln_g[l].reshape(1, -1), ln_b=sgu_ln_b[l].reshape(1, -1), sgu_w=sgu_w[l], sgu_bt=sgu_b[l].T,
            sink_row=_pad_lanes(attn_sinks[l]), alog_row=_pad_lanes(dn_a_log[l], at=DN_HEADS),
            dtb_row=_pad_lanes(dn_dt_bias[l], at=DN_HEADS), norm_row=dn_norm[l].reshape(1, -1)))

    saved = []
    h_in = x2
    for l, p in enumerate(layers):
        t = f"l{l}_"
        h = rmsnorm_fwd(h_in, p["attn_norm"], name=t + "attn_norm")
        proj = matmul(h, p["w_main"], "nn", f32, name=t + "in_proj")
        p_ba = matmul(h, p["w_ba"], "nn", f32, name=t + "in_proj_ba")
        out_a = sgu_fwd(proj, p["ln_g"], p["ln_b"], p["sgu_w"], p["sgu_bt"], name=t + "sgu")
        out_b = swa_fwd(proj, p["sink_row"], posf, inv_q, inv_k, name=t + "swa")
        xc = conv_fwd(proj, p["conv_w8"], name=t + "dn_conv")
        out_c, states = dn_fwd(xc, proj, p_ba, p["alog_row"], p["dtb_row"], p["norm_row"], name=t + "deltanet")
        merged = merge_fwd(out_a, out_b, out_c, proj, p["w_branch"], name=t + "merge")
        x_mid = matmul(merged, p["w_out"], "nn", f32, residual=h_in, name=t + "out_proj")
        h2 = rmsnorm_fwd(x_mid, p["ffn_norm"], name=t + "ffn_norm")
        gu = matmul(h2, p["w_gu"], "nn", bf16, group="n", name=t + "gate_up")
        act = swiglu_fwd(gu, name=t + "swiglu")
        x_out = matmul(act, p["w_down"], "nn", f32, residual=x_mid, group="k", name=t + "down")
        saved.append(dict(x_in=h_in, h=h, proj=proj, p_ba=p_ba, out_a=out_a, out_b=out_b, out_c=out_c, xc=xc,
                          states=states, merged=merged, x_mid=x_mid, h2=h2, gu=gu, act=act))
        h_in = x_out

    dx, d_final_norm, loss_row = loss_head(h_in, final_norm.reshape(1, -1), target, name="loss_head")
    loss = lax.psum(loss_row[0, 0], MESH_AXES)

    shard_names = ["w_in", "dn_conv_w", "w_branch", "w_out", "w_gate_up", "w_down"]
    rep_names = ["attn_norm", "sgu_ln_g", "sgu_ln_b", "sgu_w", "sgu_b", "attn_sinks", "dn_a_log", "dn_dt_bias",
                 "dn_norm", "ffn_norm"]
    per_layer = []
    for l in reversed(range(depth)):
        p, sv, t = layers[l], saved[l], f"l{l}_b_"
        d_act = matmul(dx, p["w_down"], "nt", f32, group="n", name=t + "down_dx")
        gw_down = matmul(sv["act"], dx, "tn", bf16, group="m", name=t + "down_dw")
        d_gu = swiglu_bwd(sv["gu"], d_act, name=t + "swiglu")
        gw_gu = matmul(sv["h2"], d_gu, "tn", bf16, group="n", name=t + "gate_up_dw")
        d_h2 = matmul(d_gu, p["w_gu"], "nt", f32, group="k", name=t + "gate_up_dx")
        dx_mid, g_ffn = rmsnorm_bwd(sv["x_mid"], p["ffn_norm"], d_h2, dx, name=t + "ffn_norm")
        d_merged = matmul(dx_mid, p["w_out"], "nt", f32, name=t + "out_proj_dx")
        gw_out = matmul(sv["merged"], dx_mid, "tn", bf16, name=t + "out_proj_dw")
        dproj = lax.empty((s, W_MAIN), bf16)
        dproj, d_a, d_b, d_c, gw_branch = merge_bwd(sv["out_a"], sv["out_b"], sv["out_c"], sv["proj"], p["w_branch"],
                                                   d_merged, dproj, name=t + "merge")
        dproj, g_ln_g, g_ln_b, g_sgu_w, g_sgu_bt = sgu_bwd(sv["proj"], p["ln_g"], p["ln_b"], p["sgu_w"], p["sgu_bt"],
                                                         d_a, dproj, name=t + "sgu")
        dproj, g_sink = swa_bwd(sv["proj"], p["sink_row"], posf, inv_q, inv_k, d_b, dproj, name=t + "swa")
        dproj, dxc, dba, g_alog, g_dtb, g_dnorm = dn_bwd(sv["xc"], sv["proj"], sv["p_ba"], p["alog_row"], p["dtb_row"],
                                                        p["norm_row"], sv["states"], d_c, dproj, name=t + "deltanet")
        dproj, g_conv8 = conv_bwd(sv["proj"], p["conv_w8"], dxc, dproj, name=t + "dn_conv")
        gw_main = matmul(sv["h"], dproj, "tn", bf16, name=t + "in_proj_dw")
        gw_ba = matmul(sv["h"], dba, "tn", bf16, name=t + "in_proj_ba_dw")
        d_h = matmul(dproj, p["w_main"], "nt", f32, tk=768, name=t + "in_proj_dx")
        d_h = matmul(dba, p["w_ba"], "nt", f32, residual=d_h, name=t + "in_proj_ba_dx")
        dx, g_attn = rmsnorm_bwd(sv["x_in"], p["attn_norm"], d_h, dx_mid, name=t + "attn_norm")
        per_layer.append(dict(
            w_in=_w_in_parts(gw_main, gw_ba),
            dn_conv_w=jnp.transpose(g_conv8[:DN_CONV].reshape(DN_CONV, N_DEV, -1), (1, 0, 2)),
            w_branch=jnp.transpose(gw_branch.reshape(3, MIX, N_DEV, -1), (2, 0, 1, 3)).astype(bf16),
            w_out=gw_out.reshape(N_DEV, -1, d_model), w_gate_up=gw_gu, w_down=gw_down.reshape(N_DEV, -1, d_model),
            attn_norm=g_attn, sgu_ln_g=g_ln_g, sgu_ln_b=g_ln_b, sgu_w=g_sgu_w, sgu_b=g_sgu_bt.T,
            attn_sinks=g_sink[0, :SWA_HEADS], dn_a_log=g_alog[0, DN_HEADS:2 * DN_HEADS],
            dn_dt_bias=g_dtb[0, DN_HEADS:2 * DN_HEADS], dn_norm=g_dnorm, ffn_norm=g_ffn))
    per_layer.reverse()

    parts = [jnp.stack([pp[n] for pp in per_layer], axis=1) for n in shard_names]
    rep_grads = {n: jnp.stack([pp[n].reshape(given[n].shape[1:]) for pp in per_layer]) for n in rep_names}
    rep_grads["final_norm"] = d_final_norm[0]
    rep_names = rep_names + ["final_norm"]
    rep_rows = [_rows128(given[n]).shape[0] for n in rep_names]
    pad_rows = -sum(rep_rows) % 16

    def small_rows(values):
        pieces = [_rows128(values[n]) for n in rep_names]
        return jnp.concatenate(pieces + [jnp.zeros((pad_rows, LANES), f32)], axis=0)

    recv, small_all = exchange_grads(parts, small_rows(rep_grads), name="exchange_grads")

    results = [{}, {}, {}, {}]
    for n, got_n in zip(shard_names, recv):
        shp = given[n].shape
        two = (-1, shp[-1])
        outs = adamw(given[n].reshape(two), given["m_" + n].reshape(two), given["v_" + n].reshape(two),
                     got_n.reshape((N_DEV,) + given[n].reshape(two).shape), name="adamw_" + n)
        for res, val in zip(results, outs):
            res[n] = val.reshape(shp)
    outs = adamw(small_rows(given), small_rows({n: given["m_" + n] for n in rep_names}),
                 small_rows({n: given["v_" + n] for n in rep_names}), small_all, name="adamw_replicated")
    for res, val in zip(results, outs):
        row = 0
        for n, nr in zip(rep_names, rep_rows):
            res[n] = val[row:row + nr].reshape(-1)[:given[n].size].reshape(given[n].shape)
            row += nr
    order = ["attn_norm", "w_in", "sgu_ln_g", "sgu_ln_b", "sgu_w", "sgu_b", "attn_sinks", "dn_conv_w", "dn_a_log",
             "dn_dt_bias", "dn_norm", "w_branch", "w_out", "ffn_norm", "w_gate_up", "w_down", "final_norm"]
    return (loss, dx.reshape(x.shape), *[res[n] for res in results for n in order])
```

```python
import functools

import jax
import jax.numpy as jnp
from jax import lax
from jax.experimental import pallas as pl
from jax.experimental.pallas import tpu as pltpu

f32 = jnp.float32
bf16 = jnp.bfloat16

N_DEV = 8
MESH_AXES = ("x", "y", "c")
NORM_EPS = 1e-6
MIX = 512
SGU_GROUPS, SGU_CHUNK = 4, 128
SWA_HEADS, SWA_KV, SWA_HD, WINDOW = 8, 2, 64, 128
ROPE_THETA, ROPE_DIM = 500000.0, 16
DN_HEADS, DN_HD, DN_CONV, DN_CHUNK = 4, 128, 4, 64
ADAM_LR, ADAM_B1, ADAM_B2, ADAM_EPS, ADAM_WD, ADAM_STEP = 0.001, 0.9, 0.999, 1e-08, 0.01, 10

LANES = 128
VMEM_LIMIT = 56 * 1024 * 1024

SEC_A, SEC_Z, SEC_QKV, SEC_G, SEC_B = 0, 1024, 1536, 3072, 6144
W_MAIN = 6912
C_B, C_QKV, C_Z, C_BA, C_G = 1024, 1792, 3328, 3840, 3848


def _params(n_axes, **kw):
    return pltpu.CompilerParams(dimension_semantics=("arbitrary",) * n_axes, vmem_limit_bytes=VMEM_LIMIT, **kw)


def _tile(n, target, mult=LANES):
    if n <= target:
        return n
    best = None
    for t in range(mult, target + 1, mult):
        if n % t == 0:
            best = t
    assert best is not None, (n, target, mult)
    return best


def _dg(a, b, ca, cb):
    return lax.dot_general(a.astype(bf16), b.astype(bf16), (((ca,), (cb,)), ((), ())), preferred_element_type=f32)


def _dg3(a, b, ca, cb):
    a_hi, b_hi = a.astype(bf16), b.astype(bf16)
    a_lo, b_lo = (a - a_hi.astype(f32)).astype(bf16), (b - b_hi.astype(f32)).astype(bf16)

    def dot(p, q):
        return lax.dot_general(p, q, (((ca,), (cb,)), ((), ())), preferred_element_type=f32)

    return dot(a_hi, b_hi) + (dot(a_hi, b_lo) + dot(a_lo, b_hi))


def _differentiable_dot(core):
    @functools.partial(jax.custom_vjp, nondiff_argnums=(2, 3))
    def dot(a, b, ca, cb):
        return core(a, b, ca, cb)

    def fwd(a, b, ca, cb):
        return core(a, b, ca, cb), (a, b)

    def bwd(ca, cb, res, ct):
        a, b = res
        da = core(ct, b, 1, 1 - cb) if ca == 1 else core(b, ct, 1 - cb, 1)
        db = core(a, ct, 1 - ca, 0) if cb == 0 else core(ct, a, 0, 1 - ca)
        return da, db

    dot.defvjp(fwd, bwd)
    return dot


bdot = _differentiable_dot(_dg)
_hdot = _differentiable_dot(_dg3)


def hdot(a, b, ca=1, cb=0):
    return _hdot(a, b, ca, cb)


@functools.partial(jax.custom_vjp, nondiff_argnums=(1,))
def lroll(x, shift):
    return pltpu.roll(x, shift, 1)


def _lroll_fwd(x, shift):
    return pltpu.roll(x, shift, 1), None


def _lroll_bwd(shift, _, ct):
    return (pltpu.roll(ct, ct.shape[1] - shift, 1),)


lroll.defvjp(_lroll_fwd, _lroll_bwd)


@functools.partial(jax.custom_vjp, nondiff_argnums=(1,))
def tri_inv(low, nil):
    n = low.shape[0]
    row = lax.broadcasted_iota(jnp.int32, (n, n), 0)
    col = lax.broadcasted_iota(jnp.int32, (n, n), 1)
    eye = (row == col).astype(f32)
    m = -low
    p = eye + m
    span = 2
    while span < nil:
        m = hdot(m, m)
        p = p + hdot(p, m)
        span *= 2
    return p


def _tri_inv_fwd(low, nil):
    t = tri_inv(low, nil)
    return t, t


def _tri_inv_bwd(nil, t, dt):
    return (-hdot(t, hdot(dt, t, 1, 1), 0, 0),)


tri_inv.defvjp(_tri_inv_fwd, _tri_inv_bwd)


def _sigmoid(x):
    return 1.0 / (1.0 + jnp.exp(-x))


def _rms(x, g):
    return x * lax.rsqrt(jnp.mean(x * x, axis=-1, keepdims=True) + NORM_EPS) * g


def _lane_col(x, lane_idx):
    lane = lax.broadcasted_iota(jnp.int32, x.shape, 1)
    return jnp.sum(jnp.where(lane == lane_idx, x, 0.0), axis=1, keepdims=True)


def matmul(a, b, mode, out_dtype, *, residual=None, group=None, side=None, tm=1024, tn=768, tk=1024, name):
    dims = {"a": ("m", "k") if mode != "tn" else ("k", "m"),
            "b": {"nn": ("k", "n"), "nt": ("n", "k"), "tn": ("k", "n")}[mode], "o": ("m", "n")}
    full, groups = {}, 1
    for arr, key in ((a, "a"), (b, "b")):
        grouped = group in dims[key]
        if grouped:
            groups = arr.shape[0]
        full[dims[key][0]], full[dims[key][1]] = arr.shape[1:] if grouped else arr.shape
    want = {"m": tm, "n": tn, "k": tk}
    tiles = {d: full[d] if d == group else _tile(full[d], want[d]) for d in "mnk"}
    steps = {d: groups if d == group else full[d] // tiles[d] for d in "mnk"}

    def spec(key):
        d0, d1 = dims[key]

        def index(i, j, kk):
            at = {"m": i, "n": j, "k": kk}
            if group in (d0, d1):
                return (at[group], 0 if d0 == group else at[d0], 0 if d1 == group else at[d1])
            return (at[d0], at[d1])

        block = (tiles[d0], tiles[d1])
        return pl.BlockSpec(((None,) + block) if group in (d0, d1) else block, index)

    ca, cb = {"nn": (1, 0), "nt": (1, 1), "tn": (0, 0)}[mode]
    nk = steps["k"]
    o_spec = spec("o")
    out_shape = (groups, full["m"], full["n"]) if group in ("m", "n") else (full["m"], full["n"])
    has_res = residual is not None
    n_in = 3 if has_res else 2
    n_side = side.n if side is not None else 0
    grid = (steps["m"], steps["n"], nk)

    def body(*refs):
        a_ref, b_ref = refs[:2]
        r_ref = refs[2] if has_res else None
        side_in, o_ref = refs[n_in:n_in + n_side], refs[n_in + n_side]
        side_out, acc_ref = refs[n_in + n_side + 1:n_in + 2 * n_side + 1], refs[n_in + 2 * n_side + 1]
        sems = refs[n_in + 2 * n_side + 2:]
        kk = pl.program_id(2)
        if side is not None:
            side.run_around(grid, side_in, side_out, sems, before=True)

        @pl.when(kk == 0)
        def _():
            acc_ref[...] = jnp.zeros_like(acc_ref)

        acc_ref[...] += _dg(a_ref[...], b_ref[...], ca, cb)

        @pl.when(kk == nk - 1)
        def _():
            acc = acc_ref[...]
            if has_res:
                acc = acc + r_ref[...]
            o_ref[...] = acc.astype(out_dtype)

        if side is not None:
            side.run_around(grid, side_in, side_out, sems, before=False)

    in_specs = [spec("a"), spec("b")] + ([o_spec] if has_res else [])
    args = (a, b) + ((residual,) if has_res else ())
    out_specs, out_shapes, scratch = [o_spec], [jax.ShapeDtypeStruct(out_shape, out_dtype)], [pltpu.VMEM((tiles["m"], tiles["n"]), f32)]
    if side is not None:
        in_specs, args = in_specs + side.specs, args + tuple(side.arrays)
        out_specs, out_shapes, scratch = out_specs + side.specs, out_shapes + side.out_shapes, scratch + side.scratch
    outs = pl.pallas_call(
        body, name=name, grid=grid, in_specs=in_specs, out_specs=out_specs, out_shape=out_shapes,
        scratch_shapes=scratch, compiler_params=_params(3),
    )(*args)
    return outs[0] if side is None else (outs[0], outs[1:])


def rmsnorm_fwd(x, g_row, *, name):
    s, d = x.shape
    ts = _tile(s, 512, 16)

    def body(x_ref, g_ref, o_ref):
        o_ref[...] = _rms(x_ref[...], g_ref[...]).astype(bf16)

    return pl.pallas_call(
        body, name=name, grid=(s // ts,),
        in_specs=[pl.BlockSpec((ts, d), lambda i: (i, 0)), pl.BlockSpec((1, d), lambda i: (0, 0))],
        out_specs=pl.BlockSpec((ts, d), lambda i: (i, 0)), out_shape=jax.ShapeDtypeStruct((s, d), bf16),
        compiler_params=_params(1),
    )(x, g_row)


def rmsnorm_bwd(x, g_row, dh, dres, *, name):
    s, d = x.shape
    ts = _tile(s, 512, 16)

    def body(x_ref, g_ref, dh_ref, dres_ref, dx_ref, dg_ref):
        @pl.when(pl.program_id(0) == 0)
        def _():
            dg_ref[...] = jnp.zeros_like(dg_ref)

        _, vjp = jax.vjp(_rms, x_ref[...], g_ref[...])
        dx, dg = vjp(dh_ref[...])
        dx_ref[...] = dx + dres_ref[...]
        dg_ref[...] += dg

    row = pl.BlockSpec((ts, d), lambda i: (i, 0))
    vec = pl.BlockSpec((1, d), lambda i: (0, 0))
    return pl.pallas_call(
        body, name=name, grid=(s // ts,), in_specs=[row, vec, row, row], out_specs=[row, vec],
        out_shape=[jax.ShapeDtypeStruct((s, d), f32), jax.ShapeDtypeStruct((1, d), f32)],
        compiler_params=_params(1),
    )(x, g_row, dh, dres)


def loss_head(x, g_row, target, *, name):
    s, d = x.shape
    ts = _tile(s, 512, 16)

    def body(x_ref, g_ref, t_ref, dx_ref, dg_ref, loss_ref):
        @pl.when(pl.program_id(0) == 0)
        def _():
            dg_ref[...] = jnp.zeros_like(dg_ref)
            loss_ref[...] = jnp.zeros_like(loss_ref)

        y, vjp = jax.vjp(_rms, x_ref[...], g_ref[...])
        err = y - t_ref[...]
        dx, dg = vjp(err * (1.0 / d))
        dx_ref[...] = dx
        dg_ref[...] += dg
        loss_ref[...] += 0.5 * jnp.sum(jnp.sum(err * err, axis=1, keepdims=True) * (1.0 / d), axis=0, keepdims=True)

    row = pl.BlockSpec((ts, d), lambda i: (i, 0))
    vec = pl.BlockSpec((1, d), lambda i: (0, 0))
    one = pl.BlockSpec((1, LANES), lambda i: (0, 0))
    return pl.pallas_call(
        body, name=name, grid=(s // ts,), in_specs=[row, vec, row], out_specs=[row, vec, one],
        out_shape=[jax.ShapeDtypeStruct((s, d), f32), jax.ShapeDtypeStruct((1, d), f32),
                   jax.ShapeDtypeStruct((1, LANES), f32)],
        compiler_params=_params(1),
    )(x, g_row, target)


def _sgu_chunk(p_a, ln_g, ln_b, w, b_t):
    t = SGU_CHUNK
    u = jax.nn.gelu(p_a[:, :MIX])
    v = jax.nn.gelu(p_a[:, MIX:])
    vc = v - jnp.mean(v, axis=-1, keepdims=True)
    vn = vc * lax.rsqrt(jnp.mean(vc * vc, axis=-1, keepdims=True) + NORM_EPS) * ln_g + ln_b
    causal = lax.broadcasted_iota(jnp.int32, (t, t), 0) >= lax.broadcasted_iota(jnp.int32, (t, t), 1)
    outs = []
    for g in range(SGU_GROUPS):
        sl = slice(g * LANES, (g + 1) * LANES)
        mixed = bdot(jnp.where(causal, w[g], 0.0), vn[:, sl], 1, 0) + b_t[:, g:g + 1]
        outs.append(u[:, sl] * mixed)
    return jnp.concatenate(outs, axis=1)


def _sgu_specs(s, ts):
    return [pl.BlockSpec((ts, 2 * MIX), lambda i: (i, SEC_A // (2 * MIX))),
            pl.BlockSpec((1, MIX), lambda i: (0, 0)), pl.BlockSpec((1, MIX), lambda i: (0, 0)),
            pl.BlockSpec((SGU_GROUPS, SGU_CHUNK, SGU_CHUNK), lambda i: (0, 0, 0)),
            pl.BlockSpec((SGU_CHUNK, SGU_GROUPS), lambda i: (0, 0))]


def sgu_fwd(proj, ln_g, ln_b, w, b_t, *, name):
    s = proj.shape[0]
    ts = _tile(s, 512)
    n_chunk = ts // SGU_CHUNK

    def body(p_ref, g_ref, b_ref, w_ref, bt_ref, o_ref):
        def step(c, carry):
            rows = pl.ds(pl.multiple_of(c * SGU_CHUNK, SGU_CHUNK), SGU_CHUNK)
            o_ref[rows, :] = _sgu_chunk(p_ref[rows, :], g_ref[...], b_ref[...], w_ref[...], bt_ref[...]).astype(bf16)
            return carry
        lax.fori_loop(0, n_chunk, step, 0)

    return pl.pallas_call(
        body, name=name, grid=(s // ts,), in_specs=_sgu_specs(s, ts),
        out_specs=pl.BlockSpec((ts, MIX), lambda i: (i, 0)), out_shape=jax.ShapeDtypeStruct((s, MIX), bf16),
        compiler_params=_params(1),
    )(proj, ln_g, ln_b, w, b_t)


def sgu_bwd(proj, ln_g, ln_b, w, b_t, d_out, dproj, *, name):
    s = proj.shape[0]
    ts = _tile(s, 512)
    n_chunk = ts // SGU_CHUNK

    def body(p_ref, g_ref, b_ref, w_ref, bt_ref, do_ref, _, dp_ref, dg_ref, db_ref, dw_ref, dbt_ref):
        @pl.when(pl.program_id(0) == 0)
        def _():
            dg_ref[...] = jnp.zeros_like(dg_ref)
            db_ref[...] = jnp.zeros_like(db_ref)
            dw_ref[...] = jnp.zeros_like(dw_ref)
            dbt_ref[...] = jnp.zeros_like(dbt_ref)

        def step(c, carry):
            rows = pl.ds(pl.multiple_of(c * SGU_CHUNK, SGU_CHUNK), SGU_CHUNK)
            _, vjp = jax.vjp(_sgu_chunk, p_ref[rows, :], g_ref[...], b_ref[...], w_ref[...], bt_ref[...])
            dp, dg, db, dw, dbt = vjp(do_ref[rows, :])
            dp_ref[rows, :] = dp.astype(bf16)
            dg_ref[...] += dg
            db_ref[...] += db
            dw_ref[...] += dw
            dbt_ref[...] += dbt
            return carry
        lax.fori_loop(0, n_chunk, step, 0)

    specs = _sgu_specs(s, ts)
    return pl.pallas_call(
        body, name=name, grid=(s // ts,),
        in_specs=specs + [pl.BlockSpec((ts, MIX), lambda i: (i, 0)), pl.BlockSpec(memory_space=pl.ANY)],
        out_specs=[specs[0], specs[1], specs[2], specs[3], specs[4]],
        out_shape=[jax.ShapeDtypeStruct(dproj.shape, bf16), jax.ShapeDtypeStruct((1, MIX), f32),
                   jax.ShapeDtypeStruct((1, MIX), f32), jax.ShapeDtypeStruct(w.shape, f32),
                   jax.ShapeDtypeStruct(b_t.shape, f32)],
        input_output_aliases={6: 0}, compiler_params=_params(1),
    )(proj, ln_g, ln_b, w, b_t, d_out, dproj)


def _rope(x, pos, inv_freq):
    w = x.shape[1]
    d = lax.broadcasted_iota(jnp.int32, (1, w), 1) % SWA_HD
    half = ROPE_DIM // 2
    ang = pos * inv_freq
    c, s = jnp.cos(ang), jnp.sin(ang)
    lo = jnp.where(d < half, s, 0.0)
    hi = jnp.where((d >= half) & (d < ROPE_DIM), s, 0.0)
    return x * c - lroll(x, w - half) * lo + lroll(x, half) * hi


def _swa_block(q, kp, kc, vp, vc, sink_row, pos_q, pos_p, inv_q, inv_k, prev_ok):
    t = WINDOW
    q = _rope(q, pos_q, inv_q) * (SWA_HD ** -0.5)
    keys = jnp.concatenate([_rope(kp, pos_p, inv_k), _rope(kc, pos_q, inv_k)], axis=0)
    vals = jnp.concatenate([vp, vc], axis=0)
    qi = lax.broadcasted_iota(jnp.int32, (t, 2 * t), 0)
    sj = lax.broadcasted_iota(jnp.int32, (t, 2 * t), 1)
    diff = qi + t - sj
    mask = (diff >= 0) & (diff < t) & (prev_ok | (sj >= t))
    lane_half = lax.broadcasted_iota(jnp.int32, (t, LANES), 1) // SWA_HD
    group = SWA_HEADS // SWA_KV
    slabs = []
    for pair in range(SWA_HEADS // 2):
        q_pair = q[:, pair * LANES:(pair + 1) * LANES]
        acc = jnp.zeros((t, LANES), f32)
        for half in range(2):
            h = 2 * pair + half
            kv = h // group
            qm = jnp.where(lane_half == half, q_pair, 0.0)
            if half != kv:
                qm = lroll(qm, SWA_HD)
            logits = jnp.where(mask, bdot(qm, keys, 1, 1), -1e30)
            sink = _lane_col(sink_row, h)
            m = jnp.maximum(jnp.max(logits, axis=1, keepdims=True), sink)
            p = jnp.exp(logits - m)
            probs = p / (jnp.sum(p, axis=1, keepdims=True) + jnp.exp(sink - m))
            o = jnp.where(lane_half == kv, bdot(probs, vals, 1, 0), 0.0)
            if half != kv:
                o = lroll(o, SWA_HD)
            acc = acc + o
        slabs.append(acc)
    return jnp.concatenate(slabs, axis=1)


def _swa_in_specs(nc, clamp):
    t = WINDOW
    qb, kb, vb = SEC_B // MIX, (SEC_B + MIX) // LANES, (SEC_B + MIX + LANES) // LANES

    def cur(i):
        return jnp.minimum(i, nc - 1) if clamp else i

    def prev(i):
        return jnp.maximum(cur(i) - 1, 0)

    return [pl.BlockSpec((t, MIX), lambda i: (cur(i), qb)),
            pl.BlockSpec((t, LANES), lambda i: (prev(i), kb)), pl.BlockSpec((t, LANES), lambda i: (cur(i), kb)),
            pl.BlockSpec((t, LANES), lambda i: (prev(i), vb)), pl.BlockSpec((t, LANES), lambda i: (cur(i), vb)),
            pl.BlockSpec((1, LANES), lambda i: (0, 0)),
            pl.BlockSpec((t, 1), lambda i: (cur(i), 0)), pl.BlockSpec((t, 1), lambda i: (prev(i), 0)),
            pl.BlockSpec((1, MIX), lambda i: (0, 0)), pl.BlockSpec((1, LANES), lambda i: (0, 0))]


def swa_fwd(proj, sink_row, posf, inv_q, inv_k, *, name):
    s = proj.shape[0]
    nc = s // WINDOW

    def body(q_ref, kp_ref, kc_ref, vp_ref, vc_ref, sink_ref, pq_ref, pp_ref, iq_ref, ik_ref, o_ref):
        prev_ok = pl.program_id(0) > 0
        o_ref[...] = _swa_block(q_ref[...], kp_ref[...], kc_ref[...], vp_ref[...], vc_ref[...], sink_ref[...],
                                pq_ref[...], pp_ref[...], iq_ref[...], ik_ref[...], prev_ok).astype(bf16)

    return pl.pallas_call(
        body, name=name, grid=(nc,), in_specs=_swa_in_specs(nc, False),
        out_specs=pl.BlockSpec((WINDOW, MIX), lambda i: (i, 0)), out_shape=jax.ShapeDtypeStruct((s, MIX), bf16),
        compiler_params=_params(1),
    )(proj, proj, proj, proj, proj, sink_row, posf, posf, inv_q, inv_k)


def swa_bwd(proj, sink_row, posf, inv_q, inv_k, d_out, dproj, *, side=None, name):
    s = proj.shape[0]
    nc = s // WINDOW
    t = WINDOW

    def body(q_ref, kp_ref, kc_ref, vp_ref, vc_ref, sink_ref, pq_ref, pp_ref, iq_ref, ik_ref, do_ref, _,
             dp_ref, dsink_ref, cq_ref, ck_ref, cv_ref):
        i = pl.program_id(0)

        @pl.when(i == 0)
        def _():
            dsink_ref[...] = jnp.zeros_like(dsink_ref)

        def write(dk_prev, dv_prev):
            dp_ref[:, :MIX] = cq_ref[...].astype(bf16)
            dp_ref[:, MIX:MIX + LANES] = (ck_ref[...] + dk_prev).astype(bf16)
            dp_ref[:, MIX + LANES:] = (cv_ref[...] + dv_prev).astype(bf16)

        @pl.when(i < nc)
        def _():
            fn = functools.partial(_swa_block, pos_q=pq_ref[...], pos_p=pp_ref[...], inv_q=iq_ref[...],
                                   inv_k=ik_ref[...], prev_ok=i > 0)
            _, vjp = jax.vjp(fn, q_ref[...], kp_ref[...], kc_ref[...], vp_ref[...], vc_ref[...], sink_ref[...])
            dq, dkp, dkc, dvp, dvc, dsink = vjp(do_ref[...])
            dsink_ref[...] += dsink

            @pl.when(i > 0)
            def _():
                write(dkp, dvp)

            cq_ref[...] = dq
            ck_ref[...] = dkc
            cv_ref[...] = dvc

        @pl.when(i == nc)
        def _():
            write(0.0, 0.0)

    return host_call(
        side, body, name=name, grid=(nc + 1,),
        in_specs=_swa_in_specs(nc, True) + [pl.BlockSpec((t, MIX), lambda i: (jnp.minimum(i, nc - 1), 0)),
                                            pl.BlockSpec(memory_space=pl.ANY)],
        out_specs=[pl.BlockSpec((t, MIX + 2 * LANES), lambda i: (jnp.maximum(i - 1, 0), SEC_B // (MIX + 2 * LANES))),
                   pl.BlockSpec((1, LANES), lambda i: (0, 0))],
        out_shape=[jax.ShapeDtypeStruct(dproj.shape, bf16), jax.ShapeDtypeStruct((1, LANES), f32)],
        scratch_shapes=[pltpu.VMEM((t, MIX), f32), pltpu.VMEM((t, LANES), f32), pltpu.VMEM((t, LANES), f32)],
        aliases={11: 0}, args=(proj, proj, proj, proj, proj, sink_row, posf, posf, inv_q, inv_k, d_out, dproj))


CONV_PAD = 8


def _conv_pre(xp, w, rows):
    off = CONV_PAD - (DN_CONV - 1)
    pre = xp[off:off + rows] * w[0:1]
    for i in range(1, DN_CONV):
        pre = pre + xp[off + i:off + i + rows] * w[i:i + 1]
    return pre


def conv_fwd(proj, conv_w8, *, name):
    s = proj.shape[0]
    wq = 3 * MIX
    ts = _tile(s, 512)
    nb = ts // CONV_PAD

    def body(x_ref, prev_ref, w_ref, o_ref):
        prev = jnp.where(pl.program_id(0) > 0, prev_ref[...], 0.0)
        pre = _conv_pre(jnp.concatenate([prev, x_ref[...]], axis=0), w_ref[...], ts)
        o_ref[...] = pre * _sigmoid(pre)

    return pl.pallas_call(
        body, name=name, grid=(s // ts,),
        in_specs=[pl.BlockSpec((ts, wq), lambda i: (i, SEC_QKV // wq)),
                  pl.BlockSpec((CONV_PAD, wq), lambda i: (jnp.maximum(i * nb - 1, 0), SEC_QKV // wq)),
                  pl.BlockSpec((CONV_PAD, wq), lambda i: (0, 0))],
        out_specs=pl.BlockSpec((ts, wq), lambda i: (i, 0)), out_shape=jax.ShapeDtypeStruct((s, wq), f32),
        compiler_params=_params(1),
    )(proj, proj, conv_w8)


def conv_bwd(proj, conv_w8, dxc, dproj, *, name):
    s = proj.shape[0]
    wq = 3 * MIX
    ts = _tile(s, 512)
    nb = ts // CONV_PAD
    nt = s // ts
    last_blk = s // CONV_PAD - 1

    def body(x_ref, prev_ref, next_ref, w_ref, d_ref, dnext_ref, _, dp_ref, dw_ref):
        i = pl.program_id(0)

        @pl.when(i == 0)
        def _():
            dw_ref[...] = jnp.zeros_like(dw_ref)

        w = w_ref[...]
        prev = jnp.where(i > 0, prev_ref[...], 0.0)
        more = i < nt - 1
        xp = jnp.concatenate([prev, x_ref[...], jnp.where(more, next_ref[...], 0.0)], axis=0)
        ext = ts + CONV_PAD
        pre = _conv_pre(xp, w, ext)
        sig = _sigmoid(pre)
        dxc_ext = jnp.concatenate([d_ref[...], jnp.where(more, dnext_ref[...], 0.0)], axis=0)
        dpre = dxc_ext * sig * (1.0 + pre * (1.0 - sig))
        off = CONV_PAD - (DN_CONV - 1)
        d_raw = jnp.zeros((ts, wq), f32)
        dws = []
        for k in range(DN_CONV):
            shift = DN_CONV - 1 - k
            d_raw = d_raw + dpre[shift:shift + ts] * w[k:k + 1]
            dws.append(jnp.sum(dpre[:ts] * xp[off + k:off + k + ts], axis=0, keepdims=True))
        dp_ref[...] = d_raw.astype(bf16)
        dw_ref[...] += jnp.concatenate(dws + [jnp.zeros((CONV_PAD - DN_CONV, wq), f32)], axis=0)

    sec = SEC_QKV // wq
    return pl.pallas_call(
        body, name=name, grid=(nt,),
        in_specs=[pl.BlockSpec((ts, wq), lambda i: (i, sec)),
                  pl.BlockSpec((CONV_PAD, wq), lambda i: (jnp.maximum(i * nb - 1, 0), sec)),
                  pl.BlockSpec((CONV_PAD, wq), lambda i: (jnp.minimum((i + 1) * nb, last_blk), sec)),
                  pl.BlockSpec((CONV_PAD, wq), lambda i: (0, 0)),
                  pl.BlockSpec((ts, wq), lambda i: (i, 0)),
                  pl.BlockSpec((CONV_PAD, wq), lambda i: (jnp.minimum((i + 1) * nb, last_blk), 0)),
                  pl.BlockSpec(memory_space=pl.ANY)],
        out_specs=[pl.BlockSpec((ts, wq), lambda i: (i, sec)), pl.BlockSpec((CONV_PAD, wq), lambda i: (0, 0))],
        out_shape=[jax.ShapeDtypeStruct(dproj.shape, bf16), jax.ShapeDtypeStruct((CONV_PAD, wq), f32)],
        input_output_aliases={6: 0}, compiler_params=_params(1),
    )(proj, proj, proj, conv_w8, dxc, dxc, dproj)


def _dn_chunk(state, xc, z, ba, alog_row, dtb_row, norm_row):
    c, nh = DN_CHUNK, DN_HEADS
    n = c * nh
    row = lax.broadcasted_iota(jnp.int32, (n, n), 0)
    col = lax.broadcasted_iota(jnp.int32, (n, n), 1)
    same_head = (row // c) == (col // c)
    tril, strict = same_head & (row >= col), same_head & (row > col)
    tril_c = lax.broadcasted_iota(jnp.int32, (c, c), 0) >= lax.broadcasted_iota(jnp.int32, (c, c), 1)
    beta_all = _sigmoid(ba)
    g_all = -jnp.exp(alog_row) * jax.nn.softplus(ba + dtb_row)
    gc_all = hdot(tril_c.astype(f32), g_all)
    gc_t = gc_all.T

    def stack(piece):
        return jnp.concatenate([piece(h) for h in range(nh)], axis=0)

    q = stack(lambda h: xc[:, h * DN_HD:(h + 1) * DN_HD])
    k = stack(lambda h: xc[:, MIX + h * DN_HD:MIX + (h + 1) * DN_HD])
    v = stack(lambda h: xc[:, 2 * MIX + h * DN_HD:2 * MIX + (h + 1) * DN_HD])
    zs = stack(lambda h: z[:, h * DN_HD:(h + 1) * DN_HD])
    q = q * lax.rsqrt(jnp.sum(q * q, axis=-1, keepdims=True) + NORM_EPS) * (DN_HD ** -0.5)
    k = k * lax.rsqrt(jnp.sum(k * k, axis=-1, keepdims=True) + NORM_EPS)
    beta = stack(lambda h: _lane_col(beta_all, h))
    g_cols = [_lane_col(gc_all, nh + h) for h in range(nh)]
    g_col = jnp.concatenate(g_cols, axis=0)
    g_row = jnp.concatenate([gc_t[nh + h:nh + h + 1, :] for h in range(nh)], axis=1)
    g_last = stack(lambda h: jnp.broadcast_to(g_cols[h][c - 1:c, :], (c, 1)))
    decay = jnp.where(tril, jnp.exp(jnp.where(tril, g_col - g_row, 0.0)), 0.0)
    kb = k * beta
    low = jnp.where(strict, bdot(kb, k, 1, 1) * decay, 0.0)
    t_inv = tri_inv(low, c)
    e_gc = jnp.exp(g_col)
    uw = hdot(t_inv, jnp.concatenate([v * beta, kb * e_gc], axis=1))
    u, w = uw[:, :DN_HD], uw[:, DN_HD:]
    attn = bdot(q, k, 1, 1) * decay
    own = (lax.broadcasted_iota(jnp.int32, (n, nh * DN_HD), 1) // DN_HD
           == lax.broadcasted_iota(jnp.int32, (n, nh * DN_HD), 0) // c)

    def spread(a):
        return jnp.where(own, jnp.concatenate([a] * nh, axis=1), 0.0)

    v_new = u - bdot(spread(w), state, 1, 0)
    o = bdot(spread(q * e_gc), state, 1, 0) + bdot(attn, v_new, 1, 0)
    keep = stack(lambda h: jnp.broadcast_to(jnp.exp(g_cols[h][c - 1:c, :]), (DN_HD, 1)))
    new_state = state * keep + bdot(spread(k * jnp.exp(g_last - g_col)), v_new, 0, 0)
    out = _rms(o, norm_row) * (zs * _sigmoid(zs))
    return new_state, jnp.concatenate([out[h * c:(h + 1) * c] for h in range(nh)], axis=1)


DN_STEP = 2 * DN_CHUNK


def _dn_step(state, xc, z, ba, alog_row, dtb_row, norm_row):
    outs = []
    for c in range(DN_STEP // DN_CHUNK):
        rows = slice(c * DN_CHUNK, (c + 1) * DN_CHUNK)
        state, out = _dn_chunk(state, xc[rows], z[rows], ba[rows], alog_row, dtb_row, norm_row)
        outs.append(out)
    return state, jnp.concatenate(outs, axis=0)


def _dn_specs(ts, order):
    zb = SEC_Z // MIX
    return [pl.BlockSpec((ts, 3 * MIX), lambda i: (order(i), 0)),
            pl.BlockSpec((ts, MIX), lambda i: (order(i), zb)),
            pl.BlockSpec((ts, LANES), lambda i: (order(i), 0)),
            pl.BlockSpec((1, LANES), lambda i: (0, 0)), pl.BlockSpec((1, LANES), lambda i: (0, 0)),
            pl.BlockSpec((1, LANES), lambda i: (0, 0))]


def dn_fwd(xc, proj, p_ba, alog_row, dtb_row, norm_row, *, side=None, name):
    s = xc.shape[0]
    ts = _tile(s, 512)
    n_step = ts // DN_STEP

    def body(xc_ref, z_ref, ba_ref, al_ref, dt_ref, nr_ref, o_ref, st_ref, state_ref):
        @pl.when(pl.program_id(0) == 0)
        def _():
            state_ref[...] = jnp.zeros_like(state_ref)

        def step(c, carry):
            rows = pl.ds(pl.multiple_of(c * DN_STEP, DN_STEP), DN_STEP)
            st_ref[c] = state_ref[...]
            new_state, out = _dn_step(state_ref[...], xc_ref[rows, :], z_ref[rows, :], ba_ref[rows, :],
                                      al_ref[...], dt_ref[...], nr_ref[...])
            state_ref[...] = new_state
            o_ref[rows, :] = out.astype(bf16)
            return carry
        lax.fori_loop(0, n_step, step, 0)

    return host_call(
        side, body, name=name, grid=(s // ts,), in_specs=_dn_specs(ts, lambda i: i),
        out_specs=[pl.BlockSpec((ts, MIX), lambda i: (i, 0)),
                   pl.BlockSpec((n_step, DN_HEADS * DN_HD, DN_HD), lambda i: (i, 0, 0))],
        out_shape=[jax.ShapeDtypeStruct((s, MIX), bf16),
                   jax.ShapeDtypeStruct((s // DN_STEP, DN_HEADS * DN_HD, DN_HD), f32)],
        scratch_shapes=[pltpu.VMEM((DN_HEADS * DN_HD, DN_HD), f32)], aliases={},
        args=(xc, proj, p_ba, alog_row, dtb_row, norm_row))


def dn_bwd(xc, proj, p_ba, alog_row, dtb_row, norm_row, saved, d_out, dproj, *, side=None, name):
    s = xc.shape[0]
    ts = _tile(s, 512)
    n_step = ts // DN_STEP
    nt = s // ts

    def body(xc_ref, z_ref, ba_ref, al_ref, dt_ref, nr_ref, st_ref, do_ref, _,
             dz_ref, dxc_ref, dba_ref, dal_ref, ddt_ref, dnr_ref, dstate_ref):
        @pl.when(pl.program_id(0) == 0)
        def _():
            dstate_ref[...] = jnp.zeros_like(dstate_ref)
            dal_ref[...] = jnp.zeros_like(dal_ref)
            ddt_ref[...] = jnp.zeros_like(ddt_ref)
            dnr_ref[...] = jnp.zeros_like(dnr_ref)

        def step(it, carry):
            c = n_step - 1 - it
            rows = pl.ds(pl.multiple_of(c * DN_STEP, DN_STEP), DN_STEP)
            _, vjp = jax.vjp(_dn_step, st_ref[c], xc_ref[rows, :], z_ref[rows, :], ba_ref[rows, :],
                             al_ref[...], dt_ref[...], nr_ref[...])
            d_in, dxc, dz, dba, dal, ddt, dnr = vjp((dstate_ref[...], do_ref[rows, :]))
            dstate_ref[...] = d_in
            dxc_ref[rows, :] = dxc
            dz_ref[rows, :] = dz.astype(bf16)
            dba_ref[rows, :] = dba.astype(bf16)
            dal_ref[...] += dal
            ddt_ref[...] += ddt
            dnr_ref[...] += dnr
            return carry
        lax.fori_loop(0, n_step, step, 0)

    def rev(i):
        return nt - 1 - i

    specs = _dn_specs(ts, rev)
    vec = pl.BlockSpec((1, LANES), lambda i: (0, 0))
    return host_call(
        side, body, name=name, grid=(nt,),
        in_specs=specs + [pl.BlockSpec((n_step, DN_HEADS * DN_HD, DN_HD), lambda i: (rev(i), 0, 0)),
                          pl.BlockSpec((ts, MIX), lambda i: (rev(i), 0)), pl.BlockSpec(memory_space=pl.ANY)],
        out_specs=[specs[1], specs[0], specs[2], vec, vec, vec],
        out_shape=[jax.ShapeDtypeStruct(dproj.shape, bf16), jax.ShapeDtypeStruct((s, 3 * MIX), f32),
                   jax.ShapeDtypeStruct((s, LANES), bf16)] + [jax.ShapeDtypeStruct((1, LANES), f32)] * 3,
        scratch_shapes=[pltpu.VMEM((DN_HEADS * DN_HD, DN_HD), f32)], aliases={8: 0},
        args=(xc, proj, p_ba, alog_row, dtb_row, norm_row, saved, d_out, dproj))


def _merge_in_specs(ts, d):
    row = pl.BlockSpec((ts, MIX), lambda i: (i, 0))
    return [row, row, row, pl.BlockSpec((ts, 3 * d), lambda i: (i, SEC_G // (3 * d))),
            pl.BlockSpec((3, MIX, d), lambda i: (0, 0, 0))]


def merge_fwd(out_a, out_b, out_c, proj, w_branch, *, name):
    s, d = out_a.shape[0], w_branch.shape[2]
    ts = _tile(s, 256, 16)

    def body(a_ref, b_ref, c_ref, g_ref, w_ref, o_ref):
        acc = jnp.zeros((ts, d), f32)
        for n, r in enumerate((a_ref, b_ref, c_ref)):
            acc = acc + _sigmoid(g_ref[:, n * d:(n + 1) * d]) * _dg(r[...], w_ref[n], 1, 0)
        o_ref[...] = acc.astype(bf16)

    return pl.pallas_call(
        body, name=name, grid=(s // ts,), in_specs=_merge_in_specs(ts, d),
        out_specs=pl.BlockSpec((ts, d), lambda i: (i, 0)), out_shape=jax.ShapeDtypeStruct((s, d), bf16),
        compiler_params=_params(1),
    )(out_a, out_b, out_c, proj, w_branch)


def merge_bwd(out_a, out_b, out_c, proj, w_branch, d_merged, dproj, *, name):
    s, d = out_a.shape[0], w_branch.shape[2]
    ts = _tile(s, 256, 16)

    def body(a_ref, b_ref, c_ref, g_ref, w_ref, dm_ref, _, dg_ref, da_ref, db_ref, dc_ref, dw_ref):
        @pl.when(pl.program_id(0) == 0)
        def _():
            dw_ref[...] = jnp.zeros_like(dw_ref)

        dm = dm_ref[...]
        for n, (r, dr) in enumerate(((a_ref, da_ref), (b_ref, db_ref), (c_ref, dc_ref))):
            gate = _sigmoid(g_ref[:, n * d:(n + 1) * d])
            branch = _dg(r[...], w_ref[n], 1, 0)
            dg_ref[:, n * d:(n + 1) * d] = (dm * branch * gate * (1.0 - gate)).astype(bf16)
            d_branch = dm * gate
            dr[...] = _dg(d_branch, w_ref[n], 1, 1)
            dw_ref[n] += _dg(r[...], d_branch, 0, 0)

    specs = _merge_in_specs(ts, d)
    row_f = pl.BlockSpec((ts, MIX), lambda i: (i, 0))
    return pl.pallas_call(
        body, name=name, grid=(s // ts,),
        in_specs=specs + [pl.BlockSpec((ts, d), lambda i: (i, 0)), pl.BlockSpec(memory_space=pl.ANY)],
        out_specs=[specs[3], row_f, row_f, row_f, specs[4]],
        out_shape=[jax.ShapeDtypeStruct(dproj.shape, bf16)] + [jax.ShapeDtypeStruct((s, MIX), f32)] * 3
        + [jax.ShapeDtypeStruct(w_branch.shape, f32)],
        input_output_aliases={6: 0}, compiler_params=_params(1),
    )(out_a, out_b, out_c, proj, w_branch, d_merged, dproj)


def swiglu_fwd(gu, *, name):
    g2, s, w = gu.shape
    ng = g2 // 2
    ts = _tile(s, 1024, 16)

    def body(g_ref, u_ref, o_ref):
        g = g_ref[...].astype(f32)
        o_ref[...] = (g * _sigmoid(g) * u_ref[...].astype(f32)).astype(bf16)

    return pl.pallas_call(
        body, name=name, grid=(s // ts, ng),
        in_specs=[pl.BlockSpec((None, ts, w), lambda i, j: (j, i, 0)),
                  pl.BlockSpec((None, ts, w), lambda i, j: (ng + j, i, 0))],
        out_specs=pl.BlockSpec((None, ts, w), lambda i, j: (j, i, 0)), out_shape=jax.ShapeDtypeStruct((ng, s, w), bf16),
        compiler_params=_params(2),
    )(gu, gu)


def swiglu_bwd(gu, d_act, *, name):
    g2, s, w = gu.shape
    ng = g2 // 2
    ts = _tile(s, 1024, 16)

    def body(g_ref, u_ref, d_ref, o_ref):
        g, d = g_ref[...].astype(f32), d_ref[...]
        sig = _sigmoid(g)

        @pl.when(pl.program_id(1) < ng)
        def _():
            o_ref[...] = (d * u_ref[...].astype(f32) * sig * (1.0 + g * (1.0 - sig))).astype(bf16)

        @pl.when(pl.program_id(1) >= ng)
        def _():
            o_ref[...] = (d * g * sig).astype(bf16)

    return pl.pallas_call(
        body, name=name, grid=(s // ts, g2),
        in_specs=[pl.BlockSpec((None, ts, w), lambda i, j: (j % ng, i, 0)),
                  pl.BlockSpec((None, ts, w), lambda i, j: (ng + j % ng, i, 0)),
                  pl.BlockSpec((None, ts, w), lambda i, j: (j % ng, i, 0))],
        out_specs=pl.BlockSpec((None, ts, w), lambda i, j: (j, i, 0)), out_shape=jax.ShapeDtypeStruct((g2, s, w), bf16),
        compiler_params=_params(2),
    )(gu, gu, d_act)


def adamw(w, m, v, g_parts, *, name):
    n_layers = len(g_parts)
    n_parts, r, cols = g_parts[0].shape
    lanes = -(-cols // LANES) * LANES
    tr = _tile(r, max(16, (128 * 1024) // lanes), 16)
    nr = r // tr

    def body(w_ref, m_ref, v_ref, *rest):
        gp_refs, (g_ref, d_ref, nm_ref, nv_ref) = rest[:n_layers], rest[n_layers:]
        layer = pl.program_id(0)
        g = jnp.zeros((tr, cols), f32)
        for l, gp_ref in enumerate(gp_refs):
            g_l = gp_ref[0].astype(f32)
            for k in range(1, n_parts):
                g_l = g_l + gp_ref[k].astype(f32)
            g = jnp.where(layer == l, g_l, g)
        nm = ADAM_B1 * m_ref[...] + (1.0 - ADAM_B1) * g
        nv = ADAM_B2 * v_ref[...] + (1.0 - ADAM_B2) * jnp.square(g)
        m_hat = nm / (1.0 - ADAM_B1 ** ADAM_STEP)
        v_hat = nv / (1.0 - ADAM_B2 ** ADAM_STEP)
        g_ref[...] = g
        d_ref[...] = -ADAM_LR * (m_hat / (jnp.sqrt(v_hat) + ADAM_EPS) + ADAM_WD * w_ref[...])
        nm_ref[...] = nm
        nv_ref[...] = nv

    row = pl.BlockSpec((tr, cols), lambda l, i: (l * nr + i, 0))

    def parts_spec(own):
        return pl.BlockSpec((n_parts, tr, cols),
                            lambda l, i: (0, jnp.where(l == own, i, jnp.where(l < own, 0, nr - 1)), 0))

    return pl.pallas_call(
        body, name=name, grid=(n_layers, nr), in_specs=[row, row, row] + [parts_spec(l) for l in range(n_layers)],
        out_specs=[row] * 4, out_shape=[jax.ShapeDtypeStruct(w.shape, f32)] * 4, compiler_params=_params(2),
    )(w, m, v, *g_parts)


def _mesh_pos():
    return lax.axis_index("x"), lax.axis_index("y"), lax.axis_index("c")


def _dev_index(p):
    return 4 * p[0] + 2 * p[1] + p[2]


class Exchange:
    def __init__(self, kind, arrays):
        self.kind, self.arrays, self.n = kind, list(arrays), len(arrays)
        self.specs = [pl.BlockSpec(memory_space=pl.ANY)] * self.n
        self.out_shapes = [jax.ShapeDtypeStruct(((N_DEV,) if kind == "gather" else ()) + a.shape, a.dtype)
                           for a in self.arrays]
        self.scratch = [pltpu.SemaphoreType.DMA((self.n, N_DEV - 1)), pltpu.SemaphoreType.DMA((self.n, N_DEV - 1)),
                        pltpu.SemaphoreType.DMA((self.n,))]

    def _copies(self, in_refs, out_refs, sems, with_arrivals):
        send_sems, recv_sems, local_sems = sems
        x, y, c = _mesh_pos()
        mine = _dev_index((x, y, c))

        def src(a, slab):
            return in_refs[a] if self.kind == "gather" else in_refs[a].at[slab]

        local = [pltpu.make_async_copy(src(a, mine), out_refs[a].at[mine], local_sems.at[a]) for a in range(self.n)]
        sends, arrivals = [], []
        for k in range(1, N_DEV):
            peer = (1 - x if k & 4 else x, 1 - y if k & 2 else y, 1 - c if k & 1 else c)
            theirs = _dev_index(peer)
            for a in range(self.n):
                to = dict(send_sem=send_sems.at[a, k - 1], recv_sem=recv_sems.at[a, k - 1], device_id=peer,
                          device_id_type=pl.DeviceIdType.MESH)
                sends.append(pltpu.make_async_remote_copy(src_ref=src(a, theirs), dst_ref=out_refs[a].at[mine], **to))
                if with_arrivals:
                    arrivals.append(pltpu.make_async_remote_copy(src_ref=src(a, theirs),
                                                                 dst_ref=out_refs[a].at[theirs], **to))
        return local, sends, arrivals

    def start(self, in_refs, out_refs, sems):
        local, sends, _ = self._copies(in_refs, out_refs, sems, False)
        for cp in local + sends:
            cp.start()

    def wait(self, in_refs, out_refs, sems):
        local, sends, arrivals = self._copies(in_refs, out_refs, sems, True)
        for cp in arrivals:
            cp.wait_recv()
        for cp in sends:
            cp.wait_send()
        for cp in local:
            cp.wait()

    def run_around(self, grid, in_refs, out_refs, sems, *, before):
        at = None
        for axis, size in enumerate(grid):
            hit = pl.program_id(axis) == (0 if before else size - 1)
            at = hit if at is None else at & hit

        @pl.when(at)
        def _():
            (self.start if before else self.wait)(in_refs, out_refs, sems)


def host_call(side, body, *, name, grid, in_specs, out_specs, out_shape, scratch_shapes, args, aliases):
    n_in, n_out = len(in_specs), len(out_specs)
    if side is None:
        kernel_body = body
    else:
        n = side.n
        in_specs, args = in_specs + side.specs, tuple(args) + tuple(side.arrays)
        out_specs, out_shape = out_specs + side.specs, out_shape + side.out_shapes
        scratch_shapes = scratch_shapes + side.scratch

        def kernel_body(*refs):
            ins, side_in = refs[:n_in], refs[n_in:n_in + n]
            outs, side_out = refs[n_in + n:n_in + n + n_out], refs[n_in + n + n_out:n_in + 2 * n + n_out]
            scratch, sems = refs[n_in + 2 * n + n_out:-3], refs[-3:]
            side.run_around(grid, side_in, side_out, sems, before=True)
            body(*ins, *outs, *scratch)
            side.run_around(grid, side_in, side_out, sems, before=False)

    outs = pl.pallas_call(
        kernel_body, name=name, grid=grid, in_specs=in_specs, out_specs=out_specs, out_shape=out_shape,
        scratch_shapes=scratch_shapes, input_output_aliases=aliases, compiler_params=_params(len(grid)),
    )(*args)
    return outs if side is None else (outs[:n_out], outs[n_out:])


def all_gather(blocks, *, name):
    n = len(blocks)
    any_spec = pl.BlockSpec(memory_space=pl.ANY)

    def body(*refs):
        ins, outs = refs[:n], refs[n:2 * n]
        send_sems, recv_sems, local_sems = refs[2 * n:]
        x, y, c = _mesh_pos()
        me, sibling = (x, y, c), (x, y, 1 - c)
        chips = [(1 - x, y), (x, 1 - y), (1 - x, 1 - y)]

        def copy(a, k, block, to, src=None):
            dst = outs[a].at[_dev_index(block)]
            return pltpu.make_async_remote_copy(
                src_ref=dst if src is None else src, dst_ref=dst, send_sem=send_sems.at[a, k],
                recv_sem=recv_sems.at[a, k], device_id=to, device_id_type=pl.DeviceIdType.MESH)

        mine = [pltpu.make_async_copy(ins[a], outs[a].at[_dev_index(me)], local_sems.at[a]) for a in range(n)]
        for cp in mine:
            cp.start()
        first = []
        for a in range(n):
            first.append(copy(a, 0, me, sibling, src=ins[a]))
            first += [copy(a, 1 + j, me, (*chip, c), src=ins[a]) for j, chip in enumerate(chips)]
        for cp in first:
            cp.start()
        passed = []
        for j, chip in enumerate(chips):
            for a in range(n):
                copy(a, 1 + j, (*chip, c), me).wait_recv()
                fwd = copy(a, 4 + j, (*chip, c), sibling)
                fwd.start()
                passed.append(fwd)
        for a in range(n):
            copy(a, 0, sibling, me).wait_recv()
            for j, chip in enumerate(chips):
                copy(a, 4 + j, (*chip, 1 - c), me).wait_recv()
        for cp in first + passed:
            cp.wait_send()
        for cp in mine:
            cp.wait()

    return pl.pallas_call(
        body, name=name, in_specs=[any_spec] * n, out_specs=[any_spec] * n,
        out_shape=[jax.ShapeDtypeStruct((N_DEV,) + b.shape, b.dtype) for b in blocks],
        scratch_shapes=[pltpu.SemaphoreType.DMA((n, 7)), pltpu.SemaphoreType.DMA((n, 7)),
                        pltpu.SemaphoreType.DMA((n,))],
    )(*blocks)


def exchange(jobs, *, name):
    total = sum(j.n for j in jobs)

    def body(*refs):
        ins, outs, sems = refs[:total], refs[total:2 * total], refs[2 * total:]
        pieces, off = [], 0
        for i, j in enumerate(jobs):
            pieces.append((ins[off:off + j.n], outs[off:off + j.n], sems[3 * i:3 * i + 3]))
            off += j.n
        for j, piece in zip(jobs, pieces):
            j.start(*piece)
        for j, piece in zip(jobs, pieces):
            j.wait(*piece)

    outs = pl.pallas_call(
        body, name=name, in_specs=[s for j in jobs for s in j.specs], out_specs=[s for j in jobs for s in j.specs],
        out_shape=[s for j in jobs for s in j.out_shapes], scratch_shapes=[s for j in jobs for s in j.scratch],
    )(*[a for j in jobs for a in j.arrays])
    split, off = [], 0
    for j in jobs:
        split.append(outs[off:off + j.n])
        off += j.n
    return split


def _rows128(arr):
    flat = arr.reshape(-1)
    rows = -(-flat.shape[0] // (8 * LANES)) * 8
    return jnp.pad(flat, (0, rows * LANES - flat.shape[0])).reshape(rows, LANES)


def _pad_lanes(row, width=LANES, at=0):
    return jnp.pad(row, (at, width - at - row.shape[0])).reshape(1, width)


def _w_in_sections(got):
    d = got.shape[1]
    wi = jnp.transpose(got, (1, 0, 2)).reshape(d, -1)
    w_main = jnp.concatenate([wi[:, :C_B], wi[:, C_Z:C_BA], wi[:, C_QKV:C_Z], wi[:, C_G:], wi[:, C_B:C_QKV]], axis=1)
    return w_main, jnp.pad(wi[:, C_BA:C_G], ((0, 0), (0, LANES - (C_G - C_BA))))


def _w_in_parts(gw_main, gw_ba):
    d = gw_main.shape[0]
    full = jnp.concatenate([gw_main[:, SEC_A:SEC_Z], gw_main[:, SEC_B:], gw_main[:, SEC_QKV:SEC_G],
                            gw_main[:, SEC_Z:SEC_QKV], gw_ba[:, :C_G - C_BA], gw_main[:, SEC_G:SEC_B]], axis=1)
    return jnp.transpose(full.reshape(d, N_DEV, -1), (1, 0, 2))


def kernel(x, positions, attn_norm, w_in, sgu_ln_g, sgu_ln_b, sgu_w, sgu_b, attn_sinks, dn_conv_w, dn_a_log, dn_dt_bias, dn_norm, w_branch, w_out, ffn_norm, w_gate_up, w_down, final_norm, loss_target, m_attn_norm, m_w_in, m_sgu_ln_g, m_sgu_ln_b, m_sgu_w, m_sgu_b, m_attn_sinks, m_dn_conv_w, m_dn_a_log, m_dn_dt_bias, m_dn_norm, m_w_branch, m_w_out, m_ffn_norm, m_w_gate_up, m_w_down, m_final_norm, v_attn_norm, v_w_in, v_sgu_ln_g, v_sgu_ln_b, v_sgu_w, v_sgu_b, v_attn_sinks, v_dn_conv_w, v_dn_a_log, v_dn_dt_bias, v_dn_norm, v_w_branch, v_w_out, v_ffn_norm, v_w_gate_up, v_w_down, v_final_norm):
    given = dict(locals())
    depth, d_model = attn_norm.shape
    s = x.shape[1]
    x2 = x.reshape(s, d_model)
    target = loss_target.reshape(s, d_model)
    posf = positions.reshape(s, 1).astype(f32)
    inv_freq = ROPE_THETA ** (-jnp.arange(0, ROPE_DIM, 2, dtype=f32) / ROPE_DIM)
    inv_head = jnp.concatenate([inv_freq, inv_freq, jnp.zeros((SWA_HD - ROPE_DIM,), f32)])
    inv_q = jnp.tile(inv_head, SWA_HEADS).reshape(1, MIX)
    inv_k = jnp.tile(inv_head, SWA_KV).reshape(1, LANES)

    assert depth == 2, depth
    gathered = dict(zip([("w_in", 0), ("dn_conv_w", 0), ("dn_conv_w", 1)], all_gather(
        [w_in[0].astype(bf16), dn_conv_w[0], dn_conv_w[1]], name="gather_first")))
    riders = {"l0_in_proj": [("w_branch", 0), ("w_out", 0)],
              "l0_deltanet": [("w_gate_up", 0), ("w_down", 0), ("w_in", 1)],
              "l1_deltanet": [("w_branch", 1), ("w_out", 1), ("w_gate_up", 1), ("w_down", 1)]}

    def gathering(host, call, *args, **kw):
        if host not in riders:
            return call(*args, name=host, **kw)
        side = Exchange("gather", [given[n][l].astype(bf16) for n, l in riders[host]])
        out, got = call(*args, side=side, name=host, **kw)
        gathered.update(zip(riders[host], got))
        return out

    layers, saved = [], []
    h_in = x2
    for l in range(depth):
        t = f"l{l}_"
        w_main, w_ba = _w_in_sections(gathered["w_in", l])
        conv_full = jnp.transpose(gathered["dn_conv_w", l], (1, 0, 2)).reshape(DN_CONV, -1)
        p = dict(
            w_main=w_main, w_ba=w_ba, conv_w8=jnp.pad(conv_full, ((0, CONV_PAD - DN_CONV), (0, 0))),
            attn_norm=attn_norm[l].reshape(1, -1), ffn_norm=ffn_norm[l].reshape(1, -1),
            ln_g=sgu_ln_g[l].reshape(1, -1), ln_b=sgu_ln_b[l].reshape(1, -1), sgu_w=sgu_w[l], sgu_bt=sgu_b[l].T,
            sink_row=_pad_lanes(attn_sinks[l]), alog_row=_pad_lanes(dn_a_log[l], at=DN_HEADS),
            dtb_row=_pad_lanes(dn_dt_bias[l], at=DN_HEADS), norm_row=dn_norm[l].reshape(1, -1))
        layers.append(p)
        h = rmsnorm_fwd(h_in, p["attn_norm"], name=t + "attn_norm")
        proj = gathering(t + "in_proj", matmul, h, p["w_main"], "nn", f32)
        p_ba = matmul(h, p["w_ba"], "nn", f32, name=t + "in_proj_ba")
        out_a = sgu_fwd(proj, p["ln_g"], p["ln_b"], p["sgu_w"], p["sgu_bt"], name=t + "sgu")
        out_b = swa_fwd(proj, p["sink_row"], posf, inv_q, inv_k, name=t + "swa")
        xc = conv_fwd(proj, p["conv_w8"], name=t + "dn_conv")
        out_c, states = gathering(t + "deltanet", dn_fwd, xc, proj, p_ba, p["alog_row"], p["dtb_row"], p["norm_row"])
        p.update(w_branch=jnp.transpose(gathered["w_branch", l], (1, 2, 0, 3)).reshape(3, MIX, d_model),
                 w_out=gathered["w_out", l].reshape(d_model, d_model),
                 w_gu=gathered["w_gate_up", l],
                 w_down=gathered["w_down", l].reshape(N_DEV // 2, -1, d_model))
        merged = merge_fwd(out_a, out_b, out_c, proj, p["w_branch"], name=t + "merge")
        x_mid = matmul(merged, p["w_out"], "nn", f32, residual=h_in, name=t + "out_proj")
        h2 = rmsnorm_fwd(x_mid, p["ffn_norm"], name=t + "ffn_norm")
        gu = matmul(h2, p["w_gu"], "nn", bf16, group="n", name=t + "gate_up")
        act = swiglu_fwd(gu, name=t + "swiglu")
        x_out = matmul(act, p["w_down"], "nn", f32, residual=x_mid, group="k", name=t + "down")
        saved.append(dict(x_in=h_in, h=h, proj=proj, p_ba=p_ba, out_a=out_a, out_b=out_b, out_c=out_c, xc=xc,
                          states=states, merged=merged, x_mid=x_mid, h2=h2, gu=gu, act=act))
        h_in = x_out

    dx, d_final_norm, loss_row = loss_head(h_in, final_norm.reshape(1, -1), target, name="loss_head")
    loss = lax.psum(loss_row[0, 0], MESH_AXES)

    shard_names = ["w_in", "dn_conv_w", "w_branch", "w_out", "w_gate_up", "w_down"]
    rep_names = ["attn_norm", "sgu_ln_g", "sgu_ln_b", "sgu_w", "sgu_b", "attn_sinks", "dn_a_log", "dn_dt_bias",
                 "dn_norm", "ffn_norm"]
    parts, received, per_layer = {}, {}, []
    senders = {"l0_b_swa": [("w_gate_up", 1), ("w_down", 1), ("w_out", 1), ("w_branch", 1)],
               "l0_b_deltanet": [("w_in", 1), ("w_gate_up", 0), ("w_down", 0), ("w_out", 0), ("w_branch", 0)]}

    def scattering(host, call, *args):
        if host not in senders:
            return call(*args, name=host)
        out, got = call(*args, side=Exchange("scatter", [parts[key] for key in senders[host]]), name=host)
        received.update(zip(senders[host], got))
        return out

    for l in reversed(range(depth)):
        p, sv, t = layers[l], saved[l], f"l{l}_b_"
        d_act = matmul(dx, p["w_down"], "nt", f32, group="n", name=t + "down_dx")
        gw_down = matmul(sv["act"], dx, "tn", bf16, group="m", name=t + "down_dw")
        d_gu = swiglu_bwd(sv["gu"], d_act, name=t + "swiglu")
        gw_gu = matmul(sv["h2"], d_gu, "tn", bf16, group="n", name=t + "gate_up_dw")
        d_h2 = matmul(d_gu, p["w_gu"], "nt", f32, group="k", name=t + "gate_up_dx")
        dx_mid, g_ffn = rmsnorm_bwd(sv["x_mid"], p["ffn_norm"], d_h2, dx, name=t + "ffn_norm")
        d_merged = matmul(dx_mid, p["w_out"], "nt", f32, name=t + "out_proj_dx")
        gw_out = matmul(sv["merged"], dx_mid, "tn", bf16, name=t + "out_proj_dw")
        dproj = lax.empty((s, W_MAIN), bf16)
        dproj, d_a, d_b, d_c, gw_branch = merge_bwd(sv["out_a"], sv["out_b"], sv["out_c"], sv["proj"], p["w_branch"],
                                                   d_merged, dproj, name=t + "merge")
        parts.update({("w_gate_up", l): gw_gu, ("w_down", l): gw_down.reshape(N_DEV, -1, d_model),
                      ("w_out", l): gw_out.reshape(N_DEV, -1, d_model),
                      ("w_branch", l): jnp.transpose(gw_branch.reshape(3, MIX, N_DEV, -1), (2, 0, 1, 3)).astype(bf16)})
        dproj, g_ln_g, g_ln_b, g_sgu_w, g_sgu_bt = sgu_bwd(sv["proj"], p["ln_g"], p["ln_b"], p["sgu_w"], p["sgu_bt"],
                                                         d_a, dproj, name=t + "sgu")
        dproj, g_sink = scattering(t + "swa", swa_bwd, sv["proj"], p["sink_row"], posf, inv_q, inv_k, d_b, dproj)
        dproj, dxc, dba, g_alog, g_dtb, g_dnorm = scattering(
            t + "deltanet", dn_bwd, sv["xc"], sv["proj"], sv["p_ba"], p["alog_row"], p["dtb_row"], p["norm_row"],
            sv["states"], d_c, dproj)
        dproj, g_conv8 = conv_bwd(sv["proj"], p["conv_w8"], dxc, dproj, name=t + "dn_conv")
        gw_main = matmul(sv["h"], dproj, "tn", bf16, name=t + "in_proj_dw")
        gw_ba = matmul(sv["h"], dba, "tn", bf16, name=t + "in_proj_ba_dw")
        parts["w_in", l] = _w_in_parts(gw_main, gw_ba)
        d_h = matmul(dproj, p["w_main"], "nt", f32, tk=768, name=t + "in_proj_dx")
        d_h = matmul(dba, p["w_ba"], "nt", f32, residual=d_h, name=t + "in_proj_ba_dx")
        dx, g_attn = rmsnorm_bwd(sv["x_in"], p["attn_norm"], d_h, dx_mid, name=t + "attn_norm")
        per_layer.append(dict(
            dn_conv_w=jnp.transpose(g_conv8[:DN_CONV].reshape(DN_CONV, N_DEV, -1), (1, 0, 2)),
            attn_norm=g_attn, sgu_ln_g=g_ln_g, sgu_ln_b=g_ln_b, sgu_w=g_sgu_w, sgu_b=g_sgu_bt.T,
            attn_sinks=g_sink[0, :SWA_HEADS], dn_a_log=g_alog[0, DN_HEADS:2 * DN_HEADS],
            dn_dt_bias=g_dtb[0, DN_HEADS:2 * DN_HEADS], dn_norm=g_dnorm, ffn_norm=g_ffn))
    per_layer.reverse()

    conv_parts = jnp.concatenate([pp["dn_conv_w"] for pp in per_layer], axis=1)
    rep_grads = {n: jnp.stack([pp[n].reshape(given[n].shape[1:]) for pp in per_layer]) for n in rep_names}
    rep_grads["final_norm"] = d_final_norm[0]
    rep_names = rep_names + ["final_norm"]
    rep_rows = [_rows128(given[n]).shape[0] for n in rep_names]
    pad_rows = -sum(rep_rows) % 16

    def small_rows(values):
        pieces = [_rows128(values[n]) for n in rep_names]
        return jnp.concatenate(pieces + [jnp.zeros((pad_rows, LANES), f32)], axis=0)

    (got_w_in, got_conv), (small_all,) = exchange(
        [Exchange("scatter", [parts["w_in", 0], conv_parts]), Exchange("gather", [small_rows(rep_grads)])],
        name="exchange_last")
    received["w_in", 0] = got_w_in

    results = [{}, {}, {}, {}]
    for n in shard_names:
        shp = given[n].shape
        two = (-1, shp[-1])
        by_layer = [got_conv] if n == "dn_conv_w" else [received[n, l].reshape(N_DEV, -1, shp[-1]) for l in range(depth)]
        outs = adamw(given[n].reshape(two), given["m_" + n].reshape(two), given["v_" + n].reshape(two), by_layer,
                     name="adamw_" + n)
        for res, val in zip(results, outs):
            res[n] = val.reshape(shp)
    outs = adamw(small_rows(given), small_rows({n: given["m_" + n] for n in rep_names}),
                 small_rows({n: given["v_" + n] for n in rep_names}), [small_all], name="adamw_replicated")
    for res, val in zip(results, outs):
        row = 0
        for n, nr in zip(rep_names, rep_rows):
            res[n] = val[row:row + nr].reshape(-1)[:given[n].size].reshape(given[n].shape)
            row += nr
    order = ["attn_norm", "w_in", "sgu_ln_g", "sgu_ln_b", "sgu_w", "sgu_b", "attn_sinks", "dn_conv_w", "dn_a_log",
             "dn_dt_bias", "dn_norm", "w_branch", "w_out", "ffn_norm", "w_gate_up", "w_down", "final_norm"]
    return (loss, dx.reshape(x.shape), *[res[n] for res in results for n in order])
```

```python
import functools

import jax
import jax.numpy as jnp
from jax import lax
from jax.experimental import pallas as pl
from jax.experimental.pallas import tpu as pltpu

f32 = jnp.float32
bf16 = jnp.bfloat16

N_DEV = 8
MESH_AXES = ("x", "y", "c")
NORM_EPS = 1e-6
MIX = 512
SGU_GROUPS, SGU_CHUNK = 4, 128
SWA_HEADS, SWA_KV, SWA_HD, WINDOW = 8, 2, 64, 128
ROPE_THETA, ROPE_DIM = 500000.0, 16
DN_HEADS, DN_HD, DN_CONV, DN_CHUNK = 4, 128, 4, 64
ADAM_LR, ADAM_B1, ADAM_B2, ADAM_EPS, ADAM_WD, ADAM_STEP = 0.001, 0.9, 0.999, 1e-08, 0.01, 10

LANES = 128
VMEM_LIMIT = 56 * 1024 * 1024

SEC_A, SEC_Z, SEC_QKV, SEC_G, SEC_B = 0, 1024, 1536, 3072, 6144
W_MAIN = 6912
C_B, C_QKV, C_Z, C_BA, C_G = 1024, 1792, 3328, 3840, 3848


def _params(n_axes, **kw):
    return pltpu.CompilerParams(dimension_semantics=("arbitrary",) * n_axes, vmem_limit_bytes=VMEM_LIMIT, **kw)


def _tile(n, target, mult=LANES):
    if n <= target:
        return n
    best = None
    for t in range(mult, target + 1, mult):
        if n % t == 0:
            best = t
    assert best is not None, (n, target, mult)
    return best


def _dg(a, b, ca, cb):
    return lax.dot_general(a.astype(bf16), b.astype(bf16), (((ca,), (cb,)), ((), ())), preferred_element_type=f32)


def _dg3(a, b, ca, cb):
    a_hi, b_hi = a.astype(bf16), b.astype(bf16)
    a_lo, b_lo = (a - a_hi.astype(f32)).astype(bf16), (b - b_hi.astype(f32)).astype(bf16)

    def dot(p, q):
        return lax.dot_general(p, q, (((ca,), (cb,)), ((), ())), preferred_element_type=f32)

    return dot(a_hi, b_hi) + (dot(a_hi, b_lo) + dot(a_lo, b_hi))


def _differentiable_dot(core):
    @functools.partial(jax.custom_vjp, nondiff_argnums=(2, 3))
    def dot(a, b, ca, cb):
        return core(a, b, ca, cb)

    def fwd(a, b, ca, cb):
        return core(a, b, ca, cb), (a, b)

    def bwd(ca, cb, res, ct):
        a, b = res
        da = core(ct, b, 1, 1 - cb) if ca == 1 else core(b, ct, 1 - cb, 1)
        db = core(a, ct, 1 - ca, 0) if cb == 0 else core(ct, a, 0, 1 - ca)
        return da, db

    dot.defvjp(fwd, bwd)
    return dot


bdot = _differentiable_dot(_dg)
_hdot = _differentiable_dot(_dg3)


def hdot(a, b, ca=1, cb=0):
    return _hdot(a, b, ca, cb)


@functools.partial(jax.custom_vjp, nondiff_argnums=(1,))
def lroll(x, shift):
    return pltpu.roll(x, shift, 1)


def _lroll_fwd(x, shift):
    return pltpu.roll(x, shift, 1), None


def _lroll_bwd(shift, _, ct):
    return (pltpu.roll(ct, ct.shape[1] - shift, 1),)


lroll.defvjp(_lroll_fwd, _lroll_bwd)


@functools.partial(jax.custom_vjp, nondiff_argnums=(1,))
def tri_inv(low, nil):
    n = low.shape[0]
    row = lax.broadcasted_iota(jnp.int32, (n, n), 0)
    col = lax.broadcasted_iota(jnp.int32, (n, n), 1)
    eye = (row == col).astype(f32)
    m = -low
    p = eye + m
    span = 2
    while span < nil:
        m = hdot(m, m)
        p = p + hdot(p, m)
        span *= 2
    return p


def _tri_inv_fwd(low, nil):
    t = tri_inv(low, nil)
    return t, t


def _tri_inv_bwd(nil, t, dt):
    return (-hdot(t, hdot(dt, t, 1, 1), 0, 0),)


tri_inv.defvjp(_tri_inv_fwd, _tri_inv_bwd)


@jax.custom_vjp
def tri_inv_known(low, t):
    return t


def _tri_inv_known_fwd(low, t):
    return t, t


def _tri_inv_known_bwd(t, dt):
    return _tri_inv_bwd(None, t, dt) + (jnp.zeros_like(t),)


tri_inv_known.defvjp(_tri_inv_known_fwd, _tri_inv_known_bwd)


def _sigmoid(x):
    return 1.0 / (1.0 + jnp.exp(-x))


def _rms(x, g):
    return x * lax.rsqrt(jnp.mean(x * x, axis=-1, keepdims=True) + NORM_EPS) * g


def _lane_col(x, lane_idx):
    lane = lax.broadcasted_iota(jnp.int32, x.shape, 1)
    return jnp.sum(jnp.where(lane == lane_idx, x, 0.0), axis=1, keepdims=True)


def matmul(a, b, mode, out_dtype, *, residual=None, group=None, side=None, tm=1024, tn=768, tk=1024, name):
    dims = {"a": ("m", "k") if mode != "tn" else ("k", "m"),
            "b": {"nn": ("k", "n"), "nt": ("n", "k"), "tn": ("k", "n")}[mode], "o": ("m", "n")}
    full, groups = {}, 1
    for arr, key in ((a, "a"), (b, "b")):
        grouped = group in dims[key]
        if grouped:
            groups = arr.shape[0]
        full[dims[key][0]], full[dims[key][1]] = arr.shape[1:] if grouped else arr.shape
    want = {"m": tm, "n": tn, "k": tk}
    tiles = {d: full[d] if d == group else _tile(full[d], want[d]) for d in "mnk"}
    steps = {d: groups if d == group else full[d] // tiles[d] for d in "mnk"}

    def spec(key):
        d0, d1 = dims[key]

        def index(i, j, kk):
            at = {"m": i, "n": j, "k": kk}
            if group in (d0, d1):
                return (at[group], 0 if d0 == group else at[d0], 0 if d1 == group else at[d1])
            return (at[d0], at[d1])

        block = (tiles[d0], tiles[d1])
        return pl.BlockSpec(((None,) + block) if group in (d0, d1) else block, index)

    ca, cb = {"nn": (1, 0), "nt": (1, 1), "tn": (0, 0)}[mode]
    nk = steps["k"]
    o_spec = spec("o")
    out_shape = (groups, full["m"], full["n"]) if group in ("m", "n") else (full["m"], full["n"])
    has_res = residual is not None
    n_in = 3 if has_res else 2
    n_side = side.n if side is not None else 0
    grid = (steps["m"], steps["n"], nk)

    def body(*refs):
        a_ref, b_ref = refs[:2]
        r_ref = refs[2] if has_res else None
        side_in, o_ref = refs[n_in:n_in + n_side], refs[n_in + n_side]
        side_out, acc_ref = refs[n_in + n_side + 1:n_in + 2 * n_side + 1], refs[n_in + 2 * n_side + 1]
        sems = refs[n_in + 2 * n_side + 2:]
        kk = pl.program_id(2)
        if side is not None:
            side.run_around(grid, side_in, side_out, sems, before=True)

        @pl.when(kk == 0)
        def _():
            acc_ref[...] = jnp.zeros_like(acc_ref)

        acc_ref[...] += _dg(a_ref[...], b_ref[...], ca, cb)

        @pl.when(kk == nk - 1)
        def _():
            acc = acc_ref[...]
            if has_res:
                acc = acc + r_ref[...]
            o_ref[...] = acc.astype(out_dtype)

        if side is not None:
            side.run_around(grid, side_in, side_out, sems, before=False)

    in_specs = [spec("a"), spec("b")] + ([o_spec] if has_res else [])
    args = (a, b) + ((residual,) if has_res else ())
    out_specs, out_shapes, scratch = [o_spec], [jax.ShapeDtypeStruct(out_shape, out_dtype)], [pltpu.VMEM((tiles["m"], tiles["n"]), f32)]
    if side is not None:
        in_specs, args = in_specs + side.specs, args + tuple(side.arrays)
        out_specs, out_shapes, scratch = out_specs + side.specs, out_shapes + side.out_shapes, scratch + side.scratch
    outs = pl.pallas_call(
        body, name=name, grid=grid, in_specs=in_specs, out_specs=out_specs, out_shape=out_shapes,
        scratch_shapes=scratch, compiler_params=_params(3),
    )(*args)
    return outs[0] if side is None else (outs[0], outs[1:])


def rmsnorm_fwd(x, g_row, *, name):
    s, d = x.shape
    ts = _tile(s, 512, 16)

    def body(x_ref, g_ref, o_ref):
        o_ref[...] = _rms(x_ref[...], g_ref[...]).astype(bf16)

    return pl.pallas_call(
        body, name=name, grid=(s // ts,),
        in_specs=[pl.BlockSpec((ts, d), lambda i: (i, 0)), pl.BlockSpec((1, d), lambda i: (0, 0))],
        out_specs=pl.BlockSpec((ts, d), lambda i: (i, 0)), out_shape=jax.ShapeDtypeStruct((s, d), bf16),
        compiler_params=_params(1),
    )(x, g_row)


def rmsnorm_bwd(x, g_row, dh, dres, *, name):
    s, d = x.shape
    ts = _tile(s, 512, 16)

    def body(x_ref, g_ref, dh_ref, dres_ref, dx_ref, dg_ref):
        @pl.when(pl.program_id(0) == 0)
        def _():
            dg_ref[...] = jnp.zeros_like(dg_ref)

        _, vjp = jax.vjp(_rms, x_ref[...], g_ref[...])
        dx, dg = vjp(dh_ref[...])
        dx_ref[...] = dx + dres_ref[...]
        dg_ref[...] += dg

    row = pl.BlockSpec((ts, d), lambda i: (i, 0))
    vec = pl.BlockSpec((1, d), lambda i: (0, 0))
    return pl.pallas_call(
        body, name=name, grid=(s // ts,), in_specs=[row, vec, row, row], out_specs=[row, vec],
        out_shape=[jax.ShapeDtypeStruct((s, d), f32), jax.ShapeDtypeStruct((1, d), f32)],
        compiler_params=_params(1),
    )(x, g_row, dh, dres)


def loss_head(x, g_row, target, *, name):
    s, d = x.shape
    ts = _tile(s, 512, 16)

    def body(x_ref, g_ref, t_ref, dx_ref, dg_ref, loss_ref):
        @pl.when(pl.program_id(0) == 0)
        def _():
            dg_ref[...] = jnp.zeros_like(dg_ref)
            loss_ref[...] = jnp.zeros_like(loss_ref)

        y, vjp = jax.vjp(_rms, x_ref[...], g_ref[...])
        err = y - t_ref[...]
        dx, dg = vjp(err * (1.0 / d))
        dx_ref[...] = dx
        dg_ref[...] += dg
        loss_ref[...] += 0.5 * jnp.sum(jnp.sum(err * err, axis=1, keepdims=True) * (1.0 / d), axis=0, keepdims=True)

    row = pl.BlockSpec((ts, d), lambda i: (i, 0))
    vec = pl.BlockSpec((1, d), lambda i: (0, 0))
    one = pl.BlockSpec((1, LANES), lambda i: (0, 0))
    return pl.pallas_call(
        body, name=name, grid=(s // ts,), in_specs=[row, vec, row], out_specs=[row, vec, one],
        out_shape=[jax.ShapeDtypeStruct((s, d), f32), jax.ShapeDtypeStruct((1, d), f32),
                   jax.ShapeDtypeStruct((1, LANES), f32)],
        compiler_params=_params(1),
    )(x, g_row, target)


def _sgu_chunk(p_a, ln_g, ln_b, w, b_t):
    t = SGU_CHUNK
    u = jax.nn.gelu(p_a[:, :MIX])
    v = jax.nn.gelu(p_a[:, MIX:])
    vc = v - jnp.mean(v, axis=-1, keepdims=True)
    vn = vc * lax.rsqrt(jnp.mean(vc * vc, axis=-1, keepdims=True) + NORM_EPS) * ln_g + ln_b
    causal = lax.broadcasted_iota(jnp.int32, (t, t), 0) >= lax.broadcasted_iota(jnp.int32, (t, t), 1)
    outs = []
    for g in range(SGU_GROUPS):
        sl = slice(g * LANES, (g + 1) * LANES)
        mixed = bdot(jnp.where(causal, w[g], 0.0), vn[:, sl], 1, 0) + b_t[:, g:g + 1]
        outs.append(u[:, sl] * mixed)
    return jnp.concatenate(outs, axis=1)


def _sgu_specs(s, ts):
    return [pl.BlockSpec((ts, 2 * MIX), lambda i: (i, SEC_A // (2 * MIX))),
            pl.BlockSpec((1, MIX), lambda i: (0, 0)), pl.BlockSpec((1, MIX), lambda i: (0, 0)),
            pl.BlockSpec((SGU_GROUPS, SGU_CHUNK, SGU_CHUNK), lambda i: (0, 0, 0)),
            pl.BlockSpec((SGU_CHUNK, SGU_GROUPS), lambda i: (0, 0))]


def sgu_fwd(proj, ln_g, ln_b, w, b_t, *, name):
    s = proj.shape[0]
    ts = _tile(s, 512)
    n_chunk = ts // SGU_CHUNK

    def body(p_ref, g_ref, b_ref, w_ref, bt_ref, o_ref):
        def step(c, carry):
            rows = pl.ds(pl.multiple_of(c * SGU_CHUNK, SGU_CHUNK), SGU_CHUNK)
            o_ref[rows, :] = _sgu_chunk(p_ref[rows, :], g_ref[...], b_ref[...], w_ref[...], bt_ref[...]).astype(bf16)
            return carry
        lax.fori_loop(0, n_chunk, step, 0)

    return pl.pallas_call(
        body, name=name, grid=(s // ts,), in_specs=_sgu_specs(s, ts),
        out_specs=pl.BlockSpec((ts, MIX), lambda i: (i, 0)), out_shape=jax.ShapeDtypeStruct((s, MIX), bf16),
        compiler_params=_params(1),
    )(proj, ln_g, ln_b, w, b_t)


def sgu_bwd(proj, ln_g, ln_b, w, b_t, d_out, dproj, *, name):
    s = proj.shape[0]
    ts = _tile(s, 512)
    n_chunk = ts // SGU_CHUNK

    def body(p_ref, g_ref, b_ref, w_ref, bt_ref, do_ref, _, dp_ref, dg_ref, db_ref, dw_ref, dbt_ref):
        @pl.when(pl.program_id(0) == 0)
        def _():
            dg_ref[...] = jnp.zeros_like(dg_ref)
            db_ref[...] = jnp.zeros_like(db_ref)
            dw_ref[...] = jnp.zeros_like(dw_ref)
            dbt_ref[...] = jnp.zeros_like(dbt_ref)

        def step(c, carry):
            rows = pl.ds(pl.multiple_of(c * SGU_CHUNK, SGU_CHUNK), SGU_CHUNK)
            _, vjp = jax.vjp(_sgu_chunk, p_ref[rows, :], g_ref[...], b_ref[...], w_ref[...], bt_ref[...])
            dp, dg, db, dw, dbt = vjp(do_ref[rows, :])
            dp_ref[rows, :] = dp.astype(bf16)
            dg_ref[...] += dg
            db_ref[...] += db
            dw_ref[...] += dw
            dbt_ref[...] += dbt
            return carry
        lax.fori_loop(0, n_chunk, step, 0)

    specs = _sgu_specs(s, ts)
    return pl.pallas_call(
        body, name=name, grid=(s // ts,),
        in_specs=specs + [pl.BlockSpec((ts, MIX), lambda i: (i, 0)), pl.BlockSpec(memory_space=pl.ANY)],
        out_specs=[specs[0], specs[1], specs[2], specs[3], specs[4]],
        out_shape=[jax.ShapeDtypeStruct(dproj.shape, bf16), jax.ShapeDtypeStruct((1, MIX), f32),
                   jax.ShapeDtypeStruct((1, MIX), f32), jax.ShapeDtypeStruct(w.shape, f32),
                   jax.ShapeDtypeStruct(b_t.shape, f32)],
        input_output_aliases={6: 0}, compiler_params=_params(1),
    )(proj, ln_g, ln_b, w, b_t, d_out, dproj)


def rope_tables(posf, inv_freq, *, name):
    s = posf.shape[0]
    ts = _tile(s, 1024, 8)
    half = ROPE_DIM // 2

    def body(pos_ref, inv_ref, o_ref):
        d = lax.broadcasted_iota(jnp.int32, (1, LANES), 1) % SWA_HD
        ang = pos_ref[...] * inv_ref[...]
        sin = jnp.sin(ang)
        o_ref[0] = jnp.cos(ang)
        o_ref[1] = jnp.where(d < half, sin, 0.0)
        o_ref[2] = jnp.where((d >= half) & (d < ROPE_DIM), sin, 0.0)

    return pl.pallas_call(
        body, name=name, grid=(s // ts,),
        in_specs=[pl.BlockSpec((ts, 1), lambda i: (i, 0)), pl.BlockSpec((1, LANES), lambda i: (0, 0))],
        out_specs=pl.BlockSpec((3, ts, LANES), lambda i: (0, i, 0)), out_shape=jax.ShapeDtypeStruct((3, s, LANES), f32),
        compiler_params=_params(1),
    )(posf, inv_freq)


def _rope(x, table):
    w = x.shape[1]
    half = ROPE_DIM // 2
    c, lo, hi = (jnp.concatenate([table[i]] * (w // LANES), axis=1) for i in range(3))
    return x * c - lroll(x, w - half) * lo + lroll(x, half) * hi


def _swa_block(q, kp, kc, vp, vc, sink_row, table_q, table_p, prev_ok):
    t = WINDOW
    q = _rope(q, table_q) * (SWA_HD ** -0.5)
    kp, kc = _rope(kp, table_p), _rope(kc, table_q)
    own = lax.broadcasted_iota(jnp.int32, (t, t), 0) >= lax.broadcasted_iota(jnp.int32, (t, t), 1)
    lane_half = lax.broadcasted_iota(jnp.int32, (t, LANES), 1) // SWA_HD
    group = SWA_HEADS // SWA_KV
    slabs = []
    for pair in range(SWA_HEADS // 2):
        q_pair = q[:, pair * LANES:(pair + 1) * LANES]
        acc = jnp.zeros((t, LANES), f32)
        for half in range(2):
            h = 2 * pair + half
            kv = h // group
            qm = jnp.where(lane_half == half, q_pair, 0.0)
            if half != kv:
                qm = lroll(qm, SWA_HD)
            logits = jnp.where(own, bdot(qm, kc, 1, 1), jnp.where(prev_ok, bdot(qm, kp, 1, 1), -1e30))
            sink = _lane_col(sink_row, h)
            m = jnp.maximum(jnp.max(logits, axis=1, keepdims=True), sink)
            p = jnp.exp(logits - m)
            probs = p * (1.0 / (jnp.sum(p, axis=1, keepdims=True) + jnp.exp(sink - m)))
            o = bdot(jnp.where(own, probs, 0.0), vc, 1, 0) + bdot(jnp.where(own, 0.0, probs), vp, 1, 0)
            o = jnp.where(lane_half == kv, o, 0.0)
            if half != kv:
                o = lroll(o, SWA_HD)
            acc = acc + o
        slabs.append(acc)
    return jnp.concatenate(slabs, axis=1)


def _swa_in_specs(nc, clamp):
    t = WINDOW
    qb, kb, vb = SEC_B // MIX, (SEC_B + MIX) // LANES, (SEC_B + MIX + LANES) // LANES

    def cur(i):
        return jnp.minimum(i, nc - 1) if clamp else i

    def prev(i):
        return jnp.maximum(cur(i) - 1, 0)

    return [pl.BlockSpec((t, MIX), lambda i: (cur(i), qb)),
            pl.BlockSpec((t, LANES), lambda i: (prev(i), kb)), pl.BlockSpec((t, LANES), lambda i: (cur(i), kb)),
            pl.BlockSpec((t, LANES), lambda i: (prev(i), vb)), pl.BlockSpec((t, LANES), lambda i: (cur(i), vb)),
            pl.BlockSpec((1, LANES), lambda i: (0, 0)),
            pl.BlockSpec((3, t, LANES), lambda i: (0, cur(i), 0)), pl.BlockSpec((3, t, LANES), lambda i: (0, prev(i), 0))]


def swa_fwd(proj, sink_row, tables, *, name):
    s = proj.shape[0]
    nc = s // WINDOW

    def body(q_ref, kp_ref, kc_ref, vp_ref, vc_ref, sink_ref, tq_ref, tp_ref, o_ref):
        prev_ok = pl.program_id(0) > 0
        o_ref[...] = _swa_block(q_ref[...], kp_ref[...], kc_ref[...], vp_ref[...], vc_ref[...], sink_ref[...],
                                tq_ref[...], tp_ref[...], prev_ok).astype(bf16)

    return pl.pallas_call(
        body, name=name, grid=(nc,), in_specs=_swa_in_specs(nc, False),
        out_specs=pl.BlockSpec((WINDOW, MIX), lambda i: (i, 0)), out_shape=jax.ShapeDtypeStruct((s, MIX), bf16),
        compiler_params=_params(1),
    )(proj, proj, proj, proj, proj, sink_row, tables, tables)


def swa_bwd(proj, sink_row, tables, d_out, dproj, *, side=None, name):
    s = proj.shape[0]
    nc = s // WINDOW
    t = WINDOW

    def body(q_ref, kp_ref, kc_ref, vp_ref, vc_ref, sink_ref, tq_ref, tp_ref, do_ref, _,
             dp_ref, dsink_ref, cq_ref, ck_ref, cv_ref):
        i = pl.program_id(0)

        @pl.when(i == 0)
        def _():
            dsink_ref[...] = jnp.zeros_like(dsink_ref)

        def write(dk_prev, dv_prev):
            dp_ref[:, :MIX] = cq_ref[...].astype(bf16)
            dp_ref[:, MIX:MIX + LANES] = (ck_ref[...] + dk_prev).astype(bf16)
            dp_ref[:, MIX + LANES:] = (cv_ref[...] + dv_prev).astype(bf16)

        @pl.when(i < nc)
        def _():
            fn = functools.partial(_swa_block, table_q=tq_ref[...], table_p=tp_ref[...], prev_ok=i > 0)
            _, vjp = jax.vjp(fn, q_ref[...], kp_ref[...], kc_ref[...], vp_ref[...], vc_ref[...], sink_ref[...])
            dq, dkp, dkc, dvp, dvc, dsink = vjp(do_ref[...])
            dsink_ref[...] += dsink

            @pl.when(i > 0)
            def _():
                write(dkp, dvp)

            cq_ref[...] = dq
            ck_ref[...] = dkc
            cv_ref[...] = dvc

        @pl.when(i == nc)
        def _():
            write(0.0, 0.0)

    return host_call(
        side, body, name=name, grid=(nc + 1,),
        in_specs=_swa_in_specs(nc, True) + [pl.BlockSpec((t, MIX), lambda i: (jnp.minimum(i, nc - 1), 0)),
                                            pl.BlockSpec(memory_space=pl.ANY)],
        out_specs=[pl.BlockSpec((t, MIX + 2 * LANES), lambda i: (jnp.maximum(i - 1, 0), SEC_B // (MIX + 2 * LANES))),
                   pl.BlockSpec((1, LANES), lambda i: (0, 0))],
        out_shape=[jax.ShapeDtypeStruct(dproj.shape, bf16), jax.ShapeDtypeStruct((1, LANES), f32)],
        scratch_shapes=[pltpu.VMEM((t, MIX), f32), pltpu.VMEM((t, LANES), f32), pltpu.VMEM((t, LANES), f32)],
        aliases={9: 0}, args=(proj, proj, proj, proj, proj, sink_row, tables, tables, d_out, dproj))


CONV_PAD = 8


def _conv_pre(xp, w, rows):
    off = CONV_PAD - (DN_CONV - 1)
    pre = xp[off:off + rows] * w[0:1]
    for i in range(1, DN_CONV):
        pre = pre + xp[off + i:off + i + rows] * w[i:i + 1]
    return pre


def conv_fwd(proj, conv_w8, *, name):
    s = proj.shape[0]
    wq = 3 * MIX
    ts = _tile(s, 512)
    nb = ts // CONV_PAD

    def body(x_ref, prev_ref, w_ref, o_ref):
        prev = jnp.where(pl.program_id(0) > 0, prev_ref[...], 0.0)
        pre = _conv_pre(jnp.concatenate([prev, x_ref[...]], axis=0), w_ref[...], ts)
        o_ref[...] = pre * _sigmoid(pre)

    return pl.pallas_call(
        body, name=name, grid=(s // ts,),
        in_specs=[pl.BlockSpec((ts, wq), lambda i: (i, SEC_QKV // wq)),
                  pl.BlockSpec((CONV_PAD, wq), lambda i: (jnp.maximum(i * nb - 1, 0), SEC_QKV // wq)),
                  pl.BlockSpec((CONV_PAD, wq), lambda i: (0, 0))],
        out_specs=pl.BlockSpec((ts, wq), lambda i: (i, 0)), out_shape=jax.ShapeDtypeStruct((s, wq), f32),
        compiler_params=_params(1),
    )(proj, proj, conv_w8)


def conv_bwd(proj, conv_w8, dxc, dproj, *, name):
    s = proj.shape[0]
    wq = 3 * MIX
    ts = _tile(s, 512)
    nb = ts // CONV_PAD
    nt = s // ts
    last_blk = s // CONV_PAD - 1

    def body(x_ref, prev_ref, next_ref, w_ref, d_ref, dnext_ref, _, dp_ref, dw_ref):
        i = pl.program_id(0)

        @pl.when(i == 0)
        def _():
            dw_ref[...] = jnp.zeros_like(dw_ref)

        w = w_ref[...]
        prev = jnp.where(i > 0, prev_ref[...], 0.0)
        more = i < nt - 1
        xp = jnp.concatenate([prev, x_ref[...], jnp.where(more, next_ref[...], 0.0)], axis=0)
        ext = ts + CONV_PAD
        pre = _conv_pre(xp, w, ext)
        sig = _sigmoid(pre)
        dxc_ext = jnp.concatenate([d_ref[...], jnp.where(more, dnext_ref[...], 0.0)], axis=0)
        dpre = dxc_ext * sig * (1.0 + pre * (1.0 - sig))
        off = CONV_PAD - (DN_CONV - 1)
        d_raw = jnp.zeros((ts, wq), f32)
        dws = []
        for k in range(DN_CONV):
            shift = DN_CONV - 1 - k
            d_raw = d_raw + dpre[shift:shift + ts] * w[k:k + 1]
            dws.append(jnp.sum(dpre[:ts] * xp[off + k:off + k + ts], axis=0, keepdims=True))
        dp_ref[...] = d_raw.astype(bf16)
        dw_ref[...] += jnp.concatenate(dws + [jnp.zeros((CONV_PAD - DN_CONV, wq), f32)], axis=0)

    sec = SEC_QKV // wq
    return pl.pallas_call(
        body, name=name, grid=(nt,),
        in_specs=[pl.BlockSpec((ts, wq), lambda i: (i, sec)),
                  pl.BlockSpec((CONV_PAD, wq), lambda i: (jnp.maximum(i * nb - 1, 0), sec)),
                  pl.BlockSpec((CONV_PAD, wq), lambda i: (jnp.minimum((i + 1) * nb, last_blk), sec)),
                  pl.BlockSpec((CONV_PAD, wq), lambda i: (0, 0)),
                  pl.BlockSpec((ts, wq), lambda i: (i, 0)),
                  pl.BlockSpec((CONV_PAD, wq), lambda i: (jnp.minimum((i + 1) * nb, last_blk), 0)),
                  pl.BlockSpec(memory_space=pl.ANY)],
        out_specs=[pl.BlockSpec((ts, wq), lambda i: (i, sec)), pl.BlockSpec((CONV_PAD, wq), lambda i: (0, 0))],
        out_shape=[jax.ShapeDtypeStruct(dproj.shape, bf16), jax.ShapeDtypeStruct((CONV_PAD, wq), f32)],
        input_output_aliases={6: 0}, compiler_params=_params(1),
    )(proj, proj, proj, conv_w8, dxc, dxc, dproj)


def _dn_chunk(state, xc, z, ba, alog_row, dtb_row, norm_row, t_known):
    c, nh = DN_CHUNK, DN_HEADS
    n = c * nh
    row = lax.broadcasted_iota(jnp.int32, (n, n), 0)
    col = lax.broadcasted_iota(jnp.int32, (n, n), 1)
    same_head = (row // c) == (col // c)
    tril, strict = same_head & (row >= col), same_head & (row > col)
    tril_c = lax.broadcasted_iota(jnp.int32, (c, c), 0) >= lax.broadcasted_iota(jnp.int32, (c, c), 1)
    beta_all = _sigmoid(ba)
    g_all = -jnp.exp(alog_row) * jax.nn.softplus(ba + dtb_row)
    gc_all = hdot(tril_c.astype(f32), g_all)
    gc_t = gc_all.T

    def stack(piece):
        return jnp.concatenate([piece(h) for h in range(nh)], axis=0)

    q = stack(lambda h: xc[:, h * DN_HD:(h + 1) * DN_HD])
    k = stack(lambda h: xc[:, MIX + h * DN_HD:MIX + (h + 1) * DN_HD])
    v = stack(lambda h: xc[:, 2 * MIX + h * DN_HD:2 * MIX + (h + 1) * DN_HD])
    zs = stack(lambda h: z[:, h * DN_HD:(h + 1) * DN_HD])
    q = q * lax.rsqrt(jnp.sum(q * q, axis=-1, keepdims=True) + NORM_EPS) * (DN_HD ** -0.5)
    k = k * lax.rsqrt(jnp.sum(k * k, axis=-1, keepdims=True) + NORM_EPS)
    beta = stack(lambda h: _lane_col(beta_all, h))
    g_cols = [_lane_col(gc_all, nh + h) for h in range(nh)]
    g_col = jnp.concatenate(g_cols, axis=0)
    g_row = jnp.concatenate([gc_t[nh + h:nh + h + 1, :] for h in range(nh)], axis=1)
    g_last = stack(lambda h: jnp.broadcast_to(g_cols[h][c - 1:c, :], (c, 1)))
    decay = jnp.where(tril, jnp.exp(jnp.where(tril, g_col - g_row, 0.0)), 0.0)
    kb = k * beta
    low = jnp.where(strict, bdot(kb, k, 1, 1) * decay, 0.0)
    t_inv = tri_inv(low, c) if t_known is None else tri_inv_known(low, t_known)
    e_gc = jnp.exp(g_col)
    uw = hdot(t_inv, jnp.concatenate([v * beta, kb * e_gc], axis=1))
    u, w = uw[:, :DN_HD], uw[:, DN_HD:]
    attn = bdot(q, k, 1, 1) * decay
    own = (lax.broadcasted_iota(jnp.int32, (n, nh * DN_HD), 1) // DN_HD
           == lax.broadcasted_iota(jnp.int32, (n, nh * DN_HD), 0) // c)

    def spread(a):
        return jnp.where(own, jnp.concatenate([a] * nh, axis=1), 0.0)

    v_new = u - bdot(spread(w), state, 1, 0)
    o = bdot(spread(q * e_gc), state, 1, 0) + bdot(attn, v_new, 1, 0)
    keep = stack(lambda h: jnp.broadcast_to(jnp.exp(g_cols[h][c - 1:c, :]), (DN_HD, 1)))
    new_state = state * keep + bdot(spread(k * jnp.exp(g_last - g_col)), v_new, 0, 0)
    out = _rms(o, norm_row) * (zs * _sigmoid(zs))
    return new_state, jnp.concatenate([out[h * c:(h + 1) * c] for h in range(nh)], axis=1), t_inv


DN_STEP = 2 * DN_CHUNK


def _dn_step(state, xc, z, ba, alog_row, dtb_row, norm_row, t_known=None):
    outs, t_invs = [], []
    for c in range(DN_STEP // DN_CHUNK):
        rows = slice(c * DN_CHUNK, (c + 1) * DN_CHUNK)
        state, out, t_inv = _dn_chunk(state, xc[rows], z[rows], ba[rows], alog_row, dtb_row, norm_row,
                                      None if t_known is None else t_known[c])
        outs.append(out)
        t_invs.append(t_inv)
    return state, jnp.concatenate(outs, axis=0), jnp.stack(t_invs)


def _dn_specs(ts, order):
    zb = SEC_Z // MIX
    return [pl.BlockSpec((ts, 3 * MIX), lambda i: (order(i), 0)),
            pl.BlockSpec((ts, MIX), lambda i: (order(i), zb)),
            pl.BlockSpec((ts, LANES), lambda i: (order(i), 0)),
            pl.BlockSpec((1, LANES), lambda i: (0, 0)), pl.BlockSpec((1, LANES), lambda i: (0, 0)),
            pl.BlockSpec((1, LANES), lambda i: (0, 0))]


def dn_fwd(xc, proj, p_ba, alog_row, dtb_row, norm_row, *, side=None, name):
    s = xc.shape[0]
    ts = _tile(s, 512)
    n_step = ts // DN_STEP

    per_step = DN_STEP // DN_CHUNK
    n_tri = DN_HEADS * DN_CHUNK

    def body(xc_ref, z_ref, ba_ref, al_ref, dt_ref, nr_ref, o_ref, st_ref, tri_ref, state_ref):
        @pl.when(pl.program_id(0) == 0)
        def _():
            state_ref[...] = jnp.zeros_like(state_ref)

        def step(c, carry):
            rows = pl.ds(pl.multiple_of(c * DN_STEP, DN_STEP), DN_STEP)
            st_ref[c] = state_ref[...]
            new_state, out, t_invs = _dn_step(state_ref[...], xc_ref[rows, :], z_ref[rows, :], ba_ref[rows, :],
                                              al_ref[...], dt_ref[...], nr_ref[...])
            state_ref[...] = new_state
            o_ref[rows, :] = out.astype(bf16)
            tri_ref[pl.ds(c * per_step, per_step)] = t_invs
            return carry
        lax.fori_loop(0, n_step, step, 0)

    return host_call(
        side, body, name=name, grid=(s // ts,), in_specs=_dn_specs(ts, lambda i: i),
        out_specs=[pl.BlockSpec((ts, MIX), lambda i: (i, 0)),
                   pl.BlockSpec((n_step, DN_HEADS * DN_HD, DN_HD), lambda i: (i, 0, 0)),
                   pl.BlockSpec((n_step * per_step, n_tri, n_tri), lambda i: (i, 0, 0))],
        out_shape=[jax.ShapeDtypeStruct((s, MIX), bf16),
                   jax.ShapeDtypeStruct((s // DN_STEP, DN_HEADS * DN_HD, DN_HD), f32),
                   jax.ShapeDtypeStruct((s // DN_CHUNK, n_tri, n_tri), f32)],
        scratch_shapes=[pltpu.VMEM((DN_HEADS * DN_HD, DN_HD), f32)], aliases={},
        args=(xc, proj, p_ba, alog_row, dtb_row, norm_row))


def dn_bwd(xc, proj, p_ba, alog_row, dtb_row, norm_row, saved, tri, d_out, dproj, *, side=None, name):
    s = xc.shape[0]
    ts = _tile(s, 512)
    n_step = ts // DN_STEP
    nt = s // ts

    per_step = DN_STEP // DN_CHUNK
    n_tri = DN_HEADS * DN_CHUNK

    def body(xc_ref, z_ref, ba_ref, al_ref, dt_ref, nr_ref, st_ref, tri_ref, do_ref, _,
             dz_ref, dxc_ref, dba_ref, dal_ref, ddt_ref, dnr_ref, dstate_ref):
        @pl.when(pl.program_id(0) == 0)
        def _():
            dstate_ref[...] = jnp.zeros_like(dstate_ref)
            dal_ref[...] = jnp.zeros_like(dal_ref)
            ddt_ref[...] = jnp.zeros_like(ddt_ref)
            dnr_ref[...] = jnp.zeros_like(dnr_ref)

        def step(it, carry):
            c = n_step - 1 - it
            rows = pl.ds(pl.multiple_of(c * DN_STEP, DN_STEP), DN_STEP)
            t_known = tri_ref[pl.ds(c * per_step, per_step)]
            _, vjp = jax.vjp(lambda *a: _dn_step(*a, t_known=t_known)[:2], st_ref[c], xc_ref[rows, :],
                             z_ref[rows, :], ba_ref[rows, :], al_ref[...], dt_ref[...], nr_ref[...])
            d_in, dxc, dz, dba, dal, ddt, dnr = vjp((dstate_ref[...], do_ref[rows, :]))
            dstate_ref[...] = d_in
            dxc_ref[rows, :] = dxc
            dz_ref[rows, :] = dz.astype(bf16)
            dba_ref[rows, :] = dba.astype(bf16)
            dal_ref[...] += dal
            ddt_ref[...] += ddt
            dnr_ref[...] += dnr
            return carry
        lax.fori_loop(0, n_step, step, 0)

    def rev(i):
        return nt - 1 - i

    specs = _dn_specs(ts, rev)
    vec = pl.BlockSpec((1, LANES), lambda i: (0, 0))
    return host_call(
        side, body, name=name, grid=(nt,),
        in_specs=specs + [pl.BlockSpec((n_step, DN_HEADS * DN_HD, DN_HD), lambda i: (rev(i), 0, 0)),
                          pl.BlockSpec((n_step * per_step, n_tri, n_tri), lambda i: (rev(i), 0, 0)),
                          pl.BlockSpec((ts, MIX), lambda i: (rev(i), 0)), pl.BlockSpec(memory_space=pl.ANY)],
        out_specs=[specs[1], specs[0], specs[2], vec, vec, vec],
        out_shape=[jax.ShapeDtypeStruct(dproj.shape, bf16), jax.ShapeDtypeStruct((s, 3 * MIX), f32),
                   jax.ShapeDtypeStruct((s, LANES), bf16)] + [jax.ShapeDtypeStruct((1, LANES), f32)] * 3,
        scratch_shapes=[pltpu.VMEM((DN_HEADS * DN_HD, DN_HD), f32)], aliases={9: 0},
        args=(xc, proj, p_ba, alog_row, dtb_row, norm_row, saved, tri, d_out, dproj))


def _merge_in_specs(ts, d):
    row = pl.BlockSpec((ts, MIX), lambda i: (i, 0))
    return [row, row, row, pl.BlockSpec((ts, 3 * d), lambda i: (i, SEC_G // (3 * d))),
            pl.BlockSpec((3, MIX, d), lambda i: (0, 0, 0))]


def merge_fwd(out_a, out_b, out_c, proj, w_branch, *, name):
    s, d = out_a.shape[0], w_branch.shape[2]
    ts = _tile(s, 256, 16)

    def body(a_ref, b_ref, c_ref, g_ref, w_ref, o_ref):
        acc = jnp.zeros((ts, d), f32)
        for n, r in enumerate((a_ref, b_ref, c_ref)):
            acc = acc + _sigmoid(g_ref[:, n * d:(n + 1) * d]) * _dg(r[...], w_ref[n], 1, 0)
        o_ref[...] = acc.astype(bf16)

    return pl.pallas_call(
        body, name=name, grid=(s // ts,), in_specs=_merge_in_specs(ts, d),
        out_specs=pl.BlockSpec((ts, d), lambda i: (i, 0)), out_shape=jax.ShapeDtypeStruct((s, d), bf16),
        compiler_params=_params(1),
    )(out_a, out_b, out_c, proj, w_branch)


def merge_bwd(out_a, out_b, out_c, proj, w_branch, d_merged, dproj, *, name):
    s, d = out_a.shape[0], w_branch.shape[2]
    ts = _tile(s, 256, 16)

    def body(a_ref, b_ref, c_ref, g_ref, w_ref, dm_ref, _, dg_ref, da_ref, db_ref, dc_ref, dw_ref):
        @pl.when(pl.program_id(0) == 0)
        def _():
            dw_ref[...] = jnp.zeros_like(dw_ref)

        dm = dm_ref[...]
        for n, (r, dr) in enumerate(((a_ref, da_ref), (b_ref, db_ref), (c_ref, dc_ref))):
            gate = _sigmoid(g_ref[:, n * d:(n + 1) * d])
            branch = _dg(r[...], w_ref[n], 1, 0)
            dg_ref[:, n * d:(n + 1) * d] = (dm * branch * gate * (1.0 - gate)).astype(bf16)
            d_branch = dm * gate
            dr[...] = _dg(d_branch, w_ref[n], 1, 1)
            dw_ref[n] += _dg(r[...], d_branch, 0, 0)

    specs = _merge_in_specs(ts, d)
    row_f = pl.BlockSpec((ts, MIX), lambda i: (i, 0))
    return pl.pallas_call(
        body, name=name, grid=(s // ts,),
        in_specs=specs + [pl.BlockSpec((ts, d), lambda i: (i, 0)), pl.BlockSpec(memory_space=pl.ANY)],
        out_specs=[specs[3], row_f, row_f, row_f, specs[4]],
        out_shape=[jax.ShapeDtypeStruct(dproj.shape, bf16)] + [jax.ShapeDtypeStruct((s, MIX), f32)] * 3
        + [jax.ShapeDtypeStruct(w_branch.shape, f32)],
        input_output_aliases={6: 0}, compiler_params=_params(1),
    )(out_a, out_b, out_c, proj, w_branch, d_merged, dproj)


def swiglu_fwd(gu, *, name):
    g2, s, w = gu.shape
    ng = g2 // 2
    ts = _tile(s, 1024, 16)

    def body(g_ref, u_ref, o_ref):
        g = g_ref[...].astype(f32)
        o_ref[...] = (g * _sigmoid(g) * u_ref[...].astype(f32)).astype(bf16)

    return pl.pallas_call(
        body, name=name, grid=(s // ts, ng),
        in_specs=[pl.BlockSpec((None, ts, w), lambda i, j: (j, i, 0)),
                  pl.BlockSpec((None, ts, w), lambda i, j: (ng + j, i, 0))],
        out_specs=pl.BlockSpec((None, ts, w), lambda i, j: (j, i, 0)), out_shape=jax.ShapeDtypeStruct((ng, s, w), bf16),
        compiler_params=_params(2),
    )(gu, gu)


def swiglu_bwd(gu, d_act, *, name):
    g2, s, w = gu.shape
    ng = g2 // 2
    ts = _tile(s, 1024, 16)

    def body(g_ref, u_ref, d_ref, o_ref):
        g, d = g_ref[...].astype(f32), d_ref[...]
        sig = _sigmoid(g)

        @pl.when(pl.program_id(1) < ng)
        def _():
            o_ref[...] = (d * u_ref[...].astype(f32) * sig * (1.0 + g * (1.0 - sig))).astype(bf16)

        @pl.when(pl.program_id(1) >= ng)
        def _():
            o_ref[...] = (d * g * sig).astype(bf16)

    return pl.pallas_call(
        body, name=name, grid=(s // ts, g2),
        in_specs=[pl.BlockSpec((None, ts, w), lambda i, j: (j % ng, i, 0)),
                  pl.BlockSpec((None, ts, w), lambda i, j: (ng + j % ng, i, 0)),
                  pl.BlockSpec((None, ts, w), lambda i, j: (j % ng, i, 0))],
        out_specs=pl.BlockSpec((None, ts, w), lambda i, j: (j, i, 0)), out_shape=jax.ShapeDtypeStruct((g2, s, w), bf16),
        compiler_params=_params(2),
    )(gu, gu, d_act)


def adamw(w, m, v, g_parts, *, name):
    n_layers = len(g_parts)
    n_parts, r, cols = g_parts[0].shape
    lanes = -(-cols // LANES) * LANES
    tr = _tile(r, max(16, (128 * 1024) // lanes), 16)
    nr = r // tr

    def body(w_ref, m_ref, v_ref, *rest):
        gp_refs, (g_ref, d_ref, nm_ref, nv_ref) = rest[:n_layers], rest[n_layers:]
        layer = pl.program_id(0)
        g = jnp.zeros((tr, cols), f32)
        for l, gp_ref in enumerate(gp_refs):
            g_l = gp_ref[0].astype(f32)
            for k in range(1, n_parts):
                g_l = g_l + gp_ref[k].astype(f32)
            g = jnp.where(layer == l, g_l, g)
        nm = ADAM_B1 * m_ref[...] + (1.0 - ADAM_B1) * g
        nv = ADAM_B2 * v_ref[...] + (1.0 - ADAM_B2) * jnp.square(g)
        m_hat = nm / (1.0 - ADAM_B1 ** ADAM_STEP)
        v_hat = nv / (1.0 - ADAM_B2 ** ADAM_STEP)
        g_ref[...] = g
        d_ref[...] = -ADAM_LR * (m_hat / (jnp.sqrt(v_hat) + ADAM_EPS) + ADAM_WD * w_ref[...])
        nm_ref[...] = nm
        nv_ref[...] = nv

    row = pl.BlockSpec((tr, cols), lambda l, i: (l * nr + i, 0))

    def parts_spec(own):
        return pl.BlockSpec((n_parts, tr, cols),
                            lambda l, i: (0, jnp.where(l == own, i, jnp.where(l < own, 0, nr - 1)), 0))

    return pl.pallas_call(
        body, name=name, grid=(n_layers, nr), in_specs=[row, row, row] + [parts_spec(l) for l in range(n_layers)],
        out_specs=[row] * 4, out_shape=[jax.ShapeDtypeStruct(w.shape, f32)] * 4, compiler_params=_params(2),
    )(w, m, v, *g_parts)


def _mesh_pos():
    return lax.axis_index("x"), lax.axis_index("y"), lax.axis_index("c")


def _dev_index(p):
    return 4 * p[0] + 2 * p[1] + p[2]


class Exchange:
    def __init__(self, kind, arrays):
        self.kind, self.arrays, self.n = kind, list(arrays), len(arrays)
        self.specs = [pl.BlockSpec(memory_space=pl.ANY)] * self.n
        self.out_shapes = [jax.ShapeDtypeStruct(((N_DEV,) if kind == "gather" else ()) + a.shape, a.dtype)
                           for a in self.arrays]
        self.scratch = [pltpu.SemaphoreType.DMA((self.n, N_DEV - 1)), pltpu.SemaphoreType.DMA((self.n, N_DEV - 1)),
                        pltpu.SemaphoreType.DMA((self.n,))]

    def _copies(self, in_refs, out_refs, sems, with_arrivals):
        send_sems, recv_sems, local_sems = sems
        x, y, c = _mesh_pos()
        mine = _dev_index((x, y, c))

        def src(a, slab):
            return in_refs[a] if self.kind == "gather" else in_refs[a].at[slab]

        local = [pltpu.make_async_copy(src(a, mine), out_refs[a].at[mine], local_sems.at[a]) for a in range(self.n)]
        sends, arrivals = [], []
        for k in range(1, N_DEV):
            peer = (1 - x if k & 4 else x, 1 - y if k & 2 else y, 1 - c if k & 1 else c)
            theirs = _dev_index(peer)
            for a in range(self.n):
                to = dict(send_sem=send_sems.at[a, k - 1], recv_sem=recv_sems.at[a, k - 1], device_id=peer,
                          device_id_type=pl.DeviceIdType.MESH)
                sends.append(pltpu.make_async_remote_copy(src_ref=src(a, theirs), dst_ref=out_refs[a].at[mine], **to))
                if with_arrivals:
                    arrivals.append(pltpu.make_async_remote_copy(src_ref=src(a, theirs),
                                                                 dst_ref=out_refs[a].at[theirs], **to))
        return local, sends, arrivals

    def start(self, in_refs, out_refs, sems):
        local, sends, _ = self._copies(in_refs, out_refs, sems, False)
        for cp in local + sends:
            cp.start()

    def wait(self, in_refs, out_refs, sems):
        local, sends, arrivals = self._copies(in_refs, out_refs, sems, True)
        for cp in arrivals:
            cp.wait_recv()
        for cp in sends:
            cp.wait_send()
        for cp in local:
            cp.wait()

    def run_around(self, grid, in_refs, out_refs, sems, *, before):
        at = None
        for axis, size in enumerate(grid):
            hit = pl.program_id(axis) == (0 if before else size - 1)
            at = hit if at is None else at & hit

        @pl.when(at)
        def _():
            (self.start if before else self.wait)(in_refs, out_refs, sems)


def host_call(side, body, *, name, grid, in_specs, out_specs, out_shape, scratch_shapes, args, aliases):
    n_in, n_out = len(in_specs), len(out_specs)
    if side is None:
        kernel_body = body
    else:
        n = side.n
        in_specs, args = in_specs + side.specs, tuple(args) + tuple(side.arrays)
        out_specs, out_shape = out_specs + side.specs, out_shape + side.out_shapes
        scratch_shapes = scratch_shapes + side.scratch

        def kernel_body(*refs):
            ins, side_in = refs[:n_in], refs[n_in:n_in + n]
            outs, side_out = refs[n_in + n:n_in + n + n_out], refs[n_in + n + n_out:n_in + 2 * n + n_out]
            scratch, sems = refs[n_in + 2 * n + n_out:-3], refs[-3:]
            side.run_around(grid, side_in, side_out, sems, before=True)
            body(*ins, *outs, *scratch)
            side.run_around(grid, side_in, side_out, sems, before=False)

    outs = pl.pallas_call(
        kernel_body, name=name, grid=grid, in_specs=in_specs, out_specs=out_specs, out_shape=out_shape,
        scratch_shapes=scratch_shapes, input_output_aliases=aliases, compiler_params=_params(len(grid)),
    )(*args)
    return outs if side is None else (outs[:n_out], outs[n_out:])


def all_gather(blocks, *, name):
    n = len(blocks)
    any_spec = pl.BlockSpec(memory_space=pl.ANY)

    def body(*refs):
        ins, outs = refs[:n], refs[n:2 * n]
        send_sems, recv_sems, local_sems = refs[2 * n:]
        x, y, c = _mesh_pos()
        me, sibling = (x, y, c), (x, y, 1 - c)
        chips = [(1 - x, y), (x, 1 - y), (1 - x, 1 - y)]

        def copy(a, k, block, to, src=None):
            dst = outs[a].at[_dev_index(block)]
            return pltpu.make_async_remote_copy(
                src_ref=dst if src is None else src, dst_ref=dst, send_sem=send_sems.at[a, k],
                recv_sem=recv_sems.at[a, k], device_id=to, device_id_type=pl.DeviceIdType.MESH)

        mine = [pltpu.make_async_copy(ins[a], outs[a].at[_dev_index(me)], local_sems.at[a]) for a in range(n)]
        for cp in mine:
            cp.start()
        first = []
        for a in range(n):
            first.append(copy(a, 0, me, sibling, src=ins[a]))
            first += [copy(a, 1 + j, me, (*chip, c), src=ins[a]) for j, chip in enumerate(chips)]
        for cp in first:
            cp.start()
        passed = []
        for j, chip in enumerate(chips):
            for a in range(n):
                copy(a, 1 + j, (*chip, c), me).wait_recv()
                fwd = copy(a, 4 + j, (*chip, c), sibling)
                fwd.start()
                passed.append(fwd)
        for a in range(n):
            copy(a, 0, sibling, me).wait_recv()
            for j, chip in enumerate(chips):
                copy(a, 4 + j, (*chip, 1 - c), me).wait_recv()
        for cp in first + passed:
            cp.wait_send()
        for cp in mine:
            cp.wait()

    return pl.pallas_call(
        body, name=name, in_specs=[any_spec] * n, out_specs=[any_spec] * n,
        out_shape=[jax.ShapeDtypeStruct((N_DEV,) + b.shape, b.dtype) for b in blocks],
        scratch_shapes=[pltpu.SemaphoreType.DMA((n, 7)), pltpu.SemaphoreType.DMA((n, 7)),
                        pltpu.SemaphoreType.DMA((n,))],
    )(*blocks)


def exchange(jobs, *, name):
    total = sum(j.n for j in jobs)

    def body(*refs):
        ins, outs, sems = refs[:total], refs[total:2 * total], refs[2 * total:]
        pieces, off = [], 0
        for i, j in enumerate(jobs):
            pieces.append((ins[off:off + j.n], outs[off:off + j.n], sems[3 * i:3 * i + 3]))
            off += j.n
        for j, piece in zip(jobs, pieces):
            j.start(*piece)
        for j, piece in zip(jobs, pieces):
            j.wait(*piece)

    outs = pl.pallas_call(
        body, name=name, in_specs=[s for j in jobs for s in j.specs], out_specs=[s for j in jobs for s in j.specs],
        out_shape=[s for j in jobs for s in j.out_shapes], scratch_shapes=[s for j in jobs for s in j.scratch],
    )(*[a for j in jobs for a in j.arrays])
    split, off = [], 0
    for j in jobs:
        split.append(outs[off:off + j.n])
        off += j.n
    return split


def _rows128(arr):
    flat = arr.reshape(-1)
    rows = -(-flat.shape[0] // (8 * LANES)) * 8
    return jnp.pad(flat, (0, rows * LANES - flat.shape[0])).reshape(rows, LANES)


def _pad_lanes(row, width=LANES, at=0):
    return jnp.pad(row, (at, width - at - row.shape[0])).reshape(1, width)


def _w_in_sections(got):
    d = got.shape[1]
    wi = jnp.transpose(got, (1, 0, 2)).reshape(d, -1)
    w_main = jnp.concatenate([wi[:, :C_B], wi[:, C_Z:C_BA], wi[:, C_QKV:C_Z], wi[:, C_G:], wi[:, C_B:C_QKV]], axis=1)
    return w_main, jnp.pad(wi[:, C_BA:C_G], ((0, 0), (0, LANES - (C_G - C_BA))))


def _w_in_parts(gw_main, gw_ba):
    d = gw_main.shape[0]
    full = jnp.concatenate([gw_main[:, SEC_A:SEC_Z], gw_main[:, SEC_B:], gw_main[:, SEC_QKV:SEC_G],
                            gw_main[:, SEC_Z:SEC_QKV], gw_ba[:, :C_G - C_BA], gw_main[:, SEC_G:SEC_B]], axis=1)
    return jnp.transpose(full.reshape(d, N_DEV, -1), (1, 0, 2))


def kernel(x, positions, attn_norm, w_in, sgu_ln_g, sgu_ln_b, sgu_w, sgu_b, attn_sinks, dn_conv_w, dn_a_log, dn_dt_bias, dn_norm, w_branch, w_out, ffn_norm, w_gate_up, w_down, final_norm, loss_target, m_attn_norm, m_w_in, m_sgu_ln_g, m_sgu_ln_b, m_sgu_w, m_sgu_b, m_attn_sinks, m_dn_conv_w, m_dn_a_log, m_dn_dt_bias, m_dn_norm, m_w_branch, m_w_out, m_ffn_norm, m_w_gate_up, m_w_down, m_final_norm, v_attn_norm, v_w_in, v_sgu_ln_g, v_sgu_ln_b, v_sgu_w, v_sgu_b, v_attn_sinks, v_dn_conv_w, v_dn_a_log, v_dn_dt_bias, v_dn_norm, v_w_branch, v_w_out, v_ffn_norm, v_w_gate_up, v_w_down, v_final_norm):
    given = dict(locals())
    depth, d_model = attn_norm.shape
    s = x.shape[1]
    x2 = x.reshape(s, d_model)
    target = loss_target.reshape(s, d_model)
    posf = positions.reshape(s, 1).astype(f32)
    inv_freq = ROPE_THETA ** (-jnp.arange(0, ROPE_DIM, 2, dtype=f32) / ROPE_DIM)
    inv_head = jnp.concatenate([inv_freq, inv_freq, jnp.zeros((SWA_HD - ROPE_DIM,), f32)])
    tables = rope_tables(posf, jnp.tile(inv_head, LANES // SWA_HD).reshape(1, LANES), name="rope_tables")

    assert depth == 2, depth
    gathered = dict(zip([("w_in", 0), ("dn_conv_w", 0), ("dn_conv_w", 1)], all_gather(
        [w_in[0].astype(bf16), dn_conv_w[0], dn_conv_w[1]], name="gather_first")))
    riders = {"l0_in_proj": [("w_branch", 0), ("w_out", 0)],
              "l0_deltanet": [("w_gate_up", 0), ("w_down", 0), ("w_in", 1)],
              "l1_deltanet": [("w_branch", 1), ("w_out", 1), ("w_gate_up", 1), ("w_down", 1)]}

    def gathering(host, call, *args, **kw):
        if host not in riders:
            return call(*args, name=host, **kw)
        side = Exchange("gather", [given[n][l].astype(bf16) for n, l in riders[host]])
        out, got = call(*args, side=side, name=host, **kw)
        gathered.update(zip(riders[host], got))
        return out

    layers, saved = [], []
    h_in = x2
    for l in range(depth):
        t = f"l{l}_"
        w_main, w_ba = _w_in_sections(gathered["w_in", l])
        conv_full = jnp.transpose(gathered["dn_conv_w", l], (1, 0, 2)).reshape(DN_CONV, -1)
        p = dict(
            w_main=w_main, w_ba=w_ba, conv_w8=jnp.pad(conv_full, ((0, CONV_PAD - DN_CONV), (0, 0))),
            attn_norm=attn_norm[l].reshape(1, -1), ffn_norm=ffn_norm[l].reshape(1, -1),
            ln_g=sgu_ln_g[l].reshape(1, -1), ln_b=sgu_ln_b[l].reshape(1, -1), sgu_w=sgu_w[l], sgu_bt=sgu_b[l].T,
            sink_row=_pad_lanes(attn_sinks[l]), alog_row=_pad_lanes(dn_a_log[l], at=DN_HEADS),
            dtb_row=_pad_lanes(dn_dt_bias[l], at=DN_HEADS), norm_row=dn_norm[l].reshape(1, -1))
        layers.append(p)
        h = rmsnorm_fwd(h_in, p["attn_norm"], name=t + "attn_norm")
        proj = gathering(t + "in_proj", matmul, h, p["w_main"], "nn", f32)
        p_ba = matmul(h, p["w_ba"], "nn", f32, name=t + "in_proj_ba")
        out_a = sgu_fwd(proj, p["ln_g"], p["ln_b"], p["sgu_w"], p["sgu_bt"], name=t + "sgu")
        out_b = swa_fwd(proj, p["sink_row"], tables, name=t + "swa")
        xc = conv_fwd(proj, p["conv_w8"], name=t + "dn_conv")
        out_c, states, tri = gathering(t + "deltanet", dn_fwd, xc, proj, p_ba, p["alog_row"], p["dtb_row"], p["norm_row"])
        p.update(w_branch=jnp.transpose(gathered["w_branch", l], (1, 2, 0, 3)).reshape(3, MIX, d_model),
                 w_out=gathered["w_out", l].reshape(d_model, d_model),
                 w_gu=gathered["w_gate_up", l],
                 w_down=gathered["w_down", l].reshape(N_DEV // 2, -1, d_model))
        merged = merge_fwd(out_a, out_b, out_c, proj, p["w_branch"], name=t + "merge")
        x_mid = matmul(merged, p["w_out"], "nn", f32, residual=h_in, name=t + "out_proj")
        h2 = rmsnorm_fwd(x_mid, p["ffn_norm"], name=t + "ffn_norm")
        gu = matmul(h2, p["w_gu"], "nn", bf16, group="n", name=t + "gate_up")
        act = swiglu_fwd(gu, name=t + "swiglu")
        x_out = matmul(act, p["w_down"], "nn", f32, residual=x_mid, group="k", name=t + "down")
        saved.append(dict(x_in=h_in, h=h, proj=proj, p_ba=p_ba, out_a=out_a, out_b=out_b, out_c=out_c, xc=xc,
                          states=states, tri=tri, merged=merged, x_mid=x_mid, h2=h2, gu=gu, act=act))
        h_in = x_out

    dx, d_final_norm, loss_row = loss_head(h_in, final_norm.reshape(1, -1), target, name="loss_head")
    loss = lax.psum(loss_row[0, 0], MESH_AXES)

    shard_names = ["w_in", "dn_conv_w", "w_branch", "w_out", "w_gate_up", "w_down"]
    rep_names = ["attn_norm", "sgu_ln_g", "sgu_ln_b", "sgu_w", "sgu_b", "attn_sinks", "dn_a_log", "dn_dt_bias",
                 "dn_norm", "ffn_norm"]
    parts, received, per_layer = {}, {}, []
    senders = {"l0_b_swa": [("w_gate_up", 1), ("w_down", 1), ("w_out", 1), ("w_branch", 1)],
               "l0_b_deltanet": [("w_in", 1), ("w_gate_up", 0), ("w_down", 0), ("w_out", 0), ("w_branch", 0)]}

    def scattering(host, call, *args):
        if host not in senders:
            return call(*args, name=host)
        out, got = call(*args, side=Exchange("scatter", [parts[key] for key in senders[host]]), name=host)
        received.update(zip(senders[host], got))
        return out

    for l in reversed(range(depth)):
        p, sv, t = layers[l], saved[l], f"l{l}_b_"
        d_act = matmul(dx, p["w_down"], "nt", f32, group="n", name=t + "down_dx")
        gw_down = matmul(sv["act"], dx, "tn", bf16, group="m", name=t + "down_dw")
        d_gu = swiglu_bwd(sv["gu"], d_act, name=t + "swiglu")
        gw_gu = matmul(sv["h2"], d_gu, "tn", bf16, group="n", name=t + "gate_up_dw")
        d_h2 = matmul(d_gu, p["w_gu"], "nt", f32, group="k", name=t + "gate_up_dx")
        dx_mid, g_ffn = rmsnorm_bwd(sv["x_mid"], p["ffn_norm"], d_h2, dx, name=t + "ffn_norm")
        d_merged = matmul(dx_mid, p["w_out"], "nt", f32, name=t + "out_proj_dx")
        gw_out = matmul(sv["merged"], dx_mid, "tn", bf16, name=t + "out_proj_dw")
        dproj = lax.empty((s, W_MAIN), bf16)
        dproj, d_a, d_b, d_c, gw_branch = merge_bwd(sv["out_a"], sv["out_b"], sv["out_c"], sv["proj"], p["w_branch"],
                                                   d_merged, dproj, name=t + "merge")
        parts.update({("w_gate_up", l): gw_gu, ("w_down", l): gw_down.reshape(N_DEV, -1, d_model),
                      ("w_out", l): gw_out.reshape(N_DEV, -1, d_model),
                      ("w_branch", l): jnp.transpose(gw_branch.reshape(3, MIX, N_DEV, -1), (2, 0, 1, 3)).astype(bf16)})
        dproj, g_ln_g, g_ln_b, g_sgu_w, g_sgu_bt = sgu_bwd(sv["proj"], p["ln_g"], p["ln_b"], p["sgu_w"], p["sgu_bt"],
                                                         d_a, dproj, name=t + "sgu")
        dproj, g_sink = scattering(t + "swa", swa_bwd, sv["proj"], p["sink_row"], tables, d_b, dproj)
        dproj, dxc, dba, g_alog, g_dtb, g_dnorm = scattering(
            t + "deltanet", dn_bwd, sv["xc"], sv["proj"], sv["p_ba"], p["alog_row"], p["dtb_row"], p["norm_row"],
            sv["states"], sv["tri"], d_c, dproj)
        dproj, g_conv8 = conv_bwd(sv["proj"], p["conv_w8"], dxc, dproj, name=t + "dn_conv")
        gw_main = matmul(sv["h"], dproj, "tn", bf16, name=t + "in_proj_dw")
        gw_ba = matmul(sv["h"], dba, "tn", bf16, name=t + "in_proj_ba_dw")
        parts["w_in", l] = _w_in_parts(gw_main, gw_ba)
        d_h = matmul(dproj, p["w_main"], "nt", f32, tk=768, name=t + "in_proj_dx")
        d_h = matmul(dba, p["w_ba"], "nt", f32, residual=d_h, name=t + "in_proj_ba_dx")
        dx, g_attn = rmsnorm_bwd(sv["x_in"], p["attn_norm"], d_h, dx_mid, name=t + "attn_norm")
        per_layer.append(dict(
            dn_conv_w=jnp.transpose(g_conv8[:DN_CONV].reshape(DN_CONV, N_DEV, -1), (1, 0, 2)),
            attn_norm=g_attn, sgu_ln_g=g_ln_g, sgu_ln_b=g_ln_b, sgu_w=g_sgu_w, sgu_b=g_sgu_bt.T,
            attn_sinks=g_sink[0, :SWA_HEADS], dn_a_log=g_alog[0, DN_HEADS:2 * DN_HEADS],
            dn_dt_bias=g_dtb[0, DN_HEADS:2 * DN_HEADS], dn_norm=g_dnorm, ffn_norm=g_ffn))
    per_layer.reverse()

    conv_parts = jnp.concatenate([pp["dn_conv_w"] for pp in per_layer], axis=1)
    rep_grads = {n: jnp.stack([pp[n].reshape(given[n].shape[1:]) for pp in per_layer]) for n in rep_names}
    rep_grads["final_norm"] = d_final_norm[0]
    rep_names = rep_names + ["final_norm"]
    rep_rows = [_rows128(given[n]).shape[0] for n in rep_names]
    pad_rows = -sum(rep_rows) % 16

    def small_rows(values):
        pieces = [_rows128(values[n]) for n in rep_names]
        return jnp.concatenate(pieces + [jnp.zeros((pad_rows, LANES), f32)], axis=0)

    (got_w_in, got_conv), (small_all,) = exchange(
        [Exchange("scatter", [parts["w_in", 0], conv_parts]), Exchange("gather", [small_rows(rep_grads)])],
        name="exchange_last")
    received["w_in", 0] = got_w_in

    results = [{}, {}, {}, {}]
    for n in shard_names:
        shp = given[n].shape
        two = (-1, shp[-1])
        by_layer = [got_conv] if n == "dn_conv_w" else [received[n, l].reshape(N_DEV, -1, shp[-1]) for l in range(depth)]
        outs = adamw(given[n].reshape(two), given["m_" + n].reshape(two), given["v_" + n].reshape(two), by_layer,
                     name="adamw_" + n)
        for res, val in zip(results, outs):
            res[n] = val.reshape(shp)
    outs = adamw(small_rows(given), small_rows({n: given["m_" + n] for n in rep_names}),
                 small_rows({n: given["v_" + n] for n in rep_names}), [small_all], name="adamw_replicated")
    for res, val in zip(results, outs):
        row = 0
        for n, nr in zip(rep_names, rep_rows):
            res[n] = val[row:row + nr].reshape(-1)[:given[n].size].reshape(given[n].shape)
            row += nr
    order = ["attn_norm", "w_in", "sgu_ln_g", "sgu_ln_b", "sgu_w", "sgu_b", "attn_sinks", "dn_conv_w", "dn_a_log",
             "dn_dt_bias", "dn_norm", "w_branch", "w_out", "ffn_norm", "w_gate_up", "w_down", "final_norm"]
    return (loss, dx.reshape(x.shape), *[res[n] for res in results for n in order])
```

```python
import functools

import jax
import jax.numpy as jnp
from jax import lax
from jax.experimental import pallas as pl
from jax.experimental.pallas import tpu as pltpu

f32 = jnp.float32
bf16 = jnp.bfloat16

N_DEV = 8
MESH_AXES = ("x", "y", "c")
NORM_EPS = 1e-6
MIX = 512
SGU_GROUPS, SGU_CHUNK = 4, 128
SWA_HEADS, SWA_KV, SWA_HD, WINDOW = 8, 2, 64, 128
ROPE_THETA, ROPE_DIM = 500000.0, 16
DN_HEADS, DN_HD, DN_CONV, DN_CHUNK = 4, 128, 4, 64
ADAM_LR, ADAM_B1, ADAM_B2, ADAM_EPS, ADAM_WD, ADAM_STEP = 0.001, 0.9, 0.999, 1e-08, 0.01, 10

LANES = 128
VMEM_LIMIT = 56 * 1024 * 1024

SEC_A, SEC_Z, SEC_QKV, SEC_G, SEC_B = 0, 1024, 1536, 3072, 6144
W_MAIN = 6912
C_B, C_QKV, C_Z, C_BA, C_G = 1024, 1792, 3328, 3840, 3848


def _params(n_axes, **kw):
    return pltpu.CompilerParams(dimension_semantics=("arbitrary",) * n_axes, vmem_limit_bytes=VMEM_LIMIT, **kw)


def _tile(n, target, mult=LANES):
    if n <= target:
        return n
    best = None
    for t in range(mult, target + 1, mult):
        if n % t == 0:
            best = t
    assert best is not None, (n, target, mult)
    return best


def _dg(a, b, ca, cb):
    return lax.dot_general(a.astype(bf16), b.astype(bf16), (((ca,), (cb,)), ((), ())), preferred_element_type=f32)


def _dg3(a, b, ca, cb):
    a_hi, b_hi = a.astype(bf16), b.astype(bf16)
    a_lo, b_lo = (a - a_hi.astype(f32)).astype(bf16), (b - b_hi.astype(f32)).astype(bf16)

    def dot(p, q):
        return lax.dot_general(p, q, (((ca,), (cb,)), ((), ())), preferred_element_type=f32)

    return dot(a_hi, b_hi) + (dot(a_hi, b_lo) + dot(a_lo, b_hi))


def _differentiable_dot(core):
    @functools.partial(jax.custom_vjp, nondiff_argnums=(2, 3))
    def dot(a, b, ca, cb):
        return core(a, b, ca, cb)

    def fwd(a, b, ca, cb):
        return core(a, b, ca, cb), (a, b)

    def bwd(ca, cb, res, ct):
        a, b = res
        da = core(ct, b, 1, 1 - cb) if ca == 1 else core(b, ct, 1 - cb, 1)
        db = core(a, ct, 1 - ca, 0) if cb == 0 else core(ct, a, 0, 1 - ca)
        return da, db

    dot.defvjp(fwd, bwd)
    return dot


bdot = _differentiable_dot(_dg)
_hdot = _differentiable_dot(_dg3)


def hdot(a, b, ca=1, cb=0):
    return _hdot(a, b, ca, cb)


@functools.partial(jax.custom_vjp, nondiff_argnums=(1,))
def lroll(x, shift):
    return pltpu.roll(x, shift, 1)


def _lroll_fwd(x, shift):
    return pltpu.roll(x, shift, 1), None


def _lroll_bwd(shift, _, ct):
    return (pltpu.roll(ct, ct.shape[1] - shift, 1),)


lroll.defvjp(_lroll_fwd, _lroll_bwd)


@functools.partial(jax.custom_vjp, nondiff_argnums=(1,))
def tri_inv(low, nil):
    n = low.shape[0]
    row = lax.broadcasted_iota(jnp.int32, (n, n), 0)
    col = lax.broadcasted_iota(jnp.int32, (n, n), 1)
    eye = (row == col).astype(f32)
    m = -low
    p = eye + m
    span = 2
    while span < nil:
        m = hdot(m, m)
        p = p + hdot(p, m)
        span *= 2
    return p


def _tri_inv_fwd(low, nil):
    t = tri_inv(low, nil)
    return t, t


def _tri_inv_bwd(nil, t, dt):
    return (-hdot(t, hdot(dt, t, 1, 1), 0, 0),)


tri_inv.defvjp(_tri_inv_fwd, _tri_inv_bwd)


@jax.custom_vjp
def tri_inv_known(low, t):
    return t


def _tri_inv_known_fwd(low, t):
    return t, t


def _tri_inv_known_bwd(t, dt):
    return _tri_inv_bwd(None, t, dt) + (jnp.zeros_like(t),)


tri_inv_known.defvjp(_tri_inv_known_fwd, _tri_inv_known_bwd)


def _sigmoid(x):
    return 1.0 / (1.0 + jnp.exp(-x))


def _rms(x, g):
    return x * lax.rsqrt(jnp.mean(x * x, axis=-1, keepdims=True) + NORM_EPS) * g


def _lane_col(x, lane_idx):
    lane = lax.broadcasted_iota(jnp.int32, x.shape, 1)
    return jnp.sum(jnp.where(lane == lane_idx, x, 0.0), axis=1, keepdims=True)


def matmul(a, b, mode, out_dtype, *, residual=None, group=None, side=None, tm=1024, tn=768, tk=1024, name):
    dims = {"a": ("m", "k") if mode != "tn" else ("k", "m"),
            "b": {"nn": ("k", "n"), "nt": ("n", "k"), "tn": ("k", "n")}[mode], "o": ("m", "n")}
    full, groups = {}, 1
    for arr, key in ((a, "a"), (b, "b")):
        grouped = group in dims[key]
        if grouped:
            groups = arr.shape[0]
        full[dims[key][0]], full[dims[key][1]] = arr.shape[1:] if grouped else arr.shape
    want = {"m": tm, "n": tn, "k": tk}
    per_step = min(groups, 4) if group == "k" else 1
    tiles = {d: full[d] if d == group else _tile(full[d], want[d]) for d in "mnk"}
    steps = {d: groups // per_step if d == group else full[d] // tiles[d] for d in "mnk"}

    def spec(key):
        d0, d1 = dims[key]

        def index(i, j, kk):
            at = {"m": i, "n": j, "k": kk}
            if group in (d0, d1):
                return (at[group], 0 if d0 == group else at[d0], 0 if d1 == group else at[d1])
            return (at[d0], at[d1])

        block = (tiles[d0], tiles[d1])
        if group in (d0, d1):
            block = ((per_step if group == "k" else None),) + block
        return pl.BlockSpec(block, index)

    ca, cb = {"nn": (1, 0), "nt": (1, 1), "tn": (0, 0)}[mode]
    nk = steps["k"]
    o_spec = spec("o")
    out_shape = (groups, full["m"], full["n"]) if group in ("m", "n") else (full["m"], full["n"])
    has_res = residual is not None

    def product(a_ref, b_ref):
        if group != "k":
            return _dg(a_ref[...], b_ref[...], ca, cb)
        total = _dg(a_ref[0], b_ref[0], ca, cb)
        for g in range(1, per_step):
            total = total + _dg(a_ref[g], b_ref[g], ca, cb)
        return total

    def body(a_ref, b_ref, *rest):
        r_ref = rest[0] if has_res else None
        o_ref = rest[1 if has_res else 0]

        def emit(acc):
            o_ref[...] = (acc + r_ref[...] if has_res else acc).astype(out_dtype)

        if nk == 1:
            emit(product(a_ref, b_ref))
            return
        acc_ref = rest[-1]
        kk = pl.program_id(2)

        @pl.when(kk == 0)
        def _():
            acc_ref[...] = jnp.zeros_like(acc_ref)

        acc_ref[...] += product(a_ref, b_ref)

        @pl.when(kk == nk - 1)
        def _():
            emit(acc_ref[...])

    res = host_call(
        side, body, name=name, grid=(steps["m"], steps["n"], nk),
        in_specs=[spec("a"), spec("b")] + ([o_spec] if has_res else []), out_specs=[o_spec],
        out_shape=[jax.ShapeDtypeStruct(out_shape, out_dtype)],
        scratch_shapes=[pltpu.VMEM((tiles["m"], tiles["n"]), f32)] if nk > 1 else [], aliases={},
        args=(a, b) + ((residual,) if has_res else ()))
    return res[0] if side is None else (res[0][0], res[1])


def rmsnorm_fwd(x, g_row, *, name):
    s, d = x.shape
    ts = _tile(s, 512, 16)

    def body(x_ref, g_ref, o_ref):
        o_ref[...] = _rms(x_ref[...], g_ref[...]).astype(bf16)

    return pl.pallas_call(
        body, name=name, grid=(s // ts,),
        in_specs=[pl.BlockSpec((ts, d), lambda i: (i, 0)), pl.BlockSpec((1, d), lambda i: (0, 0))],
        out_specs=pl.BlockSpec((ts, d), lambda i: (i, 0)), out_shape=jax.ShapeDtypeStruct((s, d), bf16),
        compiler_params=_params(1),
    )(x, g_row)


def rmsnorm_bwd(x, g_row, dh, dres, *, name):
    s, d = x.shape
    ts = _tile(s, 512, 16)

    def body(x_ref, g_ref, dh_ref, dres_ref, dx_ref, dg_ref):
        @pl.when(pl.program_id(0) == 0)
        def _():
            dg_ref[...] = jnp.zeros_like(dg_ref)

        _, vjp = jax.vjp(_rms, x_ref[...], g_ref[...])
        dx, dg = vjp(dh_ref[...])
        dx_ref[...] = dx + dres_ref[...]
        dg_ref[...] += dg

    row = pl.BlockSpec((ts, d), lambda i: (i, 0))
    vec = pl.BlockSpec((1, d), lambda i: (0, 0))
    return pl.pallas_call(
        body, name=name, grid=(s // ts,), in_specs=[row, vec, row, row], out_specs=[row, vec],
        out_shape=[jax.ShapeDtypeStruct((s, d), f32), jax.ShapeDtypeStruct((1, d), f32)],
        compiler_params=_params(1),
    )(x, g_row, dh, dres)


def loss_head(x, g_row, target, *, name):
    s, d = x.shape
    ts = _tile(s, 512, 16)

    def body(x_ref, g_ref, t_ref, dx_ref, dg_ref, loss_ref):
        @pl.when(pl.program_id(0) == 0)
        def _():
            dg_ref[...] = jnp.zeros_like(dg_ref)
            loss_ref[...] = jnp.zeros_like(loss_ref)

        y, vjp = jax.vjp(_rms, x_ref[...], g_ref[...])
        err = y - t_ref[...]
        dx, dg = vjp(err * (1.0 / d))
        dx_ref[...] = dx
        dg_ref[...] += dg
        loss_ref[...] += 0.5 * jnp.sum(jnp.sum(err * err, axis=1, keepdims=True) * (1.0 / d), axis=0, keepdims=True)

    row = pl.BlockSpec((ts, d), lambda i: (i, 0))
    vec = pl.BlockSpec((1, d), lambda i: (0, 0))
    one = pl.BlockSpec((1, LANES), lambda i: (0, 0))
    return pl.pallas_call(
        body, name=name, grid=(s // ts,), in_specs=[row, vec, row], out_specs=[row, vec, one],
        out_shape=[jax.ShapeDtypeStruct((s, d), f32), jax.ShapeDtypeStruct((1, d), f32),
                   jax.ShapeDtypeStruct((1, LANES), f32)],
        compiler_params=_params(1),
    )(x, g_row, target)


def _sgu_chunk(p_a, ln_g, ln_b, w, b_t):
    t = SGU_CHUNK
    u = jax.nn.gelu(p_a[:, :MIX])
    v = jax.nn.gelu(p_a[:, MIX:])
    vc = v - jnp.mean(v, axis=-1, keepdims=True)
    vn = vc * lax.rsqrt(jnp.mean(vc * vc, axis=-1, keepdims=True) + NORM_EPS) * ln_g + ln_b
    causal = lax.broadcasted_iota(jnp.int32, (t, t), 0) >= lax.broadcasted_iota(jnp.int32, (t, t), 1)
    outs = []
    for g in range(SGU_GROUPS):
        sl = slice(g * LANES, (g + 1) * LANES)
        mixed = bdot(jnp.where(causal, w[g], 0.0), vn[:, sl], 1, 0) + b_t[:, g:g + 1]
        outs.append(u[:, sl] * mixed)
    return jnp.concatenate(outs, axis=1)


def _sgu_specs(s, ts):
    return [pl.BlockSpec((ts, 2 * MIX), lambda i: (i, SEC_A // (2 * MIX))),
            pl.BlockSpec((1, MIX), lambda i: (0, 0)), pl.BlockSpec((1, MIX), lambda i: (0, 0)),
            pl.BlockSpec((SGU_GROUPS, SGU_CHUNK, SGU_CHUNK), lambda i: (0, 0, 0)),
            pl.BlockSpec((SGU_CHUNK, SGU_GROUPS), lambda i: (0, 0))]


def sgu_fwd(proj, ln_g, ln_b, w, b_t, *, name):
    s = proj.shape[0]
    ts = _tile(s, 512)
    n_chunk = ts // SGU_CHUNK

    def body(p_ref, g_ref, b_ref, w_ref, bt_ref, o_ref):
        def step(c, carry):
            rows = pl.ds(pl.multiple_of(c * SGU_CHUNK, SGU_CHUNK), SGU_CHUNK)
            o_ref[rows, :] = _sgu_chunk(p_ref[rows, :], g_ref[...], b_ref[...], w_ref[...], bt_ref[...]).astype(bf16)
            return carry
        lax.fori_loop(0, n_chunk, step, 0)

    return pl.pallas_call(
        body, name=name, grid=(s // ts,), in_specs=_sgu_specs(s, ts),
        out_specs=pl.BlockSpec((ts, MIX), lambda i: (i, 0)), out_shape=jax.ShapeDtypeStruct((s, MIX), bf16),
        compiler_params=_params(1),
    )(proj, ln_g, ln_b, w, b_t)


def sgu_bwd(proj, ln_g, ln_b, w, b_t, d_out, dproj, *, name):
    s = proj.shape[0]
    ts = _tile(s, 512)
    n_chunk = ts // SGU_CHUNK

    def body(p_ref, g_ref, b_ref, w_ref, bt_ref, do_ref, _, dp_ref, dg_ref, db_ref, dw_ref, dbt_ref):
        @pl.when(pl.program_id(0) == 0)
        def _():
            dg_ref[...] = jnp.zeros_like(dg_ref)
            db_ref[...] = jnp.zeros_like(db_ref)
            dw_ref[...] = jnp.zeros_like(dw_ref)
            dbt_ref[...] = jnp.zeros_like(dbt_ref)

        def step(c, carry):
            rows = pl.ds(pl.multiple_of(c * SGU_CHUNK, SGU_CHUNK), SGU_CHUNK)
            _, vjp = jax.vjp(_sgu_chunk, p_ref[rows, :], g_ref[...], b_ref[...], w_ref[...], bt_ref[...])
            dp, dg, db, dw, dbt = vjp(do_ref[rows, :])
            dp_ref[rows, :] = dp.astype(bf16)
            dg_ref[...] += dg
            db_ref[...] += db
            dw_ref[...] += dw
            dbt_ref[...] += dbt
            return carry
        lax.fori_loop(0, n_chunk, step, 0)

    specs = _sgu_specs(s, ts)
    return pl.pallas_call(
        body, name=name, grid=(s // ts,),
        in_specs=specs + [pl.BlockSpec((ts, MIX), lambda i: (i, 0)), pl.BlockSpec(memory_space=pl.ANY)],
        out_specs=[specs[0], specs[1], specs[2], specs[3], specs[4]],
        out_shape=[jax.ShapeDtypeStruct(dproj.shape, bf16), jax.ShapeDtypeStruct((1, MIX), f32),
                   jax.ShapeDtypeStruct((1, MIX), f32), jax.ShapeDtypeStruct(w.shape, f32),
                   jax.ShapeDtypeStruct(b_t.shape, f32)],
        input_output_aliases={6: 0}, compiler_params=_params(1),
    )(proj, ln_g, ln_b, w, b_t, d_out, dproj)


def rope_tables(posf, inv_freq, *, name):
    s = posf.shape[0]
    ts = _tile(s, 1024, 8)
    half = ROPE_DIM // 2

    def body(pos_ref, inv_ref, o_ref):
        d = lax.broadcasted_iota(jnp.int32, (1, LANES), 1) % SWA_HD
        ang = pos_ref[...] * inv_ref[...]
        sin = jnp.sin(ang)
        o_ref[0] = jnp.cos(ang)
        o_ref[1] = jnp.where(d < half, sin, 0.0)
        o_ref[2] = jnp.where((d >= half) & (d < ROPE_DIM), sin, 0.0)

    return pl.pallas_call(
        body, name=name, grid=(s // ts,),
        in_specs=[pl.BlockSpec((ts, 1), lambda i: (i, 0)), pl.BlockSpec((1, LANES), lambda i: (0, 0))],
        out_specs=pl.BlockSpec((3, ts, LANES), lambda i: (0, i, 0)), out_shape=jax.ShapeDtypeStruct((3, s, LANES), f32),
        compiler_params=_params(1),
    )(posf, inv_freq)


def _rope(x, table):
    w = x.shape[1]
    half = ROPE_DIM // 2
    c, lo, hi = (jnp.concatenate([table[i]] * (w // LANES), axis=1) for i in range(3))
    return x * c - lroll(x, w - half) * lo + lroll(x, half) * hi


def _swa_block(q, kp, kc, vp, vc, sink_row, table_q, table_p, prev_ok):
    t = WINDOW
    q = _rope(q, table_q) * (SWA_HD ** -0.5)
    keys = jnp.concatenate([_rope(kp, table_p), _rope(kc, table_q)], axis=0)
    vals = jnp.concatenate([vp, vc], axis=0)
    own = lax.broadcasted_iota(jnp.int32, (t, t), 0) >= lax.broadcasted_iota(jnp.int32, (t, t), 1)
    lane_half = lax.broadcasted_iota(jnp.int32, (t, LANES), 1) // SWA_HD
    group = SWA_HEADS // SWA_KV
    slabs = []
    for pair in range(SWA_HEADS // 2):
        q_pair = q[:, pair * LANES:(pair + 1) * LANES]
        acc = jnp.zeros((t, LANES), f32)
        for half in range(2):
            h = 2 * pair + half
            kv = h // group
            qm = jnp.where(lane_half == half, q_pair, 0.0)
            if half != kv:
                qm = lroll(qm, SWA_HD)
            both = bdot(qm, keys, 1, 1)
            logits = jnp.where(own, both[:, t:], jnp.where(prev_ok, both[:, :t], -1e30))
            sink = _lane_col(sink_row, h)
            m = jnp.maximum(jnp.max(logits, axis=1, keepdims=True), sink)
            p = jnp.exp(logits - m)
            probs = p * (1.0 / (jnp.sum(p, axis=1, keepdims=True) + jnp.exp(sink - m)))
            spread = jnp.concatenate([jnp.where(own, 0.0, probs), jnp.where(own, probs, 0.0)], axis=1)
            o = jnp.where(lane_half == kv, bdot(spread, vals, 1, 0), 0.0)
            if half != kv:
                o = lroll(o, SWA_HD)
            acc = acc + o
        slabs.append(acc)
    return jnp.concatenate(slabs, axis=1)


def _swa_in_specs(nc, clamp):
    t = WINDOW
    qb, kb, vb = SEC_B // MIX, (SEC_B + MIX) // LANES, (SEC_B + MIX + LANES) // LANES

    def cur(i):
        return jnp.minimum(i, nc - 1) if clamp else i

    def prev(i):
        return jnp.maximum(cur(i) - 1, 0)

    return [pl.BlockSpec((t, MIX), lambda i: (cur(i), qb)),
            pl.BlockSpec((t, LANES), lambda i: (prev(i), kb)), pl.BlockSpec((t, LANES), lambda i: (cur(i), kb)),
            pl.BlockSpec((t, LANES), lambda i: (prev(i), vb)), pl.BlockSpec((t, LANES), lambda i: (cur(i), vb)),
            pl.BlockSpec((1, LANES), lambda i: (0, 0)),
            pl.BlockSpec((3, t, LANES), lambda i: (0, cur(i), 0)), pl.BlockSpec((3, t, LANES), lambda i: (0, prev(i), 0))]


def swa_fwd(proj, sink_row, tables, *, name):
    s = proj.shape[0]
    nc = s // WINDOW

    def body(q_ref, kp_ref, kc_ref, vp_ref, vc_ref, sink_ref, tq_ref, tp_ref, o_ref):
        prev_ok = pl.program_id(0) > 0
        o_ref[...] = _swa_block(q_ref[...], kp_ref[...], kc_ref[...], vp_ref[...], vc_ref[...], sink_ref[...],
                                tq_ref[...], tp_ref[...], prev_ok).astype(bf16)

    return pl.pallas_call(
        body, name=name, grid=(nc,), in_specs=_swa_in_specs(nc, False),
        out_specs=pl.BlockSpec((WINDOW, MIX), lambda i: (i, 0)), out_shape=jax.ShapeDtypeStruct((s, MIX), bf16),
        compiler_params=_params(1),
    )(proj, proj, proj, proj, proj, sink_row, tables, tables)


def swa_bwd(proj, sink_row, tables, d_out, dproj, *, side=None, name):
    s = proj.shape[0]
    nc = s // WINDOW
    t = WINDOW

    def body(q_ref, kp_ref, kc_ref, vp_ref, vc_ref, sink_ref, tq_ref, tp_ref, do_ref, _,
             dp_ref, dsink_ref, cq_ref, ck_ref, cv_ref):
        i = pl.program_id(0)

        @pl.when(i == 0)
        def _():
            dsink_ref[...] = jnp.zeros_like(dsink_ref)

        def write(dk_prev, dv_prev):
            dp_ref[:, :MIX] = cq_ref[...].astype(bf16)
            dp_ref[:, MIX:MIX + LANES] = (ck_ref[...] + dk_prev).astype(bf16)
            dp_ref[:, MIX + LANES:] = (cv_ref[...] + dv_prev).astype(bf16)

        @pl.when(i < nc)
        def _():
            fn = functools.partial(_swa_block, table_q=tq_ref[...], table_p=tp_ref[...], prev_ok=i > 0)
            _, vjp = jax.vjp(fn, q_ref[...], kp_ref[...], kc_ref[...], vp_ref[...], vc_ref[...], sink_ref[...])
            dq, dkp, dkc, dvp, dvc, dsink = vjp(do_ref[...])
            dsink_ref[...] += dsink

            @pl.when(i > 0)
            def _():
                write(dkp, dvp)

            cq_ref[...] = dq
            ck_ref[...] = dkc
            cv_ref[...] = dvc

        @pl.when(i == nc)
        def _():
            write(0.0, 0.0)

    return host_call(
        side, body, name=name, grid=(nc + 1,),
        in_specs=_swa_in_specs(nc, True) + [pl.BlockSpec((t, MIX), lambda i: (jnp.minimum(i, nc - 1), 0)),
                                            pl.BlockSpec(memory_space=pl.ANY)],
        out_specs=[pl.BlockSpec((t, MIX + 2 * LANES), lambda i: (jnp.maximum(i - 1, 0), SEC_B // (MIX + 2 * LANES))),
                   pl.BlockSpec((1, LANES), lambda i: (0, 0))],
        out_shape=[jax.ShapeDtypeStruct(dproj.shape, bf16), jax.ShapeDtypeStruct((1, LANES), f32)],
        scratch_shapes=[pltpu.VMEM((t, MIX), f32), pltpu.VMEM((t, LANES), f32), pltpu.VMEM((t, LANES), f32)],
        aliases={9: 0}, args=(proj, proj, proj, proj, proj, sink_row, tables, tables, d_out, dproj))


CONV_PAD = 8


def _conv_pre(xp, w, rows):
    off = CONV_PAD - (DN_CONV - 1)
    pre = xp[off:off + rows] * w[0:1]
    for i in range(1, DN_CONV):
        pre = pre + xp[off + i:off + i + rows] * w[i:i + 1]
    return pre


def conv_fwd(proj, conv_w8, *, name):
    s = proj.shape[0]
    wq = 3 * MIX
    ts = _tile(s, 512)
    nb = ts // CONV_PAD

    def body(x_ref, prev_ref, w_ref, o_ref):
        prev = jnp.where(pl.program_id(0) > 0, prev_ref[...], 0.0)
        pre = _conv_pre(jnp.concatenate([prev, x_ref[...]], axis=0), w_ref[...], ts)
        o_ref[...] = pre * _sigmoid(pre)

    return pl.pallas_call(
        body, name=name, grid=(s // ts,),
        in_specs=[pl.BlockSpec((ts, wq), lambda i: (i, SEC_QKV // wq)),
                  pl.BlockSpec((CONV_PAD, wq), lambda i: (jnp.maximum(i * nb - 1, 0), SEC_QKV // wq)),
                  pl.BlockSpec((CONV_PAD, wq), lambda i: (0, 0))],
        out_specs=pl.BlockSpec((ts, wq), lambda i: (i, 0)), out_shape=jax.ShapeDtypeStruct((s, wq), f32),
        compiler_params=_params(1),
    )(proj, proj, conv_w8)


def conv_bwd(proj, conv_w8, dxc, dproj, *, name):
    s = proj.shape[0]
    wq = 3 * MIX
    ts = _tile(s, 512)
    nb = ts // CONV_PAD
    nt = s // ts
    last_blk = s // CONV_PAD - 1

    def body(x_ref, prev_ref, next_ref, w_ref, d_ref, dnext_ref, _, dp_ref, dw_ref):
        i = pl.program_id(0)

        @pl.when(i == 0)
        def _():
            dw_ref[...] = jnp.zeros_like(dw_ref)

        w = w_ref[...]
        prev = jnp.where(i > 0, prev_ref[...], 0.0)
        more = i < nt - 1
        xp = jnp.concatenate([prev, x_ref[...], jnp.where(more, next_ref[...], 0.0)], axis=0)
        ext = ts + CONV_PAD
        pre = _conv_pre(xp, w, ext)
        sig = _sigmoid(pre)
        dxc_ext = jnp.concatenate([d_ref[...], jnp.where(more, dnext_ref[...], 0.0)], axis=0)
        dpre = dxc_ext * sig * (1.0 + pre * (1.0 - sig))
        off = CONV_PAD - (DN_CONV - 1)
        d_raw = jnp.zeros((ts, wq), f32)
        dws = []
        for k in range(DN_CONV):
            shift = DN_CONV - 1 - k
            d_raw = d_raw + dpre[shift:shift + ts] * w[k:k + 1]
            dws.append(jnp.sum(dpre[:ts] * xp[off + k:off + k + ts], axis=0, keepdims=True))
        dp_ref[...] = d_raw.astype(bf16)
        dw_ref[...] += jnp.concatenate(dws + [jnp.zeros((CONV_PAD - DN_CONV, wq), f32)], axis=0)

    sec = SEC_QKV // wq
    return pl.pallas_call(
        body, name=name, grid=(nt,),
        in_specs=[pl.BlockSpec((ts, wq), lambda i: (i, sec)),
                  pl.BlockSpec((CONV_PAD, wq), lambda i: (jnp.maximum(i * nb - 1, 0), sec)),
                  pl.BlockSpec((CONV_PAD, wq), lambda i: (jnp.minimum((i + 1) * nb, last_blk), sec)),
                  pl.BlockSpec((CONV_PAD, wq), lambda i: (0, 0)),
                  pl.BlockSpec((ts, wq), lambda i: (i, 0)),
                  pl.BlockSpec((CONV_PAD, wq), lambda i: (jnp.minimum((i + 1) * nb, last_blk), 0)),
                  pl.BlockSpec(memory_space=pl.ANY)],
        out_specs=[pl.BlockSpec((ts, wq), lambda i: (i, sec)), pl.BlockSpec((CONV_PAD, wq), lambda i: (0, 0))],
        out_shape=[jax.ShapeDtypeStruct(dproj.shape, bf16), jax.ShapeDtypeStruct((CONV_PAD, wq), f32)],
        input_output_aliases={6: 0}, compiler_params=_params(1),
    )(proj, proj, proj, conv_w8, dxc, dxc, dproj)


def _dn_chunk(state, xc, z, ba, alog_row, dtb_row, norm_row, t_known):
    c, nh = DN_CHUNK, DN_HEADS
    n = c * nh
    row = lax.broadcasted_iota(jnp.int32, (n, n), 0)
    col = lax.broadcasted_iota(jnp.int32, (n, n), 1)
    same_head = (row // c) == (col // c)
    tril, strict = same_head & (row >= col), same_head & (row > col)
    tril_c = lax.broadcasted_iota(jnp.int32, (c, c), 0) >= lax.broadcasted_iota(jnp.int32, (c, c), 1)
    beta_all = _sigmoid(ba)
    g_all = -jnp.exp(alog_row) * jax.nn.softplus(ba + dtb_row)
    gc_all = hdot(tril_c.astype(f32), g_all)
    gc_t = gc_all.T

    def stack(piece):
        return jnp.concatenate([piece(h) for h in range(nh)], axis=0)

    q = stack(lambda h: xc[:, h * DN_HD:(h + 1) * DN_HD])
    k = stack(lambda h: xc[:, MIX + h * DN_HD:MIX + (h + 1) * DN_HD])
    v = stack(lambda h: xc[:, 2 * MIX + h * DN_HD:2 * MIX + (h + 1) * DN_HD])
    zs = stack(lambda h: z[:, h * DN_HD:(h + 1) * DN_HD])
    q = q * lax.rsqrt(jnp.sum(q * q, axis=-1, keepdims=True) + NORM_EPS) * (DN_HD ** -0.5)
    k = k * lax.rsqrt(jnp.sum(k * k, axis=-1, keepdims=True) + NORM_EPS)
    beta = stack(lambda h: _lane_col(beta_all, h))
    g_cols = [_lane_col(gc_all, nh + h) for h in range(nh)]
    g_col = jnp.concatenate(g_cols, axis=0)
    g_row = jnp.concatenate([gc_t[nh + h:nh + h + 1, :] for h in range(nh)], axis=1)
    g_last = stack(lambda h: jnp.broadcast_to(g_cols[h][c - 1:c, :], (c, 1)))
    decay = jnp.where(tril, jnp.exp(jnp.where(tril, g_col - g_row, 0.0)), 0.0)
    kb = k * beta
    low = jnp.where(strict, bdot(kb, k, 1, 1) * decay, 0.0)
    t_inv = tri_inv(low, c) if t_known is None else tri_inv_known(low, t_known)
    e_gc = jnp.exp(g_col)
    uw = hdot(t_inv, jnp.concatenate([v * beta, kb * e_gc], axis=1))
    u, w = uw[:, :DN_HD], uw[:, DN_HD:]
    attn = bdot(q, k, 1, 1) * decay
    own = (lax.broadcasted_iota(jnp.int32, (n, nh * DN_HD), 1) // DN_HD
           == lax.broadcasted_iota(jnp.int32, (n, nh * DN_HD), 0) // c)

    def spread(a):
        return jnp.where(own, jnp.concatenate([a] * nh, axis=1), 0.0)

    v_new = u - bdot(spread(w), state, 1, 0)
    o = bdot(spread(q * e_gc), state, 1, 0) + bdot(attn, v_new, 1, 0)
    keep = stack(lambda h: jnp.broadcast_to(jnp.exp(g_cols[h][c - 1:c, :]), (DN_HD, 1)))
    new_state = state * keep + bdot(spread(k * jnp.exp(g_last - g_col)), v_new, 0, 0)
    out = _rms(o, norm_row) * (zs * _sigmoid(zs))
    return new_state, jnp.concatenate([out[h * c:(h + 1) * c] for h in range(nh)], axis=1), t_inv


DN_STEP = 2 * DN_CHUNK


def _dn_step(state, xc, z, ba, alog_row, dtb_row, norm_row, t_known=None):
    outs, t_invs = [], []
    for c in range(DN_STEP // DN_CHUNK):
        rows = slice(c * DN_CHUNK, (c + 1) * DN_CHUNK)
        state, out, t_inv = _dn_chunk(state, xc[rows], z[rows], ba[rows], alog_row, dtb_row, norm_row,
                                      None if t_known is None else t_known[c])
        outs.append(out)
        t_invs.append(t_inv)
    return state, jnp.concatenate(outs, axis=0), jnp.stack(t_invs)


def _dn_specs(ts, order):
    zb = SEC_Z // MIX
    return [pl.BlockSpec((ts, 3 * MIX), lambda i: (order(i), 0)),
            pl.BlockSpec((ts, MIX), lambda i: (order(i), zb)),
            pl.BlockSpec((ts, LANES), lambda i: (order(i), 0)),
            pl.BlockSpec((1, LANES), lambda i: (0, 0)), pl.BlockSpec((1, LANES), lambda i: (0, 0)),
            pl.BlockSpec((1, LANES), lambda i: (0, 0))]


def dn_fwd(xc, proj, p_ba, alog_row, dtb_row, norm_row, *, side=None, name):
    s = xc.shape[0]
    ts = _tile(s, 512)
    n_step = ts // DN_STEP

    per_step = DN_STEP // DN_CHUNK
    n_tri = DN_HEADS * DN_CHUNK

    def body(xc_ref, z_ref, ba_ref, al_ref, dt_ref, nr_ref, o_ref, st_ref, tri_ref, state_ref):
        @pl.when(pl.program_id(0) == 0)
        def _():
            state_ref[...] = jnp.zeros_like(state_ref)

        def step(c, carry):
            rows = pl.ds(pl.multiple_of(c * DN_STEP, DN_STEP), DN_STEP)
            st_ref[c] = state_ref[...]
            new_state, out, t_invs = _dn_step(state_ref[...], xc_ref[rows, :], z_ref[rows, :], ba_ref[rows, :],
                                              al_ref[...], dt_ref[...], nr_ref[...])
            state_ref[...] = new_state
            o_ref[rows, :] = out.astype(bf16)
            tri_ref[pl.ds(c * per_step, per_step)] = t_invs
            return carry
        lax.fori_loop(0, n_step, step, 0)

    return host_call(
        side, body, name=name, grid=(s // ts,), in_specs=_dn_specs(ts, lambda i: i),
        out_specs=[pl.BlockSpec((ts, MIX), lambda i: (i, 0)),
                   pl.BlockSpec((n_step, DN_HEADS * DN_HD, DN_HD), lambda i: (i, 0, 0)),
                   pl.BlockSpec((n_step * per_step, n_tri, n_tri), lambda i: (i, 0, 0))],
        out_shape=[jax.ShapeDtypeStruct((s, MIX), bf16),
                   jax.ShapeDtypeStruct((s // DN_STEP, DN_HEADS * DN_HD, DN_HD), f32),
                   jax.ShapeDtypeStruct((s // DN_CHUNK, n_tri, n_tri), f32)],
        scratch_shapes=[pltpu.VMEM((DN_HEADS * DN_HD, DN_HD), f32)], aliases={},
        args=(xc, proj, p_ba, alog_row, dtb_row, norm_row))


def dn_bwd(xc, proj, p_ba, alog_row, dtb_row, norm_row, saved, tri, d_out, dproj, *, side=None, name):
    s = xc.shape[0]
    ts = _tile(s, 512)
    n_step = ts // DN_STEP
    nt = s // ts

    per_step = DN_STEP // DN_CHUNK
    n_tri = DN_HEADS * DN_CHUNK

    def body(xc_ref, z_ref, ba_ref, al_ref, dt_ref, nr_ref, st_ref, tri_ref, do_ref, _,
             dz_ref, dxc_ref, dba_ref, dal_ref, ddt_ref, dnr_ref, dstate_ref):
        @pl.when(pl.program_id(0) == 0)
        def _():
            dstate_ref[...] = jnp.zeros_like(dstate_ref)
            dal_ref[...] = jnp.zeros_like(dal_ref)
            ddt_ref[...] = jnp.zeros_like(ddt_ref)
            dnr_ref[...] = jnp.zeros_like(dnr_ref)

        def step(it, carry):
            c = n_step - 1 - it
            rows = pl.ds(pl.multiple_of(c * DN_STEP, DN_STEP), DN_STEP)
            t_known = tri_ref[pl.ds(c * per_step, per_step)]
            _, vjp = jax.vjp(lambda *a: _dn_step(*a, t_known=t_known)[:2], st_ref[c], xc_ref[rows, :],
                             z_ref[rows, :], ba_ref[rows, :], al_ref[...], dt_ref[...], nr_ref[...])
            d_in, dxc, dz, dba, dal, ddt, dnr = vjp((dstate_ref[...], do_ref[rows, :]))
            dstate_ref[...] = d_in
            dxc_ref[rows, :] = dxc
            dz_ref[rows, :] = dz.astype(bf16)
            dba_ref[rows, :] = dba.astype(bf16)
            dal_ref[...] += dal
            ddt_ref[...] += ddt
            dnr_ref[...] += dnr
            return carry
        lax.fori_loop(0, n_step, step, 0)

    def rev(i):
        return nt - 1 - i

    specs = _dn_specs(ts, rev)
    vec = pl.BlockSpec((1, LANES), lambda i: (0, 0))
    return host_call(
        side, body, name=name, grid=(nt,),
        in_specs=specs + [pl.BlockSpec((n_step, DN_HEADS * DN_HD, DN_HD), lambda i: (rev(i), 0, 0)),
                          pl.BlockSpec((n_step * per_step, n_tri, n_tri), lambda i: (rev(i), 0, 0)),
                          pl.BlockSpec((ts, MIX), lambda i: (rev(i), 0)), pl.BlockSpec(memory_space=pl.ANY)],
        out_specs=[specs[1], specs[0], specs[2], vec, vec, vec],
        out_shape=[jax.ShapeDtypeStruct(dproj.shape, bf16), jax.ShapeDtypeStruct((s, 3 * MIX), f32),
                   jax.ShapeDtypeStruct((s, LANES), bf16)] + [jax.ShapeDtypeStruct((1, LANES), f32)] * 3,
        scratch_shapes=[pltpu.VMEM((DN_HEADS * DN_HD, DN_HD), f32)], aliases={9: 0},
        args=(xc, proj, p_ba, alog_row, dtb_row, norm_row, saved, tri, d_out, dproj))


def _merge_in_specs(ts, d):
    row = pl.BlockSpec((ts, MIX), lambda i: (i, 0))
    return [row, row, row, pl.BlockSpec((ts, 3 * d), lambda i: (i, SEC_G // (3 * d))),
            pl.BlockSpec((3, MIX, d), lambda i: (0, 0, 0))]


def merge_fwd(out_a, out_b, out_c, proj, w_branch, *, name):
    s, d = out_a.shape[0], w_branch.shape[2]
    ts = _tile(s, 256, 16)

    def body(a_ref, b_ref, c_ref, g_ref, w_ref, o_ref):
        acc = jnp.zeros((ts, d), f32)
        for n, r in enumerate((a_ref, b_ref, c_ref)):
            acc = acc + _sigmoid(g_ref[:, n * d:(n + 1) * d]) * _dg(r[...], w_ref[n], 1, 0)
        o_ref[...] = acc.astype(bf16)

    return pl.pallas_call(
        body, name=name, grid=(s // ts,), in_specs=_merge_in_specs(ts, d),
        out_specs=pl.BlockSpec((ts, d), lambda i: (i, 0)), out_shape=jax.ShapeDtypeStruct((s, d), bf16),
        compiler_params=_params(1),
    )(out_a, out_b, out_c, proj, w_branch)


def merge_bwd(out_a, out_b, out_c, proj, w_branch, d_merged, dproj, *, name):
    s, d = out_a.shape[0], w_branch.shape[2]
    ts = _tile(s, 256, 16)

    def body(a_ref, b_ref, c_ref, g_ref, w_ref, dm_ref, _, dg_ref, da_ref, db_ref, dc_ref, dw_ref):
        @pl.when(pl.program_id(0) == 0)
        def _():
            dw_ref[...] = jnp.zeros_like(dw_ref)

        dm = dm_ref[...]
        for n, (r, dr) in enumerate(((a_ref, da_ref), (b_ref, db_ref), (c_ref, dc_ref))):
            gate = _sigmoid(g_ref[:, n * d:(n + 1) * d])
            branch = _dg(r[...], w_ref[n], 1, 0)
            dg_ref[:, n * d:(n + 1) * d] = (dm * branch * gate * (1.0 - gate)).astype(bf16)
            d_branch = dm * gate
            dr[...] = _dg(d_branch, w_ref[n], 1, 1)
            dw_ref[n] += _dg(r[...], d_branch, 0, 0)

    specs = _merge_in_specs(ts, d)
    row_f = pl.BlockSpec((ts, MIX), lambda i: (i, 0))
    return pl.pallas_call(
        body, name=name, grid=(s // ts,),
        in_specs=specs + [pl.BlockSpec((ts, d), lambda i: (i, 0)), pl.BlockSpec(memory_space=pl.ANY)],
        out_specs=[specs[3], row_f, row_f, row_f, specs[4]],
        out_shape=[jax.ShapeDtypeStruct(dproj.shape, bf16)] + [jax.ShapeDtypeStruct((s, MIX), f32)] * 3
        + [jax.ShapeDtypeStruct(w_branch.shape, f32)],
        input_output_aliases={6: 0}, compiler_params=_params(1),
    )(out_a, out_b, out_c, proj, w_branch, d_merged, dproj)


def swiglu_fwd(gu, *, name):
    g2, s, w = gu.shape
    ng = g2 // 2
    ts = _tile(s, 1024, 16)

    def body(g_ref, u_ref, o_ref):
        g = g_ref[...].astype(f32)
        o_ref[...] = (g * _sigmoid(g) * u_ref[...].astype(f32)).astype(bf16)

    return pl.pallas_call(
        body, name=name, grid=(s // ts, ng),
        in_specs=[pl.BlockSpec((None, ts, w), lambda i, j: (j, i, 0)),
                  pl.BlockSpec((None, ts, w), lambda i, j: (ng + j, i, 0))],
        out_specs=pl.BlockSpec((None, ts, w), lambda i, j: (j, i, 0)), out_shape=jax.ShapeDtypeStruct((ng, s, w), bf16),
        compiler_params=_params(2),
    )(gu, gu)


def swiglu_bwd(gu, d_act, *, name):
    g2, s, w = gu.shape
    ng = g2 // 2
    ts = _tile(s, 1024, 16)

    def body(g_ref, u_ref, d_ref, o_ref):
        g, d = g_ref[...].astype(f32), d_ref[...].astype(f32)
        sig = _sigmoid(g)

        @pl.when(pl.program_id(1) < ng)
        def _():
            o_ref[...] = (d * u_ref[...].astype(f32) * sig * (1.0 + g * (1.0 - sig))).astype(bf16)

        @pl.when(pl.program_id(1) >= ng)
        def _():
            o_ref[...] = (d * g * sig).astype(bf16)

    return pl.pallas_call(
        body, name=name, grid=(s // ts, g2),
        in_specs=[pl.BlockSpec((None, ts, w), lambda i, j: (j % ng, i, 0)),
                  pl.BlockSpec((None, ts, w), lambda i, j: (ng + j % ng, i, 0)),
                  pl.BlockSpec((None, ts, w), lambda i, j: (j % ng, i, 0))],
        out_specs=pl.BlockSpec((None, ts, w), lambda i, j: (j, i, 0)), out_shape=jax.ShapeDtypeStruct((g2, s, w), bf16),
        compiler_params=_params(2),
    )(gu, gu, d_act)


def adamw(w, m, v, g_parts, *, name):
    n_layers = len(g_parts)
    n_parts, r, cols = g_parts[0].shape
    lanes = -(-cols // LANES) * LANES
    tr = _tile(r, max(16, (128 * 1024) // lanes), 16)
    nr = r // tr

    def body(w_ref, m_ref, v_ref, *rest):
        gp_refs, (g_ref, d_ref, nm_ref, nv_ref) = rest[:n_layers], rest[n_layers:]
        layer = pl.program_id(0)
        g = jnp.zeros((tr, cols), f32)
        for l, gp_ref in enumerate(gp_refs):
            g_l = gp_ref[0].astype(f32)
            for k in range(1, n_parts):
                g_l = g_l + gp_ref[k].astype(f32)
            g = jnp.where(layer == l, g_l, g)
        nm = ADAM_B1 * m_ref[...] + (1.0 - ADAM_B1) * g
        nv = ADAM_B2 * v_ref[...] + (1.0 - ADAM_B2) * jnp.square(g)
        m_hat = nm / (1.0 - ADAM_B1 ** ADAM_STEP)
        v_hat = nv / (1.0 - ADAM_B2 ** ADAM_STEP)
        g_ref[...] = g
        d_ref[...] = -ADAM_LR * (m_hat / (jnp.sqrt(v_hat) + ADAM_EPS) + ADAM_WD * w_ref[...])
        nm_ref[...] = nm
        nv_ref[...] = nv

    row = pl.BlockSpec((tr, cols), lambda l, i: (l * nr + i, 0))

    def parts_spec(own):
        return pl.BlockSpec((n_parts, tr, cols),
                            lambda l, i: (0, jnp.where(l == own, i, jnp.where(l < own, 0, nr - 1)), 0))

    return pl.pallas_call(
        body, name=name, grid=(n_layers, nr), in_specs=[row, row, row] + [parts_spec(l) for l in range(n_layers)],
        out_specs=[row] * 4, out_shape=[jax.ShapeDtypeStruct(w.shape, f32)] * 4, compiler_params=_params(2),
    )(w, m, v, *g_parts)


def _mesh_pos():
    return lax.axis_index("x"), lax.axis_index("y"), lax.axis_index("c")


def _dev_index(p):
    return 4 * p[0] + 2 * p[1] + p[2]


class Exchange:
    def __init__(self, kind, arrays):
        self.kind, self.arrays, self.n = kind, list(arrays), len(arrays)
        self.specs = [pl.BlockSpec(memory_space=pl.ANY)] * self.n
        self.out_shapes = [jax.ShapeDtypeStruct(((N_DEV,) if kind == "gather" else ()) + a.shape, a.dtype)
                           for a in self.arrays]
        self.scratch = [pltpu.SemaphoreType.DMA((self.n, N_DEV - 1)), pltpu.SemaphoreType.DMA((self.n, N_DEV - 1)),
                        pltpu.SemaphoreType.DMA((self.n,))]

    def _copies(self, in_refs, out_refs, sems, with_arrivals):
        send_sems, recv_sems, local_sems = sems
        x, y, c = _mesh_pos()
        mine = _dev_index((x, y, c))

        def src(a, slab):
            return in_refs[a] if self.kind == "gather" else in_refs[a].at[slab]

        local = [pltpu.make_async_copy(src(a, mine), out_refs[a].at[mine], local_sems.at[a]) for a in range(self.n)]
        sends, arrivals = [], []
        for k in range(1, N_DEV):
            peer = (1 - x if k & 4 else x, 1 - y if k & 2 else y, 1 - c if k & 1 else c)
            theirs = _dev_index(peer)
            for a in range(self.n):
                to = dict(send_sem=send_sems.at[a, k - 1], recv_sem=recv_sems.at[a, k - 1], device_id=peer,
                          device_id_type=pl.DeviceIdType.MESH)
                sends.append(pltpu.make_async_remote_copy(src_ref=src(a, theirs), dst_ref=out_refs[a].at[mine], **to))
                if with_arrivals:
                    arrivals.append(pltpu.make_async_remote_copy(src_ref=src(a, theirs),
                                                                 dst_ref=out_refs[a].at[theirs], **to))
        return local, sends, arrivals

    def start(self, in_refs, out_refs, sems):
        local, sends, _ = self._copies(in_refs, out_refs, sems, False)
        for cp in local + sends:
            cp.start()

    def wait(self, in_refs, out_refs, sems):
        local, sends, arrivals = self._copies(in_refs, out_refs, sems, True)
        for cp in arrivals:
            cp.wait_recv()
        for cp in sends:
            cp.wait_send()
        for cp in local:
            cp.wait()

    def run_around(self, grid, in_refs, out_refs, sems, *, before):
        at = None
        for axis, size in enumerate(grid):
            hit = pl.program_id(axis) == (0 if before else size - 1)
            at = hit if at is None else at & hit

        @pl.when(at)
        def _():
            (self.start if before else self.wait)(in_refs, out_refs, sems)


def host_call(side, body, *, name, grid, in_specs, out_specs, out_shape, scratch_shapes, args, aliases):
    n_in, n_out = len(in_specs), len(out_specs)
    if side is None:
        kernel_body = body
    else:
        n = side.n
        in_specs, args = in_specs + side.specs, tuple(args) + tuple(side.arrays)
        out_specs, out_shape = out_specs + side.specs, out_shape + side.out_shapes
        scratch_shapes = scratch_shapes + side.scratch

        def kernel_body(*refs):
            ins, side_in = refs[:n_in], refs[n_in:n_in + n]
            outs, side_out = refs[n_in + n:n_in + n + n_out], refs[n_in + n + n_out:n_in + 2 * n + n_out]
            scratch, sems = refs[n_in + 2 * n + n_out:-3], refs[-3:]
            side.run_around(grid, side_in, side_out, sems, before=True)
            body(*ins, *outs, *scratch)
            side.run_around(grid, side_in, side_out, sems, before=False)

    outs = pl.pallas_call(
        kernel_body, name=name, grid=grid, in_specs=in_specs, out_specs=out_specs, out_shape=out_shape,
        scratch_shapes=scratch_shapes, input_output_aliases=aliases, compiler_params=_params(len(grid)),
    )(*args)
    return outs if side is None else (outs[:n_out], outs[n_out:])


def all_gather(blocks, *, name):
    n = len(blocks)
    any_spec = pl.BlockSpec(memory_space=pl.ANY)

    def body(*refs):
        ins, outs = refs[:n], refs[n:2 * n]
        send_sems, recv_sems, local_sems = refs[2 * n:]
        x, y, c = _mesh_pos()
        me, sibling = (x, y, c), (x, y, 1 - c)
        chips = [(1 - x, y), (x, 1 - y), (1 - x, 1 - y)]

        def copy(a, k, block, to, src=None):
            dst = outs[a].at[_dev_index(block)]
            return pltpu.make_async_remote_copy(
                src_ref=dst if src is None else src, dst_ref=dst, send_sem=send_sems.at[a, k],
                recv_sem=recv_sems.at[a, k], device_id=to, device_id_type=pl.DeviceIdType.MESH)

        mine = [pltpu.make_async_copy(ins[a], outs[a].at[_dev_index(me)], local_sems.at[a]) for a in range(n)]
        for cp in mine:
            cp.start()
        first = []
        for a in range(n):
            first.append(copy(a, 0, me, sibling, src=ins[a]))
            first += [copy(a, 1 + j, me, (*chip, c), src=ins[a]) for j, chip in enumerate(chips)]
        for cp in first:
            cp.start()
        passed = []
        for j, chip in enumerate(chips):
            for a in range(n):
                copy(a, 1 + j, (*chip, c), me).wait_recv()
                fwd = copy(a, 4 + j, (*chip, c), sibling)
                fwd.start()
                passed.append(fwd)
        for a in range(n):
            copy(a, 0, sibling, me).wait_recv()
            for j, chip in enumerate(chips):
                copy(a, 4 + j, (*chip, 1 - c), me).wait_recv()
        for cp in first + passed:
            cp.wait_send()
        for cp in mine:
            cp.wait()

    return pl.pallas_call(
        body, name=name, in_specs=[any_spec] * n, out_specs=[any_spec] * n,
        out_shape=[jax.ShapeDtypeStruct((N_DEV,) + b.shape, b.dtype) for b in blocks],
        scratch_shapes=[pltpu.SemaphoreType.DMA((n, 7)), pltpu.SemaphoreType.DMA((n, 7)),
                        pltpu.SemaphoreType.DMA((n,))],
    )(*blocks)


def exchange(jobs, *, name):
    total = sum(j.n for j in jobs)

    def body(*refs):
        ins, outs, sems = refs[:total], refs[total:2 * total], refs[2 * total:]
        pieces, off = [], 0
        for i, j in enumerate(jobs):
            pieces.append((ins[off:off + j.n], outs[off:off + j.n], sems[3 * i:3 * i + 3]))
            off += j.n
        for j, piece in zip(jobs, pieces):
            j.start(*piece)
        for j, piece in zip(jobs, pieces):
            j.wait(*piece)

    outs = pl.pallas_call(
        body, name=name, in_specs=[s for j in jobs for s in j.specs], out_specs=[s for j in jobs for s in j.specs],
        out_shape=[s for j in jobs for s in j.out_shapes], scratch_shapes=[s for j in jobs for s in j.scratch],
    )(*[a for j in jobs for a in j.arrays])
    split, off = [], 0
    for j in jobs:
        split.append(outs[off:off + j.n])
        off += j.n
    return split


def _rows128(arr):
    flat = arr.reshape(-1)
    rows = -(-flat.shape[0] // (8 * LANES)) * 8
    return jnp.pad(flat, (0, rows * LANES - flat.shape[0])).reshape(rows, LANES)


def _pad_lanes(row, width=LANES, at=0):
    return jnp.pad(row, (at, width - at - row.shape[0])).reshape(1, width)


def _w_in_sections(got):
    d = got.shape[1]
    wi = jnp.transpose(got, (1, 0, 2)).reshape(d, -1)
    w_main = jnp.concatenate([wi[:, :C_B], wi[:, C_Z:C_BA], wi[:, C_QKV:C_Z], wi[:, C_G:], wi[:, C_B:C_QKV]], axis=1)
    return w_main, jnp.pad(wi[:, C_BA:C_G], ((0, 0), (0, LANES - (C_G - C_BA))))


def _w_in_parts(gw_main, gw_ba):
    d = gw_main.shape[0]
    full = jnp.concatenate([gw_main[:, SEC_A:SEC_Z], gw_main[:, SEC_B:], gw_main[:, SEC_QKV:SEC_G],
                            gw_main[:, SEC_Z:SEC_QKV], gw_ba[:, :C_G - C_BA], gw_main[:, SEC_G:SEC_B]], axis=1)
    return jnp.transpose(full.reshape(d, N_DEV, -1), (1, 0, 2))


def kernel(x, positions, attn_norm, w_in, sgu_ln_g, sgu_ln_b, sgu_w, sgu_b, attn_sinks, dn_conv_w, dn_a_log, dn_dt_bias, dn_norm, w_branch, w_out, ffn_norm, w_gate_up, w_down, final_norm, loss_target, m_attn_norm, m_w_in, m_sgu_ln_g, m_sgu_ln_b, m_sgu_w, m_sgu_b, m_attn_sinks, m_dn_conv_w, m_dn_a_log, m_dn_dt_bias, m_dn_norm, m_w_branch, m_w_out, m_ffn_norm, m_w_gate_up, m_w_down, m_final_norm, v_attn_norm, v_w_in, v_sgu_ln_g, v_sgu_ln_b, v_sgu_w, v_sgu_b, v_attn_sinks, v_dn_conv_w, v_dn_a_log, v_dn_dt_bias, v_dn_norm, v_w_branch, v_w_out, v_ffn_norm, v_w_gate_up, v_w_down, v_final_norm):
    given = dict(locals())
    depth, d_model = attn_norm.shape
    s = x.shape[1]
    x2 = x.reshape(s, d_model)
    target = loss_target.reshape(s, d_model)
    posf = positions.reshape(s, 1).astype(f32)
    inv_freq = ROPE_THETA ** (-jnp.arange(0, ROPE_DIM, 2, dtype=f32) / ROPE_DIM)
    inv_head = jnp.concatenate([inv_freq, inv_freq, jnp.zeros((SWA_HD - ROPE_DIM,), f32)])
    tables = rope_tables(posf, jnp.tile(inv_head, LANES // SWA_HD).reshape(1, LANES), name="rope_tables")

    assert depth == 2, depth
    gathered = dict(zip([("w_in", 0), ("dn_conv_w", 0), ("dn_conv_w", 1)], all_gather(
        [w_in[0].astype(bf16), dn_conv_w[0], dn_conv_w[1]], name="gather_first")))
    riders = {"l0_in_proj": [("w_branch", 0), ("w_out", 0)],
              "l0_deltanet": [("w_gate_up", 0), ("w_down", 0), ("w_in", 1)],
              "l1_deltanet": [("w_branch", 1), ("w_out", 1), ("w_gate_up", 1), ("w_down", 1)]}

    def gathering(host, call, *args, **kw):
        if host not in riders:
            return call(*args, name=host, **kw)
        side = Exchange("gather", [given[n][l].astype(bf16) for n, l in riders[host]])
        out, got = call(*args, side=side, name=host, **kw)
        gathered.update(zip(riders[host], got))
        return out

    layers, saved = [], []
    h_in = x2
    for l in range(depth):
        t = f"l{l}_"
        w_main, w_ba = _w_in_sections(gathered["w_in", l])
        conv_full = jnp.transpose(gathered["dn_conv_w", l], (1, 0, 2)).reshape(DN_CONV, -1)
        p = dict(
            w_main=w_main, w_ba=w_ba, conv_w8=jnp.pad(conv_full, ((0, CONV_PAD - DN_CONV), (0, 0))),
            attn_norm=attn_norm[l].reshape(1, -1), ffn_norm=ffn_norm[l].reshape(1, -1),
            ln_g=sgu_ln_g[l].reshape(1, -1), ln_b=sgu_ln_b[l].reshape(1, -1), sgu_w=sgu_w[l], sgu_bt=sgu_b[l].T,
            sink_row=_pad_lanes(attn_sinks[l]), alog_row=_pad_lanes(dn_a_log[l], at=DN_HEADS),
            dtb_row=_pad_lanes(dn_dt_bias[l], at=DN_HEADS), norm_row=dn_norm[l].reshape(1, -1))
        layers.append(p)
        h = rmsnorm_fwd(h_in, p["attn_norm"], name=t + "attn_norm")
        proj = gathering(t + "in_proj", matmul, h, p["w_main"], "nn", f32)
        p_ba = matmul(h, p["w_ba"], "nn", f32, name=t + "in_proj_ba")
        out_a = sgu_fwd(proj, p["ln_g"], p["ln_b"], p["sgu_w"], p["sgu_bt"], name=t + "sgu")
        out_b = swa_fwd(proj, p["sink_row"], tables, name=t + "swa")
        xc = conv_fwd(proj, p["conv_w8"], name=t + "dn_conv")
        out_c, states, tri = gathering(t + "deltanet", dn_fwd, xc, proj, p_ba, p["alog_row"], p["dtb_row"], p["norm_row"])
        p.update(w_branch=jnp.transpose(gathered["w_branch", l], (1, 2, 0, 3)).reshape(3, MIX, d_model),
                 w_out=gathered["w_out", l].reshape(d_model, d_model),
                 w_gu=gathered["w_gate_up", l],
                 w_down=gathered["w_down", l].reshape(N_DEV // 2, -1, d_model))
        merged = merge_fwd(out_a, out_b, out_c, proj, p["w_branch"], name=t + "merge")
        x_mid = matmul(merged, p["w_out"], "nn", f32, residual=h_in, tn=1024, name=t + "out_proj")
        h2 = rmsnorm_fwd(x_mid, p["ffn_norm"], name=t + "ffn_norm")
        gu = matmul(h2, p["w_gu"], "nn", bf16, group="n", name=t + "gate_up")
        act = swiglu_fwd(gu, name=t + "swiglu")
        x_out = matmul(act, p["w_down"], "nn", f32, residual=x_mid, group="k", tn=1024, name=t + "down")
        saved.append(dict(x_in=h_in, h=h, proj=proj, p_ba=p_ba, out_a=out_a, out_b=out_b, out_c=out_c, xc=xc,
                          states=states, tri=tri, merged=merged, x_mid=x_mid, h2=h2, gu=gu, act=act))
        h_in = x_out

    dx, d_final_norm, loss_row = loss_head(h_in, final_norm.reshape(1, -1), target, name="loss_head")
    loss = lax.psum(loss_row[0, 0], MESH_AXES)

    shard_names = ["w_in", "dn_conv_w", "w_branch", "w_out", "w_gate_up", "w_down"]
    rep_names = ["attn_norm", "sgu_ln_g", "sgu_ln_b", "sgu_w", "sgu_b", "attn_sinks", "dn_a_log", "dn_dt_bias",
                 "dn_norm", "ffn_norm"]
    parts, received, per_layer = {}, {}, []
    senders = {"l0_b_swa": [("w_gate_up", 1), ("w_down", 1), ("w_out", 1), ("w_branch", 1)],
               "l0_b_deltanet": [("w_in", 1), ("w_gate_up", 0), ("w_down", 0), ("w_out", 0), ("w_branch", 0)],
               "l0_b_in_proj_dx": [("w_in", 0)]}

    def scattering(host, call, *args, **kw):
        if host not in senders:
            return call(*args, name=host, **kw)
        out, got = call(*args, side=Exchange("scatter", [parts[key] for key in senders[host]]), name=host, **kw)
        received.update(zip(senders[host], got))
        return out

    for l in reversed(range(depth)):
        p, sv, t = layers[l], saved[l], f"l{l}_b_"
        d_act = matmul(dx, p["w_down"], "nt", bf16, group="n", name=t + "down_dx")
        gw_down = matmul(sv["act"], dx, "tn", bf16, group="m", tk=2048, tn=512, name=t + "down_dw")
        d_gu = swiglu_bwd(sv["gu"], d_act, name=t + "swiglu")
        gw_gu = matmul(sv["h2"], d_gu, "tn", bf16, group="n", tk=2048, name=t + "gate_up_dw")
        d_h2 = matmul(d_gu, p["w_gu"], "nt", f32, group="k", tn=1024, name=t + "gate_up_dx")
        dx_mid, g_ffn = rmsnorm_bwd(sv["x_mid"], p["ffn_norm"], d_h2, dx, name=t + "ffn_norm")
        d_merged = matmul(dx_mid, p["w_out"], "nt", f32, tn=1024, name=t + "out_proj_dx")
        gw_out = matmul(sv["merged"], dx_mid, "tn", bf16, tk=2048, tn=1024, name=t + "out_proj_dw")
        dproj = lax.empty((s, W_MAIN), bf16)
        dproj, d_a, d_b, d_c, gw_branch = merge_bwd(sv["out_a"], sv["out_b"], sv["out_c"], sv["proj"], p["w_branch"],
                                                   d_merged, dproj, name=t + "merge")
        parts.update({("w_gate_up", l): gw_gu, ("w_down", l): gw_down.reshape(N_DEV, -1, d_model),
                      ("w_out", l): gw_out.reshape(N_DEV, -1, d_model),
                      ("w_branch", l): jnp.transpose(gw_branch.reshape(3, MIX, N_DEV, -1), (2, 0, 1, 3)).astype(bf16)})
        dproj, g_ln_g, g_ln_b, g_sgu_w, g_sgu_bt = sgu_bwd(sv["proj"], p["ln_g"], p["ln_b"], p["sgu_w"], p["sgu_bt"],
                                                         d_a, dproj, name=t + "sgu")
        dproj, g_sink = scattering(t + "swa", swa_bwd, sv["proj"], p["sink_row"], tables, d_b, dproj)
        dproj, dxc, dba, g_alog, g_dtb, g_dnorm = scattering(
            t + "deltanet", dn_bwd, sv["xc"], sv["proj"], sv["p_ba"], p["alog_row"], p["dtb_row"], p["norm_row"],
            sv["states"], sv["tri"], d_c, dproj)
        dproj, g_conv8 = conv_bwd(sv["proj"], p["conv_w8"], dxc, dproj, name=t + "dn_conv")
        gw_main = matmul(sv["h"], dproj, "tn", bf16, tk=2048, name=t + "in_proj_dw")
        gw_ba = matmul(sv["h"], dba, "tn", bf16, tk=2048, name=t + "in_proj_ba_dw")
        parts["w_in", l] = _w_in_parts(gw_main, gw_ba)
        d_h = scattering(t + "in_proj_dx", matmul, dproj, p["w_main"], "nt", f32, tk=2304, tn=1024)
        d_h = matmul(dba, p["w_ba"], "nt", f32, residual=d_h, name=t + "in_proj_ba_dx")
        dx, g_attn = rmsnorm_bwd(sv["x_in"], p["attn_norm"], d_h, dx_mid, name=t + "attn_norm")
        per_layer.append(dict(
            dn_conv_w=jnp.transpose(g_conv8[:DN_CONV].reshape(DN_CONV, N_DEV, -1), (1, 0, 2)),
            attn_norm=g_attn, sgu_ln_g=g_ln_g, sgu_ln_b=g_ln_b, sgu_w=g_sgu_w, sgu_b=g_sgu_bt.T,
            attn_sinks=g_sink[0, :SWA_HEADS], dn_a_log=g_alog[0, DN_HEADS:2 * DN_HEADS],
            dn_dt_bias=g_dtb[0, DN_HEADS:2 * DN_HEADS], dn_norm=g_dnorm, ffn_norm=g_ffn))
    per_layer.reverse()

    conv_parts = jnp.concatenate([pp["dn_conv_w"] for pp in per_layer], axis=1)
    rep_grads = {n: jnp.stack([pp[n].reshape(given[n].shape[1:]) for pp in per_layer]) for n in rep_names}
    rep_grads["final_norm"] = d_final_norm[0]
    rep_names = rep_names + ["final_norm"]
    rep_rows = [_rows128(given[n]).shape[0] for n in rep_names]
    pad_rows = -sum(rep_rows) % 16

    def small_rows(values):
        pieces = [_rows128(values[n]) for n in rep_names]
        return jnp.concatenate(pieces + [jnp.zeros((pad_rows, LANES), f32)], axis=0)

    (got_conv,), (small_all,) = exchange(
        [Exchange("scatter", [conv_parts]), Exchange("gather", [small_rows(rep_grads)])], name="exchange_last")

    results = [{}, {}, {}, {}]
    for n in shard_names:
        shp = given[n].shape
        two = (-1, shp[-1])
        by_layer = [got_conv] if n == "dn_conv_w" else [received[n, l].reshape(N_DEV, -1, shp[-1]) for l in range(depth)]
        outs = adamw(given[n].reshape(two), given["m_" + n].reshape(two), given["v_" + n].reshape(two), by_layer,
                     name="adamw_" + n)
        for res, val in zip(results, outs):
            res[n] = val.reshape(shp)
    outs = adamw(small_rows(given), small_rows({n: given["m_" + n] for n in rep_names}),
                 small_rows({n: given["v_" + n] for n in rep_names}), [small_all], name="adamw_replicated")
    for res, val in zip(results, outs):
        row = 0
        for n, nr in zip(rep_names, rep_rows):
            res[n] = val[row:row + nr].reshape(-1)[:given[n].size].reshape(given[n].shape)
            row += nr
    order = ["attn_norm", "w_in", "sgu_ln_g", "sgu_ln_b", "sgu_w", "sgu_b", "attn_sinks", "dn_conv_w", "dn_a_log",
             "dn_dt_bias", "dn_norm", "w_branch", "w_out", "ffn_norm", "w_gate_up", "w_down", "final_norm"]
    return (loss, dx.reshape(x.shape), *[res[n] for res in results for n in order])
```

```python
import functools

import jax
import jax.numpy as jnp
from jax import lax
from jax.experimental import pallas as pl
from jax.experimental.pallas import tpu as pltpu

f32 = jnp.float32
bf16 = jnp.bfloat16

N_DEV = 8
MESH_AXES = ("x", "y", "c")
NORM_EPS = 1e-6
MIX = 512
SGU_GROUPS, SGU_CHUNK = 4, 128
SWA_HEADS, SWA_KV, SWA_HD, WINDOW = 8, 2, 64, 128
ROPE_THETA, ROPE_DIM = 500000.0, 16
DN_HEADS, DN_HD, DN_CONV, DN_CHUNK = 4, 128, 4, 64
ADAM_LR, ADAM_B1, ADAM_B2, ADAM_EPS, ADAM_WD, ADAM_STEP = 0.001, 0.9, 0.999, 1e-08, 0.01, 10

LANES = 128
VMEM_LIMIT = 56 * 1024 * 1024

SEC_A, SEC_Z, SEC_QKV, SEC_G, SEC_B = 0, 1024, 1536, 3072, 6144
W_MAIN = 6912
C_B, C_QKV, C_Z, C_BA, C_G = 1024, 1792, 3328, 3840, 3848


def _params(n_axes, **kw):
    return pltpu.CompilerParams(dimension_semantics=("arbitrary",) * n_axes, vmem_limit_bytes=VMEM_LIMIT, **kw)


def _tile(n, target, mult=LANES):
    if n <= target:
        return n
    best = None
    for t in range(mult, target + 1, mult):
        if n % t == 0:
            best = t
    assert best is not None, (n, target, mult)
    return best


def _dg(a, b, ca, cb):
    return lax.dot_general(a.astype(bf16), b.astype(bf16), (((ca,), (cb,)), ((), ())), preferred_element_type=f32)


def _dg3(a, b, ca, cb):
    a_hi, b_hi = a.astype(bf16), b.astype(bf16)
    a_lo, b_lo = (a - a_hi.astype(f32)).astype(bf16), (b - b_hi.astype(f32)).astype(bf16)

    def dot(p, q):
        return lax.dot_general(p, q, (((ca,), (cb,)), ((), ())), preferred_element_type=f32)

    return dot(a_hi, b_hi) + (dot(a_hi, b_lo) + dot(a_lo, b_hi))


def _differentiable_dot(core):
    @functools.partial(jax.custom_vjp, nondiff_argnums=(2, 3))
    def dot(a, b, ca, cb):
        return core(a, b, ca, cb)

    def fwd(a, b, ca, cb):
        return core(a, b, ca, cb), (a, b)

    def bwd(ca, cb, res, ct):
        a, b = res
        da = core(ct, b, 1, 1 - cb) if ca == 1 else core(b, ct, 1 - cb, 1)
        db = core(a, ct, 1 - ca, 0) if cb == 0 else core(ct, a, 0, 1 - ca)
        return da, db

    dot.defvjp(fwd, bwd)
    return dot


bdot = _differentiable_dot(_dg)
_hdot = _differentiable_dot(_dg3)


def hdot(a, b, ca=1, cb=0):
    return _hdot(a, b, ca, cb)


@functools.partial(jax.custom_vjp, nondiff_argnums=(1,))
def lroll(x, shift):
    return pltpu.roll(x, shift, 1)


def _lroll_fwd(x, shift):
    return pltpu.roll(x, shift, 1), None


def _lroll_bwd(shift, _, ct):
    return (pltpu.roll(ct, ct.shape[1] - shift, 1),)


lroll.defvjp(_lroll_fwd, _lroll_bwd)


@functools.partial(jax.custom_vjp, nondiff_argnums=(1,))
def tri_inv(low, nil):
    n = low.shape[0]
    row = lax.broadcasted_iota(jnp.int32, (n, n), 0)
    col = lax.broadcasted_iota(jnp.int32, (n, n), 1)
    eye = (row == col).astype(f32)
    m = -low
    p = eye + m
    span = 2
    while span < nil:
        m = hdot(m, m)
        p = p + hdot(p, m)
        span *= 2
    return p


def _tri_inv_fwd(low, nil):
    t = tri_inv(low, nil)
    return t, t


def _tri_inv_bwd(nil, t, dt):
    return (-hdot(t, hdot(dt, t, 1, 1), 0, 0),)


tri_inv.defvjp(_tri_inv_fwd, _tri_inv_bwd)


@jax.custom_vjp
def tri_inv_known(low, t):
    return t


def _tri_inv_known_fwd(low, t):
    return t, t


def _tri_inv_known_bwd(t, dt):
    return _tri_inv_bwd(None, t, dt) + (jnp.zeros_like(t),)


tri_inv_known.defvjp(_tri_inv_known_fwd, _tri_inv_known_bwd)


def _sigmoid(x):
    return 1.0 / (1.0 + jnp.exp(-x))


def _rms(x, g):
    return x * lax.rsqrt(jnp.mean(x * x, axis=-1, keepdims=True) + NORM_EPS) * g


def _lane_col(x, lane_idx):
    lane = lax.broadcasted_iota(jnp.int32, x.shape, 1)
    return jnp.sum(jnp.where(lane == lane_idx, x, 0.0), axis=1, keepdims=True)


def matmul(a, b, mode, out_dtype, *, residual=None, group=None, side=None, tm=1024, tn=768, tk=1024, name):
    dims = {"a": ("m", "k") if mode != "tn" else ("k", "m"),
            "b": {"nn": ("k", "n"), "nt": ("n", "k"), "tn": ("k", "n")}[mode], "o": ("m", "n")}
    full, groups = {}, 1
    for arr, key in ((a, "a"), (b, "b")):
        grouped = group in dims[key]
        if grouped:
            groups = arr.shape[0]
        full[dims[key][0]], full[dims[key][1]] = arr.shape[1:] if grouped else arr.shape
    want = {"m": tm, "n": tn, "k": tk}
    per_step = min(groups, 4) if group == "k" else 1
    tiles = {d: full[d] if d == group else _tile(full[d], want[d]) for d in "mnk"}
    steps = {d: groups // per_step if d == group else full[d] // tiles[d] for d in "mnk"}

    def spec(key):
        d0, d1 = dims[key]

        def index(i, j, kk):
            at = {"m": i, "n": j, "k": kk}
            if group in (d0, d1):
                return (at[group], 0 if d0 == group else at[d0], 0 if d1 == group else at[d1])
            return (at[d0], at[d1])

        block = (tiles[d0], tiles[d1])
        if group in (d0, d1):
            block = ((per_step if group == "k" else None),) + block
        return pl.BlockSpec(block, index)

    ca, cb = {"nn": (1, 0), "nt": (1, 1), "tn": (0, 0)}[mode]
    nk = steps["k"]
    o_spec = spec("o")
    out_shape = (groups, full["m"], full["n"]) if group in ("m", "n") else (full["m"], full["n"])
    has_res = residual is not None

    def product(a_ref, b_ref):
        if group != "k":
            return _dg(a_ref[...], b_ref[...], ca, cb)
        total = _dg(a_ref[0], b_ref[0], ca, cb)
        for g in range(1, per_step):
            total = total + _dg(a_ref[g], b_ref[g], ca, cb)
        return total

    def body(a_ref, b_ref, *rest):
        r_ref = rest[0] if has_res else None
        o_ref = rest[1 if has_res else 0]

        def emit(acc):
            o_ref[...] = (acc + r_ref[...] if has_res else acc).astype(out_dtype)

        if nk == 1:
            emit(product(a_ref, b_ref))
            return
        acc_ref = rest[-1]
        kk = pl.program_id(2)

        @pl.when(kk == 0)
        def _():
            acc_ref[...] = jnp.zeros_like(acc_ref)

        acc_ref[...] += product(a_ref, b_ref)

        @pl.when(kk == nk - 1)
        def _():
            emit(acc_ref[...])

    res = host_call(
        side, body, name=name, grid=(steps["m"], steps["n"], nk),
        in_specs=[spec("a"), spec("b")] + ([o_spec] if has_res else []), out_specs=[o_spec],
        out_shape=[jax.ShapeDtypeStruct(out_shape, out_dtype)],
        scratch_shapes=[pltpu.VMEM((tiles["m"], tiles["n"]), f32)] if nk > 1 else [], aliases={},
        args=(a, b) + ((residual,) if has_res else ()))
    return res[0] if side is None else (res[0][0], res[1])


def rmsnorm_fwd(x, g_row, *, name):
    s, d = x.shape
    ts = _tile(s, 512, 16)

    def body(x_ref, g_ref, o_ref):
        o_ref[...] = _rms(x_ref[...], g_ref[...]).astype(bf16)

    return pl.pallas_call(
        body, name=name, grid=(s // ts,),
        in_specs=[pl.BlockSpec((ts, d), lambda i: (i, 0)), pl.BlockSpec((1, d), lambda i: (0, 0))],
        out_specs=pl.BlockSpec((ts, d), lambda i: (i, 0)), out_shape=jax.ShapeDtypeStruct((s, d), bf16),
        compiler_params=_params(1),
    )(x, g_row)


def rmsnorm_bwd(x, g_row, dh, dres, *, name):
    s, d = x.shape
    ts = _tile(s, 512, 16)

    def body(x_ref, g_ref, dh_ref, dres_ref, dx_ref, dg_ref):
        @pl.when(pl.program_id(0) == 0)
        def _():
            dg_ref[...] = jnp.zeros_like(dg_ref)

        _, vjp = jax.vjp(_rms, x_ref[...], g_ref[...])
        dx, dg = vjp(dh_ref[...])
        dx_ref[...] = dx + dres_ref[...]
        dg_ref[...] += dg

    row = pl.BlockSpec((ts, d), lambda i: (i, 0))
    vec = pl.BlockSpec((1, d), lambda i: (0, 0))
    return pl.pallas_call(
        body, name=name, grid=(s // ts,), in_specs=[row, vec, row, row], out_specs=[row, vec],
        out_shape=[jax.ShapeDtypeStruct((s, d), f32), jax.ShapeDtypeStruct((1, d), f32)],
        compiler_params=_params(1),
    )(x, g_row, dh, dres)


def loss_head(x, g_row, target, *, name):
    s, d = x.shape
    ts = _tile(s, 512, 16)

    def body(x_ref, g_ref, t_ref, dx_ref, dg_ref, loss_ref):
        @pl.when(pl.program_id(0) == 0)
        def _():
            dg_ref[...] = jnp.zeros_like(dg_ref)
            loss_ref[...] = jnp.zeros_like(loss_ref)

        y, vjp = jax.vjp(_rms, x_ref[...], g_ref[...])
        err = y - t_ref[...]
        dx, dg = vjp(err * (1.0 / d))
        dx_ref[...] = dx
        dg_ref[...] += dg
        loss_ref[...] += 0.5 * jnp.sum(jnp.sum(err * err, axis=1, keepdims=True) * (1.0 / d), axis=0, keepdims=True)

    row = pl.BlockSpec((ts, d), lambda i: (i, 0))
    vec = pl.BlockSpec((1, d), lambda i: (0, 0))
    one = pl.BlockSpec((1, LANES), lambda i: (0, 0))
    return pl.pallas_call(
        body, name=name, grid=(s // ts,), in_specs=[row, vec, row], out_specs=[row, vec, one],
        out_shape=[jax.ShapeDtypeStruct((s, d), f32), jax.ShapeDtypeStruct((1, d), f32),
                   jax.ShapeDtypeStruct((1, LANES), f32)],
        compiler_params=_params(1),
    )(x, g_row, target)


def _sgu_chunk(p_a, ln_g, ln_b, w, b_t):
    t = SGU_CHUNK
    u = jax.nn.gelu(p_a[:, :MIX])
    v = jax.nn.gelu(p_a[:, MIX:])
    vc = v - jnp.mean(v, axis=-1, keepdims=True)
    vn = vc * lax.rsqrt(jnp.mean(vc * vc, axis=-1, keepdims=True) + NORM_EPS) * ln_g + ln_b
    causal = lax.broadcasted_iota(jnp.int32, (t, t), 0) >= lax.broadcasted_iota(jnp.int32, (t, t), 1)
    outs = []
    for g in range(SGU_GROUPS):
        sl = slice(g * LANES, (g + 1) * LANES)
        mixed = bdot(jnp.where(causal, w[g], 0.0), vn[:, sl], 1, 0) + b_t[:, g:g + 1]
        outs.append(u[:, sl] * mixed)
    return jnp.concatenate(outs, axis=1)


def _sgu_specs(s, ts):
    return [pl.BlockSpec((ts, 2 * MIX), lambda i: (i, SEC_A // (2 * MIX))),
            pl.BlockSpec((1, MIX), lambda i: (0, 0)), pl.BlockSpec((1, MIX), lambda i: (0, 0)),
            pl.BlockSpec((SGU_GROUPS, SGU_CHUNK, SGU_CHUNK), lambda i: (0, 0, 0)),
            pl.BlockSpec((SGU_CHUNK, SGU_GROUPS), lambda i: (0, 0))]


def sgu_fwd(proj, ln_g, ln_b, w, b_t, *, name):
    s = proj.shape[0]
    ts = _tile(s, 512)
    n_chunk = ts // SGU_CHUNK

    def body(p_ref, g_ref, b_ref, w_ref, bt_ref, o_ref):
        def step(c, carry):
            rows = pl.ds(pl.multiple_of(c * SGU_CHUNK, SGU_CHUNK), SGU_CHUNK)
            o_ref[rows, :] = _sgu_chunk(p_ref[rows, :].astype(f32), g_ref[...], b_ref[...], w_ref[...], bt_ref[...]).astype(bf16)
            return carry
        lax.fori_loop(0, n_chunk, step, 0)

    return pl.pallas_call(
        body, name=name, grid=(s // ts,), in_specs=_sgu_specs(s, ts),
        out_specs=pl.BlockSpec((ts, MIX), lambda i: (i, 0)), out_shape=jax.ShapeDtypeStruct((s, MIX), bf16),
        compiler_params=_params(1),
    )(proj, ln_g, ln_b, w, b_t)


def sgu_bwd(proj, ln_g, ln_b, w, b_t, d_out, dproj, *, name):
    s = proj.shape[0]
    ts = _tile(s, 512)
    n_chunk = ts // SGU_CHUNK

    def body(p_ref, g_ref, b_ref, w_ref, bt_ref, do_ref, _, dp_ref, dg_ref, db_ref, dw_ref, dbt_ref):
        @pl.when(pl.program_id(0) == 0)
        def _():
            dg_ref[...] = jnp.zeros_like(dg_ref)
            db_ref[...] = jnp.zeros_like(db_ref)
            dw_ref[...] = jnp.zeros_like(dw_ref)
            dbt_ref[...] = jnp.zeros_like(dbt_ref)

        def step(c, carry):
            rows = pl.ds(pl.multiple_of(c * SGU_CHUNK, SGU_CHUNK), SGU_CHUNK)
            _, vjp = jax.vjp(_sgu_chunk, p_ref[rows, :].astype(f32), g_ref[...], b_ref[...], w_ref[...], bt_ref[...])
            dp, dg, db, dw, dbt = vjp(do_ref[rows, :])
            dp_ref[rows, :] = dp.astype(bf16)
            dg_ref[...] += dg
            db_ref[...] += db
            dw_ref[...] += dw
            dbt_ref[...] += dbt
            return carry
        lax.fori_loop(0, n_chunk, step, 0)

    specs = _sgu_specs(s, ts)
    return pl.pallas_call(
        body, name=name, grid=(s // ts,),
        in_specs=specs + [pl.BlockSpec((ts, MIX), lambda i: (i, 0)), pl.BlockSpec(memory_space=pl.ANY)],
        out_specs=[specs[0], specs[1], specs[2], specs[3], specs[4]],
        out_shape=[jax.ShapeDtypeStruct(dproj.shape, bf16), jax.ShapeDtypeStruct((1, MIX), f32),
                   jax.ShapeDtypeStruct((1, MIX), f32), jax.ShapeDtypeStruct(w.shape, f32),
                   jax.ShapeDtypeStruct(b_t.shape, f32)],
        input_output_aliases={6: 0}, compiler_params=_params(1),
    )(proj, ln_g, ln_b, w, b_t, d_out, dproj)


def rope_tables(posf, inv_freq, *, name):
    s = posf.shape[0]
    ts = _tile(s, 1024, 8)
    half = ROPE_DIM // 2

    def body(pos_ref, inv_ref, o_ref):
        d = lax.broadcasted_iota(jnp.int32, (1, LANES), 1) % SWA_HD
        ang = pos_ref[...] * inv_ref[...]
        sin = jnp.sin(ang)
        o_ref[0] = jnp.cos(ang)
        o_ref[1] = jnp.where(d < half, sin, 0.0)
        o_ref[2] = jnp.where((d >= half) & (d < ROPE_DIM), sin, 0.0)

    return pl.pallas_call(
        body, name=name, grid=(s // ts,),
        in_specs=[pl.BlockSpec((ts, 1), lambda i: (i, 0)), pl.BlockSpec((1, LANES), lambda i: (0, 0))],
        out_specs=pl.BlockSpec((3, ts, LANES), lambda i: (0, i, 0)), out_shape=jax.ShapeDtypeStruct((3, s, LANES), f32),
        compiler_params=_params(1),
    )(posf, inv_freq)


def _rope(x, table):
    w = x.shape[1]
    half = ROPE_DIM // 2
    c, lo, hi = (jnp.concatenate([table[i]] * (w // LANES), axis=1) for i in range(3))
    return x * c - lroll(x, w - half) * lo + lroll(x, half) * hi


def _swa_block(q, kp, kc, vp, vc, sink_row, table_q, table_p, prev_ok, wide):
    t = WINDOW
    q = _rope(q, table_q) * (SWA_HD ** -0.5)
    keys = jnp.concatenate([_rope(kp, table_p), _rope(kc, table_q)], axis=0)
    vals = jnp.concatenate([vp, vc], axis=0)
    own = lax.broadcasted_iota(jnp.int32, (t, t), 0) >= lax.broadcasted_iota(jnp.int32, (t, t), 1)
    lane_half = lax.broadcasted_iota(jnp.int32, (t, LANES), 1) // SWA_HD
    group = SWA_HEADS // SWA_KV
    slabs = []
    for pair in range(SWA_HEADS // 2):
        q_pair = q[:, pair * LANES:(pair + 1) * LANES]
        acc = jnp.zeros((t, LANES), f32)
        for half in range(2):
            h = 2 * pair + half
            kv = h // group
            qm = jnp.where(lane_half == half, q_pair, 0.0)
            if half != kv:
                qm = lroll(qm, SWA_HD)
            if wide:
                both = bdot(qm, keys, 1, 1)
                s_prev, s_own = both[:, :t], both[:, t:]
            else:
                s_prev, s_own = bdot(qm, keys[:t], 1, 1), bdot(qm, keys[t:], 1, 1)
            logits = jnp.where(own, s_own, jnp.where(prev_ok, s_prev, -1e30))
            sink = _lane_col(sink_row, h)
            m = jnp.maximum(jnp.max(logits, axis=1, keepdims=True), sink)
            p = jnp.exp(logits - m)
            probs = p * (1.0 / (jnp.sum(p, axis=1, keepdims=True) + jnp.exp(sink - m)))
            p_prev, p_own = jnp.where(own, 0.0, probs), jnp.where(own, probs, 0.0)
            if wide:
                o = bdot(jnp.concatenate([p_prev, p_own], axis=1), vals, 1, 0)
            else:
                o = bdot(p_prev, vals[:t], 1, 0) + bdot(p_own, vals[t:], 1, 0)
            o = jnp.where(lane_half == kv, o, 0.0)
            if half != kv:
                o = lroll(o, SWA_HD)
            acc = acc + o
        slabs.append(acc)
    return jnp.concatenate(slabs, axis=1)


def _swa_in_specs(nc, clamp):
    t = WINDOW
    qb, kb, vb = SEC_B // MIX, (SEC_B + MIX) // LANES, (SEC_B + MIX + LANES) // LANES

    def cur(i):
        return jnp.minimum(i, nc - 1) if clamp else i

    def prev(i):
        return jnp.maximum(cur(i) - 1, 0)

    return [pl.BlockSpec((t, MIX), lambda i: (cur(i), qb)),
            pl.BlockSpec((t, LANES), lambda i: (prev(i), kb)), pl.BlockSpec((t, LANES), lambda i: (cur(i), kb)),
            pl.BlockSpec((t, LANES), lambda i: (prev(i), vb)), pl.BlockSpec((t, LANES), lambda i: (cur(i), vb)),
            pl.BlockSpec((1, LANES), lambda i: (0, 0)),
            pl.BlockSpec((3, t, LANES), lambda i: (0, cur(i), 0)), pl.BlockSpec((3, t, LANES), lambda i: (0, prev(i), 0))]


def swa_fwd(proj, sink_row, tables, *, name):
    s = proj.shape[0]
    nc = s // WINDOW

    def body(q_ref, kp_ref, kc_ref, vp_ref, vc_ref, sink_ref, tq_ref, tp_ref, o_ref):
        prev_ok = pl.program_id(0) > 0
        blocks = [r[...].astype(f32) for r in (q_ref, kp_ref, kc_ref, vp_ref, vc_ref)]
        o_ref[...] = _swa_block(*blocks, sink_ref[...],
                                tq_ref[...], tp_ref[...], prev_ok, True).astype(bf16)

    return pl.pallas_call(
        body, name=name, grid=(nc,), in_specs=_swa_in_specs(nc, False),
        out_specs=pl.BlockSpec((WINDOW, MIX), lambda i: (i, 0)), out_shape=jax.ShapeDtypeStruct((s, MIX), bf16),
        compiler_params=_params(1),
    )(proj, proj, proj, proj, proj, sink_row, tables, tables)


def swa_bwd(proj, sink_row, tables, d_out, dproj, *, side=None, name):
    s = proj.shape[0]
    nc = s // WINDOW
    t = WINDOW

    def body(q_ref, kp_ref, kc_ref, vp_ref, vc_ref, sink_ref, tq_ref, tp_ref, do_ref, _,
             dp_ref, dsink_ref, cq_ref, ck_ref, cv_ref):
        i = pl.program_id(0)

        @pl.when(i == 0)
        def _():
            dsink_ref[...] = jnp.zeros_like(dsink_ref)

        def write(dk_prev, dv_prev):
            dp_ref[:, :MIX] = cq_ref[...].astype(bf16)
            dp_ref[:, MIX:MIX + LANES] = (ck_ref[...] + dk_prev).astype(bf16)
            dp_ref[:, MIX + LANES:] = (cv_ref[...] + dv_prev).astype(bf16)

        @pl.when(i < nc)
        def _():
            fn = functools.partial(_swa_block, table_q=tq_ref[...], table_p=tp_ref[...], prev_ok=i > 0, wide=False)
            blocks = [r[...].astype(f32) for r in (q_ref, kp_ref, kc_ref, vp_ref, vc_ref)]
            _, vjp = jax.vjp(fn, *blocks, sink_ref[...])
            dq, dkp, dkc, dvp, dvc, dsink = vjp(do_ref[...])
            dsink_ref[...] += dsink

            @pl.when(i > 0)
            def _():
                write(dkp, dvp)

            cq_ref[...] = dq
            ck_ref[...] = dkc
            cv_ref[...] = dvc

        @pl.when(i == nc)
        def _():
            write(0.0, 0.0)

    return host_call(
        side, body, name=name, grid=(nc + 1,),
        in_specs=_swa_in_specs(nc, True) + [pl.BlockSpec((t, MIX), lambda i: (jnp.minimum(i, nc - 1), 0)),
                                            pl.BlockSpec(memory_space=pl.ANY)],
        out_specs=[pl.BlockSpec((t, MIX + 2 * LANES), lambda i: (jnp.maximum(i - 1, 0), SEC_B // (MIX + 2 * LANES))),
                   pl.BlockSpec((1, LANES), lambda i: (0, 0))],
        out_shape=[jax.ShapeDtypeStruct(dproj.shape, bf16), jax.ShapeDtypeStruct((1, LANES), f32)],
        scratch_shapes=[pltpu.VMEM((t, MIX), f32), pltpu.VMEM((t, LANES), f32), pltpu.VMEM((t, LANES), f32)],
        aliases={9: 0}, args=(proj, proj, proj, proj, proj, sink_row, tables, tables, d_out, dproj))


CONV_PAD = 16


def _conv_taps(xp, rows):
    off = CONV_PAD - (DN_CONV - 1)
    return [xp[off + i:off + i + rows] for i in range(DN_CONV)]


def _conv_pre(taps, w):
    pre = taps[0] * w[0:1]
    for i in range(1, DN_CONV):
        pre = pre + taps[i] * w[i:i + 1]
    return pre


def conv_fwd(proj, conv_w8, *, name):
    s = proj.shape[0]
    wq = 3 * MIX
    ts = _tile(s, 512)
    nb = ts // CONV_PAD

    def body(x_ref, prev_ref, w_ref, o_ref):
        prev = jnp.where(pl.program_id(0) > 0, prev_ref[...].astype(f32), 0.0)
        pre = _conv_pre(_conv_taps(jnp.concatenate([prev, x_ref[...].astype(f32)], axis=0), ts), w_ref[...])
        o_ref[...] = pre * _sigmoid(pre)

    return pl.pallas_call(
        body, name=name, grid=(s // ts,),
        in_specs=[pl.BlockSpec((ts, wq), lambda i: (i, SEC_QKV // wq)),
                  pl.BlockSpec((CONV_PAD, wq), lambda i: (jnp.maximum(i * nb - 1, 0), SEC_QKV // wq)),
                  pl.BlockSpec((CONV_PAD, wq), lambda i: (0, 0))],
        out_specs=pl.BlockSpec((ts, wq), lambda i: (i, 0)), out_shape=jax.ShapeDtypeStruct((s, wq), f32),
        compiler_params=_params(1),
    )(proj, proj, conv_w8)


def conv_bwd(proj, conv_w8, dxc, dproj, *, name):
    s = proj.shape[0]
    wq = 3 * MIX
    ts = _tile(s, 512)
    nb = ts // CONV_PAD
    nt = s // ts
    last_blk = s // CONV_PAD - 1

    def body(x_ref, prev_ref, next_ref, w_ref, d_ref, dnext_ref, _, dp_ref, dw_ref):
        i = pl.program_id(0)

        @pl.when(i == 0)
        def _():
            dw_ref[...] = jnp.zeros_like(dw_ref)

        w = w_ref[...]
        prev = jnp.where(i > 0, prev_ref[...].astype(f32), 0.0)
        more = i < nt - 1
        xp = jnp.concatenate([prev, x_ref[...].astype(f32), jnp.where(more, next_ref[...].astype(f32), 0.0)], axis=0)
        taps = _conv_taps(xp, ts + CONV_PAD)
        pre = _conv_pre(taps, w)
        sig = _sigmoid(pre)
        dxc_ext = jnp.concatenate([d_ref[...], jnp.where(more, dnext_ref[...], 0.0)], axis=0)
        dpre = dxc_ext * sig * (1.0 + pre * (1.0 - sig))
        d_raw = jnp.zeros((ts, wq), f32)
        dws = []
        for k in range(DN_CONV):
            shift = DN_CONV - 1 - k
            d_raw = d_raw + dpre[shift:shift + ts] * w[k:k + 1]
            dws.append(jnp.sum(dpre[:ts] * taps[k][:ts], axis=0, keepdims=True))
        dp_ref[...] = d_raw.astype(bf16)
        dw_ref[...] += jnp.concatenate(dws + [jnp.zeros((CONV_PAD - DN_CONV, wq), f32)], axis=0)

    sec = SEC_QKV // wq
    return pl.pallas_call(
        body, name=name, grid=(nt,),
        in_specs=[pl.BlockSpec((ts, wq), lambda i: (i, sec)),
                  pl.BlockSpec((CONV_PAD, wq), lambda i: (jnp.maximum(i * nb - 1, 0), sec)),
                  pl.BlockSpec((CONV_PAD, wq), lambda i: (jnp.minimum((i + 1) * nb, last_blk), sec)),
                  pl.BlockSpec((CONV_PAD, wq), lambda i: (0, 0)),
                  pl.BlockSpec((ts, wq), lambda i: (i, 0)),
                  pl.BlockSpec((CONV_PAD, wq), lambda i: (jnp.minimum((i + 1) * nb, last_blk), 0)),
                  pl.BlockSpec(memory_space=pl.ANY)],
        out_specs=[pl.BlockSpec((ts, wq), lambda i: (i, sec)), pl.BlockSpec((CONV_PAD, wq), lambda i: (0, 0))],
        out_shape=[jax.ShapeDtypeStruct(dproj.shape, bf16), jax.ShapeDtypeStruct((CONV_PAD, wq), f32)],
        input_output_aliases={6: 0}, compiler_params=_params(1),
    )(proj, proj, proj, conv_w8, dxc, dxc, dproj)


def _dn_chunk(state, xc, z, ba, alog_row, dtb_row, norm_row, t_known):
    c, nh = DN_CHUNK, DN_HEADS
    n = c * nh
    row = lax.broadcasted_iota(jnp.int32, (n, n), 0)
    col = lax.broadcasted_iota(jnp.int32, (n, n), 1)
    same_head = (row // c) == (col // c)
    tril, strict = same_head & (row >= col), same_head & (row > col)
    tril_c = lax.broadcasted_iota(jnp.int32, (c, c), 0) >= lax.broadcasted_iota(jnp.int32, (c, c), 1)
    beta_all = _sigmoid(ba)
    g_all = -jnp.exp(alog_row) * jax.nn.softplus(ba + dtb_row)
    gc_all = hdot(tril_c.astype(f32), g_all)
    gc_t = gc_all.T

    def stack(piece):
        return jnp.concatenate([piece(h) for h in range(nh)], axis=0)

    q = stack(lambda h: xc[:, h * DN_HD:(h + 1) * DN_HD])
    k = stack(lambda h: xc[:, MIX + h * DN_HD:MIX + (h + 1) * DN_HD])
    v = stack(lambda h: xc[:, 2 * MIX + h * DN_HD:2 * MIX + (h + 1) * DN_HD])
    zs = stack(lambda h: z[:, h * DN_HD:(h + 1) * DN_HD])
    q = q * lax.rsqrt(jnp.sum(q * q, axis=-1, keepdims=True) + NORM_EPS) * (DN_HD ** -0.5)
    k = k * lax.rsqrt(jnp.sum(k * k, axis=-1, keepdims=True) + NORM_EPS)
    beta = stack(lambda h: _lane_col(beta_all, h))
    g_cols = [_lane_col(gc_all, nh + h) for h in range(nh)]
    g_col = jnp.concatenate(g_cols, axis=0)
    g_row = jnp.concatenate([gc_t[nh + h:nh + h + 1, :] for h in range(nh)], axis=1)
    g_last = stack(lambda h: jnp.broadcast_to(g_cols[h][c - 1:c, :], (c, 1)))
    decay = jnp.where(tril, jnp.exp(jnp.where(tril, g_col - g_row, 0.0)), 0.0)
    kb = k * beta
    low = jnp.where(strict, bdot(kb, k, 1, 1) * decay, 0.0)
    t_inv = tri_inv(low, c) if t_known is None else tri_inv_known(low, t_known)
    e_gc = jnp.exp(g_col)
    uw = hdot(t_inv, jnp.concatenate([v * beta, kb * e_gc], axis=1))
    u, w = uw[:, :DN_HD], uw[:, DN_HD:]
    attn = bdot(q, k, 1, 1) * decay
    own = (lax.broadcasted_iota(jnp.int32, (n, nh * DN_HD), 1) // DN_HD
           == lax.broadcasted_iota(jnp.int32, (n, nh * DN_HD), 0) // c)

    def spread(a):
        return jnp.where(own, jnp.concatenate([a] * nh, axis=1), 0.0)

    v_new = u - bdot(spread(w), state, 1, 0)
    o = bdot(spread(q * e_gc), state, 1, 0) + bdot(attn, v_new, 1, 0)
    keep = stack(lambda h: jnp.broadcast_to(jnp.exp(g_cols[h][c - 1:c, :]), (DN_HD, 1)))
    new_state = state * keep + bdot(spread(k * jnp.exp(g_last - g_col)), v_new, 0, 0)
    out = _rms(o, norm_row) * (zs * _sigmoid(zs))
    return new_state, jnp.concatenate([out[h * c:(h + 1) * c] for h in range(nh)], axis=1), t_inv


DN_STEP = 2 * DN_CHUNK


def _dn_step(state, xc, z, ba, alog_row, dtb_row, norm_row, t_known=None):
    outs, t_invs = [], []
    for c in range(DN_STEP // DN_CHUNK):
        rows = slice(c * DN_CHUNK, (c + 1) * DN_CHUNK)
        state, out, t_inv = _dn_chunk(state, xc[rows], z[rows], ba[rows], alog_row, dtb_row, norm_row,
                                      None if t_known is None else t_known[c])
        outs.append(out)
        t_invs.append(t_inv)
    return state, jnp.concatenate(outs, axis=0), jnp.stack(t_invs)


def _dn_specs(ts, order):
    zb = SEC_Z // MIX
    return [pl.BlockSpec((ts, 3 * MIX), lambda i: (order(i), 0)),
            pl.BlockSpec((ts, MIX), lambda i: (order(i), zb)),
            pl.BlockSpec((ts, LANES), lambda i: (order(i), 0)),
            pl.BlockSpec((1, LANES), lambda i: (0, 0)), pl.BlockSpec((1, LANES), lambda i: (0, 0)),
            pl.BlockSpec((1, LANES), lambda i: (0, 0))]


def dn_fwd(xc, proj, p_ba, alog_row, dtb_row, norm_row, *, side=None, name):
    s = xc.shape[0]
    ts = _tile(s, 512)
    n_step = ts // DN_STEP

    per_step = DN_STEP // DN_CHUNK
    n_tri = DN_HEADS * DN_CHUNK

    def body(xc_ref, z_ref, ba_ref, al_ref, dt_ref, nr_ref, o_ref, st_ref, tri_ref, state_ref):
        @pl.when(pl.program_id(0) == 0)
        def _():
            state_ref[...] = jnp.zeros_like(state_ref)

        def step(c, carry):
            rows = pl.ds(pl.multiple_of(c * DN_STEP, DN_STEP), DN_STEP)
            st_ref[c] = state_ref[...]
            new_state, out, t_invs = _dn_step(state_ref[...], xc_ref[rows, :], z_ref[rows, :].astype(f32), ba_ref[rows, :],
                                              al_ref[...], dt_ref[...], nr_ref[...])
            state_ref[...] = new_state
            o_ref[rows, :] = out.astype(bf16)
            tri_ref[pl.ds(c * per_step, per_step)] = t_invs
            return carry
        lax.fori_loop(0, n_step, step, 0)

    return host_call(
        side, body, name=name, grid=(s // ts,), in_specs=_dn_specs(ts, lambda i: i),
        out_specs=[pl.BlockSpec((ts, MIX), lambda i: (i, 0)),
                   pl.BlockSpec((n_step, DN_HEADS * DN_HD, DN_HD), lambda i: (i, 0, 0)),
                   pl.BlockSpec((n_step * per_step, n_tri, n_tri), lambda i: (i, 0, 0))],
        out_shape=[jax.ShapeDtypeStruct((s, MIX), bf16),
                   jax.ShapeDtypeStruct((s // DN_STEP, DN_HEADS * DN_HD, DN_HD), f32),
                   jax.ShapeDtypeStruct((s // DN_CHUNK, n_tri, n_tri), f32)],
        scratch_shapes=[pltpu.VMEM((DN_HEADS * DN_HD, DN_HD), f32)], aliases={},
        args=(xc, proj, p_ba, alog_row, dtb_row, norm_row))


def dn_bwd(xc, proj, p_ba, alog_row, dtb_row, norm_row, saved, tri, d_out, dproj, *, side=None, name):
    s = xc.shape[0]
    ts = _tile(s, 512)
    n_step = ts // DN_STEP
    nt = s // ts

    per_step = DN_STEP // DN_CHUNK
    n_tri = DN_HEADS * DN_CHUNK

    def body(xc_ref, z_ref, ba_ref, al_ref, dt_ref, nr_ref, st_ref, tri_ref, do_ref, _,
             dz_ref, dxc_ref, dba_ref, dal_ref, ddt_ref, dnr_ref, dstate_ref):
        @pl.when(pl.program_id(0) == 0)
        def _():
            dstate_ref[...] = jnp.zeros_like(dstate_ref)
            dal_ref[...] = jnp.zeros_like(dal_ref)
            ddt_ref[...] = jnp.zeros_like(ddt_ref)
            dnr_ref[...] = jnp.zeros_like(dnr_ref)

        def step(it, carry):
            c = n_step - 1 - it
            rows = pl.ds(pl.multiple_of(c * DN_STEP, DN_STEP), DN_STEP)
            t_known = tri_ref[pl.ds(c * per_step, per_step)]
            _, vjp = jax.vjp(lambda *a: _dn_step(*a, t_known=t_known)[:2], st_ref[c], xc_ref[rows, :],
                             z_ref[rows, :].astype(f32), ba_ref[rows, :], al_ref[...], dt_ref[...], nr_ref[...])
            d_in, dxc, dz, dba, dal, ddt, dnr = vjp((dstate_ref[...], do_ref[rows, :]))
            dstate_ref[...] = d_in
            dxc_ref[rows, :] = dxc
            dz_ref[rows, :] = dz.astype(bf16)
            dba_ref[rows, :] = dba.astype(bf16)
            dal_ref[...] += dal
            ddt_ref[...] += ddt
            dnr_ref[...] += dnr
            return carry
        lax.fori_loop(0, n_step, step, 0)

    def rev(i):
        return nt - 1 - i

    specs = _dn_specs(ts, rev)
    vec = pl.BlockSpec((1, LANES), lambda i: (0, 0))
    return host_call(
        side, body, name=name, grid=(nt,),
        in_specs=specs + [pl.BlockSpec((n_step, DN_HEADS * DN_HD, DN_HD), lambda i: (rev(i), 0, 0)),
                          pl.BlockSpec((n_step * per_step, n_tri, n_tri), lambda i: (rev(i), 0, 0)),
                          pl.BlockSpec((ts, MIX), lambda i: (rev(i), 0)), pl.BlockSpec(memory_space=pl.ANY)],
        out_specs=[specs[1], specs[0], specs[2], vec, vec, vec],
        out_shape=[jax.ShapeDtypeStruct(dproj.shape, bf16), jax.ShapeDtypeStruct((s, 3 * MIX), f32),
                   jax.ShapeDtypeStruct((s, LANES), bf16)] + [jax.ShapeDtypeStruct((1, LANES), f32)] * 3,
        scratch_shapes=[pltpu.VMEM((DN_HEADS * DN_HD, DN_HD), f32)], aliases={9: 0},
        args=(xc, proj, p_ba, alog_row, dtb_row, norm_row, saved, tri, d_out, dproj))


def _merge_in_specs(ts, d):
    row = pl.BlockSpec((ts, MIX), lambda i: (i, 0))
    return [row, row, row, pl.BlockSpec((ts, 3 * d), lambda i: (i, SEC_G // (3 * d))),
            pl.BlockSpec((3, MIX, d), lambda i: (0, 0, 0))]


def merge_fwd(out_a, out_b, out_c, proj, w_branch, *, name):
    s, d = out_a.shape[0], w_branch.shape[2]
    ts = _tile(s, 256, 16)

    def body(a_ref, b_ref, c_ref, g_ref, w_ref, o_ref):
        acc = jnp.zeros((ts, d), f32)
        for n, r in enumerate((a_ref, b_ref, c_ref)):
            acc = acc + _sigmoid(g_ref[:, n * d:(n + 1) * d].astype(f32)) * _dg(r[...], w_ref[n], 1, 0)
        o_ref[...] = acc.astype(bf16)

    return pl.pallas_call(
        body, name=name, grid=(s // ts,), in_specs=_merge_in_specs(ts, d),
        out_specs=pl.BlockSpec((ts, d), lambda i: (i, 0)), out_shape=jax.ShapeDtypeStruct((s, d), bf16),
        compiler_params=_params(1),
    )(out_a, out_b, out_c, proj, w_branch)


def merge_bwd(out_a, out_b, out_c, proj, w_branch, d_merged, dproj, *, name):
    s, d = out_a.shape[0], w_branch.shape[2]
    ts = _tile(s, 256, 16)

    def body(a_ref, b_ref, c_ref, g_ref, w_ref, dm_ref, _, dg_ref, da_ref, db_ref, dc_ref, dw_ref):
        @pl.when(pl.program_id(0) == 0)
        def _():
            dw_ref[...] = jnp.zeros_like(dw_ref)

        dm = dm_ref[...]
        for n, (r, dr) in enumerate(((a_ref, da_ref), (b_ref, db_ref), (c_ref, dc_ref))):
            gate = _sigmoid(g_ref[:, n * d:(n + 1) * d].astype(f32))
            branch = _dg(r[...], w_ref[n], 1, 0)
            dg_ref[:, n * d:(n + 1) * d] = (dm * branch * gate * (1.0 - gate)).astype(bf16)
            d_branch = dm * gate
            dr[...] = _dg(d_branch, w_ref[n], 1, 1)
            dw_ref[n] += _dg(r[...], d_branch, 0, 0)

    specs = _merge_in_specs(ts, d)
    row_f = pl.BlockSpec((ts, MIX), lambda i: (i, 0))
    return pl.pallas_call(
        body, name=name, grid=(s // ts,),
        in_specs=specs + [pl.BlockSpec((ts, d), lambda i: (i, 0)), pl.BlockSpec(memory_space=pl.ANY)],
        out_specs=[specs[3], row_f, row_f, row_f, specs[4]],
        out_shape=[jax.ShapeDtypeStruct(dproj.shape, bf16)] + [jax.ShapeDtypeStruct((s, MIX), f32)] * 3
        + [jax.ShapeDtypeStruct(w_branch.shape, f32)],
        input_output_aliases={6: 0}, compiler_params=_params(1),
    )(out_a, out_b, out_c, proj, w_branch, d_merged, dproj)


def swiglu_fwd(gu, *, name):
    g2, s, w = gu.shape
    ng = g2 // 2
    ts = _tile(s, 1024, 16)

    def body(g_ref, u_ref, o_ref):
        g = g_ref[...].astype(f32)
        o_ref[...] = (g * _sigmoid(g) * u_ref[...].astype(f32)).astype(bf16)

    return pl.pallas_call(
        body, name=name, grid=(s // ts, ng),
        in_specs=[pl.BlockSpec((None, ts, w), lambda i, j: (j, i, 0)),
                  pl.BlockSpec((None, ts, w), lambda i, j: (ng + j, i, 0))],
        out_specs=pl.BlockSpec((None, ts, w), lambda i, j: (j, i, 0)), out_shape=jax.ShapeDtypeStruct((ng, s, w), bf16),
        compiler_params=_params(2),
    )(gu, gu)


def swiglu_bwd(gu, d_act, *, name):
    g2, s, w = gu.shape
    ng = g2 // 2
    ts = _tile(s, 1024, 16)

    def body(g_ref, u_ref, d_ref, o_ref):
        g, d = g_ref[...].astype(f32), d_ref[...].astype(f32)
        sig = _sigmoid(g)

        @pl.when(pl.program_id(1) < ng)
        def _():
            o_ref[...] = (d * u_ref[...].astype(f32) * sig * (1.0 + g * (1.0 - sig))).astype(bf16)

        @pl.when(pl.program_id(1) >= ng)
        def _():
            o_ref[...] = (d * g * sig).astype(bf16)

    return pl.pallas_call(
        body, name=name, grid=(s // ts, g2),
        in_specs=[pl.BlockSpec((None, ts, w), lambda i, j: (j % ng, i, 0)),
                  pl.BlockSpec((None, ts, w), lambda i, j: (ng + j % ng, i, 0)),
                  pl.BlockSpec((None, ts, w), lambda i, j: (j % ng, i, 0))],
        out_specs=pl.BlockSpec((None, ts, w), lambda i, j: (j, i, 0)), out_shape=jax.ShapeDtypeStruct((g2, s, w), bf16),
        compiler_params=_params(2),
    )(gu, gu, d_act)


def adamw(w, m, v, g_parts, *, name):
    n_layers = len(g_parts)
    n_parts, r, cols = g_parts[0].shape
    lanes = -(-cols // LANES) * LANES
    tr = _tile(r, max(16, (128 * 1024) // lanes), 16)
    nr = r // tr

    def body(w_ref, m_ref, v_ref, *rest):
        gp_refs, (g_ref, d_ref, nm_ref, nv_ref) = rest[:n_layers], rest[n_layers:]
        layer = pl.program_id(0)
        g = jnp.zeros((tr, cols), f32)
        for l, gp_ref in enumerate(gp_refs):
            g_l = gp_ref[0].astype(f32)
            for k in range(1, n_parts):
                g_l = g_l + gp_ref[k].astype(f32)
            g = jnp.where(layer == l, g_l, g)
        nm = ADAM_B1 * m_ref[...] + (1.0 - ADAM_B1) * g
        nv = ADAM_B2 * v_ref[...] + (1.0 - ADAM_B2) * jnp.square(g)
        m_hat = nm / (1.0 - ADAM_B1 ** ADAM_STEP)
        v_hat = nv / (1.0 - ADAM_B2 ** ADAM_STEP)
        g_ref[...] = g
        d_ref[...] = -ADAM_LR * (m_hat / (jnp.sqrt(v_hat) + ADAM_EPS) + ADAM_WD * w_ref[...])
        nm_ref[...] = nm
        nv_ref[...] = nv

    row = pl.BlockSpec((tr, cols), lambda l, i: (l * nr + i, 0))

    def parts_spec(own):
        return pl.BlockSpec((n_parts, tr, cols),
                            lambda l, i: (0, jnp.where(l == own, i, jnp.where(l < own, 0, nr - 1)), 0))

    return pl.pallas_call(
        body, name=name, grid=(n_layers, nr), in_specs=[row, row, row] + [parts_spec(l) for l in range(n_layers)],
        out_specs=[row] * 4, out_shape=[jax.ShapeDtypeStruct(w.shape, f32)] * 4, compiler_params=_params(2),
    )(w, m, v, *g_parts)


def _mesh_pos():
    return lax.axis_index("x"), lax.axis_index("y"), lax.axis_index("c")


def _dev_index(p):
    return 4 * p[0] + 2 * p[1] + p[2]


class Exchange:
    def __init__(self, kind, arrays):
        self.kind, self.arrays, self.n = kind, list(arrays), len(arrays)
        self.specs = [pl.BlockSpec(memory_space=pl.ANY)] * self.n
        self.out_shapes = [jax.ShapeDtypeStruct(((N_DEV,) if kind == "gather" else ()) + a.shape, a.dtype)
                           for a in self.arrays]
        self.scratch = [pltpu.SemaphoreType.DMA((self.n, N_DEV - 1)), pltpu.SemaphoreType.DMA((self.n, N_DEV - 1)),
                        pltpu.SemaphoreType.DMA((self.n,))]

    def _copies(self, in_refs, out_refs, sems, with_arrivals):
        send_sems, recv_sems, local_sems = sems
        x, y, c = _mesh_pos()
        mine = _dev_index((x, y, c))

        def src(a, slab):
            return in_refs[a] if self.kind == "gather" else in_refs[a].at[slab]

        local = [pltpu.make_async_copy(src(a, mine), out_refs[a].at[mine], local_sems.at[a]) for a in range(self.n)]
        sends, arrivals = [], []
        for k in range(1, N_DEV):
            peer = (1 - x if k & 4 else x, 1 - y if k & 2 else y, 1 - c if k & 1 else c)
            theirs = _dev_index(peer)
            for a in range(self.n):
                to = dict(send_sem=send_sems.at[a, k - 1], recv_sem=recv_sems.at[a, k - 1], device_id=peer,
                          device_id_type=pl.DeviceIdType.MESH)
                sends.append(pltpu.make_async_remote_copy(src_ref=src(a, theirs), dst_ref=out_refs[a].at[mine], **to))
                if with_arrivals:
                    arrivals.append(pltpu.make_async_remote_copy(src_ref=src(a, theirs),
                                                                 dst_ref=out_refs[a].at[theirs], **to))
        return local, sends, arrivals

    def start(self, in_refs, out_refs, sems):
        local, sends, _ = self._copies(in_refs, out_refs, sems, False)
        for cp in local + sends:
            cp.start()

    def wait(self, in_refs, out_refs, sems):
        local, sends, arrivals = self._copies(in_refs, out_refs, sems, True)
        for cp in arrivals:
            cp.wait_recv()
        for cp in sends:
            cp.wait_send()
        for cp in local:
            cp.wait()

    def run_around(self, grid, in_refs, out_refs, sems, *, before):
        at = None
        for axis, size in enumerate(grid):
            hit = pl.program_id(axis) == (0 if before else size - 1)
            at = hit if at is None else at & hit

        @pl.when(at)
        def _():
            (self.start if before else self.wait)(in_refs, out_refs, sems)


def host_call(side, body, *, name, grid, in_specs, out_specs, out_shape, scratch_shapes, args, aliases):
    n_in, n_out = len(in_specs), len(out_specs)
    if side is None:
        kernel_body = body
    else:
        n = side.n
        in_specs, args = in_specs + side.specs, tuple(args) + tuple(side.arrays)
        out_specs, out_shape = out_specs + side.specs, out_shape + side.out_shapes
        scratch_shapes = scratch_shapes + side.scratch

        def kernel_body(*refs):
            ins, side_in = refs[:n_in], refs[n_in:n_in + n]
            outs, side_out = refs[n_in + n:n_in + n + n_out], refs[n_in + n + n_out:n_in + 2 * n + n_out]
            scratch, sems = refs[n_in + 2 * n + n_out:-3], refs[-3:]
            side.run_around(grid, side_in, side_out, sems, before=True)
            body(*ins, *outs, *scratch)
            side.run_around(grid, side_in, side_out, sems, before=False)

    outs = pl.pallas_call(
        kernel_body, name=name, grid=grid, in_specs=in_specs, out_specs=out_specs, out_shape=out_shape,
        scratch_shapes=scratch_shapes, input_output_aliases=aliases, compiler_params=_params(len(grid)),
    )(*args)
    return outs if side is None else (outs[:n_out], outs[n_out:])


def all_gather(blocks, *, name):
    n = len(blocks)
    any_spec = pl.BlockSpec(memory_space=pl.ANY)

    def body(*refs):
        ins, outs = refs[:n], refs[n:2 * n]
        send_sems, recv_sems, local_sems = refs[2 * n:]
        x, y, c = _mesh_pos()
        me, sibling = (x, y, c), (x, y, 1 - c)
        chips = [(1 - x, y), (x, 1 - y), (1 - x, 1 - y)]

        def copy(a, k, block, to, src=None):
            dst = outs[a].at[_dev_index(block)]
            return pltpu.make_async_remote_copy(
                src_ref=dst if src is None else src, dst_ref=dst, send_sem=send_sems.at[a, k],
                recv_sem=recv_sems.at[a, k], device_id=to, device_id_type=pl.DeviceIdType.MESH)

        mine = [pltpu.make_async_copy(ins[a], outs[a].at[_dev_index(me)], local_sems.at[a]) for a in range(n)]
        for cp in mine:
            cp.start()
        first = []
        for a in range(n):
            first.append(copy(a, 0, me, sibling, src=ins[a]))
            first += [copy(a, 1 + j, me, (*chip, c), src=ins[a]) for j, chip in enumerate(chips)]
        for cp in first:
            cp.start()
        passed = []
        for j, chip in enumerate(chips):
            for a in range(n):
                copy(a, 1 + j, (*chip, c), me).wait_recv()
                fwd = copy(a, 4 + j, (*chip, c), sibling)
                fwd.start()
                passed.append(fwd)
        for a in range(n):
            copy(a, 0, sibling, me).wait_recv()
            for j, chip in enumerate(chips):
                copy(a, 4 + j, (*chip, 1 - c), me).wait_recv()
        for cp in first + passed:
            cp.wait_send()
        for cp in mine:
            cp.wait()

    return pl.pallas_call(
        body, name=name, in_specs=[any_spec] * n, out_specs=[any_spec] * n,
        out_shape=[jax.ShapeDtypeStruct((N_DEV,) + b.shape, b.dtype) for b in blocks],
        scratch_shapes=[pltpu.SemaphoreType.DMA((n, 7)), pltpu.SemaphoreType.DMA((n, 7)),
                        pltpu.SemaphoreType.DMA((n,))],
    )(*blocks)


def exchange(jobs, *, name):
    total = sum(j.n for j in jobs)

    def body(*refs):
        ins, outs, sems = refs[:total], refs[total:2 * total], refs[2 * total:]
        pieces, off = [], 0
        for i, j in enumerate(jobs):
            pieces.append((ins[off:off + j.n], outs[off:off + j.n], sems[3 * i:3 * i + 3]))
            off += j.n
        for j, piece in zip(jobs, pieces):
            j.start(*piece)
        for j, piece in zip(jobs, pieces):
            j.wait(*piece)

    outs = pl.pallas_call(
        body, name=name, in_specs=[s for j in jobs for s in j.specs], out_specs=[s for j in jobs for s in j.specs],
        out_shape=[s for j in jobs for s in j.out_shapes], scratch_shapes=[s for j in jobs for s in j.scratch],
    )(*[a for j in jobs for a in j.arrays])
    split, off = [], 0
    for j in jobs:
        split.append(outs[off:off + j.n])
        off += j.n
    return split


def _rows128(arr):
    flat = arr.reshape(-1)
    rows = -(-flat.shape[0] // (8 * LANES)) * 8
    return jnp.pad(flat, (0, rows * LANES - flat.shape[0])).reshape(rows, LANES)


def _pad_lanes(row, width=LANES, at=0):
    return jnp.pad(row, (at, width - at - row.shape[0])).reshape(1, width)


def _w_in_sections(got):
    d = got.shape[1]
    wi = jnp.transpose(got, (1, 0, 2)).reshape(d, -1)
    w_main = jnp.concatenate([wi[:, :C_B], wi[:, C_Z:C_BA], wi[:, C_QKV:C_Z], wi[:, C_G:], wi[:, C_B:C_QKV]], axis=1)
    return w_main, jnp.pad(wi[:, C_BA:C_G], ((0, 0), (0, LANES - (C_G - C_BA))))


def _w_in_parts(gw_main, gw_ba):
    d = gw_main.shape[0]
    full = jnp.concatenate([gw_main[:, SEC_A:SEC_Z], gw_main[:, SEC_B:], gw_main[:, SEC_QKV:SEC_G],
                            gw_main[:, SEC_Z:SEC_QKV], gw_ba[:, :C_G - C_BA], gw_main[:, SEC_G:SEC_B]], axis=1)
    return jnp.transpose(full.reshape(d, N_DEV, -1), (1, 0, 2))


def kernel(x, positions, attn_norm, w_in, sgu_ln_g, sgu_ln_b, sgu_w, sgu_b, attn_sinks, dn_conv_w, dn_a_log, dn_dt_bias, dn_norm, w_branch, w_out, ffn_norm, w_gate_up, w_down, final_norm, loss_target, m_attn_norm, m_w_in, m_sgu_ln_g, m_sgu_ln_b, m_sgu_w, m_sgu_b, m_attn_sinks, m_dn_conv_w, m_dn_a_log, m_dn_dt_bias, m_dn_norm, m_w_branch, m_w_out, m_ffn_norm, m_w_gate_up, m_w_down, m_final_norm, v_attn_norm, v_w_in, v_sgu_ln_g, v_sgu_ln_b, v_sgu_w, v_sgu_b, v_attn_sinks, v_dn_conv_w, v_dn_a_log, v_dn_dt_bias, v_dn_norm, v_w_branch, v_w_out, v_ffn_norm, v_w_gate_up, v_w_down, v_final_norm):
    given = dict(locals())
    depth, d_model = attn_norm.shape
    s = x.shape[1]
    x2 = x.reshape(s, d_model)
    target = loss_target.reshape(s, d_model)
    posf = positions.reshape(s, 1).astype(f32)
    inv_freq = ROPE_THETA ** (-jnp.arange(0, ROPE_DIM, 2, dtype=f32) / ROPE_DIM)
    inv_head = jnp.concatenate([inv_freq, inv_freq, jnp.zeros((SWA_HD - ROPE_DIM,), f32)])
    tables = rope_tables(posf, jnp.tile(inv_head, LANES // SWA_HD).reshape(1, LANES), name="rope_tables")

    assert depth == 2, depth
    gathered = dict(zip([("w_in", 0), ("dn_conv_w", 0), ("dn_conv_w", 1)], all_gather(
        [w_in[0].astype(bf16), dn_conv_w[0], dn_conv_w[1]], name="gather_first")))
    riders = {"l0_in_proj": [("w_branch", 0), ("w_out", 0)],
              "l0_deltanet": [("w_gate_up", 0), ("w_down", 0), ("w_in", 1)],
              "l1_deltanet": [("w_branch", 1), ("w_out", 1), ("w_gate_up", 1), ("w_down", 1)]}

    def gathering(host, call, *args, **kw):
        if host not in riders:
            return call(*args, name=host, **kw)
        side = Exchange("gather", [given[n][l].astype(bf16) for n, l in riders[host]])
        out, got = call(*args, side=side, name=host, **kw)
        gathered.update(zip(riders[host], got))
        return out

    layers, saved = [], []
    h_in = x2
    for l in range(depth):
        t = f"l{l}_"
        w_main, w_ba = _w_in_sections(gathered["w_in", l])
        conv_full = jnp.transpose(gathered["dn_conv_w", l], (1, 0, 2)).reshape(DN_CONV, -1)
        p = dict(
            w_main=w_main, w_ba=w_ba, conv_w8=jnp.pad(conv_full, ((0, CONV_PAD - DN_CONV), (0, 0))),
            attn_norm=attn_norm[l].reshape(1, -1), ffn_norm=ffn_norm[l].reshape(1, -1),
            ln_g=sgu_ln_g[l].reshape(1, -1), ln_b=sgu_ln_b[l].reshape(1, -1), sgu_w=sgu_w[l], sgu_bt=sgu_b[l].T,
            sink_row=_pad_lanes(attn_sinks[l]), alog_row=_pad_lanes(dn_a_log[l], at=DN_HEADS),
            dtb_row=_pad_lanes(dn_dt_bias[l], at=DN_HEADS), norm_row=dn_norm[l].reshape(1, -1))
        layers.append(p)
        h = rmsnorm_fwd(h_in, p["attn_norm"], name=t + "attn_norm")
        proj = gathering(t + "in_proj", matmul, h, p["w_main"], "nn", bf16)
        p_ba = matmul(h, p["w_ba"], "nn", f32, name=t + "in_proj_ba")
        out_a = sgu_fwd(proj, p["ln_g"], p["ln_b"], p["sgu_w"], p["sgu_bt"], name=t + "sgu")
        out_b = swa_fwd(proj, p["sink_row"], tables, name=t + "swa")
        xc = conv_fwd(proj, p["conv_w8"], name=t + "dn_conv")
        out_c, states, tri = gathering(t + "deltanet", dn_fwd, xc, proj, p_ba, p["alog_row"], p["dtb_row"], p["norm_row"])
        p.update(w_branch=jnp.transpose(gathered["w_branch", l], (1, 2, 0, 3)).reshape(3, MIX, d_model),
                 w_out=gathered["w_out", l].reshape(d_model, d_model),
                 w_gu=gathered["w_gate_up", l],
                 w_down=gathered["w_down", l].reshape(N_DEV // 2, -1, d_model))
        merged = merge_fwd(out_a, out_b, out_c, proj, p["w_branch"], name=t + "merge")
        x_mid = matmul(merged, p["w_out"], "nn", f32, residual=h_in, tn=1024, name=t + "out_proj")
        h2 = rmsnorm_fwd(x_mid, p["ffn_norm"], name=t + "ffn_norm")
        gu = matmul(h2, p["w_gu"], "nn", bf16, group="n", name=t + "gate_up")
        act = swiglu_fwd(gu, name=t + "swiglu")
        x_out = matmul(act, p["w_down"], "nn", f32, residual=x_mid, group="k", tn=1024, name=t + "down")
        saved.append(dict(x_in=h_in, h=h, proj=proj, p_ba=p_ba, out_a=out_a, out_b=out_b, out_c=out_c, xc=xc,
                          states=states, tri=tri, merged=merged, x_mid=x_mid, h2=h2, gu=gu, act=act))
        h_in = x_out

    dx, d_final_norm, loss_row = loss_head(h_in, final_norm.reshape(1, -1), target, name="loss_head")
    loss = lax.psum(loss_row[0, 0], MESH_AXES)

    shard_names = ["w_in", "dn_conv_w", "w_branch", "w_out", "w_gate_up", "w_down"]
    rep_names = ["attn_norm", "sgu_ln_g", "sgu_ln_b", "sgu_w", "sgu_b", "attn_sinks", "dn_a_log", "dn_dt_bias",
                 "dn_norm", "ffn_norm"]
    parts, received, per_layer = {}, {}, []
    senders = {"l0_b_swa": [("w_gate_up", 1), ("w_down", 1), ("w_out", 1), ("w_branch", 1)],
               "l0_b_deltanet": [("w_in", 1), ("w_gate_up", 0), ("w_down", 0), ("w_out", 0), ("w_branch", 0)],
               "l0_b_in_proj_dx": [("w_in", 0)]}

    def scattering(host, call, *args, **kw):
        if host not in senders:
            return call(*args, name=host, **kw)
        out, got = call(*args, side=Exchange("scatter", [parts[key] for key in senders[host]]), name=host, **kw)
        received.update(zip(senders[host], got))
        return out

    for l in reversed(range(depth)):
        p, sv, t = layers[l], saved[l], f"l{l}_b_"
        d_act = matmul(dx, p["w_down"], "nt", bf16, group="n", name=t + "down_dx")
        gw_down = matmul(sv["act"], dx, "tn", bf16, group="m", tk=2048, tn=512, name=t + "down_dw")
        d_gu = swiglu_bwd(sv["gu"], d_act, name=t + "swiglu")
        gw_gu = matmul(sv["h2"], d_gu, "tn", bf16, group="n", tk=2048, name=t + "gate_up_dw")
        d_h2 = matmul(d_gu, p["w_gu"], "nt", f32, group="k", tn=1024, name=t + "gate_up_dx")
        dx_mid, g_ffn = rmsnorm_bwd(sv["x_mid"], p["ffn_norm"], d_h2, dx, name=t + "ffn_norm")
        d_merged = matmul(dx_mid, p["w_out"], "nt", f32, tn=1024, name=t + "out_proj_dx")
        gw_out = matmul(sv["merged"], dx_mid, "tn", bf16, tk=2048, tn=1024, name=t + "out_proj_dw")
        dproj = lax.empty((s, W_MAIN), bf16)
        dproj, d_a, d_b, d_c, gw_branch = merge_bwd(sv["out_a"], sv["out_b"], sv["out_c"], sv["proj"], p["w_branch"],
                                                   d_merged, dproj, name=t + "merge")
        parts.update({("w_gate_up", l): gw_gu, ("w_down", l): gw_down.reshape(N_DEV, -1, d_model),
                      ("w_out", l): gw_out.reshape(N_DEV, -1, d_model),
                      ("w_branch", l): jnp.transpose(gw_branch.reshape(3, MIX, N_DEV, -1), (2, 0, 1, 3)).astype(bf16)})
        dproj, g_ln_g, g_ln_b, g_sgu_w, g_sgu_bt = sgu_bwd(sv["proj"], p["ln_g"], p["ln_b"], p["sgu_w"], p["sgu_bt"],
                                                         d_a, dproj, name=t + "sgu")
        dproj, g_sink = scattering(t + "swa", swa_bwd, sv["proj"], p["sink_row"], tables, d_b, dproj)
        dproj, dxc, dba, g_alog, g_dtb, g_dnorm = scattering(
            t + "deltanet", dn_bwd, sv["xc"], sv["proj"], sv["p_ba"], p["alog_row"], p["dtb_row"], p["norm_row"],
            sv["states"], sv["tri"], d_c, dproj)
        dproj, g_conv8 = conv_bwd(sv["proj"], p["conv_w8"], dxc, dproj, name=t + "dn_conv")
        gw_main = matmul(sv["h"], dproj, "tn", bf16, tk=2048, name=t + "in_proj_dw")
        gw_ba = matmul(sv["h"], dba, "tn", bf16, tk=2048, name=t + "in_proj_ba_dw")
        parts["w_in", l] = _w_in_parts(gw_main, gw_ba)
        d_h = scattering(t + "in_proj_dx", matmul, dproj, p["w_main"], "nt", f32, tk=2304, tn=1024)
        d_h = matmul(dba, p["w_ba"], "nt", f32, residual=d_h, name=t + "in_proj_ba_dx")
        dx, g_attn = rmsnorm_bwd(sv["x_in"], p["attn_norm"], d_h, dx_mid, name=t + "attn_norm")
        per_layer.append(dict(
            dn_conv_w=jnp.transpose(g_conv8[:DN_CONV].reshape(DN_CONV, N_DEV, -1), (1, 0, 2)),
            attn_norm=g_attn, sgu_ln_g=g_ln_g, sgu_ln_b=g_ln_b, sgu_w=g_sgu_w, sgu_b=g_sgu_bt.T,
            attn_sinks=g_sink[0, :SWA_HEADS], dn_a_log=g_alog[0, DN_HEADS:2 * DN_HEADS],
            dn_dt_bias=g_dtb[0, DN_HEADS:2 * DN_HEADS], dn_norm=g_dnorm, ffn_norm=g_ffn))
    per_layer.reverse()

    conv_parts = jnp.concatenate([pp["dn_conv_w"] for pp in per_layer], axis=1)
    rep_grads = {n: jnp.stack([pp[n].reshape(given[n].shape[1:]) for pp in per_layer]) for n in rep_names}
    rep_grads["final_norm"] = d_final_norm[0]
    rep_names = rep_names + ["final_norm"]
    rep_rows = [_rows128(given[n]).shape[0] for n in rep_names]
    pad_rows = -sum(rep_rows) % 16

    def small_rows(values):
        pieces = [_rows128(values[n]) for n in rep_names]
        return jnp.concatenate(pieces + [jnp.zeros((pad_rows, LANES), f32)], axis=0)

    (got_conv,), (small_all,) = exchange(
        [Exchange("scatter", [conv_parts]), Exchange("gather", [small_rows(rep_grads)])], name="exchange_last")

    results = [{}, {}, {}, {}]
    for n in shard_names:
        shp = given[n].shape
        two = (-1, shp[-1])
        by_layer = [got_conv] if n == "dn_conv_w" else [received[n, l].reshape(N_DEV, -1, shp[-1]) for l in range(depth)]
        outs = adamw(given[n].reshape(two), given["m_" + n].reshape(two), given["v_" + n].reshape(two), by_layer,
                     name="adamw_" + n)
        for res, val in zip(results, outs):
            res[n] = val.reshape(shp)
    outs = adamw(small_rows(given), small_rows({n: given["m_" + n] for n in rep_names}),
                 small_rows({n: given["v_" + n] for n in rep_names}), [small_all], name="adamw_replicated")
    for res, val in zip(results, outs):
        row = 0
        for n, nr in zip(rep_names, rep_rows):
            res[n] = val[row:row + nr].reshape(-1)[:given[n].size].reshape(given[n].shape)
            row += nr
    order = ["attn_norm", "w_in", "sgu_ln_g", "sgu_ln_b", "sgu_w", "sgu_b", "attn_sinks", "dn_conv_w", "dn_a_log",
             "dn_dt_bias", "dn_norm", "w_branch", "w_out", "ffn_norm", "w_gate_up", "w_down", "final_norm"]
    return (loss, dx.reshape(x.shape), *[res[n] for res in results for n in order])
```

```python
import functools

import jax
import jax.numpy as jnp
from jax import lax
from jax.experimental import pallas as pl
from jax.experimental.pallas import tpu as pltpu

f32 = jnp.float32
bf16 = jnp.bfloat16

N_DEV = 8
MESH_AXES = ("x", "y", "c")
NORM_EPS = 1e-6
MIX = 512
SGU_GROUPS, SGU_CHUNK = 4, 128
SWA_HEADS, SWA_KV, SWA_HD, WINDOW = 8, 2, 64, 128
ROPE_THETA, ROPE_DIM = 500000.0, 16
DN_HEADS, DN_HD, DN_CONV, DN_CHUNK = 4, 128, 4, 64
ADAM_LR, ADAM_B1, ADAM_B2, ADAM_EPS, ADAM_WD, ADAM_STEP = 0.001, 0.9, 0.999, 1e-08, 0.01, 10

LANES = 128
VMEM_LIMIT = 56 * 1024 * 1024

SEC_A, SEC_Z, SEC_QKV, SEC_G, SEC_B = 0, 1024, 1536, 3072, 6144
W_MAIN = 6912
C_B, C_QKV, C_Z, C_BA, C_G = 1024, 1792, 3328, 3840, 3848


def _params(n_axes, **kw):
    return pltpu.CompilerParams(dimension_semantics=("arbitrary",) * n_axes, vmem_limit_bytes=VMEM_LIMIT, **kw)


def _tile(n, target, mult=LANES):
    if n <= target:
        return n
    best = None
    for t in range(mult, target + 1, mult):
        if n % t == 0:
            best = t
    assert best is not None, (n, target, mult)
    return best


def _dg(a, b, ca, cb):
    return lax.dot_general(a.astype(bf16), b.astype(bf16), (((ca,), (cb,)), ((), ())), preferred_element_type=f32)


def _dg3(a, b, ca, cb):
    a_hi, b_hi = a.astype(bf16), b.astype(bf16)
    a_lo, b_lo = (a - a_hi.astype(f32)).astype(bf16), (b - b_hi.astype(f32)).astype(bf16)

    def dot(p, q):
        return lax.dot_general(p, q, (((ca,), (cb,)), ((), ())), preferred_element_type=f32)

    return dot(a_hi, b_hi) + (dot(a_hi, b_lo) + dot(a_lo, b_hi))


def _differentiable_dot(core):
    @functools.partial(jax.custom_vjp, nondiff_argnums=(2, 3))
    def dot(a, b, ca, cb):
        return core(a, b, ca, cb)

    def fwd(a, b, ca, cb):
        return core(a, b, ca, cb), (a, b)

    def bwd(ca, cb, res, ct):
        a, b = res
        da = core(ct, b, 1, 1 - cb) if ca == 1 else core(b, ct, 1 - cb, 1)
        db = core(a, ct, 1 - ca, 0) if cb == 0 else core(ct, a, 0, 1 - ca)
        return da, db

    dot.defvjp(fwd, bwd)
    return dot


bdot = _differentiable_dot(_dg)
_hdot = _differentiable_dot(_dg3)


def hdot(a, b, ca=1, cb=0):
    return _hdot(a, b, ca, cb)


@functools.partial(jax.custom_vjp, nondiff_argnums=(1,))
def lroll(x, shift):
    return pltpu.roll(x, shift, 1)


def _lroll_fwd(x, shift):
    return pltpu.roll(x, shift, 1), None


def _lroll_bwd(shift, _, ct):
    return (pltpu.roll(ct, ct.shape[1] - shift, 1),)


lroll.defvjp(_lroll_fwd, _lroll_bwd)


@functools.partial(jax.custom_vjp, nondiff_argnums=(1,))
def tri_inv(low, nil):
    n = low.shape[0]
    row = lax.broadcasted_iota(jnp.int32, (n, n), 0)
    col = lax.broadcasted_iota(jnp.int32, (n, n), 1)
    eye = (row == col).astype(f32)
    m = -low
    p = eye + m
    span = 2
    while span < nil:
        m = hdot(m, m)
        p = p + hdot(p, m)
        span *= 2
    return p


def _tri_inv_fwd(low, nil):
    t = tri_inv(low, nil)
    return t, t


def _tri_inv_bwd(nil, t, dt):
    return (-hdot(t, hdot(dt, t, 1, 1), 0, 0),)


tri_inv.defvjp(_tri_inv_fwd, _tri_inv_bwd)


@jax.custom_vjp
def tri_inv_known(low, t):
    return t


def _tri_inv_known_fwd(low, t):
    return t, t


def _tri_inv_known_bwd(t, dt):
    return _tri_inv_bwd(None, t, dt) + (jnp.zeros_like(t),)


tri_inv_known.defvjp(_tri_inv_known_fwd, _tri_inv_known_bwd)


def _sigmoid(x):
    return 1.0 / (1.0 + jnp.exp(-x))


def _rms(x, g):
    return x * lax.rsqrt(jnp.mean(x * x, axis=-1, keepdims=True) + NORM_EPS) * g


def _lane_col(x, lane_idx):
    lane = lax.broadcasted_iota(jnp.int32, x.shape, 1)
    return jnp.sum(jnp.where(lane == lane_idx, x, 0.0), axis=1, keepdims=True)


def matmul(a, b, mode, out_dtype, *, residual=None, group=None, side=None, tm=1024, tn=768, tk=1024, name):
    dims = {"a": ("m", "k") if mode != "tn" else ("k", "m"),
            "b": {"nn": ("k", "n"), "nt": ("n", "k"), "tn": ("k", "n")}[mode], "o": ("m", "n")}
    full, groups = {}, 1
    for arr, key in ((a, "a"), (b, "b")):
        grouped = group in dims[key]
        if grouped:
            groups = arr.shape[0]
        full[dims[key][0]], full[dims[key][1]] = arr.shape[1:] if grouped else arr.shape
    want = {"m": tm, "n": tn, "k": tk}
    per_step = min(groups, 4) if group == "k" else 1
    tiles = {d: full[d] if d == group else _tile(full[d], want[d]) for d in "mnk"}
    steps = {d: groups // per_step if d == group else full[d] // tiles[d] for d in "mnk"}

    def spec(key):
        d0, d1 = dims[key]

        def index(i, j, kk):
            at = {"m": i, "n": j, "k": kk}
            if group in (d0, d1):
                return (at[group], 0 if d0 == group else at[d0], 0 if d1 == group else at[d1])
            return (at[d0], at[d1])

        block = (tiles[d0], tiles[d1])
        if group in (d0, d1):
            block = ((per_step if group == "k" else None),) + block
        return pl.BlockSpec(block, index)

    ca, cb = {"nn": (1, 0), "nt": (1, 1), "tn": (0, 0)}[mode]
    nk = steps["k"]
    o_spec = spec("o")
    out_shape = (groups, full["m"], full["n"]) if group in ("m", "n") else (full["m"], full["n"])
    has_res = residual is not None

    def product(a_ref, b_ref):
        if group != "k":
            return _dg(a_ref[...], b_ref[...], ca, cb)
        total = _dg(a_ref[0], b_ref[0], ca, cb)
        for g in range(1, per_step):
            total = total + _dg(a_ref[g], b_ref[g], ca, cb)
        return total

    def body(a_ref, b_ref, *rest):
        r_ref = rest[0] if has_res else None
        o_ref = rest[1 if has_res else 0]

        def emit(acc):
            o_ref[...] = (acc + r_ref[...] if has_res else acc).astype(out_dtype)

        if nk == 1:
            emit(product(a_ref, b_ref))
            return
        acc_ref = rest[-1]
        kk = pl.program_id(2)

        @pl.when(kk == 0)
        def _():
            acc_ref[...] = jnp.zeros_like(acc_ref)

        acc_ref[...] += product(a_ref, b_ref)

        @pl.when(kk == nk - 1)
        def _():
            emit(acc_ref[...])

    res = host_call(
        side, body, name=name, grid=(steps["m"], steps["n"], nk),
        in_specs=[spec("a"), spec("b")] + ([o_spec] if has_res else []), out_specs=[o_spec],
        out_shape=[jax.ShapeDtypeStruct(out_shape, out_dtype)],
        scratch_shapes=[pltpu.VMEM((tiles["m"], tiles["n"]), f32)] if nk > 1 else [], aliases={},
        args=(a, b) + ((residual,) if has_res else ()))
    return res[0] if side is None else (res[0][0], res[1])


def rmsnorm_fwd(x, g_row, *, name):
    s, d = x.shape
    ts = _tile(s, 512, 16)

    def body(x_ref, g_ref, o_ref):
        o_ref[...] = _rms(x_ref[...], g_ref[...]).astype(bf16)

    return pl.pallas_call(
        body, name=name, grid=(s // ts,),
        in_specs=[pl.BlockSpec((ts, d), lambda i: (i, 0)), pl.BlockSpec((1, d), lambda i: (0, 0))],
        out_specs=pl.BlockSpec((ts, d), lambda i: (i, 0)), out_shape=jax.ShapeDtypeStruct((s, d), bf16),
        compiler_params=_params(1),
    )(x, g_row)


def rmsnorm_bwd(x, g_row, dh, dres, *, name):
    s, d = x.shape
    ts = _tile(s, 512, 16)

    def body(x_ref, g_ref, dh_ref, dres_ref, dx_ref, dg_ref):
        @pl.when(pl.program_id(0) == 0)
        def _():
            dg_ref[...] = jnp.zeros_like(dg_ref)

        _, vjp = jax.vjp(_rms, x_ref[...], g_ref[...])
        dx, dg = vjp(dh_ref[...])
        dx_ref[...] = dx + dres_ref[...]
        dg_ref[...] += dg

    row = pl.BlockSpec((ts, d), lambda i: (i, 0))
    vec = pl.BlockSpec((1, d), lambda i: (0, 0))
    return pl.pallas_call(
        body, name=name, grid=(s // ts,), in_specs=[row, vec, row, row], out_specs=[row, vec],
        out_shape=[jax.ShapeDtypeStruct((s, d), f32), jax.ShapeDtypeStruct((1, d), f32)],
        compiler_params=_params(1),
    )(x, g_row, dh, dres)


def loss_head(x, g_row, target, *, name):
    s, d = x.shape
    ts = _tile(s, 512, 16)

    def body(x_ref, g_ref, t_ref, dx_ref, dg_ref, loss_ref):
        @pl.when(pl.program_id(0) == 0)
        def _():
            dg_ref[...] = jnp.zeros_like(dg_ref)
            loss_ref[...] = jnp.zeros_like(loss_ref)

        y, vjp = jax.vjp(_rms, x_ref[...], g_ref[...])
        err = y - t_ref[...]
        dx, dg = vjp(err * (1.0 / d))
        dx_ref[...] = dx
        dg_ref[...] += dg
        loss_ref[...] += 0.5 * jnp.sum(jnp.sum(err * err, axis=1, keepdims=True) * (1.0 / d), axis=0, keepdims=True)

    row = pl.BlockSpec((ts, d), lambda i: (i, 0))
    vec = pl.BlockSpec((1, d), lambda i: (0, 0))
    one = pl.BlockSpec((1, LANES), lambda i: (0, 0))
    return pl.pallas_call(
        body, name=name, grid=(s // ts,), in_specs=[row, vec, row], out_specs=[row, vec, one],
        out_shape=[jax.ShapeDtypeStruct((s, d), f32), jax.ShapeDtypeStruct((1, d), f32),
                   jax.ShapeDtypeStruct((1, LANES), f32)],
        compiler_params=_params(1),
    )(x, g_row, target)


def _sgu_chunk(p_a, ln_g, ln_b, w, b_t):
    t = SGU_CHUNK
    u = jax.nn.gelu(p_a[:, :MIX])
    v = jax.nn.gelu(p_a[:, MIX:])
    vc = v - jnp.mean(v, axis=-1, keepdims=True)
    vn = vc * lax.rsqrt(jnp.mean(vc * vc, axis=-1, keepdims=True) + NORM_EPS) * ln_g + ln_b
    causal = lax.broadcasted_iota(jnp.int32, (t, t), 0) >= lax.broadcasted_iota(jnp.int32, (t, t), 1)
    outs = []
    for g in range(SGU_GROUPS):
        sl = slice(g * LANES, (g + 1) * LANES)
        mixed = bdot(jnp.where(causal, w[g], 0.0), vn[:, sl], 1, 0) + b_t[:, g:g + 1]
        outs.append(u[:, sl] * mixed)
    return jnp.concatenate(outs, axis=1)


def _sgu_specs(s, ts):
    return [pl.BlockSpec((ts, 2 * MIX), lambda i: (i, SEC_A // (2 * MIX))),
            pl.BlockSpec((1, MIX), lambda i: (0, 0)), pl.BlockSpec((1, MIX), lambda i: (0, 0)),
            pl.BlockSpec((SGU_GROUPS, SGU_CHUNK, SGU_CHUNK), lambda i: (0, 0, 0)),
            pl.BlockSpec((SGU_CHUNK, SGU_GROUPS), lambda i: (0, 0))]


def sgu_fwd(proj, ln_g, ln_b, w, b_t, *, name):
    s = proj.shape[0]
    ts = _tile(s, 512)
    n_chunk = ts // SGU_CHUNK

    def body(p_ref, g_ref, b_ref, w_ref, bt_ref, o_ref):
        def step(c, carry):
            rows = pl.ds(pl.multiple_of(c * SGU_CHUNK, SGU_CHUNK), SGU_CHUNK)
            o_ref[rows, :] = _sgu_chunk(p_ref[rows, :].astype(f32), g_ref[...], b_ref[...], w_ref[...], bt_ref[...]).astype(bf16)
            return carry
        lax.fori_loop(0, n_chunk, step, 0)

    return pl.pallas_call(
        body, name=name, grid=(s // ts,), in_specs=_sgu_specs(s, ts),
        out_specs=pl.BlockSpec((ts, MIX), lambda i: (i, 0)), out_shape=jax.ShapeDtypeStruct((s, MIX), bf16),
        compiler_params=_params(1),
    )(proj, ln_g, ln_b, w, b_t)


def sgu_bwd(proj, ln_g, ln_b, w, b_t, d_out, dproj, *, name):
    s = proj.shape[0]
    ts = _tile(s, 512)
    n_chunk = ts // SGU_CHUNK

    def body(p_ref, g_ref, b_ref, w_ref, bt_ref, do_ref, _, dp_ref, dg_ref, db_ref, dw_ref, dbt_ref):
        @pl.when(pl.program_id(0) == 0)
        def _():
            dg_ref[...] = jnp.zeros_like(dg_ref)
            db_ref[...] = jnp.zeros_like(db_ref)
            dw_ref[...] = jnp.zeros_like(dw_ref)
            dbt_ref[...] = jnp.zeros_like(dbt_ref)

        def step(c, carry):
            rows = pl.ds(pl.multiple_of(c * SGU_CHUNK, SGU_CHUNK), SGU_CHUNK)
            _, vjp = jax.vjp(_sgu_chunk, p_ref[rows, :].astype(f32), g_ref[...], b_ref[...], w_ref[...], bt_ref[...])
            dp, dg, db, dw, dbt = vjp(do_ref[rows, :])
            dp_ref[rows, :] = dp.astype(bf16)
            dg_ref[...] += dg
            db_ref[...] += db
            dw_ref[...] += dw
            dbt_ref[...] += dbt
            return carry
        lax.fori_loop(0, n_chunk, step, 0)

    specs = _sgu_specs(s, ts)
    return pl.pallas_call(
        body, name=name, grid=(s // ts,),
        in_specs=specs + [pl.BlockSpec((ts, MIX), lambda i: (i, 0)), pl.BlockSpec(memory_space=pl.ANY)],
        out_specs=[specs[0], specs[1], specs[2], specs[3], specs[4]],
        out_shape=[jax.ShapeDtypeStruct(dproj.shape, bf16), jax.ShapeDtypeStruct((1, MIX), f32),
                   jax.ShapeDtypeStruct((1, MIX), f32), jax.ShapeDtypeStruct(w.shape, f32),
                   jax.ShapeDtypeStruct(b_t.shape, f32)],
        input_output_aliases={6: 0}, compiler_params=_params(1),
    )(proj, ln_g, ln_b, w, b_t, d_out, dproj)


def rope_tables(posf, inv_freq, *, name):
    s = posf.shape[0]
    ts = _tile(s, 1024, 8)
    half = ROPE_DIM // 2

    def body(pos_ref, inv_ref, o_ref):
        d = lax.broadcasted_iota(jnp.int32, (1, LANES), 1) % SWA_HD
        ang = pos_ref[...] * inv_ref[...]
        sin = jnp.sin(ang)
        o_ref[0] = jnp.cos(ang)
        o_ref[1] = jnp.where(d < half, sin, 0.0)
        o_ref[2] = jnp.where((d >= half) & (d < ROPE_DIM), sin, 0.0)

    return pl.pallas_call(
        body, name=name, grid=(s // ts,),
        in_specs=[pl.BlockSpec((ts, 1), lambda i: (i, 0)), pl.BlockSpec((1, LANES), lambda i: (0, 0))],
        out_specs=pl.BlockSpec((3, ts, LANES), lambda i: (0, i, 0)), out_shape=jax.ShapeDtypeStruct((3, s, LANES), f32),
        compiler_params=_params(1),
    )(posf, inv_freq)


def _rope(x, table):
    w = x.shape[1]
    half = ROPE_DIM // 2
    c, lo, hi = (jnp.concatenate([table[i]] * (w // LANES), axis=1) for i in range(3))
    return x * c - lroll(x, w - half) * lo + lroll(x, half) * hi


def _swa_block(q, kp, kc, vp, vc, sink_row, table_q, table_p, prev_ok, wide):
    t = WINDOW
    q = _rope(q, table_q) * (SWA_HD ** -0.5)
    keys = jnp.concatenate([_rope(kp, table_p), _rope(kc, table_q)], axis=0)
    vals = jnp.concatenate([vp, vc], axis=0)
    own = lax.broadcasted_iota(jnp.int32, (t, t), 0) >= lax.broadcasted_iota(jnp.int32, (t, t), 1)
    lane_half = lax.broadcasted_iota(jnp.int32, (t, LANES), 1) // SWA_HD
    group = SWA_HEADS // SWA_KV
    slabs = []
    for pair in range(SWA_HEADS // 2):
        q_pair = q[:, pair * LANES:(pair + 1) * LANES]
        acc = jnp.zeros((t, LANES), f32)
        for half in range(2):
            h = 2 * pair + half
            kv = h // group
            qm = jnp.where(lane_half == half, q_pair, 0.0)
            if half != kv:
                qm = lroll(qm, SWA_HD)
            if wide:
                both = bdot(qm, keys, 1, 1)
                s_prev, s_own = both[:, :t], both[:, t:]
            else:
                s_prev, s_own = bdot(qm, keys[:t], 1, 1), bdot(qm, keys[t:], 1, 1)
            logits = jnp.where(own, s_own, jnp.where(prev_ok, s_prev, -1e30))
            sink = _lane_col(sink_row, h)
            m = lax.stop_gradient(jnp.maximum(jnp.max(logits, axis=1, keepdims=True), sink))
            p = jnp.exp(logits - m)
            probs = p * (1.0 / (jnp.sum(p, axis=1, keepdims=True) + jnp.exp(sink - m)))
            p_prev, p_own = jnp.where(own, 0.0, probs), jnp.where(own, probs, 0.0)
            if wide:
                o = bdot(jnp.concatenate([p_prev, p_own], axis=1), vals, 1, 0)
            else:
                o = bdot(p_prev, vals[:t], 1, 0) + bdot(p_own, vals[t:], 1, 0)
            o = jnp.where(lane_half == kv, o, 0.0)
            if half != kv:
                o = lroll(o, SWA_HD)
            acc = acc + o
        slabs.append(acc)
    return jnp.concatenate(slabs, axis=1)


def _swa_in_specs(nc, clamp):
    t = WINDOW
    qb, kb, vb = SEC_B // MIX, (SEC_B + MIX) // LANES, (SEC_B + MIX + LANES) // LANES

    def cur(i):
        return jnp.minimum(i, nc - 1) if clamp else i

    def prev(i):
        return jnp.maximum(cur(i) - 1, 0)

    return [pl.BlockSpec((t, MIX), lambda i: (cur(i), qb)),
            pl.BlockSpec((t, LANES), lambda i: (prev(i), kb)), pl.BlockSpec((t, LANES), lambda i: (cur(i), kb)),
            pl.BlockSpec((t, LANES), lambda i: (prev(i), vb)), pl.BlockSpec((t, LANES), lambda i: (cur(i), vb)),
            pl.BlockSpec((1, LANES), lambda i: (0, 0)),
            pl.BlockSpec((3, t, LANES), lambda i: (0, cur(i), 0)), pl.BlockSpec((3, t, LANES), lambda i: (0, prev(i), 0))]


def swa_fwd(proj, sink_row, tables, *, name):
    s = proj.shape[0]
    nc = s // WINDOW

    def body(q_ref, kp_ref, kc_ref, vp_ref, vc_ref, sink_ref, tq_ref, tp_ref, o_ref):
        prev_ok = pl.program_id(0) > 0
        blocks = [r[...].astype(f32) for r in (q_ref, kp_ref, kc_ref, vp_ref, vc_ref)]
        o_ref[...] = _swa_block(*blocks, sink_ref[...],
                                tq_ref[...], tp_ref[...], prev_ok, True).astype(bf16)

    return pl.pallas_call(
        body, name=name, grid=(nc,), in_specs=_swa_in_specs(nc, False),
        out_specs=pl.BlockSpec((WINDOW, MIX), lambda i: (i, 0)), out_shape=jax.ShapeDtypeStruct((s, MIX), bf16),
        compiler_params=_params(1),
    )(proj, proj, proj, proj, proj, sink_row, tables, tables)


def swa_bwd(proj, sink_row, tables, d_out, dproj, *, side=None, name):
    s = proj.shape[0]
    nc = s // WINDOW
    t = WINDOW

    def body(q_ref, kp_ref, kc_ref, vp_ref, vc_ref, sink_ref, tq_ref, tp_ref, do_ref, _,
             dp_ref, dsink_ref, cq_ref, ck_ref, cv_ref):
        i = pl.program_id(0)

        @pl.when(i == 0)
        def _():
            dsink_ref[...] = jnp.zeros_like(dsink_ref)

        def write(dk_prev, dv_prev):
            dp_ref[:, :MIX] = cq_ref[...].astype(bf16)
            dp_ref[:, MIX:MIX + LANES] = (ck_ref[...] + dk_prev).astype(bf16)
            dp_ref[:, MIX + LANES:] = (cv_ref[...] + dv_prev).astype(bf16)

        @pl.when(i < nc)
        def _():
            fn = functools.partial(_swa_block, table_q=tq_ref[...], table_p=tp_ref[...], prev_ok=i > 0, wide=False)
            blocks = [r[...].astype(f32) for r in (q_ref, kp_ref, kc_ref, vp_ref, vc_ref)]
            _, vjp = jax.vjp(fn, *blocks, sink_ref[...])
            dq, dkp, dkc, dvp, dvc, dsink = vjp(do_ref[...])
            dsink_ref[...] += dsink

            @pl.when(i > 0)
            def _():
                write(dkp, dvp)

            cq_ref[...] = dq
            ck_ref[...] = dkc
            cv_ref[...] = dvc

        @pl.when(i == nc)
        def _():
            write(0.0, 0.0)

    return host_call(
        side, body, name=name, grid=(nc + 1,),
        in_specs=_swa_in_specs(nc, True) + [pl.BlockSpec((t, MIX), lambda i: (jnp.minimum(i, nc - 1), 0)),
                                            pl.BlockSpec(memory_space=pl.ANY)],
        out_specs=[pl.BlockSpec((t, MIX + 2 * LANES), lambda i: (jnp.maximum(i - 1, 0), SEC_B // (MIX + 2 * LANES))),
                   pl.BlockSpec((1, LANES), lambda i: (0, 0))],
        out_shape=[jax.ShapeDtypeStruct(dproj.shape, bf16), jax.ShapeDtypeStruct((1, LANES), f32)],
        scratch_shapes=[pltpu.VMEM((t, MIX), f32), pltpu.VMEM((t, LANES), f32), pltpu.VMEM((t, LANES), f32)],
        aliases={9: 0}, args=(proj, proj, proj, proj, proj, sink_row, tables, tables, d_out, dproj))


CONV_PAD = 16


def _conv_taps(xp, rows):
    off = CONV_PAD - (DN_CONV - 1)
    return [xp[off + i:off + i + rows] for i in range(DN_CONV)]


def _conv_pre(taps, w):
    pre = taps[0] * w[0:1]
    for i in range(1, DN_CONV):
        pre = pre + taps[i] * w[i:i + 1]
    return pre


def conv_fwd(proj, conv_w8, *, name):
    s = proj.shape[0]
    wq = 3 * MIX
    ts = _tile(s, 512)
    nb = ts // CONV_PAD

    def body(x_ref, prev_ref, w_ref, o_ref):
        prev = jnp.where(pl.program_id(0) > 0, prev_ref[...].astype(f32), 0.0)
        pre = _conv_pre(_conv_taps(jnp.concatenate([prev, x_ref[...].astype(f32)], axis=0), ts), w_ref[...])
        o_ref[...] = pre * _sigmoid(pre)

    return pl.pallas_call(
        body, name=name, grid=(s // ts,),
        in_specs=[pl.BlockSpec((ts, wq), lambda i: (i, SEC_QKV // wq)),
                  pl.BlockSpec((CONV_PAD, wq), lambda i: (jnp.maximum(i * nb - 1, 0), SEC_QKV // wq)),
                  pl.BlockSpec((CONV_PAD, wq), lambda i: (0, 0))],
        out_specs=pl.BlockSpec((ts, wq), lambda i: (i, 0)), out_shape=jax.ShapeDtypeStruct((s, wq), f32),
        compiler_params=_params(1),
    )(proj, proj, conv_w8)


def conv_bwd(proj, conv_w8, dxc, dproj, *, name):
    s = proj.shape[0]
    wq = 3 * MIX
    ts = _tile(s, 512)
    nb = ts // CONV_PAD
    nt = s // ts
    last_blk = s // CONV_PAD - 1

    def body(x_ref, prev_ref, next_ref, w_ref, d_ref, dnext_ref, _, dp_ref, dw_ref):
        i = pl.program_id(0)

        @pl.when(i == 0)
        def _():
            dw_ref[...] = jnp.zeros_like(dw_ref)

        w = w_ref[...]
        prev = jnp.where(i > 0, prev_ref[...].astype(f32), 0.0)
        more = i < nt - 1
        xp = jnp.concatenate([prev, x_ref[...].astype(f32), jnp.where(more, next_ref[...].astype(f32), 0.0)], axis=0)
        taps = _conv_taps(xp, ts + CONV_PAD)
        pre = _conv_pre(taps, w)
        sig = _sigmoid(pre)
        dxc_ext = jnp.concatenate([d_ref[...], jnp.where(more, dnext_ref[...], 0.0)], axis=0)
        dpre = dxc_ext * sig * (1.0 + pre * (1.0 - sig))
        d_raw = jnp.zeros((ts, wq), f32)
        dws = []
        for k in range(DN_CONV):
            shift = DN_CONV - 1 - k
            d_raw = d_raw + dpre[shift:shift + ts] * w[k:k + 1]
            dws.append(jnp.sum(dpre[:ts] * taps[k][:ts], axis=0, keepdims=True))
        dp_ref[...] = d_raw.astype(bf16)
        dw_ref[...] += jnp.concatenate(dws + [jnp.zeros((CONV_PAD - DN_CONV, wq), f32)], axis=0)

    sec = SEC_QKV // wq
    return pl.pallas_call(
        body, name=name, grid=(nt,),
        in_specs=[pl.BlockSpec((ts, wq), lambda i: (i, sec)),
                  pl.BlockSpec((CONV_PAD, wq), lambda i: (jnp.maximum(i * nb - 1, 0), sec)),
                  pl.BlockSpec((CONV_PAD, wq), lambda i: (jnp.minimum((i + 1) * nb, last_blk), sec)),
                  pl.BlockSpec((CONV_PAD, wq), lambda i: (0, 0)),
                  pl.BlockSpec((ts, wq), lambda i: (i, 0)),
                  pl.BlockSpec((CONV_PAD, wq), lambda i: (jnp.minimum((i + 1) * nb, last_blk), 0)),
                  pl.BlockSpec(memory_space=pl.ANY)],
        out_specs=[pl.BlockSpec((ts, wq), lambda i: (i, sec)), pl.BlockSpec((CONV_PAD, wq), lambda i: (0, 0))],
        out_shape=[jax.ShapeDtypeStruct(dproj.shape, bf16), jax.ShapeDtypeStruct((CONV_PAD, wq), f32)],
        input_output_aliases={6: 0}, compiler_params=_params(1),
    )(proj, proj, proj, conv_w8, dxc, dxc, dproj)


def _dn_chunk(state, xc, z, ba, alog_row, dtb_row, norm_row, t_known):
    c, nh = DN_CHUNK, DN_HEADS
    n = c * nh
    row = lax.broadcasted_iota(jnp.int32, (n, n), 0)
    col = lax.broadcasted_iota(jnp.int32, (n, n), 1)
    same_head = (row // c) == (col // c)
    tril, strict = same_head & (row >= col), same_head & (row > col)
    tril_c = lax.broadcasted_iota(jnp.int32, (c, c), 0) >= lax.broadcasted_iota(jnp.int32, (c, c), 1)
    beta_all = _sigmoid(ba)
    g_all = -jnp.exp(alog_row) * jax.nn.softplus(ba + dtb_row)
    gc_all = hdot(tril_c.astype(f32), g_all)
    gc_t = gc_all.T

    def stack(piece):
        return jnp.concatenate([piece(h) for h in range(nh)], axis=0)

    q = stack(lambda h: xc[:, h * DN_HD:(h + 1) * DN_HD])
    k = stack(lambda h: xc[:, MIX + h * DN_HD:MIX + (h + 1) * DN_HD])
    v = stack(lambda h: xc[:, 2 * MIX + h * DN_HD:2 * MIX + (h + 1) * DN_HD])
    zs = stack(lambda h: z[:, h * DN_HD:(h + 1) * DN_HD])
    q = q * lax.rsqrt(jnp.sum(q * q, axis=-1, keepdims=True) + NORM_EPS) * (DN_HD ** -0.5)
    k = k * lax.rsqrt(jnp.sum(k * k, axis=-1, keepdims=True) + NORM_EPS)
    beta = stack(lambda h: _lane_col(beta_all, h))
    g_cols = [_lane_col(gc_all, nh + h) for h in range(nh)]
    g_col = jnp.concatenate(g_cols, axis=0)
    g_row = jnp.concatenate([gc_t[nh + h:nh + h + 1, :] for h in range(nh)], axis=1)
    g_last = stack(lambda h: jnp.broadcast_to(g_cols[h][c - 1:c, :], (c, 1)))
    decay = jnp.where(tril, jnp.exp(jnp.where(tril, g_col - g_row, 0.0)), 0.0)
    kb = k * beta
    low = jnp.where(strict, bdot(kb, k, 1, 1) * decay, 0.0)
    t_inv = tri_inv(low, c) if t_known is None else tri_inv_known(low, t_known)
    e_gc = jnp.exp(g_col)
    uw = hdot(t_inv, jnp.concatenate([v * beta, kb * e_gc], axis=1))
    u, w = uw[:, :DN_HD], uw[:, DN_HD:]
    attn = bdot(q, k, 1, 1) * decay
    own = (lax.broadcasted_iota(jnp.int32, (n, nh * DN_HD), 1) // DN_HD
           == lax.broadcasted_iota(jnp.int32, (n, nh * DN_HD), 0) // c)

    def spread(a):
        return jnp.where(own, jnp.concatenate([a] * nh, axis=1), 0.0)

    v_new = u - bdot(spread(w), state, 1, 0)
    o = bdot(spread(q * e_gc), state, 1, 0) + bdot(attn, v_new, 1, 0)
    keep = stack(lambda h: jnp.broadcast_to(jnp.exp(g_cols[h][c - 1:c, :]), (DN_HD, 1)))
    new_state = state * keep + bdot(spread(k * jnp.exp(g_last - g_col)), v_new, 0, 0)
    out = _rms(o, norm_row) * (zs * _sigmoid(zs))
    return new_state, jnp.concatenate([out[h * c:(h + 1) * c] for h in range(nh)], axis=1), t_inv


DN_STEP = 4 * DN_CHUNK


def _dn_step(state, xc, z, ba, alog_row, dtb_row, norm_row, t_known=None):
    outs, t_invs = [], []
    for c in range(DN_STEP // DN_CHUNK):
        rows = slice(c * DN_CHUNK, (c + 1) * DN_CHUNK)
        state, out, t_inv = _dn_chunk(state, xc[rows], z[rows], ba[rows], alog_row, dtb_row, norm_row,
                                      None if t_known is None else t_known[c])
        outs.append(out)
        t_invs.append(t_inv)
    return state, jnp.concatenate(outs, axis=0), jnp.stack(t_invs)


def _dn_specs(ts, order):
    zb = SEC_Z // MIX
    return [pl.BlockSpec((ts, 3 * MIX), lambda i: (order(i), 0)),
            pl.BlockSpec((ts, MIX), lambda i: (order(i), zb)),
            pl.BlockSpec((ts, LANES), lambda i: (order(i), 0)),
            pl.BlockSpec((1, LANES), lambda i: (0, 0)), pl.BlockSpec((1, LANES), lambda i: (0, 0)),
            pl.BlockSpec((1, LANES), lambda i: (0, 0))]


def dn_fwd(xc, proj, p_ba, alog_row, dtb_row, norm_row, *, side=None, name):
    s = xc.shape[0]
    ts = _tile(s, 512)
    n_step = ts // DN_STEP

    per_step = DN_STEP // DN_CHUNK
    n_tri = DN_HEADS * DN_CHUNK

    def body(xc_ref, z_ref, ba_ref, al_ref, dt_ref, nr_ref, o_ref, st_ref, tri_ref, state_ref):
        @pl.when(pl.program_id(0) == 0)
        def _():
            state_ref[...] = jnp.zeros_like(state_ref)

        def step(c, carry):
            rows = pl.ds(pl.multiple_of(c * DN_STEP, DN_STEP), DN_STEP)
            st_ref[c] = state_ref[...]
            new_state, out, t_invs = _dn_step(state_ref[...], xc_ref[rows, :], z_ref[rows, :].astype(f32), ba_ref[rows, :],
                                              al_ref[...], dt_ref[...], nr_ref[...])
            state_ref[...] = new_state
            o_ref[rows, :] = out.astype(bf16)
            tri_ref[pl.ds(c * per_step, per_step)] = t_invs
            return carry
        lax.fori_loop(0, n_step, step, 0)

    return host_call(
        side, body, name=name, grid=(s // ts,), in_specs=_dn_specs(ts, lambda i: i),
        out_specs=[pl.BlockSpec((ts, MIX), lambda i: (i, 0)),
                   pl.BlockSpec((n_step, DN_HEADS * DN_HD, DN_HD), lambda i: (i, 0, 0)),
                   pl.BlockSpec((n_step * per_step, n_tri, n_tri), lambda i: (i, 0, 0))],
        out_shape=[jax.ShapeDtypeStruct((s, MIX), bf16),
                   jax.ShapeDtypeStruct((s // DN_STEP, DN_HEADS * DN_HD, DN_HD), f32),
                   jax.ShapeDtypeStruct((s // DN_CHUNK, n_tri, n_tri), f32)],
        scratch_shapes=[pltpu.VMEM((DN_HEADS * DN_HD, DN_HD), f32)], aliases={},
        args=(xc, proj, p_ba, alog_row, dtb_row, norm_row))


def dn_bwd(xc, proj, p_ba, alog_row, dtb_row, norm_row, saved, tri, d_out, dproj, *, side=None, name):
    s = xc.shape[0]
    ts = _tile(s, 512)
    n_step = ts // DN_STEP
    nt = s // ts

    per_step = DN_STEP // DN_CHUNK
    n_tri = DN_HEADS * DN_CHUNK

    def body(xc_ref, z_ref, ba_ref, al_ref, dt_ref, nr_ref, st_ref, tri_ref, do_ref, _,
             dz_ref, dxc_ref, dba_ref, dal_ref, ddt_ref, dnr_ref, dstate_ref):
        @pl.when(pl.program_id(0) == 0)
        def _():
            dstate_ref[...] = jnp.zeros_like(dstate_ref)
            dal_ref[...] = jnp.zeros_like(dal_ref)
            ddt_ref[...] = jnp.zeros_like(ddt_ref)
            dnr_ref[...] = jnp.zeros_like(dnr_ref)

        def step(it, carry):
            c = n_step - 1 - it
            rows = pl.ds(pl.multiple_of(c * DN_STEP, DN_STEP), DN_STEP)
            t_known = tri_ref[pl.ds(c * per_step, per_step)]
            _, vjp = jax.vjp(lambda *a: _dn_step(*a, t_known=t_known)[:2], st_ref[c], xc_ref[rows, :],
                             z_ref[rows, :].astype(f32), ba_ref[rows, :], al_ref[...], dt_ref[...], nr_ref[...])
            d_in, dxc, dz, dba, dal, ddt, dnr = vjp((dstate_ref[...], do_ref[rows, :]))
            dstate_ref[...] = d_in
            dxc_ref[rows, :] = dxc
            dz_ref[rows, :] = dz.astype(bf16)
            dba_ref[rows, :] = dba.astype(bf16)
            dal_ref[...] += dal
            ddt_ref[...] += ddt
            dnr_ref[...] += dnr
            return carry
        lax.fori_loop(0, n_step, step, 0)

    def rev(i):
        return nt - 1 - i

    specs = _dn_specs(ts, rev)
    vec = pl.BlockSpec((1, LANES), lambda i: (0, 0))
    return host_call(
        side, body, name=name, grid=(nt,),
        in_specs=specs + [pl.BlockSpec((n_step, DN_HEADS * DN_HD, DN_HD), lambda i: (rev(i), 0, 0)),
                          pl.BlockSpec((n_step * per_step, n_tri, n_tri), lambda i: (rev(i), 0, 0)),
                          pl.BlockSpec((ts, MIX), lambda i: (rev(i), 0)), pl.BlockSpec(memory_space=pl.ANY)],
        out_specs=[specs[1], specs[0], specs[2], vec, vec, vec],
        out_shape=[jax.ShapeDtypeStruct(dproj.shape, bf16), jax.ShapeDtypeStruct((s, 3 * MIX), f32),
                   jax.ShapeDtypeStruct((s, LANES), bf16)] + [jax.ShapeDtypeStruct((1, LANES), f32)] * 3,
        scratch_shapes=[pltpu.VMEM((DN_HEADS * DN_HD, DN_HD), f32)], aliases={9: 0},
        args=(xc, proj, p_ba, alog_row, dtb_row, norm_row, saved, tri, d_out, dproj))


def _merge_in_specs(ts, d):
    row = pl.BlockSpec((ts, MIX), lambda i: (i, 0))
    return [row, row, row, pl.BlockSpec((ts, 3 * d), lambda i: (i, SEC_G // (3 * d))),
            pl.BlockSpec((3, MIX, d), lambda i: (0, 0, 0))]


def merge_fwd(out_a, out_b, out_c, proj, w_branch, *, name):
    s, d = out_a.shape[0], w_branch.shape[2]
    ts = _tile(s, 512, 16)

    def body(a_ref, b_ref, c_ref, g_ref, w_ref, o_ref):
        acc = jnp.zeros((ts, d), f32)
        for n, r in enumerate((a_ref, b_ref, c_ref)):
            acc = acc + _sigmoid(g_ref[:, n * d:(n + 1) * d].astype(f32)) * _dg(r[...], w_ref[n], 1, 0)
        o_ref[...] = acc.astype(bf16)

    return pl.pallas_call(
        body, name=name, grid=(s // ts,), in_specs=_merge_in_specs(ts, d),
        out_specs=pl.BlockSpec((ts, d), lambda i: (i, 0)), out_shape=jax.ShapeDtypeStruct((s, d), bf16),
        compiler_params=_params(1),
    )(out_a, out_b, out_c, proj, w_branch)


def merge_bwd(out_a, out_b, out_c, proj, w_branch, d_merged, dproj, *, name):
    s, d = out_a.shape[0], w_branch.shape[2]
    ts = _tile(s, 512, 16)

    def body(a_ref, b_ref, c_ref, g_ref, w_ref, dm_ref, _, dg_ref, da_ref, db_ref, dc_ref, dw_ref):
        @pl.when(pl.program_id(0) == 0)
        def _():
            dw_ref[...] = jnp.zeros_like(dw_ref)

        dm = dm_ref[...]
        for n, (r, dr) in enumerate(((a_ref, da_ref), (b_ref, db_ref), (c_ref, dc_ref))):
            gate = _sigmoid(g_ref[:, n * d:(n + 1) * d].astype(f32))
            branch = _dg(r[...], w_ref[n], 1, 0)
            dg_ref[:, n * d:(n + 1) * d] = (dm * branch * gate * (1.0 - gate)).astype(bf16)
            d_branch = dm * gate
            dr[...] = _dg(d_branch, w_ref[n], 1, 1)
            dw_ref[n] += _dg(r[...], d_branch, 0, 0)

    specs = _merge_in_specs(ts, d)
    row_f = pl.BlockSpec((ts, MIX), lambda i: (i, 0))
    return pl.pallas_call(
        body, name=name, grid=(s // ts,),
        in_specs=specs + [pl.BlockSpec((ts, d), lambda i: (i, 0)), pl.BlockSpec(memory_space=pl.ANY)],
        out_specs=[specs[3], row_f, row_f, row_f, specs[4]],
        out_shape=[jax.ShapeDtypeStruct(dproj.shape, bf16)] + [jax.ShapeDtypeStruct((s, MIX), f32)] * 3
        + [jax.ShapeDtypeStruct(w_branch.shape, f32)],
        input_output_aliases={6: 0}, compiler_params=_params(1),
    )(out_a, out_b, out_c, proj, w_branch, d_merged, dproj)


def swiglu_fwd(gu, *, name):
    g2, s, w = gu.shape
    ng = g2 // 2
    ts = _tile(s, 1024, 16)

    def body(g_ref, u_ref, o_ref):
        g = g_ref[...].astype(f32)
        o_ref[...] = (g * _sigmoid(g) * u_ref[...].astype(f32)).astype(bf16)

    return pl.pallas_call(
        body, name=name, grid=(s // ts, ng),
        in_specs=[pl.BlockSpec((None, ts, w), lambda i, j: (j, i, 0)),
                  pl.BlockSpec((None, ts, w), lambda i, j: (ng + j, i, 0))],
        out_specs=pl.BlockSpec((None, ts, w), lambda i, j: (j, i, 0)), out_shape=jax.ShapeDtypeStruct((ng, s, w), bf16),
        compiler_params=_params(2),
    )(gu, gu)


def swiglu_bwd(gu, d_act, *, name):
    g2, s, w = gu.shape
    ng = g2 // 2
    ts = _tile(s, 1024, 16)

    def body(g_ref, u_ref, d_ref, dg_ref, du_ref):
        g, d = g_ref[...].astype(f32), d_ref[...].astype(f32)
        sig = _sigmoid(g)
        dg_ref[...] = (d * u_ref[...].astype(f32) * sig * (1.0 + g * (1.0 - sig))).astype(bf16)
        du_ref[...] = (d * g * sig).astype(bf16)

    lo = pl.BlockSpec((None, ts, w), lambda i, j: (j, i, 0))
    return pl.pallas_call(
        body, name=name, grid=(s // ts, ng),
        in_specs=[lo, pl.BlockSpec((None, ts, w), lambda i, j: (ng + j, i, 0)), lo], out_specs=[lo, lo],
        out_shape=[jax.ShapeDtypeStruct((ng, s, w), bf16)] * 2, compiler_params=_params(2),
    )(gu, gu, d_act)


def adamw(w, m, v, g_parts, *, name):
    n_layers = len(g_parts)
    n_parts, r, cols = g_parts[0].shape
    lanes = -(-cols // LANES) * LANES
    tr = _tile(r, max(16, (128 * 1024) // lanes), 16)
    nr = r // tr

    def body(w_ref, m_ref, v_ref, *rest):
        gp_refs, (g_ref, d_ref, nm_ref, nv_ref) = rest[:n_layers], rest[n_layers:]
        layer = pl.program_id(0)
        g = jnp.zeros((tr, cols), f32)
        for l, gp_ref in enumerate(gp_refs):
            g_l = gp_ref[0].astype(f32)
            for k in range(1, n_parts):
                g_l = g_l + gp_ref[k].astype(f32)
            g = jnp.where(layer == l, g_l, g)
        nm = ADAM_B1 * m_ref[...] + (1.0 - ADAM_B1) * g
        nv = ADAM_B2 * v_ref[...] + (1.0 - ADAM_B2) * jnp.square(g)
        m_hat = nm / (1.0 - ADAM_B1 ** ADAM_STEP)
        v_hat = nv / (1.0 - ADAM_B2 ** ADAM_STEP)
        g_ref[...] = g
        d_ref[...] = -ADAM_LR * (m_hat / (jnp.sqrt(v_hat) + ADAM_EPS) + ADAM_WD * w_ref[...])
        nm_ref[...] = nm
        nv_ref[...] = nv

    row = pl.BlockSpec((tr, cols), lambda l, i: (l * nr + i, 0))

    def parts_spec(own):
        return pl.BlockSpec((n_parts, tr, cols),
                            lambda l, i: (0, jnp.where(l == own, i, jnp.where(l < own, 0, nr - 1)), 0))

    return pl.pallas_call(
        body, name=name, grid=(n_layers, nr), in_specs=[row, row, row] + [parts_spec(l) for l in range(n_layers)],
        out_specs=[row] * 4, out_shape=[jax.ShapeDtypeStruct(w.shape, f32)] * 4, compiler_params=_params(2),
    )(w, m, v, *g_parts)


def _mesh_pos():
    return lax.axis_index("x"), lax.axis_index("y"), lax.axis_index("c")


def _dev_index(p):
    return 4 * p[0] + 2 * p[1] + p[2]


class Exchange:
    def __init__(self, kind, arrays):
        self.kind, self.arrays, self.n = kind, list(arrays), len(arrays)
        self.specs = [pl.BlockSpec(memory_space=pl.ANY)] * self.n
        self.out_shapes = [jax.ShapeDtypeStruct(((N_DEV,) if kind == "gather" else ()) + a.shape, a.dtype)
                           for a in self.arrays]
        self.scratch = [pltpu.SemaphoreType.DMA((self.n, N_DEV - 1)), pltpu.SemaphoreType.DMA((self.n, N_DEV - 1)),
                        pltpu.SemaphoreType.DMA((self.n,))]

    def _copies(self, in_refs, out_refs, sems, with_arrivals):
        send_sems, recv_sems, local_sems = sems
        x, y, c = _mesh_pos()
        mine = _dev_index((x, y, c))

        def src(a, slab):
            return in_refs[a] if self.kind == "gather" else in_refs[a].at[slab]

        local = [pltpu.make_async_copy(src(a, mine), out_refs[a].at[mine], local_sems.at[a]) for a in range(self.n)]
        sends, arrivals = [], []
        for k in range(1, N_DEV):
            peer = (1 - x if k & 4 else x, 1 - y if k & 2 else y, 1 - c if k & 1 else c)
            theirs = _dev_index(peer)
            for a in range(self.n):
                to = dict(send_sem=send_sems.at[a, k - 1], recv_sem=recv_sems.at[a, k - 1], device_id=peer,
                          device_id_type=pl.DeviceIdType.MESH)
                sends.append(pltpu.make_async_remote_copy(src_ref=src(a, theirs), dst_ref=out_refs[a].at[mine], **to))
                if with_arrivals:
                    arrivals.append(pltpu.make_async_remote_copy(src_ref=src(a, theirs),
                                                                 dst_ref=out_refs[a].at[theirs], **to))
        return local, sends, arrivals

    def start(self, in_refs, out_refs, sems):
        local, sends, _ = self._copies(in_refs, out_refs, sems, False)
        for cp in local + sends:
            cp.start()

    def wait(self, in_refs, out_refs, sems):
        local, sends, arrivals = self._copies(in_refs, out_refs, sems, True)
        for cp in arrivals:
            cp.wait_recv()
        for cp in sends:
            cp.wait_send()
        for cp in local:
            cp.wait()

    def run_around(self, grid, in_refs, out_refs, sems, *, before):
        at = None
        for axis, size in enumerate(grid):
            hit = pl.program_id(axis) == (0 if before else size - 1)
            at = hit if at is None else at & hit

        @pl.when(at)
        def _():
            (self.start if before else self.wait)(in_refs, out_refs, sems)


def host_call(side, body, *, name, grid, in_specs, out_specs, out_shape, scratch_shapes, args, aliases):
    n_in, n_out = len(in_specs), len(out_specs)
    if side is None:
        kernel_body = body
    else:
        n = side.n
        in_specs, args = in_specs + side.specs, tuple(args) + tuple(side.arrays)
        out_specs, out_shape = out_specs + side.specs, out_shape + side.out_shapes
        scratch_shapes = scratch_shapes + side.scratch

        def kernel_body(*refs):
            ins, side_in = refs[:n_in], refs[n_in:n_in + n]
            outs, side_out = refs[n_in + n:n_in + n + n_out], refs[n_in + n + n_out:n_in + 2 * n + n_out]
            scratch, sems = refs[n_in + 2 * n + n_out:-3], refs[-3:]
            side.run_around(grid, side_in, side_out, sems, before=True)
            body(*ins, *outs, *scratch)
            side.run_around(grid, side_in, side_out, sems, before=False)

    outs = pl.pallas_call(
        kernel_body, name=name, grid=grid, in_specs=in_specs, out_specs=out_specs, out_shape=out_shape,
        scratch_shapes=scratch_shapes, input_output_aliases=aliases, compiler_params=_params(len(grid)),
    )(*args)
    return outs if side is None else (outs[:n_out], outs[n_out:])


def all_gather(blocks, *, name):
    n = len(blocks)
    any_spec = pl.BlockSpec(memory_space=pl.ANY)

    def body(*refs):
        ins, outs = refs[:n], refs[n:2 * n]
        send_sems, recv_sems, local_sems = refs[2 * n:]
        x, y, c = _mesh_pos()
        me, sibling = (x, y, c), (x, y, 1 - c)
        chips = [(1 - x, y), (x, 1 - y), (1 - x, 1 - y)]

        def copy(a, k, block, to, src=None):
            dst = outs[a].at[_dev_index(block)]
            return pltpu.make_async_remote_copy(
                src_ref=dst if src is None else src, dst_ref=dst, send_sem=send_sems.at[a, k],
                recv_sem=recv_sems.at[a, k], device_id=to, device_id_type=pl.DeviceIdType.MESH)

        mine = [pltpu.make_async_copy(ins[a], outs[a].at[_dev_index(me)], local_sems.at[a]) for a in range(n)]
        for cp in mine:
            cp.start()
        first = []
        for a in range(n):
            first.append(copy(a, 0, me, sibling, src=ins[a]))
            first += [copy(a, 1 + j, me, (*chip, c), src=ins[a]) for j, chip in enumerate(chips)]
        for cp in first:
            cp.start()
        passed = []
        for j, chip in enumerate(chips):
            for a in range(n):
                copy(a, 1 + j, (*chip, c), me).wait_recv()
                fwd = copy(a, 4 + j, (*chip, c), sibling)
                fwd.start()
                passed.append(fwd)
        for a in range(n):
            copy(a, 0, sibling, me).wait_recv()
            for j, chip in enumerate(chips):
                copy(a, 4 + j, (*chip, 1 - c), me).wait_recv()
        for cp in first + passed:
            cp.wait_send()
        for cp in mine:
            cp.wait()

    return pl.pallas_call(
        body, name=name, in_specs=[any_spec] * n, out_specs=[any_spec] * n,
        out_shape=[jax.ShapeDtypeStruct((N_DEV,) + b.shape, b.dtype) for b in blocks],
        scratch_shapes=[pltpu.SemaphoreType.DMA((n, 7)), pltpu.SemaphoreType.DMA((n, 7)),
                        pltpu.SemaphoreType.DMA((n,))],
    )(*blocks)


def exchange(jobs, *, name):
    total = sum(j.n for j in jobs)

    def body(*refs):
        ins, outs, sems = refs[:total], refs[total:2 * total], refs[2 * total:]
        pieces, off = [], 0
        for i, j in enumerate(jobs):
            pieces.append((ins[off:off + j.n], outs[off:off + j.n], sems[3 * i:3 * i + 3]))
            off += j.n
        for j, piece in zip(jobs, pieces):
            j.start(*piece)
        for j, piece in zip(jobs, pieces):
            j.wait(*piece)

    outs = pl.pallas_call(
        body, name=name, in_specs=[s for j in jobs for s in j.specs], out_specs=[s for j in jobs for s in j.specs],
        out_shape=[s for j in jobs for s in j.out_shapes], scratch_shapes=[s for j in jobs for s in j.scratch],
    )(*[a for j in jobs for a in j.arrays])
    split, off = [], 0
    for j in jobs:
        split.append(outs[off:off + j.n])
        off += j.n
    return split


def _rows128(arr):
    flat = arr.reshape(-1)
    rows = -(-flat.shape[0] // (8 * LANES)) * 8
    return jnp.pad(flat, (0, rows * LANES - flat.shape[0])).reshape(rows, LANES)


def _pad_lanes(row, width=LANES, at=0):
    return jnp.pad(row, (at, width - at - row.shape[0])).reshape(1, width)


def _w_in_sections(got):
    d = got.shape[1]
    wi = jnp.transpose(got, (1, 0, 2)).reshape(d, -1)
    w_main = jnp.concatenate([wi[:, :C_B], wi[:, C_Z:C_BA], wi[:, C_QKV:C_Z], wi[:, C_G:], wi[:, C_B:C_QKV]], axis=1)
    return w_main, jnp.pad(wi[:, C_BA:C_G], ((0, 0), (0, LANES - (C_G - C_BA))))


def _w_in_parts(gw_main, gw_ba):
    d = gw_main.shape[0]
    full = jnp.concatenate([gw_main[:, SEC_A:SEC_Z], gw_main[:, SEC_B:], gw_main[:, SEC_QKV:SEC_G],
                            gw_main[:, SEC_Z:SEC_QKV], gw_ba[:, :C_G - C_BA], gw_main[:, SEC_G:SEC_B]], axis=1)
    return jnp.transpose(full.reshape(d, N_DEV, -1), (1, 0, 2))


def kernel(x, positions, attn_norm, w_in, sgu_ln_g, sgu_ln_b, sgu_w, sgu_b, attn_sinks, dn_conv_w, dn_a_log, dn_dt_bias, dn_norm, w_branch, w_out, ffn_norm, w_gate_up, w_down, final_norm, loss_target, m_attn_norm, m_w_in, m_sgu_ln_g, m_sgu_ln_b, m_sgu_w, m_sgu_b, m_attn_sinks, m_dn_conv_w, m_dn_a_log, m_dn_dt_bias, m_dn_norm, m_w_branch, m_w_out, m_ffn_norm, m_w_gate_up, m_w_down, m_final_norm, v_attn_norm, v_w_in, v_sgu_ln_g, v_sgu_ln_b, v_sgu_w, v_sgu_b, v_attn_sinks, v_dn_conv_w, v_dn_a_log, v_dn_dt_bias, v_dn_norm, v_w_branch, v_w_out, v_ffn_norm, v_w_gate_up, v_w_down, v_final_norm):
    given = dict(locals())
    depth, d_model = attn_norm.shape
    s = x.shape[1]
    x2 = x.reshape(s, d_model)
    target = loss_target.reshape(s, d_model)
    posf = positions.reshape(s, 1).astype(f32)
    inv_freq = ROPE_THETA ** (-jnp.arange(0, ROPE_DIM, 2, dtype=f32) / ROPE_DIM)
    inv_head = jnp.concatenate([inv_freq, inv_freq, jnp.zeros((SWA_HD - ROPE_DIM,), f32)])
    tables = rope_tables(posf, jnp.tile(inv_head, LANES // SWA_HD).reshape(1, LANES), name="rope_tables")

    assert depth == 2, depth
    gathered = dict(zip([("w_in", 0), ("dn_conv_w", 0), ("dn_conv_w", 1)], all_gather(
        [w_in[0].astype(bf16), dn_conv_w[0], dn_conv_w[1]], name="gather_first")))
    riders = {"l0_in_proj": [("w_branch", 0), ("w_out", 0)],
              "l0_deltanet": [("w_gate_up", 0), ("w_down", 0), ("w_in", 1)],
              "l1_deltanet": [("w_branch", 1), ("w_out", 1), ("w_gate_up", 1), ("w_down", 1)]}

    def gathering(host, call, *args, **kw):
        if host not in riders:
            return call(*args, name=host, **kw)
        side = Exchange("gather", [given[n][l].astype(bf16) for n, l in riders[host]])
        out, got = call(*args, side=side, name=host, **kw)
        gathered.update(zip(riders[host], got))
        return out

    layers, saved = [], []
    h_in = x2
    for l in range(depth):
        t = f"l{l}_"
        w_main, w_ba = _w_in_sections(gathered["w_in", l])
        conv_full = jnp.transpose(gathered["dn_conv_w", l], (1, 0, 2)).reshape(DN_CONV, -1)
        p = dict(
            w_main=w_main, w_ba=w_ba, conv_w8=jnp.pad(conv_full, ((0, CONV_PAD - DN_CONV), (0, 0))),
            attn_norm=attn_norm[l].reshape(1, -1), ffn_norm=ffn_norm[l].reshape(1, -1),
            ln_g=sgu_ln_g[l].reshape(1, -1), ln_b=sgu_ln_b[l].reshape(1, -1), sgu_w=sgu_w[l], sgu_bt=sgu_b[l].T,
            sink_row=_pad_lanes(attn_sinks[l]), alog_row=_pad_lanes(dn_a_log[l], at=DN_HEADS),
            dtb_row=_pad_lanes(dn_dt_bias[l], at=DN_HEADS), norm_row=dn_norm[l].reshape(1, -1))
        layers.append(p)
        h = rmsnorm_fwd(h_in, p["attn_norm"], name=t + "attn_norm")
        proj = gathering(t + "in_proj", matmul, h, p["w_main"], "nn", bf16)
        p_ba = matmul(h, p["w_ba"], "nn", f32, name=t + "in_proj_ba")
        out_a = sgu_fwd(proj, p["ln_g"], p["ln_b"], p["sgu_w"], p["sgu_bt"], name=t + "sgu")
        out_b = swa_fwd(proj, p["sink_row"], tables, name=t + "swa")
        xc = conv_fwd(proj, p["conv_w8"], name=t + "dn_conv")
        out_c, states, tri = gathering(t + "deltanet", dn_fwd, xc, proj, p_ba, p["alog_row"], p["dtb_row"], p["norm_row"])
        p.update(w_branch=jnp.transpose(gathered["w_branch", l], (1, 2, 0, 3)).reshape(3, MIX, d_model),
                 w_out=gathered["w_out", l].reshape(d_model, d_model),
                 w_gu=gathered["w_gate_up", l],
                 w_down=gathered["w_down", l].reshape(N_DEV // 2, -1, d_model))
        merged = merge_fwd(out_a, out_b, out_c, proj, p["w_branch"], name=t + "merge")
        x_mid = matmul(merged, p["w_out"], "nn", f32, residual=h_in, tn=1024, name=t + "out_proj")
        h2 = rmsnorm_fwd(x_mid, p["ffn_norm"], name=t + "ffn_norm")
        gu = matmul(h2, p["w_gu"], "nn", bf16, group="n", name=t + "gate_up")
        act = swiglu_fwd(gu, name=t + "swiglu")
        x_out = matmul(act, p["w_down"], "nn", f32, residual=x_mid, group="k", tn=1024, name=t + "down")
        saved.append(dict(x_in=h_in, h=h, proj=proj, p_ba=p_ba, out_a=out_a, out_b=out_b, out_c=out_c, xc=xc,
                          states=states, tri=tri, merged=merged, x_mid=x_mid, h2=h2, gu=gu, act=act))
        h_in = x_out

    dx, d_final_norm, loss_row = loss_head(h_in, final_norm.reshape(1, -1), target, name="loss_head")
    loss = lax.psum(loss_row[0, 0], MESH_AXES)

    shard_names = ["w_in", "dn_conv_w", "w_branch", "w_out", "w_gate_up", "w_down"]
    rep_names = ["attn_norm", "sgu_ln_g", "sgu_ln_b", "sgu_w", "sgu_b", "attn_sinks", "dn_a_log", "dn_dt_bias",
                 "dn_norm", "ffn_norm"]
    parts, received, per_layer = {}, {}, []
    senders = {"l0_b_swa": [("w_gate_up", 1), ("w_down", 1), ("w_out", 1), ("w_branch", 1)],
               "l0_b_deltanet": [("w_in", 1), ("w_gate_up", 0), ("w_down", 0), ("w_out", 0), ("w_branch", 0)],
               "l0_b_in_proj_dx": [("w_in", 0)]}

    def scattering(host, call, *args, **kw):
        if host not in senders:
            return call(*args, name=host, **kw)
        out, got = call(*args, side=Exchange("scatter", [parts[key] for key in senders[host]]), name=host, **kw)
        received.update(zip(senders[host], got))
        return out

    for l in reversed(range(depth)):
        p, sv, t = layers[l], saved[l], f"l{l}_b_"
        d_act = matmul(dx, p["w_down"], "nt", bf16, group="n", name=t + "down_dx")
        gw_down = matmul(sv["act"], dx, "tn", bf16, group="m", tk=2048, tn=512, name=t + "down_dw")
        d_gate, d_up = swiglu_bwd(sv["gu"], d_act, name=t + "swiglu")
        gw_gu = jnp.concatenate([matmul(sv["h2"], d_half, "tn", bf16, group="n", tk=2048, name=t + "gate_up_dw" + tag)
                                 for d_half, tag in ((d_gate, "_gate"), (d_up, "_up"))], axis=0)
        half = N_DEV // 2
        d_h2 = matmul(d_gate, p["w_gu"][:half], "nt", f32, group="k", tn=1024, name=t + "gate_up_dx_gate")
        d_h2 = matmul(d_up, p["w_gu"][half:], "nt", f32, group="k", tn=1024, residual=d_h2, name=t + "gate_up_dx_up")
        dx_mid, g_ffn = rmsnorm_bwd(sv["x_mid"], p["ffn_norm"], d_h2, dx, name=t + "ffn_norm")
        d_merged = matmul(dx_mid, p["w_out"], "nt", f32, tn=1024, name=t + "out_proj_dx")
        gw_out = matmul(sv["merged"], dx_mid, "tn", bf16, tk=2048, tn=1024, name=t + "out_proj_dw")
        dproj = lax.empty((s, W_MAIN), bf16)
        dproj, d_a, d_b, d_c, gw_branch = merge_bwd(sv["out_a"], sv["out_b"], sv["out_c"], sv["proj"], p["w_branch"],
                                                   d_merged, dproj, name=t + "merge")
        parts.update({("w_gate_up", l): gw_gu, ("w_down", l): gw_down.reshape(N_DEV, -1, d_model),
                      ("w_out", l): gw_out.reshape(N_DEV, -1, d_model),
                      ("w_branch", l): jnp.transpose(gw_branch.reshape(3, MIX, N_DEV, -1), (2, 0, 1, 3)).astype(bf16)})
        dproj, g_ln_g, g_ln_b, g_sgu_w, g_sgu_bt = sgu_bwd(sv["proj"], p["ln_g"], p["ln_b"], p["sgu_w"], p["sgu_bt"],
                                                         d_a, dproj, name=t + "sgu")
        dproj, g_sink = scattering(t + "swa", swa_bwd, sv["proj"], p["sink_row"], tables, d_b, dproj)
        dproj, dxc, dba, g_alog, g_dtb, g_dnorm = scattering(
            t + "deltanet", dn_bwd, sv["xc"], sv["proj"], sv["p_ba"], p["alog_row"], p["dtb_row"], p["norm_row"],
            sv["states"], sv["tri"], d_c, dproj)
        dproj, g_conv8 = conv_bwd(sv["proj"], p["conv_w8"], dxc, dproj, name=t + "dn_conv")
        gw_main = matmul(sv["h"], dproj, "tn", bf16, tk=2048, name=t + "in_proj_dw")
        gw_ba = matmul(sv["h"], dba, "tn", bf16, tk=2048, name=t + "in_proj_ba_dw")
        parts["w_in", l] = _w_in_parts(gw_main, gw_ba)
        d_h = scattering(t + "in_proj_dx", matmul, dproj, p["w_main"], "nt", f32, tk=2304, tn=1024)
        d_h = matmul(dba, p["w_ba"], "nt", f32, residual=d_h, name=t + "in_proj_ba_dx")
        dx, g_attn = rmsnorm_bwd(sv["x_in"], p["attn_norm"], d_h, dx_mid, name=t + "attn_norm")
        per_layer.append(dict(
            dn_conv_w=jnp.transpose(g_conv8[:DN_CONV].reshape(DN_CONV, N_DEV, -1), (1, 0, 2)),
            attn_norm=g_attn, sgu_ln_g=g_ln_g, sgu_ln_b=g_ln_b, sgu_w=g_sgu_w, sgu_b=g_sgu_bt.T,
            attn_sinks=g_sink[0, :SWA_HEADS], dn_a_log=g_alog[0, DN_HEADS:2 * DN_HEADS],
            dn_dt_bias=g_dtb[0, DN_HEADS:2 * DN_HEADS], dn_norm=g_dnorm, ffn_norm=g_ffn))
    per_layer.reverse()

    conv_parts = jnp.concatenate([pp["dn_conv_w"] for pp in per_layer], axis=1)
    rep_grads = {n: jnp.stack([pp[n].reshape(given[n].shape[1:]) for pp in per_layer]) for n in rep_names}
    rep_grads["final_norm"] = d_final_norm[0]
    rep_names = rep_names + ["final_norm"]
    rep_rows = [_rows128(given[n]).shape[0] for n in rep_names]
    pad_rows = -sum(rep_rows) % 16

    def small_rows(values):
        pieces = [_rows128(values[n]) for n in rep_names]
        return jnp.concatenate(pieces + [jnp.zeros((pad_rows, LANES), f32)], axis=0)

    (got_conv,), (small_all,) = exchange(
        [Exchange("scatter", [conv_parts]), Exchange("gather", [small_rows(rep_grads)])], name="exchange_last")

    results = [{}, {}, {}, {}]
    for n in shard_names:
        shp = given[n].shape
        two = (-1, shp[-1])
        by_layer = [got_conv] if n == "dn_conv_w" else [received[n, l].reshape(N_DEV, -1, shp[-1]) for l in range(depth)]
        outs = adamw(given[n].reshape(two), given["m_" + n].reshape(two), given["v_" + n].reshape(two), by_layer,
                     name="adamw_" + n)
        for res, val in zip(results, outs):
            res[n] = val.reshape(shp)
    outs = adamw(small_rows(given), small_rows({n: given["m_" + n] for n in rep_names}),
                 small_rows({n: given["v_" + n] for n in rep_names}), [small_all], name="adamw_replicated")
    for res, val in zip(results, outs):
        row = 0
        for n, nr in zip(rep_names, rep_rows):
            res[n] = val[row:row + nr].reshape(-1)[:given[n].size].reshape(given[n].shape)
            row += nr
    order = ["attn_norm", "w_in", "sgu_ln_g", "sgu_ln_b", "sgu_w", "sgu_b", "attn_sinks", "dn_conv_w", "dn_a_log",
             "dn_dt_bias", "dn_norm", "w_branch", "w_out", "ffn_norm", "w_gate_up", "w_down", "final_norm"]
    return (loss, dx.reshape(x.shape), *[res[n] for res in results for n in order])
```

```python
import functools

import jax
import jax.numpy as jnp
from jax import lax
from jax.experimental import pallas as pl
from jax.experimental.pallas import tpu as pltpu

f32 = jnp.float32
bf16 = jnp.bfloat16

N_DEV = 8
MESH_AXES = ("x", "y", "c")
NORM_EPS = 1e-6
MIX = 512
SGU_GROUPS, SGU_CHUNK = 4, 128
SWA_HEADS, SWA_KV, SWA_HD, WINDOW = 8, 2, 64, 128
ROPE_THETA, ROPE_DIM = 500000.0, 16
DN_HEADS, DN_HD, DN_CONV, DN_CHUNK = 4, 128, 4, 64
ADAM_LR, ADAM_B1, ADAM_B2, ADAM_EPS, ADAM_WD, ADAM_STEP = 0.001, 0.9, 0.999, 1e-08, 0.01, 10

LANES = 128
VMEM_LIMIT = 56 * 1024 * 1024

SEC_A, SEC_Z, SEC_QKV, SEC_G, SEC_B = 0, 1024, 1536, 3072, 6144
W_MAIN = 6912
C_B, C_QKV, C_Z, C_BA, C_G = 1024, 1792, 3328, 3840, 3848


def _params(n_axes, **kw):
    return pltpu.CompilerParams(dimension_semantics=("arbitrary",) * n_axes, vmem_limit_bytes=VMEM_LIMIT, **kw)


def _tile(n, target, mult=LANES):
    if n <= target:
        return n
    best = None
    for t in range(mult, target + 1, mult):
        if n % t == 0:
            best = t
    assert best is not None, (n, target, mult)
    return best


def _dg(a, b, ca, cb):
    return lax.dot_general(a.astype(bf16), b.astype(bf16), (((ca,), (cb,)), ((), ())), preferred_element_type=f32)


def _dg3(a, b, ca, cb):
    a_hi, b_hi = a.astype(bf16), b.astype(bf16)
    a_lo, b_lo = (a - a_hi.astype(f32)).astype(bf16), (b - b_hi.astype(f32)).astype(bf16)

    def dot(p, q):
        return lax.dot_general(p, q, (((ca,), (cb,)), ((), ())), preferred_element_type=f32)

    return dot(a_hi, b_hi) + (dot(a_hi, b_lo) + dot(a_lo, b_hi))


def _differentiable_dot(core):
    @functools.partial(jax.custom_vjp, nondiff_argnums=(2, 3))
    def dot(a, b, ca, cb):
        return core(a, b, ca, cb)

    def fwd(a, b, ca, cb):
        return core(a, b, ca, cb), (a, b)

    def bwd(ca, cb, res, ct):
        a, b = res
        da = core(ct, b, 1, 1 - cb) if ca == 1 else core(b, ct, 1 - cb, 1)
        db = core(a, ct, 1 - ca, 0) if cb == 0 else core(ct, a, 0, 1 - ca)
        return da, db

    dot.defvjp(fwd, bwd)
    return dot


bdot = _differentiable_dot(_dg)
_hdot = _differentiable_dot(_dg3)


def hdot(a, b, ca=1, cb=0):
    return _hdot(a, b, ca, cb)


@functools.partial(jax.custom_vjp, nondiff_argnums=(1,))
def lroll(x, shift):
    return pltpu.roll(x, shift, 1)


def _lroll_fwd(x, shift):
    return pltpu.roll(x, shift, 1), None


def _lroll_bwd(shift, _, ct):
    return (pltpu.roll(ct, ct.shape[1] - shift, 1),)


lroll.defvjp(_lroll_fwd, _lroll_bwd)


@functools.partial(jax.custom_vjp, nondiff_argnums=(1,))
def tri_inv(low, nil):
    n = low.shape[0]
    row = lax.broadcasted_iota(jnp.int32, (n, n), 0)
    col = lax.broadcasted_iota(jnp.int32, (n, n), 1)
    eye = (row == col).astype(f32)
    m = -low
    p = eye + m
    span = 2
    while span < nil:
        m = hdot(m, m)
        p = p + hdot(p, m)
        span *= 2
    return p


def _tri_inv_fwd(low, nil):
    t = tri_inv(low, nil)
    return t, t


def _tri_inv_bwd(nil, t, dt):
    return (-hdot(t, hdot(dt, t, 1, 1), 0, 0),)


tri_inv.defvjp(_tri_inv_fwd, _tri_inv_bwd)


@jax.custom_vjp
def tri_inv_known(low, t):
    return t


def _tri_inv_known_fwd(low, t):
    return t, t


def _tri_inv_known_bwd(t, dt):
    return _tri_inv_bwd(None, t, dt) + (jnp.zeros_like(t),)


tri_inv_known.defvjp(_tri_inv_known_fwd, _tri_inv_known_bwd)


def _sigmoid(x):
    return 1.0 / (1.0 + jnp.exp(-x))


def _rms(x, g):
    return x * lax.rsqrt(jnp.mean(x * x, axis=-1, keepdims=True) + NORM_EPS) * g


def _lane_col(x, lane_idx):
    lane = lax.broadcasted_iota(jnp.int32, x.shape, 1)
    return jnp.sum(jnp.where(lane == lane_idx, x, 0.0), axis=1, keepdims=True)


def matmul(a, b, mode, out_dtype, *, residual=None, group=None, side=None, tm=1024, tn=768, tk=1024, name):
    dims = {"a": ("m", "k") if mode != "tn" else ("k", "m"),
            "b": {"nn": ("k", "n"), "nt": ("n", "k"), "tn": ("k", "n")}[mode], "o": ("m", "n")}
    full, groups = {}, 1
    for arr, key in ((a, "a"), (b, "b")):
        grouped = group in dims[key]
        if grouped:
            groups = arr.shape[0]
        full[dims[key][0]], full[dims[key][1]] = arr.shape[1:] if grouped else arr.shape
    want = {"m": tm, "n": tn, "k": tk}
    per_step = min(groups, 4) if group == "k" else 1
    tiles = {d: full[d] if d == group else _tile(full[d], want[d]) for d in "mnk"}
    steps = {d: groups // per_step if d == group else full[d] // tiles[d] for d in "mnk"}

    def spec(key):
        d0, d1 = dims[key]

        def index(i, j, kk):
            at = {"m": i, "n": j, "k": kk}
            if group in (d0, d1):
                return (at[group], 0 if d0 == group else at[d0], 0 if d1 == group else at[d1])
            return (at[d0], at[d1])

        block = (tiles[d0], tiles[d1])
        if group in (d0, d1):
            block = ((per_step if group == "k" else None),) + block
        return pl.BlockSpec(block, index)

    ca, cb = {"nn": (1, 0), "nt": (1, 1), "tn": (0, 0)}[mode]
    nk = steps["k"]
    o_spec = spec("o")
    out_shape = (groups, full["m"], full["n"]) if group in ("m", "n") else (full["m"], full["n"])
    has_res = residual is not None

    def product(a_ref, b_ref):
        if group != "k":
            return _dg(a_ref[...], b_ref[...], ca, cb)
        total = _dg(a_ref[0], b_ref[0], ca, cb)
        for g in range(1, per_step):
            total = total + _dg(a_ref[g], b_ref[g], ca, cb)
        return total

    def body(a_ref, b_ref, *rest):
        r_ref = rest[0] if has_res else None
        o_ref = rest[1 if has_res else 0]

        def emit(acc):
            o_ref[...] = (acc + r_ref[...] if has_res else acc).astype(out_dtype)

        if nk == 1:
            emit(product(a_ref, b_ref))
            return
        acc_ref = rest[-1]
        kk = pl.program_id(2)

        @pl.when(kk == 0)
        def _():
            acc_ref[...] = jnp.zeros_like(acc_ref)

        acc_ref[...] += product(a_ref, b_ref)

        @pl.when(kk == nk - 1)
        def _():
            emit(acc_ref[...])

    res = host_call(
        side, body, name=name, grid=(steps["m"], steps["n"], nk),
        in_specs=[spec("a"), spec("b")] + ([o_spec] if has_res else []), out_specs=[o_spec],
        out_shape=[jax.ShapeDtypeStruct(out_shape, out_dtype)],
        scratch_shapes=[pltpu.VMEM((tiles["m"], tiles["n"]), f32)] if nk > 1 else [], aliases={},
        args=(a, b) + ((residual,) if has_res else ()))
    return res[0] if side is None else (res[0][0], res[1])


def rmsnorm_fwd(x, g_row, *, name):
    s, d = x.shape
    ts = _tile(s, 512, 16)

    def body(x_ref, g_ref, o_ref):
        o_ref[...] = _rms(x_ref[...], g_ref[...]).astype(bf16)

    return pl.pallas_call(
        body, name=name, grid=(s // ts,),
        in_specs=[pl.BlockSpec((ts, d), lambda i: (i, 0)), pl.BlockSpec((1, d), lambda i: (0, 0))],
        out_specs=pl.BlockSpec((ts, d), lambda i: (i, 0)), out_shape=jax.ShapeDtypeStruct((s, d), bf16),
        compiler_params=_params(1),
    )(x, g_row)


def rmsnorm_bwd(x, g_row, dh, dres, *, name):
    s, d = x.shape
    ts = _tile(s, 512, 16)

    def body(x_ref, g_ref, dh_ref, dres_ref, dx_ref, dg_ref):
        @pl.when(pl.program_id(0) == 0)
        def _():
            dg_ref[...] = jnp.zeros_like(dg_ref)

        _, vjp = jax.vjp(_rms, x_ref[...], g_ref[...])
        dx, dg = vjp(dh_ref[...])
        dx_ref[...] = dx + dres_ref[...]
        dg_ref[...] += dg

    row = pl.BlockSpec((ts, d), lambda i: (i, 0))
    vec = pl.BlockSpec((1, d), lambda i: (0, 0))
    return pl.pallas_call(
        body, name=name, grid=(s // ts,), in_specs=[row, vec, row, row], out_specs=[row, vec],
        out_shape=[jax.ShapeDtypeStruct((s, d), f32), jax.ShapeDtypeStruct((1, d), f32)],
        compiler_params=_params(1),
    )(x, g_row, dh, dres)


def loss_head(x, g_row, target, *, name):
    s, d = x.shape
    ts = _tile(s, 512, 16)

    def body(x_ref, g_ref, t_ref, dx_ref, dg_ref, loss_ref):
        @pl.when(pl.program_id(0) == 0)
        def _():
            dg_ref[...] = jnp.zeros_like(dg_ref)
            loss_ref[...] = jnp.zeros_like(loss_ref)

        y, vjp = jax.vjp(_rms, x_ref[...], g_ref[...])
        err = y - t_ref[...]
        dx, dg = vjp(err * (1.0 / d))
        dx_ref[...] = dx
        dg_ref[...] += dg
        loss_ref[...] += 0.5 * jnp.sum(jnp.sum(err * err, axis=1, keepdims=True) * (1.0 / d), axis=0, keepdims=True)

    row = pl.BlockSpec((ts, d), lambda i: (i, 0))
    vec = pl.BlockSpec((1, d), lambda i: (0, 0))
    one = pl.BlockSpec((1, LANES), lambda i: (0, 0))
    return pl.pallas_call(
        body, name=name, grid=(s // ts,), in_specs=[row, vec, row], out_specs=[row, vec, one],
        out_shape=[jax.ShapeDtypeStruct((s, d), f32), jax.ShapeDtypeStruct((1, d), f32),
                   jax.ShapeDtypeStruct((1, LANES), f32)],
        compiler_params=_params(1),
    )(x, g_row, target)


def _sgu_chunk(p_a, ln_g, ln_b, w, b_t):
    t = SGU_CHUNK
    u = jax.nn.gelu(p_a[:, :MIX])
    v = jax.nn.gelu(p_a[:, MIX:])
    vc = v - jnp.mean(v, axis=-1, keepdims=True)
    vn = vc * lax.rsqrt(jnp.mean(vc * vc, axis=-1, keepdims=True) + NORM_EPS) * ln_g + ln_b
    causal = lax.broadcasted_iota(jnp.int32, (t, t), 0) >= lax.broadcasted_iota(jnp.int32, (t, t), 1)
    outs = []
    for g in range(SGU_GROUPS):
        sl = slice(g * LANES, (g + 1) * LANES)
        mixed = bdot(jnp.where(causal, w[g], 0.0), vn[:, sl], 1, 0) + b_t[:, g:g + 1]
        outs.append(u[:, sl] * mixed)
    return jnp.concatenate(outs, axis=1)


def _sgu_specs(s, ts):
    return [pl.BlockSpec((ts, 2 * MIX), lambda i: (i, SEC_A // (2 * MIX))),
            pl.BlockSpec((1, MIX), lambda i: (0, 0)), pl.BlockSpec((1, MIX), lambda i: (0, 0)),
            pl.BlockSpec((SGU_GROUPS, SGU_CHUNK, SGU_CHUNK), lambda i: (0, 0, 0)),
            pl.BlockSpec((SGU_CHUNK, SGU_GROUPS), lambda i: (0, 0))]


def sgu_fwd(proj, ln_g, ln_b, w, b_t, *, name):
    s = proj.shape[0]
    ts = _tile(s, 512)
    n_chunk = ts // SGU_CHUNK

    def body(p_ref, g_ref, b_ref, w_ref, bt_ref, o_ref):
        def step(c, carry):
            rows = pl.ds(pl.multiple_of(c * SGU_CHUNK, SGU_CHUNK), SGU_CHUNK)
            o_ref[rows, :] = _sgu_chunk(p_ref[rows, :].astype(f32), g_ref[...], b_ref[...], w_ref[...], bt_ref[...]).astype(bf16)
            return carry
        lax.fori_loop(0, n_chunk, step, 0)

    return pl.pallas_call(
        body, name=name, grid=(s // ts,), in_specs=_sgu_specs(s, ts),
        out_specs=pl.BlockSpec((ts, MIX), lambda i: (i, 0)), out_shape=jax.ShapeDtypeStruct((s, MIX), bf16),
        compiler_params=_params(1),
    )(proj, ln_g, ln_b, w, b_t)


def sgu_bwd(proj, ln_g, ln_b, w, b_t, d_out, dproj, *, name):
    s = proj.shape[0]
    ts = _tile(s, 512)
    n_chunk = ts // SGU_CHUNK

    def body(p_ref, g_ref, b_ref, w_ref, bt_ref, do_ref, _, dp_ref, dg_ref, db_ref, dw_ref, dbt_ref):
        @pl.when(pl.program_id(0) == 0)
        def _():
            dg_ref[...] = jnp.zeros_like(dg_ref)
            db_ref[...] = jnp.zeros_like(db_ref)
            dw_ref[...] = jnp.zeros_like(dw_ref)
            dbt_ref[...] = jnp.zeros_like(dbt_ref)

        def step(c, carry):
            rows = pl.ds(pl.multiple_of(c * SGU_CHUNK, SGU_CHUNK), SGU_CHUNK)
            _, vjp = jax.vjp(_sgu_chunk, p_ref[rows, :].astype(f32), g_ref[...], b_ref[...], w_ref[...], bt_ref[...])
            dp, dg, db, dw, dbt = vjp(do_ref[rows, :])
            dp_ref[rows, :] = dp.astype(bf16)
            dg_ref[...] += dg
            db_ref[...] += db
            dw_ref[...] += dw
            dbt_ref[...] += dbt
            return carry
        lax.fori_loop(0, n_chunk, step, 0)

    specs = _sgu_specs(s, ts)
    return pl.pallas_call(
        body, name=name, grid=(s // ts,),
        in_specs=specs + [pl.BlockSpec((ts, MIX), lambda i: (i, 0)), pl.BlockSpec(memory_space=pl.ANY)],
        out_specs=[specs[0], specs[1], specs[2], specs[3], specs[4]],
        out_shape=[jax.ShapeDtypeStruct(dproj.shape, bf16), jax.ShapeDtypeStruct((1, MIX), f32),
                   jax.ShapeDtypeStruct((1, MIX), f32), jax.ShapeDtypeStruct(w.shape, f32),
                   jax.ShapeDtypeStruct(b_t.shape, f32)],
        input_output_aliases={6: 0}, compiler_params=_params(1),
    )(proj, ln_g, ln_b, w, b_t, d_out, dproj)


def rope_tables(posf, inv_freq, *, name):
    s = posf.shape[0]
    ts = _tile(s, 1024, 8)
    half = ROPE_DIM // 2

    def body(pos_ref, inv_ref, o_ref):
        d = lax.broadcasted_iota(jnp.int32, (1, LANES), 1) % SWA_HD
        ang = pos_ref[...] * inv_ref[...]
        sin = jnp.sin(ang)
        o_ref[0] = jnp.cos(ang)
        o_ref[1] = jnp.where(d < half, sin, 0.0)
        o_ref[2] = jnp.where((d >= half) & (d < ROPE_DIM), sin, 0.0)

    return pl.pallas_call(
        body, name=name, grid=(s // ts,),
        in_specs=[pl.BlockSpec((ts, 1), lambda i: (i, 0)), pl.BlockSpec((1, LANES), lambda i: (0, 0))],
        out_specs=pl.BlockSpec((3, ts, LANES), lambda i: (0, i, 0)), out_shape=jax.ShapeDtypeStruct((3, s, LANES), f32),
        compiler_params=_params(1),
    )(posf, inv_freq)


def _rope(x, table):
    w = x.shape[1]
    half = ROPE_DIM // 2
    c, lo, hi = (jnp.concatenate([table[i]] * (w // LANES), axis=1) for i in range(3))
    return x * c - lroll(x, w - half) * lo + lroll(x, half) * hi


def _swa_block(q, kp, kc, vp, vc, sink_row, table_q, table_p, prev_ok, wide):
    t = WINDOW
    q = _rope(q, table_q) * (SWA_HD ** -0.5)
    keys = jnp.concatenate([_rope(kp, table_p), _rope(kc, table_q)], axis=0)
    vals = jnp.concatenate([vp, vc], axis=0)
    own = lax.broadcasted_iota(jnp.int32, (t, t), 0) >= lax.broadcasted_iota(jnp.int32, (t, t), 1)
    lane_half = lax.broadcasted_iota(jnp.int32, (t, LANES), 1) // SWA_HD
    group = SWA_HEADS // SWA_KV
    slabs = []
    for pair in range(SWA_HEADS // 2):
        q_pair = q[:, pair * LANES:(pair + 1) * LANES]
        acc = jnp.zeros((t, LANES), f32)
        for half in range(2):
            h = 2 * pair + half
            kv = h // group
            qm = jnp.where(lane_half == half, q_pair, 0.0)
            if half != kv:
                qm = lroll(qm, SWA_HD)
            if wide:
                both = bdot(qm, keys, 1, 1)
                s_prev, s_own = both[:, :t], both[:, t:]
            else:
                s_prev, s_own = bdot(qm, keys[:t], 1, 1), bdot(qm, keys[t:], 1, 1)
            logits = jnp.where(own, s_own, jnp.where(prev_ok, s_prev, -1e30))
            sink = _lane_col(sink_row, h)
            m = lax.stop_gradient(jnp.maximum(jnp.max(logits, axis=1, keepdims=True), sink))
            p = jnp.exp(logits - m)
            probs = p * (1.0 / (jnp.sum(p, axis=1, keepdims=True) + jnp.exp(sink - m)))
            p_prev, p_own = jnp.where(own, 0.0, probs), jnp.where(own, probs, 0.0)
            if wide:
                o = bdot(jnp.concatenate([p_prev, p_own], axis=1), vals, 1, 0)
            else:
                o = bdot(p_prev, vals[:t], 1, 0) + bdot(p_own, vals[t:], 1, 0)
            o = jnp.where(lane_half == kv, o, 0.0)
            if half != kv:
                o = lroll(o, SWA_HD)
            acc = acc + o
        slabs.append(acc)
    return jnp.concatenate(slabs, axis=1)


def _swa_in_specs(nc, clamp):
    t = WINDOW
    qb, kb, vb = SEC_B // MIX, (SEC_B + MIX) // LANES, (SEC_B + MIX + LANES) // LANES

    def cur(i):
        return jnp.minimum(i, nc - 1) if clamp else i

    def prev(i):
        return jnp.maximum(cur(i) - 1, 0)

    return [pl.BlockSpec((t, MIX), lambda i: (cur(i), qb)),
            pl.BlockSpec((t, LANES), lambda i: (prev(i), kb)), pl.BlockSpec((t, LANES), lambda i: (cur(i), kb)),
            pl.BlockSpec((t, LANES), lambda i: (prev(i), vb)), pl.BlockSpec((t, LANES), lambda i: (cur(i), vb)),
            pl.BlockSpec((1, LANES), lambda i: (0, 0)),
            pl.BlockSpec((3, t, LANES), lambda i: (0, cur(i), 0)), pl.BlockSpec((3, t, LANES), lambda i: (0, prev(i), 0))]


def swa_fwd(proj, sink_row, tables, *, name):
    s = proj.shape[0]
    nc = s // WINDOW

    def body(q_ref, kp_ref, kc_ref, vp_ref, vc_ref, sink_ref, tq_ref, tp_ref, o_ref):
        prev_ok = pl.program_id(0) > 0
        blocks = [r[...].astype(f32) for r in (q_ref, kp_ref, kc_ref, vp_ref, vc_ref)]
        o_ref[...] = _swa_block(*blocks, sink_ref[...],
                                tq_ref[...], tp_ref[...], prev_ok, True).astype(bf16)

    return pl.pallas_call(
        body, name=name, grid=(nc,), in_specs=_swa_in_specs(nc, False),
        out_specs=pl.BlockSpec((WINDOW, MIX), lambda i: (i, 0)), out_shape=jax.ShapeDtypeStruct((s, MIX), bf16),
        compiler_params=_params(1),
    )(proj, proj, proj, proj, proj, sink_row, tables, tables)


def swa_bwd(proj, sink_row, tables, d_out, dproj, *, side=None, name):
    s = proj.shape[0]
    nc = s // WINDOW
    t = WINDOW

    def body(q_ref, kp_ref, kc_ref, vp_ref, vc_ref, sink_ref, tq_ref, tp_ref, do_ref, _,
             dp_ref, dsink_ref, cq_ref, ck_ref, cv_ref):
        i = pl.program_id(0)

        @pl.when(i == 0)
        def _():
            dsink_ref[...] = jnp.zeros_like(dsink_ref)

        def write(dk_prev, dv_prev):
            dp_ref[:, :MIX] = cq_ref[...].astype(bf16)
            dp_ref[:, MIX:MIX + LANES] = (ck_ref[...] + dk_prev).astype(bf16)
            dp_ref[:, MIX + LANES:] = (cv_ref[...] + dv_prev).astype(bf16)

        @pl.when(i < nc)
        def _():
            fn = functools.partial(_swa_block, table_q=tq_ref[...], table_p=tp_ref[...], prev_ok=i > 0, wide=False)
            blocks = [r[...].astype(f32) for r in (q_ref, kp_ref, kc_ref, vp_ref, vc_ref)]
            _, vjp = jax.vjp(fn, *blocks, sink_ref[...])
            dq, dkp, dkc, dvp, dvc, dsink = vjp(do_ref[...])
            dsink_ref[...] += dsink

            @pl.when(i > 0)
            def _():
                write(dkp, dvp)

            cq_ref[...] = dq
            ck_ref[...] = dkc
            cv_ref[...] = dvc

        @pl.when(i == nc)
        def _():
            write(0.0, 0.0)

    return host_call(
        side, body, name=name, grid=(nc + 1,),
        in_specs=_swa_in_specs(nc, True) + [pl.BlockSpec((t, MIX), lambda i: (jnp.minimum(i, nc - 1), 0)),
                                            pl.BlockSpec(memory_space=pl.ANY)],
        out_specs=[pl.BlockSpec((t, MIX + 2 * LANES), lambda i: (jnp.maximum(i - 1, 0), SEC_B // (MIX + 2 * LANES))),
                   pl.BlockSpec((1, LANES), lambda i: (0, 0))],
        out_shape=[jax.ShapeDtypeStruct(dproj.shape, bf16), jax.ShapeDtypeStruct((1, LANES), f32)],
        scratch_shapes=[pltpu.VMEM((t, MIX), f32), pltpu.VMEM((t, LANES), f32), pltpu.VMEM((t, LANES), f32)],
        aliases={9: 0}, args=(proj, proj, proj, proj, proj, sink_row, tables, tables, d_out, dproj))


CONV_PAD = 16


def _conv_taps(xp, rows):
    off = CONV_PAD - (DN_CONV - 1)
    return [xp[off + i:off + i + rows] for i in range(DN_CONV)]


def _conv_pre(taps, w):
    pre = taps[0] * w[0:1]
    for i in range(1, DN_CONV):
        pre = pre + taps[i] * w[i:i + 1]
    return pre


def conv_fwd(proj, conv_w8, *, name):
    s = proj.shape[0]
    wq = 3 * MIX
    ts = _tile(s, 512)
    nb = ts // CONV_PAD

    def body(x_ref, prev_ref, w_ref, o_ref):
        prev = jnp.where(pl.program_id(0) > 0, prev_ref[...].astype(f32), 0.0)
        pre = _conv_pre(_conv_taps(jnp.concatenate([prev, x_ref[...].astype(f32)], axis=0), ts), w_ref[...])
        o_ref[...] = pre * _sigmoid(pre)

    return pl.pallas_call(
        body, name=name, grid=(s // ts,),
        in_specs=[pl.BlockSpec((ts, wq), lambda i: (i, SEC_QKV // wq)),
                  pl.BlockSpec((CONV_PAD, wq), lambda i: (jnp.maximum(i * nb - 1, 0), SEC_QKV // wq)),
                  pl.BlockSpec((CONV_PAD, wq), lambda i: (0, 0))],
        out_specs=pl.BlockSpec((ts, wq), lambda i: (i, 0)), out_shape=jax.ShapeDtypeStruct((s, wq), f32),
        compiler_params=_params(1),
    )(proj, proj, conv_w8)


def conv_bwd(proj, conv_w8, dxc, dproj, *, name):
    s = proj.shape[0]
    wq = 3 * MIX
    ts = _tile(s, 512)
    nb = ts // CONV_PAD
    nt = s // ts
    last_blk = s // CONV_PAD - 1

    def body(x_ref, prev_ref, next_ref, w_ref, d_ref, dnext_ref, _, dp_ref, dw_ref):
        i = pl.program_id(0)

        @pl.when(i == 0)
        def _():
            dw_ref[...] = jnp.zeros_like(dw_ref)

        w = w_ref[...]
        prev = jnp.where(i > 0, prev_ref[...].astype(f32), 0.0)
        more = i < nt - 1
        xp = jnp.concatenate([prev, x_ref[...].astype(f32), jnp.where(more, next_ref[...].astype(f32), 0.0)], axis=0)
        taps = _conv_taps(xp, ts + CONV_PAD)
        pre = _conv_pre(taps, w)
        sig = _sigmoid(pre)
        dxc_ext = jnp.concatenate([d_ref[...], jnp.where(more, dnext_ref[...], 0.0)], axis=0)
        dpre = dxc_ext * sig * (1.0 + pre * (1.0 - sig))
        d_raw = jnp.zeros((ts, wq), f32)
        dws = []
        for k in range(DN_CONV):
            shift = DN_CONV - 1 - k
            d_raw = d_raw + dpre[shift:shift + ts] * w[k:k + 1]
            dws.append(jnp.sum(dpre[:ts] * taps[k][:ts], axis=0, keepdims=True))
        dp_ref[...] = d_raw.astype(bf16)
        dw_ref[...] += jnp.concatenate(dws + [jnp.zeros((CONV_PAD - DN_CONV, wq), f32)], axis=0)

    sec = SEC_QKV // wq
    return pl.pallas_call(
        body, name=name, grid=(nt,),
        in_specs=[pl.BlockSpec((ts, wq), lambda i: (i, sec)),
                  pl.BlockSpec((CONV_PAD, wq), lambda i: (jnp.maximum(i * nb - 1, 0), sec)),
                  pl.BlockSpec((CONV_PAD, wq), lambda i: (jnp.minimum((i + 1) * nb, last_blk), sec)),
                  pl.BlockSpec((CONV_PAD, wq), lambda i: (0, 0)),
                  pl.BlockSpec((ts, wq), lambda i: (i, 0)),
                  pl.BlockSpec((CONV_PAD, wq), lambda i: (jnp.minimum((i + 1) * nb, last_blk), 0)),
                  pl.BlockSpec(memory_space=pl.ANY)],
        out_specs=[pl.BlockSpec((ts, wq), lambda i: (i, sec)), pl.BlockSpec((CONV_PAD, wq), lambda i: (0, 0))],
        out_shape=[jax.ShapeDtypeStruct(dproj.shape, bf16), jax.ShapeDtypeStruct((CONV_PAD, wq), f32)],
        input_output_aliases={6: 0}, compiler_params=_params(1),
    )(proj, proj, proj, conv_w8, dxc, dxc, dproj)


def _dn_chunk(state, xc, z, ba, alog_row, dtb_row, norm_row, t_known):
    c, nh = DN_CHUNK, DN_HEADS
    n = c * nh
    row = lax.broadcasted_iota(jnp.int32, (n, n), 0)
    col = lax.broadcasted_iota(jnp.int32, (n, n), 1)
    same_head = (row // c) == (col // c)
    tril, strict = same_head & (row >= col), same_head & (row > col)
    tril_c = lax.broadcasted_iota(jnp.int32, (c, c), 0) >= lax.broadcasted_iota(jnp.int32, (c, c), 1)
    beta_all = _sigmoid(ba)
    g_all = -jnp.exp(alog_row) * jax.nn.softplus(ba + dtb_row)
    gc_all = hdot(tril_c.astype(f32), g_all)
    gc_t = gc_all.T

    def stack(piece):
        return jnp.concatenate([piece(h) for h in range(nh)], axis=0)

    q = stack(lambda h: xc[:, h * DN_HD:(h + 1) * DN_HD])
    k = stack(lambda h: xc[:, MIX + h * DN_HD:MIX + (h + 1) * DN_HD])
    v = stack(lambda h: xc[:, 2 * MIX + h * DN_HD:2 * MIX + (h + 1) * DN_HD])
    zs = stack(lambda h: z[:, h * DN_HD:(h + 1) * DN_HD])
    q = q * lax.rsqrt(jnp.sum(q * q, axis=-1, keepdims=True) + NORM_EPS) * (DN_HD ** -0.5)
    k = k * lax.rsqrt(jnp.sum(k * k, axis=-1, keepdims=True) + NORM_EPS)
    beta = stack(lambda h: _lane_col(beta_all, h))
    g_cols = [_lane_col(gc_all, nh + h) for h in range(nh)]
    g_col = jnp.concatenate(g_cols, axis=0)
    g_row = jnp.concatenate([gc_t[nh + h:nh + h + 1, :] for h in range(nh)], axis=1)
    g_last = stack(lambda h: jnp.broadcast_to(g_cols[h][c - 1:c, :], (c, 1)))
    decay = jnp.where(tril, jnp.exp(jnp.where(tril, g_col - g_row, 0.0)), 0.0)
    kb = k * beta
    low = jnp.where(strict, bdot(kb, k, 1, 1) * decay, 0.0)
    t_inv = tri_inv(low, c) if t_known is None else tri_inv_known(low, t_known)
    e_gc = jnp.exp(g_col)
    uw = bdot(t_inv, jnp.concatenate([v * beta, kb * e_gc], axis=1), 1, 0)
    u, w = uw[:, :DN_HD], uw[:, DN_HD:]
    attn = bdot(q, k, 1, 1) * decay
    own = (lax.broadcasted_iota(jnp.int32, (n, nh * DN_HD), 1) // DN_HD
           == lax.broadcasted_iota(jnp.int32, (n, nh * DN_HD), 0) // c)

    def spread(a):
        return jnp.where(own, jnp.concatenate([a] * nh, axis=1), 0.0)

    v_new = u - bdot(spread(w), state, 1, 0)
    o = bdot(spread(q * e_gc), state, 1, 0) + bdot(attn, v_new, 1, 0)
    keep = stack(lambda h: jnp.broadcast_to(jnp.exp(g_cols[h][c - 1:c, :]), (DN_HD, 1)))
    new_state = state * keep + bdot(spread(k * jnp.exp(g_last - g_col)), v_new, 0, 0)
    out = _rms(o, norm_row) * (zs * _sigmoid(zs))
    return new_state, jnp.concatenate([out[h * c:(h + 1) * c] for h in range(nh)], axis=1), t_inv


DN_STEP = 4 * DN_CHUNK


def _dn_step(state, xc, z, ba, alog_row, dtb_row, norm_row, t_known=None):
    outs, t_invs = [], []
    for c in range(DN_STEP // DN_CHUNK):
        rows = slice(c * DN_CHUNK, (c + 1) * DN_CHUNK)
        state, out, t_inv = _dn_chunk(state, xc[rows], z[rows], ba[rows], alog_row, dtb_row, norm_row,
                                      None if t_known is None else t_known[c])
        outs.append(out)
        t_invs.append(t_inv)
    return state, jnp.concatenate(outs, axis=0), jnp.stack(t_invs)


def _dn_specs(ts, order):
    zb = SEC_Z // MIX
    return [pl.BlockSpec((ts, 3 * MIX), lambda i: (order(i), 0)),
            pl.BlockSpec((ts, MIX), lambda i: (order(i), zb)),
            pl.BlockSpec((ts, LANES), lambda i: (order(i), 0)),
            pl.BlockSpec((1, LANES), lambda i: (0, 0)), pl.BlockSpec((1, LANES), lambda i: (0, 0)),
            pl.BlockSpec((1, LANES), lambda i: (0, 0))]


def dn_fwd(xc, proj, p_ba, alog_row, dtb_row, norm_row, *, side=None, name):
    s = xc.shape[0]
    ts = _tile(s, 512)
    n_step = ts // DN_STEP

    per_step = DN_STEP // DN_CHUNK
    n_tri = DN_HEADS * DN_CHUNK

    def body(xc_ref, z_ref, ba_ref, al_ref, dt_ref, nr_ref, o_ref, st_ref, tri_ref, state_ref):
        @pl.when(pl.program_id(0) == 0)
        def _():
            state_ref[...] = jnp.zeros_like(state_ref)

        def step(c, carry):
            rows = pl.ds(pl.multiple_of(c * DN_STEP, DN_STEP), DN_STEP)
            st_ref[c] = state_ref[...]
            new_state, out, t_invs = _dn_step(state_ref[...], xc_ref[rows, :], z_ref[rows, :].astype(f32), ba_ref[rows, :],
                                              al_ref[...], dt_ref[...], nr_ref[...])
            state_ref[...] = new_state
            o_ref[rows, :] = out.astype(bf16)
            tri_ref[pl.ds(c * per_step, per_step)] = t_invs
            return carry
        lax.fori_loop(0, n_step, step, 0)

    return host_call(
        side, body, name=name, grid=(s // ts,), in_specs=_dn_specs(ts, lambda i: i),
        out_specs=[pl.BlockSpec((ts, MIX), lambda i: (i, 0)),
                   pl.BlockSpec((n_step, DN_HEADS * DN_HD, DN_HD), lambda i: (i, 0, 0)),
                   pl.BlockSpec((n_step * per_step, n_tri, n_tri), lambda i: (i, 0, 0))],
        out_shape=[jax.ShapeDtypeStruct((s, MIX), bf16),
                   jax.ShapeDtypeStruct((s // DN_STEP, DN_HEADS * DN_HD, DN_HD), f32),
                   jax.ShapeDtypeStruct((s // DN_CHUNK, n_tri, n_tri), f32)],
        scratch_shapes=[pltpu.VMEM((DN_HEADS * DN_HD, DN_HD), f32)], aliases={},
        args=(xc, proj, p_ba, alog_row, dtb_row, norm_row))


def dn_bwd(xc, proj, p_ba, alog_row, dtb_row, norm_row, saved, tri, d_out, dproj, *, side=None, name):
    s = xc.shape[0]
    ts = _tile(s, 512)
    n_step = ts // DN_STEP
    nt = s // ts

    per_step = DN_STEP // DN_CHUNK
    n_tri = DN_HEADS * DN_CHUNK

    def body(xc_ref, z_ref, ba_ref, al_ref, dt_ref, nr_ref, st_ref, tri_ref, do_ref, _,
             dz_ref, dxc_ref, dba_ref, dal_ref, ddt_ref, dnr_ref, dstate_ref):
        @pl.when(pl.program_id(0) == 0)
        def _():
            dstate_ref[...] = jnp.zeros_like(dstate_ref)
            dal_ref[...] = jnp.zeros_like(dal_ref)
            ddt_ref[...] = jnp.zeros_like(ddt_ref)
            dnr_ref[...] = jnp.zeros_like(dnr_ref)

        def step(it, carry):
            c = n_step - 1 - it
            rows = pl.ds(pl.multiple_of(c * DN_STEP, DN_STEP), DN_STEP)
            t_known = tri_ref[pl.ds(c * per_step, per_step)]
            _, vjp = jax.vjp(lambda *a: _dn_step(*a, t_known=t_known)[:2], st_ref[c], xc_ref[rows, :],
                             z_ref[rows, :].astype(f32), ba_ref[rows, :], al_ref[...], dt_ref[...], nr_ref[...])
            d_in, dxc, dz, dba, dal, ddt, dnr = vjp((dstate_ref[...], do_ref[rows, :]))
            dstate_ref[...] = d_in
            dxc_ref[rows, :] = dxc
            dz_ref[rows, :] = dz.astype(bf16)
            dba_ref[rows, :] = dba.astype(bf16)
            dal_ref[...] += dal
            ddt_ref[...] += ddt
            dnr_ref[...] += dnr
            return carry
        lax.fori_loop(0, n_step, step, 0)

    def rev(i):
        return nt - 1 - i

    specs = _dn_specs(ts, rev)
    vec = pl.BlockSpec((1, LANES), lambda i: (0, 0))
    return host_call(
        side, body, name=name, grid=(nt,),
        in_specs=specs + [pl.BlockSpec((n_step, DN_HEADS * DN_HD, DN_HD), lambda i: (rev(i), 0, 0)),
                          pl.BlockSpec((n_step * per_step, n_tri, n_tri), lambda i: (rev(i), 0, 0)),
                          pl.BlockSpec((ts, MIX), lambda i: (rev(i), 0)), pl.BlockSpec(memory_space=pl.ANY)],
        out_specs=[specs[1], specs[0], specs[2], vec, vec, vec],
        out_shape=[jax.ShapeDtypeStruct(dproj.shape, bf16), jax.ShapeDtypeStruct((s, 3 * MIX), f32),
                   jax.ShapeDtypeStruct((s, LANES), bf16)] + [jax.ShapeDtypeStruct((1, LANES), f32)] * 3,
        scratch_shapes=[pltpu.VMEM((DN_HEADS * DN_HD, DN_HD), f32)], aliases={9: 0},
        args=(xc, proj, p_ba, alog_row, dtb_row, norm_row, saved, tri, d_out, dproj))


def _merge_in_specs(ts, d):
    row = pl.BlockSpec((ts, MIX), lambda i: (i, 0))
    return [row, row, row, pl.BlockSpec((ts, 3 * d), lambda i: (i, SEC_G // (3 * d))),
            pl.BlockSpec((3, MIX, d), lambda i: (0, 0, 0))]


def merge_fwd(out_a, out_b, out_c, proj, w_branch, *, name):
    s, d = out_a.shape[0], w_branch.shape[2]
    ts = _tile(s, 512, 16)

    def body(a_ref, b_ref, c_ref, g_ref, w_ref, o_ref):
        acc = jnp.zeros((ts, d), f32)
        for n, r in enumerate((a_ref, b_ref, c_ref)):
            acc = acc + _sigmoid(g_ref[:, n * d:(n + 1) * d].astype(f32)) * _dg(r[...], w_ref[n], 1, 0)
        o_ref[...] = acc.astype(bf16)

    return pl.pallas_call(
        body, name=name, grid=(s // ts,), in_specs=_merge_in_specs(ts, d),
        out_specs=pl.BlockSpec((ts, d), lambda i: (i, 0)), out_shape=jax.ShapeDtypeStruct((s, d), bf16),
        compiler_params=_params(1),
    )(out_a, out_b, out_c, proj, w_branch)


def merge_bwd(out_a, out_b, out_c, proj, w_branch, d_merged, dproj, *, name):
    s, d = out_a.shape[0], w_branch.shape[2]
    ts = _tile(s, 512, 16)

    def body(a_ref, b_ref, c_ref, g_ref, w_ref, dm_ref, _, dg_ref, da_ref, db_ref, dc_ref, dw_ref):
        @pl.when(pl.program_id(0) == 0)
        def _():
            dw_ref[...] = jnp.zeros_like(dw_ref)

        dm = dm_ref[...]
        for n, (r, dr) in enumerate(((a_ref, da_ref), (b_ref, db_ref), (c_ref, dc_ref))):
            gate = _sigmoid(g_ref[:, n * d:(n + 1) * d].astype(f32))
            branch = _dg(r[...], w_ref[n], 1, 0)
            dg_ref[:, n * d:(n + 1) * d] = (dm * branch * gate * (1.0 - gate)).astype(bf16)
            d_branch = dm * gate
            dr[...] = _dg(d_branch, w_ref[n], 1, 1)
            dw_ref[n] += _dg(r[...], d_branch, 0, 0)

    specs = _merge_in_specs(ts, d)
    row_f = pl.BlockSpec((ts, MIX), lambda i: (i, 0))
    return pl.pallas_call(
        body, name=name, grid=(s // ts,),
        in_specs=specs + [pl.BlockSpec((ts, d), lambda i: (i, 0)), pl.BlockSpec(memory_space=pl.ANY)],
        out_specs=[specs[3], row_f, row_f, row_f, specs[4]],
        out_shape=[jax.ShapeDtypeStruct(dproj.shape, bf16)] + [jax.ShapeDtypeStruct((s, MIX), f32)] * 3
        + [jax.ShapeDtypeStruct(w_branch.shape, f32)],
        input_output_aliases={6: 0}, compiler_params=_params(1),
    )(out_a, out_b, out_c, proj, w_branch, d_merged, dproj)


def swiglu_fwd(gu, *, name):
    g2, s, w = gu.shape
    ng = g2 // 2
    ts = _tile(s, 1024, 16)

    def body(g_ref, u_ref, o_ref):
        g = g_ref[...].astype(f32)
        o_ref[...] = (g * _sigmoid(g) * u_ref[...].astype(f32)).astype(bf16)

    return pl.pallas_call(
        body, name=name, grid=(s // ts, ng),
        in_specs=[pl.BlockSpec((None, ts, w), lambda i, j: (j, i, 0)),
                  pl.BlockSpec((None, ts, w), lambda i, j: (ng + j, i, 0))],
        out_specs=pl.BlockSpec((None, ts, w), lambda i, j: (j, i, 0)), out_shape=jax.ShapeDtypeStruct((ng, s, w), bf16),
        compiler_params=_params(2),
    )(gu, gu)


def swiglu_bwd(gu, d_act, *, name):
    g2, s, w = gu.shape
    ng = g2 // 2
    ts = _tile(s, 1024, 16)

    def body(g_ref, u_ref, d_ref, dg_ref, du_ref):
        g, d = g_ref[...].astype(f32), d_ref[...].astype(f32)
        sig = _sigmoid(g)
        dg_ref[...] = (d * u_ref[...].astype(f32) * sig * (1.0 + g * (1.0 - sig))).astype(bf16)
        du_ref[...] = (d * g * sig).astype(bf16)

    lo = pl.BlockSpec((None, ts, w), lambda i, j: (j, i, 0))
    return pl.pallas_call(
        body, name=name, grid=(s // ts, ng),
        in_specs=[lo, pl.BlockSpec((None, ts, w), lambda i, j: (ng + j, i, 0)), lo], out_specs=[lo, lo],
        out_shape=[jax.ShapeDtypeStruct((ng, s, w), bf16)] * 2, compiler_params=_params(2),
    )(gu, gu, d_act)


def adamw(w, m, v, g_parts, *, name):
    n_layers = len(g_parts)
    n_parts, r, cols = g_parts[0].shape
    lanes = -(-cols // LANES) * LANES
    tr = _tile(r, max(16, (128 * 1024) // lanes), 16)
    nr = r // tr

    def body(w_ref, m_ref, v_ref, *rest):
        gp_refs, (g_ref, d_ref, nm_ref, nv_ref) = rest[:n_layers], rest[n_layers:]
        layer = pl.program_id(0)
        g = jnp.zeros((tr, cols), f32)
        for l, gp_ref in enumerate(gp_refs):
            g_l = gp_ref[0].astype(f32)
            for k in range(1, n_parts):
                g_l = g_l + gp_ref[k].astype(f32)
            g = jnp.where(layer == l, g_l, g)
        nm = ADAM_B1 * m_ref[...] + (1.0 - ADAM_B1) * g
        nv = ADAM_B2 * v_ref[...] + (1.0 - ADAM_B2) * jnp.square(g)
        m_hat = nm / (1.0 - ADAM_B1 ** ADAM_STEP)
        v_hat = nv / (1.0 - ADAM_B2 ** ADAM_STEP)
        g_ref[...] = g
        d_ref[...] = -ADAM_LR * (m_hat / (jnp.sqrt(v_hat) + ADAM_EPS) + ADAM_WD * w_ref[...])
        nm_ref[...] = nm
        nv_ref[...] = nv

    row = pl.BlockSpec((tr, cols), lambda l, i: (l * nr + i, 0))

    def parts_spec(own):
        return pl.BlockSpec((n_parts, tr, cols),
                            lambda l, i: (0, jnp.where(l == own, i, jnp.where(l < own, 0, nr - 1)), 0))

    return pl.pallas_call(
        body, name=name, grid=(n_layers, nr), in_specs=[row, row, row] + [parts_spec(l) for l in range(n_layers)],
        out_specs=[row] * 4, out_shape=[jax.ShapeDtypeStruct(w.shape, f32)] * 4, compiler_params=_params(2),
    )(w, m, v, *g_parts)


def _mesh_pos():
    return lax.axis_index("x"), lax.axis_index("y"), lax.axis_index("c")


def _dev_index(p):
    return 4 * p[0] + 2 * p[1] + p[2]


class Exchange:
    def __init__(self, kind, arrays):
        self.kind, self.arrays, self.n = kind, list(arrays), len(arrays)
        self.specs = [pl.BlockSpec(memory_space=pl.ANY)] * self.n
        self.out_shapes = [jax.ShapeDtypeStruct(((N_DEV,) if kind == "gather" else ()) + a.shape, a.dtype)
                           for a in self.arrays]
        self.scratch = [pltpu.SemaphoreType.DMA((self.n, N_DEV - 1)), pltpu.SemaphoreType.DMA((self.n, N_DEV - 1)),
                        pltpu.SemaphoreType.DMA((self.n,))]

    def _copies(self, in_refs, out_refs, sems, with_arrivals):
        send_sems, recv_sems, local_sems = sems
        x, y, c = _mesh_pos()
        mine = _dev_index((x, y, c))

        def src(a, slab):
            return in_refs[a] if self.kind == "gather" else in_refs[a].at[slab]

        local = [pltpu.make_async_copy(src(a, mine), out_refs[a].at[mine], local_sems.at[a]) for a in range(self.n)]
        sends, arrivals = [], []
        for k in range(1, N_DEV):
            peer = (1 - x if k & 4 else x, 1 - y if k & 2 else y, 1 - c if k & 1 else c)
            theirs = _dev_index(peer)
            for a in range(self.n):
                to = dict(send_sem=send_sems.at[a, k - 1], recv_sem=recv_sems.at[a, k - 1], device_id=peer,
                          device_id_type=pl.DeviceIdType.MESH)
                sends.append(pltpu.make_async_remote_copy(src_ref=src(a, theirs), dst_ref=out_refs[a].at[mine], **to))
                if with_arrivals:
                    arrivals.append(pltpu.make_async_remote_copy(src_ref=src(a, theirs),
                                                                 dst_ref=out_refs[a].at[theirs], **to))
        return local, sends, arrivals

    def start(self, in_refs, out_refs, sems):
        local, sends, _ = self._copies(in_refs, out_refs, sems, False)
        for cp in local + sends:
            cp.start()

    def wait(self, in_refs, out_refs, sems):
        local, sends, arrivals = self._copies(in_refs, out_refs, sems, True)
        for cp in arrivals:
            cp.wait_recv()
        for cp in sends:
            cp.wait_send()
        for cp in local:
            cp.wait()

    def run_around(self, grid, in_refs, out_refs, sems, *, before):
        at = None
        for axis, size in enumerate(grid):
            hit = pl.program_id(axis) == (0 if before else size - 1)
            at = hit if at is None else at & hit

        @pl.when(at)
        def _():
            (self.start if before else self.wait)(in_refs, out_refs, sems)


def host_call(side, body, *, name, grid, in_specs, out_specs, out_shape, scratch_shapes, args, aliases):
    n_in, n_out = len(in_specs), len(out_specs)
    if side is None:
        kernel_body = body
    else:
        n = side.n
        in_specs, args = in_specs + side.specs, tuple(args) + tuple(side.arrays)
        out_specs, out_shape = out_specs + side.specs, out_shape + side.out_shapes
        scratch_shapes = scratch_shapes + side.scratch

        def kernel_body(*refs):
            ins, side_in = refs[:n_in], refs[n_in:n_in + n]
            outs, side_out = refs[n_in + n:n_in + n + n_out], refs[n_in + n + n_out:n_in + 2 * n + n_out]
            scratch, sems = refs[n_in + 2 * n + n_out:-3], refs[-3:]
            side.run_around(grid, side_in, side_out, sems, before=True)
            body(*ins, *outs, *scratch)
            side.run_around(grid, side_in, side_out, sems, before=False)

    outs = pl.pallas_call(
        kernel_body, name=name, grid=grid, in_specs=in_specs, out_specs=out_specs, out_shape=out_shape,
        scratch_shapes=scratch_shapes, input_output_aliases=aliases, compiler_params=_params(len(grid)),
    )(*args)
    return outs if side is None else (outs[:n_out], outs[n_out:])


def all_gather(blocks, *, name):
    n = len(blocks)
    any_spec = pl.BlockSpec(memory_space=pl.ANY)

    def body(*refs):
        ins, outs = refs[:n], refs[n:2 * n]
        send_sems, recv_sems, local_sems = refs[2 * n:]
        x, y, c = _mesh_pos()
        me, sibling = (x, y, c), (x, y, 1 - c)
        chips = [(1 - x, y), (x, 1 - y), (1 - x, 1 - y)]

        def copy(a, k, block, to, src=None):
            dst = outs[a].at[_dev_index(block)]
            return pltpu.make_async_remote_copy(
                src_ref=dst if src is None else src, dst_ref=dst, send_sem=send_sems.at[a, k],
                recv_sem=recv_sems.at[a, k], device_id=to, device_id_type=pl.DeviceIdType.MESH)

        mine = [pltpu.make_async_copy(ins[a], outs[a].at[_dev_index(me)], local_sems.at[a]) for a in range(n)]
        for cp in mine:
            cp.start()
        first = []
        for a in range(n):
            first.append(copy(a, 0, me, sibling, src=ins[a]))
            first += [copy(a, 1 + j, me, (*chip, c), src=ins[a]) for j, chip in enumerate(chips)]
        for cp in first:
            cp.start()
        passed = []
        for j, chip in enumerate(chips):
            for a in range(n):
                copy(a, 1 + j, (*chip, c), me).wait_recv()
                fwd = copy(a, 4 + j, (*chip, c), sibling)
                fwd.start()
                passed.append(fwd)
        for a in range(n):
            copy(a, 0, sibling, me).wait_recv()
            for j, chip in enumerate(chips):
                copy(a, 4 + j, (*chip, 1 - c), me).wait_recv()
        for cp in first + passed:
            cp.wait_send()
        for cp in mine:
            cp.wait()

    return pl.pallas_call(
        body, name=name, in_specs=[any_spec] * n, out_specs=[any_spec] * n,
        out_shape=[jax.ShapeDtypeStruct((N_DEV,) + b.shape, b.dtype) for b in blocks],
        scratch_shapes=[pltpu.SemaphoreType.DMA((n, 7)), pltpu.SemaphoreType.DMA((n, 7)),
                        pltpu.SemaphoreType.DMA((n,))],
    )(*blocks)


def exchange(jobs, *, name):
    total = sum(j.n for j in jobs)

    def body(*refs):
        ins, outs, sems = refs[:total], refs[total:2 * total], refs[2 * total:]
        pieces, off = [], 0
        for i, j in enumerate(jobs):
            pieces.append((ins[off:off + j.n], outs[off:off + j.n], sems[3 * i:3 * i + 3]))
            off += j.n
        for j, piece in zip(jobs, pieces):
            j.start(*piece)
        for j, piece in zip(jobs, pieces):
            j.wait(*piece)

    outs = pl.pallas_call(
        body, name=name, in_specs=[s for j in jobs for s in j.specs], out_specs=[s for j in jobs for s in j.specs],
        out_shape=[s for j in jobs for s in j.out_shapes], scratch_shapes=[s for j in jobs for s in j.scratch],
    )(*[a for j in jobs for a in j.arrays])
    split, off = [], 0
    for j in jobs:
        split.append(outs[off:off + j.n])
        off += j.n
    return split


def _rows128(arr):
    flat = arr.reshape(-1)
    rows = -(-flat.shape[0] // (8 * LANES)) * 8
    return jnp.pad(flat, (0, rows * LANES - flat.shape[0])).reshape(rows, LANES)


def _pad_lanes(row, width=LANES, at=0):
    return jnp.pad(row, (at, width - at - row.shape[0])).reshape(1, width)


def _w_in_sections(got):
    d = got.shape[1]
    wi = jnp.transpose(got, (1, 0, 2)).reshape(d, -1)
    w_main = jnp.concatenate([wi[:, :C_B], wi[:, C_Z:C_BA], wi[:, C_QKV:C_Z], wi[:, C_G:], wi[:, C_B:C_QKV]], axis=1)
    return w_main, jnp.pad(wi[:, C_BA:C_G], ((0, 0), (0, LANES - (C_G - C_BA))))


def _w_in_parts(gw_main, gw_ba):
    d = gw_main.shape[0]
    full = jnp.concatenate([gw_main[:, SEC_A:SEC_Z], gw_main[:, SEC_B:], gw_main[:, SEC_QKV:SEC_G],
                            gw_main[:, SEC_Z:SEC_QKV], gw_ba[:, :C_G - C_BA], gw_main[:, SEC_G:SEC_B]], axis=1)
    return jnp.transpose(full.reshape(d, N_DEV, -1), (1, 0, 2))


def kernel(x, positions, attn_norm, w_in, sgu_ln_g, sgu_ln_b, sgu_w, sgu_b, attn_sinks, dn_conv_w, dn_a_log, dn_dt_bias, dn_norm, w_branch, w_out, ffn_norm, w_gate_up, w_down, final_norm, loss_target, m_attn_norm, m_w_in, m_sgu_ln_g, m_sgu_ln_b, m_sgu_w, m_sgu_b, m_attn_sinks, m_dn_conv_w, m_dn_a_log, m_dn_dt_bias, m_dn_norm, m_w_branch, m_w_out, m_ffn_norm, m_w_gate_up, m_w_down, m_final_norm, v_attn_norm, v_w_in, v_sgu_ln_g, v_sgu_ln_b, v_sgu_w, v_sgu_b, v_attn_sinks, v_dn_conv_w, v_dn_a_log, v_dn_dt_bias, v_dn_norm, v_w_branch, v_w_out, v_ffn_norm, v_w_gate_up, v_w_down, v_final_norm):
    given = dict(locals())
    depth, d_model = attn_norm.shape
    s = x.shape[1]
    x2 = x.reshape(s, d_model)
    target = loss_target.reshape(s, d_model)
    posf = positions.reshape(s, 1).astype(f32)
    inv_freq = ROPE_THETA ** (-jnp.arange(0, ROPE_DIM, 2, dtype=f32) / ROPE_DIM)
    inv_head = jnp.concatenate([inv_freq, inv_freq, jnp.zeros((SWA_HD - ROPE_DIM,), f32)])
    tables = rope_tables(posf, jnp.tile(inv_head, LANES // SWA_HD).reshape(1, LANES), name="rope_tables")

    assert depth == 2, depth
    gathered = dict(zip([("w_in", 0), ("dn_conv_w", 0), ("dn_conv_w", 1)], all_gather(
        [w_in[0].astype(bf16), dn_conv_w[0], dn_conv_w[1]], name="gather_first")))
    riders = {"l0_in_proj": [("w_branch", 0), ("w_out", 0)],
              "l0_deltanet": [("w_gate_up", 0), ("w_down", 0), ("w_in", 1)],
              "l1_deltanet": [("w_branch", 1), ("w_out", 1), ("w_gate_up", 1), ("w_down", 1)]}

    def gathering(host, call, *args, **kw):
        if host not in riders:
            return call(*args, name=host, **kw)
        side = Exchange("gather", [given[n][l].astype(bf16) for n, l in riders[host]])
        out, got = call(*args, side=side, name=host, **kw)
        gathered.update(zip(riders[host], got))
        return out

    layers, saved = [], []
    h_in = x2
    for l in range(depth):
        t = f"l{l}_"
        w_main, w_ba = _w_in_sections(gathered["w_in", l])
        conv_full = jnp.transpose(gathered["dn_conv_w", l], (1, 0, 2)).reshape(DN_CONV, -1)
        p = dict(
            w_main=w_main, w_ba=w_ba, conv_w8=jnp.pad(conv_full, ((0, CONV_PAD - DN_CONV), (0, 0))),
            attn_norm=attn_norm[l].reshape(1, -1), ffn_norm=ffn_norm[l].reshape(1, -1),
            ln_g=sgu_ln_g[l].reshape(1, -1), ln_b=sgu_ln_b[l].reshape(1, -1), sgu_w=sgu_w[l], sgu_bt=sgu_b[l].T,
            sink_row=_pad_lanes(attn_sinks[l]), alog_row=_pad_lanes(dn_a_log[l], at=DN_HEADS),
            dtb_row=_pad_lanes(dn_dt_bias[l], at=DN_HEADS), norm_row=dn_norm[l].reshape(1, -1))
        layers.append(p)
        h = rmsnorm_fwd(h_in, p["attn_norm"], name=t + "attn_norm")
        proj = gathering(t + "in_proj", matmul, h, p["w_main"], "nn", bf16)
        p_ba = matmul(h, p["w_ba"], "nn", f32, name=t + "in_proj_ba")
        out_a = sgu_fwd(proj, p["ln_g"], p["ln_b"], p["sgu_w"], p["sgu_bt"], name=t + "sgu")
        out_b = swa_fwd(proj, p["sink_row"], tables, name=t + "swa")
        xc = conv_fwd(proj, p["conv_w8"], name=t + "dn_conv")
        out_c, states, tri = gathering(t + "deltanet", dn_fwd, xc, proj, p_ba, p["alog_row"], p["dtb_row"], p["norm_row"])
        p.update(w_branch=jnp.transpose(gathered["w_branch", l], (1, 2, 0, 3)).reshape(3, MIX, d_model),
                 w_out=gathered["w_out", l].reshape(d_model, d_model),
                 w_gu=gathered["w_gate_up", l],
                 w_down=gathered["w_down", l].reshape(N_DEV // 2, -1, d_model))
        merged = merge_fwd(out_a, out_b, out_c, proj, p["w_branch"], name=t + "merge")
        x_mid = matmul(merged, p["w_out"], "nn", f32, residual=h_in, tn=1024, name=t + "out_proj")
        h2 = rmsnorm_fwd(x_mid, p["ffn_norm"], name=t + "ffn_norm")
        gu = matmul(h2, p["w_gu"], "nn", bf16, group="n", name=t + "gate_up")
        act = swiglu_fwd(gu, name=t + "swiglu")
        x_out = matmul(act, p["w_down"], "nn", f32, residual=x_mid, group="k", tn=1024, name=t + "down")
        saved.append(dict(x_in=h_in, h=h, proj=proj, p_ba=p_ba, out_a=out_a, out_b=out_b, out_c=out_c, xc=xc,
                          states=states, tri=tri, merged=merged, x_mid=x_mid, h2=h2, gu=gu, act=act))
        h_in = x_out

    dx, d_final_norm, loss_row = loss_head(h_in, final_norm.reshape(1, -1), target, name="loss_head")
    loss = lax.psum(loss_row[0, 0], MESH_AXES)

    shard_names = ["w_in", "dn_conv_w", "w_branch", "w_out", "w_gate_up", "w_down"]
    rep_names = ["attn_norm", "sgu_ln_g", "sgu_ln_b", "sgu_w", "sgu_b", "attn_sinks", "dn_a_log", "dn_dt_bias",
                 "dn_norm", "ffn_norm"]
    parts, received, per_layer = {}, {}, []
    senders = {"l0_b_swa": [("w_gate_up", 1), ("w_down", 1), ("w_out", 1), ("w_branch", 1)],
               "l0_b_deltanet": [("w_in", 1), ("w_gate_up", 0), ("w_down", 0), ("w_out", 0), ("w_branch", 0)],
               "l0_b_in_proj_dx": [("w_in", 0)]}

    def scattering(host, call, *args, **kw):
        if host not in senders:
            return call(*args, name=host, **kw)
        out, got = call(*args, side=Exchange("scatter", [parts[key] for key in senders[host]]), name=host, **kw)
        received.update(zip(senders[host], got))
        return out

    for l in reversed(range(depth)):
        p, sv, t = layers[l], saved[l], f"l{l}_b_"
        d_act = matmul(dx, p["w_down"], "nt", bf16, group="n", name=t + "down_dx")
        gw_down = matmul(sv["act"], dx, "tn", bf16, group="m", tk=2048, tn=512, name=t + "down_dw")
        d_gate, d_up = swiglu_bwd(sv["gu"], d_act, name=t + "swiglu")
        gw_gu = jnp.concatenate([matmul(sv["h2"], d_half, "tn", bf16, group="n", tk=2048, name=t + "gate_up_dw" + tag)
                                 for d_half, tag in ((d_gate, "_gate"), (d_up, "_up"))], axis=0)
        half = N_DEV // 2
        d_h2 = matmul(d_gate, p["w_gu"][:half], "nt", f32, group="k", tn=1024, name=t + "gate_up_dx_gate")
        d_h2 = matmul(d_up, p["w_gu"][half:], "nt", f32, group="k", tn=1024, residual=d_h2, name=t + "gate_up_dx_up")
        dx_mid, g_ffn = rmsnorm_bwd(sv["x_mid"], p["ffn_norm"], d_h2, dx, name=t + "ffn_norm")
        d_merged = matmul(dx_mid, p["w_out"], "nt", f32, tn=1024, name=t + "out_proj_dx")
        gw_out = matmul(sv["merged"], dx_mid, "tn", bf16, tk=2048, tn=1024, name=t + "out_proj_dw")
        dproj = lax.empty((s, W_MAIN), bf16)
        dproj, d_a, d_b, d_c, gw_branch = merge_bwd(sv["out_a"], sv["out_b"], sv["out_c"], sv["proj"], p["w_branch"],
                                                   d_merged, dproj, name=t + "merge")
        parts.update({("w_gate_up", l): gw_gu, ("w_down", l): gw_down.reshape(N_DEV, -1, d_model),
                      ("w_out", l): gw_out.reshape(N_DEV, -1, d_model),
                      ("w_branch", l): jnp.transpose(gw_branch.reshape(3, MIX, N_DEV, -1), (2, 0, 1, 3)).astype(bf16)})
        dproj, g_ln_g, g_ln_b, g_sgu_w, g_sgu_bt = sgu_bwd(sv["proj"], p["ln_g"], p["ln_b"], p["sgu_w"], p["sgu_bt"],
                                                         d_a, dproj, name=t + "sgu")
        dproj, g_sink = scattering(t + "swa", swa_bwd, sv["proj"], p["sink_row"], tables, d_b, dproj)
        dproj, dxc, dba, g_alog, g_dtb, g_dnorm = scattering(
            t + "deltanet", dn_bwd, sv["xc"], sv["proj"], sv["p_ba"], p["alog_row"], p["dtb_row"], p["norm_row"],
            sv["states"], sv["tri"], d_c, dproj)
        dproj, g_conv8 = conv_bwd(sv["proj"], p["conv_w8"], dxc, dproj, name=t + "dn_conv")
        gw_main = matmul(sv["h"], dproj, "tn", bf16, tk=2048, name=t + "in_proj_dw")
        gw_ba = matmul(sv["h"], dba, "tn", bf16, tk=2048, name=t + "in_proj_ba_dw")
        parts["w_in", l] = _w_in_parts(gw_main, gw_ba)
        d_h = scattering(t + "in_proj_dx", matmul, dproj, p["w_main"], "nt", f32, tk=2304, tn=1024)
        d_h = matmul(dba, p["w_ba"], "nt", f32, residual=d_h, name=t + "in_proj_ba_dx")
        dx, g_attn = rmsnorm_bwd(sv["x_in"], p["attn_norm"], d_h, dx_mid, name=t + "attn_norm")
        per_layer.append(dict(
            dn_conv_w=jnp.transpose(g_conv8[:DN_CONV].reshape(DN_CONV, N_DEV, -1), (1, 0, 2)),
            attn_norm=g_attn, sgu_ln_g=g_ln_g, sgu_ln_b=g_ln_b, sgu_w=g_sgu_w, sgu_b=g_sgu_bt.T,
            attn_sinks=g_sink[0, :SWA_HEADS], dn_a_log=g_alog[0, DN_HEADS:2 * DN_HEADS],
            dn_dt_bias=g_dtb[0, DN_HEADS:2 * DN_HEADS], dn_norm=g_dnorm, ffn_norm=g_ffn))
    per_layer.reverse()

    conv_parts = jnp.concatenate([pp["dn_conv_w"] for pp in per_layer], axis=1)
    rep_grads = {n: jnp.stack([pp[n].reshape(given[n].shape[1:]) for pp in per_layer]) for n in rep_names}
    rep_grads["final_norm"] = d_final_norm[0]
    rep_names = rep_names + ["final_norm"]
    rep_rows = [_rows128(given[n]).shape[0] for n in rep_names]
    pad_rows = -sum(rep_rows) % 16

    def small_rows(values):
        pieces = [_rows128(values[n]) for n in rep_names]
        return jnp.concatenate(pieces + [jnp.zeros((pad_rows, LANES), f32)], axis=0)

    (got_conv,), (small_all,) = exchange(
        [Exchange("scatter", [conv_parts]), Exchange("gather", [small_rows(rep_grads)])], name="exchange_last")

    results = [{}, {}, {}, {}]
    for n in shard_names:
        shp = given[n].shape
        two = (-1, shp[-1])
        by_layer = [got_conv] if n == "dn_conv_w" else [received[n, l].reshape(N_DEV, -1, shp[-1]) for l in range(depth)]
        outs = adamw(given[n].reshape(two), given["m_" + n].reshape(two), given["v_" + n].reshape(two), by_layer,
                     name="adamw_" + n)
        for res, val in zip(results, outs):
            res[n] = val.reshape(shp)
    outs = adamw(small_rows(given), small_rows({n: given["m_" + n] for n in rep_names}),
                 small_rows({n: given["v_" + n] for n in rep_names}), [small_all], name="adamw_replicated")
    for res, val in zip(results, outs):
        row = 0
        for n, nr in zip(rep_names, rep_rows):
            res[n] = val[row:row + nr].reshape(-1)[:given[n].size].reshape(given[n].shape)
            row += nr
    order = ["attn_norm", "w_in", "sgu_ln_g", "sgu_ln_b", "sgu_w", "sgu_b", "attn_sinks", "dn_conv_w", "dn_a_log",
             "dn_dt_bias", "dn_norm", "w_branch", "w_out", "ffn_norm", "w_gate_up", "w_down", "final_norm"]
    return (loss, dx.reshape(x.shape), *[res[n] for res in results for n in order])
```

```python
import functools

import jax
import jax.numpy as jnp
from jax import lax
from jax.experimental import pallas as pl
from jax.experimental.pallas import tpu as pltpu

f32 = jnp.float32
bf16 = jnp.bfloat16

N_DEV = 8
MESH_AXES = ("x", "y", "c")
NORM_EPS = 1e-6
MIX = 512
SGU_GROUPS, SGU_CHUNK = 4, 128
SWA_HEADS, SWA_KV, SWA_HD, WINDOW = 8, 2, 64, 128
ROPE_THETA, ROPE_DIM = 500000.0, 16
DN_HEADS, DN_HD, DN_CONV, DN_CHUNK = 4, 128, 4, 64
ADAM_LR, ADAM_B1, ADAM_B2, ADAM_EPS, ADAM_WD, ADAM_STEP = 0.001, 0.9, 0.999, 1e-08, 0.01, 10

LANES = 128
VMEM_LIMIT = 56 * 1024 * 1024

SEC_A, SEC_Z, SEC_QKV, SEC_G, SEC_B = 0, 1024, 1536, 3072, 6144
W_MAIN = 6912
C_B, C_QKV, C_Z, C_BA, C_G = 1024, 1792, 3328, 3840, 3848


def _params(n_axes, **kw):
    return pltpu.CompilerParams(dimension_semantics=("arbitrary",) * n_axes, vmem_limit_bytes=VMEM_LIMIT, **kw)


def _tile(n, target, mult=LANES):
    if n <= target:
        return n
    best = None
    for t in range(mult, target + 1, mult):
        if n % t == 0:
            best = t
    assert best is not None, (n, target, mult)
    return best


def _dg(a, b, ca, cb):
    return lax.dot_general(a.astype(bf16), b.astype(bf16), (((ca,), (cb,)), ((), ())), preferred_element_type=f32)


def _dg3(a, b, ca, cb):
    a_hi, b_hi = a.astype(bf16), b.astype(bf16)
    a_lo, b_lo = (a - a_hi.astype(f32)).astype(bf16), (b - b_hi.astype(f32)).astype(bf16)

    def dot(p, q):
        return lax.dot_general(p, q, (((ca,), (cb,)), ((), ())), preferred_element_type=f32)

    return dot(a_hi, b_hi) + (dot(a_hi, b_lo) + dot(a_lo, b_hi))


def _differentiable_dot(core):
    @functools.partial(jax.custom_vjp, nondiff_argnums=(2, 3))
    def dot(a, b, ca, cb):
        return core(a, b, ca, cb)

    def fwd(a, b, ca, cb):
        return core(a, b, ca, cb), (a, b)

    def bwd(ca, cb, res, ct):
        a, b = res
        da = core(ct, b, 1, 1 - cb) if ca == 1 else core(b, ct, 1 - cb, 1)
        db = core(a, ct, 1 - ca, 0) if cb == 0 else core(ct, a, 0, 1 - ca)
        return da, db

    dot.defvjp(fwd, bwd)
    return dot


bdot = _differentiable_dot(_dg)
_hdot = _differentiable_dot(_dg3)


def hdot(a, b, ca=1, cb=0):
    return _hdot(a, b, ca, cb)


@functools.partial(jax.custom_vjp, nondiff_argnums=(1,))
def lroll(x, shift):
    return pltpu.roll(x, shift, 1)


def _lroll_fwd(x, shift):
    return pltpu.roll(x, shift, 1), None


def _lroll_bwd(shift, _, ct):
    return (pltpu.roll(ct, ct.shape[1] - shift, 1),)


lroll.defvjp(_lroll_fwd, _lroll_bwd)


@functools.partial(jax.custom_vjp, nondiff_argnums=(1,))
def tri_inv(low, nil):
    n = low.shape[0]
    row = lax.broadcasted_iota(jnp.int32, (n, n), 0)
    col = lax.broadcasted_iota(jnp.int32, (n, n), 1)
    eye = (row == col).astype(f32)
    m = -low
    p = eye + m
    span = 2
    while span < nil:
        m = hdot(m, m)
        p = p + hdot(p, m)
        span *= 2
    return p


def _tri_inv_fwd(low, nil):
    t = tri_inv(low, nil)
    return t, t


def _tri_inv_bwd(nil, t, dt):
    return (-hdot(t, hdot(dt, t, 1, 1), 0, 0),)


tri_inv.defvjp(_tri_inv_fwd, _tri_inv_bwd)


@jax.custom_vjp
def tri_inv_known(low, t):
    return t


def _tri_inv_known_fwd(low, t):
    return t, t


def _tri_inv_known_bwd(t, dt):
    return _tri_inv_bwd(None, t, dt) + (jnp.zeros_like(t),)


tri_inv_known.defvjp(_tri_inv_known_fwd, _tri_inv_known_bwd)


def _sigmoid(x):
    return 1.0 / (1.0 + jnp.exp(-x))


def _rms(x, g):
    return x * lax.rsqrt(jnp.mean(x * x, axis=-1, keepdims=True) + NORM_EPS) * g


def _lane_col(x, lane_idx):
    lane = lax.broadcasted_iota(jnp.int32, x.shape, 1)
    return jnp.sum(jnp.where(lane == lane_idx, x, 0.0), axis=1, keepdims=True)


def matmul(a, b, mode, out_dtype, *, residual=None, group=None, side=None, tm=1024, tn=768, tk=1024, name):
    dims = {"a": ("m", "k") if mode != "tn" else ("k", "m"),
            "b": {"nn": ("k", "n"), "nt": ("n", "k"), "tn": ("k", "n")}[mode], "o": ("m", "n")}
    full, groups = {}, 1
    for arr, key in ((a, "a"), (b, "b")):
        grouped = group in dims[key]
        if grouped:
            groups = arr.shape[0]
        full[dims[key][0]], full[dims[key][1]] = arr.shape[1:] if grouped else arr.shape
    want = {"m": tm, "n": tn, "k": tk}
    per_step = min(groups, 4) if group == "k" else 1
    tiles = {d: full[d] if d == group else _tile(full[d], want[d]) for d in "mnk"}
    steps = {d: groups // per_step if d == group else full[d] // tiles[d] for d in "mnk"}

    def spec(key):
        d0, d1 = dims[key]

        def index(i, j, kk):
            at = {"m": i, "n": j, "k": kk}
            if group in (d0, d1):
                return (at[group], 0 if d0 == group else at[d0], 0 if d1 == group else at[d1])
            return (at[d0], at[d1])

        block = (tiles[d0], tiles[d1])
        if group in (d0, d1):
            block = ((per_step if group == "k" else None),) + block
        return pl.BlockSpec(block, index)

    ca, cb = {"nn": (1, 0), "nt": (1, 1), "tn": (0, 0)}[mode]
    nk = steps["k"]
    o_spec = spec("o")
    out_shape = (groups, full["m"], full["n"]) if group in ("m", "n") else (full["m"], full["n"])
    has_res = residual is not None

    def product(a_ref, b_ref):
        if group != "k":
            return _dg(a_ref[...], b_ref[...], ca, cb)
        total = _dg(a_ref[0], b_ref[0], ca, cb)
        for g in range(1, per_step):
            total = total + _dg(a_ref[g], b_ref[g], ca, cb)
        return total

    def body(a_ref, b_ref, *rest):
        r_ref = rest[0] if has_res else None
        o_ref = rest[1 if has_res else 0]

        def emit(acc):
            o_ref[...] = (acc + r_ref[...] if has_res else acc).astype(out_dtype)

        if nk == 1:
            emit(product(a_ref, b_ref))
            return
        acc_ref = rest[-1]
        kk = pl.program_id(2)

        @pl.when(kk == 0)
        def _():
            acc_ref[...] = jnp.zeros_like(acc_ref)

        acc_ref[...] += product(a_ref, b_ref)

        @pl.when(kk == nk - 1)
        def _():
            emit(acc_ref[...])

    res = host_call(
        side, body, name=name, grid=(steps["m"], steps["n"], nk),
        in_specs=[spec("a"), spec("b")] + ([o_spec] if has_res else []), out_specs=[o_spec],
        out_shape=[jax.ShapeDtypeStruct(out_shape, out_dtype)],
        scratch_shapes=[pltpu.VMEM((tiles["m"], tiles["n"]), f32)] if nk > 1 else [], aliases={},
        args=(a, b) + ((residual,) if has_res else ()))
    return res[0] if side is None else (res[0][0], res[1])


def rmsnorm_fwd(x, g_row, *, name):
    s, d = x.shape
    ts = _tile(s, 512, 16)

    def body(x_ref, g_ref, o_ref):
        o_ref[...] = _rms(x_ref[...], g_ref[...]).astype(bf16)

    return pl.pallas_call(
        body, name=name, grid=(s // ts,),
        in_specs=[pl.BlockSpec((ts, d), lambda i: (i, 0)), pl.BlockSpec((1, d), lambda i: (0, 0))],
        out_specs=pl.BlockSpec((ts, d), lambda i: (i, 0)), out_shape=jax.ShapeDtypeStruct((s, d), bf16),
        compiler_params=_params(1),
    )(x, g_row)


def rmsnorm_bwd(x, g_row, dh, dres, *, name):
    s, d = x.shape
    ts = _tile(s, 512, 16)

    def body(x_ref, g_ref, dh_ref, dres_ref, dx_ref, dg_ref):
        @pl.when(pl.program_id(0) == 0)
        def _():
            dg_ref[...] = jnp.zeros_like(dg_ref)

        _, vjp = jax.vjp(_rms, x_ref[...], g_ref[...])
        dx, dg = vjp(dh_ref[...])
        dx_ref[...] = dx + dres_ref[...]
        dg_ref[...] += dg

    row = pl.BlockSpec((ts, d), lambda i: (i, 0))
    vec = pl.BlockSpec((1, d), lambda i: (0, 0))
    return pl.pallas_call(
        body, name=name, grid=(s // ts,), in_specs=[row, vec, row, row], out_specs=[row, vec],
        out_shape=[jax.ShapeDtypeStruct((s, d), f32), jax.ShapeDtypeStruct((1, d), f32)],
        compiler_params=_params(1),
    )(x, g_row, dh, dres)


def loss_head(x, g_row, target, *, name):
    s, d = x.shape
    ts = _tile(s, 512, 16)

    def body(x_ref, g_ref, t_ref, dx_ref, dg_ref, loss_ref):
        @pl.when(pl.program_id(0) == 0)
        def _():
            dg_ref[...] = jnp.zeros_like(dg_ref)
            loss_ref[...] = jnp.zeros_like(loss_ref)

        y, vjp = jax.vjp(_rms, x_ref[...], g_ref[...])
        err = y - t_ref[...]
        dx, dg = vjp(err * (1.0 / d))
        dx_ref[...] = dx
        dg_ref[...] += dg
        loss_ref[...] += 0.5 * jnp.sum(jnp.sum(err * err, axis=1, keepdims=True) * (1.0 / d), axis=0, keepdims=True)

    row = pl.BlockSpec((ts, d), lambda i: (i, 0))
    vec = pl.BlockSpec((1, d), lambda i: (0, 0))
    one = pl.BlockSpec((1, LANES), lambda i: (0, 0))
    return pl.pallas_call(
        body, name=name, grid=(s // ts,), in_specs=[row, vec, row], out_specs=[row, vec, one],
        out_shape=[jax.ShapeDtypeStruct((s, d), f32), jax.ShapeDtypeStruct((1, d), f32),
                   jax.ShapeDtypeStruct((1, LANES), f32)],
        compiler_params=_params(1),
    )(x, g_row, target)


def _sgu_chunk(p_a, ln_g, ln_b, w, b_t):
    t = SGU_CHUNK
    u = jax.nn.gelu(p_a[:, :MIX])
    v = jax.nn.gelu(p_a[:, MIX:])
    vc = v - jnp.mean(v, axis=-1, keepdims=True)
    vn = vc * lax.rsqrt(jnp.mean(vc * vc, axis=-1, keepdims=True) + NORM_EPS) * ln_g + ln_b
    causal = lax.broadcasted_iota(jnp.int32, (t, t), 0) >= lax.broadcasted_iota(jnp.int32, (t, t), 1)
    outs = []
    for g in range(SGU_GROUPS):
        sl = slice(g * LANES, (g + 1) * LANES)
        mixed = bdot(jnp.where(causal, w[g], 0.0), vn[:, sl], 1, 0) + b_t[:, g:g + 1]
        outs.append(u[:, sl] * mixed)
    return jnp.concatenate(outs, axis=1)


def _sgu_specs(s, ts):
    return [pl.BlockSpec((ts, 2 * MIX), lambda i: (i, SEC_A // (2 * MIX))),
            pl.BlockSpec((1, MIX), lambda i: (0, 0)), pl.BlockSpec((1, MIX), lambda i: (0, 0)),
            pl.BlockSpec((SGU_GROUPS, SGU_CHUNK, SGU_CHUNK), lambda i: (0, 0, 0)),
            pl.BlockSpec((SGU_CHUNK, SGU_GROUPS), lambda i: (0, 0))]


def sgu_fwd(proj, ln_g, ln_b, w, b_t, *, name):
    s = proj.shape[0]
    ts = _tile(s, 512)
    n_chunk = ts // SGU_CHUNK

    def body(p_ref, g_ref, b_ref, w_ref, bt_ref, o_ref):
        def step(c, carry):
            rows = pl.ds(pl.multiple_of(c * SGU_CHUNK, SGU_CHUNK), SGU_CHUNK)
            o_ref[rows, :] = _sgu_chunk(p_ref[rows, :].astype(f32), g_ref[...], b_ref[...], w_ref[...], bt_ref[...]).astype(bf16)
            return carry
        lax.fori_loop(0, n_chunk, step, 0)

    return pl.pallas_call(
        body, name=name, grid=(s // ts,), in_specs=_sgu_specs(s, ts),
        out_specs=pl.BlockSpec((ts, MIX), lambda i: (i, 0)), out_shape=jax.ShapeDtypeStruct((s, MIX), bf16),
        compiler_params=_params(1),
    )(proj, ln_g, ln_b, w, b_t)


def sgu_bwd(proj, ln_g, ln_b, w, b_t, d_out, dproj, *, name):
    s = proj.shape[0]
    ts = _tile(s, 512)
    n_chunk = ts // SGU_CHUNK

    def body(p_ref, g_ref, b_ref, w_ref, bt_ref, do_ref, _, dp_ref, dg_ref, db_ref, dw_ref, dbt_ref):
        @pl.when(pl.program_id(0) == 0)
        def _():
            dg_ref[...] = jnp.zeros_like(dg_ref)
            db_ref[...] = jnp.zeros_like(db_ref)
            dw_ref[...] = jnp.zeros_like(dw_ref)
            dbt_ref[...] = jnp.zeros_like(dbt_ref)

        def step(c, carry):
            rows = pl.ds(pl.multiple_of(c * SGU_CHUNK, SGU_CHUNK), SGU_CHUNK)
            _, vjp = jax.vjp(_sgu_chunk, p_ref[rows, :].astype(f32), g_ref[...], b_ref[...], w_ref[...], bt_ref[...])
            dp, dg, db, dw, dbt = vjp(do_ref[rows, :])
            dp_ref[rows, :] = dp.astype(bf16)
            dg_ref[...] += dg
            db_ref[...] += db
            dw_ref[...] += dw
            dbt_ref[...] += dbt
            return carry
        lax.fori_loop(0, n_chunk, step, 0)

    specs = _sgu_specs(s, ts)
    return pl.pallas_call(
        body, name=name, grid=(s // ts,),
        in_specs=specs + [pl.BlockSpec((ts, MIX), lambda i: (i, 0)), pl.BlockSpec(memory_space=pl.ANY)],
        out_specs=[specs[0], specs[1], specs[2], specs[3], specs[4]],
        out_shape=[jax.ShapeDtypeStruct(dproj.shape, bf16), jax.ShapeDtypeStruct((1, MIX), f32),
                   jax.ShapeDtypeStruct((1, MIX), f32), jax.ShapeDtypeStruct(w.shape, f32),
                   jax.ShapeDtypeStruct(b_t.shape, f32)],
        input_output_aliases={6: 0}, compiler_params=_params(1),
    )(proj, ln_g, ln_b, w, b_t, d_out, dproj)


def rope_tables(posf, inv_freq, *, name):
    s = posf.shape[0]
    ts = _tile(s, 1024, 8)
    half = ROPE_DIM // 2

    def body(pos_ref, inv_ref, o_ref):
        d = lax.broadcasted_iota(jnp.int32, (1, LANES), 1) % SWA_HD
        ang = pos_ref[...] * inv_ref[...]
        sin = jnp.sin(ang)
        o_ref[0] = jnp.cos(ang)
        o_ref[1] = jnp.where(d < half, sin, 0.0)
        o_ref[2] = jnp.where((d >= half) & (d < ROPE_DIM), sin, 0.0)

    return pl.pallas_call(
        body, name=name, grid=(s // ts,),
        in_specs=[pl.BlockSpec((ts, 1), lambda i: (i, 0)), pl.BlockSpec((1, LANES), lambda i: (0, 0))],
        out_specs=pl.BlockSpec((3, ts, LANES), lambda i: (0, i, 0)), out_shape=jax.ShapeDtypeStruct((3, s, LANES), f32),
        compiler_params=_params(1),
    )(posf, inv_freq)


def _rope(x, table):
    w = x.shape[1]
    half = ROPE_DIM // 2
    c, lo, hi = (jnp.concatenate([table[i]] * (w // LANES), axis=1) for i in range(3))
    return x * c - lroll(x, w - half) * lo + lroll(x, half) * hi


def _swa_block(q, kp, kc, vp, vc, sink_row, table_q, table_p, prev_ok, wide):
    t = WINDOW
    q = _rope(q, table_q) * (SWA_HD ** -0.5)
    keys = jnp.concatenate([_rope(kp, table_p), _rope(kc, table_q)], axis=0)
    vals = jnp.concatenate([vp, vc], axis=0)
    own = lax.broadcasted_iota(jnp.int32, (t, t), 0) >= lax.broadcasted_iota(jnp.int32, (t, t), 1)
    lane_half = lax.broadcasted_iota(jnp.int32, (t, LANES), 1) // SWA_HD
    group = SWA_HEADS // SWA_KV
    slabs = []
    for pair in range(SWA_HEADS // 2):
        q_pair = q[:, pair * LANES:(pair + 1) * LANES]
        acc = jnp.zeros((t, LANES), f32)
        for half in range(2):
            h = 2 * pair + half
            kv = h // group
            qm = jnp.where(lane_half == half, q_pair, 0.0)
            if half != kv:
                qm = lroll(qm, SWA_HD)
            if wide:
                both = bdot(qm, keys, 1, 1)
                s_prev, s_own = both[:, :t], both[:, t:]
            else:
                s_prev, s_own = bdot(qm, keys[:t], 1, 1), bdot(qm, keys[t:], 1, 1)
            logits = jnp.where(own, s_own, jnp.where(prev_ok, s_prev, -1e30))
            sink = _lane_col(sink_row, h)
            m = lax.stop_gradient(jnp.maximum(jnp.max(logits, axis=1, keepdims=True), sink))
            p = jnp.exp(logits - m)
            probs = p * (1.0 / (jnp.sum(p, axis=1, keepdims=True) + jnp.exp(sink - m)))
            p_prev, p_own = jnp.where(own, 0.0, probs), jnp.where(own, probs, 0.0)
            if wide:
                o = bdot(jnp.concatenate([p_prev, p_own], axis=1), vals, 1, 0)
            else:
                o = bdot(p_prev, vals[:t], 1, 0) + bdot(p_own, vals[t:], 1, 0)
            o = jnp.where(lane_half == kv, o, 0.0)
            if half != kv:
                o = lroll(o, SWA_HD)
            acc = acc + o
        slabs.append(acc)
    return jnp.concatenate(slabs, axis=1)


def _swa_in_specs(nc, clamp):
    t = WINDOW
    qb, kb, vb = SEC_B // MIX, (SEC_B + MIX) // LANES, (SEC_B + MIX + LANES) // LANES

    def cur(i):
        return jnp.minimum(i, nc - 1) if clamp else i

    def prev(i):
        return jnp.maximum(cur(i) - 1, 0)

    return [pl.BlockSpec((t, MIX), lambda i: (cur(i), qb)),
            pl.BlockSpec((t, LANES), lambda i: (prev(i), kb)), pl.BlockSpec((t, LANES), lambda i: (cur(i), kb)),
            pl.BlockSpec((t, LANES), lambda i: (prev(i), vb)), pl.BlockSpec((t, LANES), lambda i: (cur(i), vb)),
            pl.BlockSpec((1, LANES), lambda i: (0, 0)),
            pl.BlockSpec((3, t, LANES), lambda i: (0, cur(i), 0)), pl.BlockSpec((3, t, LANES), lambda i: (0, prev(i), 0))]


def swa_fwd(proj, sink_row, tables, *, name):
    s = proj.shape[0]
    nc = s // WINDOW

    def body(q_ref, kp_ref, kc_ref, vp_ref, vc_ref, sink_ref, tq_ref, tp_ref, o_ref):
        prev_ok = pl.program_id(0) > 0
        blocks = [r[...].astype(f32) for r in (q_ref, kp_ref, kc_ref, vp_ref, vc_ref)]
        o_ref[...] = _swa_block(*blocks, sink_ref[...],
                                tq_ref[...], tp_ref[...], prev_ok, True).astype(bf16)

    return pl.pallas_call(
        body, name=name, grid=(nc,), in_specs=_swa_in_specs(nc, False),
        out_specs=pl.BlockSpec((WINDOW, MIX), lambda i: (i, 0)), out_shape=jax.ShapeDtypeStruct((s, MIX), bf16),
        compiler_params=_params(1),
    )(proj, proj, proj, proj, proj, sink_row, tables, tables)


def swa_bwd(proj, sink_row, tables, d_out, dproj, *, side=None, name):
    s = proj.shape[0]
    nc = s // WINDOW
    t = WINDOW

    def body(q_ref, kp_ref, kc_ref, vp_ref, vc_ref, sink_ref, tq_ref, tp_ref, do_ref, _,
             dp_ref, dsink_ref, cq_ref, ck_ref, cv_ref):
        i = pl.program_id(0)

        @pl.when(i == 0)
        def _():
            dsink_ref[...] = jnp.zeros_like(dsink_ref)

        def write(dk_prev, dv_prev):
            dp_ref[:, :MIX] = cq_ref[...].astype(bf16)
            dp_ref[:, MIX:MIX + LANES] = (ck_ref[...] + dk_prev).astype(bf16)
            dp_ref[:, MIX + LANES:] = (cv_ref[...] + dv_prev).astype(bf16)

        @pl.when(i < nc)
        def _():
            fn = functools.partial(_swa_block, table_q=tq_ref[...], table_p=tp_ref[...], prev_ok=i > 0, wide=False)
            blocks = [r[...].astype(f32) for r in (q_ref, kp_ref, kc_ref, vp_ref, vc_ref)]
            _, vjp = jax.vjp(fn, *blocks, sink_ref[...])
            dq, dkp, dkc, dvp, dvc, dsink = vjp(do_ref[...])
            dsink_ref[...] += dsink

            @pl.when(i > 0)
            def _():
                write(dkp, dvp)

            cq_ref[...] = dq
            ck_ref[...] = dkc
            cv_ref[...] = dvc

        @pl.when(i == nc)
        def _():
            write(0.0, 0.0)

    return host_call(
        side, body, name=name, grid=(nc + 1,),
        in_specs=_swa_in_specs(nc, True) + [pl.BlockSpec((t, MIX), lambda i: (jnp.minimum(i, nc - 1), 0)),
                                            pl.BlockSpec(memory_space=pl.ANY)],
        out_specs=[pl.BlockSpec((t, MIX + 2 * LANES), lambda i: (jnp.maximum(i - 1, 0), SEC_B // (MIX + 2 * LANES))),
                   pl.BlockSpec((1, LANES), lambda i: (0, 0))],
        out_shape=[jax.ShapeDtypeStruct(dproj.shape, bf16), jax.ShapeDtypeStruct((1, LANES), f32)],
        scratch_shapes=[pltpu.VMEM((t, MIX), f32), pltpu.VMEM((t, LANES), f32), pltpu.VMEM((t, LANES), f32)],
        aliases={9: 0}, args=(proj, proj, proj, proj, proj, sink_row, tables, tables, d_out, dproj))


CONV_PAD = 16


def _conv_taps(xp, rows):
    off = CONV_PAD - (DN_CONV - 1)
    return [xp[off + i:off + i + rows] for i in range(DN_CONV)]


def _conv_pre(taps, w):
    pre = taps[0] * w[0:1]
    for i in range(1, DN_CONV):
        pre = pre + taps[i] * w[i:i + 1]
    return pre


def conv_fwd(proj, conv_w8, *, name):
    s = proj.shape[0]
    wq = 3 * MIX
    ts = _tile(s, 512)
    nb = ts // CONV_PAD

    def body(x_ref, prev_ref, w_ref, o_ref):
        prev = jnp.where(pl.program_id(0) > 0, prev_ref[...].astype(f32), 0.0)
        pre = _conv_pre(_conv_taps(jnp.concatenate([prev, x_ref[...].astype(f32)], axis=0), ts), w_ref[...])
        o_ref[...] = pre * _sigmoid(pre)

    return pl.pallas_call(
        body, name=name, grid=(s // ts,),
        in_specs=[pl.BlockSpec((ts, wq), lambda i: (i, SEC_QKV // wq)),
                  pl.BlockSpec((CONV_PAD, wq), lambda i: (jnp.maximum(i * nb - 1, 0), SEC_QKV // wq)),
                  pl.BlockSpec((CONV_PAD, wq), lambda i: (0, 0))],
        out_specs=pl.BlockSpec((ts, wq), lambda i: (i, 0)), out_shape=jax.ShapeDtypeStruct((s, wq), f32),
        compiler_params=_params(1),
    )(proj, proj, conv_w8)


def conv_bwd(proj, conv_w8, dxc, dproj, *, side=None, name):
    s = proj.shape[0]
    wq = 3 * MIX
    ts = _tile(s, 512)
    nb = ts // CONV_PAD
    nt = s // ts
    last_blk = s // CONV_PAD - 1

    def body(x_ref, prev_ref, next_ref, w_ref, d_ref, dnext_ref, _, dp_ref, dw_ref):
        i = pl.program_id(0)

        @pl.when(i == 0)
        def _():
            dw_ref[...] = jnp.zeros_like(dw_ref)

        w = w_ref[...]
        prev = jnp.where(i > 0, prev_ref[...].astype(f32), 0.0)
        more = i < nt - 1
        xp = jnp.concatenate([prev, x_ref[...].astype(f32), jnp.where(more, next_ref[...].astype(f32), 0.0)], axis=0)
        taps = _conv_taps(xp, ts + CONV_PAD)
        pre = _conv_pre(taps, w)
        sig = _sigmoid(pre)
        dxc_ext = jnp.concatenate([d_ref[...], jnp.where(more, dnext_ref[...], 0.0)], axis=0)
        dpre = dxc_ext * sig * (1.0 + pre * (1.0 - sig))
        d_raw = jnp.zeros((ts, wq), f32)
        dws = []
        for k in range(DN_CONV):
            shift = DN_CONV - 1 - k
            d_raw = d_raw + dpre[shift:shift + ts] * w[k:k + 1]
            dws.append(jnp.sum(dpre[:ts] * taps[k][:ts], axis=0, keepdims=True))
        dp_ref[...] = d_raw.astype(bf16)
        dw_ref[...] += jnp.concatenate(dws + [jnp.zeros((CONV_PAD - DN_CONV, wq), f32)], axis=0)

    sec = SEC_QKV // wq
    return host_call(
        side, body, name=name, grid=(nt,),
        in_specs=[pl.BlockSpec((ts, wq), lambda i: (i, sec)),
                  pl.BlockSpec((CONV_PAD, wq), lambda i: (jnp.maximum(i * nb - 1, 0), sec)),
                  pl.BlockSpec((CONV_PAD, wq), lambda i: (jnp.minimum((i + 1) * nb, last_blk), sec)),
                  pl.BlockSpec((CONV_PAD, wq), lambda i: (0, 0)),
                  pl.BlockSpec((ts, wq), lambda i: (i, 0)),
                  pl.BlockSpec((CONV_PAD, wq), lambda i: (jnp.minimum((i + 1) * nb, last_blk), 0)),
                  pl.BlockSpec(memory_space=pl.ANY)],
        out_specs=[pl.BlockSpec((ts, wq), lambda i: (i, sec)), pl.BlockSpec((CONV_PAD, wq), lambda i: (0, 0))],
        out_shape=[jax.ShapeDtypeStruct(dproj.shape, bf16), jax.ShapeDtypeStruct((CONV_PAD, wq), f32)],
        scratch_shapes=[], aliases={6: 0}, args=(proj, proj, proj, conv_w8, dxc, dxc, dproj))


def _dn_chunk(state, xc, z, ba, alog_row, dtb_row, norm_row, t_known):
    c, nh = DN_CHUNK, DN_HEADS
    n = c * nh
    row = lax.broadcasted_iota(jnp.int32, (n, n), 0)
    col = lax.broadcasted_iota(jnp.int32, (n, n), 1)
    same_head = (row // c) == (col // c)
    tril, strict = same_head & (row >= col), same_head & (row > col)
    tril_c = lax.broadcasted_iota(jnp.int32, (c, c), 0) >= lax.broadcasted_iota(jnp.int32, (c, c), 1)
    beta_all = _sigmoid(ba)
    g_all = -jnp.exp(alog_row) * jax.nn.softplus(ba + dtb_row)
    gc_all = hdot(tril_c.astype(f32), g_all)
    gc_t = gc_all.T

    def stack(piece):
        return jnp.concatenate([piece(h) for h in range(nh)], axis=0)

    q = stack(lambda h: xc[:, h * DN_HD:(h + 1) * DN_HD])
    k = stack(lambda h: xc[:, MIX + h * DN_HD:MIX + (h + 1) * DN_HD])
    v = stack(lambda h: xc[:, 2 * MIX + h * DN_HD:2 * MIX + (h + 1) * DN_HD])
    zs = stack(lambda h: z[:, h * DN_HD:(h + 1) * DN_HD])
    q = q * lax.rsqrt(jnp.sum(q * q, axis=-1, keepdims=True) + NORM_EPS) * (DN_HD ** -0.5)
    k = k * lax.rsqrt(jnp.sum(k * k, axis=-1, keepdims=True) + NORM_EPS)
    beta = stack(lambda h: _lane_col(beta_all, h))
    g_cols = [_lane_col(gc_all, nh + h) for h in range(nh)]
    g_col = jnp.concatenate(g_cols, axis=0)
    g_row = jnp.concatenate([gc_t[nh + h:nh + h + 1, :] for h in range(nh)], axis=1)
    g_last = stack(lambda h: jnp.broadcast_to(g_cols[h][c - 1:c, :], (c, 1)))
    decay = jnp.where(tril, jnp.exp(jnp.where(tril, g_col - g_row, 0.0)), 0.0)
    kb = k * beta
    low = jnp.where(strict, bdot(kb, k, 1, 1) * decay, 0.0)
    t_inv = tri_inv(low, c) if t_known is None else tri_inv_known(low, t_known)
    e_gc = jnp.exp(g_col)
    uw = bdot(t_inv, jnp.concatenate([v * beta, kb * e_gc], axis=1), 1, 0)
    u, w = uw[:, :DN_HD], uw[:, DN_HD:]
    attn = bdot(q, k, 1, 1) * decay
    own = (lax.broadcasted_iota(jnp.int32, (n, nh * DN_HD), 1) // DN_HD
           == lax.broadcasted_iota(jnp.int32, (n, nh * DN_HD), 0) // c)

    def spread(a):
        return jnp.where(own, jnp.concatenate([a] * nh, axis=1), 0.0)

    v_new = u - bdot(spread(w), state, 1, 0)
    o = bdot(spread(q * e_gc), state, 1, 0) + bdot(attn, v_new, 1, 0)
    keep = stack(lambda h: jnp.broadcast_to(jnp.exp(g_cols[h][c - 1:c, :]), (DN_HD, 1)))
    new_state = state * keep + bdot(spread(k * jnp.exp(g_last - g_col)), v_new, 0, 0)
    out = _rms(o, norm_row) * (zs * _sigmoid(zs))
    return new_state, jnp.concatenate([out[h * c:(h + 1) * c] for h in range(nh)], axis=1), t_inv


DN_STEP = 4 * DN_CHUNK


def _dn_step(state, xc, z, ba, alog_row, dtb_row, norm_row, t_known=None):
    outs, t_invs = [], []
    for c in range(DN_STEP // DN_CHUNK):
        rows = slice(c * DN_CHUNK, (c + 1) * DN_CHUNK)
        state, out, t_inv = _dn_chunk(state, xc[rows], z[rows], ba[rows], alog_row, dtb_row, norm_row,
                                      None if t_known is None else t_known[c])
        outs.append(out)
        t_invs.append(t_inv)
    return state, jnp.concatenate(outs, axis=0), jnp.stack(t_invs)


def _dn_specs(ts, order):
    zb = SEC_Z // MIX
    return [pl.BlockSpec((ts, 3 * MIX), lambda i: (order(i), 0)),
            pl.BlockSpec((ts, MIX), lambda i: (order(i), zb)),
            pl.BlockSpec((ts, LANES), lambda i: (order(i), 0)),
            pl.BlockSpec((1, LANES), lambda i: (0, 0)), pl.BlockSpec((1, LANES), lambda i: (0, 0)),
            pl.BlockSpec((1, LANES), lambda i: (0, 0))]


def dn_fwd(xc, proj, p_ba, alog_row, dtb_row, norm_row, *, side=None, name):
    s = xc.shape[0]
    ts = _tile(s, 512)
    n_step = ts // DN_STEP

    per_step = DN_STEP // DN_CHUNK
    n_tri = DN_HEADS * DN_CHUNK

    def body(xc_ref, z_ref, ba_ref, al_ref, dt_ref, nr_ref, o_ref, st_ref, tri_ref, state_ref):
        @pl.when(pl.program_id(0) == 0)
        def _():
            state_ref[...] = jnp.zeros_like(state_ref)

        def step(c, carry):
            rows = pl.ds(pl.multiple_of(c * DN_STEP, DN_STEP), DN_STEP)
            st_ref[c] = state_ref[...]
            new_state, out, t_invs = _dn_step(state_ref[...], xc_ref[rows, :], z_ref[rows, :].astype(f32), ba_ref[rows, :],
                                              al_ref[...], dt_ref[...], nr_ref[...])
            state_ref[...] = new_state
            o_ref[rows, :] = out.astype(bf16)
            tri_ref[pl.ds(c * per_step, per_step)] = t_invs
            return carry
        lax.fori_loop(0, n_step, step, 0)

    return host_call(
        side, body, name=name, grid=(s // ts,), in_specs=_dn_specs(ts, lambda i: i),
        out_specs=[pl.BlockSpec((ts, MIX), lambda i: (i, 0)),
                   pl.BlockSpec((n_step, DN_HEADS * DN_HD, DN_HD), lambda i: (i, 0, 0)),
                   pl.BlockSpec((n_step * per_step, n_tri, n_tri), lambda i: (i, 0, 0))],
        out_shape=[jax.ShapeDtypeStruct((s, MIX), bf16),
                   jax.ShapeDtypeStruct((s // DN_STEP, DN_HEADS * DN_HD, DN_HD), f32),
                   jax.ShapeDtypeStruct((s // DN_CHUNK, n_tri, n_tri), f32)],
        scratch_shapes=[pltpu.VMEM((DN_HEADS * DN_HD, DN_HD), f32)], aliases={},
        args=(xc, proj, p_ba, alog_row, dtb_row, norm_row))


def dn_bwd(xc, proj, p_ba, alog_row, dtb_row, norm_row, saved, tri, d_out, dproj, *, side=None, name):
    s = xc.shape[0]
    ts = _tile(s, 512)
    n_step = ts // DN_STEP
    nt = s // ts

    per_step = DN_STEP // DN_CHUNK
    n_tri = DN_HEADS * DN_CHUNK

    def body(xc_ref, z_ref, ba_ref, al_ref, dt_ref, nr_ref, st_ref, tri_ref, do_ref, _,
             dz_ref, dxc_ref, dba_ref, dal_ref, ddt_ref, dnr_ref, dstate_ref):
        @pl.when(pl.program_id(0) == 0)
        def _():
            dstate_ref[...] = jnp.zeros_like(dstate_ref)
            dal_ref[...] = jnp.zeros_like(dal_ref)
            ddt_ref[...] = jnp.zeros_like(ddt_ref)
            dnr_ref[...] = jnp.zeros_like(dnr_ref)

        def step(it, carry):
            c = n_step - 1 - it
            rows = pl.ds(pl.multiple_of(c * DN_STEP, DN_STEP), DN_STEP)
            t_known = tri_ref[pl.ds(c * per_step, per_step)]
            _, vjp = jax.vjp(lambda *a: _dn_step(*a, t_known=t_known)[:2], st_ref[c], xc_ref[rows, :],
                             z_ref[rows, :].astype(f32), ba_ref[rows, :], al_ref[...], dt_ref[...], nr_ref[...])
            d_in, dxc, dz, dba, dal, ddt, dnr = vjp((dstate_ref[...], do_ref[rows, :]))
            dstate_ref[...] = d_in
            dxc_ref[rows, :] = dxc
            dz_ref[rows, :] = dz.astype(bf16)
            dba_ref[rows, :] = dba.astype(bf16)
            dal_ref[...] += dal
            ddt_ref[...] += ddt
            dnr_ref[...] += dnr
            return carry
        lax.fori_loop(0, n_step, step, 0)

    def rev(i):
        return nt - 1 - i

    specs = _dn_specs(ts, rev)
    vec = pl.BlockSpec((1, LANES), lambda i: (0, 0))
    return host_call(
        side, body, name=name, grid=(nt,),
        in_specs=specs + [pl.BlockSpec((n_step, DN_HEADS * DN_HD, DN_HD), lambda i: (rev(i), 0, 0)),
                          pl.BlockSpec((n_step * per_step, n_tri, n_tri), lambda i: (rev(i), 0, 0)),
                          pl.BlockSpec((ts, MIX), lambda i: (rev(i), 0)), pl.BlockSpec(memory_space=pl.ANY)],
        out_specs=[specs[1], specs[0], specs[2], vec, vec, vec],
        out_shape=[jax.ShapeDtypeStruct(dproj.shape, bf16), jax.ShapeDtypeStruct((s, 3 * MIX), f32),
                   jax.ShapeDtypeStruct((s, LANES), bf16)] + [jax.ShapeDtypeStruct((1, LANES), f32)] * 3,
        scratch_shapes=[pltpu.VMEM((DN_HEADS * DN_HD, DN_HD), f32)], aliases={9: 0},
        args=(xc, proj, p_ba, alog_row, dtb_row, norm_row, saved, tri, d_out, dproj))


def _merge_in_specs(ts, d):
    row = pl.BlockSpec((ts, MIX), lambda i: (i, 0))
    return [row, row, row, pl.BlockSpec((ts, 3 * d), lambda i: (i, SEC_G // (3 * d))),
            pl.BlockSpec((3, MIX, d), lambda i: (0, 0, 0))]


def merge_fwd(out_a, out_b, out_c, proj, w_branch, *, name):
    s, d = out_a.shape[0], w_branch.shape[2]
    ts = _tile(s, 512, 16)

    def body(a_ref, b_ref, c_ref, g_ref, w_ref, o_ref):
        acc = jnp.zeros((ts, d), f32)
        for n, r in enumerate((a_ref, b_ref, c_ref)):
            acc = acc + _sigmoid(g_ref[:, n * d:(n + 1) * d].astype(f32)) * _dg(r[...], w_ref[n], 1, 0)
        o_ref[...] = acc.astype(bf16)

    return pl.pallas_call(
        body, name=name, grid=(s // ts,), in_specs=_merge_in_specs(ts, d),
        out_specs=pl.BlockSpec((ts, d), lambda i: (i, 0)), out_shape=jax.ShapeDtypeStruct((s, d), bf16),
        compiler_params=_params(1),
    )(out_a, out_b, out_c, proj, w_branch)


def merge_bwd(out_a, out_b, out_c, proj, w_branch, d_merged, dproj, *, name):
    s, d = out_a.shape[0], w_branch.shape[2]
    ts = _tile(s, 512, 16)

    def body(a_ref, b_ref, c_ref, g_ref, w_ref, dm_ref, _, dg_ref, da_ref, db_ref, dc_ref, dw_ref):
        @pl.when(pl.program_id(0) == 0)
        def _():
            dw_ref[...] = jnp.zeros_like(dw_ref)

        dm = dm_ref[...]
        for n, (r, dr) in enumerate(((a_ref, da_ref), (b_ref, db_ref), (c_ref, dc_ref))):
            gate = _sigmoid(g_ref[:, n * d:(n + 1) * d].astype(f32))
            branch = _dg(r[...], w_ref[n], 1, 0)
            dg_ref[:, n * d:(n + 1) * d] = (dm * branch * gate * (1.0 - gate)).astype(bf16)
            d_branch = dm * gate
            dr[...] = _dg(d_branch, w_ref[n], 1, 1)
            dw_ref[n] += _dg(r[...], d_branch, 0, 0)

    specs = _merge_in_specs(ts, d)
    row_f = pl.BlockSpec((ts, MIX), lambda i: (i, 0))
    return pl.pallas_call(
        body, name=name, grid=(s // ts,),
        in_specs=specs + [pl.BlockSpec((ts, d), lambda i: (i, 0)), pl.BlockSpec(memory_space=pl.ANY)],
        out_specs=[specs[3], row_f, row_f, row_f, specs[4]],
        out_shape=[jax.ShapeDtypeStruct(dproj.shape, bf16)] + [jax.ShapeDtypeStruct((s, MIX), f32)] * 3
        + [jax.ShapeDtypeStruct(w_branch.shape, f32)],
        input_output_aliases={6: 0}, compiler_params=_params(1),
    )(out_a, out_b, out_c, proj, w_branch, d_merged, dproj)


def swiglu_fwd(gu, *, name):
    g2, s, w = gu.shape
    ng = g2 // 2
    ts = _tile(s, 1024, 16)

    def body(g_ref, u_ref, o_ref):
        g = g_ref[...].astype(f32)
        o_ref[...] = (g * _sigmoid(g) * u_ref[...].astype(f32)).astype(bf16)

    return pl.pallas_call(
        body, name=name, grid=(s // ts, ng),
        in_specs=[pl.BlockSpec((None, ts, w), lambda i, j: (j, i, 0)),
                  pl.BlockSpec((None, ts, w), lambda i, j: (ng + j, i, 0))],
        out_specs=pl.BlockSpec((None, ts, w), lambda i, j: (j, i, 0)), out_shape=jax.ShapeDtypeStruct((ng, s, w), bf16),
        compiler_params=_params(2),
    )(gu, gu)


def swiglu_bwd(gu, d_act, *, name):
    g2, s, w = gu.shape
    ng = g2 // 2
    ts = _tile(s, 1024, 16)

    def body(g_ref, u_ref, d_ref, dg_ref, du_ref):
        g, d = g_ref[...].astype(f32), d_ref[...].astype(f32)
        sig = _sigmoid(g)
        dg_ref[...] = (d * u_ref[...].astype(f32) * sig * (1.0 + g * (1.0 - sig))).astype(bf16)
        du_ref[...] = (d * g * sig).astype(bf16)

    lo = pl.BlockSpec((None, ts, w), lambda i, j: (j, i, 0))
    return pl.pallas_call(
        body, name=name, grid=(s // ts, ng),
        in_specs=[lo, pl.BlockSpec((None, ts, w), lambda i, j: (ng + j, i, 0)), lo], out_specs=[lo, lo],
        out_shape=[jax.ShapeDtypeStruct((ng, s, w), bf16)] * 2, compiler_params=_params(2),
    )(gu, gu, d_act)


def adamw(w, m, v, g_parts, *, name):
    n_layers = len(g_parts)
    n_parts, r, cols = g_parts[0].shape
    lanes = -(-cols // LANES) * LANES
    tr = _tile(r, max(16, (128 * 1024) // lanes), 16)
    nr = r // tr

    def body(w_ref, m_ref, v_ref, *rest):
        gp_refs, (g_ref, d_ref, nm_ref, nv_ref) = rest[:n_layers], rest[n_layers:]
        layer = pl.program_id(0)
        g = jnp.zeros((tr, cols), f32)
        for l, gp_ref in enumerate(gp_refs):
            g_l = gp_ref[0].astype(f32)
            for k in range(1, n_parts):
                g_l = g_l + gp_ref[k].astype(f32)
            g = jnp.where(layer == l, g_l, g)
        nm = ADAM_B1 * m_ref[...] + (1.0 - ADAM_B1) * g
        nv = ADAM_B2 * v_ref[...] + (1.0 - ADAM_B2) * jnp.square(g)
        m_hat = nm / (1.0 - ADAM_B1 ** ADAM_STEP)
        v_hat = nv / (1.0 - ADAM_B2 ** ADAM_STEP)
        g_ref[...] = g
        d_ref[...] = -ADAM_LR * (m_hat / (jnp.sqrt(v_hat) + ADAM_EPS) + ADAM_WD * w_ref[...])
        nm_ref[...] = nm
        nv_ref[...] = nv

    row = pl.BlockSpec((tr, cols), lambda l, i: (l * nr + i, 0))

    def parts_spec(own):
        return pl.BlockSpec((n_parts, tr, cols),
                            lambda l, i: (0, jnp.where(l == own, i, jnp.where(l < own, 0, nr - 1)), 0))

    return pl.pallas_call(
        body, name=name, grid=(n_layers, nr), in_specs=[row, row, row] + [parts_spec(l) for l in range(n_layers)],
        out_specs=[row] * 4, out_shape=[jax.ShapeDtypeStruct(w.shape, f32)] * 4, compiler_params=_params(2),
    )(w, m, v, *g_parts)


def _mesh_pos():
    return lax.axis_index("x"), lax.axis_index("y"), lax.axis_index("c")


def _dev_index(p):
    return 4 * p[0] + 2 * p[1] + p[2]


class Exchange:
    def __init__(self, kind, arrays):
        self.kind, self.arrays, self.n = kind, list(arrays), len(arrays)
        self.specs = [pl.BlockSpec(memory_space=pl.ANY)] * self.n
        self.out_shapes = [jax.ShapeDtypeStruct(((N_DEV,) if kind == "gather" else ()) + a.shape, a.dtype)
                           for a in self.arrays]
        self.scratch = [pltpu.SemaphoreType.DMA((self.n, N_DEV - 1)), pltpu.SemaphoreType.DMA((self.n, N_DEV - 1)),
                        pltpu.SemaphoreType.DMA((self.n,))]

    def _copies(self, in_refs, out_refs, sems, with_arrivals):
        send_sems, recv_sems, local_sems = sems
        x, y, c = _mesh_pos()
        mine = _dev_index((x, y, c))

        def src(a, slab):
            return in_refs[a] if self.kind == "gather" else in_refs[a].at[slab]

        local = [pltpu.make_async_copy(src(a, mine), out_refs[a].at[mine], local_sems.at[a]) for a in range(self.n)]
        sends, arrivals = [], []
        for k in range(1, N_DEV):
            peer = (1 - x if k & 4 else x, 1 - y if k & 2 else y, 1 - c if k & 1 else c)
            theirs = _dev_index(peer)
            for a in range(self.n):
                to = dict(send_sem=send_sems.at[a, k - 1], recv_sem=recv_sems.at[a, k - 1], device_id=peer,
                          device_id_type=pl.DeviceIdType.MESH)
                sends.append(pltpu.make_async_remote_copy(src_ref=src(a, theirs), dst_ref=out_refs[a].at[mine], **to))
                if with_arrivals:
                    arrivals.append(pltpu.make_async_remote_copy(src_ref=src(a, theirs),
                                                                 dst_ref=out_refs[a].at[theirs], **to))
        return local, sends, arrivals

    def start(self, in_refs, out_refs, sems):
        local, sends, _ = self._copies(in_refs, out_refs, sems, False)
        for cp in local + sends:
            cp.start()

    def wait(self, in_refs, out_refs, sems):
        local, sends, arrivals = self._copies(in_refs, out_refs, sems, True)
        for cp in arrivals:
            cp.wait_recv()
        for cp in sends:
            cp.wait_send()
        for cp in local:
            cp.wait()

    def run_around(self, grid, in_refs, out_refs, sems, *, before):
        at = None
        for axis, size in enumerate(grid):
            hit = pl.program_id(axis) == (0 if before else size - 1)
            at = hit if at is None else at & hit

        @pl.when(at)
        def _():
            (self.start if before else self.wait)(in_refs, out_refs, sems)


def host_call(side, body, *, name, grid, in_specs, out_specs, out_shape, scratch_shapes, args, aliases):
    n_in, n_out = len(in_specs), len(out_specs)
    if side is None:
        kernel_body = body
    else:
        n = side.n
        in_specs, args = in_specs + side.specs, tuple(args) + tuple(side.arrays)
        out_specs, out_shape = out_specs + side.specs, out_shape + side.out_shapes
        scratch_shapes = scratch_shapes + side.scratch

        def kernel_body(*refs):
            ins, side_in = refs[:n_in], refs[n_in:n_in + n]
            outs, side_out = refs[n_in + n:n_in + n + n_out], refs[n_in + n + n_out:n_in + 2 * n + n_out]
            scratch, sems = refs[n_in + 2 * n + n_out:-3], refs[-3:]
            side.run_around(grid, side_in, side_out, sems, before=True)
            body(*ins, *outs, *scratch)
            side.run_around(grid, side_in, side_out, sems, before=False)

    outs = pl.pallas_call(
        kernel_body, name=name, grid=grid, in_specs=in_specs, out_specs=out_specs, out_shape=out_shape,
        scratch_shapes=scratch_shapes, input_output_aliases=aliases, compiler_params=_params(len(grid)),
    )(*args)
    return outs if side is None else (outs[:n_out], outs[n_out:])


def all_gather(blocks, *, name):
    n = len(blocks)
    any_spec = pl.BlockSpec(memory_space=pl.ANY)

    def body(*refs):
        ins, outs = refs[:n], refs[n:2 * n]
        send_sems, recv_sems, local_sems = refs[2 * n:]
        x, y, c = _mesh_pos()
        me, sibling = (x, y, c), (x, y, 1 - c)
        chips = [(1 - x, y), (x, 1 - y), (1 - x, 1 - y)]

        def copy(a, k, block, to, src=None):
            dst = outs[a].at[_dev_index(block)]
            return pltpu.make_async_remote_copy(
                src_ref=dst if src is None else src, dst_ref=dst, send_sem=send_sems.at[a, k],
                recv_sem=recv_sems.at[a, k], device_id=to, device_id_type=pl.DeviceIdType.MESH)

        mine = [pltpu.make_async_copy(ins[a], outs[a].at[_dev_index(me)], local_sems.at[a]) for a in range(n)]
        for cp in mine:
            cp.start()
        first = []
        for a in range(n):
            first.append(copy(a, 0, me, sibling, src=ins[a]))
            first += [copy(a, 1 + j, me, (*chip, c), src=ins[a]) for j, chip in enumerate(chips)]
        for cp in first:
            cp.start()
        passed = []
        for j, chip in enumerate(chips):
            for a in range(n):
                copy(a, 1 + j, (*chip, c), me).wait_recv()
                fwd = copy(a, 4 + j, (*chip, c), sibling)
                fwd.start()
                passed.append(fwd)
        for a in range(n):
            copy(a, 0, sibling, me).wait_recv()
            for j, chip in enumerate(chips):
                copy(a, 4 + j, (*chip, 1 - c), me).wait_recv()
        for cp in first + passed:
            cp.wait_send()
        for cp in mine:
            cp.wait()

    return pl.pallas_call(
        body, name=name, in_specs=[any_spec] * n, out_specs=[any_spec] * n,
        out_shape=[jax.ShapeDtypeStruct((N_DEV,) + b.shape, b.dtype) for b in blocks],
        scratch_shapes=[pltpu.SemaphoreType.DMA((n, 7)), pltpu.SemaphoreType.DMA((n, 7)),
                        pltpu.SemaphoreType.DMA((n,))],
    )(*blocks)


def exchange(jobs, *, name):
    total = sum(j.n for j in jobs)

    def body(*refs):
        ins, outs, sems = refs[:total], refs[total:2 * total], refs[2 * total:]
        pieces, off = [], 0
        for i, j in enumerate(jobs):
            pieces.append((ins[off:off + j.n], outs[off:off + j.n], sems[3 * i:3 * i + 3]))
            off += j.n
        for j, piece in zip(jobs, pieces):
            j.start(*piece)
        for j, piece in zip(jobs, pieces):
            j.wait(*piece)

    outs = pl.pallas_call(
        body, name=name, in_specs=[s for j in jobs for s in j.specs], out_specs=[s for j in jobs for s in j.specs],
        out_shape=[s for j in jobs for s in j.out_shapes], scratch_shapes=[s for j in jobs for s in j.scratch],
    )(*[a for j in jobs for a in j.arrays])
    split, off = [], 0
    for j in jobs:
        split.append(outs[off:off + j.n])
        off += j.n
    return split


def _rows128(arr):
    flat = arr.reshape(-1)
    rows = -(-flat.shape[0] // (8 * LANES)) * 8
    return jnp.pad(flat, (0, rows * LANES - flat.shape[0])).reshape(rows, LANES)


def _pad_lanes(row, width=LANES, at=0):
    return jnp.pad(row, (at, width - at - row.shape[0])).reshape(1, width)


def _w_in_sections(got):
    d = got.shape[1]
    wi = jnp.transpose(got, (1, 0, 2)).reshape(d, -1)
    w_main = jnp.concatenate([wi[:, :C_B], wi[:, C_Z:C_BA], wi[:, C_QKV:C_Z], wi[:, C_G:], wi[:, C_B:C_QKV]], axis=1)
    return w_main, jnp.pad(wi[:, C_BA:C_G], ((0, 0), (0, LANES - (C_G - C_BA))))


def _w_in_parts(gw_main, gw_ba):
    d = gw_main.shape[0]
    full = jnp.concatenate([gw_main[:, SEC_A:SEC_Z], gw_main[:, SEC_B:], gw_main[:, SEC_QKV:SEC_G],
                            gw_main[:, SEC_Z:SEC_QKV], gw_ba[:, :C_G - C_BA], gw_main[:, SEC_G:SEC_B]], axis=1)
    return jnp.transpose(full.reshape(d, N_DEV, -1), (1, 0, 2))


def kernel(x, positions, attn_norm, w_in, sgu_ln_g, sgu_ln_b, sgu_w, sgu_b, attn_sinks, dn_conv_w, dn_a_log, dn_dt_bias, dn_norm, w_branch, w_out, ffn_norm, w_gate_up, w_down, final_norm, loss_target, m_attn_norm, m_w_in, m_sgu_ln_g, m_sgu_ln_b, m_sgu_w, m_sgu_b, m_attn_sinks, m_dn_conv_w, m_dn_a_log, m_dn_dt_bias, m_dn_norm, m_w_branch, m_w_out, m_ffn_norm, m_w_gate_up, m_w_down, m_final_norm, v_attn_norm, v_w_in, v_sgu_ln_g, v_sgu_ln_b, v_sgu_w, v_sgu_b, v_attn_sinks, v_dn_conv_w, v_dn_a_log, v_dn_dt_bias, v_dn_norm, v_w_branch, v_w_out, v_ffn_norm, v_w_gate_up, v_w_down, v_final_norm):
    given = dict(locals())
    depth, d_model = attn_norm.shape
    s = x.shape[1]
    x2 = x.reshape(s, d_model)
    target = loss_target.reshape(s, d_model)
    posf = positions.reshape(s, 1).astype(f32)
    inv_freq = ROPE_THETA ** (-jnp.arange(0, ROPE_DIM, 2, dtype=f32) / ROPE_DIM)
    inv_head = jnp.concatenate([inv_freq, inv_freq, jnp.zeros((SWA_HD - ROPE_DIM,), f32)])
    tables = rope_tables(posf, jnp.tile(inv_head, LANES // SWA_HD).reshape(1, LANES), name="rope_tables")

    assert depth == 2, depth
    gathered = dict(zip([("w_in", 0), ("dn_conv_w", 0), ("dn_conv_w", 1)], all_gather(
        [w_in[0].astype(bf16), dn_conv_w[0], dn_conv_w[1]], name="gather_first")))
    riders = {"l0_in_proj": [("w_branch", 0), ("w_out", 0)],
              "l0_deltanet": [("w_gate_up", 0), ("w_down", 0), ("w_in", 1)],
              "l1_deltanet": [("w_branch", 1), ("w_out", 1), ("w_gate_up", 1), ("w_down", 1)]}

    def gathering(host, call, *args, **kw):
        if host not in riders:
            return call(*args, name=host, **kw)
        side = Exchange("gather", [given[n][l].astype(bf16) for n, l in riders[host]])
        out, got = call(*args, side=side, name=host, **kw)
        gathered.update(zip(riders[host], got))
        return out

    layers, saved = [], []
    h_in = x2
    for l in range(depth):
        t = f"l{l}_"
        w_main, w_ba = _w_in_sections(gathered["w_in", l])
        conv_full = jnp.transpose(gathered["dn_conv_w", l], (1, 0, 2)).reshape(DN_CONV, -1)
        p = dict(
            w_main=w_main, w_ba=w_ba, conv_w8=jnp.pad(conv_full, ((0, CONV_PAD - DN_CONV), (0, 0))),
            attn_norm=attn_norm[l].reshape(1, -1), ffn_norm=ffn_norm[l].reshape(1, -1),
            ln_g=sgu_ln_g[l].reshape(1, -1), ln_b=sgu_ln_b[l].reshape(1, -1), sgu_w=sgu_w[l], sgu_bt=sgu_b[l].T,
            sink_row=_pad_lanes(attn_sinks[l]), alog_row=_pad_lanes(dn_a_log[l], at=DN_HEADS),
            dtb_row=_pad_lanes(dn_dt_bias[l], at=DN_HEADS), norm_row=dn_norm[l].reshape(1, -1))
        layers.append(p)
        h = rmsnorm_fwd(h_in, p["attn_norm"], name=t + "attn_norm")
        proj = gathering(t + "in_proj", matmul, h, p["w_main"], "nn", bf16)
        p_ba = matmul(h, p["w_ba"], "nn", f32, name=t + "in_proj_ba")
        out_a = sgu_fwd(proj, p["ln_g"], p["ln_b"], p["sgu_w"], p["sgu_bt"], name=t + "sgu")
        out_b = swa_fwd(proj, p["sink_row"], tables, name=t + "swa")
        xc = conv_fwd(proj, p["conv_w8"], name=t + "dn_conv")
        out_c, states, tri = gathering(t + "deltanet", dn_fwd, xc, proj, p_ba, p["alog_row"], p["dtb_row"], p["norm_row"])
        p.update(w_branch=jnp.transpose(gathered["w_branch", l], (1, 2, 0, 3)).reshape(3, MIX, d_model),
                 w_out=gathered["w_out", l].reshape(d_model, d_model),
                 w_gu=gathered["w_gate_up", l],
                 w_down=gathered["w_down", l].reshape(N_DEV // 2, -1, d_model))
        merged = merge_fwd(out_a, out_b, out_c, proj, p["w_branch"], name=t + "merge")
        x_mid = matmul(merged, p["w_out"], "nn", f32, residual=h_in, tn=1024, name=t + "out_proj")
        h2 = rmsnorm_fwd(x_mid, p["ffn_norm"], name=t + "ffn_norm")
        gu = matmul(h2, p["w_gu"], "nn", bf16, group="n", name=t + "gate_up")
        act = swiglu_fwd(gu, name=t + "swiglu")
        x_out = matmul(act, p["w_down"], "nn", f32, residual=x_mid, group="k", tn=1024, name=t + "down")
        saved.append(dict(x_in=h_in, h=h, proj=proj, p_ba=p_ba, out_a=out_a, out_b=out_b, out_c=out_c, xc=xc,
                          states=states, tri=tri, merged=merged, x_mid=x_mid, h2=h2, gu=gu, act=act))
        h_in = x_out

    dx, d_final_norm, loss_row = loss_head(h_in, final_norm.reshape(1, -1), target, name="loss_head")
    loss = lax.psum(loss_row[0, 0], MESH_AXES)

    shard_names = ["w_in", "dn_conv_w", "w_branch", "w_out", "w_gate_up", "w_down"]
    rep_names = ["attn_norm", "sgu_ln_g", "sgu_ln_b", "sgu_w", "sgu_b", "attn_sinks", "dn_a_log", "dn_dt_bias",
                 "dn_norm", "ffn_norm"]
    parts, received, per_layer = {}, {}, []
    senders = {"l0_b_swa": [("w_gate_up", 1), ("w_down", 1), ("w_out", 1), ("w_branch", 1)],
               "l0_b_deltanet": [("w_in", 1), ("w_gate_up", 0), ("w_down", 0)],
               "l0_b_dn_conv": [("w_out", 0), ("w_branch", 0)],
               "l0_b_in_proj_dx": [("w_in", 0)]}

    def scattering(host, call, *args, **kw):
        if host not in senders:
            return call(*args, name=host, **kw)
        out, got = call(*args, side=Exchange("scatter", [parts[key] for key in senders[host]]), name=host, **kw)
        received.update(zip(senders[host], got))
        return out

    for l in reversed(range(depth)):
        p, sv, t = layers[l], saved[l], f"l{l}_b_"
        d_act = matmul(dx, p["w_down"], "nt", bf16, group="n", name=t + "down_dx")
        gw_down = matmul(sv["act"], dx, "tn", bf16, group="m", tk=2048, tn=512, name=t + "down_dw")
        d_gate, d_up = swiglu_bwd(sv["gu"], d_act, name=t + "swiglu")
        gw_gu = jnp.concatenate([matmul(sv["h2"], d_half, "tn", bf16, group="n", tk=2048, name=t + "gate_up_dw" + tag)
                                 for d_half, tag in ((d_gate, "_gate"), (d_up, "_up"))], axis=0)
        half = N_DEV // 2
        d_h2 = matmul(d_gate, p["w_gu"][:half], "nt", f32, group="k", tn=1024, name=t + "gate_up_dx_gate")
        d_h2 = matmul(d_up, p["w_gu"][half:], "nt", f32, group="k", tn=1024, residual=d_h2, name=t + "gate_up_dx_up")
        dx_mid, g_ffn = rmsnorm_bwd(sv["x_mid"], p["ffn_norm"], d_h2, dx, name=t + "ffn_norm")
        d_merged = matmul(dx_mid, p["w_out"], "nt", f32, tn=1024, name=t + "out_proj_dx")
        gw_out = matmul(sv["merged"], dx_mid, "tn", bf16, tk=2048, tn=1024, name=t + "out_proj_dw")
        dproj = lax.empty((s, W_MAIN), bf16)
        dproj, d_a, d_b, d_c, gw_branch = merge_bwd(sv["out_a"], sv["out_b"], sv["out_c"], sv["proj"], p["w_branch"],
                                                   d_merged, dproj, name=t + "merge")
        parts.update({("w_gate_up", l): gw_gu, ("w_down", l): gw_down.reshape(N_DEV, -1, d_model),
                      ("w_out", l): gw_out.reshape(N_DEV, -1, d_model),
                      ("w_branch", l): jnp.transpose(gw_branch.reshape(3, MIX, N_DEV, -1), (2, 0, 1, 3)).astype(bf16)})
        dproj, g_ln_g, g_ln_b, g_sgu_w, g_sgu_bt = sgu_bwd(sv["proj"], p["ln_g"], p["ln_b"], p["sgu_w"], p["sgu_bt"],
                                                         d_a, dproj, name=t + "sgu")
        dproj, g_sink = scattering(t + "swa", swa_bwd, sv["proj"], p["sink_row"], tables, d_b, dproj)
        dproj, dxc, dba, g_alog, g_dtb, g_dnorm = scattering(
            t + "deltanet", dn_bwd, sv["xc"], sv["proj"], sv["p_ba"], p["alog_row"], p["dtb_row"], p["norm_row"],
            sv["states"], sv["tri"], d_c, dproj)
        dproj, g_conv8 = scattering(t + "dn_conv", conv_bwd, sv["proj"], p["conv_w8"], dxc, dproj)
        gw_main = matmul(sv["h"], dproj, "tn", bf16, tk=2048, name=t + "in_proj_dw")
        gw_ba = matmul(sv["h"], dba, "tn", bf16, tk=2048, name=t + "in_proj_ba_dw")
        parts["w_in", l] = _w_in_parts(gw_main, gw_ba)
        d_h = scattering(t + "in_proj_dx", matmul, dproj, p["w_main"], "nt", f32, tk=2304, tn=1024)
        d_h = matmul(dba, p["w_ba"], "nt", f32, residual=d_h, name=t + "in_proj_ba_dx")
        dx, g_attn = rmsnorm_bwd(sv["x_in"], p["attn_norm"], d_h, dx_mid, name=t + "attn_norm")
        per_layer.append(dict(
            dn_conv_w=jnp.transpose(g_conv8[:DN_CONV].reshape(DN_CONV, N_DEV, -1), (1, 0, 2)),
            attn_norm=g_attn, sgu_ln_g=g_ln_g, sgu_ln_b=g_ln_b, sgu_w=g_sgu_w, sgu_b=g_sgu_bt.T,
            attn_sinks=g_sink[0, :SWA_HEADS], dn_a_log=g_alog[0, DN_HEADS:2 * DN_HEADS],
            dn_dt_bias=g_dtb[0, DN_HEADS:2 * DN_HEADS], dn_norm=g_dnorm, ffn_norm=g_ffn))
    per_layer.reverse()

    conv_parts = jnp.concatenate([pp["dn_conv_w"] for pp in per_layer], axis=1)
    rep_grads = {n: jnp.stack([pp[n].reshape(given[n].shape[1:]) for pp in per_layer]) for n in rep_names}
    rep_grads["final_norm"] = d_final_norm[0]
    rep_names = rep_names + ["final_norm"]
    rep_rows = [_rows128(given[n]).shape[0] for n in rep_names]
    pad_rows = -sum(rep_rows) % 16

    def small_rows(values):
        pieces = [_rows128(values[n]) for n in rep_names]
        return jnp.concatenate(pieces + [jnp.zeros((pad_rows, LANES), f32)], axis=0)

    (got_conv,), (small_all,) = exchange(
        [Exchange("scatter", [conv_parts]), Exchange("gather", [small_rows(rep_grads)])], name="exchange_last")

    results = [{}, {}, {}, {}]
    for n in shard_names:
        shp = given[n].shape
        two = (-1, shp[-1])
        by_layer = [got_conv] if n == "dn_conv_w" else [received[n, l].reshape(N_DEV, -1, shp[-1]) for l in range(depth)]
        outs = adamw(given[n].reshape(two), given["m_" + n].reshape(two), given["v_" + n].reshape(two), by_layer,
                     name="adamw_" + n)
        for res, val in zip(results, outs):
            res[n] = val.reshape(shp)
    outs = adamw(small_rows(given), small_rows({n: given["m_" + n] for n in rep_names}),
                 small_rows({n: given["v_" + n] for n in rep_names}), [small_all], name="adamw_replicated")
    for res, val in zip(results, outs):
        row = 0
        for n, nr in zip(rep_names, rep_rows):
            res[n] = val[row:row + nr].reshape(-1)[:given[n].size].reshape(given[n].shape)
            row += nr
    order = ["attn_norm", "w_in", "sgu_ln_g", "sgu_ln_b", "sgu_w", "sgu_b", "attn_sinks", "dn_conv_w", "dn_a_log",
             "dn_dt_bias", "dn_norm", "w_branch", "w_out", "ffn_norm", "w_gate_up", "w_down", "final_norm"]
    return (loss, dx.reshape(x.shape), *[res[n] for res in results for n in order])
```

```python
import functools

import jax
import jax.numpy as jnp
from jax import lax
from jax.experimental import pallas as pl
from jax.experimental.pallas import tpu as pltpu

f32 = jnp.float32
bf16 = jnp.bfloat16

N_DEV = 8
MESH_AXES = ("x", "y", "c")
NORM_EPS = 1e-6
MIX = 512
SGU_GROUPS, SGU_CHUNK = 4, 128
SWA_HEADS, SWA_KV, SWA_HD, WINDOW = 8, 2, 64, 128
ROPE_THETA, ROPE_DIM = 500000.0, 16
DN_HEADS, DN_HD, DN_CONV, DN_CHUNK = 4, 128, 4, 64
ADAM_LR, ADAM_B1, ADAM_B2, ADAM_EPS, ADAM_WD, ADAM_STEP = 0.001, 0.9, 0.999, 1e-08, 0.01, 10

LANES = 128
VMEM_LIMIT = 56 * 1024 * 1024

SEC_A, SEC_Z, SEC_QKV, SEC_G, SEC_B = 0, 1024, 1536, 3072, 6144
W_MAIN = 6912
C_B, C_QKV, C_Z, C_BA, C_G = 1024, 1792, 3328, 3840, 3848


def _params(n_axes, **kw):
    return pltpu.CompilerParams(dimension_semantics=("arbitrary",) * n_axes, vmem_limit_bytes=VMEM_LIMIT, **kw)


def _tile(n, target, mult=LANES):
    if n <= target:
        return n
    best = None
    for t in range(mult, target + 1, mult):
        if n % t == 0:
            best = t
    assert best is not None, (n, target, mult)
    return best


def _dg(a, b, ca, cb):
    return lax.dot_general(a.astype(bf16), b.astype(bf16), (((ca,), (cb,)), ((), ())), preferred_element_type=f32)


def _dg3(a, b, ca, cb):
    a_hi, b_hi = a.astype(bf16), b.astype(bf16)
    a_lo, b_lo = (a - a_hi.astype(f32)).astype(bf16), (b - b_hi.astype(f32)).astype(bf16)

    def dot(p, q):
        return lax.dot_general(p, q, (((ca,), (cb,)), ((), ())), preferred_element_type=f32)

    return dot(a_hi, b_hi) + (dot(a_hi, b_lo) + dot(a_lo, b_hi))


def _differentiable_dot(core):
    @functools.partial(jax.custom_vjp, nondiff_argnums=(2, 3))
    def dot(a, b, ca, cb):
        return core(a, b, ca, cb)

    def fwd(a, b, ca, cb):
        return core(a, b, ca, cb), (a, b)

    def bwd(ca, cb, res, ct):
        a, b = res
        da = core(ct, b, 1, 1 - cb) if ca == 1 else core(b, ct, 1 - cb, 1)
        db = core(a, ct, 1 - ca, 0) if cb == 0 else core(ct, a, 0, 1 - ca)
        return da, db

    dot.defvjp(fwd, bwd)
    return dot


bdot = _differentiable_dot(_dg)
_hdot = _differentiable_dot(_dg3)


def hdot(a, b, ca=1, cb=0):
    return _hdot(a, b, ca, cb)


@functools.partial(jax.custom_vjp, nondiff_argnums=(1,))
def lroll(x, shift):
    return pltpu.roll(x, shift, 1)


def _lroll_fwd(x, shift):
    return pltpu.roll(x, shift, 1), None


def _lroll_bwd(shift, _, ct):
    return (pltpu.roll(ct, ct.shape[1] - shift, 1),)


lroll.defvjp(_lroll_fwd, _lroll_bwd)


@functools.partial(jax.custom_vjp, nondiff_argnums=(1,))
def tri_inv(low, nil):
    n = low.shape[0]
    row = lax.broadcasted_iota(jnp.int32, (n, n), 0)
    col = lax.broadcasted_iota(jnp.int32, (n, n), 1)
    eye = (row == col).astype(f32)
    m = -low
    p = eye + m
    span = 2
    while span < nil:
        dot = hdot if span == 2 else (lambda a, b: bdot(a, b, 1, 0))
        m = dot(m, m)
        p = p + dot(p, m)
        span *= 2
    return p


def _tri_inv_fwd(low, nil):
    t = tri_inv(low, nil)
    return t, t


def _tri_inv_bwd(nil, t, dt):
    return (-hdot(t, hdot(dt, t, 1, 1), 0, 0),)


tri_inv.defvjp(_tri_inv_fwd, _tri_inv_bwd)


@jax.custom_vjp
def tri_inv_known(low, t):
    return t


def _tri_inv_known_fwd(low, t):
    return t, t


def _tri_inv_known_bwd(t, dt):
    return _tri_inv_bwd(None, t, dt) + (jnp.zeros_like(t),)


tri_inv_known.defvjp(_tri_inv_known_fwd, _tri_inv_known_bwd)


def _sigmoid(x):
    return 1.0 / (1.0 + jnp.exp(-x))


def _rms(x, g):
    return x * lax.rsqrt(jnp.mean(x * x, axis=-1, keepdims=True) + NORM_EPS) * g


def _lane_col(x, lane_idx):
    lane = lax.broadcasted_iota(jnp.int32, x.shape, 1)
    return jnp.sum(jnp.where(lane == lane_idx, x, 0.0), axis=1, keepdims=True)


def matmul(a, b, mode, out_dtype, *, residual=None, group=None, side=None, tm=1024, tn=768, tk=1024, name):
    dims = {"a": ("m", "k") if mode != "tn" else ("k", "m"),
            "b": {"nn": ("k", "n"), "nt": ("n", "k"), "tn": ("k", "n")}[mode], "o": ("m", "n")}
    full, groups = {}, 1
    for arr, key in ((a, "a"), (b, "b")):
        grouped = group in dims[key]
        if grouped:
            groups = arr.shape[0]
        full[dims[key][0]], full[dims[key][1]] = arr.shape[1:] if grouped else arr.shape
    want = {"m": tm, "n": tn, "k": tk}
    per_step = min(groups, 4) if group == "k" else 1
    tiles = {d: full[d] if d == group else _tile(full[d], want[d]) for d in "mnk"}
    steps = {d: groups // per_step if d == group else full[d] // tiles[d] for d in "mnk"}

    def spec(key):
        d0, d1 = dims[key]

        def index(i, j, kk):
            at = {"m": i, "n": j, "k": kk}
            if group in (d0, d1):
                return (at[group], 0 if d0 == group else at[d0], 0 if d1 == group else at[d1])
            return (at[d0], at[d1])

        block = (tiles[d0], tiles[d1])
        if group in (d0, d1):
            block = ((per_step if group == "k" else None),) + block
        return pl.BlockSpec(block, index)

    ca, cb = {"nn": (1, 0), "nt": (1, 1), "tn": (0, 0)}[mode]
    nk = steps["k"]
    o_spec = spec("o")
    out_shape = (groups, full["m"], full["n"]) if group in ("m", "n") else (full["m"], full["n"])
    has_res = residual is not None

    def product(a_ref, b_ref):
        if group != "k":
            return _dg(a_ref[...], b_ref[...], ca, cb)
        total = _dg(a_ref[0], b_ref[0], ca, cb)
        for g in range(1, per_step):
            total = total + _dg(a_ref[g], b_ref[g], ca, cb)
        return total

    def body(a_ref, b_ref, *rest):
        r_ref = rest[0] if has_res else None
        o_ref = rest[1 if has_res else 0]

        def emit(acc):
            o_ref[...] = (acc + r_ref[...] if has_res else acc).astype(out_dtype)

        if nk == 1:
            emit(product(a_ref, b_ref))
            return
        acc_ref = rest[-1]
        kk = pl.program_id(2)

        @pl.when(kk == 0)
        def _():
            acc_ref[...] = jnp.zeros_like(acc_ref)

        acc_ref[...] += product(a_ref, b_ref)

        @pl.when(kk == nk - 1)
        def _():
            emit(acc_ref[...])

    res = host_call(
        side, body, name=name, grid=(steps["m"], steps["n"], nk),
        in_specs=[spec("a"), spec("b")] + ([o_spec] if has_res else []), out_specs=[o_spec],
        out_shape=[jax.ShapeDtypeStruct(out_shape, out_dtype)],
        scratch_shapes=[pltpu.VMEM((tiles["m"], tiles["n"]), f32)] if nk > 1 else [], aliases={},
        args=(a, b) + ((residual,) if has_res else ()))
    return res[0] if side is None else (res[0][0], res[1])


def rmsnorm_fwd(x, g_row, *, name):
    s, d = x.shape
    ts = _tile(s, 512, 16)

    def body(x_ref, g_ref, o_ref):
        o_ref[...] = _rms(x_ref[...], g_ref[...]).astype(bf16)

    return pl.pallas_call(
        body, name=name, grid=(s // ts,),
        in_specs=[pl.BlockSpec((ts, d), lambda i: (i, 0)), pl.BlockSpec((1, d), lambda i: (0, 0))],
        out_specs=pl.BlockSpec((ts, d), lambda i: (i, 0)), out_shape=jax.ShapeDtypeStruct((s, d), bf16),
        compiler_params=_params(1),
    )(x, g_row)


def rmsnorm_bwd(x, g_row, dh, dres, *, name):
    s, d = x.shape
    ts = _tile(s, 512, 16)

    def body(x_ref, g_ref, dh_ref, dres_ref, dx_ref, dg_ref):
        @pl.when(pl.program_id(0) == 0)
        def _():
            dg_ref[...] = jnp.zeros_like(dg_ref)

        _, vjp = jax.vjp(_rms, x_ref[...], g_ref[...])
        dx, dg = vjp(dh_ref[...])
        dx_ref[...] = dx + dres_ref[...]
        dg_ref[...] += dg

    row = pl.BlockSpec((ts, d), lambda i: (i, 0))
    vec = pl.BlockSpec((1, d), lambda i: (0, 0))
    return pl.pallas_call(
        body, name=name, grid=(s // ts,), in_specs=[row, vec, row, row], out_specs=[row, vec],
        out_shape=[jax.ShapeDtypeStruct((s, d), f32), jax.ShapeDtypeStruct((1, d), f32)],
        compiler_params=_params(1),
    )(x, g_row, dh, dres)


def loss_head(x, g_row, target, *, name):
    s, d = x.shape
    ts = _tile(s, 512, 16)

    def body(x_ref, g_ref, t_ref, dx_ref, dg_ref, loss_ref):
        @pl.when(pl.program_id(0) == 0)
        def _():
            dg_ref[...] = jnp.zeros_like(dg_ref)
            loss_ref[...] = jnp.zeros_like(loss_ref)

        y, vjp = jax.vjp(_rms, x_ref[...], g_ref[...])
        err = y - t_ref[...]
        dx, dg = vjp(err * (1.0 / d))
        dx_ref[...] = dx
        dg_ref[...] += dg
        loss_ref[...] += 0.5 * jnp.sum(jnp.sum(err * err, axis=1, keepdims=True) * (1.0 / d), axis=0, keepdims=True)

    row = pl.BlockSpec((ts, d), lambda i: (i, 0))
    vec = pl.BlockSpec((1, d), lambda i: (0, 0))
    one = pl.BlockSpec((1, LANES), lambda i: (0, 0))
    return pl.pallas_call(
        body, name=name, grid=(s // ts,), in_specs=[row, vec, row], out_specs=[row, vec, one],
        out_shape=[jax.ShapeDtypeStruct((s, d), f32), jax.ShapeDtypeStruct((1, d), f32),
                   jax.ShapeDtypeStruct((1, LANES), f32)],
        compiler_params=_params(1),
    )(x, g_row, target)


def _sgu_chunk(p_a, ln_g, ln_b, w, b_t):
    t = SGU_CHUNK
    u = jax.nn.gelu(p_a[:, :MIX])
    v = jax.nn.gelu(p_a[:, MIX:])
    vc = v - jnp.mean(v, axis=-1, keepdims=True)
    vn = vc * lax.rsqrt(jnp.mean(vc * vc, axis=-1, keepdims=True) + NORM_EPS) * ln_g + ln_b
    causal = lax.broadcasted_iota(jnp.int32, (t, t), 0) >= lax.broadcasted_iota(jnp.int32, (t, t), 1)
    outs = []
    for g in range(SGU_GROUPS):
        sl = slice(g * LANES, (g + 1) * LANES)
        mixed = bdot(jnp.where(causal, w[g], 0.0), vn[:, sl], 1, 0) + b_t[:, g:g + 1]
        outs.append(u[:, sl] * mixed)
    return jnp.concatenate(outs, axis=1)


def _sgu_specs(s, ts):
    return [pl.BlockSpec((ts, 2 * MIX), lambda i: (i, SEC_A // (2 * MIX))),
            pl.BlockSpec((1, MIX), lambda i: (0, 0)), pl.BlockSpec((1, MIX), lambda i: (0, 0)),
            pl.BlockSpec((SGU_GROUPS, SGU_CHUNK, SGU_CHUNK), lambda i: (0, 0, 0)),
            pl.BlockSpec((SGU_CHUNK, SGU_GROUPS), lambda i: (0, 0))]


def sgu_fwd(proj, ln_g, ln_b, w, b_t, *, name):
    s = proj.shape[0]
    ts = _tile(s, 512)
    n_chunk = ts // SGU_CHUNK

    def body(p_ref, g_ref, b_ref, w_ref, bt_ref, o_ref):
        def step(c, carry):
            rows = pl.ds(pl.multiple_of(c * SGU_CHUNK, SGU_CHUNK), SGU_CHUNK)
            o_ref[rows, :] = _sgu_chunk(p_ref[rows, :].astype(f32), g_ref[...], b_ref[...], w_ref[...], bt_ref[...]).astype(bf16)
            return carry
        lax.fori_loop(0, n_chunk, step, 0)

    return pl.pallas_call(
        body, name=name, grid=(s // ts,), in_specs=_sgu_specs(s, ts),
        out_specs=pl.BlockSpec((ts, MIX), lambda i: (i, 0)), out_shape=jax.ShapeDtypeStruct((s, MIX), bf16),
        compiler_params=_params(1),
    )(proj, ln_g, ln_b, w, b_t)


def sgu_bwd(proj, ln_g, ln_b, w, b_t, d_out, dproj, *, name):
    s = proj.shape[0]
    ts = _tile(s, 512)
    n_chunk = ts // SGU_CHUNK

    def body(p_ref, g_ref, b_ref, w_ref, bt_ref, do_ref, _, dp_ref, dg_ref, db_ref, dw_ref, dbt_ref):
        @pl.when(pl.program_id(0) == 0)
        def _():
            dg_ref[...] = jnp.zeros_like(dg_ref)
            db_ref[...] = jnp.zeros_like(db_ref)
            dw_ref[...] = jnp.zeros_like(dw_ref)
            dbt_ref[...] = jnp.zeros_like(dbt_ref)

        def step(c, carry):
            rows = pl.ds(pl.multiple_of(c * SGU_CHUNK, SGU_CHUNK), SGU_CHUNK)
            _, vjp = jax.vjp(_sgu_chunk, p_ref[rows, :].astype(f32), g_ref[...], b_ref[...], w_ref[...], bt_ref[...])
            dp, dg, db, dw, dbt = vjp(do_ref[rows, :])
            dp_ref[rows, :] = dp.astype(bf16)
            dg_ref[...] += dg
            db_ref[...] += db
            dw_ref[...] += dw
            dbt_ref[...] += dbt
            return carry
        lax.fori_loop(0, n_chunk, step, 0)

    specs = _sgu_specs(s, ts)
    return pl.pallas_call(
        body, name=name, grid=(s // ts,),
        in_specs=specs + [pl.BlockSpec((ts, MIX), lambda i: (i, 0)), pl.BlockSpec(memory_space=pl.ANY)],
        out_specs=[specs[0], specs[1], specs[2], specs[3], specs[4]],
        out_shape=[jax.ShapeDtypeStruct(dproj.shape, bf16), jax.ShapeDtypeStruct((1, MIX), f32),
                   jax.ShapeDtypeStruct((1, MIX), f32), jax.ShapeDtypeStruct(w.shape, f32),
                   jax.ShapeDtypeStruct(b_t.shape, f32)],
        input_output_aliases={6: 0}, compiler_params=_params(1),
    )(proj, ln_g, ln_b, w, b_t, d_out, dproj)


def rope_tables(posf, inv_freq, *, name):
    s = posf.shape[0]
    ts = _tile(s, 1024, 8)
    half = ROPE_DIM // 2

    def body(pos_ref, inv_ref, o_ref):
        d = lax.broadcasted_iota(jnp.int32, (1, LANES), 1) % SWA_HD
        ang = pos_ref[...] * inv_ref[...]
        sin = jnp.sin(ang)
        o_ref[0] = jnp.cos(ang)
        o_ref[1] = jnp.where(d < half, sin, 0.0)
        o_ref[2] = jnp.where((d >= half) & (d < ROPE_DIM), sin, 0.0)

    return pl.pallas_call(
        body, name=name, grid=(s // ts,),
        in_specs=[pl.BlockSpec((ts, 1), lambda i: (i, 0)), pl.BlockSpec((1, LANES), lambda i: (0, 0))],
        out_specs=pl.BlockSpec((3, ts, LANES), lambda i: (0, i, 0)), out_shape=jax.ShapeDtypeStruct((3, s, LANES), f32),
        compiler_params=_params(1),
    )(posf, inv_freq)


def _rope(x, table):
    w = x.shape[1]
    half = ROPE_DIM // 2
    c, lo, hi = (jnp.concatenate([table[i]] * (w // LANES), axis=1) for i in range(3))
    return x * c - lroll(x, w - half) * lo + lroll(x, half) * hi


def _swa_block(q, kp, kc, vp, vc, sink_row, table_q, table_p, prev_ok, wide):
    t = WINDOW
    q = _rope(q, table_q) * (SWA_HD ** -0.5)
    keys = jnp.concatenate([_rope(kp, table_p), _rope(kc, table_q)], axis=0)
    vals = jnp.concatenate([vp, vc], axis=0)
    own = lax.broadcasted_iota(jnp.int32, (t, t), 0) >= lax.broadcasted_iota(jnp.int32, (t, t), 1)
    lane_half = lax.broadcasted_iota(jnp.int32, (t, LANES), 1) // SWA_HD
    group = SWA_HEADS // SWA_KV
    slabs = []
    for pair in range(SWA_HEADS // 2):
        q_pair = q[:, pair * LANES:(pair + 1) * LANES]
        acc = jnp.zeros((t, LANES), f32)
        for half in range(2):
            h = 2 * pair + half
            kv = h // group
            qm = jnp.where(lane_half == half, q_pair, 0.0)
            if half != kv:
                qm = lroll(qm, SWA_HD)
            if wide:
                both = bdot(qm, keys, 1, 1)
                s_prev, s_own = both[:, :t], both[:, t:]
            else:
                s_prev, s_own = bdot(qm, keys[:t], 1, 1), bdot(qm, keys[t:], 1, 1)
            logits = jnp.where(own, s_own, jnp.where(prev_ok, s_prev, -1e30))
            sink = _lane_col(sink_row, h)
            m = lax.stop_gradient(jnp.maximum(jnp.max(logits, axis=1, keepdims=True), sink))
            p = jnp.exp(logits - m)
            probs = p * (1.0 / (jnp.sum(p, axis=1, keepdims=True) + jnp.exp(sink - m)))
            p_prev, p_own = jnp.where(own, 0.0, probs), jnp.where(own, probs, 0.0)
            if wide:
                o = bdot(jnp.concatenate([p_prev, p_own], axis=1), vals, 1, 0)
            else:
                o = bdot(p_prev, vals[:t], 1, 0) + bdot(p_own, vals[t:], 1, 0)
            o = jnp.where(lane_half == kv, o, 0.0)
            if half != kv:
                o = lroll(o, SWA_HD)
            acc = acc + o
        slabs.append(acc)
    return jnp.concatenate(slabs, axis=1)


def _swa_in_specs(nc, clamp):
    t = WINDOW
    qb, kb, vb = SEC_B // MIX, (SEC_B + MIX) // LANES, (SEC_B + MIX + LANES) // LANES

    def cur(i):
        return jnp.minimum(i, nc - 1) if clamp else i

    def prev(i):
        return jnp.maximum(cur(i) - 1, 0)

    return [pl.BlockSpec((t, MIX), lambda i: (cur(i), qb)),
            pl.BlockSpec((t, LANES), lambda i: (prev(i), kb)), pl.BlockSpec((t, LANES), lambda i: (cur(i), kb)),
            pl.BlockSpec((t, LANES), lambda i: (prev(i), vb)), pl.BlockSpec((t, LANES), lambda i: (cur(i), vb)),
            pl.BlockSpec((1, LANES), lambda i: (0, 0)),
            pl.BlockSpec((3, t, LANES), lambda i: (0, cur(i), 0)), pl.BlockSpec((3, t, LANES), lambda i: (0, prev(i), 0))]


def swa_fwd(proj, sink_row, tables, *, name):
    s = proj.shape[0]
    nc = s // WINDOW

    def body(q_ref, kp_ref, kc_ref, vp_ref, vc_ref, sink_ref, tq_ref, tp_ref, o_ref):
        prev_ok = pl.program_id(0) > 0
        blocks = [r[...].astype(f32) for r in (q_ref, kp_ref, kc_ref, vp_ref, vc_ref)]
        o_ref[...] = _swa_block(*blocks, sink_ref[...],
                                tq_ref[...], tp_ref[...], prev_ok, True).astype(bf16)

    return pl.pallas_call(
        body, name=name, grid=(nc,), in_specs=_swa_in_specs(nc, False),
        out_specs=pl.BlockSpec((WINDOW, MIX), lambda i: (i, 0)), out_shape=jax.ShapeDtypeStruct((s, MIX), bf16),
        compiler_params=_params(1),
    )(proj, proj, proj, proj, proj, sink_row, tables, tables)


def swa_bwd(proj, sink_row, tables, d_out, dproj, *, side=None, name):
    s = proj.shape[0]
    nc = s // WINDOW
    t = WINDOW

    def body(q_ref, kp_ref, kc_ref, vp_ref, vc_ref, sink_ref, tq_ref, tp_ref, do_ref, _,
             dp_ref, dsink_ref, cq_ref, ck_ref, cv_ref):
        i = pl.program_id(0)

        @pl.when(i == 0)
        def _():
            dsink_ref[...] = jnp.zeros_like(dsink_ref)

        def write(dk_prev, dv_prev):
            dp_ref[:, :MIX] = cq_ref[...].astype(bf16)
            dp_ref[:, MIX:MIX + LANES] = (ck_ref[...] + dk_prev).astype(bf16)
            dp_ref[:, MIX + LANES:] = (cv_ref[...] + dv_prev).astype(bf16)

        @pl.when(i < nc)
        def _():
            fn = functools.partial(_swa_block, table_q=tq_ref[...], table_p=tp_ref[...], prev_ok=i > 0, wide=False)
            blocks = [r[...].astype(f32) for r in (q_ref, kp_ref, kc_ref, vp_ref, vc_ref)]
            _, vjp = jax.vjp(fn, *blocks, sink_ref[...])
            dq, dkp, dkc, dvp, dvc, dsink = vjp(do_ref[...])
            dsink_ref[...] += dsink

            @pl.when(i > 0)
            def _():
                write(dkp, dvp)

            cq_ref[...] = dq
            ck_ref[...] = dkc
            cv_ref[...] = dvc

        @pl.when(i == nc)
        def _():
            write(0.0, 0.0)

    return host_call(
        side, body, name=name, grid=(nc + 1,),
        in_specs=_swa_in_specs(nc, True) + [pl.BlockSpec((t, MIX), lambda i: (jnp.minimum(i, nc - 1), 0)),
                                            pl.BlockSpec(memory_space=pl.ANY)],
        out_specs=[pl.BlockSpec((t, MIX + 2 * LANES), lambda i: (jnp.maximum(i - 1, 0), SEC_B // (MIX + 2 * LANES))),
                   pl.BlockSpec((1, LANES), lambda i: (0, 0))],
        out_shape=[jax.ShapeDtypeStruct(dproj.shape, bf16), jax.ShapeDtypeStruct((1, LANES), f32)],
        scratch_shapes=[pltpu.VMEM((t, MIX), f32), pltpu.VMEM((t, LANES), f32), pltpu.VMEM((t, LANES), f32)],
        aliases={9: 0}, args=(proj, proj, proj, proj, proj, sink_row, tables, tables, d_out, dproj))


CONV_PAD = 16


def _conv_taps(xp, rows):
    off = CONV_PAD - (DN_CONV - 1)
    return [xp[off + i:off + i + rows] for i in range(DN_CONV)]


def _conv_pre(taps, w):
    pre = taps[0] * w[0:1]
    for i in range(1, DN_CONV):
        pre = pre + taps[i] * w[i:i + 1]
    return pre


def conv_fwd(proj, conv_w8, *, name):
    s = proj.shape[0]
    wq = 3 * MIX
    ts = _tile(s, 512)
    nb = ts // CONV_PAD

    def body(x_ref, prev_ref, w_ref, o_ref):
        prev = jnp.where(pl.program_id(0) > 0, prev_ref[...].astype(f32), 0.0)
        pre = _conv_pre(_conv_taps(jnp.concatenate([prev, x_ref[...].astype(f32)], axis=0), ts), w_ref[...])
        o_ref[...] = pre * _sigmoid(pre)

    return pl.pallas_call(
        body, name=name, grid=(s // ts,),
        in_specs=[pl.BlockSpec((ts, wq), lambda i: (i, SEC_QKV // wq)),
                  pl.BlockSpec((CONV_PAD, wq), lambda i: (jnp.maximum(i * nb - 1, 0), SEC_QKV // wq)),
                  pl.BlockSpec((CONV_PAD, wq), lambda i: (0, 0))],
        out_specs=pl.BlockSpec((ts, wq), lambda i: (i, 0)), out_shape=jax.ShapeDtypeStruct((s, wq), f32),
        compiler_params=_params(1),
    )(proj, proj, conv_w8)


def conv_bwd(proj, conv_w8, dxc, dproj, *, side=None, name):
    s = proj.shape[0]
    wq = 3 * MIX
    ts = _tile(s, 512)
    nb = ts // CONV_PAD
    nt = s // ts
    last_blk = s // CONV_PAD - 1

    def body(x_ref, prev_ref, next_ref, w_ref, d_ref, dnext_ref, _, dp_ref, dw_ref):
        i = pl.program_id(0)

        @pl.when(i == 0)
        def _():
            dw_ref[...] = jnp.zeros_like(dw_ref)

        w = w_ref[...]
        prev = jnp.where(i > 0, prev_ref[...].astype(f32), 0.0)
        more = i < nt - 1
        xp = jnp.concatenate([prev, x_ref[...].astype(f32), jnp.where(more, next_ref[...].astype(f32), 0.0)], axis=0)
        taps = _conv_taps(xp, ts + CONV_PAD)
        pre = _conv_pre(taps, w)
        sig = _sigmoid(pre)
        dxc_ext = jnp.concatenate([d_ref[...], jnp.where(more, dnext_ref[...], 0.0)], axis=0)
        dpre = dxc_ext * sig * (1.0 + pre * (1.0 - sig))
        d_raw = jnp.zeros((ts, wq), f32)
        dws = []
        for k in range(DN_CONV):
            shift = DN_CONV - 1 - k
            d_raw = d_raw + dpre[shift:shift + ts] * w[k:k + 1]
            dws.append(jnp.sum(dpre[:ts] * taps[k][:ts], axis=0, keepdims=True))
        dp_ref[...] = d_raw.astype(bf16)
        dw_ref[...] += jnp.concatenate(dws + [jnp.zeros((CONV_PAD - DN_CONV, wq), f32)], axis=0)

    sec = SEC_QKV // wq
    return host_call(
        side, body, name=name, grid=(nt,),
        in_specs=[pl.BlockSpec((ts, wq), lambda i: (i, sec)),
                  pl.BlockSpec((CONV_PAD, wq), lambda i: (jnp.maximum(i * nb - 1, 0), sec)),
                  pl.BlockSpec((CONV_PAD, wq), lambda i: (jnp.minimum((i + 1) * nb, last_blk), sec)),
                  pl.BlockSpec((CONV_PAD, wq), lambda i: (0, 0)),
                  pl.BlockSpec((ts, wq), lambda i: (i, 0)),
                  pl.BlockSpec((CONV_PAD, wq), lambda i: (jnp.minimum((i + 1) * nb, last_blk), 0)),
                  pl.BlockSpec(memory_space=pl.ANY)],
        out_specs=[pl.BlockSpec((ts, wq), lambda i: (i, sec)), pl.BlockSpec((CONV_PAD, wq), lambda i: (0, 0))],
        out_shape=[jax.ShapeDtypeStruct(dproj.shape, bf16), jax.ShapeDtypeStruct((CONV_PAD, wq), f32)],
        scratch_shapes=[], aliases={6: 0}, args=(proj, proj, proj, conv_w8, dxc, dxc, dproj))


def _dn_chunk(state, xc, z, ba, alog_row, dtb_row, norm_row, t_known):
    c, nh = DN_CHUNK, DN_HEADS
    n = c * nh
    row = lax.broadcasted_iota(jnp.int32, (n, n), 0)
    col = lax.broadcasted_iota(jnp.int32, (n, n), 1)
    same_head = (row // c) == (col // c)
    tril, strict = same_head & (row >= col), same_head & (row > col)
    tril_c = lax.broadcasted_iota(jnp.int32, (c, c), 0) >= lax.broadcasted_iota(jnp.int32, (c, c), 1)
    beta_all = _sigmoid(ba)
    g_all = -jnp.exp(alog_row) * jax.nn.softplus(ba + dtb_row)
    gc_all = hdot(tril_c.astype(f32), g_all)
    gc_t = gc_all.T

    def stack(piece):
        return jnp.concatenate([piece(h) for h in range(nh)], axis=0)

    q = stack(lambda h: xc[:, h * DN_HD:(h + 1) * DN_HD])
    k = stack(lambda h: xc[:, MIX + h * DN_HD:MIX + (h + 1) * DN_HD])
    v = stack(lambda h: xc[:, 2 * MIX + h * DN_HD:2 * MIX + (h + 1) * DN_HD])
    zs = stack(lambda h: z[:, h * DN_HD:(h + 1) * DN_HD])
    q = q * lax.rsqrt(jnp.sum(q * q, axis=-1, keepdims=True) + NORM_EPS) * (DN_HD ** -0.5)
    k = k * lax.rsqrt(jnp.sum(k * k, axis=-1, keepdims=True) + NORM_EPS)
    beta = stack(lambda h: _lane_col(beta_all, h))
    g_cols = [_lane_col(gc_all, nh + h) for h in range(nh)]
    g_col = jnp.concatenate(g_cols, axis=0)
    g_row = jnp.concatenate([gc_t[nh + h:nh + h + 1, :] for h in range(nh)], axis=1)
    g_last = stack(lambda h: jnp.broadcast_to(g_cols[h][c - 1:c, :], (c, 1)))
    decay = jnp.where(tril, jnp.exp(jnp.where(tril, g_col - g_row, 0.0)), 0.0)
    kb = k * beta
    low = jnp.where(strict, bdot(kb, k, 1, 1) * decay, 0.0)
    t_inv = tri_inv(low, c) if t_known is None else tri_inv_known(low, t_known)
    e_gc = jnp.exp(g_col)
    uw = bdot(t_inv, jnp.concatenate([v * beta, kb * e_gc], axis=1), 1, 0)
    u, w = uw[:, :DN_HD], uw[:, DN_HD:]
    attn = bdot(q, k, 1, 1) * decay
    own = (lax.broadcasted_iota(jnp.int32, (n, nh * DN_HD), 1) // DN_HD
           == lax.broadcasted_iota(jnp.int32, (n, nh * DN_HD), 0) // c)

    def spread(a):
        return jnp.where(own, jnp.concatenate([a] * nh, axis=1), 0.0)

    v_new = u - bdot(spread(w), state, 1, 0)
    o = bdot(spread(q * e_gc), state, 1, 0) + bdot(attn, v_new, 1, 0)
    keep = stack(lambda h: jnp.broadcast_to(jnp.exp(g_cols[h][c - 1:c, :]), (DN_HD, 1)))
    new_state = state * keep + bdot(spread(k * jnp.exp(g_last - g_col)), v_new, 0, 0)
    out = _rms(o, norm_row) * (zs * _sigmoid(zs))
    return new_state, jnp.concatenate([out[h * c:(h + 1) * c] for h in range(nh)], axis=1), t_inv


DN_STEP = 4 * DN_CHUNK


def _dn_step(state, xc, z, ba, alog_row, dtb_row, norm_row, t_known=None):
    outs, t_invs = [], []
    for c in range(DN_STEP // DN_CHUNK):
        rows = slice(c * DN_CHUNK, (c + 1) * DN_CHUNK)
        state, out, t_inv = _dn_chunk(state, xc[rows], z[rows], ba[rows], alog_row, dtb_row, norm_row,
                                      None if t_known is None else t_known[c])
        outs.append(out)
        t_invs.append(t_inv)
    return state, jnp.concatenate(outs, axis=0), jnp.stack(t_invs)


def _dn_specs(ts, order):
    zb = SEC_Z // MIX
    return [pl.BlockSpec((ts, 3 * MIX), lambda i: (order(i), 0)),
            pl.BlockSpec((ts, MIX), lambda i: (order(i), zb)),
            pl.BlockSpec((ts, LANES), lambda i: (order(i), 0)),
            pl.BlockSpec((1, LANES), lambda i: (0, 0)), pl.BlockSpec((1, LANES), lambda i: (0, 0)),
            pl.BlockSpec((1, LANES), lambda i: (0, 0))]


def dn_fwd(xc, proj, p_ba, alog_row, dtb_row, norm_row, *, side=None, name):
    s = xc.shape[0]
    ts = _tile(s, 512)
    n_step = ts // DN_STEP

    per_step = DN_STEP // DN_CHUNK
    n_tri = DN_HEADS * DN_CHUNK

    def body(xc_ref, z_ref, ba_ref, al_ref, dt_ref, nr_ref, o_ref, st_ref, tri_ref, state_ref):
        @pl.when(pl.program_id(0) == 0)
        def _():
            state_ref[...] = jnp.zeros_like(state_ref)

        def step(c, carry):
            rows = pl.ds(pl.multiple_of(c * DN_STEP, DN_STEP), DN_STEP)
            st_ref[c] = state_ref[...]
            new_state, out, t_invs = _dn_step(state_ref[...], xc_ref[rows, :], z_ref[rows, :].astype(f32), ba_ref[rows, :],
                                              al_ref[...], dt_ref[...], nr_ref[...])
            state_ref[...] = new_state
            o_ref[rows, :] = out.astype(bf16)
            tri_ref[pl.ds(c * per_step, per_step)] = t_invs
            return carry
        lax.fori_loop(0, n_step, step, 0)

    return host_call(
        side, body, name=name, grid=(s // ts,), in_specs=_dn_specs(ts, lambda i: i),
        out_specs=[pl.BlockSpec((ts, MIX), lambda i: (i, 0)),
                   pl.BlockSpec((n_step, DN_HEADS * DN_HD, DN_HD), lambda i: (i, 0, 0)),
                   pl.BlockSpec((n_step * per_step, n_tri, n_tri), lambda i: (i, 0, 0))],
        out_shape=[jax.ShapeDtypeStruct((s, MIX), bf16),
                   jax.ShapeDtypeStruct((s // DN_STEP, DN_HEADS * DN_HD, DN_HD), f32),
                   jax.ShapeDtypeStruct((s // DN_CHUNK, n_tri, n_tri), f32)],
        scratch_shapes=[pltpu.VMEM((DN_HEADS * DN_HD, DN_HD), f32)], aliases={},
        args=(xc, proj, p_ba, alog_row, dtb_row, norm_row))


def dn_bwd(xc, proj, p_ba, alog_row, dtb_row, norm_row, saved, tri, d_out, dproj, *, side=None, name):
    s = xc.shape[0]
    ts = _tile(s, 512)
    n_step = ts // DN_STEP
    nt = s // ts

    per_step = DN_STEP // DN_CHUNK
    n_tri = DN_HEADS * DN_CHUNK

    def body(xc_ref, z_ref, ba_ref, al_ref, dt_ref, nr_ref, st_ref, tri_ref, do_ref, _,
             dz_ref, dxc_ref, dba_ref, dal_ref, ddt_ref, dnr_ref, dstate_ref):
        @pl.when(pl.program_id(0) == 0)
        def _():
            dstate_ref[...] = jnp.zeros_like(dstate_ref)
            dal_ref[...] = jnp.zeros_like(dal_ref)
            ddt_ref[...] = jnp.zeros_like(ddt_ref)
            dnr_ref[...] = jnp.zeros_like(dnr_ref)

        def step(it, carry):
            c = n_step - 1 - it
            rows = pl.ds(pl.multiple_of(c * DN_STEP, DN_STEP), DN_STEP)
            t_known = tri_ref[pl.ds(c * per_step, per_step)]
            _, vjp = jax.vjp(lambda *a: _dn_step(*a, t_known=t_known)[:2], st_ref[c], xc_ref[rows, :],
                             z_ref[rows, :].astype(f32), ba_ref[rows, :], al_ref[...], dt_ref[...], nr_ref[...])
            d_in, dxc, dz, dba, dal, ddt, dnr = vjp((dstate_ref[...], do_ref[rows, :]))
            dstate_ref[...] = d_in
            dxc_ref[rows, :] = dxc
            dz_ref[rows, :] = dz.astype(bf16)
            dba_ref[rows, :] = dba.astype(bf16)
            dal_ref[...] += dal
            ddt_ref[...] += ddt
            dnr_ref[...] += dnr
            return carry
        lax.fori_loop(0, n_step, step, 0)

    def rev(i):
        return nt - 1 - i

    specs = _dn_specs(ts, rev)
    vec = pl.BlockSpec((1, LANES), lambda i: (0, 0))
    return host_call(
        side, body, name=name, grid=(nt,),
        in_specs=specs + [pl.BlockSpec((n_step, DN_HEADS * DN_HD, DN_HD), lambda i: (rev(i), 0, 0)),
                          pl.BlockSpec((n_step * per_step, n_tri, n_tri), lambda i: (rev(i), 0, 0)),
                          pl.BlockSpec((ts, MIX), lambda i: (rev(i), 0)), pl.BlockSpec(memory_space=pl.ANY)],
        out_specs=[specs[1], specs[0], specs[2], vec, vec, vec],
        out_shape=[jax.ShapeDtypeStruct(dproj.shape, bf16), jax.ShapeDtypeStruct((s, 3 * MIX), f32),
                   jax.ShapeDtypeStruct((s, LANES), bf16)] + [jax.ShapeDtypeStruct((1, LANES), f32)] * 3,
        scratch_shapes=[pltpu.VMEM((DN_HEADS * DN_HD, DN_HD), f32)], aliases={9: 0},
        args=(xc, proj, p_ba, alog_row, dtb_row, norm_row, saved, tri, d_out, dproj))


def _merge_in_specs(ts, d):
    row = pl.BlockSpec((ts, MIX), lambda i: (i, 0))
    return [row, row, row, pl.BlockSpec((ts, 3 * d), lambda i: (i, SEC_G // (3 * d))),
            pl.BlockSpec((3, MIX, d), lambda i: (0, 0, 0))]


def merge_fwd(out_a, out_b, out_c, proj, w_branch, *, name):
    s, d = out_a.shape[0], w_branch.shape[2]
    ts = _tile(s, 512, 16)

    def body(a_ref, b_ref, c_ref, g_ref, w_ref, o_ref):
        acc = jnp.zeros((ts, d), f32)
        for n, r in enumerate((a_ref, b_ref, c_ref)):
            acc = acc + _sigmoid(g_ref[:, n * d:(n + 1) * d].astype(f32)) * _dg(r[...], w_ref[n], 1, 0)
        o_ref[...] = acc.astype(bf16)

    return pl.pallas_call(
        body, name=name, grid=(s // ts,), in_specs=_merge_in_specs(ts, d),
        out_specs=pl.BlockSpec((ts, d), lambda i: (i, 0)), out_shape=jax.ShapeDtypeStruct((s, d), bf16),
        compiler_params=_params(1),
    )(out_a, out_b, out_c, proj, w_branch)


def merge_bwd(out_a, out_b, out_c, proj, w_branch, d_merged, dproj, *, name):
    s, d = out_a.shape[0], w_branch.shape[2]
    ts = _tile(s, 512, 16)

    def body(a_ref, b_ref, c_ref, g_ref, w_ref, dm_ref, _, dg_ref, da_ref, db_ref, dc_ref, dw_ref):
        @pl.when(pl.program_id(0) == 0)
        def _():
            dw_ref[...] = jnp.zeros_like(dw_ref)

        dm = dm_ref[...]
        for n, (r, dr) in enumerate(((a_ref, da_ref), (b_ref, db_ref), (c_ref, dc_ref))):
            gate = _sigmoid(g_ref[:, n * d:(n + 1) * d].astype(f32))
            branch = _dg(r[...], w_ref[n], 1, 0)
            dg_ref[:, n * d:(n + 1) * d] = (dm * branch * gate * (1.0 - gate)).astype(bf16)
            d_branch = dm * gate
            dr[...] = _dg(d_branch, w_ref[n], 1, 1)
            dw_ref[n] += _dg(r[...], d_branch, 0, 0)

    specs = _merge_in_specs(ts, d)
    row_f = pl.BlockSpec((ts, MIX), lambda i: (i, 0))
    return pl.pallas_call(
        body, name=name, grid=(s // ts,),
        in_specs=specs + [pl.BlockSpec((ts, d), lambda i: (i, 0)), pl.BlockSpec(memory_space=pl.ANY)],
        out_specs=[specs[3], row_f, row_f, row_f, specs[4]],
        out_shape=[jax.ShapeDtypeStruct(dproj.shape, bf16)] + [jax.ShapeDtypeStruct((s, MIX), f32)] * 3
        + [jax.ShapeDtypeStruct(w_branch.shape, f32)],
        input_output_aliases={6: 0}, compiler_params=_params(1),
    )(out_a, out_b, out_c, proj, w_branch, d_merged, dproj)


def swiglu_fwd(gu, *, name):
    g2, s, w = gu.shape
    ng = g2 // 2
    ts = _tile(s, 1024, 16)

    def body(g_ref, u_ref, o_ref):
        g = g_ref[...].astype(f32)
        o_ref[...] = (g * _sigmoid(g) * u_ref[...].astype(f32)).astype(bf16)

    return pl.pallas_call(
        body, name=name, grid=(s // ts, ng),
        in_specs=[pl.BlockSpec((None, ts, w), lambda i, j: (j, i, 0)),
                  pl.BlockSpec((None, ts, w), lambda i, j: (ng + j, i, 0))],
        out_specs=pl.BlockSpec((None, ts, w), lambda i, j: (j, i, 0)), out_shape=jax.ShapeDtypeStruct((ng, s, w), bf16),
        compiler_params=_params(2),
    )(gu, gu)


def swiglu_bwd(gu, d_act, *, name):
    g2, s, w = gu.shape
    ng = g2 // 2
    ts = _tile(s, 1024, 16)

    def body(g_ref, u_ref, d_ref, dg_ref, du_ref):
        g, d = g_ref[...].astype(f32), d_ref[...].astype(f32)
        sig = _sigmoid(g)
        dg_ref[...] = (d * u_ref[...].astype(f32) * sig * (1.0 + g * (1.0 - sig))).astype(bf16)
        du_ref[...] = (d * g * sig).astype(bf16)

    lo = pl.BlockSpec((None, ts, w), lambda i, j: (j, i, 0))
    return pl.pallas_call(
        body, name=name, grid=(s // ts, ng),
        in_specs=[lo, pl.BlockSpec((None, ts, w), lambda i, j: (ng + j, i, 0)), lo], out_specs=[lo, lo],
        out_shape=[jax.ShapeDtypeStruct((ng, s, w), bf16)] * 2, compiler_params=_params(2),
    )(gu, gu, d_act)


def adamw(w, m, v, g_parts, *, name):
    n_layers = len(g_parts)
    n_parts, r, cols = g_parts[0].shape
    lanes = -(-cols // LANES) * LANES
    tr = _tile(r, max(16, (128 * 1024) // lanes), 16)
    nr = r // tr

    def body(w_ref, m_ref, v_ref, *rest):
        gp_refs, (g_ref, d_ref, nm_ref, nv_ref) = rest[:n_layers], rest[n_layers:]
        layer = pl.program_id(0)
        g = jnp.zeros((tr, cols), f32)
        for l, gp_ref in enumerate(gp_refs):
            g_l = gp_ref[0].astype(f32)
            for k in range(1, n_parts):
                g_l = g_l + gp_ref[k].astype(f32)
            g = jnp.where(layer == l, g_l, g)
        nm = ADAM_B1 * m_ref[...] + (1.0 - ADAM_B1) * g
        nv = ADAM_B2 * v_ref[...] + (1.0 - ADAM_B2) * jnp.square(g)
        m_hat = nm / (1.0 - ADAM_B1 ** ADAM_STEP)
        v_hat = nv / (1.0 - ADAM_B2 ** ADAM_STEP)
        g_ref[...] = g
        d_ref[...] = -ADAM_LR * (m_hat / (jnp.sqrt(v_hat) + ADAM_EPS) + ADAM_WD * w_ref[...])
        nm_ref[...] = nm
        nv_ref[...] = nv

    row = pl.BlockSpec((tr, cols), lambda l, i: (l * nr + i, 0))

    def parts_spec(own):
        return pl.BlockSpec((n_parts, tr, cols),
                            lambda l, i: (0, jnp.where(l == own, i, jnp.where(l < own, 0, nr - 1)), 0))

    return pl.pallas_call(
        body, name=name, grid=(n_layers, nr), in_specs=[row, row, row] + [parts_spec(l) for l in range(n_layers)],
        out_specs=[row] * 4, out_shape=[jax.ShapeDtypeStruct(w.shape, f32)] * 4, compiler_params=_params(2),
    )(w, m, v, *g_parts)


def _mesh_pos():
    return lax.axis_index("x"), lax.axis_index("y"), lax.axis_index("c")


def _dev_index(p):
    return 4 * p[0] + 2 * p[1] + p[2]


class Exchange:
    def __init__(self, kind, arrays):
        self.kind, self.arrays, self.n = kind, list(arrays), len(arrays)
        self.specs = [pl.BlockSpec(memory_space=pl.ANY)] * self.n
        self.out_shapes = [jax.ShapeDtypeStruct(((N_DEV,) if kind == "gather" else ()) + a.shape, a.dtype)
                           for a in self.arrays]
        self.scratch = [pltpu.SemaphoreType.DMA((self.n, N_DEV - 1)), pltpu.SemaphoreType.DMA((self.n, N_DEV - 1)),
                        pltpu.SemaphoreType.DMA((self.n,))]

    def _copies(self, in_refs, out_refs, sems, with_arrivals):
        send_sems, recv_sems, local_sems = sems
        x, y, c = _mesh_pos()
        mine = _dev_index((x, y, c))

        def src(a, slab):
            return in_refs[a] if self.kind == "gather" else in_refs[a].at[slab]

        local = [pltpu.make_async_copy(src(a, mine), out_refs[a].at[mine], local_sems.at[a]) for a in range(self.n)]
        sends, arrivals = [], []
        for k in range(1, N_DEV):
            peer = (1 - x if k & 4 else x, 1 - y if k & 2 else y, 1 - c if k & 1 else c)
            theirs = _dev_index(peer)
            for a in range(self.n):
                to = dict(send_sem=send_sems.at[a, k - 1], recv_sem=recv_sems.at[a, k - 1], device_id=peer,
                          device_id_type=pl.DeviceIdType.MESH)
                sends.append(pltpu.make_async_remote_copy(src_ref=src(a, theirs), dst_ref=out_refs[a].at[mine], **to))
                if with_arrivals:
                    arrivals.append(pltpu.make_async_remote_copy(src_ref=src(a, theirs),
                                                                 dst_ref=out_refs[a].at[theirs], **to))
        return local, sends, arrivals

    def start(self, in_refs, out_refs, sems):
        local, sends, _ = self._copies(in_refs, out_refs, sems, False)
        for cp in local + sends:
            cp.start()

    def wait(self, in_refs, out_refs, sems):
        local, sends, arrivals = self._copies(in_refs, out_refs, sems, True)
        for cp in arrivals:
            cp.wait_recv()
        for cp in sends:
            cp.wait_send()
        for cp in local:
            cp.wait()

    def run_around(self, grid, in_refs, out_refs, sems, *, before):
        at = None
        for axis, size in enumerate(grid):
            hit = pl.program_id(axis) == (0 if before else size - 1)
            at = hit if at is None else at & hit

        @pl.when(at)
        def _():
            (self.start if before else self.wait)(in_refs, out_refs, sems)


def host_call(side, body, *, name, grid, in_specs, out_specs, out_shape, scratch_shapes, args, aliases):
    n_in, n_out = len(in_specs), len(out_specs)
    if side is None:
        kernel_body = body
    else:
        n = side.n
        in_specs, args = in_specs + side.specs, tuple(args) + tuple(side.arrays)
        out_specs, out_shape = out_specs + side.specs, out_shape + side.out_shapes
        scratch_shapes = scratch_shapes + side.scratch

        def kernel_body(*refs):
            ins, side_in = refs[:n_in], refs[n_in:n_in + n]
            outs, side_out = refs[n_in + n:n_in + n + n_out], refs[n_in + n + n_out:n_in + 2 * n + n_out]
            scratch, sems = refs[n_in + 2 * n + n_out:-3], refs[-3:]
            side.run_around(grid, side_in, side_out, sems, before=True)
            body(*ins, *outs, *scratch)
            side.run_around(grid, side_in, side_out, sems, before=False)

    outs = pl.pallas_call(
        kernel_body, name=name, grid=grid, in_specs=in_specs, out_specs=out_specs, out_shape=out_shape,
        scratch_shapes=scratch_shapes, input_output_aliases=aliases, compiler_params=_params(len(grid)),
    )(*args)
    return outs if side is None else (outs[:n_out], outs[n_out:])


def all_gather(blocks, *, name):
    n = len(blocks)
    any_spec = pl.BlockSpec(memory_space=pl.ANY)

    def body(*refs):
        ins, outs = refs[:n], refs[n:2 * n]
        send_sems, recv_sems, local_sems = refs[2 * n:]
        x, y, c = _mesh_pos()
        me, sibling = (x, y, c), (x, y, 1 - c)
        chips = [(1 - x, y), (x, 1 - y), (1 - x, 1 - y)]

        def copy(a, k, block, to, src=None):
            dst = outs[a].at[_dev_index(block)]
            return pltpu.make_async_remote_copy(
                src_ref=dst if src is None else src, dst_ref=dst, send_sem=send_sems.at[a, k],
                recv_sem=recv_sems.at[a, k], device_id=to, device_id_type=pl.DeviceIdType.MESH)

        mine = [pltpu.make_async_copy(ins[a], outs[a].at[_dev_index(me)], local_sems.at[a]) for a in range(n)]
        for cp in mine:
            cp.start()
        first = []
        for a in range(n):
            first.append(copy(a, 0, me, sibling, src=ins[a]))
            first += [copy(a, 1 + j, me, (*chip, c), src=ins[a]) for j, chip in enumerate(chips)]
        for cp in first:
            cp.start()
        passed = []
        for j, chip in enumerate(chips):
            for a in range(n):
                copy(a, 1 + j, (*chip, c), me).wait_recv()
                fwd = copy(a, 4 + j, (*chip, c), sibling)
                fwd.start()
                passed.append(fwd)
        for a in range(n):
            copy(a, 0, sibling, me).wait_recv()
            for j, chip in enumerate(chips):
                copy(a, 4 + j, (*chip, 1 - c), me).wait_recv()
        for cp in first + passed:
            cp.wait_send()
        for cp in mine:
            cp.wait()

    return pl.pallas_call(
        body, name=name, in_specs=[any_spec] * n, out_specs=[any_spec] * n,
        out_shape=[jax.ShapeDtypeStruct((N_DEV,) + b.shape, b.dtype) for b in blocks],
        scratch_shapes=[pltpu.SemaphoreType.DMA((n, 7)), pltpu.SemaphoreType.DMA((n, 7)),
                        pltpu.SemaphoreType.DMA((n,))],
    )(*blocks)


def exchange(jobs, *, name):
    total = sum(j.n for j in jobs)

    def body(*refs):
        ins, outs, sems = refs[:total], refs[total:2 * total], refs[2 * total:]
        pieces, off = [], 0
        for i, j in enumerate(jobs):
            pieces.append((ins[off:off + j.n], outs[off:off + j.n], sems[3 * i:3 * i + 3]))
            off += j.n
        for j, piece in zip(jobs, pieces):
            j.start(*piece)
        for j, piece in zip(jobs, pieces):
            j.wait(*piece)

    outs = pl.pallas_call(
        body, name=name, in_specs=[s for j in jobs for s in j.specs], out_specs=[s for j in jobs for s in j.specs],
        out_shape=[s for j in jobs for s in j.out_shapes], scratch_shapes=[s for j in jobs for s in j.scratch],
    )(*[a for j in jobs for a in j.arrays])
    split, off = [], 0
    for j in jobs:
        split.append(outs[off:off + j.n])
        off += j.n
    return split


def _rows128(arr):
    flat = arr.reshape(-1)
    rows = -(-flat.shape[0] // (8 * LANES)) * 8
    return jnp.pad(flat, (0, rows * LANES - flat.shape[0])).reshape(rows, LANES)


def _pad_lanes(row, width=LANES, at=0):
    return jnp.pad(row, (at, width - at - row.shape[0])).reshape(1, width)


def _w_in_sections(got):
    d = got.shape[1]
    wi = jnp.transpose(got, (1, 0, 2)).reshape(d, -1)
    w_main = jnp.concatenate([wi[:, :C_B], wi[:, C_Z:C_BA], wi[:, C_QKV:C_Z], wi[:, C_G:], wi[:, C_B:C_QKV]], axis=1)
    return w_main, jnp.pad(wi[:, C_BA:C_G], ((0, 0), (0, LANES - (C_G - C_BA))))


def _w_in_parts(gw_main, gw_ba):
    d = gw_main.shape[0]
    full = jnp.concatenate([gw_main[:, SEC_A:SEC_Z], gw_main[:, SEC_B:], gw_main[:, SEC_QKV:SEC_G],
                            gw_main[:, SEC_Z:SEC_QKV], gw_ba[:, :C_G - C_BA], gw_main[:, SEC_G:SEC_B]], axis=1)
    return jnp.transpose(full.reshape(d, N_DEV, -1), (1, 0, 2))


def kernel(x, positions, attn_norm, w_in, sgu_ln_g, sgu_ln_b, sgu_w, sgu_b, attn_sinks, dn_conv_w, dn_a_log, dn_dt_bias, dn_norm, w_branch, w_out, ffn_norm, w_gate_up, w_down, final_norm, loss_target, m_attn_norm, m_w_in, m_sgu_ln_g, m_sgu_ln_b, m_sgu_w, m_sgu_b, m_attn_sinks, m_dn_conv_w, m_dn_a_log, m_dn_dt_bias, m_dn_norm, m_w_branch, m_w_out, m_ffn_norm, m_w_gate_up, m_w_down, m_final_norm, v_attn_norm, v_w_in, v_sgu_ln_g, v_sgu_ln_b, v_sgu_w, v_sgu_b, v_attn_sinks, v_dn_conv_w, v_dn_a_log, v_dn_dt_bias, v_dn_norm, v_w_branch, v_w_out, v_ffn_norm, v_w_gate_up, v_w_down, v_final_norm):
    given = dict(locals())
    depth, d_model = attn_norm.shape
    s = x.shape[1]
    x2 = x.reshape(s, d_model)
    target = loss_target.reshape(s, d_model)
    posf = positions.reshape(s, 1).astype(f32)
    inv_freq = ROPE_THETA ** (-jnp.arange(0, ROPE_DIM, 2, dtype=f32) / ROPE_DIM)
    inv_head = jnp.concatenate([inv_freq, inv_freq, jnp.zeros((SWA_HD - ROPE_DIM,), f32)])
    tables = rope_tables(posf, jnp.tile(inv_head, LANES // SWA_HD).reshape(1, LANES), name="rope_tables")

    assert depth == 2, depth
    gathered = dict(zip([("w_in", 0), ("dn_conv_w", 0), ("dn_conv_w", 1)], all_gather(
        [w_in[0].astype(bf16), dn_conv_w[0], dn_conv_w[1]], name="gather_first")))
    riders = {"l0_in_proj": [("w_branch", 0), ("w_out", 0)],
              "l0_deltanet": [("w_gate_up", 0), ("w_down", 0), ("w_in", 1)],
              "l1_deltanet": [("w_branch", 1), ("w_out", 1), ("w_gate_up", 1), ("w_down", 1)]}

    def gathering(host, call, *args, **kw):
        if host not in riders:
            return call(*args, name=host, **kw)
        side = Exchange("gather", [given[n][l].astype(bf16) for n, l in riders[host]])
        out, got = call(*args, side=side, name=host, **kw)
        gathered.update(zip(riders[host], got))
        return out

    layers, saved = [], []
    h_in = x2
    for l in range(depth):
        t = f"l{l}_"
        w_main, w_ba = _w_in_sections(gathered["w_in", l])
        conv_full = jnp.transpose(gathered["dn_conv_w", l], (1, 0, 2)).reshape(DN_CONV, -1)
        p = dict(
            w_main=w_main, w_ba=w_ba, conv_w8=jnp.pad(conv_full, ((0, CONV_PAD - DN_CONV), (0, 0))),
            attn_norm=attn_norm[l].reshape(1, -1), ffn_norm=ffn_norm[l].reshape(1, -1),
            ln_g=sgu_ln_g[l].reshape(1, -1), ln_b=sgu_ln_b[l].reshape(1, -1), sgu_w=sgu_w[l], sgu_bt=sgu_b[l].T,
            sink_row=_pad_lanes(attn_sinks[l]), alog_row=_pad_lanes(dn_a_log[l], at=DN_HEADS),
            dtb_row=_pad_lanes(dn_dt_bias[l], at=DN_HEADS), norm_row=dn_norm[l].reshape(1, -1))
        layers.append(p)
        h = rmsnorm_fwd(h_in, p["attn_norm"], name=t + "attn_norm")
        proj = gathering(t + "in_proj", matmul, h, p["w_main"], "nn", bf16)
        p_ba = matmul(h, p["w_ba"], "nn", f32, name=t + "in_proj_ba")
        out_a = sgu_fwd(proj, p["ln_g"], p["ln_b"], p["sgu_w"], p["sgu_bt"], name=t + "sgu")
        out_b = swa_fwd(proj, p["sink_row"], tables, name=t + "swa")
        xc = conv_fwd(proj, p["conv_w8"], name=t + "dn_conv")
        out_c, states, tri = gathering(t + "deltanet", dn_fwd, xc, proj, p_ba, p["alog_row"], p["dtb_row"], p["norm_row"])
        p.update(w_branch=jnp.transpose(gathered["w_branch", l], (1, 2, 0, 3)).reshape(3, MIX, d_model),
                 w_out=gathered["w_out", l].reshape(d_model, d_model),
                 w_gu=gathered["w_gate_up", l],
                 w_down=gathered["w_down", l].reshape(N_DEV // 2, -1, d_model))
        merged = merge_fwd(out_a, out_b, out_c, proj, p["w_branch"], name=t + "merge")
        x_mid = matmul(merged, p["w_out"], "nn", f32, residual=h_in, tn=1024, name=t + "out_proj")
        h2 = rmsnorm_fwd(x_mid, p["ffn_norm"], name=t + "ffn_norm")
        gu = matmul(h2, p["w_gu"], "nn", bf16, group="n", name=t + "gate_up")
        act = swiglu_fwd(gu, name=t + "swiglu")
        x_out = matmul(act, p["w_down"], "nn", f32, residual=x_mid, group="k", tn=1024, name=t + "down")
        saved.append(dict(x_in=h_in, h=h, proj=proj, p_ba=p_ba, out_a=out_a, out_b=out_b, out_c=out_c, xc=xc,
                          states=states, tri=tri, merged=merged, x_mid=x_mid, h2=h2, gu=gu, act=act))
        h_in = x_out

    dx, d_final_norm, loss_row = loss_head(h_in, final_norm.reshape(1, -1), target, name="loss_head")
    loss = lax.psum(loss_row[0, 0], MESH_AXES)

    shard_names = ["w_in", "dn_conv_w", "w_branch", "w_out", "w_gate_up", "w_down"]
    rep_names = ["attn_norm", "sgu_ln_g", "sgu_ln_b", "sgu_w", "sgu_b", "attn_sinks", "dn_a_log", "dn_dt_bias",
                 "dn_norm", "ffn_norm"]
    parts, received, per_layer = {}, {}, []
    senders = {"l0_b_swa": [("w_gate_up", 1), ("w_down", 1), ("w_out", 1), ("w_branch", 1)],
               "l0_b_deltanet": [("w_in", 1), ("w_gate_up", 0), ("w_down", 0)],
               "l0_b_dn_conv": [("w_out", 0), ("w_branch", 0)],
               "l0_b_in_proj_dx": [("w_in", 0)]}

    def scattering(host, call, *args, **kw):
        if host not in senders:
            return call(*args, name=host, **kw)
        out, got = call(*args, side=Exchange("scatter", [parts[key] for key in senders[host]]), name=host, **kw)
        received.update(zip(senders[host], got))
        return out

    for l in reversed(range(depth)):
        p, sv, t = layers[l], saved[l], f"l{l}_b_"
        d_act = matmul(dx, p["w_down"], "nt", bf16, group="n", name=t + "down_dx")
        gw_down = matmul(sv["act"], dx, "tn", bf16, group="m", tk=2048, tn=512, name=t + "down_dw")
        d_gate, d_up = swiglu_bwd(sv["gu"], d_act, name=t + "swiglu")
        gw_gu = jnp.concatenate([matmul(sv["h2"], d_half, "tn", bf16, group="n", tk=2048, name=t + "gate_up_dw" + tag)
                                 for d_half, tag in ((d_gate, "_gate"), (d_up, "_up"))], axis=0)
        half = N_DEV // 2
        d_h2 = matmul(d_gate, p["w_gu"][:half], "nt", f32, group="k", tn=1024, name=t + "gate_up_dx_gate")
        d_h2 = matmul(d_up, p["w_gu"][half:], "nt", f32, group="k", tn=1024, residual=d_h2, name=t + "gate_up_dx_up")
        dx_mid, g_ffn = rmsnorm_bwd(sv["x_mid"], p["ffn_norm"], d_h2, dx, name=t + "ffn_norm")
        d_merged = matmul(dx_mid, p["w_out"], "nt", f32, tn=1024, name=t + "out_proj_dx")
        gw_out = matmul(sv["merged"], dx_mid, "tn", bf16, tk=2048, tn=1024, name=t + "out_proj_dw")
        dproj = lax.empty((s, W_MAIN), bf16)
        dproj, d_a, d_b, d_c, gw_branch = merge_bwd(sv["out_a"], sv["out_b"], sv["out_c"], sv["proj"], p["w_branch"],
                                                   d_merged, dproj, name=t + "merge")
        parts.update({("w_gate_up", l): gw_gu, ("w_down", l): gw_down.reshape(N_DEV, -1, d_model),
                      ("w_out", l): gw_out.reshape(N_DEV, -1, d_model),
                      ("w_branch", l): jnp.transpose(gw_branch.reshape(3, MIX, N_DEV, -1), (2, 0, 1, 3)).astype(bf16)})
        dproj, g_ln_g, g_ln_b, g_sgu_w, g_sgu_bt = sgu_bwd(sv["proj"], p["ln_g"], p["ln_b"], p["sgu_w"], p["sgu_bt"],
                                                         d_a, dproj, name=t + "sgu")
        dproj, g_sink = scattering(t + "swa", swa_bwd, sv["proj"], p["sink_row"], tables, d_b, dproj)
        dproj, dxc, dba, g_alog, g_dtb, g_dnorm = scattering(
            t + "deltanet", dn_bwd, sv["xc"], sv["proj"], sv["p_ba"], p["alog_row"], p["dtb_row"], p["norm_row"],
            sv["states"], sv["tri"], d_c, dproj)
        dproj, g_conv8 = scattering(t + "dn_conv", conv_bwd, sv["proj"], p["conv_w8"], dxc, dproj)
        gw_main = matmul(sv["h"], dproj, "tn", bf16, tk=2048, name=t + "in_proj_dw")
        gw_ba = matmul(sv["h"], dba, "tn", bf16, tk=2048, name=t + "in_proj_ba_dw")
        parts["w_in", l] = _w_in_parts(gw_main, gw_ba)
        d_h = scattering(t + "in_proj_dx", matmul, dproj, p["w_main"], "nt", f32, tk=2304, tn=1024)
        d_h = matmul(dba, p["w_ba"], "nt", f32, residual=d_h, name=t + "in_proj_ba_dx")
        dx, g_attn = rmsnorm_bwd(sv["x_in"], p["attn_norm"], d_h, dx_mid, name=t + "attn_norm")
        per_layer.append(dict(
            dn_conv_w=jnp.transpose(g_conv8[:DN_CONV].reshape(DN_CONV, N_DEV, -1), (1, 0, 2)),
            attn_norm=g_attn, sgu_ln_g=g_ln_g, sgu_ln_b=g_ln_b, sgu_w=g_sgu_w, sgu_b=g_sgu_bt.T,
            attn_sinks=g_sink[0, :SWA_HEADS], dn_a_log=g_alog[0, DN_HEADS:2 * DN_HEADS],
            dn_dt_bias=g_dtb[0, DN_HEADS:2 * DN_HEADS], dn_norm=g_dnorm, ffn_norm=g_ffn))
    per_layer.reverse()

    conv_parts = jnp.concatenate([pp["dn_conv_w"] for pp in per_layer], axis=1)
    rep_grads = {n: jnp.stack([pp[n].reshape(given[n].shape[1:]) for pp in per_layer]) for n in rep_names}
    rep_grads["final_norm"] = d_final_norm[0]
    rep_names = rep_names + ["final_norm"]
    rep_rows = [_rows128(given[n]).shape[0] for n in rep_names]
    pad_rows = -sum(rep_rows) % 16

    def small_rows(values):
        pieces = [_rows128(values[n]) for n in rep_names]
        return jnp.concatenate(pieces + [jnp.zeros((pad_rows, LANES), f32)], axis=0)

    (got_conv,), (small_all,) = exchange(
        [Exchange("scatter", [conv_parts]), Exchange("gather", [small_rows(rep_grads)])], name="exchange_last")

    results = [{}, {}, {}, {}]
    for n in shard_names:
        shp = given[n].shape
        two = (-1, shp[-1])
        by_layer = [got_conv] if n == "dn_conv_w" else [received[n, l].reshape(N_DEV, -1, shp[-1]) for l in range(depth)]
        outs = adamw(given[n].reshape(two), given["m_" + n].reshape(two), given["v_" + n].reshape(two), by_layer,
                     name="adamw_" + n)
        for res, val in zip(results, outs):
            res[n] = val.reshape(shp)
    outs = adamw(small_rows(given), small_rows({n: given["m_" + n] for n in rep_names}),
                 small_rows({n: given["v_" + n] for n in rep_names}), [small_all], name="adamw_replicated")
    for res, val in zip(results, outs):
        row = 0
        for n, nr in zip(rep_names, rep_rows):
            res[n] = val[row:row + nr].reshape(-1)[:given[n].size].reshape(given[n].shape)
            row += nr
    order = ["attn_norm", "w_in", "sgu_ln_g", "sgu_ln_b", "sgu_w", "sgu_b", "attn_sinks", "dn_conv_w", "dn_a_log",
             "dn_dt_bias", "dn_norm", "w_branch", "w_out", "ffn_norm", "w_gate_up", "w_down", "final_norm"]
    return (loss, dx.reshape(x.shape), *[res[n] for res in results for n in order])
```

```python
import functools

import jax
import jax.numpy as jnp
from jax import lax
from jax.experimental import pallas as pl
from jax.experimental.pallas import tpu as pltpu

f32 = jnp.float32
bf16 = jnp.bfloat16

N_DEV = 8
MESH_AXES = ("x", "y", "c")
NORM_EPS = 1e-6
MIX = 512
SGU_GROUPS, SGU_CHUNK = 4, 128
SWA_HEADS, SWA_KV, SWA_HD, WINDOW = 8, 2, 64, 128
ROPE_THETA, ROPE_DIM = 500000.0, 16
DN_HEADS, DN_HD, DN_CONV, DN_CHUNK = 4, 128, 4, 64
ADAM_LR, ADAM_B1, ADAM_B2, ADAM_EPS, ADAM_WD, ADAM_STEP = 0.001, 0.9, 0.999, 1e-08, 0.01, 10

LANES = 128
VMEM_LIMIT = 56 * 1024 * 1024

SEC_A, SEC_Z, SEC_QKV, SEC_G, SEC_B = 0, 1024, 1536, 3072, 6144
W_MAIN = 6912
C_B, C_QKV, C_Z, C_BA, C_G = 1024, 1792, 3328, 3840, 3848


def _params(n_axes, **kw):
    return pltpu.CompilerParams(dimension_semantics=("arbitrary",) * n_axes, vmem_limit_bytes=VMEM_LIMIT, **kw)


def _tile(n, target, mult=LANES):
    if n <= target:
        return n
    best = None
    for t in range(mult, target + 1, mult):
        if n % t == 0:
            best = t
    assert best is not None, (n, target, mult)
    return best


def _dg(a, b, ca, cb):
    return lax.dot_general(a.astype(bf16), b.astype(bf16), (((ca,), (cb,)), ((), ())), preferred_element_type=f32)


def _dg3(a, b, ca, cb):
    a_hi, b_hi = a.astype(bf16), b.astype(bf16)
    a_lo, b_lo = (a - a_hi.astype(f32)).astype(bf16), (b - b_hi.astype(f32)).astype(bf16)

    def dot(p, q):
        return lax.dot_general(p, q, (((ca,), (cb,)), ((), ())), preferred_element_type=f32)

    return dot(a_hi, b_hi) + (dot(a_hi, b_lo) + dot(a_lo, b_hi))


def _differentiable_dot(core):
    @functools.partial(jax.custom_vjp, nondiff_argnums=(2, 3))
    def dot(a, b, ca, cb):
        return core(a, b, ca, cb)

    def fwd(a, b, ca, cb):
        return core(a, b, ca, cb), (a, b)

    def bwd(ca, cb, res, ct):
        a, b = res
        da = core(ct, b, 1, 1 - cb) if ca == 1 else core(b, ct, 1 - cb, 1)
        db = core(a, ct, 1 - ca, 0) if cb == 0 else core(ct, a, 0, 1 - ca)
        return da, db

    dot.defvjp(fwd, bwd)
    return dot


bdot = _differentiable_dot(_dg)
_hdot = _differentiable_dot(_dg3)


def hdot(a, b, ca=1, cb=0):
    return _hdot(a, b, ca, cb)


@functools.partial(jax.custom_vjp, nondiff_argnums=(1,))
def lroll(x, shift):
    return pltpu.roll(x, shift, 1)


def _lroll_fwd(x, shift):
    return pltpu.roll(x, shift, 1), None


def _lroll_bwd(shift, _, ct):
    return (pltpu.roll(ct, ct.shape[1] - shift, 1),)


lroll.defvjp(_lroll_fwd, _lroll_bwd)


@functools.partial(jax.custom_vjp, nondiff_argnums=(1,))
def tri_inv(low, nil):
    n = low.shape[0]
    row = lax.broadcasted_iota(jnp.int32, (n, n), 0)
    col = lax.broadcasted_iota(jnp.int32, (n, n), 1)
    eye = (row == col).astype(f32)
    m = -low
    p = eye + m
    span = 2
    while span < nil:
        dot = hdot if span == 2 else (lambda a, b: bdot(a, b, 1, 0))
        m = dot(m, m)
        p = p + dot(p, m)
        span *= 2
    return p


def _tri_inv_fwd(low, nil):
    t = tri_inv(low, nil)
    return t, t


def _tri_inv_bwd(nil, t, dt):
    return (-bdot(t, bdot(dt, t, 1, 1), 0, 0),)


tri_inv.defvjp(_tri_inv_fwd, _tri_inv_bwd)


@jax.custom_vjp
def tri_inv_known(low, t):
    return t


def _tri_inv_known_fwd(low, t):
    return t, t


def _tri_inv_known_bwd(t, dt):
    return _tri_inv_bwd(None, t, dt) + (jnp.zeros_like(t),)


tri_inv_known.defvjp(_tri_inv_known_fwd, _tri_inv_known_bwd)


def _sigmoid(x):
    return 1.0 / (1.0 + jnp.exp(-x))


def _rms(x, g):
    return x * lax.rsqrt(jnp.mean(x * x, axis=-1, keepdims=True) + NORM_EPS) * g


def _lane_col(x, lane_idx):
    lane = lax.broadcasted_iota(jnp.int32, x.shape, 1)
    return jnp.sum(jnp.where(lane == lane_idx, x, 0.0), axis=1, keepdims=True)


def matmul(a, b, mode, out_dtype, *, residual=None, group=None, side=None, tm=1024, tn=768, tk=1024, name):
    dims = {"a": ("m", "k") if mode != "tn" else ("k", "m"),
            "b": {"nn": ("k", "n"), "nt": ("n", "k"), "tn": ("k", "n")}[mode], "o": ("m", "n")}
    full, groups = {}, 1
    for arr, key in ((a, "a"), (b, "b")):
        grouped = group in dims[key]
        if grouped:
            groups = arr.shape[0]
        full[dims[key][0]], full[dims[key][1]] = arr.shape[1:] if grouped else arr.shape
    want = {"m": tm, "n": tn, "k": tk}
    per_step = min(groups, 4) if group == "k" else 1
    tiles = {d: full[d] if d == group else _tile(full[d], want[d]) for d in "mnk"}
    steps = {d: groups // per_step if d == group else full[d] // tiles[d] for d in "mnk"}

    def spec(key):
        d0, d1 = dims[key]

        def index(i, j, kk):
            at = {"m": i, "n": j, "k": kk}
            if group in (d0, d1):
                return (at[group], 0 if d0 == group else at[d0], 0 if d1 == group else at[d1])
            return (at[d0], at[d1])

        block = (tiles[d0], tiles[d1])
        if group in (d0, d1):
            block = ((per_step if group == "k" else None),) + block
        return pl.BlockSpec(block, index)

    ca, cb = {"nn": (1, 0), "nt": (1, 1), "tn": (0, 0)}[mode]
    nk = steps["k"]
    o_spec = spec("o")
    out_shape = (groups, full["m"], full["n"]) if group in ("m", "n") else (full["m"], full["n"])
    has_res = residual is not None

    def product(a_ref, b_ref):
        if group != "k":
            return _dg(a_ref[...], b_ref[...], ca, cb)
        total = _dg(a_ref[0], b_ref[0], ca, cb)
        for g in range(1, per_step):
            total = total + _dg(a_ref[g], b_ref[g], ca, cb)
        return total

    def body(a_ref, b_ref, *rest):
        r_ref = rest[0] if has_res else None
        o_ref = rest[1 if has_res else 0]

        def emit(acc):
            o_ref[...] = (acc + r_ref[...] if has_res else acc).astype(out_dtype)

        if nk == 1:
            emit(product(a_ref, b_ref))
            return
        acc_ref = rest[-1]
        kk = pl.program_id(2)

        @pl.when(kk == 0)
        def _():
            acc_ref[...] = jnp.zeros_like(acc_ref)

        acc_ref[...] += product(a_ref, b_ref)

        @pl.when(kk == nk - 1)
        def _():
            emit(acc_ref[...])

    res = host_call(
        side, body, name=name, grid=(steps["m"], steps["n"], nk),
        in_specs=[spec("a"), spec("b")] + ([o_spec] if has_res else []), out_specs=[o_spec],
        out_shape=[jax.ShapeDtypeStruct(out_shape, out_dtype)],
        scratch_shapes=[pltpu.VMEM((tiles["m"], tiles["n"]), f32)] if nk > 1 else [], aliases={},
        args=(a, b) + ((residual,) if has_res else ()))
    return res[0] if side is None else (res[0][0], res[1])


def rmsnorm_fwd(x, g_row, *, name):
    s, d = x.shape
    ts = _tile(s, 512, 16)

    def body(x_ref, g_ref, o_ref):
        o_ref[...] = _rms(x_ref[...], g_ref[...]).astype(bf16)

    return pl.pallas_call(
        body, name=name, grid=(s // ts,),
        in_specs=[pl.BlockSpec((ts, d), lambda i: (i, 0)), pl.BlockSpec((1, d), lambda i: (0, 0))],
        out_specs=pl.BlockSpec((ts, d), lambda i: (i, 0)), out_shape=jax.ShapeDtypeStruct((s, d), bf16),
        compiler_params=_params(1),
    )(x, g_row)


def rmsnorm_bwd(x, g_row, dh, dres, *, name):
    s, d = x.shape
    ts = _tile(s, 512, 16)

    def body(x_ref, g_ref, dh_ref, dres_ref, dx_ref, dg_ref):
        @pl.when(pl.program_id(0) == 0)
        def _():
            dg_ref[...] = jnp.zeros_like(dg_ref)

        _, vjp = jax.vjp(_rms, x_ref[...], g_ref[...])
        dx, dg = vjp(dh_ref[...])
        dx_ref[...] = dx + dres_ref[...]
        dg_ref[...] += dg

    row = pl.BlockSpec((ts, d), lambda i: (i, 0))
    vec = pl.BlockSpec((1, d), lambda i: (0, 0))
    return pl.pallas_call(
        body, name=name, grid=(s // ts,), in_specs=[row, vec, row, row], out_specs=[row, vec],
        out_shape=[jax.ShapeDtypeStruct((s, d), f32), jax.ShapeDtypeStruct((1, d), f32)],
        compiler_params=_params(1),
    )(x, g_row, dh, dres)


def loss_head(x, g_row, target, *, name):
    s, d = x.shape
    ts = _tile(s, 512, 16)

    def body(x_ref, g_ref, t_ref, dx_ref, dg_ref, loss_ref):
        @pl.when(pl.program_id(0) == 0)
        def _():
            dg_ref[...] = jnp.zeros_like(dg_ref)
            loss_ref[...] = jnp.zeros_like(loss_ref)

        y, vjp = jax.vjp(_rms, x_ref[...], g_ref[...])
        err = y - t_ref[...]
        dx, dg = vjp(err * (1.0 / d))
        dx_ref[...] = dx
        dg_ref[...] += dg
        loss_ref[...] += 0.5 * jnp.sum(jnp.sum(err * err, axis=1, keepdims=True) * (1.0 / d), axis=0, keepdims=True)

    row = pl.BlockSpec((ts, d), lambda i: (i, 0))
    vec = pl.BlockSpec((1, d), lambda i: (0, 0))
    one = pl.BlockSpec((1, LANES), lambda i: (0, 0))
    return pl.pallas_call(
        body, name=name, grid=(s // ts,), in_specs=[row, vec, row], out_specs=[row, vec, one],
        out_shape=[jax.ShapeDtypeStruct((s, d), f32), jax.ShapeDtypeStruct((1, d), f32),
                   jax.ShapeDtypeStruct((1, LANES), f32)],
        compiler_params=_params(1),
    )(x, g_row, target)


def _sgu_chunk(p_a, ln_g, ln_b, w, b_t):
    t = SGU_CHUNK
    u = jax.nn.gelu(p_a[:, :MIX])
    v = jax.nn.gelu(p_a[:, MIX:])
    vc = v - jnp.mean(v, axis=-1, keepdims=True)
    vn = vc * lax.rsqrt(jnp.mean(vc * vc, axis=-1, keepdims=True) + NORM_EPS) * ln_g + ln_b
    causal = lax.broadcasted_iota(jnp.int32, (t, t), 0) >= lax.broadcasted_iota(jnp.int32, (t, t), 1)
    outs = []
    for g in range(SGU_GROUPS):
        sl = slice(g * LANES, (g + 1) * LANES)
        mixed = bdot(jnp.where(causal, w[g], 0.0), vn[:, sl], 1, 0) + b_t[:, g:g + 1]
        outs.append(u[:, sl] * mixed)
    return jnp.concatenate(outs, axis=1)


def _sgu_specs(s, ts):
    return [pl.BlockSpec((ts, 2 * MIX), lambda i: (i, SEC_A // (2 * MIX))),
            pl.BlockSpec((1, MIX), lambda i: (0, 0)), pl.BlockSpec((1, MIX), lambda i: (0, 0)),
            pl.BlockSpec((SGU_GROUPS, SGU_CHUNK, SGU_CHUNK), lambda i: (0, 0, 0)),
            pl.BlockSpec((SGU_CHUNK, SGU_GROUPS), lambda i: (0, 0))]


def sgu_fwd(proj, ln_g, ln_b, w, b_t, *, name):
    s = proj.shape[0]
    ts = _tile(s, 512)
    n_chunk = ts // SGU_CHUNK

    def body(p_ref, g_ref, b_ref, w_ref, bt_ref, o_ref):
        def step(c, carry):
            rows = pl.ds(pl.multiple_of(c * SGU_CHUNK, SGU_CHUNK), SGU_CHUNK)
            o_ref[rows, :] = _sgu_chunk(p_ref[rows, :].astype(f32), g_ref[...], b_ref[...], w_ref[...], bt_ref[...]).astype(bf16)
            return carry
        lax.fori_loop(0, n_chunk, step, 0)

    return pl.pallas_call(
        body, name=name, grid=(s // ts,), in_specs=_sgu_specs(s, ts),
        out_specs=pl.BlockSpec((ts, MIX), lambda i: (i, 0)), out_shape=jax.ShapeDtypeStruct((s, MIX), bf16),
        compiler_params=_params(1),
    )(proj, ln_g, ln_b, w, b_t)


def sgu_bwd(proj, ln_g, ln_b, w, b_t, d_out, dproj, *, name):
    s = proj.shape[0]
    ts = _tile(s, 512)
    n_chunk = ts // SGU_CHUNK

    def body(p_ref, g_ref, b_ref, w_ref, bt_ref, do_ref, _, dp_ref, dg_ref, db_ref, dw_ref, dbt_ref):
        @pl.when(pl.program_id(0) == 0)
        def _():
            dg_ref[...] = jnp.zeros_like(dg_ref)
            db_ref[...] = jnp.zeros_like(db_ref)
            dw_ref[...] = jnp.zeros_like(dw_ref)
            dbt_ref[...] = jnp.zeros_like(dbt_ref)

        def step(c, carry):
            rows = pl.ds(pl.multiple_of(c * SGU_CHUNK, SGU_CHUNK), SGU_CHUNK)
            _, vjp = jax.vjp(_sgu_chunk, p_ref[rows, :].astype(f32), g_ref[...], b_ref[...], w_ref[...], bt_ref[...])
            dp, dg, db, dw, dbt = vjp(do_ref[rows, :])
            dp_ref[rows, :] = dp.astype(bf16)
            dg_ref[...] += dg
            db_ref[...] += db
            dw_ref[...] += dw
            dbt_ref[...] += dbt
            return carry
        lax.fori_loop(0, n_chunk, step, 0)

    specs = _sgu_specs(s, ts)
    return pl.pallas_call(
        body, name=name, grid=(s // ts,),
        in_specs=specs + [pl.BlockSpec((ts, MIX), lambda i: (i, 0)), pl.BlockSpec(memory_space=pl.ANY)],
        out_specs=[specs[0], specs[1], specs[2], specs[3], specs[4]],
        out_shape=[jax.ShapeDtypeStruct(dproj.shape, bf16), jax.ShapeDtypeStruct((1, MIX), f32),
                   jax.ShapeDtypeStruct((1, MIX), f32), jax.ShapeDtypeStruct(w.shape, f32),
                   jax.ShapeDtypeStruct(b_t.shape, f32)],
        input_output_aliases={6: 0}, compiler_params=_params(1),
    )(proj, ln_g, ln_b, w, b_t, d_out, dproj)


def rope_tables(posf, inv_freq, *, name):
    s = posf.shape[0]
    ts = _tile(s, 1024, 8)
    half = ROPE_DIM // 2

    def body(pos_ref, inv_ref, o_ref):
        d = lax.broadcasted_iota(jnp.int32, (1, LANES), 1) % SWA_HD
        ang = pos_ref[...] * inv_ref[...]
        sin = jnp.sin(ang)
        o_ref[0] = jnp.cos(ang)
        o_ref[1] = jnp.where(d < half, sin, 0.0)
        o_ref[2] = jnp.where((d >= half) & (d < ROPE_DIM), sin, 0.0)

    return pl.pallas_call(
        body, name=name, grid=(s // ts,),
        in_specs=[pl.BlockSpec((ts, 1), lambda i: (i, 0)), pl.BlockSpec((1, LANES), lambda i: (0, 0))],
        out_specs=pl.BlockSpec((3, ts, LANES), lambda i: (0, i, 0)), out_shape=jax.ShapeDtypeStruct((3, s, LANES), f32),
        compiler_params=_params(1),
    )(posf, inv_freq)


def _rope(x, table):
    w = x.shape[1]
    half = ROPE_DIM // 2
    c, lo, hi = (jnp.concatenate([table[i]] * (w // LANES), axis=1) for i in range(3))
    return x * c - lroll(x, w - half) * lo + lroll(x, half) * hi


def _swa_block(q, kp, kc, vp, vc, sink_row, table_q, table_p, prev_ok, wide):
    t = WINDOW
    q = _rope(q, table_q) * (SWA_HD ** -0.5)
    keys = jnp.concatenate([_rope(kp, table_p), _rope(kc, table_q)], axis=0)
    vals = jnp.concatenate([vp, vc], axis=0)
    own = lax.broadcasted_iota(jnp.int32, (t, t), 0) >= lax.broadcasted_iota(jnp.int32, (t, t), 1)
    lane_half = lax.broadcasted_iota(jnp.int32, (t, LANES), 1) // SWA_HD
    group = SWA_HEADS // SWA_KV
    slabs = []
    for pair in range(SWA_HEADS // 2):
        q_pair = q[:, pair * LANES:(pair + 1) * LANES]
        acc = jnp.zeros((t, LANES), f32)
        for half in range(2):
            h = 2 * pair + half
            kv = h // group
            qm = jnp.where(lane_half == half, q_pair, 0.0)
            if half != kv:
                qm = lroll(qm, SWA_HD)
            if wide:
                both = bdot(qm, keys, 1, 1)
                s_prev, s_own = both[:, :t], both[:, t:]
            else:
                s_prev, s_own = bdot(qm, keys[:t], 1, 1), bdot(qm, keys[t:], 1, 1)
            logits = jnp.where(own, s_own, jnp.where(prev_ok, s_prev, -1e30))
            sink = _lane_col(sink_row, h)
            m = lax.stop_gradient(jnp.maximum(jnp.max(logits, axis=1, keepdims=True), sink))
            p = jnp.exp(logits - m)
            probs = p * (1.0 / (jnp.sum(p, axis=1, keepdims=True) + jnp.exp(sink - m)))
            p_prev, p_own = jnp.where(own, 0.0, probs), jnp.where(own, probs, 0.0)
            if wide:
                o = bdot(jnp.concatenate([p_prev, p_own], axis=1), vals, 1, 0)
            else:
                o = bdot(p_prev, vals[:t], 1, 0) + bdot(p_own, vals[t:], 1, 0)
            o = jnp.where(lane_half == kv, o, 0.0)
            if half != kv:
                o = lroll(o, SWA_HD)
            acc = acc + o
        slabs.append(acc)
    return jnp.concatenate(slabs, axis=1)


def _swa_in_specs(nc, clamp):
    t = WINDOW
    qb, kb, vb = SEC_B // MIX, (SEC_B + MIX) // LANES, (SEC_B + MIX + LANES) // LANES

    def cur(i):
        return jnp.minimum(i, nc - 1) if clamp else i

    def prev(i):
        return jnp.maximum(cur(i) - 1, 0)

    return [pl.BlockSpec((t, MIX), lambda i: (cur(i), qb)),
            pl.BlockSpec((t, LANES), lambda i: (prev(i), kb)), pl.BlockSpec((t, LANES), lambda i: (cur(i), kb)),
            pl.BlockSpec((t, LANES), lambda i: (prev(i), vb)), pl.BlockSpec((t, LANES), lambda i: (cur(i), vb)),
            pl.BlockSpec((1, LANES), lambda i: (0, 0)),
            pl.BlockSpec((3, t, LANES), lambda i: (0, cur(i), 0)), pl.BlockSpec((3, t, LANES), lambda i: (0, prev(i), 0))]


def swa_fwd(proj, sink_row, tables, *, name):
    s = proj.shape[0]
    nc = s // WINDOW

    def body(q_ref, kp_ref, kc_ref, vp_ref, vc_ref, sink_ref, tq_ref, tp_ref, o_ref):
        prev_ok = pl.program_id(0) > 0
        blocks = [r[...].astype(f32) for r in (q_ref, kp_ref, kc_ref, vp_ref, vc_ref)]
        o_ref[...] = _swa_block(*blocks, sink_ref[...],
                                tq_ref[...], tp_ref[...], prev_ok, True).astype(bf16)

    return pl.pallas_call(
        body, name=name, grid=(nc,), in_specs=_swa_in_specs(nc, False),
        out_specs=pl.BlockSpec((WINDOW, MIX), lambda i: (i, 0)), out_shape=jax.ShapeDtypeStruct((s, MIX), bf16),
        compiler_params=_params(1),
    )(proj, proj, proj, proj, proj, sink_row, tables, tables)


def swa_bwd(proj, sink_row, tables, d_out, dproj, *, side=None, name):
    s = proj.shape[0]
    nc = s // WINDOW
    t = WINDOW

    def body(q_ref, kp_ref, kc_ref, vp_ref, vc_ref, sink_ref, tq_ref, tp_ref, do_ref, _,
             dp_ref, dsink_ref, cq_ref, ck_ref, cv_ref):
        i = pl.program_id(0)

        @pl.when(i == 0)
        def _():
            dsink_ref[...] = jnp.zeros_like(dsink_ref)

        def write(dk_prev, dv_prev):
            dp_ref[:, :MIX] = cq_ref[...].astype(bf16)
            dp_ref[:, MIX:MIX + LANES] = (ck_ref[...] + dk_prev).astype(bf16)
            dp_ref[:, MIX + LANES:] = (cv_ref[...] + dv_prev).astype(bf16)

        @pl.when(i < nc)
        def _():
            fn = functools.partial(_swa_block, table_q=tq_ref[...], table_p=tp_ref[...], prev_ok=i > 0, wide=False)
            blocks = [r[...].astype(f32) for r in (q_ref, kp_ref, kc_ref, vp_ref, vc_ref)]
            _, vjp = jax.vjp(fn, *blocks, sink_ref[...])
            dq, dkp, dkc, dvp, dvc, dsink = vjp(do_ref[...])
            dsink_ref[...] += dsink

            @pl.when(i > 0)
            def _():
                write(dkp, dvp)

            cq_ref[...] = dq
            ck_ref[...] = dkc
            cv_ref[...] = dvc

        @pl.when(i == nc)
        def _():
            write(0.0, 0.0)

    return host_call(
        side, body, name=name, grid=(nc + 1,),
        in_specs=_swa_in_specs(nc, True) + [pl.BlockSpec((t, MIX), lambda i: (jnp.minimum(i, nc - 1), 0)),
                                            pl.BlockSpec(memory_space=pl.ANY)],
        out_specs=[pl.BlockSpec((t, MIX + 2 * LANES), lambda i: (jnp.maximum(i - 1, 0), SEC_B // (MIX + 2 * LANES))),
                   pl.BlockSpec((1, LANES), lambda i: (0, 0))],
        out_shape=[jax.ShapeDtypeStruct(dproj.shape, bf16), jax.ShapeDtypeStruct((1, LANES), f32)],
        scratch_shapes=[pltpu.VMEM((t, MIX), f32), pltpu.VMEM((t, LANES), f32), pltpu.VMEM((t, LANES), f32)],
        aliases={9: 0}, args=(proj, proj, proj, proj, proj, sink_row, tables, tables, d_out, dproj))


CONV_PAD = 16


def _conv_taps(xp, rows):
    off = CONV_PAD - (DN_CONV - 1)
    return [xp[off + i:off + i + rows] for i in range(DN_CONV)]


def _conv_pre(taps, w):
    pre = taps[0] * w[0:1]
    for i in range(1, DN_CONV):
        pre = pre + taps[i] * w[i:i + 1]
    return pre


def conv_fwd(proj, conv_w8, *, name):
    s = proj.shape[0]
    wq = 3 * MIX
    ts = _tile(s, 512)
    nb = ts // CONV_PAD

    def body(x_ref, prev_ref, w_ref, o_ref):
        prev = jnp.where(pl.program_id(0) > 0, prev_ref[...].astype(f32), 0.0)
        pre = _conv_pre(_conv_taps(jnp.concatenate([prev, x_ref[...].astype(f32)], axis=0), ts), w_ref[...])
        o_ref[...] = pre * _sigmoid(pre)

    return pl.pallas_call(
        body, name=name, grid=(s // ts,),
        in_specs=[pl.BlockSpec((ts, wq), lambda i: (i, SEC_QKV // wq)),
                  pl.BlockSpec((CONV_PAD, wq), lambda i: (jnp.maximum(i * nb - 1, 0), SEC_QKV // wq)),
                  pl.BlockSpec((CONV_PAD, wq), lambda i: (0, 0))],
        out_specs=pl.BlockSpec((ts, wq), lambda i: (i, 0)), out_shape=jax.ShapeDtypeStruct((s, wq), f32),
        compiler_params=_params(1),
    )(proj, proj, conv_w8)


def conv_bwd(proj, conv_w8, dxc, dproj, *, side=None, name):
    s = proj.shape[0]
    wq = 3 * MIX
    ts = _tile(s, 512)
    nb = ts // CONV_PAD
    nt = s // ts
    last_blk = s // CONV_PAD - 1

    def body(x_ref, prev_ref, next_ref, w_ref, d_ref, dnext_ref, _, dp_ref, dw_ref):
        i = pl.program_id(0)

        @pl.when(i == 0)
        def _():
            dw_ref[...] = jnp.zeros_like(dw_ref)

        w = w_ref[...]
        prev = jnp.where(i > 0, prev_ref[...].astype(f32), 0.0)
        more = i < nt - 1
        xp = jnp.concatenate([prev, x_ref[...].astype(f32), jnp.where(more, next_ref[...].astype(f32), 0.0)], axis=0)
        taps = _conv_taps(xp, ts + CONV_PAD)
        pre = _conv_pre(taps, w)
        sig = _sigmoid(pre)
        dxc_ext = jnp.concatenate([d_ref[...], jnp.where(more, dnext_ref[...], 0.0)], axis=0)
        dpre = dxc_ext * sig * (1.0 + pre * (1.0 - sig))
        d_raw = jnp.zeros((ts, wq), f32)
        dws = []
        for k in range(DN_CONV):
            shift = DN_CONV - 1 - k
            d_raw = d_raw + dpre[shift:shift + ts] * w[k:k + 1]
            dws.append(jnp.sum(dpre[:ts] * taps[k][:ts], axis=0, keepdims=True))
        dp_ref[...] = d_raw.astype(bf16)
        dw_ref[...] += jnp.concatenate(dws + [jnp.zeros((CONV_PAD - DN_CONV, wq), f32)], axis=0)

    sec = SEC_QKV // wq
    return host_call(
        side, body, name=name, grid=(nt,),
        in_specs=[pl.BlockSpec((ts, wq), lambda i: (i, sec)),
                  pl.BlockSpec((CONV_PAD, wq), lambda i: (jnp.maximum(i * nb - 1, 0), sec)),
                  pl.BlockSpec((CONV_PAD, wq), lambda i: (jnp.minimum((i + 1) * nb, last_blk), sec)),
                  pl.BlockSpec((CONV_PAD, wq), lambda i: (0, 0)),
                  pl.BlockSpec((ts, wq), lambda i: (i, 0)),
                  pl.BlockSpec((CONV_PAD, wq), lambda i: (jnp.minimum((i + 1) * nb, last_blk), 0)),
                  pl.BlockSpec(memory_space=pl.ANY)],
        out_specs=[pl.BlockSpec((ts, wq), lambda i: (i, sec)), pl.BlockSpec((CONV_PAD, wq), lambda i: (0, 0))],
        out_shape=[jax.ShapeDtypeStruct(dproj.shape, bf16), jax.ShapeDtypeStruct((CONV_PAD, wq), f32)],
        scratch_shapes=[], aliases={6: 0}, args=(proj, proj, proj, conv_w8, dxc, dxc, dproj))


def _dn_chunk(state, xc, z, ba, alog_row, dtb_row, norm_row, t_known):
    c, nh = DN_CHUNK, DN_HEADS
    n = c * nh
    row = lax.broadcasted_iota(jnp.int32, (n, n), 0)
    col = lax.broadcasted_iota(jnp.int32, (n, n), 1)
    same_head = (row // c) == (col // c)
    tril, strict = same_head & (row >= col), same_head & (row > col)
    tril_c = lax.broadcasted_iota(jnp.int32, (c, c), 0) >= lax.broadcasted_iota(jnp.int32, (c, c), 1)
    beta_all = _sigmoid(ba)
    g_all = -jnp.exp(alog_row) * jax.nn.softplus(ba + dtb_row)
    gc_all = hdot(tril_c.astype(f32), g_all)
    gc_t = gc_all.T

    def stack(piece):
        return jnp.concatenate([piece(h) for h in range(nh)], axis=0)

    q = stack(lambda h: xc[:, h * DN_HD:(h + 1) * DN_HD])
    k = stack(lambda h: xc[:, MIX + h * DN_HD:MIX + (h + 1) * DN_HD])
    v = stack(lambda h: xc[:, 2 * MIX + h * DN_HD:2 * MIX + (h + 1) * DN_HD])
    zs = stack(lambda h: z[:, h * DN_HD:(h + 1) * DN_HD])
    q = q * lax.rsqrt(jnp.sum(q * q, axis=-1, keepdims=True) + NORM_EPS) * (DN_HD ** -0.5)
    k = k * lax.rsqrt(jnp.sum(k * k, axis=-1, keepdims=True) + NORM_EPS)
    beta = stack(lambda h: _lane_col(beta_all, h))
    g_cols = [_lane_col(gc_all, nh + h) for h in range(nh)]
    g_col = jnp.concatenate(g_cols, axis=0)
    g_row = jnp.concatenate([gc_t[nh + h:nh + h + 1, :] for h in range(nh)], axis=1)
    g_last = stack(lambda h: jnp.broadcast_to(g_cols[h][c - 1:c, :], (c, 1)))
    decay = jnp.where(tril, jnp.exp(jnp.where(tril, g_col - g_row, 0.0)), 0.0)
    kb = k * beta
    low = jnp.where(strict, bdot(kb, k, 1, 1) * decay, 0.0)
    t_inv = tri_inv(low, c) if t_known is None else tri_inv_known(low, t_known)
    e_gc = jnp.exp(g_col)
    uw = bdot(t_inv, jnp.concatenate([v * beta, kb * e_gc], axis=1), 1, 0)
    u, w = uw[:, :DN_HD], uw[:, DN_HD:]
    attn = bdot(q, k, 1, 1) * decay
    own = (lax.broadcasted_iota(jnp.int32, (n, nh * DN_HD), 1) // DN_HD
           == lax.broadcasted_iota(jnp.int32, (n, nh * DN_HD), 0) // c)

    def spread(a):
        return jnp.where(own, jnp.concatenate([a] * nh, axis=1), 0.0)

    v_new = u - bdot(spread(w), state, 1, 0)
    o = bdot(spread(q * e_gc), state, 1, 0) + bdot(attn, v_new, 1, 0)
    keep = stack(lambda h: jnp.broadcast_to(jnp.exp(g_cols[h][c - 1:c, :]), (DN_HD, 1)))
    new_state = state * keep + bdot(spread(k * jnp.exp(g_last - g_col)), v_new, 0, 0)
    out = _rms(o, norm_row) * (zs * _sigmoid(zs))
    return new_state, jnp.concatenate([out[h * c:(h + 1) * c] for h in range(nh)], axis=1), t_inv


DN_STEP = 4 * DN_CHUNK


def _dn_step(state, xc, z, ba, alog_row, dtb_row, norm_row, t_known=None):
    outs, t_invs = [], []
    for c in range(DN_STEP // DN_CHUNK):
        rows = slice(c * DN_CHUNK, (c + 1) * DN_CHUNK)
        state, out, t_inv = _dn_chunk(state, xc[rows], z[rows], ba[rows], alog_row, dtb_row, norm_row,
                                      None if t_known is None else t_known[c])
        outs.append(out)
        t_invs.append(t_inv)
    return state, jnp.concatenate(outs, axis=0), jnp.stack(t_invs)


def _dn_specs(ts, order):
    zb = SEC_Z // MIX
    return [pl.BlockSpec((ts, 3 * MIX), lambda i: (order(i), 0)),
            pl.BlockSpec((ts, MIX), lambda i: (order(i), zb)),
            pl.BlockSpec((ts, LANES), lambda i: (order(i), 0)),
            pl.BlockSpec((1, LANES), lambda i: (0, 0)), pl.BlockSpec((1, LANES), lambda i: (0, 0)),
            pl.BlockSpec((1, LANES), lambda i: (0, 0))]


def dn_fwd(xc, proj, p_ba, alog_row, dtb_row, norm_row, *, side=None, name):
    s = xc.shape[0]
    ts = _tile(s, 512)
    n_step = ts // DN_STEP

    per_step = DN_STEP // DN_CHUNK
    n_tri = DN_HEADS * DN_CHUNK

    def body(xc_ref, z_ref, ba_ref, al_ref, dt_ref, nr_ref, o_ref, st_ref, tri_ref, state_ref):
        @pl.when(pl.program_id(0) == 0)
        def _():
            state_ref[...] = jnp.zeros_like(state_ref)

        def step(c, carry):
            rows = pl.ds(pl.multiple_of(c * DN_STEP, DN_STEP), DN_STEP)
            st_ref[c] = state_ref[...]
            new_state, out, t_invs = _dn_step(state_ref[...], xc_ref[rows, :], z_ref[rows, :].astype(f32), ba_ref[rows, :],
                                              al_ref[...], dt_ref[...], nr_ref[...])
            state_ref[...] = new_state
            o_ref[rows, :] = out.astype(bf16)
            tri_ref[pl.ds(c * per_step, per_step)] = t_invs
            return carry
        lax.fori_loop(0, n_step, step, 0)

    return host_call(
        side, body, name=name, grid=(s // ts,), in_specs=_dn_specs(ts, lambda i: i),
        out_specs=[pl.BlockSpec((ts, MIX), lambda i: (i, 0)),
                   pl.BlockSpec((n_step, DN_HEADS * DN_HD, DN_HD), lambda i: (i, 0, 0)),
                   pl.BlockSpec((n_step * per_step, n_tri, n_tri), lambda i: (i, 0, 0))],
        out_shape=[jax.ShapeDtypeStruct((s, MIX), bf16),
                   jax.ShapeDtypeStruct((s // DN_STEP, DN_HEADS * DN_HD, DN_HD), f32),
                   jax.ShapeDtypeStruct((s // DN_CHUNK, n_tri, n_tri), f32)],
        scratch_shapes=[pltpu.VMEM((DN_HEADS * DN_HD, DN_HD), f32)], aliases={},
        args=(xc, proj, p_ba, alog_row, dtb_row, norm_row))


def dn_bwd(xc, proj, p_ba, alog_row, dtb_row, norm_row, saved, tri, d_out, dproj, *, side=None, name):
    s = xc.shape[0]
    ts = _tile(s, 512)
    n_step = ts // DN_STEP
    nt = s // ts

    per_step = DN_STEP // DN_CHUNK
    n_tri = DN_HEADS * DN_CHUNK

    def body(xc_ref, z_ref, ba_ref, al_ref, dt_ref, nr_ref, st_ref, tri_ref, do_ref, _,
             dz_ref, dxc_ref, dba_ref, dal_ref, ddt_ref, dnr_ref, dstate_ref):
        @pl.when(pl.program_id(0) == 0)
        def _():
            dstate_ref[...] = jnp.zeros_like(dstate_ref)
            dal_ref[...] = jnp.zeros_like(dal_ref)
            ddt_ref[...] = jnp.zeros_like(ddt_ref)
            dnr_ref[...] = jnp.zeros_like(dnr_ref)

        def step(it, carry):
            c = n_step - 1 - it
            rows = pl.ds(pl.multiple_of(c * DN_STEP, DN_STEP), DN_STEP)
            t_known = tri_ref[pl.ds(c * per_step, per_step)]
            _, vjp = jax.vjp(lambda *a: _dn_step(*a, t_known=t_known)[:2], st_ref[c], xc_ref[rows, :],
                             z_ref[rows, :].astype(f32), ba_ref[rows, :], al_ref[...], dt_ref[...], nr_ref[...])
            d_in, dxc, dz, dba, dal, ddt, dnr = vjp((dstate_ref[...], do_ref[rows, :]))
            dstate_ref[...] = d_in
            dxc_ref[rows, :] = dxc
            dz_ref[rows, :] = dz.astype(bf16)
            dba_ref[rows, :] = dba.astype(bf16)
            dal_ref[...] += dal
            ddt_ref[...] += ddt
            dnr_ref[...] += dnr
            return carry
        lax.fori_loop(0, n_step, step, 0)

    def rev(i):
        return nt - 1 - i

    specs = _dn_specs(ts, rev)
    vec = pl.BlockSpec((1, LANES), lambda i: (0, 0))
    return host_call(
        side, body, name=name, grid=(nt,),
        in_specs=specs + [pl.BlockSpec((n_step, DN_HEADS * DN_HD, DN_HD), lambda i: (rev(i), 0, 0)),
                          pl.BlockSpec((n_step * per_step, n_tri, n_tri), lambda i: (rev(i), 0, 0)),
                          pl.BlockSpec((ts, MIX), lambda i: (rev(i), 0)), pl.BlockSpec(memory_space=pl.ANY)],
        out_specs=[specs[1], specs[0], specs[2], vec, vec, vec],
        out_shape=[jax.ShapeDtypeStruct(dproj.shape, bf16), jax.ShapeDtypeStruct((s, 3 * MIX), f32),
                   jax.ShapeDtypeStruct((s, LANES), bf16)] + [jax.ShapeDtypeStruct((1, LANES), f32)] * 3,
        scratch_shapes=[pltpu.VMEM((DN_HEADS * DN_HD, DN_HD), f32)], aliases={9: 0},
        args=(xc, proj, p_ba, alog_row, dtb_row, norm_row, saved, tri, d_out, dproj))


def _merge_in_specs(ts, d):
    row = pl.BlockSpec((ts, MIX), lambda i: (i, 0))
    return [row, row, row, pl.BlockSpec((ts, 3 * d), lambda i: (i, SEC_G // (3 * d))),
            pl.BlockSpec((3, MIX, d), lambda i: (0, 0, 0))]


def merge_fwd(out_a, out_b, out_c, proj, w_branch, *, name):
    s, d = out_a.shape[0], w_branch.shape[2]
    ts = _tile(s, 512, 16)

    def body(a_ref, b_ref, c_ref, g_ref, w_ref, o_ref):
        acc = jnp.zeros((ts, d), f32)
        for n, r in enumerate((a_ref, b_ref, c_ref)):
            acc = acc + _sigmoid(g_ref[:, n * d:(n + 1) * d].astype(f32)) * _dg(r[...], w_ref[n], 1, 0)
        o_ref[...] = acc.astype(bf16)

    return pl.pallas_call(
        body, name=name, grid=(s // ts,), in_specs=_merge_in_specs(ts, d),
        out_specs=pl.BlockSpec((ts, d), lambda i: (i, 0)), out_shape=jax.ShapeDtypeStruct((s, d), bf16),
        compiler_params=_params(1),
    )(out_a, out_b, out_c, proj, w_branch)


def merge_bwd(out_a, out_b, out_c, proj, w_branch, d_merged, dproj, *, name):
    s, d = out_a.shape[0], w_branch.shape[2]
    ts = _tile(s, 512, 16)

    def body(a_ref, b_ref, c_ref, g_ref, w_ref, dm_ref, _, dg_ref, da_ref, db_ref, dc_ref, dw_ref):
        @pl.when(pl.program_id(0) == 0)
        def _():
            dw_ref[...] = jnp.zeros_like(dw_ref)

        dm = dm_ref[...]
        for n, (r, dr) in enumerate(((a_ref, da_ref), (b_ref, db_ref), (c_ref, dc_ref))):
            gate = _sigmoid(g_ref[:, n * d:(n + 1) * d].astype(f32))
            branch = _dg(r[...], w_ref[n], 1, 0)
            dg_ref[:, n * d:(n + 1) * d] = (dm * branch * gate * (1.0 - gate)).astype(bf16)
            d_branch = dm * gate
            dr[...] = _dg(d_branch, w_ref[n], 1, 1)
            dw_ref[n] += _dg(r[...], d_branch, 0, 0)

    specs = _merge_in_specs(ts, d)
    row_f = pl.BlockSpec((ts, MIX), lambda i: (i, 0))
    return pl.pallas_call(
        body, name=name, grid=(s // ts,),
        in_specs=specs + [pl.BlockSpec((ts, d), lambda i: (i, 0)), pl.BlockSpec(memory_space=pl.ANY)],
        out_specs=[specs[3], row_f, row_f, row_f, specs[4]],
        out_shape=[jax.ShapeDtypeStruct(dproj.shape, bf16)] + [jax.ShapeDtypeStruct((s, MIX), f32)] * 3
        + [jax.ShapeDtypeStruct(w_branch.shape, f32)],
        input_output_aliases={6: 0}, compiler_params=_params(1),
    )(out_a, out_b, out_c, proj, w_branch, d_merged, dproj)


def swiglu_fwd(gu, *, name):
    g2, s, w = gu.shape
    ng = g2 // 2
    ts = _tile(s, 1024, 16)

    def body(g_ref, u_ref, o_ref):
        g = g_ref[...].astype(f32)
        o_ref[...] = (g * _sigmoid(g) * u_ref[...].astype(f32)).astype(bf16)

    return pl.pallas_call(
        body, name=name, grid=(s // ts, ng),
        in_specs=[pl.BlockSpec((None, ts, w), lambda i, j: (j, i, 0)),
                  pl.BlockSpec((None, ts, w), lambda i, j: (ng + j, i, 0))],
        out_specs=pl.BlockSpec((None, ts, w), lambda i, j: (j, i, 0)), out_shape=jax.ShapeDtypeStruct((ng, s, w), bf16),
        compiler_params=_params(2),
    )(gu, gu)


def swiglu_bwd(gu, d_act, *, name):
    g2, s, w = gu.shape
    ng = g2 // 2
    ts = _tile(s, 1024, 16)

    def body(g_ref, u_ref, d_ref, dg_ref, du_ref):
        g, d = g_ref[...].astype(f32), d_ref[...].astype(f32)
        sig = _sigmoid(g)
        dg_ref[...] = (d * u_ref[...].astype(f32) * sig * (1.0 + g * (1.0 - sig))).astype(bf16)
        du_ref[...] = (d * g * sig).astype(bf16)

    lo = pl.BlockSpec((None, ts, w), lambda i, j: (j, i, 0))
    return pl.pallas_call(
        body, name=name, grid=(s // ts, ng),
        in_specs=[lo, pl.BlockSpec((None, ts, w), lambda i, j: (ng + j, i, 0)), lo], out_specs=[lo, lo],
        out_shape=[jax.ShapeDtypeStruct((ng, s, w), bf16)] * 2, compiler_params=_params(2),
    )(gu, gu, d_act)


def adamw(w, m, v, g_parts, *, name):
    n_layers = len(g_parts)
    n_parts, r, cols = g_parts[0].shape
    lanes = -(-cols // LANES) * LANES
    tr = _tile(r, max(16, (128 * 1024) // lanes), 16)
    nr = r // tr

    def body(w_ref, m_ref, v_ref, *rest):
        gp_refs, (g_ref, d_ref, nm_ref, nv_ref) = rest[:n_layers], rest[n_layers:]
        layer = pl.program_id(0)
        g = jnp.zeros((tr, cols), f32)
        for l, gp_ref in enumerate(gp_refs):
            g_l = gp_ref[0].astype(f32)
            for k in range(1, n_parts):
                g_l = g_l + gp_ref[k].astype(f32)
            g = jnp.where(layer == l, g_l, g)
        nm = ADAM_B1 * m_ref[...] + (1.0 - ADAM_B1) * g
        nv = ADAM_B2 * v_ref[...] + (1.0 - ADAM_B2) * jnp.square(g)
        m_hat = nm / (1.0 - ADAM_B1 ** ADAM_STEP)
        v_hat = nv / (1.0 - ADAM_B2 ** ADAM_STEP)
        g_ref[...] = g
        d_ref[...] = -ADAM_LR * (m_hat / (jnp.sqrt(v_hat) + ADAM_EPS) + ADAM_WD * w_ref[...])
        nm_ref[...] = nm
        nv_ref[...] = nv

    row = pl.BlockSpec((tr, cols), lambda l, i: (l * nr + i, 0))

    def parts_spec(own):
        return pl.BlockSpec((n_parts, tr, cols),
                            lambda l, i: (0, jnp.where(l == own, i, jnp.where(l < own, 0, nr - 1)), 0))

    return pl.pallas_call(
        body, name=name, grid=(n_layers, nr), in_specs=[row, row, row] + [parts_spec(l) for l in range(n_layers)],
        out_specs=[row] * 4, out_shape=[jax.ShapeDtypeStruct(w.shape, f32)] * 4, compiler_params=_params(2),
    )(w, m, v, *g_parts)


def _mesh_pos():
    return lax.axis_index("x"), lax.axis_index("y"), lax.axis_index("c")


def _dev_index(p):
    return 4 * p[0] + 2 * p[1] + p[2]


class Exchange:
    def __init__(self, kind, arrays):
        self.kind, self.arrays, self.n = kind, list(arrays), len(arrays)
        self.specs = [pl.BlockSpec(memory_space=pl.ANY)] * self.n
        self.out_shapes = [jax.ShapeDtypeStruct(((N_DEV,) if kind == "gather" else ()) + a.shape, a.dtype)
                           for a in self.arrays]
        self.scratch = [pltpu.SemaphoreType.DMA((self.n, N_DEV - 1)), pltpu.SemaphoreType.DMA((self.n, N_DEV - 1)),
                        pltpu.SemaphoreType.DMA((self.n,))]

    def _copies(self, in_refs, out_refs, sems, with_arrivals):
        send_sems, recv_sems, local_sems = sems
        x, y, c = _mesh_pos()
        mine = _dev_index((x, y, c))

        def src(a, slab):
            return in_refs[a] if self.kind == "gather" else in_refs[a].at[slab]

        local = [pltpu.make_async_copy(src(a, mine), out_refs[a].at[mine], local_sems.at[a]) for a in range(self.n)]
        sends, arrivals = [], []
        for k in range(1, N_DEV):
            peer = (1 - x if k & 4 else x, 1 - y if k & 2 else y, 1 - c if k & 1 else c)
            theirs = _dev_index(peer)
            for a in range(self.n):
                to = dict(send_sem=send_sems.at[a, k - 1], recv_sem=recv_sems.at[a, k - 1], device_id=peer,
                          device_id_type=pl.DeviceIdType.MESH)
                sends.append(pltpu.make_async_remote_copy(src_ref=src(a, theirs), dst_ref=out_refs[a].at[mine], **to))
                if with_arrivals:
                    arrivals.append(pltpu.make_async_remote_copy(src_ref=src(a, theirs),
                                                                 dst_ref=out_refs[a].at[theirs], **to))
        return local, sends, arrivals

    def start(self, in_refs, out_refs, sems):
        local, sends, _ = self._copies(in_refs, out_refs, sems, False)
        for cp in local + sends:
            cp.start()

    def wait(self, in_refs, out_refs, sems):
        local, sends, arrivals = self._copies(in_refs, out_refs, sems, True)
        for cp in arrivals:
            cp.wait_recv()
        for cp in sends:
            cp.wait_send()
        for cp in local:
            cp.wait()

    def run_around(self, grid, in_refs, out_refs, sems, *, before):
        at = None
        for axis, size in enumerate(grid):
            hit = pl.program_id(axis) == (0 if before else size - 1)
            at = hit if at is None else at & hit

        @pl.when(at)
        def _():
            (self.start if before else self.wait)(in_refs, out_refs, sems)


def host_call(side, body, *, name, grid, in_specs, out_specs, out_shape, scratch_shapes, args, aliases):
    n_in, n_out = len(in_specs), len(out_specs)
    if side is None:
        kernel_body = body
    else:
        n = side.n
        in_specs, args = in_specs + side.specs, tuple(args) + tuple(side.arrays)
        out_specs, out_shape = out_specs + side.specs, out_shape + side.out_shapes
        scratch_shapes = scratch_shapes + side.scratch

        def kernel_body(*refs):
            ins, side_in = refs[:n_in], refs[n_in:n_in + n]
            outs, side_out = refs[n_in + n:n_in + n + n_out], refs[n_in + n + n_out:n_in + 2 * n + n_out]
            scratch, sems = refs[n_in + 2 * n + n_out:-3], refs[-3:]
            side.run_around(grid, side_in, side_out, sems, before=True)
            body(*ins, *outs, *scratch)
            side.run_around(grid, side_in, side_out, sems, before=False)

    outs = pl.pallas_call(
        kernel_body, name=name, grid=grid, in_specs=in_specs, out_specs=out_specs, out_shape=out_shape,
        scratch_shapes=scratch_shapes, input_output_aliases=aliases, compiler_params=_params(len(grid)),
    )(*args)
    return outs if side is None else (outs[:n_out], outs[n_out:])


def all_gather(blocks, *, name):
    n = len(blocks)
    any_spec = pl.BlockSpec(memory_space=pl.ANY)

    def body(*refs):
        ins, outs = refs[:n], refs[n:2 * n]
        send_sems, recv_sems, local_sems = refs[2 * n:]
        x, y, c = _mesh_pos()
        me, sibling = (x, y, c), (x, y, 1 - c)
        chips = [(1 - x, y), (x, 1 - y), (1 - x, 1 - y)]

        def copy(a, k, block, to, src=None):
            dst = outs[a].at[_dev_index(block)]
            return pltpu.make_async_remote_copy(
                src_ref=dst if src is None else src, dst_ref=dst, send_sem=send_sems.at[a, k],
                recv_sem=recv_sems.at[a, k], device_id=to, device_id_type=pl.DeviceIdType.MESH)

        mine = [pltpu.make_async_copy(ins[a], outs[a].at[_dev_index(me)], local_sems.at[a]) for a in range(n)]
        for cp in mine:
            cp.start()
        first = []
        for a in range(n):
            first.append(copy(a, 0, me, sibling, src=ins[a]))
            first += [copy(a, 1 + j, me, (*chip, c), src=ins[a]) for j, chip in enumerate(chips)]
        for cp in first:
            cp.start()
        passed = []
        for j, chip in enumerate(chips):
            for a in range(n):
                copy(a, 1 + j, (*chip, c), me).wait_recv()
                fwd = copy(a, 4 + j, (*chip, c), sibling)
                fwd.start()
                passed.append(fwd)
        for a in range(n):
            copy(a, 0, sibling, me).wait_recv()
            for j, chip in enumerate(chips):
                copy(a, 4 + j, (*chip, 1 - c), me).wait_recv()
        for cp in first + passed:
            cp.wait_send()
        for cp in mine:
            cp.wait()

    return pl.pallas_call(
        body, name=name, in_specs=[any_spec] * n, out_specs=[any_spec] * n,
        out_shape=[jax.ShapeDtypeStruct((N_DEV,) + b.shape, b.dtype) for b in blocks],
        scratch_shapes=[pltpu.SemaphoreType.DMA((n, 7)), pltpu.SemaphoreType.DMA((n, 7)),
                        pltpu.SemaphoreType.DMA((n,))],
    )(*blocks)


def exchange(jobs, *, name):
    total = sum(j.n for j in jobs)

    def body(*refs):
        ins, outs, sems = refs[:total], refs[total:2 * total], refs[2 * total:]
        pieces, off = [], 0
        for i, j in enumerate(jobs):
            pieces.append((ins[off:off + j.n], outs[off:off + j.n], sems[3 * i:3 * i + 3]))
            off += j.n
        for j, piece in zip(jobs, pieces):
            j.start(*piece)
        for j, piece in zip(jobs, pieces):
            j.wait(*piece)

    outs = pl.pallas_call(
        body, name=name, in_specs=[s for j in jobs for s in j.specs], out_specs=[s for j in jobs for s in j.specs],
        out_shape=[s for j in jobs for s in j.out_shapes], scratch_shapes=[s for j in jobs for s in j.scratch],
    )(*[a for j in jobs for a in j.arrays])
    split, off = [], 0
    for j in jobs:
        split.append(outs[off:off + j.n])
        off += j.n
    return split


def _rows128(arr):
    flat = arr.reshape(-1)
    rows = -(-flat.shape[0] // (8 * LANES)) * 8
    return jnp.pad(flat, (0, rows * LANES - flat.shape[0])).reshape(rows, LANES)


def _pad_lanes(row, width=LANES, at=0):
    return jnp.pad(row, (at, width - at - row.shape[0])).reshape(1, width)


def _w_in_sections(got):
    d = got.shape[1]
    wi = jnp.transpose(got, (1, 0, 2)).reshape(d, -1)
    w_main = jnp.concatenate([wi[:, :C_B], wi[:, C_Z:C_BA], wi[:, C_QKV:C_Z], wi[:, C_G:], wi[:, C_B:C_QKV]], axis=1)
    return w_main, jnp.pad(wi[:, C_BA:C_G], ((0, 0), (0, LANES - (C_G - C_BA))))


def _w_in_parts(gw_main, gw_ba):
    d = gw_main.shape[0]
    full = jnp.concatenate([gw_main[:, SEC_A:SEC_Z], gw_main[:, SEC_B:], gw_main[:, SEC_QKV:SEC_G],
                            gw_main[:, SEC_Z:SEC_QKV], gw_ba[:, :C_G - C_BA], gw_main[:, SEC_G:SEC_B]], axis=1)
    return jnp.transpose(full.reshape(d, N_DEV, -1), (1, 0, 2))


def kernel(x, positions, attn_norm, w_in, sgu_ln_g, sgu_ln_b, sgu_w, sgu_b, attn_sinks, dn_conv_w, dn_a_log, dn_dt_bias, dn_norm, w_branch, w_out, ffn_norm, w_gate_up, w_down, final_norm, loss_target, m_attn_norm, m_w_in, m_sgu_ln_g, m_sgu_ln_b, m_sgu_w, m_sgu_b, m_attn_sinks, m_dn_conv_w, m_dn_a_log, m_dn_dt_bias, m_dn_norm, m_w_branch, m_w_out, m_ffn_norm, m_w_gate_up, m_w_down, m_final_norm, v_attn_norm, v_w_in, v_sgu_ln_g, v_sgu_ln_b, v_sgu_w, v_sgu_b, v_attn_sinks, v_dn_conv_w, v_dn_a_log, v_dn_dt_bias, v_dn_norm, v_w_branch, v_w_out, v_ffn_norm, v_w_gate_up, v_w_down, v_final_norm):
    given = dict(locals())
    depth, d_model = attn_norm.shape
    s = x.shape[1]
    x2 = x.reshape(s, d_model)
    target = loss_target.reshape(s, d_model)
    posf = positions.reshape(s, 1).astype(f32)
    inv_freq = ROPE_THETA ** (-jnp.arange(0, ROPE_DIM, 2, dtype=f32) / ROPE_DIM)
    inv_head = jnp.concatenate([inv_freq, inv_freq, jnp.zeros((SWA_HD - ROPE_DIM,), f32)])
    tables = rope_tables(posf, jnp.tile(inv_head, LANES // SWA_HD).reshape(1, LANES), name="rope_tables")

    assert depth == 2, depth
    gathered = dict(zip([("w_in", 0), ("dn_conv_w", 0), ("dn_conv_w", 1)], all_gather(
        [w_in[0].astype(bf16), dn_conv_w[0], dn_conv_w[1]], name="gather_first")))
    riders = {"l0_in_proj": [("w_branch", 0), ("w_out", 0), ("w_down", 0)],
              "l0_deltanet": [("w_gate_up", 0), ("w_in", 1)],
              "l1_deltanet": [("w_branch", 1), ("w_out", 1), ("w_gate_up", 1), ("w_down", 1)]}

    def gathering(host, call, *args, **kw):
        if host not in riders:
            return call(*args, name=host, **kw)
        side = Exchange("gather", [given[n][l].astype(bf16) for n, l in riders[host]])
        out, got = call(*args, side=side, name=host, **kw)
        gathered.update(zip(riders[host], got))
        return out

    layers, saved = [], []
    h_in = x2
    for l in range(depth):
        t = f"l{l}_"
        w_main, w_ba = _w_in_sections(gathered["w_in", l])
        conv_full = jnp.transpose(gathered["dn_conv_w", l], (1, 0, 2)).reshape(DN_CONV, -1)
        p = dict(
            w_main=w_main, w_ba=w_ba, conv_w8=jnp.pad(conv_full, ((0, CONV_PAD - DN_CONV), (0, 0))),
            attn_norm=attn_norm[l].reshape(1, -1), ffn_norm=ffn_norm[l].reshape(1, -1),
            ln_g=sgu_ln_g[l].reshape(1, -1), ln_b=sgu_ln_b[l].reshape(1, -1), sgu_w=sgu_w[l], sgu_bt=sgu_b[l].T,
            sink_row=_pad_lanes(attn_sinks[l]), alog_row=_pad_lanes(dn_a_log[l], at=DN_HEADS),
            dtb_row=_pad_lanes(dn_dt_bias[l], at=DN_HEADS), norm_row=dn_norm[l].reshape(1, -1))
        layers.append(p)
        h = rmsnorm_fwd(h_in, p["attn_norm"], name=t + "attn_norm")
        proj = gathering(t + "in_proj", matmul, h, p["w_main"], "nn", bf16)
        p_ba = matmul(h, p["w_ba"], "nn", f32, name=t + "in_proj_ba")
        out_a = sgu_fwd(proj, p["ln_g"], p["ln_b"], p["sgu_w"], p["sgu_bt"], name=t + "sgu")
        out_b = swa_fwd(proj, p["sink_row"], tables, name=t + "swa")
        xc = conv_fwd(proj, p["conv_w8"], name=t + "dn_conv")
        out_c, states, tri = gathering(t + "deltanet", dn_fwd, xc, proj, p_ba, p["alog_row"], p["dtb_row"], p["norm_row"])
        p.update(w_branch=jnp.transpose(gathered["w_branch", l], (1, 2, 0, 3)).reshape(3, MIX, d_model),
                 w_out=gathered["w_out", l].reshape(d_model, d_model),
                 w_gu=gathered["w_gate_up", l],
                 w_down=gathered["w_down", l].reshape(N_DEV // 2, -1, d_model))
        merged = merge_fwd(out_a, out_b, out_c, proj, p["w_branch"], name=t + "merge")
        x_mid = matmul(merged, p["w_out"], "nn", f32, residual=h_in, tn=1024, name=t + "out_proj")
        h2 = rmsnorm_fwd(x_mid, p["ffn_norm"], name=t + "ffn_norm")
        gu = matmul(h2, p["w_gu"], "nn", bf16, group="n", name=t + "gate_up")
        act = swiglu_fwd(gu, name=t + "swiglu")
        x_out = matmul(act, p["w_down"], "nn", f32, residual=x_mid, group="k", tn=1024, name=t + "down")
        saved.append(dict(x_in=h_in, h=h, proj=proj, p_ba=p_ba, out_a=out_a, out_b=out_b, out_c=out_c, xc=xc,
                          states=states, tri=tri, merged=merged, x_mid=x_mid, h2=h2, gu=gu, act=act))
        h_in = x_out

    dx, d_final_norm, loss_row = loss_head(h_in, final_norm.reshape(1, -1), target, name="loss_head")
    loss = lax.psum(loss_row[0, 0], MESH_AXES)

    shard_names = ["w_in", "dn_conv_w", "w_branch", "w_out", "w_gate_up", "w_down"]
    rep_names = ["attn_norm", "sgu_ln_g", "sgu_ln_b", "sgu_w", "sgu_b", "attn_sinks", "dn_a_log", "dn_dt_bias",
                 "dn_norm", "ffn_norm"]
    parts, received, per_layer = {}, {}, []
    senders = {"l0_b_swa": [("w_gate_up", 1), ("w_down", 1), ("w_out", 1), ("w_branch", 1)],
               "l0_b_deltanet": [("w_in", 1), ("w_gate_up", 0), ("w_down", 0)],
               "l0_b_dn_conv": [("w_out", 0), ("w_branch", 0)],
               "l0_b_in_proj_dx": [("w_in", 0)]}

    def scattering(host, call, *args, **kw):
        if host not in senders:
            return call(*args, name=host, **kw)
        out, got = call(*args, side=Exchange("scatter", [parts[key] for key in senders[host]]), name=host, **kw)
        received.update(zip(senders[host], got))
        return out

    for l in reversed(range(depth)):
        p, sv, t = layers[l], saved[l], f"l{l}_b_"
        d_act = matmul(dx, p["w_down"], "nt", bf16, group="n", name=t + "down_dx")
        gw_down = matmul(sv["act"], dx, "tn", bf16, group="m", tk=2048, tn=512, name=t + "down_dw")
        d_gate, d_up = swiglu_bwd(sv["gu"], d_act, name=t + "swiglu")
        gw_gu = jnp.concatenate([matmul(sv["h2"], d_half, "tn", bf16, group="n", tk=2048, name=t + "gate_up_dw" + tag)
                                 for d_half, tag in ((d_gate, "_gate"), (d_up, "_up"))], axis=0)
        half = N_DEV // 2
        d_h2 = matmul(d_gate, p["w_gu"][:half], "nt", f32, group="k", tn=1024, name=t + "gate_up_dx_gate")
        d_h2 = matmul(d_up, p["w_gu"][half:], "nt", f32, group="k", tn=1024, residual=d_h2, name=t + "gate_up_dx_up")
        dx_mid, g_ffn = rmsnorm_bwd(sv["x_mid"], p["ffn_norm"], d_h2, dx, name=t + "ffn_norm")
        d_merged = matmul(dx_mid, p["w_out"], "nt", f32, tn=1024, name=t + "out_proj_dx")
        gw_out = matmul(sv["merged"], dx_mid, "tn", bf16, tk=2048, tn=1024, name=t + "out_proj_dw")
        dproj = lax.empty((s, W_MAIN), bf16)
        dproj, d_a, d_b, d_c, gw_branch = merge_bwd(sv["out_a"], sv["out_b"], sv["out_c"], sv["proj"], p["w_branch"],
                                                   d_merged, dproj, name=t + "merge")
        parts.update({("w_gate_up", l): gw_gu, ("w_down", l): gw_down.reshape(N_DEV, -1, d_model),
                      ("w_out", l): gw_out.reshape(N_DEV, -1, d_model),
                      ("w_branch", l): jnp.transpose(gw_branch.reshape(3, MIX, N_DEV, -1), (2, 0, 1, 3)).astype(bf16)})
        dproj, g_ln_g, g_ln_b, g_sgu_w, g_sgu_bt = sgu_bwd(sv["proj"], p["ln_g"], p["ln_b"], p["sgu_w"], p["sgu_bt"],
                                                         d_a, dproj, name=t + "sgu")
        dproj, g_sink = scattering(t + "swa", swa_bwd, sv["proj"], p["sink_row"], tables, d_b, dproj)
        dproj, dxc, dba, g_alog, g_dtb, g_dnorm = scattering(
            t + "deltanet", dn_bwd, sv["xc"], sv["proj"], sv["p_ba"], p["alog_row"], p["dtb_row"], p["norm_row"],
            sv["states"], sv["tri"], d_c, dproj)
        dproj, g_conv8 = scattering(t + "dn_conv", conv_bwd, sv["proj"], p["conv_w8"], dxc, dproj)
        gw_main = matmul(sv["h"], dproj, "tn", bf16, tk=2048, name=t + "in_proj_dw")
        gw_ba = matmul(sv["h"], dba, "tn", bf16, tk=2048, name=t + "in_proj_ba_dw")
        parts["w_in", l] = _w_in_parts(gw_main, gw_ba)
        d_h = scattering(t + "in_proj_dx", matmul, dproj, p["w_main"], "nt", f32, tk=2304, tn=1024)
        d_h = matmul(dba, p["w_ba"], "nt", f32, residual=d_h, name=t + "in_proj_ba_dx")
        dx, g_attn = rmsnorm_bwd(sv["x_in"], p["attn_norm"], d_h, dx_mid, name=t + "attn_norm")
        per_layer.append(dict(
            dn_conv_w=jnp.transpose(g_conv8[:DN_CONV].reshape(DN_CONV, N_DEV, -1), (1, 0, 2)),
            attn_norm=g_attn, sgu_ln_g=g_ln_g, sgu_ln_b=g_ln_b, sgu_w=g_sgu_w, sgu_b=g_sgu_bt.T,
            attn_sinks=g_sink[0, :SWA_HEADS], dn_a_log=g_alog[0, DN_HEADS:2 * DN_HEADS],
            dn_dt_bias=g_dtb[0, DN_HEADS:2 * DN_HEADS], dn_norm=g_dnorm, ffn_norm=g_ffn))
    per_layer.reverse()

    conv_parts = jnp.concatenate([pp["dn_conv_w"] for pp in per_layer], axis=1)
    rep_grads = {n: jnp.stack([pp[n].reshape(given[n].shape[1:]) for pp in per_layer]) for n in rep_names}
    rep_grads["final_norm"] = d_final_norm[0]
    rep_names = rep_names + ["final_norm"]
    rep_rows = [_rows128(given[n]).shape[0] for n in rep_names]
    pad_rows = -sum(rep_rows) % 16

    def small_rows(values):
        pieces = [_rows128(values[n]) for n in rep_names]
        return jnp.concatenate(pieces + [jnp.zeros((pad_rows, LANES), f32)], axis=0)

    (got_conv,), (small_all,) = exchange(
        [Exchange("scatter", [conv_parts]), Exchange("gather", [small_rows(rep_grads)])], name="exchange_last")

    results = [{}, {}, {}, {}]
    for n in shard_names:
        shp = given[n].shape
        two = (-1, shp[-1])
        by_layer = [got_conv] if n == "dn_conv_w" else [received[n, l].reshape(N_DEV, -1, shp[-1]) for l in range(depth)]
        outs = adamw(given[n].reshape(two), given["m_" + n].reshape(two), given["v_" + n].reshape(two), by_layer,
                     name="adamw_" + n)
        for res, val in zip(results, outs):
            res[n] = val.reshape(shp)
    outs = adamw(small_rows(given), small_rows({n: given["m_" + n] for n in rep_names}),
                 small_rows({n: given["v_" + n] for n in rep_names}), [small_all], name="adamw_replicated")
    for res, val in zip(results, outs):
        row = 0
        for n, nr in zip(rep_names, rep_rows):
            res[n] = val[row:row + nr].reshape(-1)[:given[n].size].reshape(given[n].shape)
            row += nr
    order = ["attn_norm", "w_in", "sgu_ln_g", "sgu_ln_b", "sgu_w", "sgu_b", "attn_sinks", "dn_conv_w", "dn_a_log",
             "dn_dt_bias", "dn_norm", "w_branch", "w_out", "ffn_norm", "w_gate_up", "w_down", "final_norm"]
    return (loss, dx.reshape(x.shape), *[res[n] for res in results for n in order])
```

```python
import functools

import jax
import jax.numpy as jnp
from jax import lax
from jax.experimental import pallas as pl
from jax.experimental.pallas import tpu as pltpu

f32 = jnp.float32
bf16 = jnp.bfloat16

N_DEV = 8
MESH_AXES = ("x", "y", "c")
NORM_EPS = 1e-6
MIX = 512
SGU_GROUPS, SGU_CHUNK = 4, 128
SWA_HEADS, SWA_KV, SWA_HD, WINDOW = 8, 2, 64, 128
ROPE_THETA, ROPE_DIM = 500000.0, 16
DN_HEADS, DN_HD, DN_CONV, DN_CHUNK = 4, 128, 4, 64
ADAM_LR, ADAM_B1, ADAM_B2, ADAM_EPS, ADAM_WD, ADAM_STEP = 0.001, 0.9, 0.999, 1e-08, 0.01, 10

LANES = 128
VMEM_LIMIT = 56 * 1024 * 1024

SEC_A, SEC_Z, SEC_QKV, SEC_G, SEC_B = 0, 1024, 1536, 3072, 6144
W_MAIN = 6912
C_B, C_QKV, C_Z, C_BA, C_G = 1024, 1792, 3328, 3840, 3848


def _params(n_axes, **kw):
    return pltpu.CompilerParams(dimension_semantics=("arbitrary",) * n_axes, vmem_limit_bytes=VMEM_LIMIT, **kw)


def _tile(n, target, mult=LANES):
    if n <= target:
        return n
    best = None
    for t in range(mult, target + 1, mult):
        if n % t == 0:
            best = t
    assert best is not None, (n, target, mult)
    return best


def _dg(a, b, ca, cb):
    return lax.dot_general(a.astype(bf16), b.astype(bf16), (((ca,), (cb,)), ((), ())), preferred_element_type=f32)


def _dg3(a, b, ca, cb):
    a_hi, b_hi = a.astype(bf16), b.astype(bf16)
    a_lo, b_lo = (a - a_hi.astype(f32)).astype(bf16), (b - b_hi.astype(f32)).astype(bf16)

    def dot(p, q):
        return lax.dot_general(p, q, (((ca,), (cb,)), ((), ())), preferred_element_type=f32)

    return dot(a_hi, b_hi) + (dot(a_hi, b_lo) + dot(a_lo, b_hi))


def _differentiable_dot(core):
    @functools.partial(jax.custom_vjp, nondiff_argnums=(2, 3))
    def dot(a, b, ca, cb):
        return core(a, b, ca, cb)

    def fwd(a, b, ca, cb):
        return core(a, b, ca, cb), (a, b)

    def bwd(ca, cb, res, ct):
        a, b = res
        da = core(ct, b, 1, 1 - cb) if ca == 1 else core(b, ct, 1 - cb, 1)
        db = core(a, ct, 1 - ca, 0) if cb == 0 else core(ct, a, 0, 1 - ca)
        return da, db

    dot.defvjp(fwd, bwd)
    return dot


bdot = _differentiable_dot(_dg)
_hdot = _differentiable_dot(_dg3)


def hdot(a, b, ca=1, cb=0):
    return _hdot(a, b, ca, cb)


@functools.partial(jax.custom_vjp, nondiff_argnums=(1,))
def lroll(x, shift):
    return pltpu.roll(x, shift, 1)


def _lroll_fwd(x, shift):
    return pltpu.roll(x, shift, 1), None


def _lroll_bwd(shift, _, ct):
    return (pltpu.roll(ct, ct.shape[1] - shift, 1),)


lroll.defvjp(_lroll_fwd, _lroll_bwd)


@functools.partial(jax.custom_vjp, nondiff_argnums=(1,))
def tri_inv(low, nil):
    n = low.shape[0]
    row = lax.broadcasted_iota(jnp.int32, (n, n), 0)
    col = lax.broadcasted_iota(jnp.int32, (n, n), 1)
    eye = (row == col).astype(f32)
    m = -low
    p = eye + m
    span = 2
    while span < nil:
        dot = hdot if span == 2 else (lambda a, b: bdot(a, b, 1, 0))
        m = dot(m, m)
        p = p + dot(p, m)
        span *= 2
    return p


def _tri_inv_fwd(low, nil):
    t = tri_inv(low, nil)
    return t, t


def _tri_inv_bwd(nil, t, dt):
    return (-bdot(t, bdot(dt, t, 1, 1), 0, 0),)


tri_inv.defvjp(_tri_inv_fwd, _tri_inv_bwd)


@jax.custom_vjp
def tri_inv_known(low, t):
    return t


def _tri_inv_known_fwd(low, t):
    return t, t


def _tri_inv_known_bwd(t, dt):
    return _tri_inv_bwd(None, t, dt) + (jnp.zeros_like(t),)


tri_inv_known.defvjp(_tri_inv_known_fwd, _tri_inv_known_bwd)


def _sigmoid(x):
    return 1.0 / (1.0 + jnp.exp(-x))


def _rms(x, g):
    return x * lax.rsqrt(jnp.mean(x * x, axis=-1, keepdims=True) + NORM_EPS) * g


def _lane_col(x, lane_idx):
    lane = lax.broadcasted_iota(jnp.int32, x.shape, 1)
    return jnp.sum(jnp.where(lane == lane_idx, x, 0.0), axis=1, keepdims=True)


def matmul(a, b, mode, out_dtype, *, residual=None, group=None, side=None, tm=1024, tn=768, tk=1024, name):
    dims = {"a": ("m", "k") if mode != "tn" else ("k", "m"),
            "b": {"nn": ("k", "n"), "nt": ("n", "k"), "tn": ("k", "n")}[mode], "o": ("m", "n")}
    full, groups = {}, 1
    for arr, key in ((a, "a"), (b, "b")):
        grouped = group in dims[key]
        if grouped:
            groups = arr.shape[0]
        full[dims[key][0]], full[dims[key][1]] = arr.shape[1:] if grouped else arr.shape
    want = {"m": tm, "n": tn, "k": tk}
    per_step = min(groups, 4) if group == "k" else 1
    tiles = {d: full[d] if d == group else _tile(full[d], want[d]) for d in "mnk"}
    steps = {d: groups // per_step if d == group else full[d] // tiles[d] for d in "mnk"}

    def spec(key):
        d0, d1 = dims[key]

        def index(i, j, kk):
            at = {"m": i, "n": j, "k": kk}
            if group in (d0, d1):
                return (at[group], 0 if d0 == group else at[d0], 0 if d1 == group else at[d1])
            return (at[d0], at[d1])

        block = (tiles[d0], tiles[d1])
        if group in (d0, d1):
            block = ((per_step if group == "k" else None),) + block
        return pl.BlockSpec(block, index)

    ca, cb = {"nn": (1, 0), "nt": (1, 1), "tn": (0, 0)}[mode]
    nk = steps["k"]
    o_spec = spec("o")
    out_shape = (groups, full["m"], full["n"]) if group in ("m", "n") else (full["m"], full["n"])
    has_res = residual is not None

    def product(a_ref, b_ref):
        if group != "k":
            return _dg(a_ref[...], b_ref[...], ca, cb)
        total = _dg(a_ref[0], b_ref[0], ca, cb)
        for g in range(1, per_step):
            total = total + _dg(a_ref[g], b_ref[g], ca, cb)
        return total

    def body(a_ref, b_ref, *rest):
        r_ref = rest[0] if has_res else None
        o_ref = rest[1 if has_res else 0]

        def emit(acc):
            o_ref[...] = (acc + r_ref[...] if has_res else acc).astype(out_dtype)

        if nk == 1:
            emit(product(a_ref, b_ref))
            return
        acc_ref = rest[-1]
        kk = pl.program_id(2)

        @pl.when(kk == 0)
        def _():
            acc_ref[...] = jnp.zeros_like(acc_ref)

        acc_ref[...] += product(a_ref, b_ref)

        @pl.when(kk == nk - 1)
        def _():
            emit(acc_ref[...])

    res = host_call(
        side, body, name=name, grid=(steps["m"], steps["n"], nk),
        in_specs=[spec("a"), spec("b")] + ([o_spec] if has_res else []), out_specs=[o_spec],
        out_shape=[jax.ShapeDtypeStruct(out_shape, out_dtype)],
        scratch_shapes=[pltpu.VMEM((tiles["m"], tiles["n"]), f32)] if nk > 1 else [], aliases={},
        args=(a, b) + ((residual,) if has_res else ()))
    return res[0] if side is None else (res[0][0], res[1])


def rmsnorm_fwd(x, g_row, *, name):
    s, d = x.shape
    ts = _tile(s, 512, 16)

    def body(x_ref, g_ref, o_ref):
        o_ref[...] = _rms(x_ref[...], g_ref[...]).astype(bf16)

    return pl.pallas_call(
        body, name=name, grid=(s // ts,),
        in_specs=[pl.BlockSpec((ts, d), lambda i: (i, 0)), pl.BlockSpec((1, d), lambda i: (0, 0))],
        out_specs=pl.BlockSpec((ts, d), lambda i: (i, 0)), out_shape=jax.ShapeDtypeStruct((s, d), bf16),
        compiler_params=_params(1),
    )(x, g_row)


def rmsnorm_bwd(x, g_row, dh, dres, *, name):
    s, d = x.shape
    ts = _tile(s, 512, 16)

    def body(x_ref, g_ref, dh_ref, dres_ref, dx_ref, dg_ref):
        @pl.when(pl.program_id(0) == 0)
        def _():
            dg_ref[...] = jnp.zeros_like(dg_ref)

        _, vjp = jax.vjp(_rms, x_ref[...], g_ref[...])
        dx, dg = vjp(dh_ref[...])
        dx_ref[...] = dx + dres_ref[...]
        dg_ref[...] += dg

    row = pl.BlockSpec((ts, d), lambda i: (i, 0))
    vec = pl.BlockSpec((1, d), lambda i: (0, 0))
    return pl.pallas_call(
        body, name=name, grid=(s // ts,), in_specs=[row, vec, row, row], out_specs=[row, vec],
        out_shape=[jax.ShapeDtypeStruct((s, d), f32), jax.ShapeDtypeStruct((1, d), f32)],
        compiler_params=_params(1),
    )(x, g_row, dh, dres)


def loss_head(x, g_row, target, *, name):
    s, d = x.shape
    ts = _tile(s, 512, 16)

    def body(x_ref, g_ref, t_ref, dx_ref, dg_ref, loss_ref):
        @pl.when(pl.program_id(0) == 0)
        def _():
            dg_ref[...] = jnp.zeros_like(dg_ref)
            loss_ref[...] = jnp.zeros_like(loss_ref)

        y, vjp = jax.vjp(_rms, x_ref[...], g_ref[...])
        err = y - t_ref[...]
        dx, dg = vjp(err * (1.0 / d))
        dx_ref[...] = dx
        dg_ref[...] += dg
        loss_ref[...] += 0.5 * jnp.sum(jnp.sum(err * err, axis=1, keepdims=True) * (1.0 / d), axis=0, keepdims=True)

    row = pl.BlockSpec((ts, d), lambda i: (i, 0))
    vec = pl.BlockSpec((1, d), lambda i: (0, 0))
    one = pl.BlockSpec((1, LANES), lambda i: (0, 0))
    return pl.pallas_call(
        body, name=name, grid=(s // ts,), in_specs=[row, vec, row], out_specs=[row, vec, one],
        out_shape=[jax.ShapeDtypeStruct((s, d), f32), jax.ShapeDtypeStruct((1, d), f32),
                   jax.ShapeDtypeStruct((1, LANES), f32)],
        compiler_params=_params(1),
    )(x, g_row, target)


def _sgu_chunk(p_a, ln_g, ln_b, w, b_t):
    t = SGU_CHUNK
    u = jax.nn.gelu(p_a[:, :MIX])
    v = jax.nn.gelu(p_a[:, MIX:])
    vc = v - jnp.mean(v, axis=-1, keepdims=True)
    vn = vc * lax.rsqrt(jnp.mean(vc * vc, axis=-1, keepdims=True) + NORM_EPS) * ln_g + ln_b
    causal = lax.broadcasted_iota(jnp.int32, (t, t), 0) >= lax.broadcasted_iota(jnp.int32, (t, t), 1)
    outs = []
    for g in range(SGU_GROUPS):
        sl = slice(g * LANES, (g + 1) * LANES)
        mixed = bdot(jnp.where(causal, w[g], 0.0), vn[:, sl], 1, 0) + b_t[:, g:g + 1]
        outs.append(u[:, sl] * mixed)
    return jnp.concatenate(outs, axis=1)


def _sgu_specs(s, ts):
    return [pl.BlockSpec((ts, 2 * MIX), lambda i: (i, SEC_A // (2 * MIX))),
            pl.BlockSpec((1, MIX), lambda i: (0, 0)), pl.BlockSpec((1, MIX), lambda i: (0, 0)),
            pl.BlockSpec((SGU_GROUPS, SGU_CHUNK, SGU_CHUNK), lambda i: (0, 0, 0)),
            pl.BlockSpec((SGU_CHUNK, SGU_GROUPS), lambda i: (0, 0))]


def sgu_fwd(proj, ln_g, ln_b, w, b_t, *, name):
    s = proj.shape[0]
    ts = _tile(s, 512)
    n_chunk = ts // SGU_CHUNK

    def body(p_ref, g_ref, b_ref, w_ref, bt_ref, o_ref):
        def step(c, carry):
            rows = pl.ds(pl.multiple_of(c * SGU_CHUNK, SGU_CHUNK), SGU_CHUNK)
            o_ref[rows, :] = _sgu_chunk(p_ref[rows, :].astype(f32), g_ref[...], b_ref[...], w_ref[...], bt_ref[...]).astype(bf16)
            return carry
        lax.fori_loop(0, n_chunk, step, 0)

    return pl.pallas_call(
        body, name=name, grid=(s // ts,), in_specs=_sgu_specs(s, ts),
        out_specs=pl.BlockSpec((ts, MIX), lambda i: (i, 0)), out_shape=jax.ShapeDtypeStruct((s, MIX), bf16),
        compiler_params=_params(1),
    )(proj, ln_g, ln_b, w, b_t)


def sgu_bwd(proj, ln_g, ln_b, w, b_t, d_out, dproj, *, name):
    s = proj.shape[0]
    ts = _tile(s, 512)
    n_chunk = ts // SGU_CHUNK

    def body(p_ref, g_ref, b_ref, w_ref, bt_ref, do_ref, _, dp_ref, dg_ref, db_ref, dw_ref, dbt_ref):
        @pl.when(pl.program_id(0) == 0)
        def _():
            dg_ref[...] = jnp.zeros_like(dg_ref)
            db_ref[...] = jnp.zeros_like(db_ref)
            dw_ref[...] = jnp.zeros_like(dw_ref)
            dbt_ref[...] = jnp.zeros_like(dbt_ref)

        def step(c, carry):
            rows = pl.ds(pl.multiple_of(c * SGU_CHUNK, SGU_CHUNK), SGU_CHUNK)
            _, vjp = jax.vjp(_sgu_chunk, p_ref[rows, :].astype(f32), g_ref[...], b_ref[...], w_ref[...], bt_ref[...])
            dp, dg, db, dw, dbt = vjp(do_ref[rows, :])
            dp_ref[rows, :] = dp.astype(bf16)
            dg_ref[...] += dg
            db_ref[...] += db
            dw_ref[...] += dw
            dbt_ref[...] += dbt
            return carry
        lax.fori_loop(0, n_chunk, step, 0)

    specs = _sgu_specs(s, ts)
    return pl.pallas_call(
        body, name=name, grid=(s // ts,),
        in_specs=specs + [pl.BlockSpec((ts, MIX), lambda i: (i, 0)), pl.BlockSpec(memory_space=pl.ANY)],
        out_specs=[specs[0], specs[1], specs[2], specs[3], specs[4]],
        out_shape=[jax.ShapeDtypeStruct(dproj.shape, bf16), jax.ShapeDtypeStruct((1, MIX), f32),
                   jax.ShapeDtypeStruct((1, MIX), f32), jax.ShapeDtypeStruct(w.shape, f32),
                   jax.ShapeDtypeStruct(b_t.shape, f32)],
        input_output_aliases={6: 0}, compiler_params=_params(1),
    )(proj, ln_g, ln_b, w, b_t, d_out, dproj)


def rope_tables(posf, inv_freq, *, name):
    s = posf.shape[0]
    ts = _tile(s, 1024, 8)
    half = ROPE_DIM // 2

    def body(pos_ref, inv_ref, o_ref):
        d = lax.broadcasted_iota(jnp.int32, (1, LANES), 1) % SWA_HD
        ang = pos_ref[...] * inv_ref[...]
        sin = jnp.sin(ang)
        o_ref[0] = jnp.cos(ang)
        o_ref[1] = jnp.where(d < half, sin, 0.0)
        o_ref[2] = jnp.where((d >= half) & (d < ROPE_DIM), sin, 0.0)

    return pl.pallas_call(
        body, name=name, grid=(s // ts,),
        in_specs=[pl.BlockSpec((ts, 1), lambda i: (i, 0)), pl.BlockSpec((1, LANES), lambda i: (0, 0))],
        out_specs=pl.BlockSpec((3, ts, LANES), lambda i: (0, i, 0)), out_shape=jax.ShapeDtypeStruct((3, s, LANES), f32),
        compiler_params=_params(1),
    )(posf, inv_freq)


def _rope(x, table):
    w = x.shape[1]
    half = ROPE_DIM // 2
    c, lo, hi = (jnp.concatenate([table[i]] * (w // LANES), axis=1) for i in range(3))
    return x * c - lroll(x, w - half) * lo + lroll(x, half) * hi


def _swa_block(q, kp, kc, vp, vc, sink_row, table_q, table_p, prev_ok, wide):
    t = WINDOW
    q = _rope(q, table_q) * (SWA_HD ** -0.5)
    keys = jnp.concatenate([_rope(kp, table_p), _rope(kc, table_q)], axis=0)
    vals = jnp.concatenate([vp, vc], axis=0)
    own = lax.broadcasted_iota(jnp.int32, (t, t), 0) >= lax.broadcasted_iota(jnp.int32, (t, t), 1)
    lane_half = lax.broadcasted_iota(jnp.int32, (t, LANES), 1) // SWA_HD
    group = SWA_HEADS // SWA_KV
    slabs = []
    for pair in range(SWA_HEADS // 2):
        q_pair = q[:, pair * LANES:(pair + 1) * LANES]
        acc = jnp.zeros((t, LANES), f32)
        for half in range(2):
            h = 2 * pair + half
            kv = h // group
            qm = jnp.where(lane_half == half, q_pair, 0.0)
            if half != kv:
                qm = lroll(qm, SWA_HD)
            if wide:
                both = bdot(qm, keys, 1, 1)
                s_prev, s_own = both[:, :t], both[:, t:]
            else:
                s_prev, s_own = bdot(qm, keys[:t], 1, 1), bdot(qm, keys[t:], 1, 1)
            logits = jnp.where(own, s_own, jnp.where(prev_ok, s_prev, -1e30))
            sink = _lane_col(sink_row, h)
            m = lax.stop_gradient(jnp.maximum(jnp.max(logits, axis=1, keepdims=True), sink))
            p = jnp.exp(logits - m)
            probs = p * (1.0 / (jnp.sum(p, axis=1, keepdims=True) + jnp.exp(sink - m)))
            p_prev, p_own = jnp.where(own, 0.0, probs), jnp.where(own, probs, 0.0)
            if wide:
                o = bdot(jnp.concatenate([p_prev, p_own], axis=1), vals, 1, 0)
            else:
                o = bdot(p_prev, vals[:t], 1, 0) + bdot(p_own, vals[t:], 1, 0)
            o = jnp.where(lane_half == kv, o, 0.0)
            if half != kv:
                o = lroll(o, SWA_HD)
            acc = acc + o
        slabs.append(acc)
    return jnp.concatenate(slabs, axis=1)


def _swa_in_specs(nc, clamp):
    t = WINDOW
    qb, kb, vb = SEC_B // MIX, (SEC_B + MIX) // LANES, (SEC_B + MIX + LANES) // LANES

    def cur(i):
        return jnp.minimum(i, nc - 1) if clamp else i

    def prev(i):
        return jnp.maximum(cur(i) - 1, 0)

    return [pl.BlockSpec((t, MIX), lambda i: (cur(i), qb)),
            pl.BlockSpec((t, LANES), lambda i: (prev(i), kb)), pl.BlockSpec((t, LANES), lambda i: (cur(i), kb)),
            pl.BlockSpec((t, LANES), lambda i: (prev(i), vb)), pl.BlockSpec((t, LANES), lambda i: (cur(i), vb)),
            pl.BlockSpec((1, LANES), lambda i: (0, 0)),
            pl.BlockSpec((3, t, LANES), lambda i: (0, cur(i), 0)), pl.BlockSpec((3, t, LANES), lambda i: (0, prev(i), 0))]


def swa_fwd(proj, sink_row, tables, *, name):
    s = proj.shape[0]
    nc = s // WINDOW

    def body(q_ref, kp_ref, kc_ref, vp_ref, vc_ref, sink_ref, tq_ref, tp_ref, o_ref):
        prev_ok = pl.program_id(0) > 0
        blocks = [r[...].astype(f32) for r in (q_ref, kp_ref, kc_ref, vp_ref, vc_ref)]
        o_ref[...] = _swa_block(*blocks, sink_ref[...],
                                tq_ref[...], tp_ref[...], prev_ok, True).astype(bf16)

    return pl.pallas_call(
        body, name=name, grid=(nc,), in_specs=_swa_in_specs(nc, False),
        out_specs=pl.BlockSpec((WINDOW, MIX), lambda i: (i, 0)), out_shape=jax.ShapeDtypeStruct((s, MIX), bf16),
        compiler_params=_params(1),
    )(proj, proj, proj, proj, proj, sink_row, tables, tables)


def swa_bwd(proj, sink_row, tables, d_out, dproj, *, side=None, name):
    s = proj.shape[0]
    nc = s // WINDOW
    t = WINDOW

    def body(q_ref, kp_ref, kc_ref, vp_ref, vc_ref, sink_ref, tq_ref, tp_ref, do_ref, _,
             dp_ref, dsink_ref, cq_ref, ck_ref, cv_ref):
        i = pl.program_id(0)

        @pl.when(i == 0)
        def _():
            dsink_ref[...] = jnp.zeros_like(dsink_ref)

        def write(dk_prev, dv_prev):
            dp_ref[:, :MIX] = cq_ref[...].astype(bf16)
            dp_ref[:, MIX:MIX + LANES] = (ck_ref[...] + dk_prev).astype(bf16)
            dp_ref[:, MIX + LANES:] = (cv_ref[...] + dv_prev).astype(bf16)

        @pl.when(i < nc)
        def _():
            fn = functools.partial(_swa_block, table_q=tq_ref[...], table_p=tp_ref[...], prev_ok=i > 0, wide=False)
            blocks = [r[...].astype(f32) for r in (q_ref, kp_ref, kc_ref, vp_ref, vc_ref)]
            _, vjp = jax.vjp(fn, *blocks, sink_ref[...])
            dq, dkp, dkc, dvp, dvc, dsink = vjp(do_ref[...])
            dsink_ref[...] += dsink

            @pl.when(i > 0)
            def _():
                write(dkp, dvp)

            cq_ref[...] = dq
            ck_ref[...] = dkc
            cv_ref[...] = dvc

        @pl.when(i == nc)
        def _():
            write(0.0, 0.0)

    return host_call(
        side, body, name=name, grid=(nc + 1,),
        in_specs=_swa_in_specs(nc, True) + [pl.BlockSpec((t, MIX), lambda i: (jnp.minimum(i, nc - 1), 0)),
                                            pl.BlockSpec(memory_space=pl.ANY)],
        out_specs=[pl.BlockSpec((t, MIX + 2 * LANES), lambda i: (jnp.maximum(i - 1, 0), SEC_B // (MIX + 2 * LANES))),
                   pl.BlockSpec((1, LANES), lambda i: (0, 0))],
        out_shape=[jax.ShapeDtypeStruct(dproj.shape, bf16), jax.ShapeDtypeStruct((1, LANES), f32)],
        scratch_shapes=[pltpu.VMEM((t, MIX), f32), pltpu.VMEM((t, LANES), f32), pltpu.VMEM((t, LANES), f32)],
        aliases={9: 0}, args=(proj, proj, proj, proj, proj, sink_row, tables, tables, d_out, dproj))


CONV_PAD = 16


def _conv_taps(xp, rows):
    off = CONV_PAD - (DN_CONV - 1)
    return [xp[off + i:off + i + rows] for i in range(DN_CONV)]


def _conv_pre(taps, w):
    pre = taps[0] * w[0:1]
    for i in range(1, DN_CONV):
        pre = pre + taps[i] * w[i:i + 1]
    return pre


def conv_fwd(proj, conv_w8, *, name):
    s = proj.shape[0]
    wq = 3 * MIX
    ts = _tile(s, 512)
    nb = ts // CONV_PAD

    def body(x_ref, prev_ref, w_ref, o_ref):
        prev = jnp.where(pl.program_id(0) > 0, prev_ref[...].astype(f32), 0.0)
        pre = _conv_pre(_conv_taps(jnp.concatenate([prev, x_ref[...].astype(f32)], axis=0), ts), w_ref[...])
        o_ref[...] = pre * _sigmoid(pre)

    return pl.pallas_call(
        body, name=name, grid=(s // ts,),
        in_specs=[pl.BlockSpec((ts, wq), lambda i: (i, SEC_QKV // wq)),
                  pl.BlockSpec((CONV_PAD, wq), lambda i: (jnp.maximum(i * nb - 1, 0), SEC_QKV // wq)),
                  pl.BlockSpec((CONV_PAD, wq), lambda i: (0, 0))],
        out_specs=pl.BlockSpec((ts, wq), lambda i: (i, 0)), out_shape=jax.ShapeDtypeStruct((s, wq), f32),
        compiler_params=_params(1),
    )(proj, proj, conv_w8)


def conv_bwd(proj, conv_w8, dxc, dproj, *, side=None, name):
    s = proj.shape[0]
    wq = 3 * MIX
    ts = _tile(s, 512)
    nb = ts // CONV_PAD
    nt = s // ts
    last_blk = s // CONV_PAD - 1

    def body(x_ref, prev_ref, next_ref, w_ref, d_ref, dnext_ref, _, dp_ref, dw_ref):
        i = pl.program_id(0)

        @pl.when(i == 0)
        def _():
            dw_ref[...] = jnp.zeros_like(dw_ref)

        w = w_ref[...]
        prev = jnp.where(i > 0, prev_ref[...].astype(f32), 0.0)
        more = i < nt - 1
        xp = jnp.concatenate([prev, x_ref[...].astype(f32), jnp.where(more, next_ref[...].astype(f32), 0.0)], axis=0)
        taps = _conv_taps(xp, ts + CONV_PAD)
        pre = _conv_pre(taps, w)
        sig = _sigmoid(pre)
        dxc_ext = jnp.concatenate([d_ref[...], jnp.where(more, dnext_ref[...], 0.0)], axis=0)
        dpre = dxc_ext * sig * (1.0 + pre * (1.0 - sig))
        d_raw = jnp.zeros((ts, wq), f32)
        dws = []
        for k in range(DN_CONV):
            shift = DN_CONV - 1 - k
            d_raw = d_raw + dpre[shift:shift + ts] * w[k:k + 1]
            dws.append(jnp.sum(dpre[:ts] * taps[k][:ts], axis=0, keepdims=True))
        dp_ref[...] = d_raw.astype(bf16)
        dw_ref[...] += jnp.concatenate(dws + [jnp.zeros((CONV_PAD - DN_CONV, wq), f32)], axis=0)

    sec = SEC_QKV // wq
    return host_call(
        side, body, name=name, grid=(nt,),
        in_specs=[pl.BlockSpec((ts, wq), lambda i: (i, sec)),
                  pl.BlockSpec((CONV_PAD, wq), lambda i: (jnp.maximum(i * nb - 1, 0), sec)),
                  pl.BlockSpec((CONV_PAD, wq), lambda i: (jnp.minimum((i + 1) * nb, last_blk), sec)),
                  pl.BlockSpec((CONV_PAD, wq), lambda i: (0, 0)),
                  pl.BlockSpec((ts, wq), lambda i: (i, 0)),
                  pl.BlockSpec((CONV_PAD, wq), lambda i: (jnp.minimum((i + 1) * nb, last_blk), 0)),
                  pl.BlockSpec(memory_space=pl.ANY)],
        out_specs=[pl.BlockSpec((ts, wq), lambda i: (i, sec)), pl.BlockSpec((CONV_PAD, wq), lambda i: (0, 0))],
        out_shape=[jax.ShapeDtypeStruct(dproj.shape, bf16), jax.ShapeDtypeStruct((CONV_PAD, wq), f32)],
        scratch_shapes=[], aliases={6: 0}, args=(proj, proj, proj, conv_w8, dxc, dxc, dproj))


def _dn_chunk(state, xc, z, ba, alog_row, dtb_row, norm_row, t_known):
    c, nh = DN_CHUNK, DN_HEADS
    n = c * nh
    row = lax.broadcasted_iota(jnp.int32, (n, n), 0)
    col = lax.broadcasted_iota(jnp.int32, (n, n), 1)
    same_head = (row // c) == (col // c)
    tril, strict = same_head & (row >= col), same_head & (row > col)
    tril_c = lax.broadcasted_iota(jnp.int32, (c, c), 0) >= lax.broadcasted_iota(jnp.int32, (c, c), 1)
    beta_all = _sigmoid(ba)
    g_all = -jnp.exp(alog_row) * jax.nn.softplus(ba + dtb_row)
    gc_all = hdot(tril_c.astype(f32), g_all)
    gc_t = gc_all.T

    def stack(piece):
        return jnp.concatenate([piece(h) for h in range(nh)], axis=0)

    q = stack(lambda h: xc[:, h * DN_HD:(h + 1) * DN_HD])
    k = stack(lambda h: xc[:, MIX + h * DN_HD:MIX + (h + 1) * DN_HD])
    v = stack(lambda h: xc[:, 2 * MIX + h * DN_HD:2 * MIX + (h + 1) * DN_HD])
    zs = stack(lambda h: z[:, h * DN_HD:(h + 1) * DN_HD])
    q = q * lax.rsqrt(jnp.sum(q * q, axis=-1, keepdims=True) + NORM_EPS) * (DN_HD ** -0.5)
    k = k * lax.rsqrt(jnp.sum(k * k, axis=-1, keepdims=True) + NORM_EPS)
    beta = stack(lambda h: _lane_col(beta_all, h))
    g_cols = [_lane_col(gc_all, nh + h) for h in range(nh)]
    g_col = jnp.concatenate(g_cols, axis=0)
    g_row = jnp.concatenate([gc_t[nh + h:nh + h + 1, :] for h in range(nh)], axis=1)
    g_last = stack(lambda h: jnp.broadcast_to(g_cols[h][c - 1:c, :], (c, 1)))
    decay = jnp.where(tril, jnp.exp(jnp.where(tril, g_col - g_row, 0.0)), 0.0)
    kb = k * beta
    low = jnp.where(strict, bdot(kb, k, 1, 1) * decay, 0.0)
    t_inv = tri_inv(low, c) if t_known is None else tri_inv_known(low, t_known)
    e_gc = jnp.exp(g_col)
    uw = bdot(t_inv, jnp.concatenate([v * beta, kb * e_gc], axis=1), 1, 0)
    u, w = uw[:, :DN_HD], uw[:, DN_HD:]
    attn = bdot(q, k, 1, 1) * decay
    own = (lax.broadcasted_iota(jnp.int32, (n, nh * DN_HD), 1) // DN_HD
           == lax.broadcasted_iota(jnp.int32, (n, nh * DN_HD), 0) // c)

    def spread(a):
        return jnp.where(own, jnp.concatenate([a] * nh, axis=1), 0.0)

    v_new = u - bdot(spread(w), state, 1, 0)
    o = bdot(spread(q * e_gc), state, 1, 0) + bdot(attn, v_new, 1, 0)
    keep = stack(lambda h: jnp.broadcast_to(jnp.exp(g_cols[h][c - 1:c, :]), (DN_HD, 1)))
    new_state = state * keep + bdot(spread(k * jnp.exp(g_last - g_col)), v_new, 0, 0)
    out = _rms(o, norm_row) * (zs * _sigmoid(zs))
    return new_state, jnp.concatenate([out[h * c:(h + 1) * c] for h in range(nh)], axis=1), t_inv


DN_STEP = 4 * DN_CHUNK


def _dn_step(state, xc, z, ba, alog_row, dtb_row, norm_row, t_known=None):
    outs, t_invs = [], []
    for c in range(DN_STEP // DN_CHUNK):
        rows = slice(c * DN_CHUNK, (c + 1) * DN_CHUNK)
        state, out, t_inv = _dn_chunk(state, xc[rows], z[rows], ba[rows], alog_row, dtb_row, norm_row,
                                      None if t_known is None else t_known[c])
        outs.append(out)
        t_invs.append(t_inv)
    return state, jnp.concatenate(outs, axis=0), jnp.stack(t_invs)


def _dn_specs(ts, order):
    zb = SEC_Z // MIX
    return [pl.BlockSpec((ts, 3 * MIX), lambda i: (order(i), 0)),
            pl.BlockSpec((ts, MIX), lambda i: (order(i), zb)),
            pl.BlockSpec((ts, LANES), lambda i: (order(i), 0)),
            pl.BlockSpec((1, LANES), lambda i: (0, 0)), pl.BlockSpec((1, LANES), lambda i: (0, 0)),
            pl.BlockSpec((1, LANES), lambda i: (0, 0))]


def dn_fwd(xc, proj, p_ba, alog_row, dtb_row, norm_row, *, side=None, name):
    s = xc.shape[0]
    ts = _tile(s, 512)
    n_step = ts // DN_STEP

    per_step = DN_STEP // DN_CHUNK
    n_tri = DN_HEADS * DN_CHUNK

    def body(xc_ref, z_ref, ba_ref, al_ref, dt_ref, nr_ref, o_ref, st_ref, tri_ref, state_ref):
        @pl.when(pl.program_id(0) == 0)
        def _():
            state_ref[...] = jnp.zeros_like(state_ref)

        def step(c, carry):
            rows = pl.ds(pl.multiple_of(c * DN_STEP, DN_STEP), DN_STEP)
            st_ref[c] = state_ref[...]
            new_state, out, t_invs = _dn_step(state_ref[...], xc_ref[rows, :], z_ref[rows, :].astype(f32), ba_ref[rows, :],
                                              al_ref[...], dt_ref[...], nr_ref[...])
            state_ref[...] = new_state
            o_ref[rows, :] = out.astype(bf16)
            tri_ref[pl.ds(c * per_step, per_step)] = t_invs
            return carry
        lax.fori_loop(0, n_step, step, 0)

    return host_call(
        side, body, name=name, grid=(s // ts,), in_specs=_dn_specs(ts, lambda i: i),
        out_specs=[pl.BlockSpec((ts, MIX), lambda i: (i, 0)),
                   pl.BlockSpec((n_step, DN_HEADS * DN_HD, DN_HD), lambda i: (i, 0, 0)),
                   pl.BlockSpec((n_step * per_step, n_tri, n_tri), lambda i: (i, 0, 0))],
        out_shape=[jax.ShapeDtypeStruct((s, MIX), bf16),
                   jax.ShapeDtypeStruct((s // DN_STEP, DN_HEADS * DN_HD, DN_HD), f32),
                   jax.ShapeDtypeStruct((s // DN_CHUNK, n_tri, n_tri), f32)],
        scratch_shapes=[pltpu.VMEM((DN_HEADS * DN_HD, DN_HD), f32)], aliases={},
        args=(xc, proj, p_ba, alog_row, dtb_row, norm_row))


def dn_bwd(xc, proj, p_ba, alog_row, dtb_row, norm_row, saved, tri, d_out, dproj, *, side=None, name):
    s = xc.shape[0]
    ts = _tile(s, 512)
    n_step = ts // DN_STEP
    nt = s // ts

    per_step = DN_STEP // DN_CHUNK
    n_tri = DN_HEADS * DN_CHUNK

    def body(xc_ref, z_ref, ba_ref, al_ref, dt_ref, nr_ref, st_ref, tri_ref, do_ref, _,
             dz_ref, dxc_ref, dba_ref, dal_ref, ddt_ref, dnr_ref, dstate_ref):
        @pl.when(pl.program_id(0) == 0)
        def _():
            dstate_ref[...] = jnp.zeros_like(dstate_ref)
            dal_ref[...] = jnp.zeros_like(dal_ref)
            ddt_ref[...] = jnp.zeros_like(ddt_ref)
            dnr_ref[...] = jnp.zeros_like(dnr_ref)

        def step(it, carry):
            c = n_step - 1 - it
            rows = pl.ds(pl.multiple_of(c * DN_STEP, DN_STEP), DN_STEP)
            t_known = tri_ref[pl.ds(c * per_step, per_step)]
            _, vjp = jax.vjp(lambda *a: _dn_step(*a, t_known=t_known)[:2], st_ref[c], xc_ref[rows, :],
                             z_ref[rows, :].astype(f32), ba_ref[rows, :], al_ref[...], dt_ref[...], nr_ref[...])
            d_in, dxc, dz, dba, dal, ddt, dnr = vjp((dstate_ref[...], do_ref[rows, :]))
            dstate_ref[...] = d_in
            dxc_ref[rows, :] = dxc
            dz_ref[rows, :] = dz.astype(bf16)
            dba_ref[rows, :] = dba.astype(bf16)
            dal_ref[...] += dal
            ddt_ref[...] += ddt
            dnr_ref[...] += dnr
            return carry
        lax.fori_loop(0, n_step, step, 0)

    def rev(i):
        return nt - 1 - i

    specs = _dn_specs(ts, rev)
    vec = pl.BlockSpec((1, LANES), lambda i: (0, 0))
    return host_call(
        side, body, name=name, grid=(nt,),
        in_specs=specs + [pl.BlockSpec((n_step, DN_HEADS * DN_HD, DN_HD), lambda i: (rev(i), 0, 0)),
                          pl.BlockSpec((n_step * per_step, n_tri, n_tri), lambda i: (rev(i), 0, 0)),
                          pl.BlockSpec((ts, MIX), lambda i: (rev(i), 0)), pl.BlockSpec(memory_space=pl.ANY)],
        out_specs=[specs[1], specs[0], specs[2], vec, vec, vec],
        out_shape=[jax.ShapeDtypeStruct(dproj.shape, bf16), jax.ShapeDtypeStruct((s, 3 * MIX), f32),
                   jax.ShapeDtypeStruct((s, LANES), bf16)] + [jax.ShapeDtypeStruct((1, LANES), f32)] * 3,
        scratch_shapes=[pltpu.VMEM((DN_HEADS * DN_HD, DN_HD), f32)], aliases={9: 0},
        args=(xc, proj, p_ba, alog_row, dtb_row, norm_row, saved, tri, d_out, dproj))


def _merge_in_specs(ts, d):
    row = pl.BlockSpec((ts, MIX), lambda i: (i, 0))
    return [row, row, row, pl.BlockSpec((ts, 3 * d), lambda i: (i, SEC_G // (3 * d))),
            pl.BlockSpec((3, MIX, d), lambda i: (0, 0, 0))]


def merge_fwd(out_a, out_b, out_c, proj, w_branch, *, name):
    s, d = out_a.shape[0], w_branch.shape[2]
    ts = _tile(s, 512, 16)

    def body(a_ref, b_ref, c_ref, g_ref, w_ref, o_ref):
        acc = jnp.zeros((ts, d), f32)
        for n, r in enumerate((a_ref, b_ref, c_ref)):
            acc = acc + _sigmoid(g_ref[:, n * d:(n + 1) * d].astype(f32)) * _dg(r[...], w_ref[n], 1, 0)
        o_ref[...] = acc.astype(bf16)

    return pl.pallas_call(
        body, name=name, grid=(s // ts,), in_specs=_merge_in_specs(ts, d),
        out_specs=pl.BlockSpec((ts, d), lambda i: (i, 0)), out_shape=jax.ShapeDtypeStruct((s, d), bf16),
        compiler_params=_params(1),
    )(out_a, out_b, out_c, proj, w_branch)


def merge_bwd(out_a, out_b, out_c, proj, w_branch, d_merged, dproj, *, name):
    s, d = out_a.shape[0], w_branch.shape[2]
    ts = _tile(s, 512, 16)

    def body(a_ref, b_ref, c_ref, g_ref, w_ref, dm_ref, _, dg_ref, da_ref, db_ref, dc_ref, dw_ref):
        @pl.when(pl.program_id(0) == 0)
        def _():
            dw_ref[...] = jnp.zeros_like(dw_ref)

        dm = dm_ref[...]
        for n, (r, dr) in enumerate(((a_ref, da_ref), (b_ref, db_ref), (c_ref, dc_ref))):
            gate = _sigmoid(g_ref[:, n * d:(n + 1) * d].astype(f32))
            branch = _dg(r[...], w_ref[n], 1, 0)
            dg_ref[:, n * d:(n + 1) * d] = (dm * branch * gate * (1.0 - gate)).astype(bf16)
            d_branch = dm * gate
            dr[...] = _dg(d_branch, w_ref[n], 1, 1)
            dw_ref[n] += _dg(r[...], d_branch, 0, 0)

    specs = _merge_in_specs(ts, d)
    row_f = pl.BlockSpec((ts, MIX), lambda i: (i, 0))
    return pl.pallas_call(
        body, name=name, grid=(s // ts,),
        in_specs=specs + [pl.BlockSpec((ts, d), lambda i: (i, 0)), pl.BlockSpec(memory_space=pl.ANY)],
        out_specs=[specs[3], row_f, row_f, row_f, specs[4]],
        out_shape=[jax.ShapeDtypeStruct(dproj.shape, bf16)] + [jax.ShapeDtypeStruct((s, MIX), f32)] * 3
        + [jax.ShapeDtypeStruct(w_branch.shape, f32)],
        input_output_aliases={6: 0}, compiler_params=_params(1),
    )(out_a, out_b, out_c, proj, w_branch, d_merged, dproj)


def swiglu_fwd(gu, *, name):
    g2, s, w = gu.shape
    ng = g2 // 2
    ts = _tile(s, 1024, 16)

    def body(g_ref, u_ref, o_ref):
        g = g_ref[...].astype(f32)
        o_ref[...] = (g * _sigmoid(g) * u_ref[...].astype(f32)).astype(bf16)

    return pl.pallas_call(
        body, name=name, grid=(s // ts, ng),
        in_specs=[pl.BlockSpec((None, ts, w), lambda i, j: (j, i, 0)),
                  pl.BlockSpec((None, ts, w), lambda i, j: (ng + j, i, 0))],
        out_specs=pl.BlockSpec((None, ts, w), lambda i, j: (j, i, 0)), out_shape=jax.ShapeDtypeStruct((ng, s, w), bf16),
        compiler_params=_params(2),
    )(gu, gu)


def swiglu_bwd(gu, d_act, *, name):
    g2, s, w = gu.shape
    ng = g2 // 2
    ts = _tile(s, 1024, 16)

    def body(g_ref, u_ref, d_ref, dg_ref, du_ref):
        g, d = g_ref[...].astype(f32), d_ref[...].astype(f32)
        sig = _sigmoid(g)
        dg_ref[...] = (d * u_ref[...].astype(f32) * sig * (1.0 + g * (1.0 - sig))).astype(bf16)
        du_ref[...] = (d * g * sig).astype(bf16)

    lo = pl.BlockSpec((None, ts, w), lambda i, j: (j, i, 0))
    return pl.pallas_call(
        body, name=name, grid=(s // ts, ng),
        in_specs=[lo, pl.BlockSpec((None, ts, w), lambda i, j: (ng + j, i, 0)), lo], out_specs=[lo, lo],
        out_shape=[jax.ShapeDtypeStruct((ng, s, w), bf16)] * 2, compiler_params=_params(2),
    )(gu, gu, d_act)


def adamw(w, m, v, g_parts, *, name):
    n_layers = len(g_parts)
    n_parts, r, cols = g_parts[0].shape
    lanes = -(-cols // LANES) * LANES
    tr = _tile(r, max(16, (128 * 1024) // lanes), 16)
    nr = r // tr

    def body(w_ref, m_ref, v_ref, *rest):
        gp_refs, (g_ref, d_ref, nm_ref, nv_ref) = rest[:n_layers], rest[n_layers:]
        layer = pl.program_id(0)
        g = jnp.zeros((tr, cols), f32)
        for l, gp_ref in enumerate(gp_refs):
            g_l = gp_ref[0].astype(f32)
            for k in range(1, n_parts):
                g_l = g_l + gp_ref[k].astype(f32)
            g = jnp.where(layer == l, g_l, g)
        nm = ADAM_B1 * m_ref[...] + (1.0 - ADAM_B1) * g
        nv = ADAM_B2 * v_ref[...] + (1.0 - ADAM_B2) * jnp.square(g)
        m_hat = nm / (1.0 - ADAM_B1 ** ADAM_STEP)
        v_hat = nv / (1.0 - ADAM_B2 ** ADAM_STEP)
        g_ref[...] = g
        d_ref[...] = -ADAM_LR * (m_hat / (jnp.sqrt(v_hat) + ADAM_EPS) + ADAM_WD * w_ref[...])
        nm_ref[...] = nm
        nv_ref[...] = nv

    row = pl.BlockSpec((tr, cols), lambda l, i: (l * nr + i, 0))

    def parts_spec(own):
        return pl.BlockSpec((n_parts, tr, cols),
                            lambda l, i: (0, jnp.where(l == own, i, jnp.where(l < own, 0, nr - 1)), 0))

    return pl.pallas_call(
        body, name=name, grid=(n_layers, nr), in_specs=[row, row, row] + [parts_spec(l) for l in range(n_layers)],
        out_specs=[row] * 4, out_shape=[jax.ShapeDtypeStruct(w.shape, f32)] * 4, compiler_params=_params(2),
    )(w, m, v, *g_parts)


def _mesh_pos():
    return lax.axis_index("x"), lax.axis_index("y"), lax.axis_index("c")


def _dev_index(p):
    return 4 * p[0] + 2 * p[1] + p[2]


class Exchange:
    def __init__(self, kind, arrays):
        self.kind, self.arrays, self.n = kind, list(arrays), len(arrays)
        self.specs = [pl.BlockSpec(memory_space=pl.ANY)] * self.n
        self.out_shapes = [jax.ShapeDtypeStruct(((N_DEV,) if kind == "gather" else ()) + a.shape, a.dtype)
                           for a in self.arrays]
        self.scratch = [pltpu.SemaphoreType.DMA((self.n, N_DEV - 1)), pltpu.SemaphoreType.DMA((self.n, N_DEV - 1)),
                        pltpu.SemaphoreType.DMA((self.n,))]

    def _copies(self, in_refs, out_refs, sems, with_arrivals):
        send_sems, recv_sems, local_sems = sems
        x, y, c = _mesh_pos()
        mine = _dev_index((x, y, c))

        def src(a, slab):
            return in_refs[a] if self.kind == "gather" else in_refs[a].at[slab]

        local = [pltpu.make_async_copy(src(a, mine), out_refs[a].at[mine], local_sems.at[a]) for a in range(self.n)]
        sends, arrivals = [], []
        for k in range(1, N_DEV):
            peer = (1 - x if k & 4 else x, 1 - y if k & 2 else y, 1 - c if k & 1 else c)
            theirs = _dev_index(peer)
            for a in range(self.n):
                to = dict(send_sem=send_sems.at[a, k - 1], recv_sem=recv_sems.at[a, k - 1], device_id=peer,
                          device_id_type=pl.DeviceIdType.MESH)
                sends.append(pltpu.make_async_remote_copy(src_ref=src(a, theirs), dst_ref=out_refs[a].at[mine], **to))
                if with_arrivals:
                    arrivals.append(pltpu.make_async_remote_copy(src_ref=src(a, theirs),
                                                                 dst_ref=out_refs[a].at[theirs], **to))
        return local, sends, arrivals

    def start(self, in_refs, out_refs, sems):
        local, sends, _ = self._copies(in_refs, out_refs, sems, False)
        for cp in local + sends:
            cp.start()

    def wait(self, in_refs, out_refs, sems):
        local, sends, arrivals = self._copies(in_refs, out_refs, sems, True)
        for cp in arrivals:
            cp.wait_recv()
        for cp in sends:
            cp.wait_send()
        for cp in local:
            cp.wait()

    def run_around(self, grid, in_refs, out_refs, sems, *, before):
        at = None
        for axis, size in enumerate(grid):
            hit = pl.program_id(axis) == (0 if before else size - 1)
            at = hit if at is None else at & hit

        @pl.when(at)
        def _():
            (self.start if before else self.wait)(in_refs, out_refs, sems)


def host_call(side, body, *, name, grid, in_specs, out_specs, out_shape, scratch_shapes, args, aliases):
    n_in, n_out = len(in_specs), len(out_specs)
    if side is None:
        kernel_body = body
    else:
        n = side.n
        in_specs, args = in_specs + side.specs, tuple(args) + tuple(side.arrays)
        out_specs, out_shape = out_specs + side.specs, out_shape + side.out_shapes
        scratch_shapes = scratch_shapes + side.scratch

        def kernel_body(*refs):
            ins, side_in = refs[:n_in], refs[n_in:n_in + n]
            outs, side_out = refs[n_in + n:n_in + n + n_out], refs[n_in + n + n_out:n_in + 2 * n + n_out]
            scratch, sems = refs[n_in + 2 * n + n_out:-3], refs[-3:]
            side.run_around(grid, side_in, side_out, sems, before=True)
            body(*ins, *outs, *scratch)
            side.run_around(grid, side_in, side_out, sems, before=False)

    outs = pl.pallas_call(
        kernel_body, name=name, grid=grid, in_specs=in_specs, out_specs=out_specs, out_shape=out_shape,
        scratch_shapes=scratch_shapes, input_output_aliases=aliases, compiler_params=_params(len(grid)),
    )(*args)
    return outs if side is None else (outs[:n_out], outs[n_out:])


def all_gather(blocks, *, name):
    n = len(blocks)
    any_spec = pl.BlockSpec(memory_space=pl.ANY)

    def body(*refs):
        ins, outs = refs[:n], refs[n:2 * n]
        send_sems, recv_sems, local_sems = refs[2 * n:]
        x, y, c = _mesh_pos()
        me, sibling = (x, y, c), (x, y, 1 - c)
        chips = [(1 - x, y), (x, 1 - y), (1 - x, 1 - y)]

        def copy(a, k, block, to, src=None):
            dst = outs[a].at[_dev_index(block)]
            return pltpu.make_async_remote_copy(
                src_ref=dst if src is None else src, dst_ref=dst, send_sem=send_sems.at[a, k],
                recv_sem=recv_sems.at[a, k], device_id=to, device_id_type=pl.DeviceIdType.MESH)

        mine = [pltpu.make_async_copy(ins[a], outs[a].at[_dev_index(me)], local_sems.at[a]) for a in range(n)]
        for cp in mine:
            cp.start()
        first = []
        for a in range(n):
            first.append(copy(a, 0, me, sibling, src=ins[a]))
            first += [copy(a, 1 + j, me, (*chip, c), src=ins[a]) for j, chip in enumerate(chips)]
        for cp in first:
            cp.start()
        passed = []
        for j, chip in enumerate(chips):
            for a in range(n):
                copy(a, 1 + j, (*chip, c), me).wait_recv()
                fwd = copy(a, 4 + j, (*chip, c), sibling)
                fwd.start()
                passed.append(fwd)
        for a in range(n):
            copy(a, 0, sibling, me).wait_recv()
            for j, chip in enumerate(chips):
                copy(a, 4 + j, (*chip, 1 - c), me).wait_recv()
        for cp in first + passed:
            cp.wait_send()
        for cp in mine:
            cp.wait()

    return pl.pallas_call(
        body, name=name, in_specs=[any_spec] * n, out_specs=[any_spec] * n,
        out_shape=[jax.ShapeDtypeStruct((N_DEV,) + b.shape, b.dtype) for b in blocks],
        scratch_shapes=[pltpu.SemaphoreType.DMA((n, 7)), pltpu.SemaphoreType.DMA((n, 7)),
                        pltpu.SemaphoreType.DMA((n,))],
    )(*blocks)


def exchange(jobs, *, name):
    total = sum(j.n for j in jobs)

    def body(*refs):
        ins, outs, sems = refs[:total], refs[total:2 * total], refs[2 * total:]
        pieces, off = [], 0
        for i, j in enumerate(jobs):
            pieces.append((ins[off:off + j.n], outs[off:off + j.n], sems[3 * i:3 * i + 3]))
            off += j.n
        for j, piece in zip(jobs, pieces):
            j.start(*piece)
        for j, piece in zip(jobs, pieces):
            j.wait(*piece)

    outs = pl.pallas_call(
        body, name=name, in_specs=[s for j in jobs for s in j.specs], out_specs=[s for j in jobs for s in j.specs],
        out_shape=[s for j in jobs for s in j.out_shapes], scratch_shapes=[s for j in jobs for s in j.scratch],
    )(*[a for j in jobs for a in j.arrays])
    split, off = [], 0
    for j in jobs:
        split.append(outs[off:off + j.n])
        off += j.n
    return split


def _rows128(arr):
    flat = arr.reshape(-1)
    rows = -(-flat.shape[0] // (8 * LANES)) * 8
    return jnp.pad(flat, (0, rows * LANES - flat.shape[0])).reshape(rows, LANES)


def _pad_lanes(row, width=LANES, at=0):
    return jnp.pad(row, (at, width - at - row.shape[0])).reshape(1, width)


def _w_in_sections(got):
    d = got.shape[1]
    wi = jnp.transpose(got, (1, 0, 2)).reshape(d, -1)
    w_main = jnp.concatenate([wi[:, :C_B], wi[:, C_Z:C_BA], wi[:, C_QKV:C_Z], wi[:, C_G:], wi[:, C_B:C_QKV]], axis=1)
    return w_main, jnp.pad(wi[:, C_BA:C_G], ((0, 0), (0, LANES - (C_G - C_BA))))


def _w_in_parts(gw_main, gw_ba):
    d = gw_main.shape[0]
    full = jnp.concatenate([gw_main[:, SEC_A:SEC_Z], gw_main[:, SEC_B:], gw_main[:, SEC_QKV:SEC_G],
                            gw_main[:, SEC_Z:SEC_QKV], gw_ba[:, :C_G - C_BA], gw_main[:, SEC_G:SEC_B]], axis=1)
    return jnp.transpose(full.reshape(d, N_DEV, -1), (1, 0, 2))


def kernel(x, positions, attn_norm, w_in, sgu_ln_g, sgu_ln_b, sgu_w, sgu_b, attn_sinks, dn_conv_w, dn_a_log, dn_dt_bias, dn_norm, w_branch, w_out, ffn_norm, w_gate_up, w_down, final_norm, loss_target, m_attn_norm, m_w_in, m_sgu_ln_g, m_sgu_ln_b, m_sgu_w, m_sgu_b, m_attn_sinks, m_dn_conv_w, m_dn_a_log, m_dn_dt_bias, m_dn_norm, m_w_branch, m_w_out, m_ffn_norm, m_w_gate_up, m_w_down, m_final_norm, v_attn_norm, v_w_in, v_sgu_ln_g, v_sgu_ln_b, v_sgu_w, v_sgu_b, v_attn_sinks, v_dn_conv_w, v_dn_a_log, v_dn_dt_bias, v_dn_norm, v_w_branch, v_w_out, v_ffn_norm, v_w_gate_up, v_w_down, v_final_norm):
    given = dict(locals())
    depth, d_model = attn_norm.shape
    s = x.shape[1]
    x2 = x.reshape(s, d_model)
    target = loss_target.reshape(s, d_model)
    posf = positions.reshape(s, 1).astype(f32)
    inv_freq = ROPE_THETA ** (-jnp.arange(0, ROPE_DIM, 2, dtype=f32) / ROPE_DIM)
    inv_head = jnp.concatenate([inv_freq, inv_freq, jnp.zeros((SWA_HD - ROPE_DIM,), f32)])
    tables = rope_tables(posf, jnp.tile(inv_head, LANES // SWA_HD).reshape(1, LANES), name="rope_tables")

    assert depth == 2, depth
    gathered = dict(zip([("w_in", 0), ("dn_conv_w", 0), ("dn_conv_w", 1)], all_gather(
        [w_in[0].astype(bf16), dn_conv_w[0], dn_conv_w[1]], name="gather_first")))
    riders = {"l0_in_proj": [("w_branch", 0), ("w_out", 0), ("w_down", 0)],
              "l0_deltanet": [("w_gate_up", 0), ("w_in", 1)],
              "l1_deltanet": [("w_branch", 1), ("w_out", 1), ("w_gate_up", 1), ("w_down", 1)]}

    def gathering(host, call, *args, **kw):
        if host not in riders:
            return call(*args, name=host, **kw)
        side = Exchange("gather", [given[n][l].astype(bf16) for n, l in riders[host]])
        out, got = call(*args, side=side, name=host, **kw)
        gathered.update(zip(riders[host], got))
        return out

    layers, saved = [], []
    h_in = x2
    for l in range(depth):
        t = f"l{l}_"
        w_main, w_ba = _w_in_sections(gathered["w_in", l])
        conv_full = jnp.transpose(gathered["dn_conv_w", l], (1, 0, 2)).reshape(DN_CONV, -1)
        p = dict(
            w_main=w_main, w_ba=w_ba, conv_w8=jnp.pad(conv_full, ((0, CONV_PAD - DN_CONV), (0, 0))),
            attn_norm=attn_norm[l].reshape(1, -1), ffn_norm=ffn_norm[l].reshape(1, -1),
            ln_g=sgu_ln_g[l].reshape(1, -1), ln_b=sgu_ln_b[l].reshape(1, -1), sgu_w=sgu_w[l], sgu_bt=sgu_b[l].T,
            sink_row=_pad_lanes(attn_sinks[l]), alog_row=_pad_lanes(dn_a_log[l], at=DN_HEADS),
            dtb_row=_pad_lanes(dn_dt_bias[l], at=DN_HEADS), norm_row=dn_norm[l].reshape(1, -1))
        layers.append(p)
        h = rmsnorm_fwd(h_in, p["attn_norm"], name=t + "attn_norm")
        proj = gathering(t + "in_proj", matmul, h, p["w_main"], "nn", bf16)
        p_ba = matmul(h, p["w_ba"], "nn", f32, name=t + "in_proj_ba")
        out_a = sgu_fwd(proj, p["ln_g"], p["ln_b"], p["sgu_w"], p["sgu_bt"], name=t + "sgu")
        out_b = swa_fwd(proj, p["sink_row"], tables, name=t + "swa")
        xc = conv_fwd(proj, p["conv_w8"], name=t + "dn_conv")
        out_c, states, tri = gathering(t + "deltanet", dn_fwd, xc, proj, p_ba, p["alog_row"], p["dtb_row"], p["norm_row"])
        p.update(w_branch=jnp.transpose(gathered["w_branch", l], (1, 2, 0, 3)).reshape(3, MIX, d_model),
                 w_out=gathered["w_out", l].reshape(d_model, d_model),
                 w_gu=gathered["w_gate_up", l],
                 w_down=gathered["w_down", l].reshape(N_DEV // 2, -1, d_model))
        merged = merge_fwd(out_a, out_b, out_c, proj, p["w_branch"], name=t + "merge")
        x_mid = matmul(merged, p["w_out"], "nn", f32, residual=h_in, tn=1024, name=t + "out_proj")
        h2 = rmsnorm_fwd(x_mid, p["ffn_norm"], name=t + "ffn_norm")
        gu = matmul(h2, p["w_gu"], "nn", bf16, group="n", name=t + "gate_up")
        act = swiglu_fwd(gu, name=t + "swiglu")
        x_out = matmul(act, p["w_down"], "nn", f32, residual=x_mid, group="k", tn=1024, name=t + "down")
        saved.append(dict(x_in=h_in, h=h, proj=proj, p_ba=p_ba, out_a=out_a, out_b=out_b, out_c=out_c, xc=xc,
                          states=states, tri=tri, merged=merged, x_mid=x_mid, h2=h2, gu=gu, act=act))
        h_in = x_out

    dx, d_final_norm, loss_row = loss_head(h_in, final_norm.reshape(1, -1), target, name="loss_head")
    loss = lax.psum(loss_row[0, 0], MESH_AXES)

    shard_names = ["w_in", "dn_conv_w", "w_branch", "w_out", "w_gate_up", "w_down"]
    rep_names = ["attn_norm", "sgu_ln_g", "sgu_ln_b", "sgu_w", "sgu_b", "attn_sinks", "dn_a_log", "dn_dt_bias",
                 "dn_norm", "ffn_norm"]
    parts, received, per_layer = {}, {}, []
    senders = {"l0_b_swa": [("w_gate_up", 1), ("w_down", 1), ("w_out", 1), ("w_branch", 1)],
               "l0_b_deltanet": [("w_in", 1), ("w_down", 0)],
               "l0_b_in_proj_dw": [("w_gate_up", 0)],
               "l0_b_dn_conv": [("w_out", 0), ("w_branch", 0)],
               "l0_b_in_proj_dx": [("w_in", 0)]}

    def scattering(host, call, *args, **kw):
        if host not in senders:
            return call(*args, name=host, **kw)
        out, got = call(*args, side=Exchange("scatter", [parts[key] for key in senders[host]]), name=host, **kw)
        received.update(zip(senders[host], got))
        return out

    for l in reversed(range(depth)):
        p, sv, t = layers[l], saved[l], f"l{l}_b_"
        d_act = matmul(dx, p["w_down"], "nt", bf16, group="n", name=t + "down_dx")
        gw_down = matmul(sv["act"], dx, "tn", bf16, group="m", tk=2048, tn=512, name=t + "down_dw")
        d_gate, d_up = swiglu_bwd(sv["gu"], d_act, name=t + "swiglu")
        gw_gu = jnp.concatenate([matmul(sv["h2"], d_half, "tn", bf16, group="n", tk=2048, name=t + "gate_up_dw" + tag)
                                 for d_half, tag in ((d_gate, "_gate"), (d_up, "_up"))], axis=0)
        half = N_DEV // 2
        d_h2 = matmul(d_gate, p["w_gu"][:half], "nt", f32, group="k", tn=1024, name=t + "gate_up_dx_gate")
        d_h2 = matmul(d_up, p["w_gu"][half:], "nt", f32, group="k", tn=1024, residual=d_h2, name=t + "gate_up_dx_up")
        dx_mid, g_ffn = rmsnorm_bwd(sv["x_mid"], p["ffn_norm"], d_h2, dx, name=t + "ffn_norm")
        d_merged = matmul(dx_mid, p["w_out"], "nt", f32, tn=1024, name=t + "out_proj_dx")
        gw_out = matmul(sv["merged"], dx_mid, "tn", bf16, tk=2048, tn=1024, name=t + "out_proj_dw")
        dproj = lax.empty((s, W_MAIN), bf16)
        dproj, d_a, d_b, d_c, gw_branch = merge_bwd(sv["out_a"], sv["out_b"], sv["out_c"], sv["proj"], p["w_branch"],
                                                   d_merged, dproj, name=t + "merge")
        parts.update({("w_gate_up", l): gw_gu, ("w_down", l): gw_down.reshape(N_DEV, -1, d_model),
                      ("w_out", l): gw_out.reshape(N_DEV, -1, d_model),
                      ("w_branch", l): jnp.transpose(gw_branch.reshape(3, MIX, N_DEV, -1), (2, 0, 1, 3)).astype(bf16)})
        dproj, g_ln_g, g_ln_b, g_sgu_w, g_sgu_bt = sgu_bwd(sv["proj"], p["ln_g"], p["ln_b"], p["sgu_w"], p["sgu_bt"],
                                                         d_a, dproj, name=t + "sgu")
        dproj, g_sink = scattering(t + "swa", swa_bwd, sv["proj"], p["sink_row"], tables, d_b, dproj)
        dproj, dxc, dba, g_alog, g_dtb, g_dnorm = scattering(
            t + "deltanet", dn_bwd, sv["xc"], sv["proj"], sv["p_ba"], p["alog_row"], p["dtb_row"], p["norm_row"],
            sv["states"], sv["tri"], d_c, dproj)
        dproj, g_conv8 = scattering(t + "dn_conv", conv_bwd, sv["proj"], p["conv_w8"], dxc, dproj)
        gw_main = scattering(t + "in_proj_dw", matmul, sv["h"], dproj, "tn", bf16, tk=2048)
        gw_ba = matmul(sv["h"], dba, "tn", bf16, tk=2048, name=t + "in_proj_ba_dw")
        parts["w_in", l] = _w_in_parts(gw_main, gw_ba)
        d_h = scattering(t + "in_proj_dx", matmul, dproj, p["w_main"], "nt", f32, tk=2304, tn=1024)
        d_h = matmul(dba, p["w_ba"], "nt", f32, residual=d_h, name=t + "in_proj_ba_dx")
        dx, g_attn = rmsnorm_bwd(sv["x_in"], p["attn_norm"], d_h, dx_mid, name=t + "attn_norm")
        per_layer.append(dict(
            dn_conv_w=jnp.transpose(g_conv8[:DN_CONV].reshape(DN_CONV, N_DEV, -1), (1, 0, 2)),
            attn_norm=g_attn, sgu_ln_g=g_ln_g, sgu_ln_b=g_ln_b, sgu_w=g_sgu_w, sgu_b=g_sgu_bt.T,
            attn_sinks=g_sink[0, :SWA_HEADS], dn_a_log=g_alog[0, DN_HEADS:2 * DN_HEADS],
            dn_dt_bias=g_dtb[0, DN_HEADS:2 * DN_HEADS], dn_norm=g_dnorm, ffn_norm=g_ffn))
    per_layer.reverse()

    conv_parts = jnp.concatenate([pp["dn_conv_w"] for pp in per_layer], axis=1)
    rep_grads = {n: jnp.stack([pp[n].reshape(given[n].shape[1:]) for pp in per_layer]) for n in rep_names}
    rep_grads["final_norm"] = d_final_norm[0]
    rep_names = rep_names + ["final_norm"]
    rep_rows = [_rows128(given[n]).shape[0] for n in rep_names]
    pad_rows = -sum(rep_rows) % 16

    def small_rows(values):
        pieces = [_rows128(values[n]) for n in rep_names]
        return jnp.concatenate(pieces + [jnp.zeros((pad_rows, LANES), f32)], axis=0)

    (got_conv,), (small_all,) = exchange(
        [Exchange("scatter", [conv_parts]), Exchange("gather", [small_rows(rep_grads)])], name="exchange_last")

    results = [{}, {}, {}, {}]
    for n in shard_names:
        shp = given[n].shape
        two = (-1, shp[-1])
        by_layer = [got_conv] if n == "dn_conv_w" else [received[n, l].reshape(N_DEV, -1, shp[-1]) for l in range(depth)]
        outs = adamw(given[n].reshape(two), given["m_" + n].reshape(two), given["v_" + n].reshape(two), by_layer,
                     name="adamw_" + n)
        for res, val in zip(results, outs):
            res[n] = val.reshape(shp)
    outs = adamw(small_rows(given), small_rows({n: given["m_" + n] for n in rep_names}),
                 small_rows({n: given["v_" + n] for n in rep_names}), [small_all], name="adamw_replicated")
    for res, val in zip(results, outs):
        row = 0
        for n, nr in zip(rep_names, rep_rows):
            res[n] = val[row:row + nr].reshape(-1)[:given[n].size].reshape(given[n].shape)
            row += nr
    order = ["attn_norm", "w_in", "sgu_ln_g", "sgu_ln_b", "sgu_w", "sgu_b", "attn_sinks", "dn_conv_w", "dn_a_log",
             "dn_dt_bias", "dn_norm", "w_branch", "w_out", "ffn_norm", "w_gate_up", "w_down", "final_norm"]
    return (loss, dx.reshape(x.shape), *[res[n] for res in results for n in order])
```

```python
import functools

import jax
import jax.numpy as jnp
from jax import lax
from jax.experimental import pallas as pl
from jax.experimental.pallas import tpu as pltpu

f32 = jnp.float32
bf16 = jnp.bfloat16

N_DEV = 8
MESH_AXES = ("x", "y", "c")
NORM_EPS = 1e-6
MIX = 512
SGU_GROUPS, SGU_CHUNK = 4, 128
SWA_HEADS, SWA_KV, SWA_HD, WINDOW = 8, 2, 64, 128
ROPE_THETA, ROPE_DIM = 500000.0, 16
DN_HEADS, DN_HD, DN_CONV, DN_CHUNK = 4, 128, 4, 64
ADAM_LR, ADAM_B1, ADAM_B2, ADAM_EPS, ADAM_WD, ADAM_STEP = 0.001, 0.9, 0.999, 1e-08, 0.01, 10

LANES = 128
VMEM_LIMIT = 56 * 1024 * 1024

SEC_A, SEC_Z, SEC_QKV, SEC_G, SEC_B = 0, 1024, 1536, 3072, 6144
W_MAIN = 6912
C_B, C_QKV, C_Z, C_BA, C_G = 1024, 1792, 3328, 3840, 3848


def _params(n_axes, **kw):
    return pltpu.CompilerParams(dimension_semantics=("arbitrary",) * n_axes, vmem_limit_bytes=VMEM_LIMIT, **kw)


def _tile(n, target, mult=LANES):
    if n <= target:
        return n
    best = None
    for t in range(mult, target + 1, mult):
        if n % t == 0:
            best = t
    assert best is not None, (n, target, mult)
    return best


def _dg(a, b, ca, cb):
    return lax.dot_general(a.astype(bf16), b.astype(bf16), (((ca,), (cb,)), ((), ())), preferred_element_type=f32)


def _dg3(a, b, ca, cb):
    a_hi, b_hi = a.astype(bf16), b.astype(bf16)
    a_lo, b_lo = (a - a_hi.astype(f32)).astype(bf16), (b - b_hi.astype(f32)).astype(bf16)

    def dot(p, q):
        return lax.dot_general(p, q, (((ca,), (cb,)), ((), ())), preferred_element_type=f32)

    return dot(a_hi, b_hi) + (dot(a_hi, b_lo) + dot(a_lo, b_hi))


def _differentiable_dot(core):
    @functools.partial(jax.custom_vjp, nondiff_argnums=(2, 3))
    def dot(a, b, ca, cb):
        return core(a, b, ca, cb)

    def fwd(a, b, ca, cb):
        return core(a, b, ca, cb), (a, b)

    def bwd(ca, cb, res, ct):
        a, b = res
        da = core(ct, b, 1, 1 - cb) if ca == 1 else core(b, ct, 1 - cb, 1)
        db = core(a, ct, 1 - ca, 0) if cb == 0 else core(ct, a, 0, 1 - ca)
        return da, db

    dot.defvjp(fwd, bwd)
    return dot


bdot = _differentiable_dot(_dg)
_hdot = _differentiable_dot(_dg3)


def hdot(a, b, ca=1, cb=0):
    return _hdot(a, b, ca, cb)


@functools.partial(jax.custom_vjp, nondiff_argnums=(1,))
def lroll(x, shift):
    return pltpu.roll(x, shift, 1)


def _lroll_fwd(x, shift):
    return pltpu.roll(x, shift, 1), None


def _lroll_bwd(shift, _, ct):
    return (pltpu.roll(ct, ct.shape[1] - shift, 1),)


lroll.defvjp(_lroll_fwd, _lroll_bwd)


@functools.partial(jax.custom_vjp, nondiff_argnums=(1,))
def tri_inv(low, nil):
    n = low.shape[0]
    row = lax.broadcasted_iota(jnp.int32, (n, n), 0)
    col = lax.broadcasted_iota(jnp.int32, (n, n), 1)
    eye = (row == col).astype(f32)
    m = -low
    p = eye + m
    span = 2
    while span < nil:
        dot = hdot if span == 2 else (lambda a, b: bdot(a, b, 1, 0))
        m = dot(m, m)
        p = p + dot(p, m)
        span *= 2
    return p


def _tri_inv_fwd(low, nil):
    t = tri_inv(low, nil)
    return t, t


def _tri_inv_bwd(nil, t, dt):
    return (-bdot(t, bdot(dt, t, 1, 1), 0, 0),)


tri_inv.defvjp(_tri_inv_fwd, _tri_inv_bwd)


@jax.custom_vjp
def tri_inv_known(low, t):
    return t


def _tri_inv_known_fwd(low, t):
    return t, t


def _tri_inv_known_bwd(t, dt):
    return _tri_inv_bwd(None, t, dt) + (jnp.zeros_like(t),)


tri_inv_known.defvjp(_tri_inv_known_fwd, _tri_inv_known_bwd)


def _sigmoid(x):
    return 1.0 / (1.0 + jnp.exp(-x))


def _rms(x, g):
    return x * lax.rsqrt(jnp.mean(x * x, axis=-1, keepdims=True) + NORM_EPS) * g


def _lane_col(x, lane_idx):
    lane = lax.broadcasted_iota(jnp.int32, x.shape, 1)
    return jnp.sum(jnp.where(lane == lane_idx, x, 0.0), axis=1, keepdims=True)


def matmul(a, b, mode, out_dtype, *, residual=None, group=None, side=None, tm=1024, tn=768, tk=1024, name):
    dims = {"a": ("m", "k") if mode != "tn" else ("k", "m"),
            "b": {"nn": ("k", "n"), "nt": ("n", "k"), "tn": ("k", "n")}[mode], "o": ("m", "n")}
    full, groups = {}, 1
    for arr, key in ((a, "a"), (b, "b")):
        grouped = group in dims[key]
        if grouped:
            groups = arr.shape[0]
        full[dims[key][0]], full[dims[key][1]] = arr.shape[1:] if grouped else arr.shape
    want = {"m": tm, "n": tn, "k": tk}
    per_step = min(groups, 4) if group == "k" else 1
    tiles = {d: full[d] if d == group else _tile(full[d], want[d]) for d in "mnk"}
    steps = {d: groups // per_step if d == group else full[d] // tiles[d] for d in "mnk"}

    def spec(key):
        d0, d1 = dims[key]

        def index(i, j, kk):
            at = {"m": i, "n": j, "k": kk}
            if group in (d0, d1):
                return (at[group], 0 if d0 == group else at[d0], 0 if d1 == group else at[d1])
            return (at[d0], at[d1])

        block = (tiles[d0], tiles[d1])
        if group in (d0, d1):
            block = ((per_step if group == "k" else None),) + block
        return pl.BlockSpec(block, index)

    ca, cb = {"nn": (1, 0), "nt": (1, 1), "tn": (0, 0)}[mode]
    nk = steps["k"]
    o_spec = spec("o")
    out_shape = (groups, full["m"], full["n"]) if group in ("m", "n") else (full["m"], full["n"])
    has_res = residual is not None

    def product(a_ref, b_ref):
        if group != "k":
            return _dg(a_ref[...], b_ref[...], ca, cb)
        total = _dg(a_ref[0], b_ref[0], ca, cb)
        for g in range(1, per_step):
            total = total + _dg(a_ref[g], b_ref[g], ca, cb)
        return total

    def body(a_ref, b_ref, *rest):
        r_ref = rest[0] if has_res else None
        o_ref = rest[1 if has_res else 0]

        def emit(acc):
            o_ref[...] = (acc + r_ref[...] if has_res else acc).astype(out_dtype)

        if nk == 1:
            emit(product(a_ref, b_ref))
            return
        acc_ref = rest[-1]
        kk = pl.program_id(2)

        @pl.when(kk == 0)
        def _():
            acc_ref[...] = jnp.zeros_like(acc_ref)

        acc_ref[...] += product(a_ref, b_ref)

        @pl.when(kk == nk - 1)
        def _():
            emit(acc_ref[...])

    res = host_call(
        side, body, name=name, grid=(steps["m"], steps["n"], nk),
        in_specs=[spec("a"), spec("b")] + ([o_spec] if has_res else []), out_specs=[o_spec],
        out_shape=[jax.ShapeDtypeStruct(out_shape, out_dtype)],
        scratch_shapes=[pltpu.VMEM((tiles["m"], tiles["n"]), f32)] if nk > 1 else [], aliases={},
        args=(a, b) + ((residual,) if has_res else ()))
    return res[0] if side is None else (res[0][0], res[1])


def rmsnorm_fwd(x, g_row, *, name):
    s, d = x.shape
    ts = _tile(s, 512, 16)

    def body(x_ref, g_ref, o_ref):
        o_ref[...] = _rms(x_ref[...], g_ref[...]).astype(bf16)

    return pl.pallas_call(
        body, name=name, grid=(s // ts,),
        in_specs=[pl.BlockSpec((ts, d), lambda i: (i, 0)), pl.BlockSpec((1, d), lambda i: (0, 0))],
        out_specs=pl.BlockSpec((ts, d), lambda i: (i, 0)), out_shape=jax.ShapeDtypeStruct((s, d), bf16),
        compiler_params=_params(1),
    )(x, g_row)


def rmsnorm_bwd(x, g_row, dh, dres, *, name):
    s, d = x.shape
    ts = _tile(s, 512, 16)

    def body(x_ref, g_ref, dh_ref, dres_ref, dx_ref, dg_ref):
        @pl.when(pl.program_id(0) == 0)
        def _():
            dg_ref[...] = jnp.zeros_like(dg_ref)

        _, vjp = jax.vjp(_rms, x_ref[...], g_ref[...])
        dx, dg = vjp(dh_ref[...])
        dx_ref[...] = dx + dres_ref[...]
        dg_ref[...] += dg

    row = pl.BlockSpec((ts, d), lambda i: (i, 0))
    vec = pl.BlockSpec((1, d), lambda i: (0, 0))
    return pl.pallas_call(
        body, name=name, grid=(s // ts,), in_specs=[row, vec, row, row], out_specs=[row, vec],
        out_shape=[jax.ShapeDtypeStruct((s, d), f32), jax.ShapeDtypeStruct((1, d), f32)],
        compiler_params=_params(1),
    )(x, g_row, dh, dres)


def loss_head(x, g_row, target, *, name):
    s, d = x.shape
    ts = _tile(s, 512, 16)

    def body(x_ref, g_ref, t_ref, dx_ref, dg_ref, loss_ref):
        @pl.when(pl.program_id(0) == 0)
        def _():
            dg_ref[...] = jnp.zeros_like(dg_ref)
            loss_ref[...] = jnp.zeros_like(loss_ref)

        y, vjp = jax.vjp(_rms, x_ref[...], g_ref[...])
        err = y - t_ref[...]
        dx, dg = vjp(err * (1.0 / d))
        dx_ref[...] = dx
        dg_ref[...] += dg
        loss_ref[...] += 0.5 * jnp.sum(jnp.sum(err * err, axis=1, keepdims=True) * (1.0 / d), axis=0, keepdims=True)

    row = pl.BlockSpec((ts, d), lambda i: (i, 0))
    vec = pl.BlockSpec((1, d), lambda i: (0, 0))
    one = pl.BlockSpec((1, LANES), lambda i: (0, 0))
    return pl.pallas_call(
        body, name=name, grid=(s // ts,), in_specs=[row, vec, row], out_specs=[row, vec, one],
        out_shape=[jax.ShapeDtypeStruct((s, d), f32), jax.ShapeDtypeStruct((1, d), f32),
                   jax.ShapeDtypeStruct((1, LANES), f32)],
        compiler_params=_params(1),
    )(x, g_row, target)


def _sgu_chunk(p_a, ln_g, ln_b, w, b_t):
    t = SGU_CHUNK
    u = jax.nn.gelu(p_a[:, :MIX])
    v = jax.nn.gelu(p_a[:, MIX:])
    vc = v - jnp.mean(v, axis=-1, keepdims=True)
    vn = vc * lax.rsqrt(jnp.mean(vc * vc, axis=-1, keepdims=True) + NORM_EPS) * ln_g + ln_b
    causal = lax.broadcasted_iota(jnp.int32, (t, t), 0) >= lax.broadcasted_iota(jnp.int32, (t, t), 1)
    outs = []
    for g in range(SGU_GROUPS):
        sl = slice(g * LANES, (g + 1) * LANES)
        mixed = bdot(jnp.where(causal, w[g], 0.0), vn[:, sl], 1, 0) + b_t[:, g:g + 1]
        outs.append(u[:, sl] * mixed)
    return jnp.concatenate(outs, axis=1)


def _sgu_specs(s, ts):
    return [pl.BlockSpec((ts, 2 * MIX), lambda i: (i, SEC_A // (2 * MIX))),
            pl.BlockSpec((1, MIX), lambda i: (0, 0)), pl.BlockSpec((1, MIX), lambda i: (0, 0)),
            pl.BlockSpec((SGU_GROUPS, SGU_CHUNK, SGU_CHUNK), lambda i: (0, 0, 0)),
            pl.BlockSpec((SGU_CHUNK, SGU_GROUPS), lambda i: (0, 0))]


def sgu_fwd(proj, ln_g, ln_b, w, b_t, *, name):
    s = proj.shape[0]
    ts = _tile(s, 512)
    n_chunk = ts // SGU_CHUNK

    def body(p_ref, g_ref, b_ref, w_ref, bt_ref, o_ref):
        def step(c, carry):
            rows = pl.ds(pl.multiple_of(c * SGU_CHUNK, SGU_CHUNK), SGU_CHUNK)
            o_ref[rows, :] = _sgu_chunk(p_ref[rows, :].astype(f32), g_ref[...], b_ref[...], w_ref[...], bt_ref[...]).astype(bf16)
            return carry
        lax.fori_loop(0, n_chunk, step, 0)

    return pl.pallas_call(
        body, name=name, grid=(s // ts,), in_specs=_sgu_specs(s, ts),
        out_specs=pl.BlockSpec((ts, MIX), lambda i: (i, 0)), out_shape=jax.ShapeDtypeStruct((s, MIX), bf16),
        compiler_params=_params(1),
    )(proj, ln_g, ln_b, w, b_t)


def sgu_bwd(proj, ln_g, ln_b, w, b_t, d_out, dproj, *, name):
    s = proj.shape[0]
    ts = _tile(s, 512)
    n_chunk = ts // SGU_CHUNK

    def body(p_ref, g_ref, b_ref, w_ref, bt_ref, do_ref, _, dp_ref, dg_ref, db_ref, dw_ref, dbt_ref):
        @pl.when(pl.program_id(0) == 0)
        def _():
            dg_ref[...] = jnp.zeros_like(dg_ref)
            db_ref[...] = jnp.zeros_like(db_ref)
            dw_ref[...] = jnp.zeros_like(dw_ref)
            dbt_ref[...] = jnp.zeros_like(dbt_ref)

        def step(c, carry):
            rows = pl.ds(pl.multiple_of(c * SGU_CHUNK, SGU_CHUNK), SGU_CHUNK)
            _, vjp = jax.vjp(_sgu_chunk, p_ref[rows, :].astype(f32), g_ref[...], b_ref[...], w_ref[...], bt_ref[...])
            dp, dg, db, dw, dbt = vjp(do_ref[rows, :])
            dp_ref[rows, :] = dp.astype(bf16)
            dg_ref[...] += dg
            db_ref[...] += db
            dw_ref[...] += dw
            dbt_ref[...] += dbt
            return carry
        lax.fori_loop(0, n_chunk, step, 0)

    specs = _sgu_specs(s, ts)
    return pl.pallas_call(
        body, name=name, grid=(s // ts,),
        in_specs=specs + [pl.BlockSpec((ts, MIX), lambda i: (i, 0)), pl.BlockSpec(memory_space=pl.ANY)],
        out_specs=[specs[0], specs[1], specs[2], specs[3], specs[4]],
        out_shape=[jax.ShapeDtypeStruct(dproj.shape, bf16), jax.ShapeDtypeStruct((1, MIX), f32),
                   jax.ShapeDtypeStruct((1, MIX), f32), jax.ShapeDtypeStruct(w.shape, f32),
                   jax.ShapeDtypeStruct(b_t.shape, f32)],
        input_output_aliases={6: 0}, compiler_params=_params(1),
    )(proj, ln_g, ln_b, w, b_t, d_out, dproj)


def rope_tables(posf, inv_freq, *, name):
    s = posf.shape[0]
    ts = _tile(s, 1024, 8)
    half = ROPE_DIM // 2

    def body(pos_ref, inv_ref, o_ref):
        d = lax.broadcasted_iota(jnp.int32, (1, LANES), 1) % SWA_HD
        ang = pos_ref[...] * inv_ref[...]
        sin = jnp.sin(ang)
        o_ref[0] = jnp.cos(ang)
        o_ref[1] = jnp.where(d < half, sin, 0.0)
        o_ref[2] = jnp.where((d >= half) & (d < ROPE_DIM), sin, 0.0)

    return pl.pallas_call(
        body, name=name, grid=(s // ts,),
        in_specs=[pl.BlockSpec((ts, 1), lambda i: (i, 0)), pl.BlockSpec((1, LANES), lambda i: (0, 0))],
        out_specs=pl.BlockSpec((3, ts, LANES), lambda i: (0, i, 0)), out_shape=jax.ShapeDtypeStruct((3, s, LANES), f32),
        compiler_params=_params(1),
    )(posf, inv_freq)


def _rope(x, table):
    w = x.shape[1]
    half = ROPE_DIM // 2
    c, lo, hi = (jnp.concatenate([table[i]] * (w // LANES), axis=1) for i in range(3))
    return x * c - lroll(x, w - half) * lo + lroll(x, half) * hi


def _swa_block(q, kp, kc, vp, vc, sink_row, table_q, table_p, prev_ok, wide):
    t = WINDOW
    q = _rope(q, table_q) * (SWA_HD ** -0.5)
    keys = jnp.concatenate([_rope(kp, table_p), _rope(kc, table_q)], axis=0)
    vals = jnp.concatenate([vp, vc], axis=0)
    own = lax.broadcasted_iota(jnp.int32, (t, t), 0) >= lax.broadcasted_iota(jnp.int32, (t, t), 1)
    lane_half = lax.broadcasted_iota(jnp.int32, (t, LANES), 1) // SWA_HD
    group = SWA_HEADS // SWA_KV
    slabs = []
    for pair in range(SWA_HEADS // 2):
        q_pair = q[:, pair * LANES:(pair + 1) * LANES]
        acc = jnp.zeros((t, LANES), f32)
        for half in range(2):
            h = 2 * pair + half
            kv = h // group
            qm = jnp.where(lane_half == half, q_pair, 0.0)
            if half != kv:
                qm = lroll(qm, SWA_HD)
            if wide:
                both = bdot(qm, keys, 1, 1)
                s_prev, s_own = both[:, :t], both[:, t:]
            else:
                s_prev, s_own = bdot(qm, keys[:t], 1, 1), bdot(qm, keys[t:], 1, 1)
            logits = jnp.where(own, s_own, jnp.where(prev_ok, s_prev, -1e30))
            sink = _lane_col(sink_row, h)
            m = lax.stop_gradient(jnp.maximum(jnp.max(logits, axis=1, keepdims=True), sink))
            p = jnp.exp(logits - m)
            probs = p * (1.0 / (jnp.sum(p, axis=1, keepdims=True) + jnp.exp(sink - m)))
            p_prev, p_own = jnp.where(own, 0.0, probs), jnp.where(own, probs, 0.0)
            if wide:
                o = bdot(jnp.concatenate([p_prev, p_own], axis=1), vals, 1, 0)
            else:
                o = bdot(p_prev, vals[:t], 1, 0) + bdot(p_own, vals[t:], 1, 0)
            o = jnp.where(lane_half == kv, o, 0.0)
            if half != kv:
                o = lroll(o, SWA_HD)
            acc = acc + o
        slabs.append(acc)
    return jnp.concatenate(slabs, axis=1)


def _swa_in_specs(nc, clamp):
    t = WINDOW
    qb, kb, vb = SEC_B // MIX, (SEC_B + MIX) // LANES, (SEC_B + MIX + LANES) // LANES

    def cur(i):
        return jnp.minimum(i, nc - 1) if clamp else i

    def prev(i):
        return jnp.maximum(cur(i) - 1, 0)

    return [pl.BlockSpec((t, MIX), lambda i: (cur(i), qb)),
            pl.BlockSpec((t, LANES), lambda i: (prev(i), kb)), pl.BlockSpec((t, LANES), lambda i: (cur(i), kb)),
            pl.BlockSpec((t, LANES), lambda i: (prev(i), vb)), pl.BlockSpec((t, LANES), lambda i: (cur(i), vb)),
            pl.BlockSpec((1, LANES), lambda i: (0, 0)),
            pl.BlockSpec((3, t, LANES), lambda i: (0, cur(i), 0)), pl.BlockSpec((3, t, LANES), lambda i: (0, prev(i), 0))]


def swa_fwd(proj, sink_row, tables, *, name):
    s = proj.shape[0]
    nc = s // WINDOW

    def body(q_ref, kp_ref, kc_ref, vp_ref, vc_ref, sink_ref, tq_ref, tp_ref, o_ref):
        prev_ok = pl.program_id(0) > 0
        blocks = [r[...].astype(f32) for r in (q_ref, kp_ref, kc_ref, vp_ref, vc_ref)]
        o_ref[...] = _swa_block(*blocks, sink_ref[...],
                                tq_ref[...], tp_ref[...], prev_ok, True).astype(bf16)

    return pl.pallas_call(
        body, name=name, grid=(nc,), in_specs=_swa_in_specs(nc, False),
        out_specs=pl.BlockSpec((WINDOW, MIX), lambda i: (i, 0)), out_shape=jax.ShapeDtypeStruct((s, MIX), bf16),
        compiler_params=_params(1),
    )(proj, proj, proj, proj, proj, sink_row, tables, tables)


def swa_bwd(proj, sink_row, tables, d_out, dproj, *, side=None, name):
    s = proj.shape[0]
    nc = s // WINDOW
    t = WINDOW

    def body(q_ref, kp_ref, kc_ref, vp_ref, vc_ref, sink_ref, tq_ref, tp_ref, do_ref, _,
             dp_ref, dsink_ref, cq_ref, ck_ref, cv_ref):
        i = pl.program_id(0)

        @pl.when(i == 0)
        def _():
            dsink_ref[...] = jnp.zeros_like(dsink_ref)

        def write(dk_prev, dv_prev):
            dp_ref[:, :MIX] = cq_ref[...].astype(bf16)
            dp_ref[:, MIX:MIX + LANES] = (ck_ref[...] + dk_prev).astype(bf16)
            dp_ref[:, MIX + LANES:] = (cv_ref[...] + dv_prev).astype(bf16)

        @pl.when(i < nc)
        def _():
            fn = functools.partial(_swa_block, table_q=tq_ref[...], table_p=tp_ref[...], prev_ok=i > 0, wide=False)
            blocks = [r[...].astype(f32) for r in (q_ref, kp_ref, kc_ref, vp_ref, vc_ref)]
            _, vjp = jax.vjp(fn, *blocks, sink_ref[...])
            dq, dkp, dkc, dvp, dvc, dsink = vjp(do_ref[...])
            dsink_ref[...] += dsink

            @pl.when(i > 0)
            def _():
                write(dkp, dvp)

            cq_ref[...] = dq
            ck_ref[...] = dkc
            cv_ref[...] = dvc

        @pl.when(i == nc)
        def _():
            write(0.0, 0.0)

    return host_call(
        side, body, name=name, grid=(nc + 1,),
        in_specs=_swa_in_specs(nc, True) + [pl.BlockSpec((t, MIX), lambda i: (jnp.minimum(i, nc - 1), 0)),
                                            pl.BlockSpec(memory_space=pl.ANY)],
        out_specs=[pl.BlockSpec((t, MIX + 2 * LANES), lambda i: (jnp.maximum(i - 1, 0), SEC_B // (MIX + 2 * LANES))),
                   pl.BlockSpec((1, LANES), lambda i: (0, 0))],
        out_shape=[jax.ShapeDtypeStruct(dproj.shape, bf16), jax.ShapeDtypeStruct((1, LANES), f32)],
        scratch_shapes=[pltpu.VMEM((t, MIX), f32), pltpu.VMEM((t, LANES), f32), pltpu.VMEM((t, LANES), f32)],
        aliases={9: 0}, args=(proj, proj, proj, proj, proj, sink_row, tables, tables, d_out, dproj))


CONV_PAD = 16


def _conv_taps(xp, rows):
    off = CONV_PAD - (DN_CONV - 1)
    return [xp[off + i:off + i + rows] for i in range(DN_CONV)]


def _conv_pre(taps, w):
    pre = taps[0] * w[0:1]
    for i in range(1, DN_CONV):
        pre = pre + taps[i] * w[i:i + 1]
    return pre


def conv_fwd(proj, conv_w8, *, name):
    s = proj.shape[0]
    wq = 3 * MIX
    ts = _tile(s, 512)
    nb = ts // CONV_PAD

    def body(x_ref, prev_ref, w_ref, o_ref):
        prev = jnp.where(pl.program_id(0) > 0, prev_ref[...].astype(f32), 0.0)
        pre = _conv_pre(_conv_taps(jnp.concatenate([prev, x_ref[...].astype(f32)], axis=0), ts), w_ref[...])
        o_ref[...] = pre * _sigmoid(pre)

    return pl.pallas_call(
        body, name=name, grid=(s // ts,),
        in_specs=[pl.BlockSpec((ts, wq), lambda i: (i, SEC_QKV // wq)),
                  pl.BlockSpec((CONV_PAD, wq), lambda i: (jnp.maximum(i * nb - 1, 0), SEC_QKV // wq)),
                  pl.BlockSpec((CONV_PAD, wq), lambda i: (0, 0))],
        out_specs=pl.BlockSpec((ts, wq), lambda i: (i, 0)), out_shape=jax.ShapeDtypeStruct((s, wq), f32),
        compiler_params=_params(1),
    )(proj, proj, conv_w8)


def conv_bwd(proj, conv_w8, dxc, dproj, *, side=None, name):
    s = proj.shape[0]
    wq = 3 * MIX
    ts = _tile(s, 512)
    nb = ts // CONV_PAD
    nt = s // ts
    last_blk = s // CONV_PAD - 1

    def body(x_ref, prev_ref, next_ref, w_ref, d_ref, dnext_ref, _, dp_ref, dw_ref):
        i = pl.program_id(0)

        @pl.when(i == 0)
        def _():
            dw_ref[...] = jnp.zeros_like(dw_ref)

        w = w_ref[...]
        prev = jnp.where(i > 0, prev_ref[...].astype(f32), 0.0)
        more = i < nt - 1
        xp = jnp.concatenate([prev, x_ref[...].astype(f32), jnp.where(more, next_ref[...].astype(f32), 0.0)], axis=0)
        taps = _conv_taps(xp, ts + CONV_PAD)
        pre = _conv_pre(taps, w)
        sig = _sigmoid(pre)
        dxc_ext = jnp.concatenate([d_ref[...], jnp.where(more, dnext_ref[...], 0.0)], axis=0)
        dpre = dxc_ext * sig * (1.0 + pre * (1.0 - sig))
        d_raw = jnp.zeros((ts, wq), f32)
        dws = []
        for k in range(DN_CONV):
            shift = DN_CONV - 1 - k
            d_raw = d_raw + dpre[shift:shift + ts] * w[k:k + 1]
            dws.append(jnp.sum(dpre[:ts] * taps[k][:ts], axis=0, keepdims=True))
        dp_ref[...] = d_raw.astype(bf16)
        dw_ref[...] += jnp.concatenate(dws + [jnp.zeros((CONV_PAD - DN_CONV, wq), f32)], axis=0)

    sec = SEC_QKV // wq
    return host_call(
        side, body, name=name, grid=(nt,),
        in_specs=[pl.BlockSpec((ts, wq), lambda i: (i, sec)),
                  pl.BlockSpec((CONV_PAD, wq), lambda i: (jnp.maximum(i * nb - 1, 0), sec)),
                  pl.BlockSpec((CONV_PAD, wq), lambda i: (jnp.minimum((i + 1) * nb, last_blk), sec)),
                  pl.BlockSpec((CONV_PAD, wq), lambda i: (0, 0)),
                  pl.BlockSpec((ts, wq), lambda i: (i, 0)),
                  pl.BlockSpec((CONV_PAD, wq), lambda i: (jnp.minimum((i + 1) * nb, last_blk), 0)),
                  pl.BlockSpec(memory_space=pl.ANY)],
        out_specs=[pl.BlockSpec((ts, wq), lambda i: (i, sec)), pl.BlockSpec((CONV_PAD, wq), lambda i: (0, 0))],
        out_shape=[jax.ShapeDtypeStruct(dproj.shape, bf16), jax.ShapeDtypeStruct((CONV_PAD, wq), f32)],
        scratch_shapes=[], aliases={6: 0}, args=(proj, proj, proj, conv_w8, dxc, dxc, dproj))


def _dn_chunk(state, xc, z, ba, alog_row, dtb_row, norm_row, t_known):
    c, nh = DN_CHUNK, DN_HEADS
    n = c * nh
    row = lax.broadcasted_iota(jnp.int32, (n, n), 0)
    col = lax.broadcasted_iota(jnp.int32, (n, n), 1)
    same_head = (row // c) == (col // c)
    tril, strict = same_head & (row >= col), same_head & (row > col)
    tril_c = lax.broadcasted_iota(jnp.int32, (c, c), 0) >= lax.broadcasted_iota(jnp.int32, (c, c), 1)
    beta_all = _sigmoid(ba)
    g_all = -jnp.exp(alog_row) * jax.nn.softplus(ba + dtb_row)
    gc_all = hdot(tril_c.astype(f32), g_all)
    gc_t = gc_all.T

    def stack(piece):
        return jnp.concatenate([piece(h) for h in range(nh)], axis=0)

    q = stack(lambda h: xc[:, h * DN_HD:(h + 1) * DN_HD])
    k = stack(lambda h: xc[:, MIX + h * DN_HD:MIX + (h + 1) * DN_HD])
    v = stack(lambda h: xc[:, 2 * MIX + h * DN_HD:2 * MIX + (h + 1) * DN_HD])
    zs = stack(lambda h: z[:, h * DN_HD:(h + 1) * DN_HD])
    q = q * lax.rsqrt(jnp.sum(q * q, axis=-1, keepdims=True) + NORM_EPS) * (DN_HD ** -0.5)
    k = k * lax.rsqrt(jnp.sum(k * k, axis=-1, keepdims=True) + NORM_EPS)
    beta = stack(lambda h: _lane_col(beta_all, h))
    g_cols = [_lane_col(gc_all, nh + h) for h in range(nh)]
    g_col = jnp.concatenate(g_cols, axis=0)
    g_row = jnp.concatenate([gc_t[nh + h:nh + h + 1, :] for h in range(nh)], axis=1)
    g_last = stack(lambda h: jnp.broadcast_to(g_cols[h][c - 1:c, :], (c, 1)))
    decay = jnp.where(tril, jnp.exp(jnp.where(tril, g_col - g_row, 0.0)), 0.0)
    kb = k * beta
    low = jnp.where(strict, bdot(kb, k, 1, 1) * decay, 0.0)
    t_inv = tri_inv(low, c) if t_known is None else tri_inv_known(low, t_known)
    e_gc = jnp.exp(g_col)
    uw = bdot(t_inv, jnp.concatenate([v * beta, kb * e_gc], axis=1), 1, 0)
    u, w = uw[:, :DN_HD], uw[:, DN_HD:]
    attn = bdot(q, k, 1, 1) * decay
    own = (lax.broadcasted_iota(jnp.int32, (n, nh * DN_HD), 1) // DN_HD
           == lax.broadcasted_iota(jnp.int32, (n, nh * DN_HD), 0) // c)

    def spread(a):
        return jnp.where(own, jnp.concatenate([a] * nh, axis=1), 0.0)

    v_new = u - bdot(spread(w), state, 1, 0)
    o = bdot(spread(q * e_gc), state, 1, 0) + bdot(attn, v_new, 1, 0)
    keep = stack(lambda h: jnp.broadcast_to(jnp.exp(g_cols[h][c - 1:c, :]), (DN_HD, 1)))
    new_state = state * keep + bdot(spread(k * jnp.exp(g_last - g_col)), v_new, 0, 0)
    out = _rms(o, norm_row) * (zs * _sigmoid(zs))
    return new_state, jnp.concatenate([out[h * c:(h + 1) * c] for h in range(nh)], axis=1), t_inv


DN_STEP = 4 * DN_CHUNK


def _dn_step(state, xc, z, ba, alog_row, dtb_row, norm_row, t_known=None):
    outs, t_invs = [], []
    for c in range(DN_STEP // DN_CHUNK):
        rows = slice(c * DN_CHUNK, (c + 1) * DN_CHUNK)
        state, out, t_inv = _dn_chunk(state, xc[rows], z[rows], ba[rows], alog_row, dtb_row, norm_row,
                                      None if t_known is None else t_known[c])
        outs.append(out)
        t_invs.append(t_inv)
    return state, jnp.concatenate(outs, axis=0), jnp.stack(t_invs)


def _dn_specs(ts, order):
    zb = SEC_Z // MIX
    return [pl.BlockSpec((ts, 3 * MIX), lambda i: (order(i), 0)),
            pl.BlockSpec((ts, MIX), lambda i: (order(i), zb)),
            pl.BlockSpec((ts, LANES), lambda i: (order(i), 0)),
            pl.BlockSpec((1, LANES), lambda i: (0, 0)), pl.BlockSpec((1, LANES), lambda i: (0, 0)),
            pl.BlockSpec((1, LANES), lambda i: (0, 0))]


def dn_fwd(xc, proj, p_ba, alog_row, dtb_row, norm_row, *, side=None, name):
    s = xc.shape[0]
    ts = _tile(s, 512)
    n_step = ts // DN_STEP

    per_step = DN_STEP // DN_CHUNK
    n_tri = DN_HEADS * DN_CHUNK

    def body(xc_ref, z_ref, ba_ref, al_ref, dt_ref, nr_ref, o_ref, st_ref, tri_ref, state_ref):
        @pl.when(pl.program_id(0) == 0)
        def _():
            state_ref[...] = jnp.zeros_like(state_ref)

        def step(c, carry):
            rows = pl.ds(pl.multiple_of(c * DN_STEP, DN_STEP), DN_STEP)
            st_ref[c] = state_ref[...]
            new_state, out, t_invs = _dn_step(state_ref[...], xc_ref[rows, :], z_ref[rows, :].astype(f32), ba_ref[rows, :],
                                              al_ref[...], dt_ref[...], nr_ref[...])
            state_ref[...] = new_state
            o_ref[rows, :] = out.astype(bf16)
            tri_ref[pl.ds(c * per_step, per_step)] = t_invs
            return carry
        lax.fori_loop(0, n_step, step, 0)

    return host_call(
        side, body, name=name, grid=(s // ts,), in_specs=_dn_specs(ts, lambda i: i),
        out_specs=[pl.BlockSpec((ts, MIX), lambda i: (i, 0)),
                   pl.BlockSpec((n_step, DN_HEADS * DN_HD, DN_HD), lambda i: (i, 0, 0)),
                   pl.BlockSpec((n_step * per_step, n_tri, n_tri), lambda i: (i, 0, 0))],
        out_shape=[jax.ShapeDtypeStruct((s, MIX), bf16),
                   jax.ShapeDtypeStruct((s // DN_STEP, DN_HEADS * DN_HD, DN_HD), f32),
                   jax.ShapeDtypeStruct((s // DN_CHUNK, n_tri, n_tri), f32)],
        scratch_shapes=[pltpu.VMEM((DN_HEADS * DN_HD, DN_HD), f32)], aliases={},
        args=(xc, proj, p_ba, alog_row, dtb_row, norm_row))


def dn_bwd(xc, proj, p_ba, alog_row, dtb_row, norm_row, saved, tri, d_out, dproj, *, side=None, name):
    s = xc.shape[0]
    ts = _tile(s, 512)
    n_step = ts // DN_STEP
    nt = s // ts

    per_step = DN_STEP // DN_CHUNK
    n_tri = DN_HEADS * DN_CHUNK

    def body(xc_ref, z_ref, ba_ref, al_ref, dt_ref, nr_ref, st_ref, tri_ref, do_ref, _,
             dz_ref, dxc_ref, dba_ref, dal_ref, ddt_ref, dnr_ref, dstate_ref):
        @pl.when(pl.program_id(0) == 0)
        def _():
            dstate_ref[...] = jnp.zeros_like(dstate_ref)
            dal_ref[...] = jnp.zeros_like(dal_ref)
            ddt_ref[...] = jnp.zeros_like(ddt_ref)
            dnr_ref[...] = jnp.zeros_like(dnr_ref)

        def step(it, carry):
            c = n_step - 1 - it
            rows = pl.ds(pl.multiple_of(c * DN_STEP, DN_STEP), DN_STEP)
            t_known = tri_ref[pl.ds(c * per_step, per_step)]
            _, vjp = jax.vjp(lambda *a: _dn_step(*a, t_known=t_known)[:2], st_ref[c], xc_ref[rows, :],
                             z_ref[rows, :].astype(f32), ba_ref[rows, :], al_ref[...], dt_ref[...], nr_ref[...])
            d_in, dxc, dz, dba, dal, ddt, dnr = vjp((dstate_ref[...], do_ref[rows, :]))
            dstate_ref[...] = d_in
            dxc_ref[rows, :] = dxc
            dz_ref[rows, :] = dz.astype(bf16)
            dba_ref[rows, :] = dba.astype(bf16)
            dal_ref[...] += dal
            ddt_ref[...] += ddt
            dnr_ref[...] += dnr
            return carry
        lax.fori_loop(0, n_step, step, 0)

    def rev(i):
        return nt - 1 - i

    specs = _dn_specs(ts, rev)
    vec = pl.BlockSpec((1, LANES), lambda i: (0, 0))
    return host_call(
        side, body, name=name, grid=(nt,),
        in_specs=specs + [pl.BlockSpec((n_step, DN_HEADS * DN_HD, DN_HD), lambda i: (rev(i), 0, 0)),
                          pl.BlockSpec((n_step * per_step, n_tri, n_tri), lambda i: (rev(i), 0, 0)),
                          pl.BlockSpec((ts, MIX), lambda i: (rev(i), 0)), pl.BlockSpec(memory_space=pl.ANY)],
        out_specs=[specs[1], specs[0], specs[2], vec, vec, vec],
        out_shape=[jax.ShapeDtypeStruct(dproj.shape, bf16), jax.ShapeDtypeStruct((s, 3 * MIX), f32),
                   jax.ShapeDtypeStruct((s, LANES), bf16)] + [jax.ShapeDtypeStruct((1, LANES), f32)] * 3,
        scratch_shapes=[pltpu.VMEM((DN_HEADS * DN_HD, DN_HD), f32)], aliases={9: 0},
        args=(xc, proj, p_ba, alog_row, dtb_row, norm_row, saved, tri, d_out, dproj))


def _merge_in_specs(ts, d):
    row = pl.BlockSpec((ts, MIX), lambda i: (i, 0))
    return [row, row, row, pl.BlockSpec((ts, 3 * d), lambda i: (i, SEC_G // (3 * d))),
            pl.BlockSpec((3, MIX, d), lambda i: (0, 0, 0))]


def merge_fwd(out_a, out_b, out_c, proj, w_branch, *, name):
    s, d = out_a.shape[0], w_branch.shape[2]
    ts = _tile(s, 512, 16)

    def body(a_ref, b_ref, c_ref, g_ref, w_ref, o_ref):
        acc = jnp.zeros((ts, d), f32)
        for n, r in enumerate((a_ref, b_ref, c_ref)):
            acc = acc + _sigmoid(g_ref[:, n * d:(n + 1) * d].astype(f32)) * _dg(r[...], w_ref[n], 1, 0)
        o_ref[...] = acc.astype(bf16)

    return pl.pallas_call(
        body, name=name, grid=(s // ts,), in_specs=_merge_in_specs(ts, d),
        out_specs=pl.BlockSpec((ts, d), lambda i: (i, 0)), out_shape=jax.ShapeDtypeStruct((s, d), bf16),
        compiler_params=_params(1),
    )(out_a, out_b, out_c, proj, w_branch)


def merge_bwd(out_a, out_b, out_c, proj, w_branch, d_merged, dproj, *, name):
    s, d = out_a.shape[0], w_branch.shape[2]
    ts = _tile(s, 512, 16)

    def body(a_ref, b_ref, c_ref, g_ref, w_ref, dm_ref, _, dg_ref, da_ref, db_ref, dc_ref, dw_ref):
        @pl.when(pl.program_id(0) == 0)
        def _():
            dw_ref[...] = jnp.zeros_like(dw_ref)

        dm = dm_ref[...]
        for n, (r, dr) in enumerate(((a_ref, da_ref), (b_ref, db_ref), (c_ref, dc_ref))):
            gate = _sigmoid(g_ref[:, n * d:(n + 1) * d].astype(f32))
            branch = _dg(r[...], w_ref[n], 1, 0)
            dg_ref[:, n * d:(n + 1) * d] = (dm * branch * gate * (1.0 - gate)).astype(bf16)
            d_branch = dm * gate
            dr[...] = _dg(d_branch, w_ref[n], 1, 1)
            dw_ref[n] += _dg(r[...], d_branch, 0, 0)

    specs = _merge_in_specs(ts, d)
    row_f = pl.BlockSpec((ts, MIX), lambda i: (i, 0))
    return pl.pallas_call(
        body, name=name, grid=(s // ts,),
        in_specs=specs + [pl.BlockSpec((ts, d), lambda i: (i, 0)), pl.BlockSpec(memory_space=pl.ANY)],
        out_specs=[specs[3], row_f, row_f, row_f, specs[4]],
        out_shape=[jax.ShapeDtypeStruct(dproj.shape, bf16)] + [jax.ShapeDtypeStruct((s, MIX), f32)] * 3
        + [jax.ShapeDtypeStruct(w_branch.shape, f32)],
        input_output_aliases={6: 0}, compiler_params=_params(1),
    )(out_a, out_b, out_c, proj, w_branch, d_merged, dproj)


def gate_up_swiglu(h, w_gu, *, name):
    g2, d, w = w_gu.shape
    ng, s = g2 // 2, h.shape[0]
    tm = _tile(s, 1024, 16)

    def body(h_ref, wg_ref, wu_ref, g_ref, u_ref, a_ref):
        g16 = _dg(h_ref[...], wg_ref[...], 1, 0).astype(bf16)
        u16 = _dg(h_ref[...], wu_ref[...], 1, 0).astype(bf16)
        g_ref[...], u_ref[...] = g16, u16
        g = g16.astype(f32)
        a_ref[...] = (g * _sigmoid(g) * u16.astype(f32)).astype(bf16)

    out = pl.BlockSpec((None, tm, w), lambda i, j: (j, i, 0))
    return pl.pallas_call(
        body, name=name, grid=(s // tm, ng),
        in_specs=[pl.BlockSpec((tm, d), lambda i, j: (i, 0)), pl.BlockSpec((None, d, w), lambda i, j: (j, 0, 0)),
                  pl.BlockSpec((None, d, w), lambda i, j: (ng + j, 0, 0))],
        out_specs=[out, out, out], out_shape=[jax.ShapeDtypeStruct((ng, s, w), bf16)] * 3, compiler_params=_params(2),
    )(h, w_gu, w_gu)


def down_dx_swiglu(dx, w_down, gate, up, *, name):
    ng, w, d = w_down.shape
    s = dx.shape[0]
    tm = _tile(s, 1024, 16)

    def body(dx_ref, wd_ref, g_ref, u_ref, dg_ref, du_ref):
        d_act = _dg(dx_ref[...], wd_ref[...], 1, 1)
        g = g_ref[...].astype(f32)
        sig = _sigmoid(g)
        dg_ref[...] = (d_act * u_ref[...].astype(f32) * sig * (1.0 + g * (1.0 - sig))).astype(bf16)
        du_ref[...] = (d_act * g * sig).astype(bf16)

    blk = pl.BlockSpec((None, tm, w), lambda i, j: (j, i, 0))
    return pl.pallas_call(
        body, name=name, grid=(s // tm, ng),
        in_specs=[pl.BlockSpec((tm, d), lambda i, j: (i, 0)), pl.BlockSpec((None, w, d), lambda i, j: (j, 0, 0)), blk, blk],
        out_specs=[blk, blk], out_shape=[jax.ShapeDtypeStruct((ng, s, w), bf16)] * 2, compiler_params=_params(2),
    )(dx, w_down, gate, up)


def adamw(w, m, v, g_parts, *, name):
    n_layers = len(g_parts)
    n_parts, r, cols = g_parts[0].shape
    lanes = -(-cols // LANES) * LANES
    tr = _tile(r, max(16, (128 * 1024) // lanes), 16)
    nr = r // tr

    def body(w_ref, m_ref, v_ref, *rest):
        gp_refs, (g_ref, d_ref, nm_ref, nv_ref) = rest[:n_layers], rest[n_layers:]
        layer = pl.program_id(0)
        g = jnp.zeros((tr, cols), f32)
        for l, gp_ref in enumerate(gp_refs):
            g_l = gp_ref[0].astype(f32)
            for k in range(1, n_parts):
                g_l = g_l + gp_ref[k].astype(f32)
            g = jnp.where(layer == l, g_l, g)
        nm = ADAM_B1 * m_ref[...] + (1.0 - ADAM_B1) * g
        nv = ADAM_B2 * v_ref[...] + (1.0 - ADAM_B2) * jnp.square(g)
        m_hat = nm / (1.0 - ADAM_B1 ** ADAM_STEP)
        v_hat = nv / (1.0 - ADAM_B2 ** ADAM_STEP)
        g_ref[...] = g
        d_ref[...] = -ADAM_LR * (m_hat / (jnp.sqrt(v_hat) + ADAM_EPS) + ADAM_WD * w_ref[...])
        nm_ref[...] = nm
        nv_ref[...] = nv

    row = pl.BlockSpec((tr, cols), lambda l, i: (l * nr + i, 0))

    def parts_spec(own):
        return pl.BlockSpec((n_parts, tr, cols),
                            lambda l, i: (0, jnp.where(l == own, i, jnp.where(l < own, 0, nr - 1)), 0))

    return pl.pallas_call(
        body, name=name, grid=(n_layers, nr), in_specs=[row, row, row] + [parts_spec(l) for l in range(n_layers)],
        out_specs=[row] * 4, out_shape=[jax.ShapeDtypeStruct(w.shape, f32)] * 4, compiler_params=_params(2),
    )(w, m, v, *g_parts)


def _mesh_pos():
    return lax.axis_index("x"), lax.axis_index("y"), lax.axis_index("c")


def _dev_index(p):
    return 4 * p[0] + 2 * p[1] + p[2]


class Exchange:
    def __init__(self, kind, arrays):
        self.kind, self.arrays, self.n = kind, list(arrays), len(arrays)
        self.specs = [pl.BlockSpec(memory_space=pl.ANY)] * self.n
        self.out_shapes = [jax.ShapeDtypeStruct(((N_DEV,) if kind == "gather" else ()) + a.shape, a.dtype)
                           for a in self.arrays]
        self.scratch = [pltpu.SemaphoreType.DMA((self.n, N_DEV - 1)), pltpu.SemaphoreType.DMA((self.n, N_DEV - 1)),
                        pltpu.SemaphoreType.DMA((self.n,))]

    def _copies(self, in_refs, out_refs, sems, with_arrivals):
        send_sems, recv_sems, local_sems = sems
        x, y, c = _mesh_pos()
        mine = _dev_index((x, y, c))

        def src(a, slab):
            return in_refs[a] if self.kind == "gather" else in_refs[a].at[slab]

        local = [pltpu.make_async_copy(src(a, mine), out_refs[a].at[mine], local_sems.at[a]) for a in range(self.n)]
        sends, arrivals = [], []
        for k in range(1, N_DEV):
            peer = (1 - x if k & 4 else x, 1 - y if k & 2 else y, 1 - c if k & 1 else c)
            theirs = _dev_index(peer)
            for a in range(self.n):
                to = dict(send_sem=send_sems.at[a, k - 1], recv_sem=recv_sems.at[a, k - 1], device_id=peer,
                          device_id_type=pl.DeviceIdType.MESH)
                sends.append(pltpu.make_async_remote_copy(src_ref=src(a, theirs), dst_ref=out_refs[a].at[mine], **to))
                if with_arrivals:
                    arrivals.append(pltpu.make_async_remote_copy(src_ref=src(a, theirs),
                                                                 dst_ref=out_refs[a].at[theirs], **to))
        return local, sends, arrivals

    def start(self, in_refs, out_refs, sems):
        local, sends, _ = self._copies(in_refs, out_refs, sems, False)
        for cp in local + sends:
            cp.start()

    def wait(self, in_refs, out_refs, sems):
        local, sends, arrivals = self._copies(in_refs, out_refs, sems, True)
        for cp in arrivals:
            cp.wait_recv()
        for cp in sends:
            cp.wait_send()
        for cp in local:
            cp.wait()

    def run_around(self, grid, in_refs, out_refs, sems, *, before):
        at = None
        for axis, size in enumerate(grid):
            hit = pl.program_id(axis) == (0 if before else size - 1)
            at = hit if at is None else at & hit

        @pl.when(at)
        def _():
            (self.start if before else self.wait)(in_refs, out_refs, sems)


def host_call(side, body, *, name, grid, in_specs, out_specs, out_shape, scratch_shapes, args, aliases):
    n_in, n_out = len(in_specs), len(out_specs)
    if side is None:
        kernel_body = body
    else:
        n = side.n
        in_specs, args = in_specs + side.specs, tuple(args) + tuple(side.arrays)
        out_specs, out_shape = out_specs + side.specs, out_shape + side.out_shapes
        scratch_shapes = scratch_shapes + side.scratch

        def kernel_body(*refs):
            ins, side_in = refs[:n_in], refs[n_in:n_in + n]
            outs, side_out = refs[n_in + n:n_in + n + n_out], refs[n_in + n + n_out:n_in + 2 * n + n_out]
            scratch, sems = refs[n_in + 2 * n + n_out:-3], refs[-3:]
            side.run_around(grid, side_in, side_out, sems, before=True)
            body(*ins, *outs, *scratch)
            side.run_around(grid, side_in, side_out, sems, before=False)

    outs = pl.pallas_call(
        kernel_body, name=name, grid=grid, in_specs=in_specs, out_specs=out_specs, out_shape=out_shape,
        scratch_shapes=scratch_shapes, input_output_aliases=aliases, compiler_params=_params(len(grid)),
    )(*args)
    return outs if side is None else (outs[:n_out], outs[n_out:])


def all_gather(blocks, *, name):
    n = len(blocks)
    any_spec = pl.BlockSpec(memory_space=pl.ANY)

    def body(*refs):
        ins, outs = refs[:n], refs[n:2 * n]
        send_sems, recv_sems, local_sems = refs[2 * n:]
        x, y, c = _mesh_pos()
        me, sibling = (x, y, c), (x, y, 1 - c)
        chips = [(1 - x, y), (x, 1 - y), (1 - x, 1 - y)]

        def copy(a, k, block, to, src=None):
            dst = outs[a].at[_dev_index(block)]
            return pltpu.make_async_remote_copy(
                src_ref=dst if src is None else src, dst_ref=dst, send_sem=send_sems.at[a, k],
                recv_sem=recv_sems.at[a, k], device_id=to, device_id_type=pl.DeviceIdType.MESH)

        mine = [pltpu.make_async_copy(ins[a], outs[a].at[_dev_index(me)], local_sems.at[a]) for a in range(n)]
        for cp in mine:
            cp.start()
        first = []
        for a in range(n):
            first.append(copy(a, 0, me, sibling, src=ins[a]))
            first += [copy(a, 1 + j, me, (*chip, c), src=ins[a]) for j, chip in enumerate(chips)]
        for cp in first:
            cp.start()
        passed = []
        for j, chip in enumerate(chips):
            for a in range(n):
                copy(a, 1 + j, (*chip, c), me).wait_recv()
                fwd = copy(a, 4 + j, (*chip, c), sibling)
                fwd.start()
                passed.append(fwd)
        for a in range(n):
            copy(a, 0, sibling, me).wait_recv()
            for j, chip in enumerate(chips):
                copy(a, 4 + j, (*chip, 1 - c), me).wait_recv()
        for cp in first + passed:
            cp.wait_send()
        for cp in mine:
            cp.wait()

    return pl.pallas_call(
        body, name=name, in_specs=[any_spec] * n, out_specs=[any_spec] * n,
        out_shape=[jax.ShapeDtypeStruct((N_DEV,) + b.shape, b.dtype) for b in blocks],
        scratch_shapes=[pltpu.SemaphoreType.DMA((n, 7)), pltpu.SemaphoreType.DMA((n, 7)),
                        pltpu.SemaphoreType.DMA((n,))],
    )(*blocks)


def exchange(jobs, *, name):
    total = sum(j.n for j in jobs)

    def body(*refs):
        ins, outs, sems = refs[:total], refs[total:2 * total], refs[2 * total:]
        pieces, off = [], 0
        for i, j in enumerate(jobs):
            pieces.append((ins[off:off + j.n], outs[off:off + j.n], sems[3 * i:3 * i + 3]))
            off += j.n
        for j, piece in zip(jobs, pieces):
            j.start(*piece)
        for j, piece in zip(jobs, pieces):
            j.wait(*piece)

    outs = pl.pallas_call(
        body, name=name, in_specs=[s for j in jobs for s in j.specs], out_specs=[s for j in jobs for s in j.specs],
        out_shape=[s for j in jobs for s in j.out_shapes], scratch_shapes=[s for j in jobs for s in j.scratch],
    )(*[a for j in jobs for a in j.arrays])
    split, off = [], 0
    for j in jobs:
        split.append(outs[off:off + j.n])
        off += j.n
    return split


def _rows128(arr):
    flat = arr.reshape(-1)
    rows = -(-flat.shape[0] // (8 * LANES)) * 8
    return jnp.pad(flat, (0, rows * LANES - flat.shape[0])).reshape(rows, LANES)


def _pad_lanes(row, width=LANES, at=0):
    return jnp.pad(row, (at, width - at - row.shape[0])).reshape(1, width)


def _w_in_sections(got):
    d = got.shape[1]
    wi = jnp.transpose(got, (1, 0, 2)).reshape(d, -1)
    w_main = jnp.concatenate([wi[:, :C_B], wi[:, C_Z:C_BA], wi[:, C_QKV:C_Z], wi[:, C_G:], wi[:, C_B:C_QKV]], axis=1)
    return w_main, jnp.pad(wi[:, C_BA:C_G], ((0, 0), (0, LANES - (C_G - C_BA))))


def _w_in_parts(gw_main, gw_ba):
    d = gw_main.shape[0]
    full = jnp.concatenate([gw_main[:, SEC_A:SEC_Z], gw_main[:, SEC_B:], gw_main[:, SEC_QKV:SEC_G],
                            gw_main[:, SEC_Z:SEC_QKV], gw_ba[:, :C_G - C_BA], gw_main[:, SEC_G:SEC_B]], axis=1)
    return jnp.transpose(full.reshape(d, N_DEV, -1), (1, 0, 2))


def kernel(x, positions, attn_norm, w_in, sgu_ln_g, sgu_ln_b, sgu_w, sgu_b, attn_sinks, dn_conv_w, dn_a_log, dn_dt_bias, dn_norm, w_branch, w_out, ffn_norm, w_gate_up, w_down, final_norm, loss_target, m_attn_norm, m_w_in, m_sgu_ln_g, m_sgu_ln_b, m_sgu_w, m_sgu_b, m_attn_sinks, m_dn_conv_w, m_dn_a_log, m_dn_dt_bias, m_dn_norm, m_w_branch, m_w_out, m_ffn_norm, m_w_gate_up, m_w_down, m_final_norm, v_attn_norm, v_w_in, v_sgu_ln_g, v_sgu_ln_b, v_sgu_w, v_sgu_b, v_attn_sinks, v_dn_conv_w, v_dn_a_log, v_dn_dt_bias, v_dn_norm, v_w_branch, v_w_out, v_ffn_norm, v_w_gate_up, v_w_down, v_final_norm):
    given = dict(locals())
    depth, d_model = attn_norm.shape
    s = x.shape[1]
    x2 = x.reshape(s, d_model)
    target = loss_target.reshape(s, d_model)
    posf = positions.reshape(s, 1).astype(f32)
    inv_freq = ROPE_THETA ** (-jnp.arange(0, ROPE_DIM, 2, dtype=f32) / ROPE_DIM)
    inv_head = jnp.concatenate([inv_freq, inv_freq, jnp.zeros((SWA_HD - ROPE_DIM,), f32)])
    tables = rope_tables(posf, jnp.tile(inv_head, LANES // SWA_HD).reshape(1, LANES), name="rope_tables")

    assert depth == 2, depth
    gathered = dict(zip([("w_in", 0), ("dn_conv_w", 0), ("dn_conv_w", 1)], all_gather(
        [w_in[0].astype(bf16), dn_conv_w[0], dn_conv_w[1]], name="gather_first")))
    riders = {"l0_in_proj": [("w_branch", 0), ("w_out", 0), ("w_down", 0)],
              "l0_deltanet": [("w_gate_up", 0), ("w_in", 1)],
              "l1_deltanet": [("w_branch", 1), ("w_out", 1), ("w_gate_up", 1), ("w_down", 1)]}

    def gathering(host, call, *args, **kw):
        if host not in riders:
            return call(*args, name=host, **kw)
        side = Exchange("gather", [given[n][l].astype(bf16) for n, l in riders[host]])
        out, got = call(*args, side=side, name=host, **kw)
        gathered.update(zip(riders[host], got))
        return out

    layers, saved = [], []
    h_in = x2
    for l in range(depth):
        t = f"l{l}_"
        w_main, w_ba = _w_in_sections(gathered["w_in", l])
        conv_full = jnp.transpose(gathered["dn_conv_w", l], (1, 0, 2)).reshape(DN_CONV, -1)
        p = dict(
            w_main=w_main, w_ba=w_ba, conv_w8=jnp.pad(conv_full, ((0, CONV_PAD - DN_CONV), (0, 0))),
            attn_norm=attn_norm[l].reshape(1, -1), ffn_norm=ffn_norm[l].reshape(1, -1),
            ln_g=sgu_ln_g[l].reshape(1, -1), ln_b=sgu_ln_b[l].reshape(1, -1), sgu_w=sgu_w[l], sgu_bt=sgu_b[l].T,
            sink_row=_pad_lanes(attn_sinks[l]), alog_row=_pad_lanes(dn_a_log[l], at=DN_HEADS),
            dtb_row=_pad_lanes(dn_dt_bias[l], at=DN_HEADS), norm_row=dn_norm[l].reshape(1, -1))
        layers.append(p)
        h = rmsnorm_fwd(h_in, p["attn_norm"], name=t + "attn_norm")
        proj = gathering(t + "in_proj", matmul, h, p["w_main"], "nn", bf16)
        p_ba = matmul(h, p["w_ba"], "nn", f32, name=t + "in_proj_ba")
        out_a = sgu_fwd(proj, p["ln_g"], p["ln_b"], p["sgu_w"], p["sgu_bt"], name=t + "sgu")
        out_b = swa_fwd(proj, p["sink_row"], tables, name=t + "swa")
        xc = conv_fwd(proj, p["conv_w8"], name=t + "dn_conv")
        out_c, states, tri = gathering(t + "deltanet", dn_fwd, xc, proj, p_ba, p["alog_row"], p["dtb_row"], p["norm_row"])
        p.update(w_branch=jnp.transpose(gathered["w_branch", l], (1, 2, 0, 3)).reshape(3, MIX, d_model),
                 w_out=gathered["w_out", l].reshape(d_model, d_model),
                 w_gu=gathered["w_gate_up", l],
                 w_down=gathered["w_down", l].reshape(N_DEV // 2, -1, d_model))
        merged = merge_fwd(out_a, out_b, out_c, proj, p["w_branch"], name=t + "merge")
        x_mid = matmul(merged, p["w_out"], "nn", f32, residual=h_in, tn=1024, name=t + "out_proj")
        h2 = rmsnorm_fwd(x_mid, p["ffn_norm"], name=t + "ffn_norm")
        gate, up, act = gate_up_swiglu(h2, p["w_gu"], name=t + "gate_up")
        x_out = matmul(act, p["w_down"], "nn", f32, residual=x_mid, group="k", tn=1024, name=t + "down")
        saved.append(dict(x_in=h_in, h=h, proj=proj, p_ba=p_ba, out_a=out_a, out_b=out_b, out_c=out_c, xc=xc,
                          states=states, tri=tri, merged=merged, x_mid=x_mid, h2=h2, gate=gate, up=up, act=act))
        h_in = x_out

    dx, d_final_norm, loss_row = loss_head(h_in, final_norm.reshape(1, -1), target, name="loss_head")
    loss = lax.psum(loss_row[0, 0], MESH_AXES)

    shard_names = ["w_in", "dn_conv_w", "w_branch", "w_out", "w_gate_up", "w_down"]
    rep_names = ["attn_norm", "sgu_ln_g", "sgu_ln_b", "sgu_w", "sgu_b", "attn_sinks", "dn_a_log", "dn_dt_bias",
                 "dn_norm", "ffn_norm"]
    parts, received, per_layer = {}, {}, []
    senders = {"l0_b_swa": [("w_gate_up", 1), ("w_down", 1), ("w_out", 1), ("w_branch", 1)],
               "l0_b_deltanet": [("w_in", 1), ("w_gate_up", 0), ("w_down", 0)],
               "l0_b_dn_conv": [("w_out", 0), ("w_branch", 0)],
               "l0_b_in_proj_dx": [("w_in", 0)]}

    def scattering(host, call, *args, **kw):
        if host not in senders:
            return call(*args, name=host, **kw)
        out, got = call(*args, side=Exchange("scatter", [parts[key] for key in senders[host]]), name=host, **kw)
        received.update(zip(senders[host], got))
        return out

    for l in reversed(range(depth)):
        p, sv, t = layers[l], saved[l], f"l{l}_b_"
        d_gate, d_up = down_dx_swiglu(dx, p["w_down"], sv["gate"], sv["up"], name=t + "down_dx")
        gw_down = matmul(sv["act"], dx, "tn", bf16, group="m", tk=2048, tn=512, name=t + "down_dw")
        gw_gu = jnp.concatenate([matmul(sv["h2"], d_half, "tn", bf16, group="n", tk=2048, name=t + "gate_up_dw" + tag)
                                 for d_half, tag in ((d_gate, "_gate"), (d_up, "_up"))], axis=0)
        half = N_DEV // 2
        d_h2 = matmul(d_gate, p["w_gu"][:half], "nt", f32, group="k", tn=1024, name=t + "gate_up_dx_gate")
        d_h2 = matmul(d_up, p["w_gu"][half:], "nt", f32, group="k", tn=1024, residual=d_h2, name=t + "gate_up_dx_up")
        dx_mid, g_ffn = rmsnorm_bwd(sv["x_mid"], p["ffn_norm"], d_h2, dx, name=t + "ffn_norm")
        d_merged = matmul(dx_mid, p["w_out"], "nt", f32, tn=1024, name=t + "out_proj_dx")
        gw_out = matmul(sv["merged"], dx_mid, "tn", bf16, tk=2048, tn=1024, name=t + "out_proj_dw")
        dproj = lax.empty((s, W_MAIN), bf16)
        dproj, d_a, d_b, d_c, gw_branch = merge_bwd(sv["out_a"], sv["out_b"], sv["out_c"], sv["proj"], p["w_branch"],
                                                   d_merged, dproj, name=t + "merge")
        parts.update({("w_gate_up", l): gw_gu, ("w_down", l): gw_down.reshape(N_DEV, -1, d_model),
                      ("w_out", l): gw_out.reshape(N_DEV, -1, d_model),
                      ("w_branch", l): jnp.transpose(gw_branch.reshape(3, MIX, N_DEV, -1), (2, 0, 1, 3)).astype(bf16)})
        dproj, g_ln_g, g_ln_b, g_sgu_w, g_sgu_bt = sgu_bwd(sv["proj"], p["ln_g"], p["ln_b"], p["sgu_w"], p["sgu_bt"],
                                                         d_a, dproj, name=t + "sgu")
        dproj, g_sink = scattering(t + "swa", swa_bwd, sv["proj"], p["sink_row"], tables, d_b, dproj)
        dproj, dxc, dba, g_alog, g_dtb, g_dnorm = scattering(
            t + "deltanet", dn_bwd, sv["xc"], sv["proj"], sv["p_ba"], p["alog_row"], p["dtb_row"], p["norm_row"],
            sv["states"], sv["tri"], d_c, dproj)
        dproj, g_conv8 = scattering(t + "dn_conv", conv_bwd, sv["proj"], p["conv_w8"], dxc, dproj)
        gw_main = matmul(sv["h"], dproj, "tn", bf16, tk=2048, name=t + "in_proj_dw")
        gw_ba = matmul(sv["h"], dba, "tn", bf16, tk=2048, name=t + "in_proj_ba_dw")
        parts["w_in", l] = _w_in_parts(gw_main, gw_ba)
        d_h = scattering(t + "in_proj_dx", matmul, dproj, p["w_main"], "nt", f32, tk=2304, tn=1024)
        d_h = matmul(dba, p["w_ba"], "nt", f32, residual=d_h, name=t + "in_proj_ba_dx")
        dx, g_attn = rmsnorm_bwd(sv["x_in"], p["attn_norm"], d_h, dx_mid, name=t + "attn_norm")
        per_layer.append(dict(
            dn_conv_w=jnp.transpose(g_conv8[:DN_CONV].reshape(DN_CONV, N_DEV, -1), (1, 0, 2)),
            attn_norm=g_attn, sgu_ln_g=g_ln_g, sgu_ln_b=g_ln_b, sgu_w=g_sgu_w, sgu_b=g_sgu_bt.T,
            attn_sinks=g_sink[0, :SWA_HEADS], dn_a_log=g_alog[0, DN_HEADS:2 * DN_HEADS],
            dn_dt_bias=g_dtb[0, DN_HEADS:2 * DN_HEADS], dn_norm=g_dnorm, ffn_norm=g_ffn))
    per_layer.reverse()

    conv_parts = jnp.concatenate([pp["dn_conv_w"] for pp in per_layer], axis=1)
    rep_grads = {n: jnp.stack([pp[n].reshape(given[n].shape[1:]) for pp in per_layer]) for n in rep_names}
    rep_grads["final_norm"] = d_final_norm[0]
    rep_names = rep_names + ["final_norm"]
    rep_rows = [_rows128(given[n]).shape[0] for n in rep_names]
    pad_rows = -sum(rep_rows) % 16

    def small_rows(values):
        pieces = [_rows128(values[n]) for n in rep_names]
        return jnp.concatenate(pieces + [jnp.zeros((pad_rows, LANES), f32)], axis=0)

    (got_conv,), (small_all,) = exchange(
        [Exchange("scatter", [conv_parts]), Exchange("gather", [small_rows(rep_grads)])], name="exchange_last")

    results = [{}, {}, {}, {}]
    for n in shard_names:
        shp = given[n].shape
        two = (-1, shp[-1])
        by_layer = [got_conv] if n == "dn_conv_w" else [received[n, l].reshape(N_DEV, -1, shp[-1]) for l in range(depth)]
        outs = adamw(given[n].reshape(two), given["m_" + n].reshape(two), given["v_" + n].reshape(two), by_layer,
                     name="adamw_" + n)
        for res, val in zip(results, outs):
            res[n] = val.reshape(shp)
    outs = adamw(small_rows(given), small_rows({n: given["m_" + n] for n in rep_names}),
                 small_rows({n: given["v_" + n] for n in rep_names}), [small_all], name="adamw_replicated")
    for res, val in zip(results, outs):
        row = 0
        for n, nr in zip(rep_names, rep_rows):
            res[n] = val[row:row + nr].reshape(-1)[:given[n].size].reshape(given[n].shape)
            row += nr
    order = ["attn_norm", "w_in", "sgu_ln_g", "sgu_ln_b", "sgu_w", "sgu_b", "attn_sinks", "dn_conv_w", "dn_a_log",
             "dn_dt_bias", "dn_norm", "w_branch", "w_out", "ffn_norm", "w_gate_up", "w_down", "final_norm"]
    return (loss, dx.reshape(x.shape), *[res[n] for res in results for n in order])
```

```python
import functools

import jax
import jax.numpy as jnp
from jax import lax
from jax.experimental import pallas as pl
from jax.experimental.pallas import tpu as pltpu

f32 = jnp.float32
bf16 = jnp.bfloat16

N_DEV = 8
MESH_AXES = ("x", "y", "c")
NORM_EPS = 1e-6
MIX = 512
SGU_GROUPS, SGU_CHUNK = 4, 128
SWA_HEADS, SWA_KV, SWA_HD, WINDOW = 8, 2, 64, 128
ROPE_THETA, ROPE_DIM = 500000.0, 16
DN_HEADS, DN_HD, DN_CONV, DN_CHUNK = 4, 128, 4, 64
ADAM_LR, ADAM_B1, ADAM_B2, ADAM_EPS, ADAM_WD, ADAM_STEP = 0.001, 0.9, 0.999, 1e-08, 0.01, 10

LANES = 128
VMEM_LIMIT = 56 * 1024 * 1024

SEC_A, SEC_Z, SEC_QKV, SEC_G, SEC_B = 0, 1024, 1536, 3072, 6144
W_MAIN = 6912
C_B, C_QKV, C_Z, C_BA, C_G = 1024, 1792, 3328, 3840, 3848


def _params(n_axes, **kw):
    return pltpu.CompilerParams(dimension_semantics=("arbitrary",) * n_axes, vmem_limit_bytes=VMEM_LIMIT, **kw)


def _tile(n, target, mult=LANES):
    if n <= target:
        return n
    best = None
    for t in range(mult, target + 1, mult):
        if n % t == 0:
            best = t
    assert best is not None, (n, target, mult)
    return best


def _dg(a, b, ca, cb):
    return lax.dot_general(a.astype(bf16), b.astype(bf16), (((ca,), (cb,)), ((), ())), preferred_element_type=f32)


def _dg3(a, b, ca, cb):
    a_hi, b_hi = a.astype(bf16), b.astype(bf16)
    a_lo, b_lo = (a - a_hi.astype(f32)).astype(bf16), (b - b_hi.astype(f32)).astype(bf16)

    def dot(p, q):
        return lax.dot_general(p, q, (((ca,), (cb,)), ((), ())), preferred_element_type=f32)

    return dot(a_hi, b_hi) + (dot(a_hi, b_lo) + dot(a_lo, b_hi))


def _differentiable_dot(core):
    @functools.partial(jax.custom_vjp, nondiff_argnums=(2, 3))
    def dot(a, b, ca, cb):
        return core(a, b, ca, cb)

    def fwd(a, b, ca, cb):
        return core(a, b, ca, cb), (a, b)

    def bwd(ca, cb, res, ct):
        a, b = res
        da = core(ct, b, 1, 1 - cb) if ca == 1 else core(b, ct, 1 - cb, 1)
        db = core(a, ct, 1 - ca, 0) if cb == 0 else core(ct, a, 0, 1 - ca)
        return da, db

    dot.defvjp(fwd, bwd)
    return dot


bdot = _differentiable_dot(_dg)
_hdot = _differentiable_dot(_dg3)


def hdot(a, b, ca=1, cb=0):
    return _hdot(a, b, ca, cb)


@functools.partial(jax.custom_vjp, nondiff_argnums=(1,))
def lroll(x, shift):
    return pltpu.roll(x, shift, 1)


def _lroll_fwd(x, shift):
    return pltpu.roll(x, shift, 1), None


def _lroll_bwd(shift, _, ct):
    return (pltpu.roll(ct, ct.shape[1] - shift, 1),)


lroll.defvjp(_lroll_fwd, _lroll_bwd)


@functools.partial(jax.custom_vjp, nondiff_argnums=(1,))
def tri_inv(low, nil):
    n = low.shape[0]
    row = lax.broadcasted_iota(jnp.int32, (n, n), 0)
    col = lax.broadcasted_iota(jnp.int32, (n, n), 1)
    eye = (row == col).astype(f32)
    m = -low
    p = eye + m
    span = 2
    while span < nil:
        dot = hdot if span == 2 else (lambda a, b: bdot(a, b, 1, 0))
        m = dot(m, m)
        p = p + dot(p, m)
        span *= 2
    return p


def _tri_inv_fwd(low, nil):
    t = tri_inv(low, nil)
    return t, t


def _tri_inv_bwd(nil, t, dt):
    return (-bdot(t, bdot(dt, t, 1, 1), 0, 0),)


tri_inv.defvjp(_tri_inv_fwd, _tri_inv_bwd)


@jax.custom_vjp
def tri_inv_known(low, t):
    return t


def _tri_inv_known_fwd(low, t):
    return t, t


def _tri_inv_known_bwd(t, dt):
    return _tri_inv_bwd(None, t, dt) + (jnp.zeros_like(t),)


tri_inv_known.defvjp(_tri_inv_known_fwd, _tri_inv_known_bwd)


def _sigmoid(x):
    return 1.0 / (1.0 + jnp.exp(-x))


def _rms(x, g):
    return x * lax.rsqrt(jnp.mean(x * x, axis=-1, keepdims=True) + NORM_EPS) * g


def _lane_col(x, lane_idx):
    lane = lax.broadcasted_iota(jnp.int32, x.shape, 1)
    return jnp.sum(jnp.where(lane == lane_idx, x, 0.0), axis=1, keepdims=True)


def matmul(a, b, mode, out_dtype, *, residual=None, group=None, side=None, tm=1024, tn=768, tk=1024, name):
    dims = {"a": ("m", "k") if mode != "tn" else ("k", "m"),
            "b": {"nn": ("k", "n"), "nt": ("n", "k"), "tn": ("k", "n")}[mode], "o": ("m", "n")}
    full, groups = {}, 1
    for arr, key in ((a, "a"), (b, "b")):
        grouped = group in dims[key]
        if grouped:
            groups = arr.shape[0]
        full[dims[key][0]], full[dims[key][1]] = arr.shape[1:] if grouped else arr.shape
    want = {"m": tm, "n": tn, "k": tk}
    per_step = min(groups, 4) if group == "k" else 1
    tiles = {d: full[d] if d == group else _tile(full[d], want[d]) for d in "mnk"}
    steps = {d: groups // per_step if d == group else full[d] // tiles[d] for d in "mnk"}

    def spec(key):
        d0, d1 = dims[key]

        def index(i, j, kk):
            at = {"m": i, "n": j, "k": kk}
            if group in (d0, d1):
                return (at[group], 0 if d0 == group else at[d0], 0 if d1 == group else at[d1])
            return (at[d0], at[d1])

        block = (tiles[d0], tiles[d1])
        if group in (d0, d1):
            block = ((per_step if group == "k" else None),) + block
        return pl.BlockSpec(block, index)

    ca, cb = {"nn": (1, 0), "nt": (1, 1), "tn": (0, 0)}[mode]
    nk = steps["k"]
    o_spec = spec("o")
    out_shape = (groups, full["m"], full["n"]) if group in ("m", "n") else (full["m"], full["n"])
    has_res = residual is not None

    def product(a_ref, b_ref):
        if group != "k":
            return _dg(a_ref[...], b_ref[...], ca, cb)
        total = _dg(a_ref[0], b_ref[0], ca, cb)
        for g in range(1, per_step):
            total = total + _dg(a_ref[g], b_ref[g], ca, cb)
        return total

    def body(a_ref, b_ref, *rest):
        r_ref = rest[0] if has_res else None
        o_ref = rest[1 if has_res else 0]

        def emit(acc):
            o_ref[...] = (acc + r_ref[...] if has_res else acc).astype(out_dtype)

        if nk == 1:
            emit(product(a_ref, b_ref))
            return
        acc_ref = rest[-1]
        kk = pl.program_id(2)

        @pl.when(kk == 0)
        def _():
            acc_ref[...] = jnp.zeros_like(acc_ref)

        acc_ref[...] += product(a_ref, b_ref)

        @pl.when(kk == nk - 1)
        def _():
            emit(acc_ref[...])

    res = host_call(
        side, body, name=name, grid=(steps["m"], steps["n"], nk),
        in_specs=[spec("a"), spec("b")] + ([o_spec] if has_res else []), out_specs=[o_spec],
        out_shape=[jax.ShapeDtypeStruct(out_shape, out_dtype)],
        scratch_shapes=[pltpu.VMEM((tiles["m"], tiles["n"]), f32)] if nk > 1 else [], aliases={},
        args=(a, b) + ((residual,) if has_res else ()))
    return res[0] if side is None else (res[0][0], res[1])


def rmsnorm_fwd(x, g_row, *, name):
    s, d = x.shape
    ts = _tile(s, 512, 16)

    def body(x_ref, g_ref, o_ref):
        o_ref[...] = _rms(x_ref[...], g_ref[...]).astype(bf16)

    return pl.pallas_call(
        body, name=name, grid=(s // ts,),
        in_specs=[pl.BlockSpec((ts, d), lambda i: (i, 0)), pl.BlockSpec((1, d), lambda i: (0, 0))],
        out_specs=pl.BlockSpec((ts, d), lambda i: (i, 0)), out_shape=jax.ShapeDtypeStruct((s, d), bf16),
        compiler_params=_params(1),
    )(x, g_row)


def rmsnorm_bwd(x, g_row, dh, dres, *, name):
    s, d = x.shape
    ts = _tile(s, 512, 16)

    def body(x_ref, g_ref, dh_ref, dres_ref, dx_ref, dg_ref):
        @pl.when(pl.program_id(0) == 0)
        def _():
            dg_ref[...] = jnp.zeros_like(dg_ref)

        _, vjp = jax.vjp(_rms, x_ref[...], g_ref[...])
        dx, dg = vjp(dh_ref[...])
        dx_ref[...] = dx + dres_ref[...]
        dg_ref[...] += dg

    row = pl.BlockSpec((ts, d), lambda i: (i, 0))
    vec = pl.BlockSpec((1, d), lambda i: (0, 0))
    return pl.pallas_call(
        body, name=name, grid=(s // ts,), in_specs=[row, vec, row, row], out_specs=[row, vec],
        out_shape=[jax.ShapeDtypeStruct((s, d), f32), jax.ShapeDtypeStruct((1, d), f32)],
        compiler_params=_params(1),
    )(x, g_row, dh, dres)


def loss_head(x, g_row, target, *, name):
    s, d = x.shape
    ts = _tile(s, 512, 16)

    def body(x_ref, g_ref, t_ref, dx_ref, dg_ref, loss_ref):
        @pl.when(pl.program_id(0) == 0)
        def _():
            dg_ref[...] = jnp.zeros_like(dg_ref)
            loss_ref[...] = jnp.zeros_like(loss_ref)

        y, vjp = jax.vjp(_rms, x_ref[...], g_ref[...])
        err = y - t_ref[...]
        dx, dg = vjp(err * (1.0 / d))
        dx_ref[...] = dx
        dg_ref[...] += dg
        loss_ref[...] += 0.5 * jnp.sum(jnp.sum(err * err, axis=1, keepdims=True) * (1.0 / d), axis=0, keepdims=True)

    row = pl.BlockSpec((ts, d), lambda i: (i, 0))
    vec = pl.BlockSpec((1, d), lambda i: (0, 0))
    one = pl.BlockSpec((1, LANES), lambda i: (0, 0))
    return pl.pallas_call(
        body, name=name, grid=(s // ts,), in_specs=[row, vec, row], out_specs=[row, vec, one],
        out_shape=[jax.ShapeDtypeStruct((s, d), f32), jax.ShapeDtypeStruct((1, d), f32),
                   jax.ShapeDtypeStruct((1, LANES), f32)],
        compiler_params=_params(1),
    )(x, g_row, target)


def _sgu_chunk(p_a, ln_g, ln_b, w, b_t):
    t = SGU_CHUNK
    u = jax.nn.gelu(p_a[:, :MIX])
    v = jax.nn.gelu(p_a[:, MIX:])
    vc = v - jnp.mean(v, axis=-1, keepdims=True)
    vn = vc * lax.rsqrt(jnp.mean(vc * vc, axis=-1, keepdims=True) + NORM_EPS) * ln_g + ln_b
    causal = lax.broadcasted_iota(jnp.int32, (t, t), 0) >= lax.broadcasted_iota(jnp.int32, (t, t), 1)
    outs = []
    for g in range(SGU_GROUPS):
        sl = slice(g * LANES, (g + 1) * LANES)
        mixed = bdot(jnp.where(causal, w[g], 0.0), vn[:, sl], 1, 0) + b_t[:, g:g + 1]
        outs.append(u[:, sl] * mixed)
    return jnp.concatenate(outs, axis=1)


def _sgu_specs(s, ts):
    return [pl.BlockSpec((ts, 2 * MIX), lambda i: (i, SEC_A // (2 * MIX))),
            pl.BlockSpec((1, MIX), lambda i: (0, 0)), pl.BlockSpec((1, MIX), lambda i: (0, 0)),
            pl.BlockSpec((SGU_GROUPS, SGU_CHUNK, SGU_CHUNK), lambda i: (0, 0, 0)),
            pl.BlockSpec((SGU_CHUNK, SGU_GROUPS), lambda i: (0, 0))]


def sgu_fwd(proj, ln_g, ln_b, w, b_t, *, name):
    s = proj.shape[0]
    ts = _tile(s, 512)
    n_chunk = ts // SGU_CHUNK

    def body(p_ref, g_ref, b_ref, w_ref, bt_ref, o_ref):
        def step(c, carry):
            rows = pl.ds(pl.multiple_of(c * SGU_CHUNK, SGU_CHUNK), SGU_CHUNK)
            o_ref[rows, :] = _sgu_chunk(p_ref[rows, :].astype(f32), g_ref[...], b_ref[...], w_ref[...], bt_ref[...]).astype(bf16)
            return carry
        lax.fori_loop(0, n_chunk, step, 0)

    return pl.pallas_call(
        body, name=name, grid=(s // ts,), in_specs=_sgu_specs(s, ts),
        out_specs=pl.BlockSpec((ts, MIX), lambda i: (i, 0)), out_shape=jax.ShapeDtypeStruct((s, MIX), bf16),
        compiler_params=_params(1),
    )(proj, ln_g, ln_b, w, b_t)


def sgu_bwd(proj, ln_g, ln_b, w, b_t, d_out, dproj, *, name):
    s = proj.shape[0]
    ts = _tile(s, 512)
    n_chunk = ts // SGU_CHUNK

    def body(p_ref, g_ref, b_ref, w_ref, bt_ref, do_ref, _, dp_ref, dg_ref, db_ref, dw_ref, dbt_ref):
        @pl.when(pl.program_id(0) == 0)
        def _():
            dg_ref[...] = jnp.zeros_like(dg_ref)
            db_ref[...] = jnp.zeros_like(db_ref)
            dw_ref[...] = jnp.zeros_like(dw_ref)
            dbt_ref[...] = jnp.zeros_like(dbt_ref)

        def step(c, carry):
            rows = pl.ds(pl.multiple_of(c * SGU_CHUNK, SGU_CHUNK), SGU_CHUNK)
            _, vjp = jax.vjp(_sgu_chunk, p_ref[rows, :].astype(f32), g_ref[...], b_ref[...], w_ref[...], bt_ref[...])
            dp, dg, db, dw, dbt = vjp(do_ref[rows, :])
            dp_ref[rows, :] = dp.astype(bf16)
            dg_ref[...] += dg
            db_ref[...] += db
            dw_ref[...] += dw
            dbt_ref[...] += dbt
            return carry
        lax.fori_loop(0, n_chunk, step, 0)

    specs = _sgu_specs(s, ts)
    return pl.pallas_call(
        body, name=name, grid=(s // ts,),
        in_specs=specs + [pl.BlockSpec((ts, MIX), lambda i: (i, 0)), pl.BlockSpec(memory_space=pl.ANY)],
        out_specs=[specs[0], specs[1], specs[2], specs[3], specs[4]],
        out_shape=[jax.ShapeDtypeStruct(dproj.shape, bf16), jax.ShapeDtypeStruct((1, MIX), f32),
                   jax.ShapeDtypeStruct((1, MIX), f32), jax.ShapeDtypeStruct(w.shape, f32),
                   jax.ShapeDtypeStruct(b_t.shape, f32)],
        input_output_aliases={6: 0}, compiler_params=_params(1),
    )(proj, ln_g, ln_b, w, b_t, d_out, dproj)


def rope_tables(posf, inv_freq, *, name):
    s = posf.shape[0]
    ts = _tile(s, 1024, 8)
    half = ROPE_DIM // 2

    def body(pos_ref, inv_ref, o_ref):
        d = lax.broadcasted_iota(jnp.int32, (1, LANES), 1) % SWA_HD
        ang = pos_ref[...] * inv_ref[...]
        sin = jnp.sin(ang)
        o_ref[0] = jnp.cos(ang)
        o_ref[1] = jnp.where(d < half, sin, 0.0)
        o_ref[2] = jnp.where((d >= half) & (d < ROPE_DIM), sin, 0.0)

    return pl.pallas_call(
        body, name=name, grid=(s // ts,),
        in_specs=[pl.BlockSpec((ts, 1), lambda i: (i, 0)), pl.BlockSpec((1, LANES), lambda i: (0, 0))],
        out_specs=pl.BlockSpec((3, ts, LANES), lambda i: (0, i, 0)), out_shape=jax.ShapeDtypeStruct((3, s, LANES), f32),
        compiler_params=_params(1),
    )(posf, inv_freq)


def _rope(x, table):
    w = x.shape[1]
    half = ROPE_DIM // 2
    c, lo, hi = (jnp.concatenate([table[i]] * (w // LANES), axis=1) for i in range(3))
    return x * c - lroll(x, w - half) * lo + lroll(x, half) * hi


def _swa_block(q, kp, kc, vp, vc, sink_row, table_q, table_p, prev_ok, wide):
    t = WINDOW
    q = _rope(q, table_q) * (SWA_HD ** -0.5)
    keys = jnp.concatenate([_rope(kp, table_p), _rope(kc, table_q)], axis=0)
    vals = jnp.concatenate([vp, vc], axis=0)
    own = lax.broadcasted_iota(jnp.int32, (t, t), 0) >= lax.broadcasted_iota(jnp.int32, (t, t), 1)
    lane_half = lax.broadcasted_iota(jnp.int32, (t, LANES), 1) // SWA_HD
    group = SWA_HEADS // SWA_KV
    slabs = []
    for pair in range(SWA_HEADS // 2):
        q_pair = q[:, pair * LANES:(pair + 1) * LANES]
        acc = jnp.zeros((t, LANES), f32)
        for half in range(2):
            h = 2 * pair + half
            kv = h // group
            qm = jnp.where(lane_half == half, q_pair, 0.0)
            if half != kv:
                qm = lroll(qm, SWA_HD)
            if wide:
                both = bdot(qm, keys, 1, 1)
                s_prev, s_own = both[:, :t], both[:, t:]
            else:
                s_prev, s_own = bdot(qm, keys[:t], 1, 1), bdot(qm, keys[t:], 1, 1)
            logits = jnp.where(own, s_own, jnp.where(prev_ok, s_prev, -1e30))
            sink = _lane_col(sink_row, h)
            m = lax.stop_gradient(jnp.maximum(jnp.max(logits, axis=1, keepdims=True), sink))
            p = jnp.exp(logits - m)
            probs = p * (1.0 / (jnp.sum(p, axis=1, keepdims=True) + jnp.exp(sink - m)))
            p_prev, p_own = jnp.where(own, 0.0, probs), jnp.where(own, probs, 0.0)
            if wide:
                o = bdot(jnp.concatenate([p_prev, p_own], axis=1), vals, 1, 0)
            else:
                o = bdot(p_prev, vals[:t], 1, 0) + bdot(p_own, vals[t:], 1, 0)
            o = jnp.where(lane_half == kv, o, 0.0)
            if half != kv:
                o = lroll(o, SWA_HD)
            acc = acc + o
        slabs.append(acc)
    return jnp.concatenate(slabs, axis=1)


def _swa_in_specs(nc, clamp):
    t = WINDOW
    qb, kb, vb = SEC_B // MIX, (SEC_B + MIX) // LANES, (SEC_B + MIX + LANES) // LANES

    def cur(i):
        return jnp.minimum(i, nc - 1) if clamp else i

    def prev(i):
        return jnp.maximum(cur(i) - 1, 0)

    return [pl.BlockSpec((t, MIX), lambda i: (cur(i), qb)),
            pl.BlockSpec((t, LANES), lambda i: (prev(i), kb)), pl.BlockSpec((t, LANES), lambda i: (cur(i), kb)),
            pl.BlockSpec((t, LANES), lambda i: (prev(i), vb)), pl.BlockSpec((t, LANES), lambda i: (cur(i), vb)),
            pl.BlockSpec((1, LANES), lambda i: (0, 0)),
            pl.BlockSpec((3, t, LANES), lambda i: (0, cur(i), 0)), pl.BlockSpec((3, t, LANES), lambda i: (0, prev(i), 0))]


def swa_fwd(proj, sink_row, tables, *, name):
    s = proj.shape[0]
    nc = s // WINDOW

    def body(q_ref, kp_ref, kc_ref, vp_ref, vc_ref, sink_ref, tq_ref, tp_ref, o_ref):
        prev_ok = pl.program_id(0) > 0
        blocks = [r[...].astype(f32) for r in (q_ref, kp_ref, kc_ref, vp_ref, vc_ref)]
        o_ref[...] = _swa_block(*blocks, sink_ref[...],
                                tq_ref[...], tp_ref[...], prev_ok, True).astype(bf16)

    return pl.pallas_call(
        body, name=name, grid=(nc,), in_specs=_swa_in_specs(nc, False),
        out_specs=pl.BlockSpec((WINDOW, MIX), lambda i: (i, 0)), out_shape=jax.ShapeDtypeStruct((s, MIX), bf16),
        compiler_params=_params(1),
    )(proj, proj, proj, proj, proj, sink_row, tables, tables)


def swa_bwd(proj, sink_row, tables, d_out, dproj, *, side=None, name):
    s = proj.shape[0]
    nc = s // WINDOW
    t = WINDOW

    def body(q_ref, kp_ref, kc_ref, vp_ref, vc_ref, sink_ref, tq_ref, tp_ref, do_ref, _,
             dp_ref, dsink_ref, cq_ref, ck_ref, cv_ref):
        i = pl.program_id(0)

        @pl.when(i == 0)
        def _():
            dsink_ref[...] = jnp.zeros_like(dsink_ref)

        def write(dk_prev, dv_prev):
            dp_ref[:, :MIX] = cq_ref[...].astype(bf16)
            dp_ref[:, MIX:MIX + LANES] = (ck_ref[...] + dk_prev).astype(bf16)
            dp_ref[:, MIX + LANES:] = (cv_ref[...] + dv_prev).astype(bf16)

        @pl.when(i < nc)
        def _():
            fn = functools.partial(_swa_block, table_q=tq_ref[...], table_p=tp_ref[...], prev_ok=i > 0, wide=False)
            blocks = [r[...].astype(f32) for r in (q_ref, kp_ref, kc_ref, vp_ref, vc_ref)]
            _, vjp = jax.vjp(fn, *blocks, sink_ref[...])
            dq, dkp, dkc, dvp, dvc, dsink = vjp(do_ref[...])
            dsink_ref[...] += dsink

            @pl.when(i > 0)
            def _():
                write(dkp, dvp)

            cq_ref[...] = dq
            ck_ref[...] = dkc
            cv_ref[...] = dvc

        @pl.when(i == nc)
        def _():
            write(0.0, 0.0)

    return host_call(
        side, body, name=name, grid=(nc + 1,),
        in_specs=_swa_in_specs(nc, True) + [pl.BlockSpec((t, MIX), lambda i: (jnp.minimum(i, nc - 1), 0)),
                                            pl.BlockSpec(memory_space=pl.ANY)],
        out_specs=[pl.BlockSpec((t, MIX + 2 * LANES), lambda i: (jnp.maximum(i - 1, 0), SEC_B // (MIX + 2 * LANES))),
                   pl.BlockSpec((1, LANES), lambda i: (0, 0))],
        out_shape=[jax.ShapeDtypeStruct(dproj.shape, bf16), jax.ShapeDtypeStruct((1, LANES), f32)],
        scratch_shapes=[pltpu.VMEM((t, MIX), f32), pltpu.VMEM((t, LANES), f32), pltpu.VMEM((t, LANES), f32)],
        aliases={9: 0}, args=(proj, proj, proj, proj, proj, sink_row, tables, tables, d_out, dproj))


CONV_PAD = 16


def _conv_taps(xp, rows):
    off = CONV_PAD - (DN_CONV - 1)
    return [xp[off + i:off + i + rows] for i in range(DN_CONV)]


def _conv_pre(taps, w):
    pre = taps[0] * w[0:1]
    for i in range(1, DN_CONV):
        pre = pre + taps[i] * w[i:i + 1]
    return pre


def conv_fwd(proj, conv_w8, *, name):
    s = proj.shape[0]
    wq = 3 * MIX
    ts = _tile(s, 512)
    nb = ts // CONV_PAD

    def body(x_ref, prev_ref, w_ref, o_ref):
        prev = jnp.where(pl.program_id(0) > 0, prev_ref[...].astype(f32), 0.0)
        pre = _conv_pre(_conv_taps(jnp.concatenate([prev, x_ref[...].astype(f32)], axis=0), ts), w_ref[...])
        o_ref[...] = pre * _sigmoid(pre)

    return pl.pallas_call(
        body, name=name, grid=(s // ts,),
        in_specs=[pl.BlockSpec((ts, wq), lambda i: (i, SEC_QKV // wq)),
                  pl.BlockSpec((CONV_PAD, wq), lambda i: (jnp.maximum(i * nb - 1, 0), SEC_QKV // wq)),
                  pl.BlockSpec((CONV_PAD, wq), lambda i: (0, 0))],
        out_specs=pl.BlockSpec((ts, wq), lambda i: (i, 0)), out_shape=jax.ShapeDtypeStruct((s, wq), f32),
        compiler_params=_params(1),
    )(proj, proj, conv_w8)


def conv_bwd(proj, conv_w8, dxc, dproj, *, side=None, name):
    s = proj.shape[0]
    wq = 3 * MIX
    ts = _tile(s, 512)
    nb = ts // CONV_PAD
    nt = s // ts
    last_blk = s // CONV_PAD - 1

    def body(x_ref, prev_ref, next_ref, w_ref, d_ref, dnext_ref, _, dp_ref, dw_ref):
        i = pl.program_id(0)

        @pl.when(i == 0)
        def _():
            dw_ref[...] = jnp.zeros_like(dw_ref)

        w = w_ref[...]
        prev = jnp.where(i > 0, prev_ref[...].astype(f32), 0.0)
        more = i < nt - 1
        xp = jnp.concatenate([prev, x_ref[...].astype(f32), jnp.where(more, next_ref[...].astype(f32), 0.0)], axis=0)
        taps = _conv_taps(xp, ts + CONV_PAD)
        pre = _conv_pre(taps, w)
        sig = _sigmoid(pre)
        dxc_ext = jnp.concatenate([d_ref[...], jnp.where(more, dnext_ref[...], 0.0)], axis=0)
        dpre = dxc_ext * sig * (1.0 + pre * (1.0 - sig))
        d_raw = jnp.zeros((ts, wq), f32)
        dws = []
        for k in range(DN_CONV):
            shift = DN_CONV - 1 - k
            d_raw = d_raw + dpre[shift:shift + ts] * w[k:k + 1]
            dws.append(jnp.sum(dpre[:ts] * taps[k][:ts], axis=0, keepdims=True))
        dp_ref[...] = d_raw.astype(bf16)
        dw_ref[...] += jnp.concatenate(dws + [jnp.zeros((CONV_PAD - DN_CONV, wq), f32)], axis=0)

    sec = SEC_QKV // wq
    return host_call(
        side, body, name=name, grid=(nt,),
        in_specs=[pl.BlockSpec((ts, wq), lambda i: (i, sec)),
                  pl.BlockSpec((CONV_PAD, wq), lambda i: (jnp.maximum(i * nb - 1, 0), sec)),
                  pl.BlockSpec((CONV_PAD, wq), lambda i: (jnp.minimum((i + 1) * nb, last_blk), sec)),
                  pl.BlockSpec((CONV_PAD, wq), lambda i: (0, 0)),
                  pl.BlockSpec((ts, wq), lambda i: (i, 0)),
                  pl.BlockSpec((CONV_PAD, wq), lambda i: (jnp.minimum((i + 1) * nb, last_blk), 0)),
                  pl.BlockSpec(memory_space=pl.ANY)],
        out_specs=[pl.BlockSpec((ts, wq), lambda i: (i, sec)), pl.BlockSpec((CONV_PAD, wq), lambda i: (0, 0))],
        out_shape=[jax.ShapeDtypeStruct(dproj.shape, bf16), jax.ShapeDtypeStruct((CONV_PAD, wq), f32)],
        scratch_shapes=[], aliases={6: 0}, args=(proj, proj, proj, conv_w8, dxc, dxc, dproj))


def _dn_chunk(state, xc, z, ba, alog_row, dtb_row, norm_row, t_known):
    c, nh = DN_CHUNK, DN_HEADS
    n = c * nh
    row = lax.broadcasted_iota(jnp.int32, (n, n), 0)
    col = lax.broadcasted_iota(jnp.int32, (n, n), 1)
    same_head = (row // c) == (col // c)
    tril, strict = same_head & (row >= col), same_head & (row > col)
    tril_c = lax.broadcasted_iota(jnp.int32, (c, c), 0) >= lax.broadcasted_iota(jnp.int32, (c, c), 1)
    beta_all = _sigmoid(ba)
    g_all = -jnp.exp(alog_row) * jax.nn.softplus(ba + dtb_row)
    gc_all = hdot(tril_c.astype(f32), g_all)
    gc_t = gc_all.T

    def stack(piece):
        return jnp.concatenate([piece(h) for h in range(nh)], axis=0)

    q = stack(lambda h: xc[:, h * DN_HD:(h + 1) * DN_HD])
    k = stack(lambda h: xc[:, MIX + h * DN_HD:MIX + (h + 1) * DN_HD])
    v = stack(lambda h: xc[:, 2 * MIX + h * DN_HD:2 * MIX + (h + 1) * DN_HD])
    zs = stack(lambda h: z[:, h * DN_HD:(h + 1) * DN_HD])
    q = q * lax.rsqrt(jnp.sum(q * q, axis=-1, keepdims=True) + NORM_EPS) * (DN_HD ** -0.5)
    k = k * lax.rsqrt(jnp.sum(k * k, axis=-1, keepdims=True) + NORM_EPS)
    beta = stack(lambda h: _lane_col(beta_all, h))
    g_cols = [_lane_col(gc_all, nh + h) for h in range(nh)]
    g_col = jnp.concatenate(g_cols, axis=0)
    g_row = jnp.concatenate([gc_t[nh + h:nh + h + 1, :] for h in range(nh)], axis=1)
    g_last = stack(lambda h: jnp.broadcast_to(g_cols[h][c - 1:c, :], (c, 1)))
    decay = jnp.where(tril, jnp.exp(jnp.where(tril, g_col - g_row, 0.0)), 0.0)
    kb = k * beta
    low = jnp.where(strict, bdot(kb, k, 1, 1) * decay, 0.0)
    t_inv = tri_inv(low, c) if t_known is None else tri_inv_known(low, t_known)
    e_gc = jnp.exp(g_col)
    uw = bdot(t_inv, jnp.concatenate([v * beta, kb * e_gc], axis=1), 1, 0)
    u, w = uw[:, :DN_HD], uw[:, DN_HD:]
    attn = bdot(q, k, 1, 1) * decay
    own = (lax.broadcasted_iota(jnp.int32, (n, nh * DN_HD), 1) // DN_HD
           == lax.broadcasted_iota(jnp.int32, (n, nh * DN_HD), 0) // c)

    def spread(a):
        return jnp.where(own, jnp.concatenate([a] * nh, axis=1), 0.0)

    v_new = u - bdot(spread(w), state, 1, 0)
    o = bdot(spread(q * e_gc), state, 1, 0) + bdot(attn, v_new, 1, 0)
    keep = stack(lambda h: jnp.broadcast_to(jnp.exp(g_cols[h][c - 1:c, :]), (DN_HD, 1)))
    new_state = state * keep + bdot(spread(k * jnp.exp(g_last - g_col)), v_new, 0, 0)
    out = _rms(o, norm_row) * (zs * _sigmoid(zs))
    return new_state, jnp.concatenate([out[h * c:(h + 1) * c] for h in range(nh)], axis=1), t_inv


DN_STEP = 8 * DN_CHUNK


def _dn_step(state, xc, z, ba, alog_row, dtb_row, norm_row, t_known=None):
    outs, t_invs = [], []
    for c in range(DN_STEP // DN_CHUNK):
        rows = slice(c * DN_CHUNK, (c + 1) * DN_CHUNK)
        state, out, t_inv = _dn_chunk(state, xc[rows], z[rows], ba[rows], alog_row, dtb_row, norm_row,
                                      None if t_known is None else t_known[c])
        outs.append(out)
        t_invs.append(t_inv)
    return state, jnp.concatenate(outs, axis=0), jnp.stack(t_invs)


def _dn_specs(ts, order):
    zb = SEC_Z // MIX
    return [pl.BlockSpec((ts, 3 * MIX), lambda i: (order(i), 0)),
            pl.BlockSpec((ts, MIX), lambda i: (order(i), zb)),
            pl.BlockSpec((ts, LANES), lambda i: (order(i), 0)),
            pl.BlockSpec((1, LANES), lambda i: (0, 0)), pl.BlockSpec((1, LANES), lambda i: (0, 0)),
            pl.BlockSpec((1, LANES), lambda i: (0, 0))]


def dn_fwd(xc, proj, p_ba, alog_row, dtb_row, norm_row, *, side=None, name):
    s = xc.shape[0]
    ts = _tile(s, 512)
    n_step = ts // DN_STEP

    per_step = DN_STEP // DN_CHUNK
    n_tri = DN_HEADS * DN_CHUNK

    def body(xc_ref, z_ref, ba_ref, al_ref, dt_ref, nr_ref, o_ref, st_ref, tri_ref, state_ref):
        @pl.when(pl.program_id(0) == 0)
        def _():
            state_ref[...] = jnp.zeros_like(state_ref)

        def step(c, carry):
            rows = pl.ds(pl.multiple_of(c * DN_STEP, DN_STEP), DN_STEP)
            st_ref[c] = state_ref[...]
            new_state, out, t_invs = _dn_step(state_ref[...], xc_ref[rows, :], z_ref[rows, :].astype(f32), ba_ref[rows, :],
                                              al_ref[...], dt_ref[...], nr_ref[...])
            state_ref[...] = new_state
            o_ref[rows, :] = out.astype(bf16)
            tri_ref[pl.ds(c * per_step, per_step)] = t_invs
            return carry
        lax.fori_loop(0, n_step, step, 0)

    return host_call(
        side, body, name=name, grid=(s // ts,), in_specs=_dn_specs(ts, lambda i: i),
        out_specs=[pl.BlockSpec((ts, MIX), lambda i: (i, 0)),
                   pl.BlockSpec((n_step, DN_HEADS * DN_HD, DN_HD), lambda i: (i, 0, 0)),
                   pl.BlockSpec((n_step * per_step, n_tri, n_tri), lambda i: (i, 0, 0))],
        out_shape=[jax.ShapeDtypeStruct((s, MIX), bf16),
                   jax.ShapeDtypeStruct((s // DN_STEP, DN_HEADS * DN_HD, DN_HD), f32),
                   jax.ShapeDtypeStruct((s // DN_CHUNK, n_tri, n_tri), f32)],
        scratch_shapes=[pltpu.VMEM((DN_HEADS * DN_HD, DN_HD), f32)], aliases={},
        args=(xc, proj, p_ba, alog_row, dtb_row, norm_row))


def dn_bwd(xc, proj, p_ba, alog_row, dtb_row, norm_row, saved, tri, d_out, dproj, *, side=None, name):
    s = xc.shape[0]
    ts = _tile(s, 512)
    n_step = ts // DN_STEP
    nt = s // ts

    per_step = DN_STEP // DN_CHUNK
    n_tri = DN_HEADS * DN_CHUNK

    def body(xc_ref, z_ref, ba_ref, al_ref, dt_ref, nr_ref, st_ref, tri_ref, do_ref, _,
             dz_ref, dxc_ref, dba_ref, dal_ref, ddt_ref, dnr_ref, dstate_ref):
        @pl.when(pl.program_id(0) == 0)
        def _():
            dstate_ref[...] = jnp.zeros_like(dstate_ref)
            dal_ref[...] = jnp.zeros_like(dal_ref)
            ddt_ref[...] = jnp.zeros_like(ddt_ref)
            dnr_ref[...] = jnp.zeros_like(dnr_ref)

        def step(it, carry):
            c = n_step - 1 - it
            rows = pl.ds(pl.multiple_of(c * DN_STEP, DN_STEP), DN_STEP)
            t_known = tri_ref[pl.ds(c * per_step, per_step)]
            _, vjp = jax.vjp(lambda *a: _dn_step(*a, t_known=t_known)[:2], st_ref[c], xc_ref[rows, :],
                             z_ref[rows, :].astype(f32), ba_ref[rows, :], al_ref[...], dt_ref[...], nr_ref[...])
            d_in, dxc, dz, dba, dal, ddt, dnr = vjp((dstate_ref[...], do_ref[rows, :]))
            dstate_ref[...] = d_in
            dxc_ref[rows, :] = dxc
            dz_ref[rows, :] = dz.astype(bf16)
            dba_ref[rows, :] = dba.astype(bf16)
            dal_ref[...] += dal
            ddt_ref[...] += ddt
            dnr_ref[...] += dnr
            return carry
        lax.fori_loop(0, n_step, step, 0)

    def rev(i):
        return nt - 1 - i

    specs = _dn_specs(ts, rev)
    vec = pl.BlockSpec((1, LANES), lambda i: (0, 0))
    return host_call(
        side, body, name=name, grid=(nt,),
        in_specs=specs + [pl.BlockSpec((n_step, DN_HEADS * DN_HD, DN_HD), lambda i: (rev(i), 0, 0)),
                          pl.BlockSpec((n_step * per_step, n_tri, n_tri), lambda i: (rev(i), 0, 0)),
                          pl.BlockSpec((ts, MIX), lambda i: (rev(i), 0)), pl.BlockSpec(memory_space=pl.ANY)],
        out_specs=[specs[1], specs[0], specs[2], vec, vec, vec],
        out_shape=[jax.ShapeDtypeStruct(dproj.shape, bf16), jax.ShapeDtypeStruct((s, 3 * MIX), f32),
                   jax.ShapeDtypeStruct((s, LANES), bf16)] + [jax.ShapeDtypeStruct((1, LANES), f32)] * 3,
        scratch_shapes=[pltpu.VMEM((DN_HEADS * DN_HD, DN_HD), f32)], aliases={9: 0},
        args=(xc, proj, p_ba, alog_row, dtb_row, norm_row, saved, tri, d_out, dproj))


def _merge_in_specs(ts, d):
    row = pl.BlockSpec((ts, MIX), lambda i: (i, 0))
    return [row, row, row, pl.BlockSpec((ts, 3 * d), lambda i: (i, SEC_G // (3 * d))),
            pl.BlockSpec((3, MIX, d), lambda i: (0, 0, 0))]


def merge_fwd(out_a, out_b, out_c, proj, w_branch, *, name):
    s, d = out_a.shape[0], w_branch.shape[2]
    ts = _tile(s, 512, 16)

    def body(a_ref, b_ref, c_ref, g_ref, w_ref, o_ref):
        acc = jnp.zeros((ts, d), f32)
        for n, r in enumerate((a_ref, b_ref, c_ref)):
            acc = acc + _sigmoid(g_ref[:, n * d:(n + 1) * d].astype(f32)) * _dg(r[...], w_ref[n], 1, 0)
        o_ref[...] = acc.astype(bf16)

    return pl.pallas_call(
        body, name=name, grid=(s // ts,), in_specs=_merge_in_specs(ts, d),
        out_specs=pl.BlockSpec((ts, d), lambda i: (i, 0)), out_shape=jax.ShapeDtypeStruct((s, d), bf16),
        compiler_params=_params(1),
    )(out_a, out_b, out_c, proj, w_branch)


def merge_bwd(out_a, out_b, out_c, proj, w_branch, d_merged, dproj, *, name):
    s, d = out_a.shape[0], w_branch.shape[2]
    ts = _tile(s, 512, 16)

    def body(a_ref, b_ref, c_ref, g_ref, w_ref, dm_ref, _, dg_ref, da_ref, db_ref, dc_ref, dw_ref):
        @pl.when(pl.program_id(0) == 0)
        def _():
            dw_ref[...] = jnp.zeros_like(dw_ref)

        dm = dm_ref[...]
        for n, (r, dr) in enumerate(((a_ref, da_ref), (b_ref, db_ref), (c_ref, dc_ref))):
            gate = _sigmoid(g_ref[:, n * d:(n + 1) * d].astype(f32))
            branch = _dg(r[...], w_ref[n], 1, 0)
            dg_ref[:, n * d:(n + 1) * d] = (dm * branch * gate * (1.0 - gate)).astype(bf16)
            d_branch = dm * gate
            dr[...] = _dg(d_branch, w_ref[n], 1, 1)
            dw_ref[n] += _dg(r[...], d_branch, 0, 0)

    specs = _merge_in_specs(ts, d)
    row_f = pl.BlockSpec((ts, MIX), lambda i: (i, 0))
    return pl.pallas_call(
        body, name=name, grid=(s // ts,),
        in_specs=specs + [pl.BlockSpec((ts, d), lambda i: (i, 0)), pl.BlockSpec(memory_space=pl.ANY)],
        out_specs=[specs[3], row_f, row_f, row_f, specs[4]],
        out_shape=[jax.ShapeDtypeStruct(dproj.shape, bf16)] + [jax.ShapeDtypeStruct((s, MIX), f32)] * 3
        + [jax.ShapeDtypeStruct(w_branch.shape, f32)],
        input_output_aliases={6: 0}, compiler_params=_params(1),
    )(out_a, out_b, out_c, proj, w_branch, d_merged, dproj)


def gate_up_swiglu(h, w_gu, *, name):
    g2, d, w = w_gu.shape
    ng, s = g2 // 2, h.shape[0]
    tm = _tile(s, 1024, 16)

    def body(h_ref, wg_ref, wu_ref, g_ref, u_ref, a_ref):
        g16 = _dg(h_ref[...], wg_ref[...], 1, 0).astype(bf16)
        u16 = _dg(h_ref[...], wu_ref[...], 1, 0).astype(bf16)
        g_ref[...], u_ref[...] = g16, u16
        g = g16.astype(f32)
        a_ref[...] = (g * _sigmoid(g) * u16.astype(f32)).astype(bf16)

    out = pl.BlockSpec((None, tm, w), lambda i, j: (j, i, 0))
    return pl.pallas_call(
        body, name=name, grid=(s // tm, ng),
        in_specs=[pl.BlockSpec((tm, d), lambda i, j: (i, 0)), pl.BlockSpec((None, d, w), lambda i, j: (j, 0, 0)),
                  pl.BlockSpec((None, d, w), lambda i, j: (ng + j, 0, 0))],
        out_specs=[out, out, out], out_shape=[jax.ShapeDtypeStruct((ng, s, w), bf16)] * 3, compiler_params=_params(2),
    )(h, w_gu, w_gu)


def down_dx_swiglu(dx, w_down, gate, up, *, name):
    ng, w, d = w_down.shape
    s = dx.shape[0]
    tm = _tile(s, 1024, 16)

    def body(dx_ref, wd_ref, g_ref, u_ref, dg_ref, du_ref):
        d_act = _dg(dx_ref[...], wd_ref[...], 1, 1)
        g = g_ref[...].astype(f32)
        sig = _sigmoid(g)
        dg_ref[...] = (d_act * u_ref[...].astype(f32) * sig * (1.0 + g * (1.0 - sig))).astype(bf16)
        du_ref[...] = (d_act * g * sig).astype(bf16)

    blk = pl.BlockSpec((None, tm, w), lambda i, j: (j, i, 0))
    return pl.pallas_call(
        body, name=name, grid=(s // tm, ng),
        in_specs=[pl.BlockSpec((tm, d), lambda i, j: (i, 0)), pl.BlockSpec((None, w, d), lambda i, j: (j, 0, 0)), blk, blk],
        out_specs=[blk, blk], out_shape=[jax.ShapeDtypeStruct((ng, s, w), bf16)] * 2, compiler_params=_params(2),
    )(dx, w_down, gate, up)


def adamw(w, m, v, g_parts, *, name):
    n_layers = len(g_parts)
    n_parts, r, cols = g_parts[0].shape
    lanes = -(-cols // LANES) * LANES
    tr = _tile(r, max(16, (128 * 1024) // lanes), 16)
    nr = r // tr

    def body(w_ref, m_ref, v_ref, *rest):
        gp_refs, (g_ref, d_ref, nm_ref, nv_ref) = rest[:n_layers], rest[n_layers:]
        layer = pl.program_id(0)
        g = jnp.zeros((tr, cols), f32)
        for l, gp_ref in enumerate(gp_refs):
            g_l = gp_ref[0].astype(f32)
            for k in range(1, n_parts):
                g_l = g_l + gp_ref[k].astype(f32)
            g = jnp.where(layer == l, g_l, g)
        nm = ADAM_B1 * m_ref[...] + (1.0 - ADAM_B1) * g
        nv = ADAM_B2 * v_ref[...] + (1.0 - ADAM_B2) * jnp.square(g)
        m_hat = nm / (1.0 - ADAM_B1 ** ADAM_STEP)
        v_hat = nv / (1.0 - ADAM_B2 ** ADAM_STEP)
        g_ref[...] = g
        d_ref[...] = -ADAM_LR * (m_hat / (jnp.sqrt(v_hat) + ADAM_EPS) + ADAM_WD * w_ref[...])
        nm_ref[...] = nm
        nv_ref[...] = nv

    row = pl.BlockSpec((tr, cols), lambda l, i: (l * nr + i, 0))

    def parts_spec(own):
        return pl.BlockSpec((n_parts, tr, cols),
                            lambda l, i: (0, jnp.where(l == own, i, jnp.where(l < own, 0, nr - 1)), 0))

    return pl.pallas_call(
        body, name=name, grid=(n_layers, nr), in_specs=[row, row, row] + [parts_spec(l) for l in range(n_layers)],
        out_specs=[row] * 4, out_shape=[jax.ShapeDtypeStruct(w.shape, f32)] * 4, compiler_params=_params(2),
    )(w, m, v, *g_parts)


def _mesh_pos():
    return lax.axis_index("x"), lax.axis_index("y"), lax.axis_index("c")


def _dev_index(p):
    return 4 * p[0] + 2 * p[1] + p[2]


class Exchange:
    def __init__(self, kind, arrays):
        self.kind, self.arrays, self.n = kind, list(arrays), len(arrays)
        self.specs = [pl.BlockSpec(memory_space=pl.ANY)] * self.n
        self.out_shapes = [jax.ShapeDtypeStruct(((N_DEV,) if kind == "gather" else ()) + a.shape, a.dtype)
                           for a in self.arrays]
        self.scratch = [pltpu.SemaphoreType.DMA((self.n, N_DEV - 1)), pltpu.SemaphoreType.DMA((self.n, N_DEV - 1)),
                        pltpu.SemaphoreType.DMA((self.n,))]

    def _copies(self, in_refs, out_refs, sems, with_arrivals):
        send_sems, recv_sems, local_sems = sems
        x, y, c = _mesh_pos()
        mine = _dev_index((x, y, c))

        def src(a, slab):
            return in_refs[a] if self.kind == "gather" else in_refs[a].at[slab]

        local = [pltpu.make_async_copy(src(a, mine), out_refs[a].at[mine], local_sems.at[a]) for a in range(self.n)]
        sends, arrivals = [], []
        for k in range(1, N_DEV):
            peer = (1 - x if k & 4 else x, 1 - y if k & 2 else y, 1 - c if k & 1 else c)
            theirs = _dev_index(peer)
            for a in range(self.n):
                to = dict(send_sem=send_sems.at[a, k - 1], recv_sem=recv_sems.at[a, k - 1], device_id=peer,
                          device_id_type=pl.DeviceIdType.MESH)
                sends.append(pltpu.make_async_remote_copy(src_ref=src(a, theirs), dst_ref=out_refs[a].at[mine], **to))
                if with_arrivals:
                    arrivals.append(pltpu.make_async_remote_copy(src_ref=src(a, theirs),
                                                                 dst_ref=out_refs[a].at[theirs], **to))
        return local, sends, arrivals

    def start(self, in_refs, out_refs, sems):
        local, sends, _ = self._copies(in_refs, out_refs, sems, False)
        for cp in local + sends:
            cp.start()

    def wait(self, in_refs, out_refs, sems):
        local, sends, arrivals = self._copies(in_refs, out_refs, sems, True)
        for cp in arrivals:
            cp.wait_recv()
        for cp in sends:
            cp.wait_send()
        for cp in local:
            cp.wait()

    def run_around(self, grid, in_refs, out_refs, sems, *, before):
        at = None
        for axis, size in enumerate(grid):
            hit = pl.program_id(axis) == (0 if before else size - 1)
            at = hit if at is None else at & hit

        @pl.when(at)
        def _():
            (self.start if before else self.wait)(in_refs, out_refs, sems)


def host_call(side, body, *, name, grid, in_specs, out_specs, out_shape, scratch_shapes, args, aliases):
    n_in, n_out = len(in_specs), len(out_specs)
    if side is None:
        kernel_body = body
    else:
        n = side.n
        in_specs, args = in_specs + side.specs, tuple(args) + tuple(side.arrays)
        out_specs, out_shape = out_specs + side.specs, out_shape + side.out_shapes
        scratch_shapes = scratch_shapes + side.scratch

        def kernel_body(*refs):
            ins, side_in = refs[:n_in], refs[n_in:n_in + n]
            outs, side_out = refs[n_in + n:n_in + n + n_out], refs[n_in + n + n_out:n_in + 2 * n + n_out]
            scratch, sems = refs[n_in + 2 * n + n_out:-3], refs[-3:]
            side.run_around(grid, side_in, side_out, sems, before=True)
            body(*ins, *outs, *scratch)
            side.run_around(grid, side_in, side_out, sems, before=False)

    outs = pl.pallas_call(
        kernel_body, name=name, grid=grid, in_specs=in_specs, out_specs=out_specs, out_shape=out_shape,
        scratch_shapes=scratch_shapes, input_output_aliases=aliases, compiler_params=_params(len(grid)),
    )(*args)
    return outs if side is None else (outs[:n_out], outs[n_out:])


def all_gather(blocks, *, name):
    n = len(blocks)
    any_spec = pl.BlockSpec(memory_space=pl.ANY)

    def body(*refs):
        ins, outs = refs[:n], refs[n:2 * n]
        send_sems, recv_sems, local_sems = refs[2 * n:]
        x, y, c = _mesh_pos()
        me, sibling = (x, y, c), (x, y, 1 - c)
        chips = [(1 - x, y), (x, 1 - y), (1 - x, 1 - y)]

        def copy(a, k, block, to, src=None):
            dst = outs[a].at[_dev_index(block)]
            return pltpu.make_async_remote_copy(
                src_ref=dst if src is None else src, dst_ref=dst, send_sem=send_sems.at[a, k],
                recv_sem=recv_sems.at[a, k], device_id=to, device_id_type=pl.DeviceIdType.MESH)

        mine = [pltpu.make_async_copy(ins[a], outs[a].at[_dev_index(me)], local_sems.at[a]) for a in range(n)]
        for cp in mine:
            cp.start()
        first = []
        for a in range(n):
            first.append(copy(a, 0, me, sibling, src=ins[a]))
            first += [copy(a, 1 + j, me, (*chip, c), src=ins[a]) for j, chip in enumerate(chips)]
        for cp in first:
            cp.start()
        passed = []
        for j, chip in enumerate(chips):
            for a in range(n):
                copy(a, 1 + j, (*chip, c), me).wait_recv()
                fwd = copy(a, 4 + j, (*chip, c), sibling)
                fwd.start()
                passed.append(fwd)
        for a in range(n):
            copy(a, 0, sibling, me).wait_recv()
            for j, chip in enumerate(chips):
                copy(a, 4 + j, (*chip, 1 - c), me).wait_recv()
        for cp in first + passed:
            cp.wait_send()
        for cp in mine:
            cp.wait()

    return pl.pallas_call(
        body, name=name, in_specs=[any_spec] * n, out_specs=[any_spec] * n,
        out_shape=[jax.ShapeDtypeStruct((N_DEV,) + b.shape, b.dtype) for b in blocks],
        scratch_shapes=[pltpu.SemaphoreType.DMA((n, 7)), pltpu.SemaphoreType.DMA((n, 7)),
                        pltpu.SemaphoreType.DMA((n,))],
    )(*blocks)


def exchange(jobs, *, name):
    total = sum(j.n for j in jobs)

    def body(*refs):
        ins, outs, sems = refs[:total], refs[total:2 * total], refs[2 * total:]
        pieces, off = [], 0
        for i, j in enumerate(jobs):
            pieces.append((ins[off:off + j.n], outs[off:off + j.n], sems[3 * i:3 * i + 3]))
            off += j.n
        for j, piece in zip(jobs, pieces):
            j.start(*piece)
        for j, piece in zip(jobs, pieces):
            j.wait(*piece)

    outs = pl.pallas_call(
        body, name=name, in_specs=[s for j in jobs for s in j.specs], out_specs=[s for j in jobs for s in j.specs],
        out_shape=[s for j in jobs for s in j.out_shapes], scratch_shapes=[s for j in jobs for s in j.scratch],
    )(*[a for j in jobs for a in j.arrays])
    split, off = [], 0
    for j in jobs:
        split.append(outs[off:off + j.n])
        off += j.n
    return split


def _rows128(arr):
    flat = arr.reshape(-1)
    rows = -(-flat.shape[0] // (8 * LANES)) * 8
    return jnp.pad(flat, (0, rows * LANES - flat.shape[0])).reshape(rows, LANES)


def _pad_lanes(row, width=LANES, at=0):
    return jnp.pad(row, (at, width - at - row.shape[0])).reshape(1, width)


def _w_in_sections(got):
    d = got.shape[1]
    wi = jnp.transpose(got, (1, 0, 2)).reshape(d, -1)
    w_main = jnp.concatenate([wi[:, :C_B], wi[:, C_Z:C_BA], wi[:, C_QKV:C_Z], wi[:, C_G:], wi[:, C_B:C_QKV]], axis=1)
    return w_main, jnp.pad(wi[:, C_BA:C_G], ((0, 0), (0, LANES - (C_G - C_BA))))


def _w_in_parts(gw_main, gw_ba):
    d = gw_main.shape[0]
    full = jnp.concatenate([gw_main[:, SEC_A:SEC_Z], gw_main[:, SEC_B:], gw_main[:, SEC_QKV:SEC_G],
                            gw_main[:, SEC_Z:SEC_QKV], gw_ba[:, :C_G - C_BA], gw_main[:, SEC_G:SEC_B]], axis=1)
    return jnp.transpose(full.reshape(d, N_DEV, -1), (1, 0, 2))


def kernel(x, positions, attn_norm, w_in, sgu_ln_g, sgu_ln_b, sgu_w, sgu_b, attn_sinks, dn_conv_w, dn_a_log, dn_dt_bias, dn_norm, w_branch, w_out, ffn_norm, w_gate_up, w_down, final_norm, loss_target, m_attn_norm, m_w_in, m_sgu_ln_g, m_sgu_ln_b, m_sgu_w, m_sgu_b, m_attn_sinks, m_dn_conv_w, m_dn_a_log, m_dn_dt_bias, m_dn_norm, m_w_branch, m_w_out, m_ffn_norm, m_w_gate_up, m_w_down, m_final_norm, v_attn_norm, v_w_in, v_sgu_ln_g, v_sgu_ln_b, v_sgu_w, v_sgu_b, v_attn_sinks, v_dn_conv_w, v_dn_a_log, v_dn_dt_bias, v_dn_norm, v_w_branch, v_w_out, v_ffn_norm, v_w_gate_up, v_w_down, v_final_norm):
    given = dict(locals())
    depth, d_model = attn_norm.shape
    s = x.shape[1]
    x2 = x.reshape(s, d_model)
    target = loss_target.reshape(s, d_model)
    posf = positions.reshape(s, 1).astype(f32)
    inv_freq = ROPE_THETA ** (-jnp.arange(0, ROPE_DIM, 2, dtype=f32) / ROPE_DIM)
    inv_head = jnp.concatenate([inv_freq, inv_freq, jnp.zeros((SWA_HD - ROPE_DIM,), f32)])
    tables = rope_tables(posf, jnp.tile(inv_head, LANES // SWA_HD).reshape(1, LANES), name="rope_tables")

    assert depth == 2, depth
    gathered = dict(zip([("w_in", 0), ("dn_conv_w", 0), ("dn_conv_w", 1)], all_gather(
        [w_in[0].astype(bf16), dn_conv_w[0], dn_conv_w[1]], name="gather_first")))
    riders = {"l0_in_proj": [("w_branch", 0), ("w_out", 0), ("w_down", 0)],
              "l0_deltanet": [("w_gate_up", 0), ("w_in", 1)],
              "l1_deltanet": [("w_branch", 1), ("w_out", 1), ("w_gate_up", 1), ("w_down", 1)]}

    def gathering(host, call, *args, **kw):
        if host not in riders:
            return call(*args, name=host, **kw)
        side = Exchange("gather", [given[n][l].astype(bf16) for n, l in riders[host]])
        out, got = call(*args, side=side, name=host, **kw)
        gathered.update(zip(riders[host], got))
        return out

    layers, saved = [], []
    h_in = x2
    for l in range(depth):
        t = f"l{l}_"
        w_main, w_ba = _w_in_sections(gathered["w_in", l])
        conv_full = jnp.transpose(gathered["dn_conv_w", l], (1, 0, 2)).reshape(DN_CONV, -1)
        p = dict(
            w_main=w_main, w_ba=w_ba, conv_w8=jnp.pad(conv_full, ((0, CONV_PAD - DN_CONV), (0, 0))),
            attn_norm=attn_norm[l].reshape(1, -1), ffn_norm=ffn_norm[l].reshape(1, -1),
            ln_g=sgu_ln_g[l].reshape(1, -1), ln_b=sgu_ln_b[l].reshape(1, -1), sgu_w=sgu_w[l], sgu_bt=sgu_b[l].T,
            sink_row=_pad_lanes(attn_sinks[l]), alog_row=_pad_lanes(dn_a_log[l], at=DN_HEADS),
            dtb_row=_pad_lanes(dn_dt_bias[l], at=DN_HEADS), norm_row=dn_norm[l].reshape(1, -1))
        layers.append(p)
        h = rmsnorm_fwd(h_in, p["attn_norm"], name=t + "attn_norm")
        proj = gathering(t + "in_proj", matmul, h, p["w_main"], "nn", bf16)
        p_ba = matmul(h, p["w_ba"], "nn", f32, name=t + "in_proj_ba")
        out_a = sgu_fwd(proj, p["ln_g"], p["ln_b"], p["sgu_w"], p["sgu_bt"], name=t + "sgu")
        out_b = swa_fwd(proj, p["sink_row"], tables, name=t + "swa")
        xc = conv_fwd(proj, p["conv_w8"], name=t + "dn_conv")
        out_c, states, tri = gathering(t + "deltanet", dn_fwd, xc, proj, p_ba, p["alog_row"], p["dtb_row"], p["norm_row"])
        p.update(w_branch=jnp.transpose(gathered["w_branch", l], (1, 2, 0, 3)).reshape(3, MIX, d_model),
                 w_out=gathered["w_out", l].reshape(d_model, d_model),
                 w_gu=gathered["w_gate_up", l],
                 w_down=gathered["w_down", l].reshape(N_DEV // 2, -1, d_model))
        merged = merge_fwd(out_a, out_b, out_c, proj, p["w_branch"], name=t + "merge")
        x_mid = matmul(merged, p["w_out"], "nn", f32, residual=h_in, tn=1024, name=t + "out_proj")
        h2 = rmsnorm_fwd(x_mid, p["ffn_norm"], name=t + "ffn_norm")
        gate, up, act = gate_up_swiglu(h2, p["w_gu"], name=t + "gate_up")
        x_out = matmul(act, p["w_down"], "nn", f32, residual=x_mid, group="k", tn=1024, name=t + "down")
        saved.append(dict(x_in=h_in, h=h, proj=proj, p_ba=p_ba, out_a=out_a, out_b=out_b, out_c=out_c, xc=xc,
                          states=states, tri=tri, merged=merged, x_mid=x_mid, h2=h2, gate=gate, up=up, act=act))
        h_in = x_out

    dx, d_final_norm, loss_row = loss_head(h_in, final_norm.reshape(1, -1), target, name="loss_head")
    loss = lax.psum(loss_row[0, 0], MESH_AXES)

    shard_names = ["w_in", "dn_conv_w", "w_branch", "w_out", "w_gate_up", "w_down"]
    rep_names = ["attn_norm", "sgu_ln_g", "sgu_ln_b", "sgu_w", "sgu_b", "attn_sinks", "dn_a_log", "dn_dt_bias",
                 "dn_norm", "ffn_norm"]
    parts, received, per_layer = {}, {}, []
    senders = {"l0_b_swa": [("w_gate_up", 1), ("w_down", 1), ("w_out", 1), ("w_branch", 1)],
               "l0_b_deltanet": [("w_in", 1), ("w_gate_up", 0), ("w_down", 0)],
               "l0_b_dn_conv": [("w_out", 0), ("w_branch", 0)],
               "l0_b_in_proj_dx": [("w_in", 0)]}

    def scattering(host, call, *args, **kw):
        if host not in senders:
            return call(*args, name=host, **kw)
        out, got = call(*args, side=Exchange("scatter", [parts[key] for key in senders[host]]), name=host, **kw)
        received.update(zip(senders[host], got))
        return out

    for l in reversed(range(depth)):
        p, sv, t = layers[l], saved[l], f"l{l}_b_"
        d_gate, d_up = down_dx_swiglu(dx, p["w_down"], sv["gate"], sv["up"], name=t + "down_dx")
        gw_down = matmul(sv["act"], dx, "tn", bf16, group="m", tk=2048, tn=512, name=t + "down_dw")
        gw_gu = jnp.concatenate([matmul(sv["h2"], d_half, "tn", bf16, group="n", tk=2048, name=t + "gate_up_dw" + tag)
                                 for d_half, tag in ((d_gate, "_gate"), (d_up, "_up"))], axis=0)
        half = N_DEV // 2
        d_h2 = matmul(d_gate, p["w_gu"][:half], "nt", f32, group="k", tn=1024, name=t + "gate_up_dx_gate")
        d_h2 = matmul(d_up, p["w_gu"][half:], "nt", f32, group="k", tn=1024, residual=d_h2, name=t + "gate_up_dx_up")
        dx_mid, g_ffn = rmsnorm_bwd(sv["x_mid"], p["ffn_norm"], d_h2, dx, name=t + "ffn_norm")
        d_merged = matmul(dx_mid, p["w_out"], "nt", f32, tn=1024, name=t + "out_proj_dx")
        gw_out = matmul(sv["merged"], dx_mid, "tn", bf16, tk=2048, tn=1024, name=t + "out_proj_dw")
        dproj = lax.empty((s, W_MAIN), bf16)
        dproj, d_a, d_b, d_c, gw_branch = merge_bwd(sv["out_a"], sv["out_b"], sv["out_c"], sv["proj"], p["w_branch"],
                                                   d_merged, dproj, name=t + "merge")
        parts.update({("w_gate_up", l): gw_gu, ("w_down", l): gw_down.reshape(N_DEV, -1, d_model),
                      ("w_out", l): gw_out.reshape(N_DEV, -1, d_model),
                      ("w_branch", l): jnp.transpose(gw_branch.reshape(3, MIX, N_DEV, -1), (2, 0, 1, 3)).astype(bf16)})
        dproj, g_ln_g, g_ln_b, g_sgu_w, g_sgu_bt = sgu_bwd(sv["proj"], p["ln_g"], p["ln_b"], p["sgu_w"], p["sgu_bt"],
                                                         d_a, dproj, name=t + "sgu")
        dproj, g_sink = scattering(t + "swa", swa_bwd, sv["proj"], p["sink_row"], tables, d_b, dproj)
        dproj, dxc, dba, g_alog, g_dtb, g_dnorm = scattering(
            t + "deltanet", dn_bwd, sv["xc"], sv["proj"], sv["p_ba"], p["alog_row"], p["dtb_row"], p["norm_row"],
            sv["states"], sv["tri"], d_c, dproj)
        dproj, g_conv8 = scattering(t + "dn_conv", conv_bwd, sv["proj"], p["conv_w8"], dxc, dproj)
        gw_main = matmul(sv["h"], dproj, "tn", bf16, tk=2048, name=t + "in_proj_dw")
        gw_ba = matmul(sv["h"], dba, "tn", bf16, tk=2048, name=t + "in_proj_ba_dw")
        parts["w_in", l] = _w_in_parts(gw_main, gw_ba)
        d_h = scattering(t + "in_proj_dx", matmul, dproj, p["w_main"], "nt", f32, tk=2304, tn=1024)
        d_h = matmul(dba, p["w_ba"], "nt", f32, residual=d_h, name=t + "in_proj_ba_dx")
        dx, g_attn = rmsnorm_bwd(sv["x_in"], p["attn_norm"], d_h, dx_mid, name=t + "attn_norm")
        per_layer.append(dict(
            dn_conv_w=jnp.transpose(g_conv8[:DN_CONV].reshape(DN_CONV, N_DEV, -1), (1, 0, 2)),
            attn_norm=g_attn, sgu_ln_g=g_ln_g, sgu_ln_b=g_ln_b, sgu_w=g_sgu_w, sgu_b=g_sgu_bt.T,
            attn_sinks=g_sink[0, :SWA_HEADS], dn_a_log=g_alog[0, DN_HEADS:2 * DN_HEADS],
            dn_dt_bias=g_dtb[0, DN_HEADS:2 * DN_HEADS], dn_norm=g_dnorm, ffn_norm=g_ffn))
    per_layer.reverse()

    conv_parts = jnp.concatenate([pp["dn_conv_w"] for pp in per_layer], axis=1)
    rep_grads = {n: jnp.stack([pp[n].reshape(given[n].shape[1:]) for pp in per_layer]) for n in rep_names}
    rep_grads["final_norm"] = d_final_norm[0]
    rep_names = rep_names + ["final_norm"]
    rep_rows = [_rows128(given[n]).shape[0] for n in rep_names]
    pad_rows = -sum(rep_rows) % 16

    def small_rows(values):
        pieces = [_rows128(values[n]) for n in rep_names]
        return jnp.concatenate(pieces + [jnp.zeros((pad_rows, LANES), f32)], axis=0)

    (got_conv,), (small_all,) = exchange(
        [Exchange("scatter", [conv_parts]), Exchange("gather", [small_rows(rep_grads)])], name="exchange_last")

    results = [{}, {}, {}, {}]
    for n in shard_names:
        shp = given[n].shape
        two = (-1, shp[-1])
        by_layer = [got_conv] if n == "dn_conv_w" else [received[n, l].reshape(N_DEV, -1, shp[-1]) for l in range(depth)]
        outs = adamw(given[n].reshape(two), given["m_" + n].reshape(two), given["v_" + n].reshape(two), by_layer,
                     name="adamw_" + n)
        for res, val in zip(results, outs):
            res[n] = val.reshape(shp)
    outs = adamw(small_rows(given), small_rows({n: given["m_" + n] for n in rep_names}),
                 small_rows({n: given["v_" + n] for n in rep_names}), [small_all], name="adamw_replicated")
    for res, val in zip(results, outs):
        row = 0
        for n, nr in zip(rep_names, rep_rows):
            res[n] = val[row:row + nr].reshape(-1)[:given[n].size].reshape(given[n].shape)
            row += nr
    order = ["attn_norm", "w_in", "sgu_ln_g", "sgu_ln_b", "sgu_w", "sgu_b", "attn_sinks", "dn_conv_w", "dn_a_log",
             "dn_dt_bias", "dn_norm", "w_branch", "w_out", "ffn_norm", "w_gate_up", "w_down", "final_norm"]
    return (loss, dx.reshape(x.shape), *[res[n] for res in results for n in order])
```

```python
import functools

import jax
import jax.numpy as jnp
from jax import lax
from jax.experimental import pallas as pl
from jax.experimental.pallas import tpu as pltpu

f32 = jnp.float32
bf16 = jnp.bfloat16

N_DEV = 8
MESH_AXES = ("x", "y", "c")
NORM_EPS = 1e-6
MIX = 512
SGU_GROUPS, SGU_CHUNK = 4, 128
SWA_HEADS, SWA_KV, SWA_HD, WINDOW = 8, 2, 64, 128
ROPE_THETA, ROPE_DIM = 500000.0, 16
DN_HEADS, DN_HD, DN_CONV, DN_CHUNK = 4, 128, 4, 64
ADAM_LR, ADAM_B1, ADAM_B2, ADAM_EPS, ADAM_WD, ADAM_STEP = 0.001, 0.9, 0.999, 1e-08, 0.01, 10

LANES = 128
VMEM_LIMIT = 56 * 1024 * 1024

SEC_A, SEC_Z, SEC_QKV, SEC_G, SEC_B = 0, 1024, 1536, 3072, 6144
W_MAIN = 6912
C_B, C_QKV, C_Z, C_BA, C_G = 1024, 1792, 3328, 3840, 3848


def _params(n_axes, **kw):
    return pltpu.CompilerParams(dimension_semantics=("arbitrary",) * n_axes, vmem_limit_bytes=VMEM_LIMIT, **kw)


def _tile(n, target, mult=LANES):
    if n <= target:
        return n
    best = None
    for t in range(mult, target + 1, mult):
        if n % t == 0:
            best = t
    assert best is not None, (n, target, mult)
    return best


def _dg(a, b, ca, cb):
    return lax.dot_general(a.astype(bf16), b.astype(bf16), (((ca,), (cb,)), ((), ())), preferred_element_type=f32)


def _dg3(a, b, ca, cb):
    a_hi, b_hi = a.astype(bf16), b.astype(bf16)
    a_lo, b_lo = (a - a_hi.astype(f32)).astype(bf16), (b - b_hi.astype(f32)).astype(bf16)

    def dot(p, q):
        return lax.dot_general(p, q, (((ca,), (cb,)), ((), ())), preferred_element_type=f32)

    return dot(a_hi, b_hi) + (dot(a_hi, b_lo) + dot(a_lo, b_hi))


def _differentiable_dot(core):
    @functools.partial(jax.custom_vjp, nondiff_argnums=(2, 3))
    def dot(a, b, ca, cb):
        return core(a, b, ca, cb)

    def fwd(a, b, ca, cb):
        return core(a, b, ca, cb), (a, b)

    def bwd(ca, cb, res, ct):
        a, b = res
        da = core(ct, b, 1, 1 - cb) if ca == 1 else core(b, ct, 1 - cb, 1)
        db = core(a, ct, 1 - ca, 0) if cb == 0 else core(ct, a, 0, 1 - ca)
        return da, db

    dot.defvjp(fwd, bwd)
    return dot


bdot = _differentiable_dot(_dg)
_hdot = _differentiable_dot(_dg3)


def hdot(a, b, ca=1, cb=0):
    return _hdot(a, b, ca, cb)


@functools.partial(jax.custom_vjp, nondiff_argnums=(1,))
def lroll(x, shift):
    return pltpu.roll(x, shift, 1)


def _lroll_fwd(x, shift):
    return pltpu.roll(x, shift, 1), None


def _lroll_bwd(shift, _, ct):
    return (pltpu.roll(ct, ct.shape[1] - shift, 1),)


lroll.defvjp(_lroll_fwd, _lroll_bwd)


@functools.partial(jax.custom_vjp, nondiff_argnums=(1,))
def tri_inv(low, nil):
    n = low.shape[0]
    row = lax.broadcasted_iota(jnp.int32, (n, n), 0)
    col = lax.broadcasted_iota(jnp.int32, (n, n), 1)
    eye = (row == col).astype(f32)
    m = -low
    p = eye + m
    span = 2
    while span < nil:
        dot = hdot if span == 2 else (lambda a, b: bdot(a, b, 1, 0))
        m = dot(m, m)
        p = p + dot(p, m)
        span *= 2
    return p


def _tri_inv_fwd(low, nil):
    t = tri_inv(low, nil)
    return t, t


def _tri_inv_bwd(nil, t, dt):
    return (-bdot(t, bdot(dt, t, 1, 1), 0, 0),)


tri_inv.defvjp(_tri_inv_fwd, _tri_inv_bwd)


@jax.custom_vjp
def tri_inv_known(low, t):
    return t


def _tri_inv_known_fwd(low, t):
    return t, t


def _tri_inv_known_bwd(t, dt):
    return _tri_inv_bwd(None, t, dt) + (jnp.zeros_like(t),)


tri_inv_known.defvjp(_tri_inv_known_fwd, _tri_inv_known_bwd)


def _sigmoid(x):
    return 1.0 / (1.0 + jnp.exp(-x))


def _rms(x, g):
    return x * lax.rsqrt(jnp.mean(x * x, axis=-1, keepdims=True) + NORM_EPS) * g


def _lane_col(x, lane_idx):
    lane = lax.broadcasted_iota(jnp.int32, x.shape, 1)
    return jnp.sum(jnp.where(lane == lane_idx, x, 0.0), axis=1, keepdims=True)


def matmul(a, b, mode, out_dtype, *, residual=None, group=None, side=None, tm=1024, tn=768, tk=1024, name):
    dims = {"a": ("m", "k") if mode != "tn" else ("k", "m"),
            "b": {"nn": ("k", "n"), "nt": ("n", "k"), "tn": ("k", "n")}[mode], "o": ("m", "n")}
    full, groups = {}, 1
    for arr, key in ((a, "a"), (b, "b")):
        grouped = group in dims[key]
        if grouped:
            groups = arr.shape[0]
        full[dims[key][0]], full[dims[key][1]] = arr.shape[1:] if grouped else arr.shape
    want = {"m": tm, "n": tn, "k": tk}
    per_step = min(groups, 4) if group == "k" else 1
    tiles = {d: full[d] if d == group else _tile(full[d], want[d]) for d in "mnk"}
    steps = {d: groups // per_step if d == group else full[d] // tiles[d] for d in "mnk"}

    def spec(key):
        d0, d1 = dims[key]

        def index(i, j, kk):
            at = {"m": i, "n": j, "k": kk}
            if group in (d0, d1):
                return (at[group], 0 if d0 == group else at[d0], 0 if d1 == group else at[d1])
            return (at[d0], at[d1])

        block = (tiles[d0], tiles[d1])
        if group in (d0, d1):
            block = ((per_step if group == "k" else None),) + block
        return pl.BlockSpec(block, index)

    ca, cb = {"nn": (1, 0), "nt": (1, 1), "tn": (0, 0)}[mode]
    nk = steps["k"]
    o_spec = spec("o")
    out_shape = (groups, full["m"], full["n"]) if group in ("m", "n") else (full["m"], full["n"])
    has_res = residual is not None

    def product(a_ref, b_ref):
        if group != "k":
            return _dg(a_ref[...], b_ref[...], ca, cb)
        total = _dg(a_ref[0], b_ref[0], ca, cb)
        for g in range(1, per_step):
            total = total + _dg(a_ref[g], b_ref[g], ca, cb)
        return total

    def body(a_ref, b_ref, *rest):
        r_ref = rest[0] if has_res else None
        o_ref = rest[1 if has_res else 0]

        def emit(acc):
            o_ref[...] = (acc + r_ref[...] if has_res else acc).astype(out_dtype)

        if nk == 1:
            emit(product(a_ref, b_ref))
            return
        acc_ref = rest[-1]
        kk = pl.program_id(2)

        @pl.when(kk == 0)
        def _():
            acc_ref[...] = jnp.zeros_like(acc_ref)

        acc_ref[...] += product(a_ref, b_ref)

        @pl.when(kk == nk - 1)
        def _():
            emit(acc_ref[...])

    res = host_call(
        side, body, name=name, grid=(steps["m"], steps["n"], nk),
        in_specs=[spec("a"), spec("b")] + ([o_spec] if has_res else []), out_specs=[o_spec],
        out_shape=[jax.ShapeDtypeStruct(out_shape, out_dtype)],
        scratch_shapes=[pltpu.VMEM((tiles["m"], tiles["n"]), f32)] if nk > 1 else [], aliases={},
        args=(a, b) + ((residual,) if has_res else ()))
    return res[0] if side is None else (res[0][0], res[1])


def rmsnorm_fwd(x, g_row, *, name):
    s, d = x.shape
    ts = _tile(s, 512, 16)

    def body(x_ref, g_ref, o_ref):
        o_ref[...] = _rms(x_ref[...], g_ref[...]).astype(bf16)

    return pl.pallas_call(
        body, name=name, grid=(s // ts,),
        in_specs=[pl.BlockSpec((ts, d), lambda i: (i, 0)), pl.BlockSpec((1, d), lambda i: (0, 0))],
        out_specs=pl.BlockSpec((ts, d), lambda i: (i, 0)), out_shape=jax.ShapeDtypeStruct((s, d), bf16),
        compiler_params=_params(1),
    )(x, g_row)


def rmsnorm_bwd(x, g_row, dh, dres, *, name):
    s, d = x.shape
    ts = _tile(s, 512, 16)

    def body(x_ref, g_ref, dh_ref, dres_ref, dx_ref, dg_ref):
        @pl.when(pl.program_id(0) == 0)
        def _():
            dg_ref[...] = jnp.zeros_like(dg_ref)

        _, vjp = jax.vjp(_rms, x_ref[...], g_ref[...])
        dx, dg = vjp(dh_ref[...])
        dx_ref[...] = dx + dres_ref[...]
        dg_ref[...] += dg

    row = pl.BlockSpec((ts, d), lambda i: (i, 0))
    vec = pl.BlockSpec((1, d), lambda i: (0, 0))
    return pl.pallas_call(
        body, name=name, grid=(s // ts,), in_specs=[row, vec, row, row], out_specs=[row, vec],
        out_shape=[jax.ShapeDtypeStruct((s, d), f32), jax.ShapeDtypeStruct((1, d), f32)],
        compiler_params=_params(1),
    )(x, g_row, dh, dres)


def loss_head(x, g_row, target, *, name):
    s, d = x.shape
    ts = _tile(s, 512, 16)

    def body(x_ref, g_ref, t_ref, dx_ref, dg_ref, loss_ref):
        @pl.when(pl.program_id(0) == 0)
        def _():
            dg_ref[...] = jnp.zeros_like(dg_ref)
            loss_ref[...] = jnp.zeros_like(loss_ref)

        y, vjp = jax.vjp(_rms, x_ref[...], g_ref[...])
        err = y - t_ref[...]
        dx, dg = vjp(err * (1.0 / d))
        dx_ref[...] = dx
        dg_ref[...] += dg
        loss_ref[...] += 0.5 * jnp.sum(jnp.sum(err * err, axis=1, keepdims=True) * (1.0 / d), axis=0, keepdims=True)

    row = pl.BlockSpec((ts, d), lambda i: (i, 0))
    vec = pl.BlockSpec((1, d), lambda i: (0, 0))
    one = pl.BlockSpec((1, LANES), lambda i: (0, 0))
    return pl.pallas_call(
        body, name=name, grid=(s // ts,), in_specs=[row, vec, row], out_specs=[row, vec, one],
        out_shape=[jax.ShapeDtypeStruct((s, d), f32), jax.ShapeDtypeStruct((1, d), f32),
                   jax.ShapeDtypeStruct((1, LANES), f32)],
        compiler_params=_params(1),
    )(x, g_row, target)


def _sgu_chunk(p_a, ln_g, ln_b, w, b_t):
    t = SGU_CHUNK
    u = jax.nn.gelu(p_a[:, :MIX])
    v = jax.nn.gelu(p_a[:, MIX:])
    vc = v - jnp.mean(v, axis=-1, keepdims=True)
    vn = vc * lax.rsqrt(jnp.mean(vc * vc, axis=-1, keepdims=True) + NORM_EPS) * ln_g + ln_b
    causal = lax.broadcasted_iota(jnp.int32, (t, t), 0) >= lax.broadcasted_iota(jnp.int32, (t, t), 1)
    outs = []
    for g in range(SGU_GROUPS):
        sl = slice(g * LANES, (g + 1) * LANES)
        mixed = bdot(jnp.where(causal, w[g], 0.0), vn[:, sl], 1, 0) + b_t[:, g:g + 1]
        outs.append(u[:, sl] * mixed)
    return jnp.concatenate(outs, axis=1)


def _sgu_specs(s, ts):
    return [pl.BlockSpec((ts, 2 * MIX), lambda i: (i, SEC_A // (2 * MIX))),
            pl.BlockSpec((1, MIX), lambda i: (0, 0)), pl.BlockSpec((1, MIX), lambda i: (0, 0)),
            pl.BlockSpec((SGU_GROUPS, SGU_CHUNK, SGU_CHUNK), lambda i: (0, 0, 0)),
            pl.BlockSpec((SGU_CHUNK, SGU_GROUPS), lambda i: (0, 0))]


def sgu_fwd(proj, ln_g, ln_b, w, b_t, *, name):
    s = proj.shape[0]
    ts = _tile(s, 512)
    n_chunk = ts // SGU_CHUNK

    def body(p_ref, g_ref, b_ref, w_ref, bt_ref, o_ref):
        def step(c, carry):
            rows = pl.ds(pl.multiple_of(c * SGU_CHUNK, SGU_CHUNK), SGU_CHUNK)
            o_ref[rows, :] = _sgu_chunk(p_ref[rows, :].astype(f32), g_ref[...], b_ref[...], w_ref[...], bt_ref[...]).astype(bf16)
            return carry
        lax.fori_loop(0, n_chunk, step, 0)

    return pl.pallas_call(
        body, name=name, grid=(s // ts,), in_specs=_sgu_specs(s, ts),
        out_specs=pl.BlockSpec((ts, MIX), lambda i: (i, 0)), out_shape=jax.ShapeDtypeStruct((s, MIX), bf16),
        compiler_params=_params(1),
    )(proj, ln_g, ln_b, w, b_t)


def sgu_bwd(proj, ln_g, ln_b, w, b_t, d_out, dproj, *, name):
    s = proj.shape[0]
    ts = _tile(s, 512)
    n_chunk = ts // SGU_CHUNK

    def body(p_ref, g_ref, b_ref, w_ref, bt_ref, do_ref, _, dp_ref, dg_ref, db_ref, dw_ref, dbt_ref):
        @pl.when(pl.program_id(0) == 0)
        def _():
            dg_ref[...] = jnp.zeros_like(dg_ref)
            db_ref[...] = jnp.zeros_like(db_ref)
            dw_ref[...] = jnp.zeros_like(dw_ref)
            dbt_ref[...] = jnp.zeros_like(dbt_ref)

        def step(c, carry):
            rows = pl.ds(pl.multiple_of(c * SGU_CHUNK, SGU_CHUNK), SGU_CHUNK)
            _, vjp = jax.vjp(_sgu_chunk, p_ref[rows, :].astype(f32), g_ref[...], b_ref[...], w_ref[...], bt_ref[...])
            dp, dg, db, dw, dbt = vjp(do_ref[rows, :])
            dp_ref[rows, :] = dp.astype(bf16)
            dg_ref[...] += dg
            db_ref[...] += db
            dw_ref[...] += dw
            dbt_ref[...] += dbt
            return carry
        lax.fori_loop(0, n_chunk, step, 0)

    specs = _sgu_specs(s, ts)
    return pl.pallas_call(
        body, name=name, grid=(s // ts,),
        in_specs=specs + [pl.BlockSpec((ts, MIX), lambda i: (i, 0)), pl.BlockSpec(memory_space=pl.ANY)],
        out_specs=[specs[0], specs[1], specs[2], specs[3], specs[4]],
        out_shape=[jax.ShapeDtypeStruct(dproj.shape, bf16), jax.ShapeDtypeStruct((1, MIX), f32),
                   jax.ShapeDtypeStruct((1, MIX), f32), jax.ShapeDtypeStruct(w.shape, f32),
                   jax.ShapeDtypeStruct(b_t.shape, f32)],
        input_output_aliases={6: 0}, compiler_params=_params(1),
    )(proj, ln_g, ln_b, w, b_t, d_out, dproj)


def rope_tables(posf, inv_freq, *, name):
    s = posf.shape[0]
    ts = _tile(s, 1024, 8)
    half = ROPE_DIM // 2

    def body(pos_ref, inv_ref, o_ref):
        d = lax.broadcasted_iota(jnp.int32, (1, LANES), 1) % SWA_HD
        ang = pos_ref[...] * inv_ref[...]
        sin = jnp.sin(ang)
        o_ref[0] = jnp.cos(ang)
        o_ref[1] = jnp.where(d < half, sin, 0.0)
        o_ref[2] = jnp.where((d >= half) & (d < ROPE_DIM), sin, 0.0)

    return pl.pallas_call(
        body, name=name, grid=(s // ts,),
        in_specs=[pl.BlockSpec((ts, 1), lambda i: (i, 0)), pl.BlockSpec((1, LANES), lambda i: (0, 0))],
        out_specs=pl.BlockSpec((3, ts, LANES), lambda i: (0, i, 0)), out_shape=jax.ShapeDtypeStruct((3, s, LANES), f32),
        compiler_params=_params(1),
    )(posf, inv_freq)


def _rope(x, table):
    w = x.shape[1]
    half = ROPE_DIM // 2
    c, lo, hi = (jnp.concatenate([table[i]] * (w // LANES), axis=1) for i in range(3))
    return x * c - lroll(x, w - half) * lo + lroll(x, half) * hi


def _swa_block(q, kp, kc, vp, vc, sink_row, table_q, table_p, prev_ok, wide):
    t = WINDOW
    q = _rope(q, table_q) * (SWA_HD ** -0.5)
    keys = jnp.concatenate([_rope(kp, table_p), _rope(kc, table_q)], axis=0)
    vals = jnp.concatenate([vp, vc], axis=0)
    own = lax.broadcasted_iota(jnp.int32, (t, t), 0) >= lax.broadcasted_iota(jnp.int32, (t, t), 1)
    lane_half = lax.broadcasted_iota(jnp.int32, (t, LANES), 1) // SWA_HD
    group = SWA_HEADS // SWA_KV
    slabs = []
    for pair in range(SWA_HEADS // 2):
        q_pair = q[:, pair * LANES:(pair + 1) * LANES]
        acc = jnp.zeros((t, LANES), f32)
        for half in range(2):
            h = 2 * pair + half
            kv = h // group
            qm = jnp.where(lane_half == half, q_pair, 0.0)
            if half != kv:
                qm = lroll(qm, SWA_HD)
            if wide:
                both = bdot(qm, keys, 1, 1)
                s_prev, s_own = both[:, :t], both[:, t:]
            else:
                s_prev, s_own = bdot(qm, keys[:t], 1, 1), bdot(qm, keys[t:], 1, 1)
            logits = jnp.where(own, s_own, jnp.where(prev_ok, s_prev, -1e30))
            sink = _lane_col(sink_row, h)
            m = lax.stop_gradient(jnp.maximum(jnp.max(logits, axis=1, keepdims=True), sink))
            p = jnp.exp(logits - m)
            probs = p * (1.0 / (jnp.sum(p, axis=1, keepdims=True) + jnp.exp(sink - m)))
            p_prev, p_own = jnp.where(own, 0.0, probs), jnp.where(own, probs, 0.0)
            if wide:
                o = bdot(jnp.concatenate([p_prev, p_own], axis=1), vals, 1, 0)
            else:
                o = bdot(p_prev, vals[:t], 1, 0) + bdot(p_own, vals[t:], 1, 0)
            o = jnp.where(lane_half == kv, o, 0.0)
            if half != kv:
                o = lroll(o, SWA_HD)
            acc = acc + o
        slabs.append(acc)
    return jnp.concatenate(slabs, axis=1)


def _swa_in_specs(nc, clamp):
    t = WINDOW
    qb, kb, vb = SEC_B // MIX, (SEC_B + MIX) // LANES, (SEC_B + MIX + LANES) // LANES

    def cur(i):
        return jnp.minimum(i, nc - 1) if clamp else i

    def prev(i):
        return jnp.maximum(cur(i) - 1, 0)

    return [pl.BlockSpec((t, MIX), lambda i: (cur(i), qb)),
            pl.BlockSpec((t, LANES), lambda i: (prev(i), kb)), pl.BlockSpec((t, LANES), lambda i: (cur(i), kb)),
            pl.BlockSpec((t, LANES), lambda i: (prev(i), vb)), pl.BlockSpec((t, LANES), lambda i: (cur(i), vb)),
            pl.BlockSpec((1, LANES), lambda i: (0, 0)),
            pl.BlockSpec((3, t, LANES), lambda i: (0, cur(i), 0)), pl.BlockSpec((3, t, LANES), lambda i: (0, prev(i), 0))]


def swa_fwd(proj, sink_row, tables, *, name):
    s = proj.shape[0]
    nc = s // WINDOW

    def body(q_ref, kp_ref, kc_ref, vp_ref, vc_ref, sink_ref, tq_ref, tp_ref, o_ref):
        prev_ok = pl.program_id(0) > 0
        blocks = [r[...].astype(f32) for r in (q_ref, kp_ref, kc_ref, vp_ref, vc_ref)]
        o_ref[...] = _swa_block(*blocks, sink_ref[...],
                                tq_ref[...], tp_ref[...], prev_ok, True).astype(bf16)

    return pl.pallas_call(
        body, name=name, grid=(nc,), in_specs=_swa_in_specs(nc, False),
        out_specs=pl.BlockSpec((WINDOW, MIX), lambda i: (i, 0)), out_shape=jax.ShapeDtypeStruct((s, MIX), bf16),
        compiler_params=_params(1),
    )(proj, proj, proj, proj, proj, sink_row, tables, tables)


def swa_bwd(proj, sink_row, tables, d_out, dproj, *, side=None, name):
    s = proj.shape[0]
    nc = s // WINDOW
    t = WINDOW

    def body(q_ref, kp_ref, kc_ref, vp_ref, vc_ref, sink_ref, tq_ref, tp_ref, do_ref, _,
             dp_ref, dsink_ref, cq_ref, ck_ref, cv_ref):
        i = pl.program_id(0)

        @pl.when(i == 0)
        def _():
            dsink_ref[...] = jnp.zeros_like(dsink_ref)

        def write(dk_prev, dv_prev):
            dp_ref[:, :MIX] = cq_ref[...].astype(bf16)
            dp_ref[:, MIX:MIX + LANES] = (ck_ref[...] + dk_prev).astype(bf16)
            dp_ref[:, MIX + LANES:] = (cv_ref[...] + dv_prev).astype(bf16)

        @pl.when(i < nc)
        def _():
            fn = functools.partial(_swa_block, table_q=tq_ref[...], table_p=tp_ref[...], prev_ok=i > 0, wide=False)
            blocks = [r[...].astype(f32) for r in (q_ref, kp_ref, kc_ref, vp_ref, vc_ref)]
            _, vjp = jax.vjp(fn, *blocks, sink_ref[...])
            dq, dkp, dkc, dvp, dvc, dsink = vjp(do_ref[...])
            dsink_ref[...] += dsink

            @pl.when(i > 0)
            def _():
                write(dkp, dvp)

            cq_ref[...] = dq
            ck_ref[...] = dkc
            cv_ref[...] = dvc

        @pl.when(i == nc)
        def _():
            write(0.0, 0.0)

    return host_call(
        side, body, name=name, grid=(nc + 1,),
        in_specs=_swa_in_specs(nc, True) + [pl.BlockSpec((t, MIX), lambda i: (jnp.minimum(i, nc - 1), 0)),
                                            pl.BlockSpec(memory_space=pl.ANY)],
        out_specs=[pl.BlockSpec((t, MIX + 2 * LANES), lambda i: (jnp.maximum(i - 1, 0), SEC_B // (MIX + 2 * LANES))),
                   pl.BlockSpec((1, LANES), lambda i: (0, 0))],
        out_shape=[jax.ShapeDtypeStruct(dproj.shape, bf16), jax.ShapeDtypeStruct((1, LANES), f32)],
        scratch_shapes=[pltpu.VMEM((t, MIX), f32), pltpu.VMEM((t, LANES), f32), pltpu.VMEM((t, LANES), f32)],
        aliases={9: 0}, args=(proj, proj, proj, proj, proj, sink_row, tables, tables, d_out, dproj))


CONV_PAD = 16


def _conv_taps(xp, rows):
    off = CONV_PAD - (DN_CONV - 1)
    return [xp[off + i:off + i + rows] for i in range(DN_CONV)]


def _conv_pre(taps, w):
    pre = taps[0] * w[0:1]
    for i in range(1, DN_CONV):
        pre = pre + taps[i] * w[i:i + 1]
    return pre


def conv_fwd(proj, conv_w8, *, name):
    s = proj.shape[0]
    wq = 3 * MIX
    ts = _tile(s, 512)
    nb = ts // CONV_PAD

    def body(x_ref, prev_ref, w_ref, o_ref):
        prev = jnp.where(pl.program_id(0) > 0, prev_ref[...].astype(f32), 0.0)
        pre = _conv_pre(_conv_taps(jnp.concatenate([prev, x_ref[...].astype(f32)], axis=0), ts), w_ref[...])
        o_ref[...] = pre * _sigmoid(pre)

    return pl.pallas_call(
        body, name=name, grid=(s // ts,),
        in_specs=[pl.BlockSpec((ts, wq), lambda i: (i, SEC_QKV // wq)),
                  pl.BlockSpec((CONV_PAD, wq), lambda i: (jnp.maximum(i * nb - 1, 0), SEC_QKV // wq)),
                  pl.BlockSpec((CONV_PAD, wq), lambda i: (0, 0))],
        out_specs=pl.BlockSpec((ts, wq), lambda i: (i, 0)), out_shape=jax.ShapeDtypeStruct((s, wq), f32),
        compiler_params=_params(1),
    )(proj, proj, conv_w8)


def conv_bwd(proj, conv_w8, dxc, dproj, *, side=None, name):
    s = proj.shape[0]
    wq = 3 * MIX
    ts = _tile(s, 512)
    nb = ts // CONV_PAD
    nt = s // ts
    last_blk = s // CONV_PAD - 1

    def body(x_ref, prev_ref, next_ref, w_ref, d_ref, dnext_ref, _, dp_ref, dw_ref):
        i = pl.program_id(0)

        @pl.when(i == 0)
        def _():
            dw_ref[...] = jnp.zeros_like(dw_ref)

        w = w_ref[...]
        prev = jnp.where(i > 0, prev_ref[...].astype(f32), 0.0)
        more = i < nt - 1
        xp = jnp.concatenate([prev, x_ref[...].astype(f32), jnp.where(more, next_ref[...].astype(f32), 0.0)], axis=0)
        taps = _conv_taps(xp, ts + CONV_PAD)
        pre = _conv_pre(taps, w)
        sig = _sigmoid(pre)
        dxc_ext = jnp.concatenate([d_ref[...], jnp.where(more, dnext_ref[...], 0.0)], axis=0)
        dpre = dxc_ext * sig * (1.0 + pre * (1.0 - sig))
        d_raw = jnp.zeros((ts, wq), f32)
        dws = []
        for k in range(DN_CONV):
            shift = DN_CONV - 1 - k
            d_raw = d_raw + dpre[shift:shift + ts] * w[k:k + 1]
            dws.append(jnp.sum(dpre[:ts] * taps[k][:ts], axis=0, keepdims=True))
        dp_ref[...] = d_raw.astype(bf16)
        dw_ref[...] += jnp.concatenate(dws + [jnp.zeros((CONV_PAD - DN_CONV, wq), f32)], axis=0)

    sec = SEC_QKV // wq
    return host_call(
        side, body, name=name, grid=(nt,),
        in_specs=[pl.BlockSpec((ts, wq), lambda i: (i, sec)),
                  pl.BlockSpec((CONV_PAD, wq), lambda i: (jnp.maximum(i * nb - 1, 0), sec)),
                  pl.BlockSpec((CONV_PAD, wq), lambda i: (jnp.minimum((i + 1) * nb, last_blk), sec)),
                  pl.BlockSpec((CONV_PAD, wq), lambda i: (0, 0)),
                  pl.BlockSpec((ts, wq), lambda i: (i, 0)),
                  pl.BlockSpec((CONV_PAD, wq), lambda i: (jnp.minimum((i + 1) * nb, last_blk), 0)),
                  pl.BlockSpec(memory_space=pl.ANY)],
        out_specs=[pl.BlockSpec((ts, wq), lambda i: (i, sec)), pl.BlockSpec((CONV_PAD, wq), lambda i: (0, 0))],
        out_shape=[jax.ShapeDtypeStruct(dproj.shape, bf16), jax.ShapeDtypeStruct((CONV_PAD, wq), f32)],
        scratch_shapes=[], aliases={6: 0}, args=(proj, proj, proj, conv_w8, dxc, dxc, dproj))


def _dn_chunk(state, xc, z, ba, alog_row, dtb_row, norm_row, t_known):
    c, nh = DN_CHUNK, DN_HEADS
    n = c * nh
    row = lax.broadcasted_iota(jnp.int32, (n, n), 0)
    col = lax.broadcasted_iota(jnp.int32, (n, n), 1)
    same_head = (row // c) == (col // c)
    tril, strict = same_head & (row >= col), same_head & (row > col)
    tril_c = lax.broadcasted_iota(jnp.int32, (c, c), 0) >= lax.broadcasted_iota(jnp.int32, (c, c), 1)
    beta_all = _sigmoid(ba)
    g_all = -jnp.exp(alog_row) * jax.nn.softplus(ba + dtb_row)
    gc_all = hdot(tril_c.astype(f32), g_all)
    gc_t = gc_all.T

    def stack(piece):
        return jnp.concatenate([piece(h) for h in range(nh)], axis=0)

    q = stack(lambda h: xc[:, h * DN_HD:(h + 1) * DN_HD])
    k = stack(lambda h: xc[:, MIX + h * DN_HD:MIX + (h + 1) * DN_HD])
    v = stack(lambda h: xc[:, 2 * MIX + h * DN_HD:2 * MIX + (h + 1) * DN_HD])
    zs = stack(lambda h: z[:, h * DN_HD:(h + 1) * DN_HD])
    q = q * lax.rsqrt(jnp.sum(q * q, axis=-1, keepdims=True) + NORM_EPS) * (DN_HD ** -0.5)
    k = k * lax.rsqrt(jnp.sum(k * k, axis=-1, keepdims=True) + NORM_EPS)
    beta = stack(lambda h: _lane_col(beta_all, h))
    g_cols = [_lane_col(gc_all, nh + h) for h in range(nh)]
    g_col = jnp.concatenate(g_cols, axis=0)
    g_row = jnp.concatenate([gc_t[nh + h:nh + h + 1, :] for h in range(nh)], axis=1)
    g_last = stack(lambda h: jnp.broadcast_to(g_cols[h][c - 1:c, :], (c, 1)))
    decay = jnp.where(tril, jnp.exp(jnp.where(tril, g_col - g_row, 0.0)), 0.0)
    kb = k * beta
    low = jnp.where(strict, bdot(kb, k, 1, 1) * decay, 0.0)
    t_inv = tri_inv(low, c) if t_known is None else tri_inv_known(low, t_known)
    e_gc = jnp.exp(g_col)
    uw = bdot(t_inv, jnp.concatenate([v * beta, kb * e_gc], axis=1), 1, 0)
    u, w = uw[:, :DN_HD], uw[:, DN_HD:]
    attn = bdot(q, k, 1, 1) * decay
    own = (lax.broadcasted_iota(jnp.int32, (n, nh * DN_HD), 1) // DN_HD
           == lax.broadcasted_iota(jnp.int32, (n, nh * DN_HD), 0) // c)

    def spread(a):
        return jnp.where(own, jnp.concatenate([a] * nh, axis=1), 0.0)

    v_new = u - bdot(spread(w), state, 1, 0)
    o = bdot(spread(q * e_gc), state, 1, 0) + bdot(attn, v_new, 1, 0)
    keep = stack(lambda h: jnp.broadcast_to(jnp.exp(g_cols[h][c - 1:c, :]), (DN_HD, 1)))
    new_state = state * keep + bdot(spread(k * jnp.exp(g_last - g_col)), v_new, 0, 0)
    out = _rms(o, norm_row) * (zs * _sigmoid(zs))
    return new_state, jnp.concatenate([out[h * c:(h + 1) * c] for h in range(nh)], axis=1), t_inv


DN_STEP = 8 * DN_CHUNK


def _dn_step(state, xc, z, ba, alog_row, dtb_row, norm_row, t_known=None):
    outs, t_invs = [], []
    for c in range(DN_STEP // DN_CHUNK):
        rows = slice(c * DN_CHUNK, (c + 1) * DN_CHUNK)
        state, out, t_inv = _dn_chunk(state, xc[rows], z[rows], ba[rows], alog_row, dtb_row, norm_row,
                                      None if t_known is None else t_known[c])
        outs.append(out)
        t_invs.append(t_inv)
    return state, jnp.concatenate(outs, axis=0), jnp.stack(t_invs)


def _dn_specs(ts, order):
    zb = SEC_Z // MIX
    return [pl.BlockSpec((ts, 3 * MIX), lambda i: (order(i), 0)),
            pl.BlockSpec((ts, MIX), lambda i: (order(i), zb)),
            pl.BlockSpec((ts, LANES), lambda i: (order(i), 0)),
            pl.BlockSpec((1, LANES), lambda i: (0, 0)), pl.BlockSpec((1, LANES), lambda i: (0, 0)),
            pl.BlockSpec((1, LANES), lambda i: (0, 0))]


def dn_fwd(xc, proj, p_ba, alog_row, dtb_row, norm_row, *, side=None, name):
    s = xc.shape[0]
    ts = _tile(s, 512)
    n_step = ts // DN_STEP

    per_step = DN_STEP // DN_CHUNK
    n_tri = DN_HEADS * DN_CHUNK

    def body(xc_ref, z_ref, ba_ref, al_ref, dt_ref, nr_ref, o_ref, st_ref, tri_ref, state_ref):
        @pl.when(pl.program_id(0) == 0)
        def _():
            state_ref[...] = jnp.zeros_like(state_ref)

        def step(c, carry):
            rows = pl.ds(pl.multiple_of(c * DN_STEP, DN_STEP), DN_STEP)
            st_ref[c] = state_ref[...]
            new_state, out, t_invs = _dn_step(state_ref[...], xc_ref[rows, :], z_ref[rows, :].astype(f32), ba_ref[rows, :],
                                              al_ref[...], dt_ref[...], nr_ref[...])
            state_ref[...] = new_state
            o_ref[rows, :] = out.astype(bf16)
            tri_ref[pl.ds(c * per_step, per_step)] = t_invs
            return carry
        lax.fori_loop(0, n_step, step, 0)

    return host_call(
        side, body, name=name, grid=(s // ts,), in_specs=_dn_specs(ts, lambda i: i),
        out_specs=[pl.BlockSpec((ts, MIX), lambda i: (i, 0)),
                   pl.BlockSpec((n_step, DN_HEADS * DN_HD, DN_HD), lambda i: (i, 0, 0)),
                   pl.BlockSpec((n_step * per_step, n_tri, n_tri), lambda i: (i, 0, 0))],
        out_shape=[jax.ShapeDtypeStruct((s, MIX), bf16),
                   jax.ShapeDtypeStruct((s // DN_STEP, DN_HEADS * DN_HD, DN_HD), f32),
                   jax.ShapeDtypeStruct((s // DN_CHUNK, n_tri, n_tri), f32)],
        scratch_shapes=[pltpu.VMEM((DN_HEADS * DN_HD, DN_HD), f32)], aliases={},
        args=(xc, proj, p_ba, alog_row, dtb_row, norm_row))


def dn_bwd(xc, proj, p_ba, alog_row, dtb_row, norm_row, saved, tri, d_out, dproj, *, side=None, name):
    s = xc.shape[0]
    ts = _tile(s, 512)
    n_step = ts // DN_STEP
    nt = s // ts

    per_step = DN_STEP // DN_CHUNK
    n_tri = DN_HEADS * DN_CHUNK

    def body(xc_ref, z_ref, ba_ref, al_ref, dt_ref, nr_ref, st_ref, tri_ref, do_ref, _,
             dz_ref, dxc_ref, dba_ref, dal_ref, ddt_ref, dnr_ref, dstate_ref):
        @pl.when(pl.program_id(0) == 0)
        def _():
            dstate_ref[...] = jnp.zeros_like(dstate_ref)
            dal_ref[...] = jnp.zeros_like(dal_ref)
            ddt_ref[...] = jnp.zeros_like(ddt_ref)
            dnr_ref[...] = jnp.zeros_like(dnr_ref)

        def step(it, carry):
            c = n_step - 1 - it
            rows = pl.ds(pl.multiple_of(c * DN_STEP, DN_STEP), DN_STEP)
            t_known = tri_ref[pl.ds(c * per_step, per_step)]
            _, vjp = jax.vjp(lambda *a: _dn_step(*a, t_known=t_known)[:2], st_ref[c], xc_ref[rows, :],
                             z_ref[rows, :].astype(f32), ba_ref[rows, :], al_ref[...], dt_ref[...], nr_ref[...])
            d_in, dxc, dz, dba, dal, ddt, dnr = vjp((dstate_ref[...], do_ref[rows, :]))
            dstate_ref[...] = d_in
            dxc_ref[rows, :] = dxc
            dz_ref[rows, :] = dz.astype(bf16)
            dba_ref[rows, :] = dba.astype(bf16)
            dal_ref[...] += dal
            ddt_ref[...] += ddt
            dnr_ref[...] += dnr
            return carry
        lax.fori_loop(0, n_step, step, 0)

    def rev(i):
        return nt - 1 - i

    specs = _dn_specs(ts, rev)
    vec = pl.BlockSpec((1, LANES), lambda i: (0, 0))
    return host_call(
        side, body, name=name, grid=(nt,),
        in_specs=specs + [pl.BlockSpec((n_step, DN_HEADS * DN_HD, DN_HD), lambda i: (rev(i), 0, 0)),
                          pl.BlockSpec((n_step * per_step, n_tri, n_tri), lambda i: (rev(i), 0, 0)),
                          pl.BlockSpec((ts, MIX), lambda i: (rev(i), 0)), pl.BlockSpec(memory_space=pl.ANY)],
        out_specs=[specs[1], specs[0], specs[2], vec, vec, vec],
        out_shape=[jax.ShapeDtypeStruct(dproj.shape, bf16), jax.ShapeDtypeStruct((s, 3 * MIX), f32),
                   jax.ShapeDtypeStruct((s, LANES), bf16)] + [jax.ShapeDtypeStruct((1, LANES), f32)] * 3,
        scratch_shapes=[pltpu.VMEM((DN_HEADS * DN_HD, DN_HD), f32)], aliases={9: 0},
        args=(xc, proj, p_ba, alog_row, dtb_row, norm_row, saved, tri, d_out, dproj))


def _merge_in_specs(ts, d):
    row = pl.BlockSpec((ts, MIX), lambda i: (i, 0))
    return [row, row, row, pl.BlockSpec((ts, 3 * d), lambda i: (i, SEC_G // (3 * d))),
            pl.BlockSpec((3, MIX, d), lambda i: (0, 0, 0))]


def merge_fwd(out_a, out_b, out_c, proj, w_branch, *, name):
    s, d = out_a.shape[0], w_branch.shape[2]
    ts = _tile(s, 512, 16)

    def body(a_ref, b_ref, c_ref, g_ref, w_ref, o_ref):
        acc = jnp.zeros((ts, d), f32)
        for n, r in enumerate((a_ref, b_ref, c_ref)):
            acc = acc + _sigmoid(g_ref[:, n * d:(n + 1) * d].astype(f32)) * _dg(r[...], w_ref[n], 1, 0)
        o_ref[...] = acc.astype(bf16)

    return pl.pallas_call(
        body, name=name, grid=(s // ts,), in_specs=_merge_in_specs(ts, d),
        out_specs=pl.BlockSpec((ts, d), lambda i: (i, 0)), out_shape=jax.ShapeDtypeStruct((s, d), bf16),
        compiler_params=_params(1),
    )(out_a, out_b, out_c, proj, w_branch)


def merge_bwd(out_a, out_b, out_c, proj, w_branch, d_merged, dproj, *, name):
    s, d = out_a.shape[0], w_branch.shape[2]
    ts = _tile(s, 512, 16)

    def body(a_ref, b_ref, c_ref, g_ref, w_ref, dm_ref, _, dg_ref, da_ref, db_ref, dc_ref, dw_ref):
        @pl.when(pl.program_id(0) == 0)
        def _():
            dw_ref[...] = jnp.zeros_like(dw_ref)

        dm = dm_ref[...]
        for n, (r, dr) in enumerate(((a_ref, da_ref), (b_ref, db_ref), (c_ref, dc_ref))):
            gate = _sigmoid(g_ref[:, n * d:(n + 1) * d].astype(f32))
            branch = _dg(r[...], w_ref[n], 1, 0)
            dg_ref[:, n * d:(n + 1) * d] = (dm * branch * gate * (1.0 - gate)).astype(bf16)
            d_branch = dm * gate
            dr[...] = _dg(d_branch, w_ref[n], 1, 1)
            dw_ref[n] += _dg(r[...], d_branch, 0, 0)

    specs = _merge_in_specs(ts, d)
    row_f = pl.BlockSpec((ts, MIX), lambda i: (i, 0))
    return pl.pallas_call(
        body, name=name, grid=(s // ts,),
        in_specs=specs + [pl.BlockSpec((ts, d), lambda i: (i, 0)), pl.BlockSpec(memory_space=pl.ANY)],
        out_specs=[specs[3], row_f, row_f, row_f, specs[4]],
        out_shape=[jax.ShapeDtypeStruct(dproj.shape, bf16)] + [jax.ShapeDtypeStruct((s, MIX), f32)] * 3
        + [jax.ShapeDtypeStruct(w_branch.shape, f32)],
        input_output_aliases={6: 0}, compiler_params=_params(1),
    )(out_a, out_b, out_c, proj, w_branch, d_merged, dproj)


def gate_up_swiglu(h, w_gu, *, name):
    g2, d, w = w_gu.shape
    ng, s = g2 // 2, h.shape[0]
    tm = _tile(s, 1024, 16)

    def body(h_ref, wg_ref, wu_ref, g_ref, u_ref, a_ref):
        g16 = _dg(h_ref[...], wg_ref[...], 1, 0).astype(bf16)
        u16 = _dg(h_ref[...], wu_ref[...], 1, 0).astype(bf16)
        g_ref[...], u_ref[...] = g16, u16
        g = g16.astype(f32)
        a_ref[...] = (g * _sigmoid(g) * u16.astype(f32)).astype(bf16)

    out = pl.BlockSpec((None, tm, w), lambda i, j: (j, i, 0))
    return pl.pallas_call(
        body, name=name, grid=(s // tm, ng),
        in_specs=[pl.BlockSpec((tm, d), lambda i, j: (i, 0)), pl.BlockSpec((None, d, w), lambda i, j: (j, 0, 0)),
                  pl.BlockSpec((None, d, w), lambda i, j: (ng + j, 0, 0))],
        out_specs=[out, out, out], out_shape=[jax.ShapeDtypeStruct((ng, s, w), bf16)] * 3, compiler_params=_params(2),
    )(h, w_gu, w_gu)


def down_dx_swiglu(dx, w_down, gate, up, *, name):
    ng, w, d = w_down.shape
    s = dx.shape[0]
    tm = _tile(s, 1024, 16)

    def body(dx_ref, wd_ref, g_ref, u_ref, dg_ref, du_ref):
        d_act = _dg(dx_ref[...], wd_ref[...], 1, 1)
        g = g_ref[...].astype(f32)
        sig = _sigmoid(g)
        dg_ref[...] = (d_act * u_ref[...].astype(f32) * sig * (1.0 + g * (1.0 - sig))).astype(bf16)
        du_ref[...] = (d_act * g * sig).astype(bf16)

    blk = pl.BlockSpec((None, tm, w), lambda i, j: (j, i, 0))
    return pl.pallas_call(
        body, name=name, grid=(s // tm, ng),
        in_specs=[pl.BlockSpec((tm, d), lambda i, j: (i, 0)), pl.BlockSpec((None, w, d), lambda i, j: (j, 0, 0)), blk, blk],
        out_specs=[blk, blk], out_shape=[jax.ShapeDtypeStruct((ng, s, w), bf16)] * 2, compiler_params=_params(2),
    )(dx, w_down, gate, up)


def adamw(w, m, v, g_parts, *, name):
    n_layers = len(g_parts)
    n_parts, r, cols = g_parts[0].shape
    lanes = -(-cols // LANES) * LANES
    tr = _tile(r, max(16, (128 * 1024) // lanes), 16)
    nr = r // tr

    def body(w_ref, m_ref, v_ref, *rest):
        gp_refs, (g_ref, d_ref, nm_ref, nv_ref) = rest[:n_layers], rest[n_layers:]
        layer = pl.program_id(0)
        g = jnp.zeros((tr, cols), f32)
        for l, gp_ref in enumerate(gp_refs):
            g_l = gp_ref[0].astype(f32)
            for k in range(1, n_parts):
                g_l = g_l + gp_ref[k].astype(f32)
            g = jnp.where(layer == l, g_l, g)
        nm = ADAM_B1 * m_ref[...] + (1.0 - ADAM_B1) * g
        nv = ADAM_B2 * v_ref[...] + (1.0 - ADAM_B2) * jnp.square(g)
        m_hat = nm / (1.0 - ADAM_B1 ** ADAM_STEP)
        v_hat = nv / (1.0 - ADAM_B2 ** ADAM_STEP)
        g_ref[...] = g
        d_ref[...] = -ADAM_LR * (m_hat / (jnp.sqrt(v_hat) + ADAM_EPS) + ADAM_WD * w_ref[...])
        nm_ref[...] = nm
        nv_ref[...] = nv

    row = pl.BlockSpec((tr, cols), lambda l, i: (l * nr + i, 0))

    def parts_spec(own):
        return pl.BlockSpec((n_parts, tr, cols),
                            lambda l, i: (0, jnp.where(l == own, i, jnp.where(l < own, 0, nr - 1)), 0))

    return pl.pallas_call(
        body, name=name, grid=(n_layers, nr), in_specs=[row, row, row] + [parts_spec(l) for l in range(n_layers)],
        out_specs=[row] * 4, out_shape=[jax.ShapeDtypeStruct(w.shape, f32)] * 4, compiler_params=_params(2),
    )(w, m, v, *g_parts)


def _mesh_pos():
    return lax.axis_index("x"), lax.axis_index("y"), lax.axis_index("c")


def _dev_index(p):
    return 4 * p[0] + 2 * p[1] + p[2]


class Exchange:
    def __init__(self, kind, arrays):
        self.kind, self.arrays, self.n = kind, list(arrays), len(arrays)
        self.specs = [pl.BlockSpec(memory_space=pl.ANY)] * self.n
        self.out_shapes = [jax.ShapeDtypeStruct(((N_DEV,) if kind == "gather" else ()) + a.shape, a.dtype)
                           for a in self.arrays]
        self.scratch = [pltpu.SemaphoreType.DMA((self.n, N_DEV - 1)), pltpu.SemaphoreType.DMA((self.n, N_DEV - 1)),
                        pltpu.SemaphoreType.DMA((self.n,))]

    def _copies(self, in_refs, out_refs, sems, with_arrivals):
        send_sems, recv_sems, local_sems = sems
        x, y, c = _mesh_pos()
        mine = _dev_index((x, y, c))

        def src(a, slab):
            return in_refs[a] if self.kind == "gather" else in_refs[a].at[slab]

        local = [pltpu.make_async_copy(src(a, mine), out_refs[a].at[mine], local_sems.at[a]) for a in range(self.n)]
        sends, arrivals = [], []
        for k in range(1, N_DEV):
            peer = (1 - x if k & 4 else x, 1 - y if k & 2 else y, 1 - c if k & 1 else c)
            theirs = _dev_index(peer)
            for a in range(self.n):
                to = dict(send_sem=send_sems.at[a, k - 1], recv_sem=recv_sems.at[a, k - 1], device_id=peer,
                          device_id_type=pl.DeviceIdType.MESH)
                sends.append(pltpu.make_async_remote_copy(src_ref=src(a, theirs), dst_ref=out_refs[a].at[mine], **to))
                if with_arrivals:
                    arrivals.append(pltpu.make_async_remote_copy(src_ref=src(a, theirs),
                                                                 dst_ref=out_refs[a].at[theirs], **to))
        return local, sends, arrivals

    def start(self, in_refs, out_refs, sems):
        local, sends, _ = self._copies(in_refs, out_refs, sems, False)
        for cp in local + sends:
            cp.start()

    def wait(self, in_refs, out_refs, sems):
        local, sends, arrivals = self._copies(in_refs, out_refs, sems, True)
        for cp in arrivals:
            cp.wait_recv()
        for cp in sends:
            cp.wait_send()
        for cp in local:
            cp.wait()

    def run_around(self, grid, in_refs, out_refs, sems, *, before):
        at = None
        for axis, size in enumerate(grid):
            hit = pl.program_id(axis) == (0 if before else size - 1)
            at = hit if at is None else at & hit

        @pl.when(at)
        def _():
            (self.start if before else self.wait)(in_refs, out_refs, sems)


def host_call(side, body, *, name, grid, in_specs, out_specs, out_shape, scratch_shapes, args, aliases):
    n_in, n_out = len(in_specs), len(out_specs)
    if side is None:
        kernel_body = body
    else:
        n = side.n
        in_specs, args = in_specs + side.specs, tuple(args) + tuple(side.arrays)
        out_specs, out_shape = out_specs + side.specs, out_shape + side.out_shapes
        scratch_shapes = scratch_shapes + side.scratch

        def kernel_body(*refs):
            ins, side_in = refs[:n_in], refs[n_in:n_in + n]
            outs, side_out = refs[n_in + n:n_in + n + n_out], refs[n_in + n + n_out:n_in + 2 * n + n_out]
            scratch, sems = refs[n_in + 2 * n + n_out:-3], refs[-3:]
            side.run_around(grid, side_in, side_out, sems, before=True)
            body(*ins, *outs, *scratch)
            side.run_around(grid, side_in, side_out, sems, before=False)

    outs = pl.pallas_call(
        kernel_body, name=name, grid=grid, in_specs=in_specs, out_specs=out_specs, out_shape=out_shape,
        scratch_shapes=scratch_shapes, input_output_aliases=aliases, compiler_params=_params(len(grid)),
    )(*args)
    return outs if side is None else (outs[:n_out], outs[n_out:])


def all_gather(blocks, *, name):
    n = len(blocks)
    any_spec = pl.BlockSpec(memory_space=pl.ANY)

    def body(*refs):
        ins, outs = refs[:n], refs[n:2 * n]
        send_sems, recv_sems, local_sems = refs[2 * n:]
        x, y, c = _mesh_pos()
        me, sibling = (x, y, c), (x, y, 1 - c)
        chips = [(1 - x, y), (x, 1 - y), (1 - x, 1 - y)]

        def copy(a, k, block, to, src=None):
            dst = outs[a].at[_dev_index(block)]
            return pltpu.make_async_remote_copy(
                src_ref=dst if src is None else src, dst_ref=dst, send_sem=send_sems.at[a, k],
                recv_sem=recv_sems.at[a, k], device_id=to, device_id_type=pl.DeviceIdType.MESH)

        mine = [pltpu.make_async_copy(ins[a], outs[a].at[_dev_index(me)], local_sems.at[a]) for a in range(n)]
        for cp in mine:
            cp.start()
        first = []
        for a in range(n):
            first.append(copy(a, 0, me, sibling, src=ins[a]))
            first += [copy(a, 1 + j, me, (*chip, c), src=ins[a]) for j, chip in enumerate(chips)]
        for cp in first:
            cp.start()
        passed = []
        for j, chip in enumerate(chips):
            for a in range(n):
                copy(a, 1 + j, (*chip, c), me).wait_recv()
                fwd = copy(a, 4 + j, (*chip, c), sibling)
                fwd.start()
                passed.append(fwd)
        for a in range(n):
            copy(a, 0, sibling, me).wait_recv()
            for j, chip in enumerate(chips):
                copy(a, 4 + j, (*chip, 1 - c), me).wait_recv()
        for cp in first + passed:
            cp.wait_send()
        for cp in mine:
            cp.wait()

    return pl.pallas_call(
        body, name=name, in_specs=[any_spec] * n, out_specs=[any_spec] * n,
        out_shape=[jax.ShapeDtypeStruct((N_DEV,) + b.shape, b.dtype) for b in blocks],
        scratch_shapes=[pltpu.SemaphoreType.DMA((n, 7)), pltpu.SemaphoreType.DMA((n, 7)),
                        pltpu.SemaphoreType.DMA((n,))],
    )(*blocks)


def exchange(jobs, *, name):
    total = sum(j.n for j in jobs)

    def body(*refs):
        ins, outs, sems = refs[:total], refs[total:2 * total], refs[2 * total:]
        pieces, off = [], 0
        for i, j in enumerate(jobs):
            pieces.append((ins[off:off + j.n], outs[off:off + j.n], sems[3 * i:3 * i + 3]))
            off += j.n
        for j, piece in zip(jobs, pieces):
            j.start(*piece)
        for j, piece in zip(jobs, pieces):
            j.wait(*piece)

    outs = pl.pallas_call(
        body, name=name, in_specs=[s for j in jobs for s in j.specs], out_specs=[s for j in jobs for s in j.specs],
        out_shape=[s for j in jobs for s in j.out_shapes], scratch_shapes=[s for j in jobs for s in j.scratch],
    )(*[a for j in jobs for a in j.arrays])
    split, off = [], 0
    for j in jobs:
        split.append(outs[off:off + j.n])
        off += j.n
    return split


def _rows128(arr):
    flat = arr.reshape(-1)
    rows = -(-flat.shape[0] // (8 * LANES)) * 8
    return jnp.pad(flat, (0, rows * LANES - flat.shape[0])).reshape(rows, LANES)


def _pad_lanes(row, width=LANES, at=0):
    return jnp.pad(row, (at, width - at - row.shape[0])).reshape(1, width)


def _w_in_sections(got):
    d = got.shape[1]
    wi = jnp.transpose(got, (1, 0, 2)).reshape(d, -1)
    w_main = jnp.concatenate([wi[:, :C_B], wi[:, C_Z:C_BA], wi[:, C_QKV:C_Z], wi[:, C_G:], wi[:, C_B:C_QKV]], axis=1)
    return w_main, jnp.pad(wi[:, C_BA:C_G], ((0, 0), (0, LANES - (C_G - C_BA))))


def _w_in_parts(gw_main, gw_ba):
    d = gw_main.shape[0]
    full = jnp.concatenate([gw_main[:, SEC_A:SEC_Z], gw_main[:, SEC_B:], gw_main[:, SEC_QKV:SEC_G],
                            gw_main[:, SEC_Z:SEC_QKV], gw_ba[:, :C_G - C_BA], gw_main[:, SEC_G:SEC_B]], axis=1)
    return jnp.transpose(full.reshape(d, N_DEV, -1), (1, 0, 2))


def kernel(x, positions, attn_norm, w_in, sgu_ln_g, sgu_ln_b, sgu_w, sgu_b, attn_sinks, dn_conv_w, dn_a_log, dn_dt_bias, dn_norm, w_branch, w_out, ffn_norm, w_gate_up, w_down, final_norm, loss_target, m_attn_norm, m_w_in, m_sgu_ln_g, m_sgu_ln_b, m_sgu_w, m_sgu_b, m_attn_sinks, m_dn_conv_w, m_dn_a_log, m_dn_dt_bias, m_dn_norm, m_w_branch, m_w_out, m_ffn_norm, m_w_gate_up, m_w_down, m_final_norm, v_attn_norm, v_w_in, v_sgu_ln_g, v_sgu_ln_b, v_sgu_w, v_sgu_b, v_attn_sinks, v_dn_conv_w, v_dn_a_log, v_dn_dt_bias, v_dn_norm, v_w_branch, v_w_out, v_ffn_norm, v_w_gate_up, v_w_down, v_final_norm):
    given = dict(locals())
    depth, d_model = attn_norm.shape
    s = x.shape[1]
    x2 = x.reshape(s, d_model)
    target = loss_target.reshape(s, d_model)
    posf = positions.reshape(s, 1).astype(f32)
    inv_freq = ROPE_THETA ** (-jnp.arange(0, ROPE_DIM, 2, dtype=f32) / ROPE_DIM)
    inv_head = jnp.concatenate([inv_freq, inv_freq, jnp.zeros((SWA_HD - ROPE_DIM,), f32)])
    tables = rope_tables(posf, jnp.tile(inv_head, LANES // SWA_HD).reshape(1, LANES), name="rope_tables")

    assert depth == 2, depth
    gathered = dict(zip([("w_in", 0), ("dn_conv_w", 0), ("dn_conv_w", 1)], all_gather(
        [w_in[0].astype(bf16), dn_conv_w[0], dn_conv_w[1]], name="gather_first")))
    riders = {"l0_in_proj": [("w_branch", 0), ("w_out", 0), ("w_down", 0)],
              "l0_deltanet": [("w_gate_up", 0), ("w_in", 1)],
              "l1_deltanet": [("w_branch", 1), ("w_out", 1), ("w_gate_up", 1), ("w_down", 1)]}

    def gathering(host, call, *args, **kw):
        if host not in riders:
            return call(*args, name=host, **kw)
        side = Exchange("gather", [given[n][l].astype(bf16) for n, l in riders[host]])
        out, got = call(*args, side=side, name=host, **kw)
        gathered.update(zip(riders[host], got))
        return out

    layers, saved = [], []
    h_in = x2
    for l in range(depth):
        t = f"l{l}_"
        w_main, w_ba = _w_in_sections(gathered["w_in", l])
        conv_full = jnp.transpose(gathered["dn_conv_w", l], (1, 0, 2)).reshape(DN_CONV, -1)
        p = dict(
            w_main=w_main, w_ba=w_ba, conv_w8=jnp.pad(conv_full, ((0, CONV_PAD - DN_CONV), (0, 0))),
            attn_norm=attn_norm[l].reshape(1, -1), ffn_norm=ffn_norm[l].reshape(1, -1),
            ln_g=sgu_ln_g[l].reshape(1, -1), ln_b=sgu_ln_b[l].reshape(1, -1), sgu_w=sgu_w[l], sgu_bt=sgu_b[l].T,
            sink_row=_pad_lanes(attn_sinks[l]), alog_row=_pad_lanes(dn_a_log[l], at=DN_HEADS),
            dtb_row=_pad_lanes(dn_dt_bias[l], at=DN_HEADS), norm_row=dn_norm[l].reshape(1, -1))
        layers.append(p)
        h = rmsnorm_fwd(h_in, p["attn_norm"], name=t + "attn_norm")
        proj = gathering(t + "in_proj", matmul, h, p["w_main"], "nn", bf16)
        p_ba = matmul(h, p["w_ba"], "nn", f32, name=t + "in_proj_ba")
        out_a = sgu_fwd(proj, p["ln_g"], p["ln_b"], p["sgu_w"], p["sgu_bt"], name=t + "sgu")
        out_b = swa_fwd(proj, p["sink_row"], tables, name=t + "swa")
        xc = conv_fwd(proj, p["conv_w8"], name=t + "dn_conv")
        out_c, states, tri = gathering(t + "deltanet", dn_fwd, xc, proj, p_ba, p["alog_row"], p["dtb_row"], p["norm_row"])
        p.update(w_branch=jnp.transpose(gathered["w_branch", l], (1, 2, 0, 3)).reshape(3, MIX, d_model),
                 w_out=gathered["w_out", l].reshape(d_model, d_model),
                 w_gu=gathered["w_gate_up", l],
                 w_down=gathered["w_down", l].reshape(N_DEV // 2, -1, d_model))
        merged = merge_fwd(out_a, out_b, out_c, proj, p["w_branch"], name=t + "merge")
        x_mid = matmul(merged, p["w_out"], "nn", f32, residual=h_in, tn=1024, name=t + "out_proj")
        h2 = rmsnorm_fwd(x_mid, p["ffn_norm"], name=t + "ffn_norm")
        gate, up, act = gate_up_swiglu(h2, p["w_gu"], name=t + "gate_up")
        x_out = matmul(act, p["w_down"], "nn", f32, residual=x_mid, group="k", tn=1024, name=t + "down")
        saved.append(dict(x_in=h_in, h=h, proj=proj, p_ba=p_ba, out_a=out_a, out_b=out_b, out_c=out_c, xc=xc,
                          states=states, tri=tri, merged=merged, x_mid=x_mid, h2=h2, gate=gate, up=up, act=act))
        h_in = x_out

    dx, d_final_norm, loss_row = loss_head(h_in, final_norm.reshape(1, -1), target, name="loss_head")
    loss = lax.psum(loss_row[0, 0], MESH_AXES)

    shard_names = ["w_in", "dn_conv_w", "w_branch", "w_out", "w_gate_up", "w_down"]
    rep_names = ["attn_norm", "sgu_ln_g", "sgu_ln_b", "sgu_w", "sgu_b", "attn_sinks", "dn_a_log", "dn_dt_bias",
                 "dn_norm", "ffn_norm"]
    parts, received, per_layer = {}, {}, []
    senders = {"l0_b_swa": [("w_gate_up", 1), ("w_down", 1), ("w_out", 1), ("w_branch", 1)],
               "l0_b_deltanet": [("w_in", 1), ("w_gate_up", 0)],
               "l0_b_dn_conv": [("w_out", 0), ("w_branch", 0), ("w_down", 0)],
               "l0_b_in_proj_dx": [("w_in", 0)]}

    def scattering(host, call, *args, **kw):
        if host not in senders:
            return call(*args, name=host, **kw)
        out, got = call(*args, side=Exchange("scatter", [parts[key] for key in senders[host]]), name=host, **kw)
        received.update(zip(senders[host], got))
        return out

    for l in reversed(range(depth)):
        p, sv, t = layers[l], saved[l], f"l{l}_b_"
        d_gate, d_up = down_dx_swiglu(dx, p["w_down"], sv["gate"], sv["up"], name=t + "down_dx")
        gw_down = matmul(sv["act"], dx, "tn", bf16, group="m", tk=2048, tn=512, name=t + "down_dw")
        gw_gu = jnp.concatenate([matmul(sv["h2"], d_half, "tn", bf16, group="n", tk=2048, name=t + "gate_up_dw" + tag)
                                 for d_half, tag in ((d_gate, "_gate"), (d_up, "_up"))], axis=0)
        half = N_DEV // 2
        d_h2 = matmul(d_gate, p["w_gu"][:half], "nt", f32, group="k", tn=1024, name=t + "gate_up_dx_gate")
        d_h2 = matmul(d_up, p["w_gu"][half:], "nt", f32, group="k", tn=1024, residual=d_h2, name=t + "gate_up_dx_up")
        dx_mid, g_ffn = rmsnorm_bwd(sv["x_mid"], p["ffn_norm"], d_h2, dx, name=t + "ffn_norm")
        d_merged = matmul(dx_mid, p["w_out"], "nt", f32, tn=1024, name=t + "out_proj_dx")
        gw_out = matmul(sv["merged"], dx_mid, "tn", bf16, tk=2048, tn=1024, name=t + "out_proj_dw")
        dproj = lax.empty((s, W_MAIN), bf16)
        dproj, d_a, d_b, d_c, gw_branch = merge_bwd(sv["out_a"], sv["out_b"], sv["out_c"], sv["proj"], p["w_branch"],
                                                   d_merged, dproj, name=t + "merge")
        parts.update({("w_gate_up", l): gw_gu, ("w_down", l): gw_down.reshape(N_DEV, -1, d_model),
                      ("w_out", l): gw_out.reshape(N_DEV, -1, d_model),
                      ("w_branch", l): jnp.transpose(gw_branch.reshape(3, MIX, N_DEV, -1), (2, 0, 1, 3)).astype(bf16)})
        dproj, g_ln_g, g_ln_b, g_sgu_w, g_sgu_bt = sgu_bwd(sv["proj"], p["ln_g"], p["ln_b"], p["sgu_w"], p["sgu_bt"],
                                                         d_a, dproj, name=t + "sgu")
        dproj, g_sink = scattering(t + "swa", swa_bwd, sv["proj"], p["sink_row"], tables, d_b, dproj)
        dproj, dxc, dba, g_alog, g_dtb, g_dnorm = scattering(
            t + "deltanet", dn_bwd, sv["xc"], sv["proj"], sv["p_ba"], p["alog_row"], p["dtb_row"], p["norm_row"],
            sv["states"], sv["tri"], d_c, dproj)
        dproj, g_conv8 = scattering(t + "dn_conv", conv_bwd, sv["proj"], p["conv_w8"], dxc, dproj)
        gw_main = matmul(sv["h"], dproj, "tn", bf16, tk=2048, name=t + "in_proj_dw")
        gw_ba = matmul(sv["h"], dba, "tn", bf16, tk=2048, name=t + "in_proj_ba_dw")
        parts["w_in", l] = _w_in_parts(gw_main, gw_ba)
        d_h = scattering(t + "in_proj_dx", matmul, dproj, p["w_main"], "nt", f32, tk=2304, tn=1024)
        d_h = matmul(dba, p["w_ba"], "nt", f32, residual=d_h, name=t + "in_proj_ba_dx")
        dx, g_attn = rmsnorm_bwd(sv["x_in"], p["attn_norm"], d_h, dx_mid, name=t + "attn_norm")
        per_layer.append(dict(
            dn_conv_w=jnp.transpose(g_conv8[:DN_CONV].reshape(DN_CONV, N_DEV, -1), (1, 0, 2)),
            attn_norm=g_attn, sgu_ln_g=g_ln_g, sgu_ln_b=g_ln_b, sgu_w=g_sgu_w, sgu_b=g_sgu_bt.T,
            attn_sinks=g_sink[0, :SWA_HEADS], dn_a_log=g_alog[0, DN_HEADS:2 * DN_HEADS],
            dn_dt_bias=g_dtb[0, DN_HEADS:2 * DN_HEADS], dn_norm=g_dnorm, ffn_norm=g_ffn))
    per_layer.reverse()

    conv_parts = jnp.concatenate([pp["dn_conv_w"] for pp in per_layer], axis=1)
    rep_grads = {n: jnp.stack([pp[n].reshape(given[n].shape[1:]) for pp in per_layer]) for n in rep_names}
    rep_grads["final_norm"] = d_final_norm[0]
    rep_names = rep_names + ["final_norm"]
    rep_rows = [_rows128(given[n]).shape[0] for n in rep_names]
    pad_rows = -sum(rep_rows) % 16

    def small_rows(values):
        pieces = [_rows128(values[n]) for n in rep_names]
        return jnp.concatenate(pieces + [jnp.zeros((pad_rows, LANES), f32)], axis=0)

    (got_conv,), (small_all,) = exchange(
        [Exchange("scatter", [conv_parts]), Exchange("gather", [small_rows(rep_grads)])], name="exchange_last")

    results = [{}, {}, {}, {}]
    for n in shard_names:
        shp = given[n].shape
        two = (-1, shp[-1])
        by_layer = [got_conv] if n == "dn_conv_w" else [received[n, l].reshape(N_DEV, -1, shp[-1]) for l in range(depth)]
        outs = adamw(given[n].reshape(two), given["m_" + n].reshape(two), given["v_" + n].reshape(two), by_layer,
                     name="adamw_" + n)
        for res, val in zip(results, outs):
            res[n] = val.reshape(shp)
    outs = adamw(small_rows(given), small_rows({n: given["m_" + n] for n in rep_names}),
                 small_rows({n: given["v_" + n] for n in rep_names}), [small_all], name="adamw_replicated")
    for res, val in zip(results, outs):
        row = 0
        for n, nr in zip(rep_names, rep_rows):
            res[n] = val[row:row + nr].reshape(-1)[:given[n].size].reshape(given[n].shape)
            row += nr
    order = ["attn_norm", "w_in", "sgu_ln_g", "sgu_ln_b", "sgu_w", "sgu_b", "attn_sinks", "dn_conv_w", "dn_a_log",
             "dn_dt_bias", "dn_norm", "w_branch", "w_out", "ffn_norm", "w_gate_up", "w_down", "final_norm"]
    return (loss, dx.reshape(x.shape), *[res[n] for res in results for n in order])
```

```python
import functools

import jax
import jax.numpy as jnp
from jax import lax
from jax.experimental import pallas as pl
from jax.experimental.pallas import tpu as pltpu

f32 = jnp.float32
bf16 = jnp.bfloat16

N_DEV = 8
MESH_AXES = ("x", "y", "c")
NORM_EPS = 1e-6
MIX = 512
SGU_GROUPS, SGU_CHUNK = 4, 128
SWA_HEADS, SWA_KV, SWA_HD, WINDOW = 8, 2, 64, 128
ROPE_THETA, ROPE_DIM = 500000.0, 16
DN_HEADS, DN_HD, DN_CONV, DN_CHUNK = 4, 128, 4, 64
ADAM_LR, ADAM_B1, ADAM_B2, ADAM_EPS, ADAM_WD, ADAM_STEP = 0.001, 0.9, 0.999, 1e-08, 0.01, 10

LANES = 128
VMEM_LIMIT = 56 * 1024 * 1024

SEC_A, SEC_Z, SEC_QKV, SEC_G, SEC_B = 0, 1024, 1536, 3072, 6144
W_MAIN = 6912
C_B, C_QKV, C_Z, C_BA, C_G = 1024, 1792, 3328, 3840, 3848


def _params(n_axes, **kw):
    return pltpu.CompilerParams(dimension_semantics=("arbitrary",) * n_axes, vmem_limit_bytes=VMEM_LIMIT, **kw)


def _tile(n, target, mult=LANES):
    if n <= target:
        return n
    best = None
    for t in range(mult, target + 1, mult):
        if n % t == 0:
            best = t
    assert best is not None, (n, target, mult)
    return best


def _dg(a, b, ca, cb):
    return lax.dot_general(a.astype(bf16), b.astype(bf16), (((ca,), (cb,)), ((), ())), preferred_element_type=f32)


def _dg3(a, b, ca, cb):
    a_hi, b_hi = a.astype(bf16), b.astype(bf16)
    a_lo, b_lo = (a - a_hi.astype(f32)).astype(bf16), (b - b_hi.astype(f32)).astype(bf16)

    def dot(p, q):
        return lax.dot_general(p, q, (((ca,), (cb,)), ((), ())), preferred_element_type=f32)

    return dot(a_hi, b_hi) + (dot(a_hi, b_lo) + dot(a_lo, b_hi))


def _differentiable_dot(core):
    @functools.partial(jax.custom_vjp, nondiff_argnums=(2, 3))
    def dot(a, b, ca, cb):
        return core(a, b, ca, cb)

    def fwd(a, b, ca, cb):
        return core(a, b, ca, cb), (a, b)

    def bwd(ca, cb, res, ct):
        a, b = res
        da = core(ct, b, 1, 1 - cb) if ca == 1 else core(b, ct, 1 - cb, 1)
        db = core(a, ct, 1 - ca, 0) if cb == 0 else core(ct, a, 0, 1 - ca)
        return da, db

    dot.defvjp(fwd, bwd)
    return dot


bdot = _differentiable_dot(_dg)
_hdot = _differentiable_dot(_dg3)


def hdot(a, b, ca=1, cb=0):
    return _hdot(a, b, ca, cb)


@functools.partial(jax.custom_vjp, nondiff_argnums=(1,))
def lroll(x, shift):
    return pltpu.roll(x, shift, 1)


def _lroll_fwd(x, shift):
    return pltpu.roll(x, shift, 1), None


def _lroll_bwd(shift, _, ct):
    return (pltpu.roll(ct, ct.shape[1] - shift, 1),)


lroll.defvjp(_lroll_fwd, _lroll_bwd)


@functools.partial(jax.custom_vjp, nondiff_argnums=(1,))
def tri_inv(low, nil):
    n = low.shape[0]
    row = lax.broadcasted_iota(jnp.int32, (n, n), 0)
    col = lax.broadcasted_iota(jnp.int32, (n, n), 1)
    eye = (row == col).astype(f32)
    m = -low
    p = eye + m
    span = 2
    while span < nil:
        dot = hdot if span == 2 else (lambda a, b: bdot(a, b, 1, 0))
        m = dot(m, m)
        p = p + dot(p, m)
        span *= 2
    return p


def _tri_inv_fwd(low, nil):
    t = tri_inv(low, nil)
    return t, t


def _tri_inv_bwd(nil, t, dt):
    return (-bdot(t, bdot(dt, t, 1, 1), 0, 0),)


tri_inv.defvjp(_tri_inv_fwd, _tri_inv_bwd)


@jax.custom_vjp
def tri_inv_known(low, t):
    return t


def _tri_inv_known_fwd(low, t):
    return t, t


def _tri_inv_known_bwd(t, dt):
    return _tri_inv_bwd(None, t, dt) + (jnp.zeros_like(t),)


tri_inv_known.defvjp(_tri_inv_known_fwd, _tri_inv_known_bwd)


def _sigmoid(x):
    return 1.0 / (1.0 + jnp.exp(-x))


def _rms(x, g):
    return x * lax.rsqrt(jnp.mean(x * x, axis=-1, keepdims=True) + NORM_EPS) * g


def _lane_col(x, lane_idx):
    lane = lax.broadcasted_iota(jnp.int32, x.shape, 1)
    return jnp.sum(jnp.where(lane == lane_idx, x, 0.0), axis=1, keepdims=True)


def matmul(a, b, mode, out_dtype, *, residual=None, group=None, side=None, tm=1024, tn=768, tk=1024, name):
    dims = {"a": ("m", "k") if mode != "tn" else ("k", "m"),
            "b": {"nn": ("k", "n"), "nt": ("n", "k"), "tn": ("k", "n")}[mode], "o": ("m", "n")}
    full, groups = {}, 1
    for arr, key in ((a, "a"), (b, "b")):
        grouped = group in dims[key]
        if grouped:
            groups = arr.shape[0]
        full[dims[key][0]], full[dims[key][1]] = arr.shape[1:] if grouped else arr.shape
    want = {"m": tm, "n": tn, "k": tk}
    per_step = min(groups, 4) if group == "k" else 1
    tiles = {d: full[d] if d == group else _tile(full[d], want[d]) for d in "mnk"}
    steps = {d: groups // per_step if d == group else full[d] // tiles[d] for d in "mnk"}

    def spec(key):
        d0, d1 = dims[key]

        def index(i, j, kk):
            at = {"m": i, "n": j, "k": kk}
            if group in (d0, d1):
                return (at[group], 0 if d0 == group else at[d0], 0 if d1 == group else at[d1])
            return (at[d0], at[d1])

        block = (tiles[d0], tiles[d1])
        if group in (d0, d1):
            block = ((per_step if group == "k" else None),) + block
        return pl.BlockSpec(block, index)

    ca, cb = {"nn": (1, 0), "nt": (1, 1), "tn": (0, 0)}[mode]
    nk = steps["k"]
    o_spec = spec("o")
    out_shape = (groups, full["m"], full["n"]) if group in ("m", "n") else (full["m"], full["n"])
    has_res = residual is not None

    def product(a_ref, b_ref):
        if group != "k":
            return _dg(a_ref[...], b_ref[...], ca, cb)
        total = _dg(a_ref[0], b_ref[0], ca, cb)
        for g in range(1, per_step):
            total = total + _dg(a_ref[g], b_ref[g], ca, cb)
        return total

    def body(a_ref, b_ref, *rest):
        r_ref = rest[0] if has_res else None
        o_ref = rest[1 if has_res else 0]

        def emit(acc):
            o_ref[...] = (acc + r_ref[...] if has_res else acc).astype(out_dtype)

        if nk == 1:
            emit(product(a_ref, b_ref))
            return
        acc_ref = rest[-1]
        kk = pl.program_id(2)

        @pl.when(kk == 0)
        def _():
            acc_ref[...] = jnp.zeros_like(acc_ref)

        acc_ref[...] += product(a_ref, b_ref)

        @pl.when(kk == nk - 1)
        def _():
            emit(acc_ref[...])

    res = host_call(
        side, body, name=name, grid=(steps["m"], steps["n"], nk),
        in_specs=[spec("a"), spec("b")] + ([o_spec] if has_res else []), out_specs=[o_spec],
        out_shape=[jax.ShapeDtypeStruct(out_shape, out_dtype)],
        scratch_shapes=[pltpu.VMEM((tiles["m"], tiles["n"]), f32)] if nk > 1 else [], aliases={},
        args=(a, b) + ((residual,) if has_res else ()))
    return res[0] if side is None else (res[0][0], res[1])


def rmsnorm_fwd(x, g_row, *, name):
    s, d = x.shape
    ts = _tile(s, 512, 16)

    def body(x_ref, g_ref, o_ref):
        o_ref[...] = _rms(x_ref[...], g_ref[...]).astype(bf16)

    return pl.pallas_call(
        body, name=name, grid=(s // ts,),
        in_specs=[pl.BlockSpec((ts, d), lambda i: (i, 0)), pl.BlockSpec((1, d), lambda i: (0, 0))],
        out_specs=pl.BlockSpec((ts, d), lambda i: (i, 0)), out_shape=jax.ShapeDtypeStruct((s, d), bf16),
        compiler_params=_params(1),
    )(x, g_row)


def rmsnorm_bwd(x, g_row, dh, dres, *, name):
    s, d = x.shape
    ts = _tile(s, 512, 16)

    def body(x_ref, g_ref, dh_ref, dres_ref, dx_ref, dg_ref):
        @pl.when(pl.program_id(0) == 0)
        def _():
            dg_ref[...] = jnp.zeros_like(dg_ref)

        _, vjp = jax.vjp(_rms, x_ref[...], g_ref[...])
        dx, dg = vjp(dh_ref[...].astype(f32))
        dx_ref[...] = dx + dres_ref[...]
        dg_ref[...] += dg

    row = pl.BlockSpec((ts, d), lambda i: (i, 0))
    vec = pl.BlockSpec((1, d), lambda i: (0, 0))
    return pl.pallas_call(
        body, name=name, grid=(s // ts,), in_specs=[row, vec, row, row], out_specs=[row, vec],
        out_shape=[jax.ShapeDtypeStruct((s, d), f32), jax.ShapeDtypeStruct((1, d), f32)],
        compiler_params=_params(1),
    )(x, g_row, dh, dres)


def loss_head(x, g_row, target, *, name):
    s, d = x.shape
    ts = _tile(s, 512, 16)

    def body(x_ref, g_ref, t_ref, dx_ref, dg_ref, loss_ref):
        @pl.when(pl.program_id(0) == 0)
        def _():
            dg_ref[...] = jnp.zeros_like(dg_ref)
            loss_ref[...] = jnp.zeros_like(loss_ref)

        y, vjp = jax.vjp(_rms, x_ref[...], g_ref[...])
        err = y - t_ref[...]
        dx, dg = vjp(err * (1.0 / d))
        dx_ref[...] = dx
        dg_ref[...] += dg
        loss_ref[...] += 0.5 * jnp.sum(jnp.sum(err * err, axis=1, keepdims=True) * (1.0 / d), axis=0, keepdims=True)

    row = pl.BlockSpec((ts, d), lambda i: (i, 0))
    vec = pl.BlockSpec((1, d), lambda i: (0, 0))
    one = pl.BlockSpec((1, LANES), lambda i: (0, 0))
    return pl.pallas_call(
        body, name=name, grid=(s // ts,), in_specs=[row, vec, row], out_specs=[row, vec, one],
        out_shape=[jax.ShapeDtypeStruct((s, d), f32), jax.ShapeDtypeStruct((1, d), f32),
                   jax.ShapeDtypeStruct((1, LANES), f32)],
        compiler_params=_params(1),
    )(x, g_row, target)


def _sgu_chunk(p_a, ln_g, ln_b, w, b_t):
    t = SGU_CHUNK
    u = jax.nn.gelu(p_a[:, :MIX])
    v = jax.nn.gelu(p_a[:, MIX:])
    vc = v - jnp.mean(v, axis=-1, keepdims=True)
    vn = vc * lax.rsqrt(jnp.mean(vc * vc, axis=-1, keepdims=True) + NORM_EPS) * ln_g + ln_b
    causal = lax.broadcasted_iota(jnp.int32, (t, t), 0) >= lax.broadcasted_iota(jnp.int32, (t, t), 1)
    outs = []
    for g in range(SGU_GROUPS):
        sl = slice(g * LANES, (g + 1) * LANES)
        mixed = bdot(jnp.where(causal, w[g], 0.0), vn[:, sl], 1, 0) + b_t[:, g:g + 1]
        outs.append(u[:, sl] * mixed)
    return jnp.concatenate(outs, axis=1)


def _sgu_specs(s, ts):
    return [pl.BlockSpec((ts, 2 * MIX), lambda i: (i, SEC_A // (2 * MIX))),
            pl.BlockSpec((1, MIX), lambda i: (0, 0)), pl.BlockSpec((1, MIX), lambda i: (0, 0)),
            pl.BlockSpec((SGU_GROUPS, SGU_CHUNK, SGU_CHUNK), lambda i: (0, 0, 0)),
            pl.BlockSpec((SGU_CHUNK, SGU_GROUPS), lambda i: (0, 0))]


def sgu_fwd(proj, ln_g, ln_b, w, b_t, *, name):
    s = proj.shape[0]
    ts = _tile(s, 512)
    n_chunk = ts // SGU_CHUNK

    def body(p_ref, g_ref, b_ref, w_ref, bt_ref, o_ref):
        def step(c, carry):
            rows = pl.ds(pl.multiple_of(c * SGU_CHUNK, SGU_CHUNK), SGU_CHUNK)
            o_ref[rows, :] = _sgu_chunk(p_ref[rows, :].astype(f32), g_ref[...], b_ref[...], w_ref[...], bt_ref[...]).astype(bf16)
            return carry
        lax.fori_loop(0, n_chunk, step, 0)

    return pl.pallas_call(
        body, name=name, grid=(s // ts,), in_specs=_sgu_specs(s, ts),
        out_specs=pl.BlockSpec((ts, MIX), lambda i: (i, 0)), out_shape=jax.ShapeDtypeStruct((s, MIX), bf16),
        compiler_params=_params(1),
    )(proj, ln_g, ln_b, w, b_t)


def sgu_bwd(proj, ln_g, ln_b, w, b_t, d_out, dproj, *, name):
    s = proj.shape[0]
    ts = _tile(s, 512)
    n_chunk = ts // SGU_CHUNK

    def body(p_ref, g_ref, b_ref, w_ref, bt_ref, do_ref, _, dp_ref, dg_ref, db_ref, dw_ref, dbt_ref):
        @pl.when(pl.program_id(0) == 0)
        def _():
            dg_ref[...] = jnp.zeros_like(dg_ref)
            db_ref[...] = jnp.zeros_like(db_ref)
            dw_ref[...] = jnp.zeros_like(dw_ref)
            dbt_ref[...] = jnp.zeros_like(dbt_ref)

        def step(c, carry):
            rows = pl.ds(pl.multiple_of(c * SGU_CHUNK, SGU_CHUNK), SGU_CHUNK)
            _, vjp = jax.vjp(_sgu_chunk, p_ref[rows, :].astype(f32), g_ref[...], b_ref[...], w_ref[...], bt_ref[...])
            dp, dg, db, dw, dbt = vjp(do_ref[rows, :])
            dp_ref[rows, :] = dp.astype(bf16)
            dg_ref[...] += dg
            db_ref[...] += db
            dw_ref[...] += dw
            dbt_ref[...] += dbt
            return carry
        lax.fori_loop(0, n_chunk, step, 0)

    specs = _sgu_specs(s, ts)
    return pl.pallas_call(
        body, name=name, grid=(s // ts,),
        in_specs=specs + [pl.BlockSpec((ts, MIX), lambda i: (i, 0)), pl.BlockSpec(memory_space=pl.ANY)],
        out_specs=[specs[0], specs[1], specs[2], specs[3], specs[4]],
        out_shape=[jax.ShapeDtypeStruct(dproj.shape, bf16), jax.ShapeDtypeStruct((1, MIX), f32),
                   jax.ShapeDtypeStruct((1, MIX), f32), jax.ShapeDtypeStruct(w.shape, f32),
                   jax.ShapeDtypeStruct(b_t.shape, f32)],
        input_output_aliases={6: 0}, compiler_params=_params(1),
    )(proj, ln_g, ln_b, w, b_t, d_out, dproj)


def rope_tables(posf, inv_freq, *, name):
    s = posf.shape[0]
    ts = _tile(s, 1024, 8)
    half = ROPE_DIM // 2

    def body(pos_ref, inv_ref, o_ref):
        d = lax.broadcasted_iota(jnp.int32, (1, LANES), 1) % SWA_HD
        ang = pos_ref[...] * inv_ref[...]
        sin = jnp.sin(ang)
        o_ref[0] = jnp.cos(ang)
        o_ref[1] = jnp.where(d < half, sin, 0.0)
        o_ref[2] = jnp.where((d >= half) & (d < ROPE_DIM), sin, 0.0)

    return pl.pallas_call(
        body, name=name, grid=(s // ts,),
        in_specs=[pl.BlockSpec((ts, 1), lambda i: (i, 0)), pl.BlockSpec((1, LANES), lambda i: (0, 0))],
        out_specs=pl.BlockSpec((3, ts, LANES), lambda i: (0, i, 0)), out_shape=jax.ShapeDtypeStruct((3, s, LANES), f32),
        compiler_params=_params(1),
    )(posf, inv_freq)


def _rope(x, table):
    w = x.shape[1]
    half = ROPE_DIM // 2
    c, lo, hi = (jnp.concatenate([table[i]] * (w // LANES), axis=1) for i in range(3))
    return x * c - lroll(x, w - half) * lo + lroll(x, half) * hi


def _swa_block(q, kp, kc, vp, vc, sink_row, table_q, table_p, prev_ok, wide):
    t = WINDOW
    q = _rope(q, table_q) * (SWA_HD ** -0.5)
    keys = jnp.concatenate([_rope(kp, table_p), _rope(kc, table_q)], axis=0)
    vals = jnp.concatenate([vp, vc], axis=0)
    own = lax.broadcasted_iota(jnp.int32, (t, t), 0) >= lax.broadcasted_iota(jnp.int32, (t, t), 1)
    lane_half = lax.broadcasted_iota(jnp.int32, (t, LANES), 1) // SWA_HD
    group = SWA_HEADS // SWA_KV
    slabs = []
    for pair in range(SWA_HEADS // 2):
        q_pair = q[:, pair * LANES:(pair + 1) * LANES]
        acc = jnp.zeros((t, LANES), f32)
        for half in range(2):
            h = 2 * pair + half
            kv = h // group
            qm = jnp.where(lane_half == half, q_pair, 0.0)
            if half != kv:
                qm = lroll(qm, SWA_HD)
            if wide:
                both = bdot(qm, keys, 1, 1)
                s_prev, s_own = both[:, :t], both[:, t:]
            else:
                s_prev, s_own = bdot(qm, keys[:t], 1, 1), bdot(qm, keys[t:], 1, 1)
            logits = jnp.where(own, s_own, jnp.where(prev_ok, s_prev, -1e30))
            sink = _lane_col(sink_row, h)
            m = lax.stop_gradient(jnp.maximum(jnp.max(logits, axis=1, keepdims=True), sink))
            p = jnp.exp(logits - m)
            probs = p * (1.0 / (jnp.sum(p, axis=1, keepdims=True) + jnp.exp(sink - m)))
            p_prev, p_own = jnp.where(own, 0.0, probs), jnp.where(own, probs, 0.0)
            if wide:
                o = bdot(jnp.concatenate([p_prev, p_own], axis=1), vals, 1, 0)
            else:
                o = bdot(p_prev, vals[:t], 1, 0) + bdot(p_own, vals[t:], 1, 0)
            o = jnp.where(lane_half == kv, o, 0.0)
            if half != kv:
                o = lroll(o, SWA_HD)
            acc = acc + o
        slabs.append(acc)
    return jnp.concatenate(slabs, axis=1)


def _swa_in_specs(nc, clamp):
    t = WINDOW
    qb, kb, vb = SEC_B // MIX, (SEC_B + MIX) // LANES, (SEC_B + MIX + LANES) // LANES

    def cur(i):
        return jnp.minimum(i, nc - 1) if clamp else i

    def prev(i):
        return jnp.maximum(cur(i) - 1, 0)

    return [pl.BlockSpec((t, MIX), lambda i: (cur(i), qb)),
            pl.BlockSpec((t, LANES), lambda i: (prev(i), kb)), pl.BlockSpec((t, LANES), lambda i: (cur(i), kb)),
            pl.BlockSpec((t, LANES), lambda i: (prev(i), vb)), pl.BlockSpec((t, LANES), lambda i: (cur(i), vb)),
            pl.BlockSpec((1, LANES), lambda i: (0, 0)),
            pl.BlockSpec((3, t, LANES), lambda i: (0, cur(i), 0)), pl.BlockSpec((3, t, LANES), lambda i: (0, prev(i), 0))]


def swa_fwd(proj, sink_row, tables, *, name):
    s = proj.shape[0]
    nc = s // WINDOW

    def body(q_ref, kp_ref, kc_ref, vp_ref, vc_ref, sink_ref, tq_ref, tp_ref, o_ref):
        prev_ok = pl.program_id(0) > 0
        blocks = [r[...].astype(f32) for r in (q_ref, kp_ref, kc_ref, vp_ref, vc_ref)]
        o_ref[...] = _swa_block(*blocks, sink_ref[...],
                                tq_ref[...], tp_ref[...], prev_ok, True).astype(bf16)

    return pl.pallas_call(
        body, name=name, grid=(nc,), in_specs=_swa_in_specs(nc, False),
        out_specs=pl.BlockSpec((WINDOW, MIX), lambda i: (i, 0)), out_shape=jax.ShapeDtypeStruct((s, MIX), bf16),
        compiler_params=_params(1),
    )(proj, proj, proj, proj, proj, sink_row, tables, tables)


def swa_bwd(proj, sink_row, tables, d_out, dproj, *, side=None, name):
    s = proj.shape[0]
    nc = s // WINDOW
    t = WINDOW

    def body(q_ref, kp_ref, kc_ref, vp_ref, vc_ref, sink_ref, tq_ref, tp_ref, do_ref, _,
             dp_ref, dsink_ref, cq_ref, ck_ref, cv_ref):
        i = pl.program_id(0)

        @pl.when(i == 0)
        def _():
            dsink_ref[...] = jnp.zeros_like(dsink_ref)

        def write(dk_prev, dv_prev):
            dp_ref[:, :MIX] = cq_ref[...].astype(bf16)
            dp_ref[:, MIX:MIX + LANES] = (ck_ref[...] + dk_prev).astype(bf16)
            dp_ref[:, MIX + LANES:] = (cv_ref[...] + dv_prev).astype(bf16)

        @pl.when(i < nc)
        def _():
            fn = functools.partial(_swa_block, table_q=tq_ref[...], table_p=tp_ref[...], prev_ok=i > 0, wide=False)
            blocks = [r[...].astype(f32) for r in (q_ref, kp_ref, kc_ref, vp_ref, vc_ref)]
            _, vjp = jax.vjp(fn, *blocks, sink_ref[...])
            dq, dkp, dkc, dvp, dvc, dsink = vjp(do_ref[...])
            dsink_ref[...] += dsink

            @pl.when(i > 0)
            def _():
                write(dkp, dvp)

            cq_ref[...] = dq
            ck_ref[...] = dkc
            cv_ref[...] = dvc

        @pl.when(i == nc)
        def _():
            write(0.0, 0.0)

    return host_call(
        side, body, name=name, grid=(nc + 1,),
        in_specs=_swa_in_specs(nc, True) + [pl.BlockSpec((t, MIX), lambda i: (jnp.minimum(i, nc - 1), 0)),
                                            pl.BlockSpec(memory_space=pl.ANY)],
        out_specs=[pl.BlockSpec((t, MIX + 2 * LANES), lambda i: (jnp.maximum(i - 1, 0), SEC_B // (MIX + 2 * LANES))),
                   pl.BlockSpec((1, LANES), lambda i: (0, 0))],
        out_shape=[jax.ShapeDtypeStruct(dproj.shape, bf16), jax.ShapeDtypeStruct((1, LANES), f32)],
        scratch_shapes=[pltpu.VMEM((t, MIX), f32), pltpu.VMEM((t, LANES), f32), pltpu.VMEM((t, LANES), f32)],
        aliases={9: 0}, args=(proj, proj, proj, proj, proj, sink_row, tables, tables, d_out, dproj))


CONV_PAD = 16


def _conv_taps(xp, rows):
    off = CONV_PAD - (DN_CONV - 1)
    return [xp[off + i:off + i + rows] for i in range(DN_CONV)]


def _conv_pre(taps, w):
    pre = taps[0] * w[0:1]
    for i in range(1, DN_CONV):
        pre = pre + taps[i] * w[i:i + 1]
    return pre


def conv_fwd(proj, conv_w8, *, name):
    s = proj.shape[0]
    wq = 3 * MIX
    ts = _tile(s, 512)
    nb = ts // CONV_PAD

    def body(x_ref, prev_ref, w_ref, o_ref):
        prev = jnp.where(pl.program_id(0) > 0, prev_ref[...].astype(f32), 0.0)
        pre = _conv_pre(_conv_taps(jnp.concatenate([prev, x_ref[...].astype(f32)], axis=0), ts), w_ref[...])
        o_ref[...] = pre * _sigmoid(pre)

    return pl.pallas_call(
        body, name=name, grid=(s // ts,),
        in_specs=[pl.BlockSpec((ts, wq), lambda i: (i, SEC_QKV // wq)),
                  pl.BlockSpec((CONV_PAD, wq), lambda i: (jnp.maximum(i * nb - 1, 0), SEC_QKV // wq)),
                  pl.BlockSpec((CONV_PAD, wq), lambda i: (0, 0))],
        out_specs=pl.BlockSpec((ts, wq), lambda i: (i, 0)), out_shape=jax.ShapeDtypeStruct((s, wq), f32),
        compiler_params=_params(1),
    )(proj, proj, conv_w8)


def conv_bwd(proj, conv_w8, dxc, dproj, *, side=None, name):
    s = proj.shape[0]
    wq = 3 * MIX
    ts = _tile(s, 512)
    nb = ts // CONV_PAD
    nt = s // ts
    last_blk = s // CONV_PAD - 1

    def body(x_ref, prev_ref, next_ref, w_ref, d_ref, dnext_ref, _, dp_ref, dw_ref):
        i = pl.program_id(0)

        @pl.when(i == 0)
        def _():
            dw_ref[...] = jnp.zeros_like(dw_ref)

        w = w_ref[...]
        prev = jnp.where(i > 0, prev_ref[...].astype(f32), 0.0)
        more = i < nt - 1
        xp = jnp.concatenate([prev, x_ref[...].astype(f32), jnp.where(more, next_ref[...].astype(f32), 0.0)], axis=0)
        taps = _conv_taps(xp, ts + CONV_PAD)
        pre = _conv_pre(taps, w)
        sig = _sigmoid(pre)
        dxc_ext = jnp.concatenate([d_ref[...], jnp.where(more, dnext_ref[...], 0.0)], axis=0)
        dpre = dxc_ext * sig * (1.0 + pre * (1.0 - sig))
        d_raw = jnp.zeros((ts, wq), f32)
        dws = []
        for k in range(DN_CONV):
            shift = DN_CONV - 1 - k
            d_raw = d_raw + dpre[shift:shift + ts] * w[k:k + 1]
            dws.append(jnp.sum(dpre[:ts] * taps[k][:ts], axis=0, keepdims=True))
        dp_ref[...] = d_raw.astype(bf16)
        dw_ref[...] += jnp.concatenate(dws + [jnp.zeros((CONV_PAD - DN_CONV, wq), f32)], axis=0)

    sec = SEC_QKV // wq
    return host_call(
        side, body, name=name, grid=(nt,),
        in_specs=[pl.BlockSpec((ts, wq), lambda i: (i, sec)),
                  pl.BlockSpec((CONV_PAD, wq), lambda i: (jnp.maximum(i * nb - 1, 0), sec)),
                  pl.BlockSpec((CONV_PAD, wq), lambda i: (jnp.minimum((i + 1) * nb, last_blk), sec)),
                  pl.BlockSpec((CONV_PAD, wq), lambda i: (0, 0)),
                  pl.BlockSpec((ts, wq), lambda i: (i, 0)),
                  pl.BlockSpec((CONV_PAD, wq), lambda i: (jnp.minimum((i + 1) * nb, last_blk), 0)),
                  pl.BlockSpec(memory_space=pl.ANY)],
        out_specs=[pl.BlockSpec((ts, wq), lambda i: (i, sec)), pl.BlockSpec((CONV_PAD, wq), lambda i: (0, 0))],
        out_shape=[jax.ShapeDtypeStruct(dproj.shape, bf16), jax.ShapeDtypeStruct((CONV_PAD, wq), f32)],
        scratch_shapes=[], aliases={6: 0}, args=(proj, proj, proj, conv_w8, dxc, dxc, dproj))


def _dn_chunk(state, xc, z, ba, alog_row, dtb_row, norm_row, t_known):
    c, nh = DN_CHUNK, DN_HEADS
    n = c * nh
    row = lax.broadcasted_iota(jnp.int32, (n, n), 0)
    col = lax.broadcasted_iota(jnp.int32, (n, n), 1)
    same_head = (row // c) == (col // c)
    tril, strict = same_head & (row >= col), same_head & (row > col)
    tril_c = lax.broadcasted_iota(jnp.int32, (c, c), 0) >= lax.broadcasted_iota(jnp.int32, (c, c), 1)
    beta_all = _sigmoid(ba)
    g_all = -jnp.exp(alog_row) * jax.nn.softplus(ba + dtb_row)
    gc_all = hdot(tril_c.astype(f32), g_all)
    gc_t = gc_all.T

    def stack(piece):
        return jnp.concatenate([piece(h) for h in range(nh)], axis=0)

    q = stack(lambda h: xc[:, h * DN_HD:(h + 1) * DN_HD])
    k = stack(lambda h: xc[:, MIX + h * DN_HD:MIX + (h + 1) * DN_HD])
    v = stack(lambda h: xc[:, 2 * MIX + h * DN_HD:2 * MIX + (h + 1) * DN_HD])
    zs = stack(lambda h: z[:, h * DN_HD:(h + 1) * DN_HD])
    q = q * lax.rsqrt(jnp.sum(q * q, axis=-1, keepdims=True) + NORM_EPS) * (DN_HD ** -0.5)
    k = k * lax.rsqrt(jnp.sum(k * k, axis=-1, keepdims=True) + NORM_EPS)
    beta = stack(lambda h: _lane_col(beta_all, h))
    g_cols = [_lane_col(gc_all, nh + h) for h in range(nh)]
    g_col = jnp.concatenate(g_cols, axis=0)
    g_row = jnp.concatenate([gc_t[nh + h:nh + h + 1, :] for h in range(nh)], axis=1)
    g_last = stack(lambda h: jnp.broadcast_to(g_cols[h][c - 1:c, :], (c, 1)))
    decay = jnp.where(tril, jnp.exp(jnp.where(tril, g_col - g_row, 0.0)), 0.0)
    kb = k * beta
    low = jnp.where(strict, bdot(kb, k, 1, 1) * decay, 0.0)
    t_inv = tri_inv(low, c) if t_known is None else tri_inv_known(low, t_known)
    e_gc = jnp.exp(g_col)
    uw = bdot(t_inv, jnp.concatenate([v * beta, kb * e_gc], axis=1), 1, 0)
    u, w = uw[:, :DN_HD], uw[:, DN_HD:]
    attn = bdot(q, k, 1, 1) * decay
    own = (lax.broadcasted_iota(jnp.int32, (n, nh * DN_HD), 1) // DN_HD
           == lax.broadcasted_iota(jnp.int32, (n, nh * DN_HD), 0) // c)

    def spread(a):
        return jnp.where(own, jnp.concatenate([a] * nh, axis=1), 0.0)

    v_new = u - bdot(spread(w), state, 1, 0)
    o = bdot(spread(q * e_gc), state, 1, 0) + bdot(attn, v_new, 1, 0)
    keep = stack(lambda h: jnp.broadcast_to(jnp.exp(g_cols[h][c - 1:c, :]), (DN_HD, 1)))
    new_state = state * keep + bdot(spread(k * jnp.exp(g_last - g_col)), v_new, 0, 0)
    out = _rms(o, norm_row) * (zs * _sigmoid(zs))
    return new_state, jnp.concatenate([out[h * c:(h + 1) * c] for h in range(nh)], axis=1), t_inv


DN_STEP = 8 * DN_CHUNK


def _dn_step(state, xc, z, ba, alog_row, dtb_row, norm_row, t_known=None):
    outs, t_invs = [], []
    for c in range(DN_STEP // DN_CHUNK):
        rows = slice(c * DN_CHUNK, (c + 1) * DN_CHUNK)
        state, out, t_inv = _dn_chunk(state, xc[rows], z[rows], ba[rows], alog_row, dtb_row, norm_row,
                                      None if t_known is None else t_known[c])
        outs.append(out)
        t_invs.append(t_inv)
    return state, jnp.concatenate(outs, axis=0), jnp.stack(t_invs)


def _dn_specs(ts, order):
    zb = SEC_Z // MIX
    return [pl.BlockSpec((ts, 3 * MIX), lambda i: (order(i), 0)),
            pl.BlockSpec((ts, MIX), lambda i: (order(i), zb)),
            pl.BlockSpec((ts, LANES), lambda i: (order(i), 0)),
            pl.BlockSpec((1, LANES), lambda i: (0, 0)), pl.BlockSpec((1, LANES), lambda i: (0, 0)),
            pl.BlockSpec((1, LANES), lambda i: (0, 0))]


def dn_fwd(xc, proj, p_ba, alog_row, dtb_row, norm_row, *, side=None, name):
    s = xc.shape[0]
    ts = _tile(s, 512)
    n_step = ts // DN_STEP

    per_step = DN_STEP // DN_CHUNK
    n_tri = DN_HEADS * DN_CHUNK

    def body(xc_ref, z_ref, ba_ref, al_ref, dt_ref, nr_ref, o_ref, st_ref, tri_ref, state_ref):
        @pl.when(pl.program_id(0) == 0)
        def _():
            state_ref[...] = jnp.zeros_like(state_ref)

        def step(c, carry):
            rows = pl.ds(pl.multiple_of(c * DN_STEP, DN_STEP), DN_STEP)
            st_ref[c] = state_ref[...]
            new_state, out, t_invs = _dn_step(state_ref[...], xc_ref[rows, :], z_ref[rows, :].astype(f32), ba_ref[rows, :],
                                              al_ref[...], dt_ref[...], nr_ref[...])
            state_ref[...] = new_state
            o_ref[rows, :] = out.astype(bf16)
            tri_ref[pl.ds(c * per_step, per_step)] = t_invs
            return carry
        lax.fori_loop(0, n_step, step, 0)

    return host_call(
        side, body, name=name, grid=(s // ts,), in_specs=_dn_specs(ts, lambda i: i),
        out_specs=[pl.BlockSpec((ts, MIX), lambda i: (i, 0)),
                   pl.BlockSpec((n_step, DN_HEADS * DN_HD, DN_HD), lambda i: (i, 0, 0)),
                   pl.BlockSpec((n_step * per_step, n_tri, n_tri), lambda i: (i, 0, 0))],
        out_shape=[jax.ShapeDtypeStruct((s, MIX), bf16),
                   jax.ShapeDtypeStruct((s // DN_STEP, DN_HEADS * DN_HD, DN_HD), f32),
                   jax.ShapeDtypeStruct((s // DN_CHUNK, n_tri, n_tri), f32)],
        scratch_shapes=[pltpu.VMEM((DN_HEADS * DN_HD, DN_HD), f32)], aliases={},
        args=(xc, proj, p_ba, alog_row, dtb_row, norm_row))


def dn_bwd(xc, proj, p_ba, alog_row, dtb_row, norm_row, saved, tri, d_out, dproj, *, side=None, name):
    s = xc.shape[0]
    ts = _tile(s, 512)
    n_step = ts // DN_STEP
    nt = s // ts

    per_step = DN_STEP // DN_CHUNK
    n_tri = DN_HEADS * DN_CHUNK

    def body(xc_ref, z_ref, ba_ref, al_ref, dt_ref, nr_ref, st_ref, tri_ref, do_ref, _,
             dz_ref, dxc_ref, dba_ref, dal_ref, ddt_ref, dnr_ref, dstate_ref):
        @pl.when(pl.program_id(0) == 0)
        def _():
            dstate_ref[...] = jnp.zeros_like(dstate_ref)
            dal_ref[...] = jnp.zeros_like(dal_ref)
            ddt_ref[...] = jnp.zeros_like(ddt_ref)
            dnr_ref[...] = jnp.zeros_like(dnr_ref)

        def step(it, carry):
            c = n_step - 1 - it
            rows = pl.ds(pl.multiple_of(c * DN_STEP, DN_STEP), DN_STEP)
            t_known = tri_ref[pl.ds(c * per_step, per_step)]
            _, vjp = jax.vjp(lambda *a: _dn_step(*a, t_known=t_known)[:2], st_ref[c], xc_ref[rows, :],
                             z_ref[rows, :].astype(f32), ba_ref[rows, :], al_ref[...], dt_ref[...], nr_ref[...])
            d_in, dxc, dz, dba, dal, ddt, dnr = vjp((dstate_ref[...], do_ref[rows, :]))
            dstate_ref[...] = d_in
            dxc_ref[rows, :] = dxc
            dz_ref[rows, :] = dz.astype(bf16)
            dba_ref[rows, :] = dba.astype(bf16)
            dal_ref[...] += dal
            ddt_ref[...] += ddt
            dnr_ref[...] += dnr
            return carry
        lax.fori_loop(0, n_step, step, 0)

    def rev(i):
        return nt - 1 - i

    specs = _dn_specs(ts, rev)
    vec = pl.BlockSpec((1, LANES), lambda i: (0, 0))
    return host_call(
        side, body, name=name, grid=(nt,),
        in_specs=specs + [pl.BlockSpec((n_step, DN_HEADS * DN_HD, DN_HD), lambda i: (rev(i), 0, 0)),
                          pl.BlockSpec((n_step * per_step, n_tri, n_tri), lambda i: (rev(i), 0, 0)),
                          pl.BlockSpec((ts, MIX), lambda i: (rev(i), 0)), pl.BlockSpec(memory_space=pl.ANY)],
        out_specs=[specs[1], specs[0], specs[2], vec, vec, vec],
        out_shape=[jax.ShapeDtypeStruct(dproj.shape, bf16), jax.ShapeDtypeStruct((s, 3 * MIX), f32),
                   jax.ShapeDtypeStruct((s, LANES), bf16)] + [jax.ShapeDtypeStruct((1, LANES), f32)] * 3,
        scratch_shapes=[pltpu.VMEM((DN_HEADS * DN_HD, DN_HD), f32)], aliases={9: 0},
        args=(xc, proj, p_ba, alog_row, dtb_row, norm_row, saved, tri, d_out, dproj))


def _merge_in_specs(ts, d):
    row = pl.BlockSpec((ts, MIX), lambda i: (i, 0))
    return [row, row, row, pl.BlockSpec((ts, 3 * d), lambda i: (i, SEC_G // (3 * d))),
            pl.BlockSpec((3, MIX, d), lambda i: (0, 0, 0))]


def merge_fwd(out_a, out_b, out_c, proj, w_branch, *, name):
    s, d = out_a.shape[0], w_branch.shape[2]
    ts = _tile(s, 512, 16)

    def body(a_ref, b_ref, c_ref, g_ref, w_ref, o_ref):
        acc = jnp.zeros((ts, d), f32)
        for n, r in enumerate((a_ref, b_ref, c_ref)):
            acc = acc + _sigmoid(g_ref[:, n * d:(n + 1) * d].astype(f32)) * _dg(r[...], w_ref[n], 1, 0)
        o_ref[...] = acc.astype(bf16)

    return pl.pallas_call(
        body, name=name, grid=(s // ts,), in_specs=_merge_in_specs(ts, d),
        out_specs=pl.BlockSpec((ts, d), lambda i: (i, 0)), out_shape=jax.ShapeDtypeStruct((s, d), bf16),
        compiler_params=_params(1),
    )(out_a, out_b, out_c, proj, w_branch)


def merge_bwd(out_a, out_b, out_c, proj, w_branch, d_merged, dproj, *, name):
    s, d = out_a.shape[0], w_branch.shape[2]
    ts = _tile(s, 512, 16)

    def body(a_ref, b_ref, c_ref, g_ref, w_ref, dm_ref, _, dg_ref, da_ref, db_ref, dc_ref, dw_ref):
        @pl.when(pl.program_id(0) == 0)
        def _():
            dw_ref[...] = jnp.zeros_like(dw_ref)

        dm = dm_ref[...]
        for n, (r, dr) in enumerate(((a_ref, da_ref), (b_ref, db_ref), (c_ref, dc_ref))):
            gate = _sigmoid(g_ref[:, n * d:(n + 1) * d].astype(f32))
            branch = _dg(r[...], w_ref[n], 1, 0)
            dg_ref[:, n * d:(n + 1) * d] = (dm * branch * gate * (1.0 - gate)).astype(bf16)
            d_branch = dm * gate
            dr[...] = _dg(d_branch, w_ref[n], 1, 1)
            dw_ref[n] += _dg(r[...], d_branch, 0, 0)

    specs = _merge_in_specs(ts, d)
    row_f = pl.BlockSpec((ts, MIX), lambda i: (i, 0))
    return pl.pallas_call(
        body, name=name, grid=(s // ts,),
        in_specs=specs + [pl.BlockSpec((ts, d), lambda i: (i, 0)), pl.BlockSpec(memory_space=pl.ANY)],
        out_specs=[specs[3], row_f, row_f, row_f, specs[4]],
        out_shape=[jax.ShapeDtypeStruct(dproj.shape, bf16)] + [jax.ShapeDtypeStruct((s, MIX), f32)] * 3
        + [jax.ShapeDtypeStruct(w_branch.shape, f32)],
        input_output_aliases={6: 0}, compiler_params=_params(1),
    )(out_a, out_b, out_c, proj, w_branch, d_merged, dproj)


def gate_up_swiglu(h, w_gu, *, name):
    g2, d, w = w_gu.shape
    ng, s = g2 // 2, h.shape[0]
    tm = _tile(s, 1024, 16)

    def body(h_ref, wg_ref, wu_ref, g_ref, u_ref, a_ref):
        g16 = _dg(h_ref[...], wg_ref[...], 1, 0).astype(bf16)
        u16 = _dg(h_ref[...], wu_ref[...], 1, 0).astype(bf16)
        g_ref[...], u_ref[...] = g16, u16
        g = g16.astype(f32)
        a_ref[...] = (g * _sigmoid(g) * u16.astype(f32)).astype(bf16)

    out = pl.BlockSpec((None, tm, w), lambda i, j: (j, i, 0))
    return pl.pallas_call(
        body, name=name, grid=(s // tm, ng),
        in_specs=[pl.BlockSpec((tm, d), lambda i, j: (i, 0)), pl.BlockSpec((None, d, w), lambda i, j: (j, 0, 0)),
                  pl.BlockSpec((None, d, w), lambda i, j: (ng + j, 0, 0))],
        out_specs=[out, out, out], out_shape=[jax.ShapeDtypeStruct((ng, s, w), bf16)] * 3, compiler_params=_params(2),
    )(h, w_gu, w_gu)


def down_dx_swiglu(dx, w_down, gate, up, *, name):
    ng, w, d = w_down.shape
    s = dx.shape[0]
    tm = _tile(s, 1024, 16)

    def body(dx_ref, wd_ref, g_ref, u_ref, dg_ref, du_ref):
        d_act = _dg(dx_ref[...], wd_ref[...], 1, 1)
        g = g_ref[...].astype(f32)
        sig = _sigmoid(g)
        dg_ref[...] = (d_act * u_ref[...].astype(f32) * sig * (1.0 + g * (1.0 - sig))).astype(bf16)
        du_ref[...] = (d_act * g * sig).astype(bf16)

    blk = pl.BlockSpec((None, tm, w), lambda i, j: (j, i, 0))
    return pl.pallas_call(
        body, name=name, grid=(s // tm, ng),
        in_specs=[pl.BlockSpec((tm, d), lambda i, j: (i, 0)), pl.BlockSpec((None, w, d), lambda i, j: (j, 0, 0)), blk, blk],
        out_specs=[blk, blk], out_shape=[jax.ShapeDtypeStruct((ng, s, w), bf16)] * 2, compiler_params=_params(2),
    )(dx, w_down, gate, up)


def adamw(w, m, v, g_parts, *, name):
    n_layers = len(g_parts)
    n_parts, r, cols = g_parts[0].shape
    lanes = -(-cols // LANES) * LANES
    tr = _tile(r, max(16, (128 * 1024) // lanes), 16)
    nr = r // tr

    def body(w_ref, m_ref, v_ref, *rest):
        gp_refs, (g_ref, d_ref, nm_ref, nv_ref) = rest[:n_layers], rest[n_layers:]
        layer = pl.program_id(0)
        g = jnp.zeros((tr, cols), f32)
        for l, gp_ref in enumerate(gp_refs):
            g_l = gp_ref[0].astype(f32)
            for k in range(1, n_parts):
                g_l = g_l + gp_ref[k].astype(f32)
            g = jnp.where(layer == l, g_l, g)
        nm = ADAM_B1 * m_ref[...] + (1.0 - ADAM_B1) * g
        nv = ADAM_B2 * v_ref[...] + (1.0 - ADAM_B2) * jnp.square(g)
        m_hat = nm / (1.0 - ADAM_B1 ** ADAM_STEP)
        v_hat = nv / (1.0 - ADAM_B2 ** ADAM_STEP)
        g_ref[...] = g
        d_ref[...] = -ADAM_LR * (m_hat / (jnp.sqrt(v_hat) + ADAM_EPS) + ADAM_WD * w_ref[...])
        nm_ref[...] = nm
        nv_ref[...] = nv

    row = pl.BlockSpec((tr, cols), lambda l, i: (l * nr + i, 0))

    def parts_spec(own):
        return pl.BlockSpec((n_parts, tr, cols),
                            lambda l, i: (0, jnp.where(l == own, i, jnp.where(l < own, 0, nr - 1)), 0))

    return pl.pallas_call(
        body, name=name, grid=(n_layers, nr), in_specs=[row, row, row] + [parts_spec(l) for l in range(n_layers)],
        out_specs=[row] * 4, out_shape=[jax.ShapeDtypeStruct(w.shape, f32)] * 4, compiler_params=_params(2),
    )(w, m, v, *g_parts)


def _mesh_pos():
    return lax.axis_index("x"), lax.axis_index("y"), lax.axis_index("c")


def _dev_index(p):
    return 4 * p[0] + 2 * p[1] + p[2]


class Exchange:
    def __init__(self, kind, arrays):
        self.kind, self.arrays, self.n = kind, list(arrays), len(arrays)
        self.specs = [pl.BlockSpec(memory_space=pl.ANY)] * self.n
        self.out_shapes = [jax.ShapeDtypeStruct(((N_DEV,) if kind == "gather" else ()) + a.shape, a.dtype)
                           for a in self.arrays]
        self.scratch = [pltpu.SemaphoreType.DMA((self.n, N_DEV - 1)), pltpu.SemaphoreType.DMA((self.n, N_DEV - 1)),
                        pltpu.SemaphoreType.DMA((self.n,))]

    def _copies(self, in_refs, out_refs, sems, with_arrivals):
        send_sems, recv_sems, local_sems = sems
        x, y, c = _mesh_pos()
        mine = _dev_index((x, y, c))

        def src(a, slab):
            return in_refs[a] if self.kind == "gather" else in_refs[a].at[slab]

        local = [pltpu.make_async_copy(src(a, mine), out_refs[a].at[mine], local_sems.at[a]) for a in range(self.n)]
        sends, arrivals = [], []
        for k in range(1, N_DEV):
            peer = (1 - x if k & 4 else x, 1 - y if k & 2 else y, 1 - c if k & 1 else c)
            theirs = _dev_index(peer)
            for a in range(self.n):
                to = dict(send_sem=send_sems.at[a, k - 1], recv_sem=recv_sems.at[a, k - 1], device_id=peer,
                          device_id_type=pl.DeviceIdType.MESH)
                sends.append(pltpu.make_async_remote_copy(src_ref=src(a, theirs), dst_ref=out_refs[a].at[mine], **to))
                if with_arrivals:
                    arrivals.append(pltpu.make_async_remote_copy(src_ref=src(a, theirs),
                                                                 dst_ref=out_refs[a].at[theirs], **to))
        return local, sends, arrivals

    def start(self, in_refs, out_refs, sems):
        local, sends, _ = self._copies(in_refs, out_refs, sems, False)
        for cp in local + sends:
            cp.start()

    def wait(self, in_refs, out_refs, sems):
        local, sends, arrivals = self._copies(in_refs, out_refs, sems, True)
        for cp in arrivals:
            cp.wait_recv()
        for cp in sends:
            cp.wait_send()
        for cp in local:
            cp.wait()

    def run_around(self, grid, in_refs, out_refs, sems, *, before):
        at = None
        for axis, size in enumerate(grid):
            hit = pl.program_id(axis) == (0 if before else size - 1)
            at = hit if at is None else at & hit

        @pl.when(at)
        def _():
            (self.start if before else self.wait)(in_refs, out_refs, sems)


def host_call(side, body, *, name, grid, in_specs, out_specs, out_shape, scratch_shapes, args, aliases):
    n_in, n_out = len(in_specs), len(out_specs)
    if side is None:
        kernel_body = body
    else:
        n = side.n
        in_specs, args = in_specs + side.specs, tuple(args) + tuple(side.arrays)
        out_specs, out_shape = out_specs + side.specs, out_shape + side.out_shapes
        scratch_shapes = scratch_shapes + side.scratch

        def kernel_body(*refs):
            ins, side_in = refs[:n_in], refs[n_in:n_in + n]
            outs, side_out = refs[n_in + n:n_in + n + n_out], refs[n_in + n + n_out:n_in + 2 * n + n_out]
            scratch, sems = refs[n_in + 2 * n + n_out:-3], refs[-3:]
            side.run_around(grid, side_in, side_out, sems, before=True)
            body(*ins, *outs, *scratch)
            side.run_around(grid, side_in, side_out, sems, before=False)

    outs = pl.pallas_call(
        kernel_body, name=name, grid=grid, in_specs=in_specs, out_specs=out_specs, out_shape=out_shape,
        scratch_shapes=scratch_shapes, input_output_aliases=aliases, compiler_params=_params(len(grid)),
    )(*args)
    return outs if side is None else (outs[:n_out], outs[n_out:])


def all_gather(blocks, *, name):
    n = len(blocks)
    any_spec = pl.BlockSpec(memory_space=pl.ANY)

    def body(*refs):
        ins, outs = refs[:n], refs[n:2 * n]
        send_sems, recv_sems, local_sems = refs[2 * n:]
        x, y, c = _mesh_pos()
        me, sibling = (x, y, c), (x, y, 1 - c)
        chips = [(1 - x, y), (x, 1 - y), (1 - x, 1 - y)]

        def copy(a, k, block, to, src=None):
            dst = outs[a].at[_dev_index(block)]
            return pltpu.make_async_remote_copy(
                src_ref=dst if src is None else src, dst_ref=dst, send_sem=send_sems.at[a, k],
                recv_sem=recv_sems.at[a, k], device_id=to, device_id_type=pl.DeviceIdType.MESH)

        mine = [pltpu.make_async_copy(ins[a], outs[a].at[_dev_index(me)], local_sems.at[a]) for a in range(n)]
        for cp in mine:
            cp.start()
        first = []
        for a in range(n):
            first.append(copy(a, 0, me, sibling, src=ins[a]))
            first += [copy(a, 1 + j, me, (*chip, c), src=ins[a]) for j, chip in enumerate(chips)]
        for cp in first:
            cp.start()
        passed = []
        for j, chip in enumerate(chips):
            for a in range(n):
                copy(a, 1 + j, (*chip, c), me).wait_recv()
                fwd = copy(a, 4 + j, (*chip, c), sibling)
                fwd.start()
                passed.append(fwd)
        for a in range(n):
            copy(a, 0, sibling, me).wait_recv()
            for j, chip in enumerate(chips):
                copy(a, 4 + j, (*chip, 1 - c), me).wait_recv()
        for cp in first + passed:
            cp.wait_send()
        for cp in mine:
            cp.wait()

    return pl.pallas_call(
        body, name=name, in_specs=[any_spec] * n, out_specs=[any_spec] * n,
        out_shape=[jax.ShapeDtypeStruct((N_DEV,) + b.shape, b.dtype) for b in blocks],
        scratch_shapes=[pltpu.SemaphoreType.DMA((n, 7)), pltpu.SemaphoreType.DMA((n, 7)),
                        pltpu.SemaphoreType.DMA((n,))],
    )(*blocks)


def exchange(jobs, *, name):
    total = sum(j.n for j in jobs)

    def body(*refs):
        ins, outs, sems = refs[:total], refs[total:2 * total], refs[2 * total:]
        pieces, off = [], 0
        for i, j in enumerate(jobs):
            pieces.append((ins[off:off + j.n], outs[off:off + j.n], sems[3 * i:3 * i + 3]))
            off += j.n
        for j, piece in zip(jobs, pieces):
            j.start(*piece)
        for j, piece in zip(jobs, pieces):
            j.wait(*piece)

    outs = pl.pallas_call(
        body, name=name, in_specs=[s for j in jobs for s in j.specs], out_specs=[s for j in jobs for s in j.specs],
        out_shape=[s for j in jobs for s in j.out_shapes], scratch_shapes=[s for j in jobs for s in j.scratch],
    )(*[a for j in jobs for a in j.arrays])
    split, off = [], 0
    for j in jobs:
        split.append(outs[off:off + j.n])
        off += j.n
    return split


def _rows128(arr):
    flat = arr.reshape(-1)
    rows = -(-flat.shape[0] // (8 * LANES)) * 8
    return jnp.pad(flat, (0, rows * LANES - flat.shape[0])).reshape(rows, LANES)


def _pad_lanes(row, width=LANES, at=0):
    return jnp.pad(row, (at, width - at - row.shape[0])).reshape(1, width)


def _w_in_sections(got):
    d = got.shape[1]
    wi = jnp.transpose(got, (1, 0, 2)).reshape(d, -1)
    w_main = jnp.concatenate([wi[:, :C_B], wi[:, C_Z:C_BA], wi[:, C_QKV:C_Z], wi[:, C_G:], wi[:, C_B:C_QKV]], axis=1)
    return w_main, jnp.pad(wi[:, C_BA:C_G], ((0, 0), (0, LANES - (C_G - C_BA))))


def _w_in_parts(gw_main, gw_ba):
    d = gw_main.shape[0]
    full = jnp.concatenate([gw_main[:, SEC_A:SEC_Z], gw_main[:, SEC_B:], gw_main[:, SEC_QKV:SEC_G],
                            gw_main[:, SEC_Z:SEC_QKV], gw_ba[:, :C_G - C_BA], gw_main[:, SEC_G:SEC_B]], axis=1)
    return jnp.transpose(full.reshape(d, N_DEV, -1), (1, 0, 2))


def kernel(x, positions, attn_norm, w_in, sgu_ln_g, sgu_ln_b, sgu_w, sgu_b, attn_sinks, dn_conv_w, dn_a_log, dn_dt_bias, dn_norm, w_branch, w_out, ffn_norm, w_gate_up, w_down, final_norm, loss_target, m_attn_norm, m_w_in, m_sgu_ln_g, m_sgu_ln_b, m_sgu_w, m_sgu_b, m_attn_sinks, m_dn_conv_w, m_dn_a_log, m_dn_dt_bias, m_dn_norm, m_w_branch, m_w_out, m_ffn_norm, m_w_gate_up, m_w_down, m_final_norm, v_attn_norm, v_w_in, v_sgu_ln_g, v_sgu_ln_b, v_sgu_w, v_sgu_b, v_attn_sinks, v_dn_conv_w, v_dn_a_log, v_dn_dt_bias, v_dn_norm, v_w_branch, v_w_out, v_ffn_norm, v_w_gate_up, v_w_down, v_final_norm):
    given = dict(locals())
    depth, d_model = attn_norm.shape
    s = x.shape[1]
    x2 = x.reshape(s, d_model)
    target = loss_target.reshape(s, d_model)
    posf = positions.reshape(s, 1).astype(f32)
    inv_freq = ROPE_THETA ** (-jnp.arange(0, ROPE_DIM, 2, dtype=f32) / ROPE_DIM)
    inv_head = jnp.concatenate([inv_freq, inv_freq, jnp.zeros((SWA_HD - ROPE_DIM,), f32)])
    tables = rope_tables(posf, jnp.tile(inv_head, LANES // SWA_HD).reshape(1, LANES), name="rope_tables")

    assert depth == 2, depth
    gathered = dict(zip([("w_in", 0), ("dn_conv_w", 0), ("dn_conv_w", 1)], all_gather(
        [w_in[0].astype(bf16), dn_conv_w[0], dn_conv_w[1]], name="gather_first")))
    riders = {"l0_in_proj": [("w_branch", 0), ("w_out", 0), ("w_down", 0)],
              "l0_deltanet": [("w_gate_up", 0), ("w_in", 1)],
              "l1_deltanet": [("w_branch", 1), ("w_out", 1), ("w_gate_up", 1), ("w_down", 1)]}

    def gathering(host, call, *args, **kw):
        if host not in riders:
            return call(*args, name=host, **kw)
        side = Exchange("gather", [given[n][l].astype(bf16) for n, l in riders[host]])
        out, got = call(*args, side=side, name=host, **kw)
        gathered.update(zip(riders[host], got))
        return out

    layers, saved = [], []
    h_in = x2
    for l in range(depth):
        t = f"l{l}_"
        w_main, w_ba = _w_in_sections(gathered["w_in", l])
        conv_full = jnp.transpose(gathered["dn_conv_w", l], (1, 0, 2)).reshape(DN_CONV, -1)
        p = dict(
            w_main=w_main, w_ba=w_ba, conv_w8=jnp.pad(conv_full, ((0, CONV_PAD - DN_CONV), (0, 0))),
            attn_norm=attn_norm[l].reshape(1, -1), ffn_norm=ffn_norm[l].reshape(1, -1),
            ln_g=sgu_ln_g[l].reshape(1, -1), ln_b=sgu_ln_b[l].reshape(1, -1), sgu_w=sgu_w[l], sgu_bt=sgu_b[l].T,
            sink_row=_pad_lanes(attn_sinks[l]), alog_row=_pad_lanes(dn_a_log[l], at=DN_HEADS),
            dtb_row=_pad_lanes(dn_dt_bias[l], at=DN_HEADS), norm_row=dn_norm[l].reshape(1, -1))
        layers.append(p)
        h = rmsnorm_fwd(h_in, p["attn_norm"], name=t + "attn_norm")
        proj = gathering(t + "in_proj", matmul, h, p["w_main"], "nn", bf16)
        p_ba = matmul(h, p["w_ba"], "nn", f32, name=t + "in_proj_ba")
        out_a = sgu_fwd(proj, p["ln_g"], p["ln_b"], p["sgu_w"], p["sgu_bt"], name=t + "sgu")
        out_b = swa_fwd(proj, p["sink_row"], tables, name=t + "swa")
        xc = conv_fwd(proj, p["conv_w8"], name=t + "dn_conv")
        out_c, states, tri = gathering(t + "deltanet", dn_fwd, xc, proj, p_ba, p["alog_row"], p["dtb_row"], p["norm_row"])
        p.update(w_branch=jnp.transpose(gathered["w_branch", l], (1, 2, 0, 3)).reshape(3, MIX, d_model),
                 w_out=gathered["w_out", l].reshape(d_model, d_model),
                 w_gu=gathered["w_gate_up", l],
                 w_down=gathered["w_down", l].reshape(N_DEV // 2, -1, d_model))
        merged = merge_fwd(out_a, out_b, out_c, proj, p["w_branch"], name=t + "merge")
        x_mid = matmul(merged, p["w_out"], "nn", f32, residual=h_in, tn=1024, name=t + "out_proj")
        h2 = rmsnorm_fwd(x_mid, p["ffn_norm"], name=t + "ffn_norm")
        gate, up, act = gate_up_swiglu(h2, p["w_gu"], name=t + "gate_up")
        x_out = matmul(act, p["w_down"], "nn", f32, residual=x_mid, group="k", tn=1024, name=t + "down")
        saved.append(dict(x_in=h_in, h=h, proj=proj, p_ba=p_ba, out_a=out_a, out_b=out_b, out_c=out_c, xc=xc,
                          states=states, tri=tri, merged=merged, x_mid=x_mid, h2=h2, gate=gate, up=up, act=act))
        h_in = x_out

    dx, d_final_norm, loss_row = loss_head(h_in, final_norm.reshape(1, -1), target, name="loss_head")
    loss = lax.psum(loss_row[0, 0], MESH_AXES)

    shard_names = ["w_in", "dn_conv_w", "w_branch", "w_out", "w_gate_up", "w_down"]
    rep_names = ["attn_norm", "sgu_ln_g", "sgu_ln_b", "sgu_w", "sgu_b", "attn_sinks", "dn_a_log", "dn_dt_bias",
                 "dn_norm", "ffn_norm"]
    parts, received, per_layer = {}, {}, []
    senders = {"l0_b_swa": [("w_gate_up", 1), ("w_down", 1), ("w_out", 1), ("w_branch", 1)],
               "l0_b_deltanet": [("w_in", 1), ("w_gate_up", 0)],
               "l0_b_dn_conv": [("w_out", 0), ("w_branch", 0), ("w_down", 0)],
               "l0_b_in_proj_dx": [("w_in", 0)]}

    def scattering(host, call, *args, **kw):
        if host not in senders:
            return call(*args, name=host, **kw)
        out, got = call(*args, side=Exchange("scatter", [parts[key] for key in senders[host]]), name=host, **kw)
        received.update(zip(senders[host], got))
        return out

    for l in reversed(range(depth)):
        p, sv, t = layers[l], saved[l], f"l{l}_b_"
        d_gate, d_up = down_dx_swiglu(dx, p["w_down"], sv["gate"], sv["up"], name=t + "down_dx")
        gw_down = matmul(sv["act"], dx, "tn", bf16, group="m", tk=2048, tn=512, name=t + "down_dw")
        gw_gu = jnp.concatenate([matmul(sv["h2"], d_half, "tn", bf16, group="n", tk=2048, name=t + "gate_up_dw" + tag)
                                 for d_half, tag in ((d_gate, "_gate"), (d_up, "_up"))], axis=0)
        half = N_DEV // 2
        d_h2 = matmul(d_gate, p["w_gu"][:half], "nt", f32, group="k", tn=1024, name=t + "gate_up_dx_gate")
        d_h2 = matmul(d_up, p["w_gu"][half:], "nt", bf16, group="k", tn=1024, residual=d_h2, name=t + "gate_up_dx_up")
        dx_mid, g_ffn = rmsnorm_bwd(sv["x_mid"], p["ffn_norm"], d_h2, dx, name=t + "ffn_norm")
        d_merged = matmul(dx_mid, p["w_out"], "nt", f32, tn=1024, name=t + "out_proj_dx")
        gw_out = matmul(sv["merged"], dx_mid, "tn", bf16, tk=2048, tn=1024, name=t + "out_proj_dw")
        dproj = lax.empty((s, W_MAIN), bf16)
        dproj, d_a, d_b, d_c, gw_branch = merge_bwd(sv["out_a"], sv["out_b"], sv["out_c"], sv["proj"], p["w_branch"],
                                                   d_merged, dproj, name=t + "merge")
        parts.update({("w_gate_up", l): gw_gu, ("w_down", l): gw_down.reshape(N_DEV, -1, d_model),
                      ("w_out", l): gw_out.reshape(N_DEV, -1, d_model),
                      ("w_branch", l): jnp.transpose(gw_branch.reshape(3, MIX, N_DEV, -1), (2, 0, 1, 3)).astype(bf16)})
        dproj, g_ln_g, g_ln_b, g_sgu_w, g_sgu_bt = sgu_bwd(sv["proj"], p["ln_g"], p["ln_b"], p["sgu_w"], p["sgu_bt"],
                                                         d_a, dproj, name=t + "sgu")
        dproj, g_sink = scattering(t + "swa", swa_bwd, sv["proj"], p["sink_row"], tables, d_b, dproj)
        dproj, dxc, dba, g_alog, g_dtb, g_dnorm = scattering(
            t + "deltanet", dn_bwd, sv["xc"], sv["proj"], sv["p_ba"], p["alog_row"], p["dtb_row"], p["norm_row"],
            sv["states"], sv["tri"], d_c, dproj)
        dproj, g_conv8 = scattering(t + "dn_conv", conv_bwd, sv["proj"], p["conv_w8"], dxc, dproj)
        gw_main = matmul(sv["h"], dproj, "tn", bf16, tk=2048, name=t + "in_proj_dw")
        gw_ba = matmul(sv["h"], dba, "tn", bf16, tk=2048, name=t + "in_proj_ba_dw")
        parts["w_in", l] = _w_in_parts(gw_main, gw_ba)
        d_h = scattering(t + "in_proj_dx", matmul, dproj, p["w_main"], "nt", f32, tk=2304, tn=1024)
        d_h = matmul(dba, p["w_ba"], "nt", bf16, residual=d_h, name=t + "in_proj_ba_dx")
        dx, g_attn = rmsnorm_bwd(sv["x_in"], p["attn_norm"], d_h, dx_mid, name=t + "attn_norm")
        per_layer.append(dict(
            dn_conv_w=jnp.transpose(g_conv8[:DN_CONV].reshape(DN_CONV, N_DEV, -1), (1, 0, 2)),
            attn_norm=g_attn, sgu_ln_g=g_ln_g, sgu_ln_b=g_ln_b, sgu_w=g_sgu_w, sgu_b=g_sgu_bt.T,
            attn_sinks=g_sink[0, :SWA_HEADS], dn_a_log=g_alog[0, DN_HEADS:2 * DN_HEADS],
            dn_dt_bias=g_dtb[0, DN_HEADS:2 * DN_HEADS], dn_norm=g_dnorm, ffn_norm=g_ffn))
    per_layer.reverse()

    conv_parts = jnp.concatenate([pp["dn_conv_w"] for pp in per_layer], axis=1)
    rep_grads = {n: jnp.stack([pp[n].reshape(given[n].shape[1:]) for pp in per_layer]) for n in rep_names}
    rep_grads["final_norm"] = d_final_norm[0]
    rep_names = rep_names + ["final_norm"]
    rep_rows = [_rows128(given[n]).shape[0] for n in rep_names]
    pad_rows = -sum(rep_rows) % 16

    def small_rows(values):
        pieces = [_rows128(values[n]) for n in rep_names]
        return jnp.concatenate(pieces + [jnp.zeros((pad_rows, LANES), f32)], axis=0)

    (got_conv,), (small_all,) = exchange(
        [Exchange("scatter", [conv_parts]), Exchange("gather", [small_rows(rep_grads)])], name="exchange_last")

    results = [{}, {}, {}, {}]
    for n in shard_names:
        shp = given[n].shape
        two = (-1, shp[-1])
        by_layer = [got_conv] if n == "dn_conv_w" else [received[n, l].reshape(N_DEV, -1, shp[-1]) for l in range(depth)]
        outs = adamw(given[n].reshape(two), given["m_" + n].reshape(two), given["v_" + n].reshape(two), by_layer,
                     name="adamw_" + n)
        for res, val in zip(results, outs):
            res[n] = val.reshape(shp)
    outs = adamw(small_rows(given), small_rows({n: given["m_" + n] for n in rep_names}),
                 small_rows({n: given["v_" + n] for n in rep_names}), [small_all], name="adamw_replicated")
    for res, val in zip(results, outs):
        row = 0
        for n, nr in zip(rep_names, rep_rows):
            res[n] = val[row:row + nr].reshape(-1)[:given[n].size].reshape(given[n].shape)
            row += nr
    order = ["attn_norm", "w_in", "sgu_ln_g", "sgu_ln_b", "sgu_w", "sgu_b", "attn_sinks", "dn_conv_w", "dn_a_log",
             "dn_dt_bias", "dn_norm", "w_branch", "w_out", "ffn_norm", "w_gate_up", "w_down", "final_norm"]
    return (loss, dx.reshape(x.shape), *[res[n] for res in results for n in order])
```

```python
import functools

import jax
import jax.numpy as jnp
from jax import lax
from jax.experimental import pallas as pl
from jax.experimental.pallas import tpu as pltpu

f32 = jnp.float32
bf16 = jnp.bfloat16

N_DEV = 8
MESH_AXES = ("x", "y", "c")
NORM_EPS = 1e-6
MIX = 512
SGU_GROUPS, SGU_CHUNK = 4, 128
SWA_HEADS, SWA_KV, SWA_HD, WINDOW = 8, 2, 64, 128
ROPE_THETA, ROPE_DIM = 500000.0, 16
DN_HEADS, DN_HD, DN_CONV, DN_CHUNK = 4, 128, 4, 64
ADAM_LR, ADAM_B1, ADAM_B2, ADAM_EPS, ADAM_WD, ADAM_STEP = 0.001, 0.9, 0.999, 1e-08, 0.01, 10

LANES = 128
VMEM_LIMIT = 56 * 1024 * 1024

SEC_A, SEC_Z, SEC_QKV, SEC_G, SEC_B = 0, 1024, 1536, 3072, 6144
W_MAIN = 6912
C_B, C_QKV, C_Z, C_BA, C_G = 1024, 1792, 3328, 3840, 3848


def _params(n_axes, **kw):
    return pltpu.CompilerParams(dimension_semantics=("arbitrary",) * n_axes, vmem_limit_bytes=VMEM_LIMIT, **kw)


def _tile(n, target, mult=LANES):
    if n <= target:
        return n
    best = None
    for t in range(mult, target + 1, mult):
        if n % t == 0:
            best = t
    assert best is not None, (n, target, mult)
    return best


def _dg(a, b, ca, cb):
    return lax.dot_general(a.astype(bf16), b.astype(bf16), (((ca,), (cb,)), ((), ())), preferred_element_type=f32)


def _dg3(a, b, ca, cb):
    a_hi, b_hi = a.astype(bf16), b.astype(bf16)
    a_lo, b_lo = (a - a_hi.astype(f32)).astype(bf16), (b - b_hi.astype(f32)).astype(bf16)

    def dot(p, q):
        return lax.dot_general(p, q, (((ca,), (cb,)), ((), ())), preferred_element_type=f32)

    return dot(a_hi, b_hi) + (dot(a_hi, b_lo) + dot(a_lo, b_hi))


def _differentiable_dot(core):
    @functools.partial(jax.custom_vjp, nondiff_argnums=(2, 3))
    def dot(a, b, ca, cb):
        return core(a, b, ca, cb)

    def fwd(a, b, ca, cb):
        return core(a, b, ca, cb), (a, b)

    def bwd(ca, cb, res, ct):
        a, b = res
        da = core(ct, b, 1, 1 - cb) if ca == 1 else core(b, ct, 1 - cb, 1)
        db = core(a, ct, 1 - ca, 0) if cb == 0 else core(ct, a, 0, 1 - ca)
        return da, db

    dot.defvjp(fwd, bwd)
    return dot


bdot = _differentiable_dot(_dg)
_hdot = _differentiable_dot(_dg3)


def hdot(a, b, ca=1, cb=0):
    return _hdot(a, b, ca, cb)


@functools.partial(jax.custom_vjp, nondiff_argnums=(1,))
def lroll(x, shift):
    return pltpu.roll(x, shift, 1)


def _lroll_fwd(x, shift):
    return pltpu.roll(x, shift, 1), None


def _lroll_bwd(shift, _, ct):
    return (pltpu.roll(ct, ct.shape[1] - shift, 1),)


lroll.defvjp(_lroll_fwd, _lroll_bwd)


@functools.partial(jax.custom_vjp, nondiff_argnums=(1,))
def tri_inv(low, nil):
    n = low.shape[0]
    row = lax.broadcasted_iota(jnp.int32, (n, n), 0)
    col = lax.broadcasted_iota(jnp.int32, (n, n), 1)
    eye = (row == col).astype(f32)
    m = -low
    p = eye + m
    span = 2
    while span < nil:
        dot = hdot if span == 2 else (lambda a, b: bdot(a, b, 1, 0))
        m = dot(m, m)
        p = p + dot(p, m)
        span *= 2
    return p


def _tri_inv_fwd(low, nil):
    t = tri_inv(low, nil)
    return t, t


def _tri_inv_bwd(nil, t, dt):
    return (-bdot(t, bdot(dt, t, 1, 1), 0, 0),)


tri_inv.defvjp(_tri_inv_fwd, _tri_inv_bwd)


@jax.custom_vjp
def tri_inv_known(low, t):
    return t


def _tri_inv_known_fwd(low, t):
    return t, t


def _tri_inv_known_bwd(t, dt):
    return _tri_inv_bwd(None, t, dt) + (jnp.zeros_like(t),)


tri_inv_known.defvjp(_tri_inv_known_fwd, _tri_inv_known_bwd)


def _sigmoid(x):
    return 1.0 / (1.0 + jnp.exp(-x))


def _rms(x, g):
    return x * lax.rsqrt(jnp.mean(x * x, axis=-1, keepdims=True) + NORM_EPS) * g


def _lane_col(x, lane_idx):
    lane = lax.broadcasted_iota(jnp.int32, x.shape, 1)
    return jnp.sum(jnp.where(lane == lane_idx, x, 0.0), axis=1, keepdims=True)


def matmul(a, b, mode, out_dtype, *, residual=None, group=None, side=None, tm=1024, tn=768, tk=1024, name):
    dims = {"a": ("m", "k") if mode != "tn" else ("k", "m"),
            "b": {"nn": ("k", "n"), "nt": ("n", "k"), "tn": ("k", "n")}[mode], "o": ("m", "n")}
    full, groups = {}, 1
    for arr, key in ((a, "a"), (b, "b")):
        grouped = group in dims[key]
        if grouped:
            groups = arr.shape[0]
        full[dims[key][0]], full[dims[key][1]] = arr.shape[1:] if grouped else arr.shape
    want = {"m": tm, "n": tn, "k": tk}
    per_step = min(groups, 4) if group == "k" else 1
    tiles = {d: full[d] if d == group else _tile(full[d], want[d]) for d in "mnk"}
    steps = {d: groups // per_step if d == group else full[d] // tiles[d] for d in "mnk"}

    def spec(key):
        d0, d1 = dims[key]

        def index(i, j, kk):
            at = {"m": i, "n": j, "k": kk}
            if group in (d0, d1):
                return (at[group], 0 if d0 == group else at[d0], 0 if d1 == group else at[d1])
            return (at[d0], at[d1])

        block = (tiles[d0], tiles[d1])
        if group in (d0, d1):
            block = ((per_step if group == "k" else None),) + block
        return pl.BlockSpec(block, index)

    ca, cb = {"nn": (1, 0), "nt": (1, 1), "tn": (0, 0)}[mode]
    nk = steps["k"]
    o_spec = spec("o")
    out_shape = (groups, full["m"], full["n"]) if group in ("m", "n") else (full["m"], full["n"])
    has_res = residual is not None

    def product(a_ref, b_ref):
        if group != "k":
            return _dg(a_ref[...], b_ref[...], ca, cb)
        total = _dg(a_ref[0], b_ref[0], ca, cb)
        for g in range(1, per_step):
            total = total + _dg(a_ref[g], b_ref[g], ca, cb)
        return total

    def body(a_ref, b_ref, *rest):
        r_ref = rest[0] if has_res else None
        o_ref = rest[1 if has_res else 0]

        def emit(acc):
            o_ref[...] = (acc + r_ref[...] if has_res else acc).astype(out_dtype)

        if nk == 1:
            emit(product(a_ref, b_ref))
            return
        acc_ref = rest[-1]
        kk = pl.program_id(2)

        @pl.when(kk == 0)
        def _():
            acc_ref[...] = jnp.zeros_like(acc_ref)

        acc_ref[...] += product(a_ref, b_ref)

        @pl.when(kk == nk - 1)
        def _():
            emit(acc_ref[...])

    res = host_call(
        side, body, name=name, grid=(steps["m"], steps["n"], nk),
        in_specs=[spec("a"), spec("b")] + ([o_spec] if has_res else []), out_specs=[o_spec],
        out_shape=[jax.ShapeDtypeStruct(out_shape, out_dtype)],
        scratch_shapes=[pltpu.VMEM((tiles["m"], tiles["n"]), f32)] if nk > 1 else [], aliases={},
        args=(a, b) + ((residual,) if has_res else ()))
    return res[0] if side is None else (res[0][0], res[1])


def rmsnorm_fwd(x, g_row, *, name):
    s, d = x.shape
    ts = _tile(s, 1024, 16)

    def body(x_ref, g_ref, o_ref):
        o_ref[...] = _rms(x_ref[...], g_ref[...]).astype(bf16)

    return pl.pallas_call(
        body, name=name, grid=(s // ts,),
        in_specs=[pl.BlockSpec((ts, d), lambda i: (i, 0)), pl.BlockSpec((1, d), lambda i: (0, 0))],
        out_specs=pl.BlockSpec((ts, d), lambda i: (i, 0)), out_shape=jax.ShapeDtypeStruct((s, d), bf16),
        compiler_params=_params(1),
    )(x, g_row)


def rmsnorm_bwd(x, g_row, dh, dres, *, name):
    s, d = x.shape
    ts = _tile(s, 1024, 16)

    def body(x_ref, g_ref, dh_ref, dres_ref, dx_ref, dg_ref):
        @pl.when(pl.program_id(0) == 0)
        def _():
            dg_ref[...] = jnp.zeros_like(dg_ref)

        _, vjp = jax.vjp(_rms, x_ref[...], g_ref[...])
        dx, dg = vjp(dh_ref[...].astype(f32))
        dx_ref[...] = dx + dres_ref[...]
        dg_ref[...] += dg

    row = pl.BlockSpec((ts, d), lambda i: (i, 0))
    vec = pl.BlockSpec((1, d), lambda i: (0, 0))
    return pl.pallas_call(
        body, name=name, grid=(s // ts,), in_specs=[row, vec, row, row], out_specs=[row, vec],
        out_shape=[jax.ShapeDtypeStruct((s, d), f32), jax.ShapeDtypeStruct((1, d), f32)],
        compiler_params=_params(1),
    )(x, g_row, dh, dres)


def loss_head(x, g_row, target, *, name):
    s, d = x.shape
    ts = _tile(s, 512, 16)

    def body(x_ref, g_ref, t_ref, dx_ref, dg_ref, loss_ref):
        @pl.when(pl.program_id(0) == 0)
        def _():
            dg_ref[...] = jnp.zeros_like(dg_ref)
            loss_ref[...] = jnp.zeros_like(loss_ref)

        y, vjp = jax.vjp(_rms, x_ref[...], g_ref[...])
        err = y - t_ref[...]
        dx, dg = vjp(err * (1.0 / d))
        dx_ref[...] = dx
        dg_ref[...] += dg
        loss_ref[...] += 0.5 * jnp.sum(jnp.sum(err * err, axis=1, keepdims=True) * (1.0 / d), axis=0, keepdims=True)

    row = pl.BlockSpec((ts, d), lambda i: (i, 0))
    vec = pl.BlockSpec((1, d), lambda i: (0, 0))
    one = pl.BlockSpec((1, LANES), lambda i: (0, 0))
    return pl.pallas_call(
        body, name=name, grid=(s // ts,), in_specs=[row, vec, row], out_specs=[row, vec, one],
        out_shape=[jax.ShapeDtypeStruct((s, d), f32), jax.ShapeDtypeStruct((1, d), f32),
                   jax.ShapeDtypeStruct((1, LANES), f32)],
        compiler_params=_params(1),
    )(x, g_row, target)


def _sgu_chunk(p_a, ln_g, ln_b, w, b_t):
    t = SGU_CHUNK
    u = jax.nn.gelu(p_a[:, :MIX])
    v = jax.nn.gelu(p_a[:, MIX:])
    vc = v - jnp.mean(v, axis=-1, keepdims=True)
    vn = vc * lax.rsqrt(jnp.mean(vc * vc, axis=-1, keepdims=True) + NORM_EPS) * ln_g + ln_b
    causal = lax.broadcasted_iota(jnp.int32, (t, t), 0) >= lax.broadcasted_iota(jnp.int32, (t, t), 1)
    outs = []
    for g in range(SGU_GROUPS):
        sl = slice(g * LANES, (g + 1) * LANES)
        mixed = bdot(jnp.where(causal, w[g], 0.0), vn[:, sl], 1, 0) + b_t[:, g:g + 1]
        outs.append(u[:, sl] * mixed)
    return jnp.concatenate(outs, axis=1)


def _sgu_specs(s, ts):
    return [pl.BlockSpec((ts, 2 * MIX), lambda i: (i, SEC_A // (2 * MIX))),
            pl.BlockSpec((1, MIX), lambda i: (0, 0)), pl.BlockSpec((1, MIX), lambda i: (0, 0)),
            pl.BlockSpec((SGU_GROUPS, SGU_CHUNK, SGU_CHUNK), lambda i: (0, 0, 0)),
            pl.BlockSpec((SGU_CHUNK, SGU_GROUPS), lambda i: (0, 0))]


def sgu_fwd(proj, ln_g, ln_b, w, b_t, *, name):
    s = proj.shape[0]
    ts = _tile(s, 512)
    n_chunk = ts // SGU_CHUNK

    def body(p_ref, g_ref, b_ref, w_ref, bt_ref, o_ref):
        def step(c, carry):
            rows = pl.ds(pl.multiple_of(c * SGU_CHUNK, SGU_CHUNK), SGU_CHUNK)
            o_ref[rows, :] = _sgu_chunk(p_ref[rows, :].astype(f32), g_ref[...], b_ref[...], w_ref[...], bt_ref[...]).astype(bf16)
            return carry
        lax.fori_loop(0, n_chunk, step, 0)

    return pl.pallas_call(
        body, name=name, grid=(s // ts,), in_specs=_sgu_specs(s, ts),
        out_specs=pl.BlockSpec((ts, MIX), lambda i: (i, 0)), out_shape=jax.ShapeDtypeStruct((s, MIX), bf16),
        compiler_params=_params(1),
    )(proj, ln_g, ln_b, w, b_t)


def sgu_bwd(proj, ln_g, ln_b, w, b_t, d_out, dproj, *, name):
    s = proj.shape[0]
    ts = _tile(s, 512)
    n_chunk = ts // SGU_CHUNK

    def body(p_ref, g_ref, b_ref, w_ref, bt_ref, do_ref, _, dp_ref, dg_ref, db_ref, dw_ref, dbt_ref):
        @pl.when(pl.program_id(0) == 0)
        def _():
            dg_ref[...] = jnp.zeros_like(dg_ref)
            db_ref[...] = jnp.zeros_like(db_ref)
            dw_ref[...] = jnp.zeros_like(dw_ref)
            dbt_ref[...] = jnp.zeros_like(dbt_ref)

        def step(c, carry):
            rows = pl.ds(pl.multiple_of(c * SGU_CHUNK, SGU_CHUNK), SGU_CHUNK)
            _, vjp = jax.vjp(_sgu_chunk, p_ref[rows, :].astype(f32), g_ref[...], b_ref[...], w_ref[...], bt_ref[...])
            dp, dg, db, dw, dbt = vjp(do_ref[rows, :])
            dp_ref[rows, :] = dp.astype(bf16)
            dg_ref[...] += dg
            db_ref[...] += db
            dw_ref[...] += dw
            dbt_ref[...] += dbt
            return carry
        lax.fori_loop(0, n_chunk, step, 0)

    specs = _sgu_specs(s, ts)
    return pl.pallas_call(
        body, name=name, grid=(s // ts,),
        in_specs=specs + [pl.BlockSpec((ts, MIX), lambda i: (i, 0)), pl.BlockSpec(memory_space=pl.ANY)],
        out_specs=[specs[0], specs[1], specs[2], specs[3], specs[4]],
        out_shape=[jax.ShapeDtypeStruct(dproj.shape, bf16), jax.ShapeDtypeStruct((1, MIX), f32),
                   jax.ShapeDtypeStruct((1, MIX), f32), jax.ShapeDtypeStruct(w.shape, f32),
                   jax.ShapeDtypeStruct(b_t.shape, f32)],
        input_output_aliases={6: 0}, compiler_params=_params(1),
    )(proj, ln_g, ln_b, w, b_t, d_out, dproj)


def rope_tables(posf, inv_freq, *, name):
    s = posf.shape[0]
    ts = _tile(s, 1024, 8)
    half = ROPE_DIM // 2

    def body(pos_ref, inv_ref, o_ref):
        d = lax.broadcasted_iota(jnp.int32, (1, LANES), 1) % SWA_HD
        ang = pos_ref[...] * inv_ref[...]
        sin = jnp.sin(ang)
        o_ref[0] = jnp.cos(ang)
        o_ref[1] = jnp.where(d < half, sin, 0.0)
        o_ref[2] = jnp.where((d >= half) & (d < ROPE_DIM), sin, 0.0)

    return pl.pallas_call(
        body, name=name, grid=(s // ts,),
        in_specs=[pl.BlockSpec((ts, 1), lambda i: (i, 0)), pl.BlockSpec((1, LANES), lambda i: (0, 0))],
        out_specs=pl.BlockSpec((3, ts, LANES), lambda i: (0, i, 0)), out_shape=jax.ShapeDtypeStruct((3, s, LANES), f32),
        compiler_params=_params(1),
    )(posf, inv_freq)


def _rope(x, table):
    w = x.shape[1]
    half = ROPE_DIM // 2
    c, lo, hi = (jnp.concatenate([table[i]] * (w // LANES), axis=1) for i in range(3))
    return x * c - lroll(x, w - half) * lo + lroll(x, half) * hi


def _swa_block(q, kp, kc, vp, vc, sink_row, table_q, table_p, prev_ok, wide):
    t = WINDOW
    q = _rope(q, table_q) * (SWA_HD ** -0.5)
    keys = jnp.concatenate([_rope(kp, table_p), _rope(kc, table_q)], axis=0)
    vals = jnp.concatenate([vp, vc], axis=0)
    own = lax.broadcasted_iota(jnp.int32, (t, t), 0) >= lax.broadcasted_iota(jnp.int32, (t, t), 1)
    lane_half = lax.broadcasted_iota(jnp.int32, (t, LANES), 1) // SWA_HD
    group = SWA_HEADS // SWA_KV
    slabs = []
    for pair in range(SWA_HEADS // 2):
        q_pair = q[:, pair * LANES:(pair + 1) * LANES]
        acc = jnp.zeros((t, LANES), f32)
        for half in range(2):
            h = 2 * pair + half
            kv = h // group
            qm = jnp.where(lane_half == half, q_pair, 0.0)
            if half != kv:
                qm = lroll(qm, SWA_HD)
            if wide:
                both = bdot(qm, keys, 1, 1)
                s_prev, s_own = both[:, :t], both[:, t:]
            else:
                s_prev, s_own = bdot(qm, keys[:t], 1, 1), bdot(qm, keys[t:], 1, 1)
            logits = jnp.where(own, s_own, jnp.where(prev_ok, s_prev, -1e30))
            sink = _lane_col(sink_row, h)
            m = lax.stop_gradient(jnp.maximum(jnp.max(logits, axis=1, keepdims=True), sink))
            p = jnp.exp(logits - m)
            probs = p * (1.0 / (jnp.sum(p, axis=1, keepdims=True) + jnp.exp(sink - m)))
            p_prev, p_own = jnp.where(own, 0.0, probs), jnp.where(own, probs, 0.0)
            if wide:
                o = bdot(jnp.concatenate([p_prev, p_own], axis=1), vals, 1, 0)
            else:
                o = bdot(p_prev, vals[:t], 1, 0) + bdot(p_own, vals[t:], 1, 0)
            o = jnp.where(lane_half == kv, o, 0.0)
            if half != kv:
                o = lroll(o, SWA_HD)
            acc = acc + o
        slabs.append(acc)
    return jnp.concatenate(slabs, axis=1)


def _swa_in_specs(nc, clamp):
    t = WINDOW
    qb, kb, vb = SEC_B // MIX, (SEC_B + MIX) // LANES, (SEC_B + MIX + LANES) // LANES

    def cur(i):
        return jnp.minimum(i, nc - 1) if clamp else i

    def prev(i):
        return jnp.maximum(cur(i) - 1, 0)

    return [pl.BlockSpec((t, MIX), lambda i: (cur(i), qb)),
            pl.BlockSpec((t, LANES), lambda i: (prev(i), kb)), pl.BlockSpec((t, LANES), lambda i: (cur(i), kb)),
            pl.BlockSpec((t, LANES), lambda i: (prev(i), vb)), pl.BlockSpec((t, LANES), lambda i: (cur(i), vb)),
            pl.BlockSpec((1, LANES), lambda i: (0, 0)),
            pl.BlockSpec((3, t, LANES), lambda i: (0, cur(i), 0)), pl.BlockSpec((3, t, LANES), lambda i: (0, prev(i), 0))]


def swa_fwd(proj, sink_row, tables, *, name):
    s = proj.shape[0]
    nc = s // WINDOW

    def body(q_ref, kp_ref, kc_ref, vp_ref, vc_ref, sink_ref, tq_ref, tp_ref, o_ref):
        prev_ok = pl.program_id(0) > 0
        blocks = [r[...].astype(f32) for r in (q_ref, kp_ref, kc_ref, vp_ref, vc_ref)]
        o_ref[...] = _swa_block(*blocks, sink_ref[...],
                                tq_ref[...], tp_ref[...], prev_ok, True).astype(bf16)

    return pl.pallas_call(
        body, name=name, grid=(nc,), in_specs=_swa_in_specs(nc, False),
        out_specs=pl.BlockSpec((WINDOW, MIX), lambda i: (i, 0)), out_shape=jax.ShapeDtypeStruct((s, MIX), bf16),
        compiler_params=_params(1),
    )(proj, proj, proj, proj, proj, sink_row, tables, tables)


def swa_bwd(proj, sink_row, tables, d_out, dproj, *, side=None, name):
    s = proj.shape[0]
    nc = s // WINDOW
    t = WINDOW

    def body(q_ref, kp_ref, kc_ref, vp_ref, vc_ref, sink_ref, tq_ref, tp_ref, do_ref, _,
             dp_ref, dsink_ref, cq_ref, ck_ref, cv_ref):
        i = pl.program_id(0)

        @pl.when(i == 0)
        def _():
            dsink_ref[...] = jnp.zeros_like(dsink_ref)

        def write(dk_prev, dv_prev):
            dp_ref[:, :MIX] = cq_ref[...].astype(bf16)
            dp_ref[:, MIX:MIX + LANES] = (ck_ref[...] + dk_prev).astype(bf16)
            dp_ref[:, MIX + LANES:] = (cv_ref[...] + dv_prev).astype(bf16)

        @pl.when(i < nc)
        def _():
            fn = functools.partial(_swa_block, table_q=tq_ref[...], table_p=tp_ref[...], prev_ok=i > 0, wide=False)
            blocks = [r[...].astype(f32) for r in (q_ref, kp_ref, kc_ref, vp_ref, vc_ref)]
            _, vjp = jax.vjp(fn, *blocks, sink_ref[...])
            dq, dkp, dkc, dvp, dvc, dsink = vjp(do_ref[...])
            dsink_ref[...] += dsink

            @pl.when(i > 0)
            def _():
                write(dkp, dvp)

            cq_ref[...] = dq
            ck_ref[...] = dkc
            cv_ref[...] = dvc

        @pl.when(i == nc)
        def _():
            write(0.0, 0.0)

    return host_call(
        side, body, name=name, grid=(nc + 1,),
        in_specs=_swa_in_specs(nc, True) + [pl.BlockSpec((t, MIX), lambda i: (jnp.minimum(i, nc - 1), 0)),
                                            pl.BlockSpec(memory_space=pl.ANY)],
        out_specs=[pl.BlockSpec((t, MIX + 2 * LANES), lambda i: (jnp.maximum(i - 1, 0), SEC_B // (MIX + 2 * LANES))),
                   pl.BlockSpec((1, LANES), lambda i: (0, 0))],
        out_shape=[jax.ShapeDtypeStruct(dproj.shape, bf16), jax.ShapeDtypeStruct((1, LANES), f32)],
        scratch_shapes=[pltpu.VMEM((t, MIX), f32), pltpu.VMEM((t, LANES), f32), pltpu.VMEM((t, LANES), f32)],
        aliases={9: 0}, args=(proj, proj, proj, proj, proj, sink_row, tables, tables, d_out, dproj))


CONV_PAD = 16


def _conv_taps(xp, rows):
    off = CONV_PAD - (DN_CONV - 1)
    return [xp[off + i:off + i + rows] for i in range(DN_CONV)]


def _conv_pre(taps, w):
    pre = taps[0] * w[0:1]
    for i in range(1, DN_CONV):
        pre = pre + taps[i] * w[i:i + 1]
    return pre


def conv_fwd(proj, conv_w8, *, name):
    s = proj.shape[0]
    wq = 3 * MIX
    ts = _tile(s, 512)
    nb = ts // CONV_PAD

    def body(x_ref, prev_ref, w_ref, o_ref):
        prev = jnp.where(pl.program_id(0) > 0, prev_ref[...].astype(f32), 0.0)
        pre = _conv_pre(_conv_taps(jnp.concatenate([prev, x_ref[...].astype(f32)], axis=0), ts), w_ref[...])
        o_ref[...] = pre * _sigmoid(pre)

    return pl.pallas_call(
        body, name=name, grid=(s // ts,),
        in_specs=[pl.BlockSpec((ts, wq), lambda i: (i, SEC_QKV // wq)),
                  pl.BlockSpec((CONV_PAD, wq), lambda i: (jnp.maximum(i * nb - 1, 0), SEC_QKV // wq)),
                  pl.BlockSpec((CONV_PAD, wq), lambda i: (0, 0))],
        out_specs=pl.BlockSpec((ts, wq), lambda i: (i, 0)), out_shape=jax.ShapeDtypeStruct((s, wq), f32),
        compiler_params=_params(1),
    )(proj, proj, conv_w8)


def conv_bwd(proj, conv_w8, dxc, dproj, *, side=None, name):
    s = proj.shape[0]
    wq = 3 * MIX
    ts = _tile(s, 512)
    nb = ts // CONV_PAD
    nt = s // ts
    last_blk = s // CONV_PAD - 1

    def body(x_ref, prev_ref, next_ref, w_ref, d_ref, dnext_ref, _, dp_ref, dw_ref):
        i = pl.program_id(0)

        @pl.when(i == 0)
        def _():
            dw_ref[...] = jnp.zeros_like(dw_ref)

        w = w_ref[...]
        prev = jnp.where(i > 0, prev_ref[...].astype(f32), 0.0)
        more = i < nt - 1
        xp = jnp.concatenate([prev, x_ref[...].astype(f32), jnp.where(more, next_ref[...].astype(f32), 0.0)], axis=0)
        taps = _conv_taps(xp, ts + CONV_PAD)
        pre = _conv_pre(taps, w)
        sig = _sigmoid(pre)
        dxc_ext = jnp.concatenate([d_ref[...], jnp.where(more, dnext_ref[...], 0.0)], axis=0)
        dpre = dxc_ext * sig * (1.0 + pre * (1.0 - sig))
        d_raw = jnp.zeros((ts, wq), f32)
        dws = []
        for k in range(DN_CONV):
            shift = DN_CONV - 1 - k
            d_raw = d_raw + dpre[shift:shift + ts] * w[k:k + 1]
            dws.append(jnp.sum(dpre[:ts] * taps[k][:ts], axis=0, keepdims=True))
        dp_ref[...] = d_raw.astype(bf16)
        dw_ref[...] += jnp.concatenate(dws + [jnp.zeros((CONV_PAD - DN_CONV, wq), f32)], axis=0)

    sec = SEC_QKV // wq
    return host_call(
        side, body, name=name, grid=(nt,),
        in_specs=[pl.BlockSpec((ts, wq), lambda i: (i, sec)),
                  pl.BlockSpec((CONV_PAD, wq), lambda i: (jnp.maximum(i * nb - 1, 0), sec)),
                  pl.BlockSpec((CONV_PAD, wq), lambda i: (jnp.minimum((i + 1) * nb, last_blk), sec)),
                  pl.BlockSpec((CONV_PAD, wq), lambda i: (0, 0)),
                  pl.BlockSpec((ts, wq), lambda i: (i, 0)),
                  pl.BlockSpec((CONV_PAD, wq), lambda i: (jnp.minimum((i + 1) * nb, last_blk), 0)),
                  pl.BlockSpec(memory_space=pl.ANY)],
        out_specs=[pl.BlockSpec((ts, wq), lambda i: (i, sec)), pl.BlockSpec((CONV_PAD, wq), lambda i: (0, 0))],
        out_shape=[jax.ShapeDtypeStruct(dproj.shape, bf16), jax.ShapeDtypeStruct((CONV_PAD, wq), f32)],
        scratch_shapes=[], aliases={6: 0}, args=(proj, proj, proj, conv_w8, dxc, dxc, dproj))


def _dn_chunk(state, xc, z, ba, alog_row, dtb_row, norm_row, t_known):
    c, nh = DN_CHUNK, DN_HEADS
    n = c * nh
    row = lax.broadcasted_iota(jnp.int32, (n, n), 0)
    col = lax.broadcasted_iota(jnp.int32, (n, n), 1)
    same_head = (row // c) == (col // c)
    tril, strict = same_head & (row >= col), same_head & (row > col)
    tril_c = lax.broadcasted_iota(jnp.int32, (c, c), 0) >= lax.broadcasted_iota(jnp.int32, (c, c), 1)
    beta_all = _sigmoid(ba)
    g_all = -jnp.exp(alog_row) * jax.nn.softplus(ba + dtb_row)
    gc_all = hdot(tril_c.astype(f32), g_all)
    gc_t = gc_all.T

    def stack(piece):
        return jnp.concatenate([piece(h) for h in range(nh)], axis=0)

    q = stack(lambda h: xc[:, h * DN_HD:(h + 1) * DN_HD])
    k = stack(lambda h: xc[:, MIX + h * DN_HD:MIX + (h + 1) * DN_HD])
    v = stack(lambda h: xc[:, 2 * MIX + h * DN_HD:2 * MIX + (h + 1) * DN_HD])
    zs = stack(lambda h: z[:, h * DN_HD:(h + 1) * DN_HD])
    q = q * lax.rsqrt(jnp.sum(q * q, axis=-1, keepdims=True) + NORM_EPS) * (DN_HD ** -0.5)
    k = k * lax.rsqrt(jnp.sum(k * k, axis=-1, keepdims=True) + NORM_EPS)
    beta = stack(lambda h: _lane_col(beta_all, h))
    g_cols = [_lane_col(gc_all, nh + h) for h in range(nh)]
    g_col = jnp.concatenate(g_cols, axis=0)
    g_row = jnp.concatenate([gc_t[nh + h:nh + h + 1, :] for h in range(nh)], axis=1)
    g_last = stack(lambda h: jnp.broadcast_to(g_cols[h][c - 1:c, :], (c, 1)))
    decay = jnp.where(tril, jnp.exp(jnp.where(tril, g_col - g_row, 0.0)), 0.0)
    kb = k * beta
    low = jnp.where(strict, bdot(kb, k, 1, 1) * decay, 0.0)
    t_inv = tri_inv(low, c) if t_known is None else tri_inv_known(low, t_known)
    e_gc = jnp.exp(g_col)
    uw = bdot(t_inv, jnp.concatenate([v * beta, kb * e_gc], axis=1), 1, 0)
    u, w = uw[:, :DN_HD], uw[:, DN_HD:]
    attn = bdot(q, k, 1, 1) * decay
    own = (lax.broadcasted_iota(jnp.int32, (n, nh * DN_HD), 1) // DN_HD
           == lax.broadcasted_iota(jnp.int32, (n, nh * DN_HD), 0) // c)

    def spread(a):
        return jnp.where(own, jnp.concatenate([a] * nh, axis=1), 0.0)

    v_new = u - bdot(spread(w), state, 1, 0)
    o = bdot(spread(q * e_gc), state, 1, 0) + bdot(attn, v_new, 1, 0)
    keep = stack(lambda h: jnp.broadcast_to(jnp.exp(g_cols[h][c - 1:c, :]), (DN_HD, 1)))
    new_state = state * keep + bdot(spread(k * jnp.exp(g_last - g_col)), v_new, 0, 0)
    out = _rms(o, norm_row) * (zs * _sigmoid(zs))
    return new_state, jnp.concatenate([out[h * c:(h + 1) * c] for h in range(nh)], axis=1), t_inv


DN_STEP = 8 * DN_CHUNK


def _dn_step(state, xc, z, ba, alog_row, dtb_row, norm_row, t_known=None):
    outs, t_invs = [], []
    for c in range(DN_STEP // DN_CHUNK):
        rows = slice(c * DN_CHUNK, (c + 1) * DN_CHUNK)
        state, out, t_inv = _dn_chunk(state, xc[rows], z[rows], ba[rows], alog_row, dtb_row, norm_row,
                                      None if t_known is None else t_known[c])
        outs.append(out)
        t_invs.append(t_inv)
    return state, jnp.concatenate(outs, axis=0), jnp.stack(t_invs)


def _dn_specs(ts, order):
    zb = SEC_Z // MIX
    return [pl.BlockSpec((ts, 3 * MIX), lambda i: (order(i), 0)),
            pl.BlockSpec((ts, MIX), lambda i: (order(i), zb)),
            pl.BlockSpec((ts, LANES), lambda i: (order(i), 0)),
            pl.BlockSpec((1, LANES), lambda i: (0, 0)), pl.BlockSpec((1, LANES), lambda i: (0, 0)),
            pl.BlockSpec((1, LANES), lambda i: (0, 0))]


def dn_fwd(xc, proj, p_ba, alog_row, dtb_row, norm_row, *, side=None, name):
    s = xc.shape[0]
    ts = _tile(s, 512)
    n_step = ts // DN_STEP

    per_step = DN_STEP // DN_CHUNK
    n_tri = DN_HEADS * DN_CHUNK

    def body(xc_ref, z_ref, ba_ref, al_ref, dt_ref, nr_ref, o_ref, st_ref, tri_ref, state_ref):
        @pl.when(pl.program_id(0) == 0)
        def _():
            state_ref[...] = jnp.zeros_like(state_ref)

        def step(c, carry):
            rows = pl.ds(pl.multiple_of(c * DN_STEP, DN_STEP), DN_STEP)
            st_ref[c] = state_ref[...]
            new_state, out, t_invs = _dn_step(state_ref[...], xc_ref[rows, :], z_ref[rows, :].astype(f32), ba_ref[rows, :],
                                              al_ref[...], dt_ref[...], nr_ref[...])
            state_ref[...] = new_state
            o_ref[rows, :] = out.astype(bf16)
            tri_ref[pl.ds(c * per_step, per_step)] = t_invs
            return carry
        lax.fori_loop(0, n_step, step, 0)

    return host_call(
        side, body, name=name, grid=(s // ts,), in_specs=_dn_specs(ts, lambda i: i),
        out_specs=[pl.BlockSpec((ts, MIX), lambda i: (i, 0)),
                   pl.BlockSpec((n_step, DN_HEADS * DN_HD, DN_HD), lambda i: (i, 0, 0)),
                   pl.BlockSpec((n_step * per_step, n_tri, n_tri), lambda i: (i, 0, 0))],
        out_shape=[jax.ShapeDtypeStruct((s, MIX), bf16),
                   jax.ShapeDtypeStruct((s // DN_STEP, DN_HEADS * DN_HD, DN_HD), f32),
                   jax.ShapeDtypeStruct((s // DN_CHUNK, n_tri, n_tri), f32)],
        scratch_shapes=[pltpu.VMEM((DN_HEADS * DN_HD, DN_HD), f32)], aliases={},
        args=(xc, proj, p_ba, alog_row, dtb_row, norm_row))


def dn_bwd(xc, proj, p_ba, alog_row, dtb_row, norm_row, saved, tri, d_out, dproj, *, side=None, name):
    s = xc.shape[0]
    ts = _tile(s, 512)
    n_step = ts // DN_STEP
    nt = s // ts

    per_step = DN_STEP // DN_CHUNK
    n_tri = DN_HEADS * DN_CHUNK

    def body(xc_ref, z_ref, ba_ref, al_ref, dt_ref, nr_ref, st_ref, tri_ref, do_ref, _,
             dz_ref, dxc_ref, dba_ref, dal_ref, ddt_ref, dnr_ref, dstate_ref):
        @pl.when(pl.program_id(0) == 0)
        def _():
            dstate_ref[...] = jnp.zeros_like(dstate_ref)
            dal_ref[...] = jnp.zeros_like(dal_ref)
            ddt_ref[...] = jnp.zeros_like(ddt_ref)
            dnr_ref[...] = jnp.zeros_like(dnr_ref)

        def step(it, carry):
            c = n_step - 1 - it
            rows = pl.ds(pl.multiple_of(c * DN_STEP, DN_STEP), DN_STEP)
            t_known = tri_ref[pl.ds(c * per_step, per_step)]
            _, vjp = jax.vjp(lambda *a: _dn_step(*a, t_known=t_known)[:2], st_ref[c], xc_ref[rows, :],
                             z_ref[rows, :].astype(f32), ba_ref[rows, :], al_ref[...], dt_ref[...], nr_ref[...])
            d_in, dxc, dz, dba, dal, ddt, dnr = vjp((dstate_ref[...], do_ref[rows, :]))
            dstate_ref[...] = d_in
            dxc_ref[rows, :] = dxc
            dz_ref[rows, :] = dz.astype(bf16)
            dba_ref[rows, :] = dba.astype(bf16)
            dal_ref[...] += dal
            ddt_ref[...] += ddt
            dnr_ref[...] += dnr
            return carry
        lax.fori_loop(0, n_step, step, 0)

    def rev(i):
        return nt - 1 - i

    specs = _dn_specs(ts, rev)
    vec = pl.BlockSpec((1, LANES), lambda i: (0, 0))
    return host_call(
        side, body, name=name, grid=(nt,),
        in_specs=specs + [pl.BlockSpec((n_step, DN_HEADS * DN_HD, DN_HD), lambda i: (rev(i), 0, 0)),
                          pl.BlockSpec((n_step * per_step, n_tri, n_tri), lambda i: (rev(i), 0, 0)),
                          pl.BlockSpec((ts, MIX), lambda i: (rev(i), 0)), pl.BlockSpec(memory_space=pl.ANY)],
        out_specs=[specs[1], specs[0], specs[2], vec, vec, vec],
        out_shape=[jax.ShapeDtypeStruct(dproj.shape, bf16), jax.ShapeDtypeStruct((s, 3 * MIX), f32),
                   jax.ShapeDtypeStruct((s, LANES), bf16)] + [jax.ShapeDtypeStruct((1, LANES), f32)] * 3,
        scratch_shapes=[pltpu.VMEM((DN_HEADS * DN_HD, DN_HD), f32)], aliases={9: 0},
        args=(xc, proj, p_ba, alog_row, dtb_row, norm_row, saved, tri, d_out, dproj))


def _merge_in_specs(ts, d):
    row = pl.BlockSpec((ts, MIX), lambda i: (i, 0))
    return [row, row, row, pl.BlockSpec((ts, 3 * d), lambda i: (i, SEC_G // (3 * d))),
            pl.BlockSpec((3, MIX, d), lambda i: (0, 0, 0))]


def merge_fwd(out_a, out_b, out_c, proj, w_branch, *, name):
    s, d = out_a.shape[0], w_branch.shape[2]
    ts = _tile(s, 512, 16)

    def body(a_ref, b_ref, c_ref, g_ref, w_ref, o_ref):
        acc = jnp.zeros((ts, d), f32)
        for n, r in enumerate((a_ref, b_ref, c_ref)):
            acc = acc + _sigmoid(g_ref[:, n * d:(n + 1) * d].astype(f32)) * _dg(r[...], w_ref[n], 1, 0)
        o_ref[...] = acc.astype(bf16)

    return pl.pallas_call(
        body, name=name, grid=(s // ts,), in_specs=_merge_in_specs(ts, d),
        out_specs=pl.BlockSpec((ts, d), lambda i: (i, 0)), out_shape=jax.ShapeDtypeStruct((s, d), bf16),
        compiler_params=_params(1),
    )(out_a, out_b, out_c, proj, w_branch)


def merge_bwd(out_a, out_b, out_c, proj, w_branch, d_merged, dproj, *, name):
    s, d = out_a.shape[0], w_branch.shape[2]
    ts = _tile(s, 512, 16)

    def body(a_ref, b_ref, c_ref, g_ref, w_ref, dm_ref, _, dg_ref, da_ref, db_ref, dc_ref, dw_ref):
        @pl.when(pl.program_id(0) == 0)
        def _():
            dw_ref[...] = jnp.zeros_like(dw_ref)

        dm = dm_ref[...]
        for n, (r, dr) in enumerate(((a_ref, da_ref), (b_ref, db_ref), (c_ref, dc_ref))):
            gate = _sigmoid(g_ref[:, n * d:(n + 1) * d].astype(f32))
            branch = _dg(r[...], w_ref[n], 1, 0)
            dg_ref[:, n * d:(n + 1) * d] = (dm * branch * gate * (1.0 - gate)).astype(bf16)
            d_branch = dm * gate
            dr[...] = _dg(d_branch, w_ref[n], 1, 1)
            dw_ref[n] += _dg(r[...], d_branch, 0, 0)

    specs = _merge_in_specs(ts, d)
    row_f = pl.BlockSpec((ts, MIX), lambda i: (i, 0))
    return pl.pallas_call(
        body, name=name, grid=(s // ts,),
        in_specs=specs + [pl.BlockSpec((ts, d), lambda i: (i, 0)), pl.BlockSpec(memory_space=pl.ANY)],
        out_specs=[specs[3], row_f, row_f, row_f, specs[4]],
        out_shape=[jax.ShapeDtypeStruct(dproj.shape, bf16)] + [jax.ShapeDtypeStruct((s, MIX), f32)] * 3
        + [jax.ShapeDtypeStruct(w_branch.shape, f32)],
        input_output_aliases={6: 0}, compiler_params=_params(1),
    )(out_a, out_b, out_c, proj, w_branch, d_merged, dproj)


def gate_up_swiglu(h, w_gu, *, name):
    g2, d, w = w_gu.shape
    ng, s = g2 // 2, h.shape[0]
    tm = _tile(s, 1024, 16)

    def body(h_ref, wg_ref, wu_ref, g_ref, u_ref, a_ref):
        g16 = _dg(h_ref[...], wg_ref[...], 1, 0).astype(bf16)
        u16 = _dg(h_ref[...], wu_ref[...], 1, 0).astype(bf16)
        g_ref[...], u_ref[...] = g16, u16
        g = g16.astype(f32)
        a_ref[...] = (g * _sigmoid(g) * u16.astype(f32)).astype(bf16)

    out = pl.BlockSpec((None, tm, w), lambda i, j: (j, i, 0))
    return pl.pallas_call(
        body, name=name, grid=(s // tm, ng),
        in_specs=[pl.BlockSpec((tm, d), lambda i, j: (i, 0)), pl.BlockSpec((None, d, w), lambda i, j: (j, 0, 0)),
                  pl.BlockSpec((None, d, w), lambda i, j: (ng + j, 0, 0))],
        out_specs=[out, out, out], out_shape=[jax.ShapeDtypeStruct((ng, s, w), bf16)] * 3, compiler_params=_params(2),
    )(h, w_gu, w_gu)


def down_dx_swiglu(dx, w_down, gate, up, *, name):
    ng, w, d = w_down.shape
    s = dx.shape[0]
    tm = _tile(s, 1024, 16)

    def body(dx_ref, wd_ref, g_ref, u_ref, dg_ref, du_ref):
        d_act = _dg(dx_ref[...], wd_ref[...], 1, 1)
        g = g_ref[...].astype(f32)
        sig = _sigmoid(g)
        dg_ref[...] = (d_act * u_ref[...].astype(f32) * sig * (1.0 + g * (1.0 - sig))).astype(bf16)
        du_ref[...] = (d_act * g * sig).astype(bf16)

    blk = pl.BlockSpec((None, tm, w), lambda i, j: (j, i, 0))
    return pl.pallas_call(
        body, name=name, grid=(s // tm, ng),
        in_specs=[pl.BlockSpec((tm, d), lambda i, j: (i, 0)), pl.BlockSpec((None, w, d), lambda i, j: (j, 0, 0)), blk, blk],
        out_specs=[blk, blk], out_shape=[jax.ShapeDtypeStruct((ng, s, w), bf16)] * 2, compiler_params=_params(2),
    )(dx, w_down, gate, up)


def adamw(w, m, v, g_parts, *, name):
    n_layers = len(g_parts)
    n_parts, r, cols = g_parts[0].shape
    lanes = -(-cols // LANES) * LANES
    tr = _tile(r, max(16, (128 * 1024) // lanes), 16)
    nr = r // tr

    def body(w_ref, m_ref, v_ref, *rest):
        gp_refs, (g_ref, d_ref, nm_ref, nv_ref) = rest[:n_layers], rest[n_layers:]
        layer = pl.program_id(0)
        g = jnp.zeros((tr, cols), f32)
        for l, gp_ref in enumerate(gp_refs):
            g_l = gp_ref[0].astype(f32)
            for k in range(1, n_parts):
                g_l = g_l + gp_ref[k].astype(f32)
            g = jnp.where(layer == l, g_l, g)
        nm = ADAM_B1 * m_ref[...] + (1.0 - ADAM_B1) * g
        nv = ADAM_B2 * v_ref[...] + (1.0 - ADAM_B2) * jnp.square(g)
        m_hat = nm / (1.0 - ADAM_B1 ** ADAM_STEP)
        v_hat = nv / (1.0 - ADAM_B2 ** ADAM_STEP)
        g_ref[...] = g
        d_ref[...] = -ADAM_LR * (m_hat / (jnp.sqrt(v_hat) + ADAM_EPS) + ADAM_WD * w_ref[...])
        nm_ref[...] = nm
        nv_ref[...] = nv

    row = pl.BlockSpec((tr, cols), lambda l, i: (l * nr + i, 0))

    def parts_spec(own):
        return pl.BlockSpec((n_parts, tr, cols),
                            lambda l, i: (0, jnp.where(l == own, i, jnp.where(l < own, 0, nr - 1)), 0))

    return pl.pallas_call(
        body, name=name, grid=(n_layers, nr), in_specs=[row, row, row] + [parts_spec(l) for l in range(n_layers)],
        out_specs=[row] * 4, out_shape=[jax.ShapeDtypeStruct(w.shape, f32)] * 4, compiler_params=_params(2),
    )(w, m, v, *g_parts)


def _mesh_pos():
    return lax.axis_index("x"), lax.axis_index("y"), lax.axis_index("c")


def _dev_index(p):
    return 4 * p[0] + 2 * p[1] + p[2]


class Exchange:
    def __init__(self, kind, arrays):
        self.kind, self.arrays, self.n = kind, list(arrays), len(arrays)
        self.specs = [pl.BlockSpec(memory_space=pl.ANY)] * self.n
        self.out_shapes = [jax.ShapeDtypeStruct(((N_DEV,) if kind == "gather" else ()) + a.shape, a.dtype)
                           for a in self.arrays]
        self.scratch = [pltpu.SemaphoreType.DMA((self.n, N_DEV - 1)), pltpu.SemaphoreType.DMA((self.n, N_DEV - 1)),
                        pltpu.SemaphoreType.DMA((self.n,))]

    def _copies(self, in_refs, out_refs, sems, with_arrivals):
        send_sems, recv_sems, local_sems = sems
        x, y, c = _mesh_pos()
        mine = _dev_index((x, y, c))

        def src(a, slab):
            return in_refs[a] if self.kind == "gather" else in_refs[a].at[slab]

        local = [pltpu.make_async_copy(src(a, mine), out_refs[a].at[mine], local_sems.at[a]) for a in range(self.n)]
        sends, arrivals = [], []
        for k in range(1, N_DEV):
            peer = (1 - x if k & 4 else x, 1 - y if k & 2 else y, 1 - c if k & 1 else c)
            theirs = _dev_index(peer)
            for a in range(self.n):
                to = dict(send_sem=send_sems.at[a, k - 1], recv_sem=recv_sems.at[a, k - 1], device_id=peer,
                          device_id_type=pl.DeviceIdType.MESH)
                sends.append(pltpu.make_async_remote_copy(src_ref=src(a, theirs), dst_ref=out_refs[a].at[mine], **to))
                if with_arrivals:
                    arrivals.append(pltpu.make_async_remote_copy(src_ref=src(a, theirs),
                                                                 dst_ref=out_refs[a].at[theirs], **to))
        return local, sends, arrivals

    def start(self, in_refs, out_refs, sems):
        local, sends, _ = self._copies(in_refs, out_refs, sems, False)
        for cp in local + sends:
            cp.start()

    def wait(self, in_refs, out_refs, sems):
        local, sends, arrivals = self._copies(in_refs, out_refs, sems, True)
        for cp in arrivals:
            cp.wait_recv()
        for cp in sends:
            cp.wait_send()
        for cp in local:
            cp.wait()

    def run_around(self, grid, in_refs, out_refs, sems, *, before):
        at = None
        for axis, size in enumerate(grid):
            hit = pl.program_id(axis) == (0 if before else size - 1)
            at = hit if at is None else at & hit

        @pl.when(at)
        def _():
            (self.start if before else self.wait)(in_refs, out_refs, sems)


def host_call(side, body, *, name, grid, in_specs, out_specs, out_shape, scratch_shapes, args, aliases):
    n_in, n_out = len(in_specs), len(out_specs)
    if side is None:
        kernel_body = body
    else:
        n = side.n
        in_specs, args = in_specs + side.specs, tuple(args) + tuple(side.arrays)
        out_specs, out_shape = out_specs + side.specs, out_shape + side.out_shapes
        scratch_shapes = scratch_shapes + side.scratch

        def kernel_body(*refs):
            ins, side_in = refs[:n_in], refs[n_in:n_in + n]
            outs, side_out = refs[n_in + n:n_in + n + n_out], refs[n_in + n + n_out:n_in + 2 * n + n_out]
            scratch, sems = refs[n_in + 2 * n + n_out:-3], refs[-3:]
            side.run_around(grid, side_in, side_out, sems, before=True)
            body(*ins, *outs, *scratch)
            side.run_around(grid, side_in, side_out, sems, before=False)

    outs = pl.pallas_call(
        kernel_body, name=name, grid=grid, in_specs=in_specs, out_specs=out_specs, out_shape=out_shape,
        scratch_shapes=scratch_shapes, input_output_aliases=aliases, compiler_params=_params(len(grid)),
    )(*args)
    return outs if side is None else (outs[:n_out], outs[n_out:])


def all_gather(blocks, *, name):
    n = len(blocks)
    any_spec = pl.BlockSpec(memory_space=pl.ANY)

    def body(*refs):
        ins, outs = refs[:n], refs[n:2 * n]
        send_sems, recv_sems, local_sems = refs[2 * n:]
        x, y, c = _mesh_pos()
        me, sibling = (x, y, c), (x, y, 1 - c)
        chips = [(1 - x, y), (x, 1 - y), (1 - x, 1 - y)]

        def copy(a, k, block, to, src=None):
            dst = outs[a].at[_dev_index(block)]
            return pltpu.make_async_remote_copy(
                src_ref=dst if src is None else src, dst_ref=dst, send_sem=send_sems.at[a, k],
                recv_sem=recv_sems.at[a, k], device_id=to, device_id_type=pl.DeviceIdType.MESH)

        mine = [pltpu.make_async_copy(ins[a], outs[a].at[_dev_index(me)], local_sems.at[a]) for a in range(n)]
        for cp in mine:
            cp.start()
        first = []
        for a in range(n):
            first.append(copy(a, 0, me, sibling, src=ins[a]))
            first += [copy(a, 1 + j, me, (*chip, c), src=ins[a]) for j, chip in enumerate(chips)]
        for cp in first:
            cp.start()
        passed = []
        for j, chip in enumerate(chips):
            for a in range(n):
                copy(a, 1 + j, (*chip, c), me).wait_recv()
                fwd = copy(a, 4 + j, (*chip, c), sibling)
                fwd.start()
                passed.append(fwd)
        for a in range(n):
            copy(a, 0, sibling, me).wait_recv()
            for j, chip in enumerate(chips):
                copy(a, 4 + j, (*chip, 1 - c), me).wait_recv()
        for cp in first + passed:
            cp.wait_send()
        for cp in mine:
            cp.wait()

    return pl.pallas_call(
        body, name=name, in_specs=[any_spec] * n, out_specs=[any_spec] * n,
        out_shape=[jax.ShapeDtypeStruct((N_DEV,) + b.shape, b.dtype) for b in blocks],
        scratch_shapes=[pltpu.SemaphoreType.DMA((n, 7)), pltpu.SemaphoreType.DMA((n, 7)),
                        pltpu.SemaphoreType.DMA((n,))],
    )(*blocks)


def exchange(jobs, *, name):
    total = sum(j.n for j in jobs)

    def body(*refs):
        ins, outs, sems = refs[:total], refs[total:2 * total], refs[2 * total:]
        pieces, off = [], 0
        for i, j in enumerate(jobs):
            pieces.append((ins[off:off + j.n], outs[off:off + j.n], sems[3 * i:3 * i + 3]))
            off += j.n
        for j, piece in zip(jobs, pieces):
            j.start(*piece)
        for j, piece in zip(jobs, pieces):
            j.wait(*piece)

    outs = pl.pallas_call(
        body, name=name, in_specs=[s for j in jobs for s in j.specs], out_specs=[s for j in jobs for s in j.specs],
        out_shape=[s for j in jobs for s in j.out_shapes], scratch_shapes=[s for j in jobs for s in j.scratch],
    )(*[a for j in jobs for a in j.arrays])
    split, off = [], 0
    for j in jobs:
        split.append(outs[off:off + j.n])
        off += j.n
    return split


def _rows128(arr):
    flat = arr.reshape(-1)
    rows = -(-flat.shape[0] // (8 * LANES)) * 8
    return jnp.pad(flat, (0, rows * LANES - flat.shape[0])).reshape(rows, LANES)


def _pad_lanes(row, width=LANES, at=0):
    return jnp.pad(row, (at, width - at - row.shape[0])).reshape(1, width)


def _w_in_sections(got):
    d = got.shape[1]
    wi = jnp.transpose(got, (1, 0, 2)).reshape(d, -1)
    w_main = jnp.concatenate([wi[:, :C_B], wi[:, C_Z:C_BA], wi[:, C_QKV:C_Z], wi[:, C_G:], wi[:, C_B:C_QKV]], axis=1)
    return w_main, jnp.pad(wi[:, C_BA:C_G], ((0, 0), (0, LANES - (C_G - C_BA))))


def _w_in_parts(gw_main, gw_ba):
    d = gw_main.shape[0]
    full = jnp.concatenate([gw_main[:, SEC_A:SEC_Z], gw_main[:, SEC_B:], gw_main[:, SEC_QKV:SEC_G],
                            gw_main[:, SEC_Z:SEC_QKV], gw_ba[:, :C_G - C_BA], gw_main[:, SEC_G:SEC_B]], axis=1)
    return jnp.transpose(full.reshape(d, N_DEV, -1), (1, 0, 2))


def kernel(x, positions, attn_norm, w_in, sgu_ln_g, sgu_ln_b, sgu_w, sgu_b, attn_sinks, dn_conv_w, dn_a_log, dn_dt_bias, dn_norm, w_branch, w_out, ffn_norm, w_gate_up, w_down, final_norm, loss_target, m_attn_norm, m_w_in, m_sgu_ln_g, m_sgu_ln_b, m_sgu_w, m_sgu_b, m_attn_sinks, m_dn_conv_w, m_dn_a_log, m_dn_dt_bias, m_dn_norm, m_w_branch, m_w_out, m_ffn_norm, m_w_gate_up, m_w_down, m_final_norm, v_attn_norm, v_w_in, v_sgu_ln_g, v_sgu_ln_b, v_sgu_w, v_sgu_b, v_attn_sinks, v_dn_conv_w, v_dn_a_log, v_dn_dt_bias, v_dn_norm, v_w_branch, v_w_out, v_ffn_norm, v_w_gate_up, v_w_down, v_final_norm):
    given = dict(locals())
    depth, d_model = attn_norm.shape
    s = x.shape[1]
    x2 = x.reshape(s, d_model)
    target = loss_target.reshape(s, d_model)
    posf = positions.reshape(s, 1).astype(f32)
    inv_freq = ROPE_THETA ** (-jnp.arange(0, ROPE_DIM, 2, dtype=f32) / ROPE_DIM)
    inv_head = jnp.concatenate([inv_freq, inv_freq, jnp.zeros((SWA_HD - ROPE_DIM,), f32)])
    tables = rope_tables(posf, jnp.tile(inv_head, LANES // SWA_HD).reshape(1, LANES), name="rope_tables")

    assert depth == 2, depth
    gathered = dict(zip([("w_in", 0), ("dn_conv_w", 0), ("dn_conv_w", 1)], all_gather(
        [w_in[0].astype(bf16), dn_conv_w[0], dn_conv_w[1]], name="gather_first")))
    riders = {"l0_in_proj": [("w_branch", 0), ("w_out", 0), ("w_down", 0)],
              "l0_deltanet": [("w_gate_up", 0), ("w_in", 1)],
              "l1_deltanet": [("w_branch", 1), ("w_out", 1), ("w_gate_up", 1), ("w_down", 1)]}

    def gathering(host, call, *args, **kw):
        if host not in riders:
            return call(*args, name=host, **kw)
        side = Exchange("gather", [given[n][l].astype(bf16) for n, l in riders[host]])
        out, got = call(*args, side=side, name=host, **kw)
        gathered.update(zip(riders[host], got))
        return out

    layers, saved = [], []
    h_in = x2
    for l in range(depth):
        t = f"l{l}_"
        w_main, w_ba = _w_in_sections(gathered["w_in", l])
        conv_full = jnp.transpose(gathered["dn_conv_w", l], (1, 0, 2)).reshape(DN_CONV, -1)
        p = dict(
            w_main=w_main, w_ba=w_ba, conv_w8=jnp.pad(conv_full, ((0, CONV_PAD - DN_CONV), (0, 0))),
            attn_norm=attn_norm[l].reshape(1, -1), ffn_norm=ffn_norm[l].reshape(1, -1),
            ln_g=sgu_ln_g[l].reshape(1, -1), ln_b=sgu_ln_b[l].reshape(1, -1), sgu_w=sgu_w[l], sgu_bt=sgu_b[l].T,
            sink_row=_pad_lanes(attn_sinks[l]), alog_row=_pad_lanes(dn_a_log[l], at=DN_HEADS),
            dtb_row=_pad_lanes(dn_dt_bias[l], at=DN_HEADS), norm_row=dn_norm[l].reshape(1, -1))
        layers.append(p)
        h = rmsnorm_fwd(h_in, p["attn_norm"], name=t + "attn_norm")
        proj = gathering(t + "in_proj", matmul, h, p["w_main"], "nn", bf16)
        p_ba = matmul(h, p["w_ba"], "nn", f32, name=t + "in_proj_ba")
        out_a = sgu_fwd(proj, p["ln_g"], p["ln_b"], p["sgu_w"], p["sgu_bt"], name=t + "sgu")
        out_b = swa_fwd(proj, p["sink_row"], tables, name=t + "swa")
        xc = conv_fwd(proj, p["conv_w8"], name=t + "dn_conv")
        out_c, states, tri = gathering(t + "deltanet", dn_fwd, xc, proj, p_ba, p["alog_row"], p["dtb_row"], p["norm_row"])
        p.update(w_branch=jnp.transpose(gathered["w_branch", l], (1, 2, 0, 3)).reshape(3, MIX, d_model),
                 w_out=gathered["w_out", l].reshape(d_model, d_model),
                 w_gu=gathered["w_gate_up", l],
                 w_down=gathered["w_down", l].reshape(N_DEV // 2, -1, d_model))
        merged = merge_fwd(out_a, out_b, out_c, proj, p["w_branch"], name=t + "merge")
        x_mid = matmul(merged, p["w_out"], "nn", f32, residual=h_in, tn=1024, name=t + "out_proj")
        h2 = rmsnorm_fwd(x_mid, p["ffn_norm"], name=t + "ffn_norm")
        gate, up, act = gate_up_swiglu(h2, p["w_gu"], name=t + "gate_up")
        x_out = matmul(act, p["w_down"], "nn", f32, residual=x_mid, group="k", tn=1024, name=t + "down")
        saved.append(dict(x_in=h_in, h=h, proj=proj, p_ba=p_ba, out_a=out_a, out_b=out_b, out_c=out_c, xc=xc,
                          states=states, tri=tri, merged=merged, x_mid=x_mid, h2=h2, gate=gate, up=up, act=act))
        h_in = x_out

    dx, d_final_norm, loss_row = loss_head(h_in, final_norm.reshape(1, -1), target, name="loss_head")
    loss = lax.psum(loss_row[0, 0], MESH_AXES)

    shard_names = ["w_in", "dn_conv_w", "w_branch", "w_out", "w_gate_up", "w_down"]
    rep_names = ["attn_norm", "sgu_ln_g", "sgu_ln_b", "sgu_w", "sgu_b", "attn_sinks", "dn_a_log", "dn_dt_bias",
                 "dn_norm", "ffn_norm"]
    parts, received, per_layer = {}, {}, []
    senders = {"l0_b_swa": [("w_gate_up", 1), ("w_down", 1), ("w_out", 1), ("w_branch", 1)],
               "l0_b_deltanet": [("w_in", 1), ("w_gate_up", 0)],
               "l0_b_dn_conv": [("w_out", 0), ("w_branch", 0), ("w_down", 0)],
               "l0_b_in_proj_dx": [("w_in", 0)]}

    def scattering(host, call, *args, **kw):
        if host not in senders:
            return call(*args, name=host, **kw)
        out, got = call(*args, side=Exchange("scatter", [parts[key] for key in senders[host]]), name=host, **kw)
        received.update(zip(senders[host], got))
        return out

    for l in reversed(range(depth)):
        p, sv, t = layers[l], saved[l], f"l{l}_b_"
        d_gate, d_up = down_dx_swiglu(dx, p["w_down"], sv["gate"], sv["up"], name=t + "down_dx")
        gw_down = matmul(sv["act"], dx, "tn", bf16, group="m", tk=2048, tn=512, name=t + "down_dw")
        gw_gu = jnp.concatenate([matmul(sv["h2"], d_half, "tn", bf16, group="n", tk=2048, name=t + "gate_up_dw" + tag)
                                 for d_half, tag in ((d_gate, "_gate"), (d_up, "_up"))], axis=0)
        half = N_DEV // 2
        d_h2 = matmul(d_gate, p["w_gu"][:half], "nt", f32, group="k", tn=1024, name=t + "gate_up_dx_gate")
        d_h2 = matmul(d_up, p["w_gu"][half:], "nt", bf16, group="k", tn=1024, residual=d_h2, name=t + "gate_up_dx_up")
        dx_mid, g_ffn = rmsnorm_bwd(sv["x_mid"], p["ffn_norm"], d_h2, dx, name=t + "ffn_norm")
        d_merged = matmul(dx_mid, p["w_out"], "nt", f32, tn=1024, name=t + "out_proj_dx")
        gw_out = matmul(sv["merged"], dx_mid, "tn", bf16, tk=2048, tn=1024, name=t + "out_proj_dw")
        dproj = lax.empty((s, W_MAIN), bf16)
        dproj, d_a, d_b, d_c, gw_branch = merge_bwd(sv["out_a"], sv["out_b"], sv["out_c"], sv["proj"], p["w_branch"],
                                                   d_merged, dproj, name=t + "merge")
        parts.update({("w_gate_up", l): gw_gu, ("w_down", l): gw_down.reshape(N_DEV, -1, d_model),
                      ("w_out", l): gw_out.reshape(N_DEV, -1, d_model),
                      ("w_branch", l): jnp.transpose(gw_branch.reshape(3, MIX, N_DEV, -1), (2, 0, 1, 3)).astype(bf16)})
        dproj, g_ln_g, g_ln_b, g_sgu_w, g_sgu_bt = sgu_bwd(sv["proj"], p["ln_g"], p["ln_b"], p["sgu_w"], p["sgu_bt"],
                                                         d_a, dproj, name=t + "sgu")
        dproj, g_sink = scattering(t + "swa", swa_bwd, sv["proj"], p["sink_row"], tables, d_b, dproj)
        dproj, dxc, dba, g_alog, g_dtb, g_dnorm = scattering(
            t + "deltanet", dn_bwd, sv["xc"], sv["proj"], sv["p_ba"], p["alog_row"], p["dtb_row"], p["norm_row"],
            sv["states"], sv["tri"], d_c, dproj)
        dproj, g_conv8 = scattering(t + "dn_conv", conv_bwd, sv["proj"], p["conv_w8"], dxc, dproj)
        gw_main = matmul(sv["h"], dproj, "tn", bf16, tk=2048, name=t + "in_proj_dw")
        gw_ba = matmul(sv["h"], dba, "tn", bf16, tk=2048, name=t + "in_proj_ba_dw")
        parts["w_in", l] = _w_in_parts(gw_main, gw_ba)
        d_h = scattering(t + "in_proj_dx", matmul, dproj, p["w_main"], "nt", f32, tk=2304, tn=1024)
        d_h = matmul(dba, p["w_ba"], "nt", bf16, residual=d_h, name=t + "in_proj_ba_dx")
        dx, g_attn = rmsnorm_bwd(sv["x_in"], p["attn_norm"], d_h, dx_mid, name=t + "attn_norm")
        per_layer.append(dict(
            dn_conv_w=jnp.transpose(g_conv8[:DN_CONV].reshape(DN_CONV, N_DEV, -1), (1, 0, 2)),
            attn_norm=g_attn, sgu_ln_g=g_ln_g, sgu_ln_b=g_ln_b, sgu_w=g_sgu_w, sgu_b=g_sgu_bt.T,
            attn_sinks=g_sink[0, :SWA_HEADS], dn_a_log=g_alog[0, DN_HEADS:2 * DN_HEADS],
            dn_dt_bias=g_dtb[0, DN_HEADS:2 * DN_HEADS], dn_norm=g_dnorm, ffn_norm=g_ffn))
    per_layer.reverse()

    conv_parts = jnp.concatenate([pp["dn_conv_w"] for pp in per_layer], axis=1)
    rep_grads = {n: jnp.stack([pp[n].reshape(given[n].shape[1:]) for pp in per_layer]) for n in rep_names}
    rep_grads["final_norm"] = d_final_norm[0]
    rep_names = rep_names + ["final_norm"]
    rep_rows = [_rows128(given[n]).shape[0] for n in rep_names]
    pad_rows = -sum(rep_rows) % 16

    def small_rows(values):
        pieces = [_rows128(values[n]) for n in rep_names]
        return jnp.concatenate(pieces + [jnp.zeros((pad_rows, LANES), f32)], axis=0)

    (got_conv,), (small_all,) = exchange(
        [Exchange("scatter", [conv_parts]), Exchange("gather", [small_rows(rep_grads)])], name="exchange_last")

    results = [{}, {}, {}, {}]
    for n in shard_names:
        shp = given[n].shape
        two = (-1, shp[-1])
        by_layer = [got_conv] if n == "dn_conv_w" else [received[n, l].reshape(N_DEV, -1, shp[-1]) for l in range(depth)]
        outs = adamw(given[n].reshape(two), given["m_" + n].reshape(two), given["v_" + n].reshape(two), by_layer,
                     name="adamw_" + n)
        for res, val in zip(results, outs):
            res[n] = val.reshape(shp)
    outs = adamw(small_rows(given), small_rows({n: given["m_" + n] for n in rep_names}),
                 small_rows({n: given["v_" + n] for n in rep_names}), [small_all], name="adamw_replicated")
    for res, val in zip(results, outs):
        row = 0
        for n, nr in zip(rep_names, rep_rows):
            res[n] = val[row:row + nr].reshape(-1)[:given[n].size].reshape(given[n].shape)
            row += nr
    order = ["attn_norm", "w_in", "sgu_ln_g", "sgu_ln_b", "sgu_w", "sgu_b", "attn_sinks", "dn_conv_w", "dn_a_log",
             "dn_dt_bias", "dn_norm", "w_branch", "w_out", "ffn_norm", "w_gate_up", "w_down", "final_norm"]
    return (loss, dx.reshape(x.shape), *[res[n] for res in results for n in order])
```

```python
import functools

import jax
import jax.numpy as jnp
from jax import lax
from jax.experimental import pallas as pl
from jax.experimental.pallas import tpu as pltpu

f32 = jnp.float32
bf16 = jnp.bfloat16

N_DEV = 8
MESH_AXES = ("x", "y", "c")
NORM_EPS = 1e-6
MIX = 512
SGU_GROUPS, SGU_CHUNK = 4, 128
SWA_HEADS, SWA_KV, SWA_HD, WINDOW = 8, 2, 64, 128
ROPE_THETA, ROPE_DIM = 500000.0, 16
DN_HEADS, DN_HD, DN_CONV, DN_CHUNK = 4, 128, 4, 64
ADAM_LR, ADAM_B1, ADAM_B2, ADAM_EPS, ADAM_WD, ADAM_STEP = 0.001, 0.9, 0.999, 1e-08, 0.01, 10

LANES = 128
VMEM_LIMIT = 56 * 1024 * 1024

SEC_A, SEC_Z, SEC_QKV, SEC_G, SEC_B = 0, 1024, 1536, 3072, 6144
W_MAIN = 6912
C_B, C_QKV, C_Z, C_BA, C_G = 1024, 1792, 3328, 3840, 3848


def _params(n_axes, **kw):
    return pltpu.CompilerParams(dimension_semantics=("arbitrary",) * n_axes, vmem_limit_bytes=VMEM_LIMIT, **kw)


def _tile(n, target, mult=LANES):
    if n <= target:
        return n
    best = None
    for t in range(mult, target + 1, mult):
        if n % t == 0:
            best = t
    assert best is not None, (n, target, mult)
    return best


def _dg(a, b, ca, cb):
    return lax.dot_general(a.astype(bf16), b.astype(bf16), (((ca,), (cb,)), ((), ())), preferred_element_type=f32)


def _dg3(a, b, ca, cb):
    a_hi, b_hi = a.astype(bf16), b.astype(bf16)
    a_lo, b_lo = (a - a_hi.astype(f32)).astype(bf16), (b - b_hi.astype(f32)).astype(bf16)

    def dot(p, q):
        return lax.dot_general(p, q, (((ca,), (cb,)), ((), ())), preferred_element_type=f32)

    return dot(a_hi, b_hi) + (dot(a_hi, b_lo) + dot(a_lo, b_hi))


def _differentiable_dot(core):
    @functools.partial(jax.custom_vjp, nondiff_argnums=(2, 3))
    def dot(a, b, ca, cb):
        return core(a, b, ca, cb)

    def fwd(a, b, ca, cb):
        return core(a, b, ca, cb), (a, b)

    def bwd(ca, cb, res, ct):
        a, b = res
        da = core(ct, b, 1, 1 - cb) if ca == 1 else core(b, ct, 1 - cb, 1)
        db = core(a, ct, 1 - ca, 0) if cb == 0 else core(ct, a, 0, 1 - ca)
        return da, db

    dot.defvjp(fwd, bwd)
    return dot


bdot = _differentiable_dot(_dg)
_hdot = _differentiable_dot(_dg3)


def hdot(a, b, ca=1, cb=0):
    return _hdot(a, b, ca, cb)


@functools.partial(jax.custom_vjp, nondiff_argnums=(1,))
def lroll(x, shift):
    return pltpu.roll(x, shift, 1)


def _lroll_fwd(x, shift):
    return pltpu.roll(x, shift, 1), None


def _lroll_bwd(shift, _, ct):
    return (pltpu.roll(ct, ct.shape[1] - shift, 1),)


lroll.defvjp(_lroll_fwd, _lroll_bwd)


@functools.partial(jax.custom_vjp, nondiff_argnums=(1,))
def tri_inv(low, nil):
    n = low.shape[0]
    row = lax.broadcasted_iota(jnp.int32, (n, n), 0)
    col = lax.broadcasted_iota(jnp.int32, (n, n), 1)
    eye = (row == col).astype(f32)
    m = -low
    p = eye + m
    span = 2
    while span < nil:
        dot = hdot if span == 2 else (lambda a, b: bdot(a, b, 1, 0))
        m = dot(m, m)
        p = p + dot(p, m)
        span *= 2
    return p


def _tri_inv_fwd(low, nil):
    t = tri_inv(low, nil)
    return t, t


def _tri_inv_bwd(nil, t, dt):
    return (-bdot(t, bdot(dt, t, 1, 1), 0, 0),)


tri_inv.defvjp(_tri_inv_fwd, _tri_inv_bwd)


@jax.custom_vjp
def tri_inv_known(low, t):
    return t


def _tri_inv_known_fwd(low, t):
    return t, t


def _tri_inv_known_bwd(t, dt):
    return _tri_inv_bwd(None, t, dt) + (jnp.zeros_like(t),)


tri_inv_known.defvjp(_tri_inv_known_fwd, _tri_inv_known_bwd)


def _sigmoid(x):
    return 1.0 / (1.0 + jnp.exp(-x))


def _rms(x, g):
    return x * lax.rsqrt(jnp.mean(x * x, axis=-1, keepdims=True) + NORM_EPS) * g


def _lane_col(x, lane_idx):
    lane = lax.broadcasted_iota(jnp.int32, x.shape, 1)
    return jnp.sum(jnp.where(lane == lane_idx, x, 0.0), axis=1, keepdims=True)


def matmul(a, b, mode, out_dtype, *, residual=None, group=None, side=None, tm=1024, tn=768, tk=1024, name):
    dims = {"a": ("m", "k") if mode != "tn" else ("k", "m"),
            "b": {"nn": ("k", "n"), "nt": ("n", "k"), "tn": ("k", "n")}[mode], "o": ("m", "n")}
    full, groups = {}, 1
    for arr, key in ((a, "a"), (b, "b")):
        grouped = group in dims[key]
        if grouped:
            groups = arr.shape[0]
        full[dims[key][0]], full[dims[key][1]] = arr.shape[1:] if grouped else arr.shape
    want = {"m": tm, "n": tn, "k": tk}
    per_step = min(groups, 4) if group == "k" else 1
    tiles = {d: full[d] if d == group else _tile(full[d], want[d]) for d in "mnk"}
    steps = {d: groups // per_step if d == group else full[d] // tiles[d] for d in "mnk"}

    def spec(key):
        d0, d1 = dims[key]

        def index(i, j, kk):
            at = {"m": i, "n": j, "k": kk}
            if group in (d0, d1):
                return (at[group], 0 if d0 == group else at[d0], 0 if d1 == group else at[d1])
            return (at[d0], at[d1])

        block = (tiles[d0], tiles[d1])
        if group in (d0, d1):
            block = ((per_step if group == "k" else None),) + block
        return pl.BlockSpec(block, index)

    ca, cb = {"nn": (1, 0), "nt": (1, 1), "tn": (0, 0)}[mode]
    nk = steps["k"]
    o_spec = spec("o")
    out_shape = (groups, full["m"], full["n"]) if group in ("m", "n") else (full["m"], full["n"])
    has_res = residual is not None

    def product(a_ref, b_ref):
        if group != "k":
            return _dg(a_ref[...], b_ref[...], ca, cb)
        total = _dg(a_ref[0], b_ref[0], ca, cb)
        for g in range(1, per_step):
            total = total + _dg(a_ref[g], b_ref[g], ca, cb)
        return total

    def body(a_ref, b_ref, *rest):
        r_ref = rest[0] if has_res else None
        o_ref = rest[1 if has_res else 0]

        def emit(acc):
            o_ref[...] = (acc + r_ref[...] if has_res else acc).astype(out_dtype)

        if nk == 1:
            emit(product(a_ref, b_ref))
            return
        acc_ref = rest[-1]
        kk = pl.program_id(2)

        @pl.when(kk == 0)
        def _():
            acc_ref[...] = jnp.zeros_like(acc_ref)

        acc_ref[...] += product(a_ref, b_ref)

        @pl.when(kk == nk - 1)
        def _():
            emit(acc_ref[...])

    res = host_call(
        side, body, name=name, grid=(steps["m"], steps["n"], nk),
        in_specs=[spec("a"), spec("b")] + ([o_spec] if has_res else []), out_specs=[o_spec],
        out_shape=[jax.ShapeDtypeStruct(out_shape, out_dtype)],
        scratch_shapes=[pltpu.VMEM((tiles["m"], tiles["n"]), f32)] if nk > 1 else [], aliases={},
        args=(a, b) + ((residual,) if has_res else ()))
    return res[0] if side is None else (res[0][0], res[1])


def rmsnorm_fwd(x, g_row, *, name):
    s, d = x.shape
    ts = _tile(s, 1024, 16)

    def body(x_ref, g_ref, o_ref):
        o_ref[...] = _rms(x_ref[...], g_ref[...]).astype(bf16)

    return pl.pallas_call(
        body, name=name, grid=(s // ts,),
        in_specs=[pl.BlockSpec((ts, d), lambda i: (i, 0)), pl.BlockSpec((1, d), lambda i: (0, 0))],
        out_specs=pl.BlockSpec((ts, d), lambda i: (i, 0)), out_shape=jax.ShapeDtypeStruct((s, d), bf16),
        compiler_params=_params(1),
    )(x, g_row)


def rmsnorm_bwd(x, g_row, dh, dres, *, name):
    s, d = x.shape
    ts = _tile(s, 1024, 16)

    def body(x_ref, g_ref, dh_ref, dres_ref, dx_ref, dg_ref):
        @pl.when(pl.program_id(0) == 0)
        def _():
            dg_ref[...] = jnp.zeros_like(dg_ref)

        _, vjp = jax.vjp(_rms, x_ref[...], g_ref[...])
        dx, dg = vjp(dh_ref[...].astype(f32))
        dx_ref[...] = dx + dres_ref[...]
        dg_ref[...] += dg

    row = pl.BlockSpec((ts, d), lambda i: (i, 0))
    vec = pl.BlockSpec((1, d), lambda i: (0, 0))
    return pl.pallas_call(
        body, name=name, grid=(s // ts,), in_specs=[row, vec, row, row], out_specs=[row, vec],
        out_shape=[jax.ShapeDtypeStruct((s, d), f32), jax.ShapeDtypeStruct((1, d), f32)],
        compiler_params=_params(1),
    )(x, g_row, dh, dres)


def loss_head(x, g_row, target, *, name):
    s, d = x.shape
    ts = _tile(s, 512, 16)

    def body(x_ref, g_ref, t_ref, dx_ref, dg_ref, loss_ref):
        @pl.when(pl.program_id(0) == 0)
        def _():
            dg_ref[...] = jnp.zeros_like(dg_ref)
            loss_ref[...] = jnp.zeros_like(loss_ref)

        y, vjp = jax.vjp(_rms, x_ref[...], g_ref[...])
        err = y - t_ref[...]
        dx, dg = vjp(err * (1.0 / d))
        dx_ref[...] = dx
        dg_ref[...] += dg
        loss_ref[...] += 0.5 * jnp.sum(jnp.sum(err * err, axis=1, keepdims=True) * (1.0 / d), axis=0, keepdims=True)

    row = pl.BlockSpec((ts, d), lambda i: (i, 0))
    vec = pl.BlockSpec((1, d), lambda i: (0, 0))
    one = pl.BlockSpec((1, LANES), lambda i: (0, 0))
    return pl.pallas_call(
        body, name=name, grid=(s // ts,), in_specs=[row, vec, row], out_specs=[row, vec, one],
        out_shape=[jax.ShapeDtypeStruct((s, d), f32), jax.ShapeDtypeStruct((1, d), f32),
                   jax.ShapeDtypeStruct((1, LANES), f32)],
        compiler_params=_params(1),
    )(x, g_row, target)


def _sgu_chunk(p_a, ln_g, ln_b, w, b_t):
    t = SGU_CHUNK
    u = jax.nn.gelu(p_a[:, :MIX])
    v = jax.nn.gelu(p_a[:, MIX:])
    vc = v - jnp.mean(v, axis=-1, keepdims=True)
    vn = vc * lax.rsqrt(jnp.mean(vc * vc, axis=-1, keepdims=True) + NORM_EPS) * ln_g + ln_b
    causal = lax.broadcasted_iota(jnp.int32, (t, t), 0) >= lax.broadcasted_iota(jnp.int32, (t, t), 1)
    outs = []
    for g in range(SGU_GROUPS):
        sl = slice(g * LANES, (g + 1) * LANES)
        mixed = bdot(jnp.where(causal, w[g], 0.0), vn[:, sl], 1, 0) + b_t[:, g:g + 1]
        outs.append(u[:, sl] * mixed)
    return jnp.concatenate(outs, axis=1)


def _sgu_specs(s, ts):
    return [pl.BlockSpec((ts, 2 * MIX), lambda i: (i, SEC_A // (2 * MIX))),
            pl.BlockSpec((1, MIX), lambda i: (0, 0)), pl.BlockSpec((1, MIX), lambda i: (0, 0)),
            pl.BlockSpec((SGU_GROUPS, SGU_CHUNK, SGU_CHUNK), lambda i: (0, 0, 0)),
            pl.BlockSpec((SGU_CHUNK, SGU_GROUPS), lambda i: (0, 0))]


def sgu_fwd(proj, ln_g, ln_b, w, b_t, *, name):
    s = proj.shape[0]
    ts = _tile(s, 1024)
    n_chunk = ts // SGU_CHUNK

    def body(p_ref, g_ref, b_ref, w_ref, bt_ref, o_ref):
        def step(c, carry):
            rows = pl.ds(pl.multiple_of(c * SGU_CHUNK, SGU_CHUNK), SGU_CHUNK)
            o_ref[rows, :] = _sgu_chunk(p_ref[rows, :].astype(f32), g_ref[...], b_ref[...], w_ref[...], bt_ref[...]).astype(bf16)
            return carry
        lax.fori_loop(0, n_chunk, step, 0)

    return pl.pallas_call(
        body, name=name, grid=(s // ts,), in_specs=_sgu_specs(s, ts),
        out_specs=pl.BlockSpec((ts, MIX), lambda i: (i, 0)), out_shape=jax.ShapeDtypeStruct((s, MIX), bf16),
        compiler_params=_params(1),
    )(proj, ln_g, ln_b, w, b_t)


def sgu_bwd(proj, ln_g, ln_b, w, b_t, d_out, dproj, *, name):
    s = proj.shape[0]
    ts = _tile(s, 1024)
    n_chunk = ts // SGU_CHUNK

    def body(p_ref, g_ref, b_ref, w_ref, bt_ref, do_ref, _, dp_ref, dg_ref, db_ref, dw_ref, dbt_ref):
        @pl.when(pl.program_id(0) == 0)
        def _():
            dg_ref[...] = jnp.zeros_like(dg_ref)
            db_ref[...] = jnp.zeros_like(db_ref)
            dw_ref[...] = jnp.zeros_like(dw_ref)
            dbt_ref[...] = jnp.zeros_like(dbt_ref)

        def step(c, carry):
            rows = pl.ds(pl.multiple_of(c * SGU_CHUNK, SGU_CHUNK), SGU_CHUNK)
            _, vjp = jax.vjp(_sgu_chunk, p_ref[rows, :].astype(f32), g_ref[...], b_ref[...], w_ref[...], bt_ref[...])
            dp, dg, db, dw, dbt = vjp(do_ref[rows, :])
            dp_ref[rows, :] = dp.astype(bf16)
            dg_ref[...] += dg
            db_ref[...] += db
            dw_ref[...] += dw
            dbt_ref[...] += dbt
            return carry
        lax.fori_loop(0, n_chunk, step, 0)

    specs = _sgu_specs(s, ts)
    return pl.pallas_call(
        body, name=name, grid=(s // ts,),
        in_specs=specs + [pl.BlockSpec((ts, MIX), lambda i: (i, 0)), pl.BlockSpec(memory_space=pl.ANY)],
        out_specs=[specs[0], specs[1], specs[2], specs[3], specs[4]],
        out_shape=[jax.ShapeDtypeStruct(dproj.shape, bf16), jax.ShapeDtypeStruct((1, MIX), f32),
                   jax.ShapeDtypeStruct((1, MIX), f32), jax.ShapeDtypeStruct(w.shape, f32),
                   jax.ShapeDtypeStruct(b_t.shape, f32)],
        input_output_aliases={6: 0}, compiler_params=_params(1),
    )(proj, ln_g, ln_b, w, b_t, d_out, dproj)


def rope_tables(posf, inv_freq, *, name):
    s = posf.shape[0]
    ts = _tile(s, 1024, 8)
    half = ROPE_DIM // 2

    def body(pos_ref, inv_ref, o_ref):
        d = lax.broadcasted_iota(jnp.int32, (1, LANES), 1) % SWA_HD
        ang = pos_ref[...] * inv_ref[...]
        sin = jnp.sin(ang)
        o_ref[0] = jnp.cos(ang)
        o_ref[1] = jnp.where(d < half, sin, 0.0)
        o_ref[2] = jnp.where((d >= half) & (d < ROPE_DIM), sin, 0.0)

    return pl.pallas_call(
        body, name=name, grid=(s // ts,),
        in_specs=[pl.BlockSpec((ts, 1), lambda i: (i, 0)), pl.BlockSpec((1, LANES), lambda i: (0, 0))],
        out_specs=pl.BlockSpec((3, ts, LANES), lambda i: (0, i, 0)), out_shape=jax.ShapeDtypeStruct((3, s, LANES), f32),
        compiler_params=_params(1),
    )(posf, inv_freq)


def _rope(x, table):
    w = x.shape[1]
    half = ROPE_DIM // 2
    c, lo, hi = (jnp.concatenate([table[i]] * (w // LANES), axis=1) for i in range(3))
    return x * c - lroll(x, w - half) * lo + lroll(x, half) * hi


def _swa_block(q, kp, kc, vp, vc, sink_row, table_q, table_p, prev_ok, wide):
    t = WINDOW
    q = _rope(q, table_q) * (SWA_HD ** -0.5)
    keys = jnp.concatenate([_rope(kp, table_p), _rope(kc, table_q)], axis=0)
    vals = jnp.concatenate([vp, vc], axis=0)
    own = lax.broadcasted_iota(jnp.int32, (t, t), 0) >= lax.broadcasted_iota(jnp.int32, (t, t), 1)
    lane_half = lax.broadcasted_iota(jnp.int32, (t, LANES), 1) // SWA_HD
    group = SWA_HEADS // SWA_KV
    slabs = []
    for pair in range(SWA_HEADS // 2):
        q_pair = q[:, pair * LANES:(pair + 1) * LANES]
        acc = jnp.zeros((t, LANES), f32)
        for half in range(2):
            h = 2 * pair + half
            kv = h // group
            qm = jnp.where(lane_half == half, q_pair, 0.0)
            if half != kv:
                qm = lroll(qm, SWA_HD)
            if wide:
                both = bdot(qm, keys, 1, 1)
                s_prev, s_own = both[:, :t], both[:, t:]
            else:
                s_prev, s_own = bdot(qm, keys[:t], 1, 1), bdot(qm, keys[t:], 1, 1)
            logits = jnp.where(own, s_own, jnp.where(prev_ok, s_prev, -1e30))
            sink = _lane_col(sink_row, h)
            m = lax.stop_gradient(jnp.maximum(jnp.max(logits, axis=1, keepdims=True), sink))
            p = jnp.exp(logits - m)
            probs = p * (1.0 / (jnp.sum(p, axis=1, keepdims=True) + jnp.exp(sink - m)))
            p_prev, p_own = jnp.where(own, 0.0, probs), jnp.where(own, probs, 0.0)
            if wide:
                o = bdot(jnp.concatenate([p_prev, p_own], axis=1), vals, 1, 0)
            else:
                o = bdot(p_prev, vals[:t], 1, 0) + bdot(p_own, vals[t:], 1, 0)
            o = jnp.where(lane_half == kv, o, 0.0)
            if half != kv:
                o = lroll(o, SWA_HD)
            acc = acc + o
        slabs.append(acc)
    return jnp.concatenate(slabs, axis=1)


def _swa_in_specs(nc, clamp):
    t = WINDOW
    qb, kb, vb = SEC_B // MIX, (SEC_B + MIX) // LANES, (SEC_B + MIX + LANES) // LANES

    def cur(i):
        return jnp.minimum(i, nc - 1) if clamp else i

    def prev(i):
        return jnp.maximum(cur(i) - 1, 0)

    return [pl.BlockSpec((t, MIX), lambda i: (cur(i), qb)),
            pl.BlockSpec((t, LANES), lambda i: (prev(i), kb)), pl.BlockSpec((t, LANES), lambda i: (cur(i), kb)),
            pl.BlockSpec((t, LANES), lambda i: (prev(i), vb)), pl.BlockSpec((t, LANES), lambda i: (cur(i), vb)),
            pl.BlockSpec((1, LANES), lambda i: (0, 0)),
            pl.BlockSpec((3, t, LANES), lambda i: (0, cur(i), 0)), pl.BlockSpec((3, t, LANES), lambda i: (0, prev(i), 0))]


def swa_fwd(proj, sink_row, tables, *, name):
    s = proj.shape[0]
    nc = s // WINDOW

    def body(q_ref, kp_ref, kc_ref, vp_ref, vc_ref, sink_ref, tq_ref, tp_ref, o_ref):
        prev_ok = pl.program_id(0) > 0
        blocks = [r[...].astype(f32) for r in (q_ref, kp_ref, kc_ref, vp_ref, vc_ref)]
        o_ref[...] = _swa_block(*blocks, sink_ref[...],
                                tq_ref[...], tp_ref[...], prev_ok, True).astype(bf16)

    return pl.pallas_call(
        body, name=name, grid=(nc,), in_specs=_swa_in_specs(nc, False),
        out_specs=pl.BlockSpec((WINDOW, MIX), lambda i: (i, 0)), out_shape=jax.ShapeDtypeStruct((s, MIX), bf16),
        compiler_params=_params(1),
    )(proj, proj, proj, proj, proj, sink_row, tables, tables)


def swa_bwd(proj, sink_row, tables, d_out, dproj, *, side=None, name):
    s = proj.shape[0]
    nc = s // WINDOW
    t = WINDOW

    def body(q_ref, kp_ref, kc_ref, vp_ref, vc_ref, sink_ref, tq_ref, tp_ref, do_ref, _,
             dp_ref, dsink_ref, cq_ref, ck_ref, cv_ref):
        i = pl.program_id(0)

        @pl.when(i == 0)
        def _():
            dsink_ref[...] = jnp.zeros_like(dsink_ref)

        def write(dk_prev, dv_prev):
            dp_ref[:, :MIX] = cq_ref[...].astype(bf16)
            dp_ref[:, MIX:MIX + LANES] = (ck_ref[...] + dk_prev).astype(bf16)
            dp_ref[:, MIX + LANES:] = (cv_ref[...] + dv_prev).astype(bf16)

        @pl.when(i < nc)
        def _():
            fn = functools.partial(_swa_block, table_q=tq_ref[...], table_p=tp_ref[...], prev_ok=i > 0, wide=False)
            blocks = [r[...].astype(f32) for r in (q_ref, kp_ref, kc_ref, vp_ref, vc_ref)]
            _, vjp = jax.vjp(fn, *blocks, sink_ref[...])
            dq, dkp, dkc, dvp, dvc, dsink = vjp(do_ref[...])
            dsink_ref[...] += dsink

            @pl.when(i > 0)
            def _():
                write(dkp, dvp)

            cq_ref[...] = dq
            ck_ref[...] = dkc
            cv_ref[...] = dvc

        @pl.when(i == nc)
        def _():
            write(0.0, 0.0)

    return host_call(
        side, body, name=name, grid=(nc + 1,),
        in_specs=_swa_in_specs(nc, True) + [pl.BlockSpec((t, MIX), lambda i: (jnp.minimum(i, nc - 1), 0)),
                                            pl.BlockSpec(memory_space=pl.ANY)],
        out_specs=[pl.BlockSpec((t, MIX + 2 * LANES), lambda i: (jnp.maximum(i - 1, 0), SEC_B // (MIX + 2 * LANES))),
                   pl.BlockSpec((1, LANES), lambda i: (0, 0))],
        out_shape=[jax.ShapeDtypeStruct(dproj.shape, bf16), jax.ShapeDtypeStruct((1, LANES), f32)],
        scratch_shapes=[pltpu.VMEM((t, MIX), f32), pltpu.VMEM((t, LANES), f32), pltpu.VMEM((t, LANES), f32)],
        aliases={9: 0}, args=(proj, proj, proj, proj, proj, sink_row, tables, tables, d_out, dproj))


CONV_PAD = 16


def _conv_taps(xp, rows):
    off = CONV_PAD - (DN_CONV - 1)
    return [xp[off + i:off + i + rows] for i in range(DN_CONV)]


def _conv_pre(taps, w):
    pre = taps[0] * w[0:1]
    for i in range(1, DN_CONV):
        pre = pre + taps[i] * w[i:i + 1]
    return pre


def conv_fwd(proj, conv_w8, *, name):
    s = proj.shape[0]
    wq = 3 * MIX
    ts = _tile(s, 512)
    nb = ts // CONV_PAD

    def body(x_ref, prev_ref, w_ref, o_ref):
        prev = jnp.where(pl.program_id(0) > 0, prev_ref[...].astype(f32), 0.0)
        pre = _conv_pre(_conv_taps(jnp.concatenate([prev, x_ref[...].astype(f32)], axis=0), ts), w_ref[...])
        o_ref[...] = pre * _sigmoid(pre)

    return pl.pallas_call(
        body, name=name, grid=(s // ts,),
        in_specs=[pl.BlockSpec((ts, wq), lambda i: (i, SEC_QKV // wq)),
                  pl.BlockSpec((CONV_PAD, wq), lambda i: (jnp.maximum(i * nb - 1, 0), SEC_QKV // wq)),
                  pl.BlockSpec((CONV_PAD, wq), lambda i: (0, 0))],
        out_specs=pl.BlockSpec((ts, wq), lambda i: (i, 0)), out_shape=jax.ShapeDtypeStruct((s, wq), f32),
        compiler_params=_params(1),
    )(proj, proj, conv_w8)


def conv_bwd(proj, conv_w8, dxc, dproj, *, side=None, name):
    s = proj.shape[0]
    wq = 3 * MIX
    ts = _tile(s, 512)
    nb = ts // CONV_PAD
    nt = s // ts
    last_blk = s // CONV_PAD - 1

    def body(x_ref, prev_ref, next_ref, w_ref, d_ref, dnext_ref, _, dp_ref, dw_ref):
        i = pl.program_id(0)

        @pl.when(i == 0)
        def _():
            dw_ref[...] = jnp.zeros_like(dw_ref)

        w = w_ref[...]
        prev = jnp.where(i > 0, prev_ref[...].astype(f32), 0.0)
        more = i < nt - 1
        xp = jnp.concatenate([prev, x_ref[...].astype(f32), jnp.where(more, next_ref[...].astype(f32), 0.0)], axis=0)
        taps = _conv_taps(xp, ts + CONV_PAD)
        pre = _conv_pre(taps, w)
        sig = _sigmoid(pre)
        dxc_ext = jnp.concatenate([d_ref[...], jnp.where(more, dnext_ref[...], 0.0)], axis=0)
        dpre = dxc_ext * sig * (1.0 + pre * (1.0 - sig))
        d_raw = jnp.zeros((ts, wq), f32)
        dws = []
        for k in range(DN_CONV):
            shift = DN_CONV - 1 - k
            d_raw = d_raw + dpre[shift:shift + ts] * w[k:k + 1]
            dws.append(jnp.sum(dpre[:ts] * taps[k][:ts], axis=0, keepdims=True))
        dp_ref[...] = d_raw.astype(bf16)
        dw_ref[...] += jnp.concatenate(dws + [jnp.zeros((CONV_PAD - DN_CONV, wq), f32)], axis=0)

    sec = SEC_QKV // wq
    return host_call(
        side, body, name=name, grid=(nt,),
        in_specs=[pl.BlockSpec((ts, wq), lambda i: (i, sec)),
                  pl.BlockSpec((CONV_PAD, wq), lambda i: (jnp.maximum(i * nb - 1, 0), sec)),
                  pl.BlockSpec((CONV_PAD, wq), lambda i: (jnp.minimum((i + 1) * nb, last_blk), sec)),
                  pl.BlockSpec((CONV_PAD, wq), lambda i: (0, 0)),
                  pl.BlockSpec((ts, wq), lambda i: (i, 0)),
                  pl.BlockSpec((CONV_PAD, wq), lambda i: (jnp.minimum((i + 1) * nb, last_blk), 0)),
                  pl.BlockSpec(memory_space=pl.ANY)],
        out_specs=[pl.BlockSpec((ts, wq), lambda i: (i, sec)), pl.BlockSpec((CONV_PAD, wq), lambda i: (0, 0))],
        out_shape=[jax.ShapeDtypeStruct(dproj.shape, bf16), jax.ShapeDtypeStruct((CONV_PAD, wq), f32)],
        scratch_shapes=[], aliases={6: 0}, args=(proj, proj, proj, conv_w8, dxc, dxc, dproj))


def _dn_chunk(state, xc, z, ba, alog_row, dtb_row, norm_row, t_known):
    c, nh = DN_CHUNK, DN_HEADS
    n = c * nh
    row = lax.broadcasted_iota(jnp.int32, (n, n), 0)
    col = lax.broadcasted_iota(jnp.int32, (n, n), 1)
    same_head = (row // c) == (col // c)
    tril, strict = same_head & (row >= col), same_head & (row > col)
    tril_c = lax.broadcasted_iota(jnp.int32, (c, c), 0) >= lax.broadcasted_iota(jnp.int32, (c, c), 1)
    beta_all = _sigmoid(ba)
    g_all = -jnp.exp(alog_row) * jax.nn.softplus(ba + dtb_row)
    gc_all = hdot(tril_c.astype(f32), g_all)
    gc_t = gc_all.T

    def stack(piece):
        return jnp.concatenate([piece(h) for h in range(nh)], axis=0)

    q = stack(lambda h: xc[:, h * DN_HD:(h + 1) * DN_HD])
    k = stack(lambda h: xc[:, MIX + h * DN_HD:MIX + (h + 1) * DN_HD])
    v = stack(lambda h: xc[:, 2 * MIX + h * DN_HD:2 * MIX + (h + 1) * DN_HD])
    zs = stack(lambda h: z[:, h * DN_HD:(h + 1) * DN_HD])
    q = q * lax.rsqrt(jnp.sum(q * q, axis=-1, keepdims=True) + NORM_EPS) * (DN_HD ** -0.5)
    k = k * lax.rsqrt(jnp.sum(k * k, axis=-1, keepdims=True) + NORM_EPS)
    beta = stack(lambda h: _lane_col(beta_all, h))
    g_cols = [_lane_col(gc_all, nh + h) for h in range(nh)]
    g_col = jnp.concatenate(g_cols, axis=0)
    g_row = jnp.concatenate([gc_t[nh + h:nh + h + 1, :] for h in range(nh)], axis=1)
    g_last = stack(lambda h: jnp.broadcast_to(g_cols[h][c - 1:c, :], (c, 1)))
    decay = jnp.where(tril, jnp.exp(jnp.where(tril, g_col - g_row, 0.0)), 0.0)
    kb = k * beta
    low = jnp.where(strict, bdot(kb, k, 1, 1) * decay, 0.0)
    t_inv = tri_inv(low, c) if t_known is None else tri_inv_known(low, t_known)
    e_gc = jnp.exp(g_col)
    uw = bdot(t_inv, jnp.concatenate([v * beta, kb * e_gc], axis=1), 1, 0)
    u, w = uw[:, :DN_HD], uw[:, DN_HD:]
    attn = bdot(q, k, 1, 1) * decay
    own = (lax.broadcasted_iota(jnp.int32, (n, nh * DN_HD), 1) // DN_HD
           == lax.broadcasted_iota(jnp.int32, (n, nh * DN_HD), 0) // c)

    def spread(a):
        return jnp.where(own, jnp.concatenate([a] * nh, axis=1), 0.0)

    v_new = u - bdot(spread(w), state, 1, 0)
    o = bdot(spread(q * e_gc), state, 1, 0) + bdot(attn, v_new, 1, 0)
    keep = stack(lambda h: jnp.broadcast_to(jnp.exp(g_cols[h][c - 1:c, :]), (DN_HD, 1)))
    new_state = state * keep + bdot(spread(k * jnp.exp(g_last - g_col)), v_new, 0, 0)
    out = _rms(o, norm_row) * (zs * _sigmoid(zs))
    return new_state, jnp.concatenate([out[h * c:(h + 1) * c] for h in range(nh)], axis=1), t_inv


DN_STEP = 8 * DN_CHUNK


def _dn_step(state, xc, z, ba, alog_row, dtb_row, norm_row, t_known=None):
    outs, t_invs = [], []
    for c in range(DN_STEP // DN_CHUNK):
        rows = slice(c * DN_CHUNK, (c + 1) * DN_CHUNK)
        state, out, t_inv = _dn_chunk(state, xc[rows], z[rows], ba[rows], alog_row, dtb_row, norm_row,
                                      None if t_known is None else t_known[c])
        outs.append(out)
        t_invs.append(t_inv)
    return state, jnp.concatenate(outs, axis=0), jnp.stack(t_invs)


def _dn_specs(ts, order):
    zb = SEC_Z // MIX
    return [pl.BlockSpec((ts, 3 * MIX), lambda i: (order(i), 0)),
            pl.BlockSpec((ts, MIX), lambda i: (order(i), zb)),
            pl.BlockSpec((ts, LANES), lambda i: (order(i), 0)),
            pl.BlockSpec((1, LANES), lambda i: (0, 0)), pl.BlockSpec((1, LANES), lambda i: (0, 0)),
            pl.BlockSpec((1, LANES), lambda i: (0, 0))]


def dn_fwd(xc, proj, p_ba, alog_row, dtb_row, norm_row, *, side=None, name):
    s = xc.shape[0]
    ts = _tile(s, 512)
    n_step = ts // DN_STEP

    per_step = DN_STEP // DN_CHUNK
    n_tri = DN_HEADS * DN_CHUNK

    def body(xc_ref, z_ref, ba_ref, al_ref, dt_ref, nr_ref, o_ref, st_ref, tri_ref, state_ref):
        @pl.when(pl.program_id(0) == 0)
        def _():
            state_ref[...] = jnp.zeros_like(state_ref)

        def step(c, carry):
            rows = pl.ds(pl.multiple_of(c * DN_STEP, DN_STEP), DN_STEP)
            st_ref[c] = state_ref[...]
            new_state, out, t_invs = _dn_step(state_ref[...], xc_ref[rows, :], z_ref[rows, :].astype(f32), ba_ref[rows, :],
                                              al_ref[...], dt_ref[...], nr_ref[...])
            state_ref[...] = new_state
            o_ref[rows, :] = out.astype(bf16)
            tri_ref[pl.ds(c * per_step, per_step)] = t_invs
            return carry
        lax.fori_loop(0, n_step, step, 0)

    return host_call(
        side, body, name=name, grid=(s // ts,), in_specs=_dn_specs(ts, lambda i: i),
        out_specs=[pl.BlockSpec((ts, MIX), lambda i: (i, 0)),
                   pl.BlockSpec((n_step, DN_HEADS * DN_HD, DN_HD), lambda i: (i, 0, 0)),
                   pl.BlockSpec((n_step * per_step, n_tri, n_tri), lambda i: (i, 0, 0))],
        out_shape=[jax.ShapeDtypeStruct((s, MIX), bf16),
                   jax.ShapeDtypeStruct((s // DN_STEP, DN_HEADS * DN_HD, DN_HD), f32),
                   jax.ShapeDtypeStruct((s // DN_CHUNK, n_tri, n_tri), f32)],
        scratch_shapes=[pltpu.VMEM((DN_HEADS * DN_HD, DN_HD), f32)], aliases={},
        args=(xc, proj, p_ba, alog_row, dtb_row, norm_row))


def dn_bwd(xc, proj, p_ba, alog_row, dtb_row, norm_row, saved, tri, d_out, dproj, *, side=None, name):
    s = xc.shape[0]
    ts = _tile(s, 512)
    n_step = ts // DN_STEP
    nt = s // ts

    per_step = DN_STEP // DN_CHUNK
    n_tri = DN_HEADS * DN_CHUNK

    def body(xc_ref, z_ref, ba_ref, al_ref, dt_ref, nr_ref, st_ref, tri_ref, do_ref, _,
             dz_ref, dxc_ref, dba_ref, dal_ref, ddt_ref, dnr_ref, dstate_ref):
        @pl.when(pl.program_id(0) == 0)
        def _():
            dstate_ref[...] = jnp.zeros_like(dstate_ref)
            dal_ref[...] = jnp.zeros_like(dal_ref)
            ddt_ref[...] = jnp.zeros_like(ddt_ref)
            dnr_ref[...] = jnp.zeros_like(dnr_ref)

        def step(it, carry):
            c = n_step - 1 - it
            rows = pl.ds(pl.multiple_of(c * DN_STEP, DN_STEP), DN_STEP)
            t_known = tri_ref[pl.ds(c * per_step, per_step)]
            _, vjp = jax.vjp(lambda *a: _dn_step(*a, t_known=t_known)[:2], st_ref[c], xc_ref[rows, :],
                             z_ref[rows, :].astype(f32), ba_ref[rows, :], al_ref[...], dt_ref[...], nr_ref[...])
            d_in, dxc, dz, dba, dal, ddt, dnr = vjp((dstate_ref[...], do_ref[rows, :]))
            dstate_ref[...] = d_in
            dxc_ref[rows, :] = dxc
            dz_ref[rows, :] = dz.astype(bf16)
            dba_ref[rows, :] = dba.astype(bf16)
            dal_ref[...] += dal
            ddt_ref[...] += ddt
            dnr_ref[...] += dnr
            return carry
        lax.fori_loop(0, n_step, step, 0)

    def rev(i):
        return nt - 1 - i

    specs = _dn_specs(ts, rev)
    vec = pl.BlockSpec((1, LANES), lambda i: (0, 0))
    return host_call(
        side, body, name=name, grid=(nt,),
        in_specs=specs + [pl.BlockSpec((n_step, DN_HEADS * DN_HD, DN_HD), lambda i: (rev(i), 0, 0)),
                          pl.BlockSpec((n_step * per_step, n_tri, n_tri), lambda i: (rev(i), 0, 0)),
                          pl.BlockSpec((ts, MIX), lambda i: (rev(i), 0)), pl.BlockSpec(memory_space=pl.ANY)],
        out_specs=[specs[1], specs[0], specs[2], vec, vec, vec],
        out_shape=[jax.ShapeDtypeStruct(dproj.shape, bf16), jax.ShapeDtypeStruct((s, 3 * MIX), f32),
                   jax.ShapeDtypeStruct((s, LANES), bf16)] + [jax.ShapeDtypeStruct((1, LANES), f32)] * 3,
        scratch_shapes=[pltpu.VMEM((DN_HEADS * DN_HD, DN_HD), f32)], aliases={9: 0},
        args=(xc, proj, p_ba, alog_row, dtb_row, norm_row, saved, tri, d_out, dproj))


def _merge_in_specs(ts, d):
    row = pl.BlockSpec((ts, MIX), lambda i: (i, 0))
    return [row, row, row, pl.BlockSpec((ts, 3 * d), lambda i: (i, SEC_G // (3 * d))),
            pl.BlockSpec((3, MIX, d), lambda i: (0, 0, 0))]


def merge_fwd(out_a, out_b, out_c, proj, w_branch, *, name):
    s, d = out_a.shape[0], w_branch.shape[2]
    ts = _tile(s, 512, 16)

    def body(a_ref, b_ref, c_ref, g_ref, w_ref, o_ref):
        acc = jnp.zeros((ts, d), f32)
        for n, r in enumerate((a_ref, b_ref, c_ref)):
            acc = acc + _sigmoid(g_ref[:, n * d:(n + 1) * d].astype(f32)) * _dg(r[...], w_ref[n], 1, 0)
        o_ref[...] = acc.astype(bf16)

    return pl.pallas_call(
        body, name=name, grid=(s // ts,), in_specs=_merge_in_specs(ts, d),
        out_specs=pl.BlockSpec((ts, d), lambda i: (i, 0)), out_shape=jax.ShapeDtypeStruct((s, d), bf16),
        compiler_params=_params(1),
    )(out_a, out_b, out_c, proj, w_branch)


def merge_bwd(out_a, out_b, out_c, proj, w_branch, d_merged, dproj, *, name):
    s, d = out_a.shape[0], w_branch.shape[2]
    ts = _tile(s, 512, 16)

    def body(a_ref, b_ref, c_ref, g_ref, w_ref, dm_ref, _, dg_ref, da_ref, db_ref, dc_ref, dw_ref):
        @pl.when(pl.program_id(0) == 0)
        def _():
            dw_ref[...] = jnp.zeros_like(dw_ref)

        dm = dm_ref[...]
        for n, (r, dr) in enumerate(((a_ref, da_ref), (b_ref, db_ref), (c_ref, dc_ref))):
            gate = _sigmoid(g_ref[:, n * d:(n + 1) * d].astype(f32))
            branch = _dg(r[...], w_ref[n], 1, 0)
            dg_ref[:, n * d:(n + 1) * d] = (dm * branch * gate * (1.0 - gate)).astype(bf16)
            d_branch = dm * gate
            dr[...] = _dg(d_branch, w_ref[n], 1, 1)
            dw_ref[n] += _dg(r[...], d_branch, 0, 0)

    specs = _merge_in_specs(ts, d)
    row_f = pl.BlockSpec((ts, MIX), lambda i: (i, 0))
    return pl.pallas_call(
        body, name=name, grid=(s // ts,),
        in_specs=specs + [pl.BlockSpec((ts, d), lambda i: (i, 0)), pl.BlockSpec(memory_space=pl.ANY)],
        out_specs=[specs[3], row_f, row_f, row_f, specs[4]],
        out_shape=[jax.ShapeDtypeStruct(dproj.shape, bf16)] + [jax.ShapeDtypeStruct((s, MIX), f32)] * 3
        + [jax.ShapeDtypeStruct(w_branch.shape, f32)],
        input_output_aliases={6: 0}, compiler_params=_params(1),
    )(out_a, out_b, out_c, proj, w_branch, d_merged, dproj)


def gate_up_swiglu(h, w_gu, *, name):
    g2, d, w = w_gu.shape
    ng, s = g2 // 2, h.shape[0]
    tm = _tile(s, 1024, 16)

    def body(h_ref, wg_ref, wu_ref, g_ref, u_ref, a_ref):
        g16 = _dg(h_ref[...], wg_ref[...], 1, 0).astype(bf16)
        u16 = _dg(h_ref[...], wu_ref[...], 1, 0).astype(bf16)
        g_ref[...], u_ref[...] = g16, u16
        g = g16.astype(f32)
        a_ref[...] = (g * _sigmoid(g) * u16.astype(f32)).astype(bf16)

    out = pl.BlockSpec((None, tm, w), lambda i, j: (j, i, 0))
    return pl.pallas_call(
        body, name=name, grid=(s // tm, ng),
        in_specs=[pl.BlockSpec((tm, d), lambda i, j: (i, 0)), pl.BlockSpec((None, d, w), lambda i, j: (j, 0, 0)),
                  pl.BlockSpec((None, d, w), lambda i, j: (ng + j, 0, 0))],
        out_specs=[out, out, out], out_shape=[jax.ShapeDtypeStruct((ng, s, w), bf16)] * 3, compiler_params=_params(2),
    )(h, w_gu, w_gu)


def down_dx_swiglu(dx, w_down, gate, up, *, name):
    ng, w, d = w_down.shape
    s = dx.shape[0]
    tm = _tile(s, 1024, 16)

    def body(dx_ref, wd_ref, g_ref, u_ref, dg_ref, du_ref):
        d_act = _dg(dx_ref[...], wd_ref[...], 1, 1)
        g = g_ref[...].astype(f32)
        sig = _sigmoid(g)
        dg_ref[...] = (d_act * u_ref[...].astype(f32) * sig * (1.0 + g * (1.0 - sig))).astype(bf16)
        du_ref[...] = (d_act * g * sig).astype(bf16)

    blk = pl.BlockSpec((None, tm, w), lambda i, j: (j, i, 0))
    return pl.pallas_call(
        body, name=name, grid=(s // tm, ng),
        in_specs=[pl.BlockSpec((tm, d), lambda i, j: (i, 0)), pl.BlockSpec((None, w, d), lambda i, j: (j, 0, 0)), blk, blk],
        out_specs=[blk, blk], out_shape=[jax.ShapeDtypeStruct((ng, s, w), bf16)] * 2, compiler_params=_params(2),
    )(dx, w_down, gate, up)


def adamw(w, m, v, g_parts, *, name):
    n_layers = len(g_parts)
    n_parts, r, cols = g_parts[0].shape
    lanes = -(-cols // LANES) * LANES
    tr = _tile(r, max(16, (128 * 1024) // lanes), 16)
    nr = r // tr

    def body(w_ref, m_ref, v_ref, *rest):
        gp_refs, (g_ref, d_ref, nm_ref, nv_ref) = rest[:n_layers], rest[n_layers:]
        layer = pl.program_id(0)
        g = jnp.zeros((tr, cols), f32)
        for l, gp_ref in enumerate(gp_refs):
            g_l = gp_ref[0].astype(f32)
            for k in range(1, n_parts):
                g_l = g_l + gp_ref[k].astype(f32)
            g = jnp.where(layer == l, g_l, g)
        nm = ADAM_B1 * m_ref[...] + (1.0 - ADAM_B1) * g
        nv = ADAM_B2 * v_ref[...] + (1.0 - ADAM_B2) * jnp.square(g)
        m_hat = nm / (1.0 - ADAM_B1 ** ADAM_STEP)
        v_hat = nv / (1.0 - ADAM_B2 ** ADAM_STEP)
        g_ref[...] = g
        d_ref[...] = -ADAM_LR * (m_hat / (jnp.sqrt(v_hat) + ADAM_EPS) + ADAM_WD * w_ref[...])
        nm_ref[...] = nm
        nv_ref[...] = nv

    row = pl.BlockSpec((tr, cols), lambda l, i: (l * nr + i, 0))

    def parts_spec(own):
        return pl.BlockSpec((n_parts, tr, cols),
                            lambda l, i: (0, jnp.where(l == own, i, jnp.where(l < own, 0, nr - 1)), 0))

    return pl.pallas_call(
        body, name=name, grid=(n_layers, nr), in_specs=[row, row, row] + [parts_spec(l) for l in range(n_layers)],
        out_specs=[row] * 4, out_shape=[jax.ShapeDtypeStruct(w.shape, f32)] * 4, compiler_params=_params(2),
    )(w, m, v, *g_parts)


def _mesh_pos():
    return lax.axis_index("x"), lax.axis_index("y"), lax.axis_index("c")


def _dev_index(p):
    return 4 * p[0] + 2 * p[1] + p[2]


class Exchange:
    def __init__(self, kind, arrays):
        self.kind, self.arrays, self.n = kind, list(arrays), len(arrays)
        self.specs = [pl.BlockSpec(memory_space=pl.ANY)] * self.n
        self.out_shapes = [jax.ShapeDtypeStruct(((N_DEV,) if kind == "gather" else ()) + a.shape, a.dtype)
                           for a in self.arrays]
        self.scratch = [pltpu.SemaphoreType.DMA((self.n, N_DEV - 1)), pltpu.SemaphoreType.DMA((self.n, N_DEV - 1)),
                        pltpu.SemaphoreType.DMA((self.n,))]

    def _copies(self, in_refs, out_refs, sems, with_arrivals):
        send_sems, recv_sems, local_sems = sems
        x, y, c = _mesh_pos()
        mine = _dev_index((x, y, c))

        def src(a, slab):
            return in_refs[a] if self.kind == "gather" else in_refs[a].at[slab]

        local = [pltpu.make_async_copy(src(a, mine), out_refs[a].at[mine], local_sems.at[a]) for a in range(self.n)]
        sends, arrivals = [], []
        for k in range(1, N_DEV):
            peer = (1 - x if k & 4 else x, 1 - y if k & 2 else y, 1 - c if k & 1 else c)
            theirs = _dev_index(peer)
            for a in range(self.n):
                to = dict(send_sem=send_sems.at[a, k - 1], recv_sem=recv_sems.at[a, k - 1], device_id=peer,
                          device_id_type=pl.DeviceIdType.MESH)
                sends.append(pltpu.make_async_remote_copy(src_ref=src(a, theirs), dst_ref=out_refs[a].at[mine], **to))
                if with_arrivals:
                    arrivals.append(pltpu.make_async_remote_copy(src_ref=src(a, theirs),
                                                                 dst_ref=out_refs[a].at[theirs], **to))
        return local, sends, arrivals

    def start(self, in_refs, out_refs, sems):
        local, sends, _ = self._copies(in_refs, out_refs, sems, False)
        for cp in local + sends:
            cp.start()

    def wait(self, in_refs, out_refs, sems):
        local, sends, arrivals = self._copies(in_refs, out_refs, sems, True)
        for cp in arrivals:
            cp.wait_recv()
        for cp in sends:
            cp.wait_send()
        for cp in local:
            cp.wait()

    def run_around(self, grid, in_refs, out_refs, sems, *, before):
        at = None
        for axis, size in enumerate(grid):
            hit = pl.program_id(axis) == (0 if before else size - 1)
            at = hit if at is None else at & hit

        @pl.when(at)
        def _():
            (self.start if before else self.wait)(in_refs, out_refs, sems)


def host_call(side, body, *, name, grid, in_specs, out_specs, out_shape, scratch_shapes, args, aliases):
    n_in, n_out = len(in_specs), len(out_specs)
    if side is None:
        kernel_body = body
    else:
        n = side.n
        in_specs, args = in_specs + side.specs, tuple(args) + tuple(side.arrays)
        out_specs, out_shape = out_specs + side.specs, out_shape + side.out_shapes
        scratch_shapes = scratch_shapes + side.scratch

        def kernel_body(*refs):
            ins, side_in = refs[:n_in], refs[n_in:n_in + n]
            outs, side_out = refs[n_in + n:n_in + n + n_out], refs[n_in + n + n_out:n_in + 2 * n + n_out]
            scratch, sems = refs[n_in + 2 * n + n_out:-3], refs[-3:]
            side.run_around(grid, side_in, side_out, sems, before=True)
            body(*ins, *outs, *scratch)
            side.run_around(grid, side_in, side_out, sems, before=False)

    outs = pl.pallas_call(
        kernel_body, name=name, grid=grid, in_specs=in_specs, out_specs=out_specs, out_shape=out_shape,
        scratch_shapes=scratch_shapes, input_output_aliases=aliases, compiler_params=_params(len(grid)),
    )(*args)
    return outs if side is None else (outs[:n_out], outs[n_out:])


def all_gather(blocks, *, name):
    n = len(blocks)
    any_spec = pl.BlockSpec(memory_space=pl.ANY)

    def body(*refs):
        ins, outs = refs[:n], refs[n:2 * n]
        send_sems, recv_sems, local_sems = refs[2 * n:]
        x, y, c = _mesh_pos()
        me, sibling = (x, y, c), (x, y, 1 - c)
        chips = [(1 - x, y), (x, 1 - y), (1 - x, 1 - y)]

        def copy(a, k, block, to, src=None):
            dst = outs[a].at[_dev_index(block)]
            return pltpu.make_async_remote_copy(
                src_ref=dst if src is None else src, dst_ref=dst, send_sem=send_sems.at[a, k],
                recv_sem=recv_sems.at[a, k], device_id=to, device_id_type=pl.DeviceIdType.MESH)

        mine = [pltpu.make_async_copy(ins[a], outs[a].at[_dev_index(me)], local_sems.at[a]) for a in range(n)]
        for cp in mine:
            cp.start()
        first = []
        for a in range(n):
            first.append(copy(a, 0, me, sibling, src=ins[a]))
            first += [copy(a, 1 + j, me, (*chip, c), src=ins[a]) for j, chip in enumerate(chips)]
        for cp in first:
            cp.start()
        passed = []
        for j, chip in enumerate(chips):
            for a in range(n):
                copy(a, 1 + j, (*chip, c), me).wait_recv()
                fwd = copy(a, 4 + j, (*chip, c), sibling)
                fwd.start()
                passed.append(fwd)
        for a in range(n):
            copy(a, 0, sibling, me).wait_recv()
            for j, chip in enumerate(chips):
                copy(a, 4 + j, (*chip, 1 - c), me).wait_recv()
        for cp in first + passed:
            cp.wait_send()
        for cp in mine:
            cp.wait()

    return pl.pallas_call(
        body, name=name, in_specs=[any_spec] * n, out_specs=[any_spec] * n,
        out_shape=[jax.ShapeDtypeStruct((N_DEV,) + b.shape, b.dtype) for b in blocks],
        scratch_shapes=[pltpu.SemaphoreType.DMA((n, 7)), pltpu.SemaphoreType.DMA((n, 7)),
                        pltpu.SemaphoreType.DMA((n,))],
    )(*blocks)


def exchange(jobs, *, name):
    total = sum(j.n for j in jobs)

    def body(*refs):
        ins, outs, sems = refs[:total], refs[total:2 * total], refs[2 * total:]
        pieces, off = [], 0
        for i, j in enumerate(jobs):
            pieces.append((ins[off:off + j.n], outs[off:off + j.n], sems[3 * i:3 * i + 3]))
            off += j.n
        for j, piece in zip(jobs, pieces):
            j.start(*piece)
        for j, piece in zip(jobs, pieces):
            j.wait(*piece)

    outs = pl.pallas_call(
        body, name=name, in_specs=[s for j in jobs for s in j.specs], out_specs=[s for j in jobs for s in j.specs],
        out_shape=[s for j in jobs for s in j.out_shapes], scratch_shapes=[s for j in jobs for s in j.scratch],
    )(*[a for j in jobs for a in j.arrays])
    split, off = [], 0
    for j in jobs:
        split.append(outs[off:off + j.n])
        off += j.n
    return split


def _rows128(arr):
    flat = arr.reshape(-1)
    rows = -(-flat.shape[0] // (8 * LANES)) * 8
    return jnp.pad(flat, (0, rows * LANES - flat.shape[0])).reshape(rows, LANES)


def _pad_lanes(row, width=LANES, at=0):
    return jnp.pad(row, (at, width - at - row.shape[0])).reshape(1, width)


def _w_in_sections(got):
    d = got.shape[1]
    wi = jnp.transpose(got, (1, 0, 2)).reshape(d, -1)
    w_main = jnp.concatenate([wi[:, :C_B], wi[:, C_Z:C_BA], wi[:, C_QKV:C_Z], wi[:, C_G:], wi[:, C_B:C_QKV]], axis=1)
    return w_main, jnp.pad(wi[:, C_BA:C_G], ((0, 0), (0, LANES - (C_G - C_BA))))


def _w_in_parts(gw_main, gw_ba):
    d = gw_main.shape[0]
    full = jnp.concatenate([gw_main[:, SEC_A:SEC_Z], gw_main[:, SEC_B:], gw_main[:, SEC_QKV:SEC_G],
                            gw_main[:, SEC_Z:SEC_QKV], gw_ba[:, :C_G - C_BA], gw_main[:, SEC_G:SEC_B]], axis=1)
    return jnp.transpose(full.reshape(d, N_DEV, -1), (1, 0, 2))


def kernel(x, positions, attn_norm, w_in, sgu_ln_g, sgu_ln_b, sgu_w, sgu_b, attn_sinks, dn_conv_w, dn_a_log, dn_dt_bias, dn_norm, w_branch, w_out, ffn_norm, w_gate_up, w_down, final_norm, loss_target, m_attn_norm, m_w_in, m_sgu_ln_g, m_sgu_ln_b, m_sgu_w, m_sgu_b, m_attn_sinks, m_dn_conv_w, m_dn_a_log, m_dn_dt_bias, m_dn_norm, m_w_branch, m_w_out, m_ffn_norm, m_w_gate_up, m_w_down, m_final_norm, v_attn_norm, v_w_in, v_sgu_ln_g, v_sgu_ln_b, v_sgu_w, v_sgu_b, v_attn_sinks, v_dn_conv_w, v_dn_a_log, v_dn_dt_bias, v_dn_norm, v_w_branch, v_w_out, v_ffn_norm, v_w_gate_up, v_w_down, v_final_norm):
    given = dict(locals())
    depth, d_model = attn_norm.shape
    s = x.shape[1]
    x2 = x.reshape(s, d_model)
    target = loss_target.reshape(s, d_model)
    posf = positions.reshape(s, 1).astype(f32)
    inv_freq = ROPE_THETA ** (-jnp.arange(0, ROPE_DIM, 2, dtype=f32) / ROPE_DIM)
    inv_head = jnp.concatenate([inv_freq, inv_freq, jnp.zeros((SWA_HD - ROPE_DIM,), f32)])
    tables = rope_tables(posf, jnp.tile(inv_head, LANES // SWA_HD).reshape(1, LANES), name="rope_tables")

    assert depth == 2, depth
    gathered = dict(zip([("w_in", 0), ("dn_conv_w", 0), ("dn_conv_w", 1)], all_gather(
        [w_in[0].astype(bf16), dn_conv_w[0], dn_conv_w[1]], name="gather_first")))
    riders = {"l0_in_proj": [("w_branch", 0), ("w_out", 0), ("w_down", 0)],
              "l0_deltanet": [("w_gate_up", 0), ("w_in", 1)],
              "l1_deltanet": [("w_branch", 1), ("w_out", 1), ("w_gate_up", 1), ("w_down", 1)]}

    def gathering(host, call, *args, **kw):
        if host not in riders:
            return call(*args, name=host, **kw)
        side = Exchange("gather", [given[n][l].astype(bf16) for n, l in riders[host]])
        out, got = call(*args, side=side, name=host, **kw)
        gathered.update(zip(riders[host], got))
        return out

    layers, saved = [], []
    h_in = x2
    for l in range(depth):
        t = f"l{l}_"
        w_main, w_ba = _w_in_sections(gathered["w_in", l])
        conv_full = jnp.transpose(gathered["dn_conv_w", l], (1, 0, 2)).reshape(DN_CONV, -1)
        p = dict(
            w_main=w_main, w_ba=w_ba, conv_w8=jnp.pad(conv_full, ((0, CONV_PAD - DN_CONV), (0, 0))),
            attn_norm=attn_norm[l].reshape(1, -1), ffn_norm=ffn_norm[l].reshape(1, -1),
            ln_g=sgu_ln_g[l].reshape(1, -1), ln_b=sgu_ln_b[l].reshape(1, -1), sgu_w=sgu_w[l], sgu_bt=sgu_b[l].T,
            sink_row=_pad_lanes(attn_sinks[l]), alog_row=_pad_lanes(dn_a_log[l], at=DN_HEADS),
            dtb_row=_pad_lanes(dn_dt_bias[l], at=DN_HEADS), norm_row=dn_norm[l].reshape(1, -1))
        layers.append(p)
        h = rmsnorm_fwd(h_in, p["attn_norm"], name=t + "attn_norm")
        proj = gathering(t + "in_proj", matmul, h, p["w_main"], "nn", bf16)
        p_ba = matmul(h, p["w_ba"], "nn", f32, name=t + "in_proj_ba")
        out_a = sgu_fwd(proj, p["ln_g"], p["ln_b"], p["sgu_w"], p["sgu_bt"], name=t + "sgu")
        out_b = swa_fwd(proj, p["sink_row"], tables, name=t + "swa")
        xc = conv_fwd(proj, p["conv_w8"], name=t + "dn_conv")
        out_c, states, tri = gathering(t + "deltanet", dn_fwd, xc, proj, p_ba, p["alog_row"], p["dtb_row"], p["norm_row"])
        p.update(w_branch=jnp.transpose(gathered["w_branch", l], (1, 2, 0, 3)).reshape(3, MIX, d_model),
                 w_out=gathered["w_out", l].reshape(d_model, d_model),
                 w_gu=gathered["w_gate_up", l],
                 w_down=gathered["w_down", l].reshape(N_DEV // 2, -1, d_model))
        merged = merge_fwd(out_a, out_b, out_c, proj, p["w_branch"], name=t + "merge")
        x_mid = matmul(merged, p["w_out"], "nn", f32, residual=h_in, tn=1024, name=t + "out_proj")
        h2 = rmsnorm_fwd(x_mid, p["ffn_norm"], name=t + "ffn_norm")
        gate, up, act = gate_up_swiglu(h2, p["w_gu"], name=t + "gate_up")
        x_out = matmul(act, p["w_down"], "nn", f32, residual=x_mid, group="k", tn=1024, name=t + "down")
        saved.append(dict(x_in=h_in, h=h, proj=proj, p_ba=p_ba, out_a=out_a, out_b=out_b, out_c=out_c, xc=xc,
                          states=states, tri=tri, merged=merged, x_mid=x_mid, h2=h2, gate=gate, up=up, act=act))
        h_in = x_out

    dx, d_final_norm, loss_row = loss_head(h_in, final_norm.reshape(1, -1), target, name="loss_head")
    loss = lax.psum(loss_row[0, 0], MESH_AXES)

    shard_names = ["w_in", "dn_conv_w", "w_branch", "w_out", "w_gate_up", "w_down"]
    rep_names = ["attn_norm", "sgu_ln_g", "sgu_ln_b", "sgu_w", "sgu_b", "attn_sinks", "dn_a_log", "dn_dt_bias",
                 "dn_norm", "ffn_norm"]
    parts, received, per_layer = {}, {}, []
    senders = {"l0_b_swa": [("w_gate_up", 1), ("w_down", 1), ("w_out", 1), ("w_branch", 1)],
               "l0_b_deltanet": [("w_in", 1), ("w_gate_up", 0)],
               "l0_b_dn_conv": [("w_out", 0), ("w_branch", 0), ("w_down", 0)],
               "l0_b_in_proj_dx": [("w_in", 0)]}

    def scattering(host, call, *args, **kw):
        if host not in senders:
            return call(*args, name=host, **kw)
        out, got = call(*args, side=Exchange("scatter", [parts[key] for key in senders[host]]), name=host, **kw)
        received.update(zip(senders[host], got))
        return out

    for l in reversed(range(depth)):
        p, sv, t = layers[l], saved[l], f"l{l}_b_"
        d_gate, d_up = down_dx_swiglu(dx, p["w_down"], sv["gate"], sv["up"], name=t + "down_dx")
        gw_down = matmul(sv["act"], dx, "tn", bf16, group="m", tk=2048, tn=512, name=t + "down_dw")
        gw_gu = jnp.concatenate([matmul(sv["h2"], d_half, "tn", bf16, group="n", tk=2048, name=t + "gate_up_dw" + tag)
                                 for d_half, tag in ((d_gate, "_gate"), (d_up, "_up"))], axis=0)
        half = N_DEV // 2
        d_h2 = matmul(d_gate, p["w_gu"][:half], "nt", f32, group="k", tn=1024, name=t + "gate_up_dx_gate")
        d_h2 = matmul(d_up, p["w_gu"][half:], "nt", bf16, group="k", tn=1024, residual=d_h2, name=t + "gate_up_dx_up")
        dx_mid, g_ffn = rmsnorm_bwd(sv["x_mid"], p["ffn_norm"], d_h2, dx, name=t + "ffn_norm")
        d_merged = matmul(dx_mid, p["w_out"], "nt", f32, tn=1024, name=t + "out_proj_dx")
        gw_out = matmul(sv["merged"], dx_mid, "tn", bf16, tk=2048, tn=1024, name=t + "out_proj_dw")
        dproj = lax.empty((s, W_MAIN), bf16)
        dproj, d_a, d_b, d_c, gw_branch = merge_bwd(sv["out_a"], sv["out_b"], sv["out_c"], sv["proj"], p["w_branch"],
                                                   d_merged, dproj, name=t + "merge")
        parts.update({("w_gate_up", l): gw_gu, ("w_down", l): gw_down.reshape(N_DEV, -1, d_model),
                      ("w_out", l): gw_out.reshape(N_DEV, -1, d_model),
                      ("w_branch", l): jnp.transpose(gw_branch.reshape(3, MIX, N_DEV, -1), (2, 0, 1, 3)).astype(bf16)})
        dproj, g_ln_g, g_ln_b, g_sgu_w, g_sgu_bt = sgu_bwd(sv["proj"], p["ln_g"], p["ln_b"], p["sgu_w"], p["sgu_bt"],
                                                         d_a, dproj, name=t + "sgu")
        dproj, g_sink = scattering(t + "swa", swa_bwd, sv["proj"], p["sink_row"], tables, d_b, dproj)
        dproj, dxc, dba, g_alog, g_dtb, g_dnorm = scattering(
            t + "deltanet", dn_bwd, sv["xc"], sv["proj"], sv["p_ba"], p["alog_row"], p["dtb_row"], p["norm_row"],
            sv["states"], sv["tri"], d_c, dproj)
        dproj, g_conv8 = scattering(t + "dn_conv", conv_bwd, sv["proj"], p["conv_w8"], dxc, dproj)
        gw_main = matmul(sv["h"], dproj, "tn", bf16, tk=2048, name=t + "in_proj_dw")
        gw_ba = matmul(sv["h"], dba, "tn", bf16, tk=2048, name=t + "in_proj_ba_dw")
        parts["w_in", l] = _w_in_parts(gw_main, gw_ba)
        d_h = scattering(t + "in_proj_dx", matmul, dproj, p["w_main"], "nt", f32, tk=2304, tn=1024)
        d_h = matmul(dba, p["w_ba"], "nt", bf16, residual=d_h, name=t + "in_proj_ba_dx")
        dx, g_attn = rmsnorm_bwd(sv["x_in"], p["attn_norm"], d_h, dx_mid, name=t + "attn_norm")
        per_layer.append(dict(
            dn_conv_w=jnp.transpose(g_conv8[:DN_CONV].reshape(DN_CONV, N_DEV, -1), (1, 0, 2)),
            attn_norm=g_attn, sgu_ln_g=g_ln_g, sgu_ln_b=g_ln_b, sgu_w=g_sgu_w, sgu_b=g_sgu_bt.T,
            attn_sinks=g_sink[0, :SWA_HEADS], dn_a_log=g_alog[0, DN_HEADS:2 * DN_HEADS],
            dn_dt_bias=g_dtb[0, DN_HEADS:2 * DN_HEADS], dn_norm=g_dnorm, ffn_norm=g_ffn))
    per_layer.reverse()

    conv_parts = jnp.concatenate([pp["dn_conv_w"] for pp in per_layer], axis=1)
    rep_grads = {n: jnp.stack([pp[n].reshape(given[n].shape[1:]) for pp in per_layer]) for n in rep_names}
    rep_grads["final_norm"] = d_final_norm[0]
    rep_names = rep_names + ["final_norm"]
    rep_rows = [_rows128(given[n]).shape[0] for n in rep_names]
    pad_rows = -sum(rep_rows) % 16

    def small_rows(values):
        pieces = [_rows128(values[n]) for n in rep_names]
        return jnp.concatenate(pieces + [jnp.zeros((pad_rows, LANES), f32)], axis=0)

    (got_conv,), (small_all,) = exchange(
        [Exchange("scatter", [conv_parts]), Exchange("gather", [small_rows(rep_grads)])], name="exchange_last")

    results = [{}, {}, {}, {}]
    for n in shard_names:
        shp = given[n].shape
        two = (-1, shp[-1])
        by_layer = [got_conv] if n == "dn_conv_w" else [received[n, l].reshape(N_DEV, -1, shp[-1]) for l in range(depth)]
        outs = adamw(given[n].reshape(two), given["m_" + n].reshape(two), given["v_" + n].reshape(two), by_layer,
                     name="adamw_" + n)
        for res, val in zip(results, outs):
            res[n] = val.reshape(shp)
    outs = adamw(small_rows(given), small_rows({n: given["m_" + n] for n in rep_names}),
                 small_rows({n: given["v_" + n] for n in rep_names}), [small_all], name="adamw_replicated")
    for res, val in zip(results, outs):
        row = 0
        for n, nr in zip(rep_names, rep_rows):
            res[n] = val[row:row + nr].reshape(-1)[:given[n].size].reshape(given[n].shape)
            row += nr
    order = ["attn_norm", "w_in", "sgu_ln_g", "sgu_ln_b", "sgu_w", "sgu_b", "attn_sinks", "dn_conv_w", "dn_a_log",
             "dn_dt_bias", "dn_norm", "w_branch", "w_out", "ffn_norm", "w_gate_up", "w_down", "final_norm"]
    return (loss, dx.reshape(x.shape), *[res[n] for res in results for n in order])
```
